```python
import jax, jax.numpy as jnp
from jax import lax
import numpy as np

D_MODEL = 1024
BATCH = 8
SEQ = 8192
DEPTH = 1

MIX_WIDTH = D_MODEL
POOL_WIDTH = D_MODEL // 4
POOL_WINDOWS = (2, 4, 8, 16)
POOL_GROUP = POOL_WIDTH // len(POOL_WINDOWS)
HEAD_DIM = 64
ATTN_WIDTH = MIX_WIDTH - POOL_WIDTH
N_HEADS = ATTN_WIDTH // HEAD_DIM
DILATED_CONFIGS = ((128, 1), (512, 4), (2048, 16))
BLOCK = 128
ROPE_THETA = 10000.0
IN_WIDTH = POOL_WIDTH + 3 * ATTN_WIDTH
_FF_RAW = -(-8 * D_MODEL // 3)
D_FF = ((_FF_RAW + 255) // 256) * 256
EPS = 1e-6

kernel_name = "hybrid_pool_dilated_attn_block"


def rms_norm(x, g):
    xf = x.astype(jnp.float32)
    y = xf * lax.rsqrt(jnp.mean(xf * xf, axis=-1, keepdims=True) + EPS)
    return (y * g.astype(jnp.float32)).astype(x.dtype)


def rope(x, pos):
    half = x.shape[-1] // 2
    freqs = ROPE_THETA ** (-jnp.arange(half, dtype=jnp.float32) * (2.0 / x.shape[-1]))
    ang = pos.astype(jnp.float32)[:, None] * freqs[None, :]
    cos = jnp.cos(ang)[None, :, None, :]
    sin = jnp.sin(ang)[None, :, None, :]
    xf = x.astype(jnp.float32)
    x1, x2 = xf[..., :half], xf[..., half:]
    out = jnp.concatenate([x1 * cos - x2 * sin, x2 * cos + x1 * sin], axis=-1)
    return out.astype(x.dtype)


def multi_scale_pool(u, w_pool, pool_scale):
    B, S, _ = u.shape
    ug = u.astype(jnp.float32).reshape(B, S, len(POOL_WINDOWS), POOL_GROUP)
    csum = lax.cumsum(ug, axis=1)
    t = jnp.arange(S)
    outs = []
    for gi, win in enumerate(POOL_WINDOWS):
        cg = csum[:, :, gi]
        shifted = jnp.pad(cg, ((0, 0), (win, 0), (0, 0)))[:, :S]
        cnt = jnp.minimum(t + 1, win).astype(jnp.float32)[None, :, None]
        outs.append((cg - shifted) / cnt - ug[:, :, gi])
    d = jnp.stack(outs, axis=2)
    y = jnp.einsum('bsgc,gcd->bsgd', d, w_pool.astype(jnp.float32))
    y = y.reshape(B, S, POOL_WIDTH) * pool_scale.astype(jnp.float32)
    return y.astype(u.dtype)


def dilated_branch(q, k, v, window, dilation):
    B, S, H, Dh = q.shape
    L = S // dilation
    nb = -(-L // BLOCK)
    Lp = nb * BLOCK
    w_sub = window // dilation

    def to_sub(a):
        a = a.reshape(B, L, dilation, H, Dh).transpose(0, 2, 1, 3, 4)
        return jnp.pad(a, ((0, 0), (0, 0), (0, Lp - L), (0, 0), (0, 0)))

    qs = to_sub(q).reshape(B, dilation, nb, BLOCK, H, Dh)
    kp = jnp.pad(to_sub(k), ((0, 0), (0, 0), (BLOCK, 0), (0, 0), (0, 0)))
    vp = jnp.pad(to_sub(v), ((0, 0), (0, 0), (BLOCK, 0), (0, 0), (0, 0)))

    def band(a):
        prev = a[:, :, :Lp].reshape(B, dilation, nb, BLOCK, H, Dh)
        cur = a[:, :, BLOCK:].reshape(B, dilation, nb, BLOCK, H, Dh)
        return jnp.concatenate([prev, cur], axis=3)

    kb, vb = band(kp), band(vp)
    scale = 1.0 / np.sqrt(Dh).astype(np.float32)
    s = jnp.einsum('brnqhd,brnkhd->brnhqk', qs.astype(jnp.float32), kb.astype(jnp.float32)) * scale

    qi = jnp.arange(BLOCK)[:, None]
    kj = jnp.arange(2 * BLOCK)[None, :]
    dist = qi + BLOCK - kj
    blk = jnp.arange(nb)[:, None, None]
    valid = (dist >= 0) & (dist <= w_sub) & (blk * BLOCK + kj - BLOCK >= 0)
    s = jnp.where(valid[None, None, :, None], s, -jnp.inf)

    m = jnp.max(s, axis=-1, keepdims=True)
    e = jnp.exp(s - m)
    den = jnp.sum(e, axis=-1, keepdims=True)
    lse = (m + jnp.log(den))[..., 0]
    o = jnp.einsum('brnhqk,brnkhd->brnqhd', e / den, vb.astype(jnp.float32))

    o = o.reshape(B, dilation, Lp, H, Dh)[:, :, :L].transpose(0, 2, 1, 3, 4).reshape(B, S, H, Dh)
    lse = lse.transpose(0, 1, 2, 4, 3).reshape(B, dilation, Lp, H)[:, :, :L]
    lse = lse.transpose(0, 2, 1, 3).reshape(B, S, H)
    return o, lse


def dilated_attention(q, k, v):
    outs, lses = [], []
    for window, dilation in DILATED_CONFIGS:
        o, lse = dilated_branch(q, k, v, window, dilation)
        outs.append(o)
        lses.append(lse)
    w = jax.nn.softmax(jnp.stack(lses, axis=0), axis=0)
    o = jnp.sum(w[..., None] * jnp.stack(outs, axis=0), axis=0)
    return o.astype(q.dtype)


def _fwd_setup_inputs(seed: int = 0) -> dict:
    key = jax.random.key(seed)
    ks = jax.random.split(key, 13)
    f32 = jnp.float32
    nrm = lambda k, shape, s: jax.random.normal(k, shape, f32) * s
    return {
        "x": jax.random.normal(ks[0], (BATCH, SEQ, D_MODEL), f32),
        "ln_pre_mix": 1.0 + nrm(ks[1], (DEPTH, D_MODEL), 0.05),
        "w_in": nrm(ks[2], (DEPTH, D_MODEL, IN_WIDTH), D_MODEL ** -0.5),
        "w_pool": nrm(ks[3], (DEPTH, len(POOL_WINDOWS), POOL_GROUP, POOL_GROUP), POOL_GROUP ** -0.5),
        "pool_scale": 1.0 + nrm(ks[4], (DEPTH, POOL_WIDTH), 0.1),
        "w_out": nrm(ks[5], (DEPTH, MIX_WIDTH, D_MODEL), MIX_WIDTH ** -0.5),
        "ln_post_mix": 1.0 + nrm(ks[6], (DEPTH, D_MODEL), 0.05),
        "ln_pre_ffn": 1.0 + nrm(ks[7], (DEPTH, D_MODEL), 0.05),
        "w_gate": nrm(ks[8], (DEPTH, D_MODEL, D_FF), D_MODEL ** -0.5),
        "w_up": nrm(ks[9], (DEPTH, D_MODEL, D_FF), D_MODEL ** -0.5),
        "w_down": nrm(ks[10], (DEPTH, D_FF, D_MODEL), D_FF ** -0.5),
        "ln_post_ffn": 1.0 + nrm(ks[11], (DEPTH, D_MODEL), 0.05),
    }


def _fwd_reference(x, ln_pre_mix, w_in, w_pool, pool_scale, w_out, ln_post_mix,
              ln_pre_ffn, w_gate, w_up, w_down, ln_post_ffn):
    B, S, _ = x.shape
    pos = jnp.arange(S)
    for l in range(DEPTH):
        h = rms_norm(x, ln_pre_mix[l])
        proj = h @ w_in[l]
        u_pool = proj[..., :POOL_WIDTH]
        q = proj[..., POOL_WIDTH:POOL_WIDTH + ATTN_WIDTH].reshape(B, S, N_HEADS, HEAD_DIM)
        k = proj[..., POOL_WIDTH + ATTN_WIDTH:POOL_WIDTH + 2 * ATTN_WIDTH].reshape(B, S, N_HEADS, HEAD_DIM)
        v = proj[..., POOL_WIDTH + 2 * ATTN_WIDTH:].reshape(B, S, N_HEADS, HEAD_DIM)
        q, k = rope(q, pos), rope(k, pos)
        pool_out = multi_scale_pool(u_pool, w_pool[l], pool_scale[l])
        attn_out = dilated_attention(q, k, v).reshape(B, S, ATTN_WIDTH)
        mix = jnp.concatenate([pool_out, attn_out], axis=-1) @ w_out[l]
        x = x + rms_norm(mix, ln_post_mix[l])
        h = rms_norm(x, ln_pre_ffn[l])
        f = (jax.nn.silu(h @ w_gate[l]) * (h @ w_up[l])) @ w_down[l]
        x = x + rms_norm(f, ln_post_ffn[l])
    return x


import jax as _jax
import jax.numpy as _jnp

TWIN_FORMAT = 'train_step'
FWD_PARAMS = ['x', 'ln_pre_mix', 'w_in', 'w_pool', 'pool_scale', 'w_out', 'ln_post_mix', 'ln_pre_ffn', 'w_gate', 'w_up', 'w_down', 'ln_post_ffn']
TWIN_WEIGHTS = ['ln_pre_mix', 'w_in', 'w_pool', 'pool_scale', 'w_out', 'ln_post_mix', 'ln_pre_ffn', 'w_gate', 'w_up', 'w_down', 'ln_post_ffn']
TWIN_DIFF_INPUT = 'x'
TWIN_INPUTS = ['x', 'ln_pre_mix', 'w_in', 'w_pool', 'pool_scale', 'w_out', 'ln_post_mix', 'ln_pre_ffn', 'w_gate', 'w_up', 'w_down', 'ln_post_ffn', 'loss_target', 'm_ln_pre_mix', 'm_w_in', 'm_w_pool', 'm_pool_scale', 'm_w_out', 'm_ln_post_mix', 'm_ln_pre_ffn', 'm_w_gate', 'm_w_up', 'm_w_down', 'm_ln_post_ffn', 'v_ln_pre_mix', 'v_w_in', 'v_w_pool', 'v_pool_scale', 'v_w_out', 'v_ln_post_mix', 'v_ln_pre_ffn', 'v_w_gate', 'v_w_up', 'v_w_down', 'v_ln_post_ffn']
TWIN_OUTPUTS = ['loss', 'grad_x', 'grad_ln_pre_mix', 'grad_w_in', 'grad_w_pool', 'grad_pool_scale', 'grad_w_out', 'grad_ln_post_mix', 'grad_ln_pre_ffn', 'grad_w_gate', 'grad_w_up', 'grad_w_down', 'grad_ln_post_ffn', 'delta_ln_pre_mix', 'delta_w_in', 'delta_w_pool', 'delta_pool_scale', 'delta_w_out', 'delta_ln_post_mix', 'delta_ln_pre_ffn', 'delta_w_gate', 'delta_w_up', 'delta_w_down', 'delta_ln_post_ffn', 'new_m_ln_pre_mix', 'new_m_w_in', 'new_m_w_pool', 'new_m_pool_scale', 'new_m_w_out', 'new_m_ln_post_mix', 'new_m_ln_pre_ffn', 'new_m_w_gate', 'new_m_w_up', 'new_m_w_down', 'new_m_ln_post_ffn', 'new_v_ln_pre_mix', 'new_v_w_in', 'new_v_w_pool', 'new_v_pool_scale', 'new_v_w_out', 'new_v_ln_post_mix', 'new_v_ln_pre_ffn', 'new_v_w_gate', 'new_v_w_up', 'new_v_w_down', 'new_v_ln_post_ffn']
TWIN_LEAF_KINDS = {'loss': 'loss', 'grad_x': 'grad_x', 'grad_ln_pre_mix': 'grad_w', 'grad_w_in': 'grad_w', 'grad_w_pool': 'grad_w', 'grad_pool_scale': 'grad_w', 'grad_w_out': 'grad_w', 'grad_ln_post_mix': 'grad_w', 'grad_ln_pre_ffn': 'grad_w', 'grad_w_gate': 'grad_w', 'grad_w_up': 'grad_w', 'grad_w_down': 'grad_w', 'grad_ln_post_ffn': 'grad_w', 'delta_ln_pre_mix': 'delta_w', 'delta_w_in': 'delta_w', 'delta_w_pool': 'delta_w', 'delta_pool_scale': 'delta_w', 'delta_w_out': 'delta_w', 'delta_ln_post_mix': 'delta_w', 'delta_ln_pre_ffn': 'delta_w', 'delta_w_gate': 'delta_w', 'delta_w_up': 'delta_w', 'delta_w_down': 'delta_w', 'delta_ln_post_ffn': 'delta_w', 'new_m_ln_pre_mix': 'new_m', 'new_m_w_in': 'new_m', 'new_m_w_pool': 'new_m', 'new_m_pool_scale': 'new_m', 'new_m_w_out': 'new_m', 'new_m_ln_post_mix': 'new_m', 'new_m_ln_pre_ffn': 'new_m', 'new_m_w_gate': 'new_m', 'new_m_w_up': 'new_m', 'new_m_w_down': 'new_m', 'new_m_ln_post_ffn': 'new_m', 'new_v_ln_pre_mix': 'new_v', 'new_v_w_in': 'new_v', 'new_v_w_pool': 'new_v', 'new_v_pool_scale': 'new_v', 'new_v_w_out': 'new_v', 'new_v_ln_post_mix': 'new_v', 'new_v_ln_pre_ffn': 'new_v', 'new_v_w_gate': 'new_v', 'new_v_w_up': 'new_v', 'new_v_w_down': 'new_v', 'new_v_ln_post_ffn': 'new_v'}


def _forward(args):
    return _fwd_reference(*[args[k] for k in FWD_PARAMS])


def _output_shape():
    def fwd():
        inp = _fwd_setup_inputs(0)
        return _fwd_reference(*[inp[k] for k in FWD_PARAMS])
    out = _jax.eval_shape(fwd)
    return out.shape, out.dtype

N_MICROBATCH = 1
ADAM_LR = 0.001
ADAM_B1 = 0.9
ADAM_B2 = 0.999
ADAM_EPS = 1e-08
ADAM_WD = 0.01
ADAM_STEP = 10
PER_EXAMPLE_BATCH_AXIS = {'x': 0, 'loss_target': 0}
SHARED_INPUTS = []
_WEIGHT_DTYPES = {'ln_pre_mix': _jnp.float32, 'w_in': _jnp.float32, 'w_pool': _jnp.float32, 'pool_scale': _jnp.float32, 'w_out': _jnp.float32, 'ln_post_mix': _jnp.float32, 'ln_pre_ffn': _jnp.float32, 'w_gate': _jnp.float32, 'w_up': _jnp.float32, 'w_down': _jnp.float32, 'ln_post_ffn': _jnp.float32}
MOMENT_SCALE = {'ln_pre_mix': 1.380168e+00, 'w_in': 8.435643e-01, 'w_pool': 3.708955e+00, 'pool_scale': 5.448596e+00, 'w_out': 2.142645e+00, 'ln_post_mix': 6.461086e+01, 'ln_pre_ffn': 1.431501e+00, 'w_gate': 2.898043e-01, 'w_up': 8.314358e-01, 'w_down': 1.378006e+00, 'ln_post_ffn': 6.416101e+01}


def _to_microbatches(a, axis):
    t = _jnp.moveaxis(a, axis, 0)
    t = t.reshape((N_MICROBATCH, t.shape[0] // N_MICROBATCH) + t.shape[1:])
    return _jnp.moveaxis(t, 1, axis + 1)


def setup_inputs(seed: int = 0) -> dict:
    inp = _fwd_setup_inputs(seed)
    key = _jax.random.fold_in(_jax.random.key(seed), 7919)
    shape, _ = _output_shape()
    out = dict(inp)
    out["loss_target"] = _jax.random.normal(_jax.random.fold_in(key, 0), shape, _jnp.float32)
    for i, name in enumerate(TWIN_WEIGHTS):
        w = inp[name].astype(_jnp.float32)
        if MOMENT_SCALE is None:
            s = _jnp.sqrt(_jnp.mean(_jnp.square(w)) + 1e-30)
        else:
            s = MOMENT_SCALE[name]
        km, kv = _jax.random.split(_jax.random.fold_in(key, i + 1))
        out[name] = w
        out["m_" + name] = s * _jax.random.normal(km, w.shape, _jnp.float32)
        out["v_" + name] = (s * s) * _jax.random.uniform(kv, w.shape, _jnp.float32, 0.5, 1.5)
    if N_MICROBATCH > 1:
        for name, axis in PER_EXAMPLE_BATCH_AXIS.items():
            out[name] = _to_microbatches(out[name], axis)
    return {'x': out['x'], 'ln_pre_mix': out['ln_pre_mix'], 'w_in': out['w_in'], 'w_pool': out['w_pool'], 'pool_scale': out['pool_scale'], 'w_out': out['w_out'], 'ln_post_mix': out['ln_post_mix'], 'ln_pre_ffn': out['ln_pre_ffn'], 'w_gate': out['w_gate'], 'w_up': out['w_up'], 'w_down': out['w_down'], 'ln_post_ffn': out['ln_post_ffn'], 'loss_target': out['loss_target'], 'm_ln_pre_mix': out['m_ln_pre_mix'], 'm_w_in': out['m_w_in'], 'm_w_pool': out['m_w_pool'], 'm_pool_scale': out['m_pool_scale'], 'm_w_out': out['m_w_out'], 'm_ln_post_mix': out['m_ln_post_mix'], 'm_ln_pre_ffn': out['m_ln_pre_ffn'], 'm_w_gate': out['m_w_gate'], 'm_w_up': out['m_w_up'], 'm_w_down': out['m_w_down'], 'm_ln_post_ffn': out['m_ln_post_ffn'], 'v_ln_pre_mix': out['v_ln_pre_mix'], 'v_w_in': out['v_w_in'], 'v_w_pool': out['v_w_pool'], 'v_pool_scale': out['v_pool_scale'], 'v_w_out': out['v_w_out'], 'v_ln_post_mix': out['v_ln_post_mix'], 'v_ln_pre_ffn': out['v_ln_pre_ffn'], 'v_w_gate': out['v_w_gate'], 'v_w_up': out['v_w_up'], 'v_w_down': out['v_w_down'], 'v_ln_post_ffn': out['v_ln_post_ffn']}


def _loss(weights, diff, rest, loss_target):
    with _jax.named_scope("forward"):
        args = {**rest, TWIN_DIFF_INPUT: diff, **{k: w.astype(_WEIGHT_DTYPES[k]) for k, w in weights.items()}}
        y = _forward(args)
    with _jax.named_scope("loss_head"):
        err = _jnp.square(y.astype(_jnp.float32) - loss_target)
        return 0.5 * _jnp.sum(_jnp.mean(err, axis=-1)) if err.ndim else 0.5 * err


def _adamw(w, g, m, v):
    m = ADAM_B1 * m + (1.0 - ADAM_B1) * g
    v = ADAM_B2 * v + (1.0 - ADAM_B2) * _jnp.square(g)
    m_hat = m / (1.0 - ADAM_B1 ** ADAM_STEP)
    v_hat = v / (1.0 - ADAM_B2 ** ADAM_STEP)
    delta = -ADAM_LR * (m_hat / (_jnp.sqrt(v_hat) + ADAM_EPS) + ADAM_WD * w)
    return delta, m, v


def reference(x, ln_pre_mix, w_in, w_pool, pool_scale, w_out, ln_post_mix, ln_pre_ffn, w_gate, w_up, w_down, ln_post_ffn, loss_target, m_ln_pre_mix, m_w_in, m_w_pool, m_pool_scale, m_w_out, m_ln_post_mix, m_ln_pre_ffn, m_w_gate, m_w_up, m_w_down, m_ln_post_ffn, v_ln_pre_mix, v_w_in, v_w_pool, v_pool_scale, v_w_out, v_ln_post_mix, v_ln_pre_ffn, v_w_gate, v_w_up, v_w_down, v_ln_post_ffn):
    given = dict(x=x, ln_pre_mix=ln_pre_mix, w_in=w_in, w_pool=w_pool, pool_scale=pool_scale, w_out=w_out, ln_post_mix=ln_post_mix, ln_pre_ffn=ln_pre_ffn, w_gate=w_gate, w_up=w_up, w_down=w_down, ln_post_ffn=ln_post_ffn, loss_target=loss_target, m_ln_pre_mix=m_ln_pre_mix, m_w_in=m_w_in, m_w_pool=m_w_pool, m_pool_scale=m_pool_scale, m_w_out=m_w_out, m_ln_post_mix=m_ln_post_mix, m_ln_pre_ffn=m_ln_pre_ffn, m_w_gate=m_w_gate, m_w_up=m_w_up, m_w_down=m_w_down, m_ln_post_ffn=m_ln_post_ffn, v_ln_pre_mix=v_ln_pre_mix, v_w_in=v_w_in, v_w_pool=v_w_pool, v_pool_scale=v_pool_scale, v_w_out=v_w_out, v_ln_post_mix=v_ln_post_mix, v_ln_pre_ffn=v_ln_pre_ffn, v_w_gate=v_w_gate, v_w_up=v_w_up, v_w_down=v_w_down, v_ln_post_ffn=v_ln_post_ffn)
    weights = {n: given[n] for n in TWIN_WEIGHTS}
    shared = {n: given[n] for n in SHARED_INPUTS}
    per_example = {n: given[n] for n in ['x']}
    grad_fn = _jax.value_and_grad(_loss, argnums=(0, 1))

    def one_microbatch(ex, loss_target):
        ex = dict(ex)
        diff = ex.pop(TWIN_DIFF_INPUT)
        return grad_fn(weights, diff, {**shared, **ex}, loss_target)

    if N_MICROBATCH == 1:
        loss, (grad_w, grad_x) = one_microbatch(per_example, given["loss_target"])
    else:
        def body(carry, xs):
            loss_sum, grad_sum = carry
            l_k, (gw_k, gx_k) = one_microbatch(xs[0], xs[1])
            with _jax.named_scope("update"):
                return (loss_sum + l_k, _jax.tree.map(_jnp.add, grad_sum, gw_k)), gx_k

        init = (_jnp.zeros((), _jnp.float32), _jax.tree.map(_jnp.zeros_like, weights))
        (loss, grad_w), grad_x = _jax.lax.scan(body, init, (per_example, given["loss_target"]))
    with _jax.named_scope("update"):
        delta_w, new_m, new_v = {}, {}, {}
        for n in TWIN_WEIGHTS:
            delta_w[n], new_m[n], new_v[n] = _adamw(weights[n], grad_w[n], given["m_" + n], given["v_" + n])
    return (loss, grad_x, *[grad_w[n] for n in TWIN_WEIGHTS], *[delta_w[n] for n in TWIN_WEIGHTS],
            *[new_m[n] for n in TWIN_WEIGHTS], *[new_v[n] for n in TWIN_WEIGHTS])
```

```python
import jax
import jax.numpy as jnp
from jax import lax
from jax.experimental import pallas as pl
from jax.experimental.pallas import tpu as pltpu

F32 = jnp.float32
BF16 = jnp.bfloat16

D_MODEL = 1024
POOL_W = 256
ATTN_W = 768
IN_W = 2560
D_FF = 2816
POOL_WINDOWS = (2, 4, 8, 16)
POOL_HALO = 16
DILATIONS = (1, 4, 16)
BLK = 128
LANES = 128
HEAD_DIM = 64
N_GROUPS = ATTN_W // LANES
ROPE_THETA = 10000.0
EPS = 1e-6
NEG = -1e30
N_DEV = 8
SMALL_ROWS = 24

ADAM_LR = 0.001
ADAM_B1 = 0.9
ADAM_B2 = 0.999
ADAM_EPS = 1e-08
ADAM_WD = 0.01
ADAM_STEP = 10

VMEM_LIMIT = 56 * 1024 * 1024


def _dot(a, b):
    return jnp.dot(a, b, preferred_element_type=F32)


def _dot_nt(a, b):
    return lax.dot_general(a, b, (((1,), (1,)), ((), ())), preferred_element_type=F32)


def _dot_tn(a, b):
    return lax.dot_general(a, b, (((0,), (0,)), ((), ())), preferred_element_type=F32)


def _params(n_grid):
    return pltpu.CompilerParams(dimension_semantics=("arbitrary",) * n_grid, vmem_limit_bytes=VMEM_LIMIT)


def _tok(tm, c):
    return pl.BlockSpec((tm, c), lambda i: (i, 0))


def _res(shape):
    return pl.BlockSpec(shape, lambda i: (0,) * len(shape))


def _rms_fwd(x, g):
    r = lax.rsqrt(jnp.mean(x * x, axis=-1, keepdims=True) + EPS)
    return x * r * g


def _rms_bwd(x, g, dy):
    r = lax.rsqrt(jnp.mean(x * x, axis=-1, keepdims=True) + EPS)
    xh = x * r
    gd = dy * g
    dx = r * (gd - xh * jnp.mean(gd * xh, axis=-1, keepdims=True))
    return dx, jnp.sum(dy * xh, axis=0, keepdims=True)


def _rope(x, c, s, sign):
    lane = lax.broadcasted_iota(jnp.int32, (x.shape[0], LANES), 1)
    first = (lane % HEAD_DIM) < (HEAD_DIM // 2)
    outs = []
    for g in range(x.shape[1] // LANES):
        xg = x[:, g * LANES:(g + 1) * LANES]
        rot = jnp.where(first, pltpu.roll(xg, LANES - HEAD_DIM // 2, 1), pltpu.roll(xg, HEAD_DIM // 2, 1))
        outs.append(xg * c + sign * (rot * s))
    return jnp.concatenate(outs, axis=1)


def _proj_fwd(x, g1, w_in_t, cos, sin, tm):
    T = x.shape[0]

    def body(x_ref, g_ref, w_ref, c_ref, s_ref, h_ref, u_ref, q_ref, k_ref, v_ref):
        h = _rms_fwd(x_ref[...], g_ref[...]).astype(BF16)
        h_ref[...] = h
        proj = _dot_nt(h, w_ref[...])
        c = c_ref[...]
        s = s_ref[...]
        u_ref[...] = proj[:, :POOL_W]
        q_ref[...] = _rope(proj[:, POOL_W:POOL_W + ATTN_W], c, s, 1.0).astype(BF16)
        k_ref[...] = _rope(proj[:, POOL_W + ATTN_W:POOL_W + 2 * ATTN_W], c, s, 1.0).astype(BF16)
        v_ref[...] = proj[:, POOL_W + 2 * ATTN_W:].astype(BF16)

    return pl.pallas_call(
        body, name="proj_fwd", grid=(T // tm,),
        in_specs=[_tok(tm, D_MODEL), _res((1, D_MODEL)), _res((IN_W, D_MODEL)), _tok(tm, LANES), _tok(tm, LANES)],
        out_specs=[_tok(tm, D_MODEL), _tok(tm, POOL_W), _tok(tm, ATTN_W), _tok(tm, ATTN_W), _tok(tm, ATTN_W)],
        out_shape=[jax.ShapeDtypeStruct((T, D_MODEL), BF16), jax.ShapeDtypeStruct((T, POOL_W), F32),
                   jax.ShapeDtypeStruct((T, ATTN_W), BF16), jax.ShapeDtypeStruct((T, ATTN_W), BF16),
                   jax.ShapeDtypeStruct((T, ATTN_W), BF16)],
        compiler_params=_params(1),
    )(x, g1, w_in_t, cos, sin)


def _pool_window(lane):
    return jnp.where(lane < 64, 2, jnp.where(lane < 128, 4, jnp.where(lane < 192, 8, 16)))


def _pool_select(lane, a2, a4, a8, a16):
    return jnp.where(lane < 64, a2, jnp.where(lane < 128, a4, jnp.where(lane < 192, a8, a16)))


def _pool_delta(cur, prev, i, tm):
    prev = jnp.where(i > 0, prev, 0.0)
    ext = jnp.concatenate([prev, cur], axis=0)
    s2 = ext + pltpu.roll(ext, 1, 0)
    s4 = s2 + pltpu.roll(s2, 2, 0)
    s8 = s4 + pltpu.roll(s4, 4, 0)
    s16 = s8 + pltpu.roll(s8, 8, 0)
    lane = lax.broadcasted_iota(jnp.int32, (tm, POOL_W), 1)
    row = lax.broadcasted_iota(jnp.int32, (tm, POOL_W), 0) + i * tm
    ws = _pool_select(lane, s2[POOL_HALO:], s4[POOL_HALO:], s8[POOL_HALO:], s16[POOL_HALO:])
    cnt = jnp.minimum(row + 1, _pool_window(lane)).astype(F32)
    return ws / cnt - cur


def _pool_fwd(u, wbd, scale, tm):
    T = u.shape[0]
    hb = tm // POOL_HALO

    def body(u_ref, prev_ref, w_ref, sc_ref, o_ref):
        d = _pool_delta(u_ref[...], prev_ref[...], pl.program_id(0), tm)
        o_ref[...] = (_dot(d.astype(BF16), w_ref[...]) * sc_ref[...]).astype(BF16)

    return pl.pallas_call(
        body, name="pool_fwd", grid=(T // tm,),
        in_specs=[_tok(tm, POOL_W), pl.BlockSpec((POOL_HALO, POOL_W), lambda i: (jnp.maximum(i * hb - 1, 0), 0)),
                  _res((POOL_W, POOL_W)), _res((1, POOL_W))],
        out_specs=_tok(tm, POOL_W),
        out_shape=jax.ShapeDtypeStruct((T, POOL_W), BF16),
        compiler_params=_params(1),
    )(u, u, wbd, scale)


def _attn_mask(n):
    qi = lax.broadcasted_iota(jnp.int32, (BLK, 2 * BLK), 0)
    kj = lax.broadcasted_iota(jnp.int32, (BLK, 2 * BLK), 1)
    dist = qi + BLK - kj
    return (dist >= 0) & (dist <= BLK) & ((kj >= BLK) | (n > 0))


def _head_col(tile, lane, h):
    return jnp.sum(jnp.where(lane == h, tile, 0.0), axis=1, keepdims=True)


def _attn_fwd(q, k, v, dil, prev, last):
    T = q.shape[0]
    L = T // dil
    nb = L // BLK
    first = prev is None
    out_dtype = BF16 if last else F32

    def body(*refs):
        if first:
            q_ref, kc_ref, kp_ref, vc_ref, vp_ref, acc_ref, lse_ref = refs
        else:
            q_ref, kc_ref, kp_ref, vc_ref, vp_ref, acc_in, lse_in, acc_ref, lse_ref = refs
        valid = _attn_mask(pl.program_id(1))
        lane = lax.broadcasted_iota(jnp.int32, (BLK, LANES), 1)
        lo = lane < HEAD_DIM
        lse_tile = jnp.zeros((BLK, LANES), F32)
        outs = []
        for g in range(N_GROUPS):
            sl = slice(g * LANES, (g + 1) * LANES)
            qg = q_ref[:, sl] * 0.125
            kcat = jnp.concatenate([kp_ref[:, sl], kc_ref[:, sl]], axis=0)
            vcat = jnp.concatenate([vp_ref[:, sl], vc_ref[:, sl]], axis=0)
            pair = None
            for hh in range(2):
                h = 2 * g + hh
                hm = lo if hh == 0 else jnp.logical_not(lo)
                s = _dot_nt(jnp.where(hm, qg, jnp.zeros_like(qg)), kcat)
                s = jnp.where(valid, s, NEG)
                m = jnp.max(s, axis=1, keepdims=True)
                e = jnp.exp(s - m)
                den = jnp.sum(e, axis=1, keepdims=True)
                o = _dot(e.astype(BF16), vcat) / den
                lse = m + jnp.log(den)
                if not first:
                    lse_old = _head_col(lse_in[...], lane, h)
                    mx = jnp.maximum(lse_old, lse)
                    new = mx + jnp.log(jnp.exp(lse_old - mx) + jnp.exp(lse - mx))
                    o = acc_in[:, sl] * jnp.exp(lse_old - new) + o * jnp.exp(lse - new)
                    lse = new
                pair = o if hh == 0 else jnp.where(lo, pair, o)
                lse_tile = jnp.where(lane == h, lse, lse_tile)
            outs.append(pair)
        acc_ref[...] = jnp.concatenate(outs, axis=1).astype(out_dtype)
        lse_ref[...] = lse_tile

    cur = lambda r, n: (n, r)
    prv = lambda r, n: (jnp.maximum(n - 1, 0), r)
    wide = lambda f: pl.BlockSpec((BLK, ATTN_W), f)
    narrow = pl.BlockSpec((BLK, LANES), cur)
    ins = [q, k, k, v, v]
    in_specs = [wide(cur), wide(cur), wide(prv), wide(cur), wide(prv)]
    if not first:
        ins += [prev[0], prev[1]]
        in_specs += [wide(cur), narrow]
    view = lambda a: a.reshape(L, dil * a.shape[1])
    acc, lse = pl.pallas_call(
        body, name=f"attn_fwd_d{dil}", grid=(dil, nb),
        in_specs=in_specs, out_specs=[wide(cur), narrow],
        out_shape=[jax.ShapeDtypeStruct((L, dil * ATTN_W), out_dtype), jax.ShapeDtypeStruct((L, dil * LANES), F32)],
        compiler_params=_params(2),
    )(*[view(a) for a in ins])
    return acc.reshape(T, ATTN_W), lse.reshape(T, LANES)


def _mix_fwd(pool, attn, x, w_out, g2, g3, tm):
    T = x.shape[0]

    def body(p_ref, a_ref, x_ref, w_ref, g2_ref, g3_ref, cat_ref, mix_ref, x2_ref, h2_ref):
        p = p_ref[...]
        a = a_ref[...]
        cat_ref[...] = jnp.concatenate([p, a], axis=1)
        mix = _dot(p, w_ref[:POOL_W, :]) + _dot(a, w_ref[POOL_W:, :])
        mix_ref[...] = mix
        x2 = x_ref[...] + _rms_fwd(mix, g2_ref[...])
        x2_ref[...] = x2
        h2_ref[...] = _rms_fwd(x2, g3_ref[...]).astype(BF16)

    return pl.pallas_call(
        body, name="mix_fwd", grid=(T // tm,),
        in_specs=[_tok(tm, POOL_W), _tok(tm, ATTN_W), _tok(tm, D_MODEL), _res((D_MODEL, D_MODEL)),
                  _res((1, D_MODEL)), _res((1, D_MODEL))],
        out_specs=[_tok(tm, D_MODEL)] * 4,
        out_shape=[jax.ShapeDtypeStruct((T, D_MODEL), BF16), jax.ShapeDtypeStruct((T, D_MODEL), F32),
                   jax.ShapeDtypeStruct((T, D_MODEL), F32), jax.ShapeDtypeStruct((T, D_MODEL), BF16)],
        compiler_params=_params(1),
    )(pool, attn, x, w_out, g2, g3)


def _ffn_up(h2, wg_t, wu_t, tm):
    T = h2.shape[0]

    def body(h_ref, wg_ref, wu_ref, g_ref, u_ref, a_ref):
        h = h_ref[...]
        gate = _dot_nt(h, wg_ref[...])
        up = _dot_nt(h, wu_ref[...])
        g_ref[...] = gate.astype(BF16)
        u_ref[...] = up.astype(BF16)
        a_ref[...] = (gate * (1.0 / (1.0 + jnp.exp(-gate))) * up).astype(BF16)

    return pl.pallas_call(
        body, name="ffn_up", grid=(T // tm,),
        in_specs=[_tok(tm, D_MODEL), _res((D_FF, D_MODEL)), _res((D_FF, D_MODEL))],
        out_specs=[_tok(tm, D_FF)] * 3,
        out_shape=[jax.ShapeDtypeStruct((T, D_FF), BF16)] * 3,
        compiler_params=_params(1),
    )(h2, wg_t, wu_t)


def _ffn_down_loss(act, w_down, x2, g4, tgt, tm):
    T = act.shape[0]

    def body(a_ref, w_ref, x2_ref, g_ref, t_ref, df_ref, dy_ref, dg_ref, loss_ref):
        i = pl.program_id(0)

        @pl.when(i == 0)
        def _():
            dg_ref[...] = jnp.zeros_like(dg_ref)
            loss_ref[...] = jnp.zeros_like(loss_ref)

        f = _dot(a_ref[...], w_ref[...])
        g = g_ref[...]
        err = x2_ref[...] + _rms_fwd(f, g) - t_ref[...]
        loss_ref[...] += 0.5 * jnp.sum(jnp.mean(err * err, axis=-1, keepdims=True), axis=0, keepdims=True)
        dy = err * (1.0 / D_MODEL)
        dy_ref[...] = dy
        df, dg = _rms_bwd(f, g, dy)
        dg_ref[...] += dg
        df_ref[...] = df.astype(BF16)

    return pl.pallas_call(
        body, name="ffn_down_loss", grid=(T // tm,),
        in_specs=[_tok(tm, D_FF), _res((D_FF, D_MODEL)), _tok(tm, D_MODEL), _res((1, D_MODEL)), _tok(tm, D_MODEL)],
        out_specs=[_tok(tm, D_MODEL), _tok(tm, D_MODEL), _res((1, D_MODEL)), _res((1, 1))],
        out_shape=[jax.ShapeDtypeStruct((T, D_MODEL), BF16), jax.ShapeDtypeStruct((T, D_MODEL), F32),
                   jax.ShapeDtypeStruct((1, D_MODEL), F32), jax.ShapeDtypeStruct((1, 1), F32)],
        compiler_params=_params(1),
    )(act, w_down, x2, g4, tgt)


def _ffn_act_bwd(df, w_down, gate, up, tm):
    T = df.shape[0]

    def body(df_ref, w_ref, g_ref, u_ref, dg_ref, du_ref):
        dact = _dot_nt(df_ref[...], w_ref[...])
        g = g_ref[...].astype(F32)
        u = u_ref[...].astype(F32)
        sg = 1.0 / (1.0 + jnp.exp(-g))
        dg_ref[...] = (dact * u * (sg * (1.0 + g * (1.0 - sg)))).astype(BF16)
        du_ref[...] = (dact * (g * sg)).astype(BF16)

    return pl.pallas_call(
        body, name="ffn_act_bwd", grid=(T // tm,),
        in_specs=[_tok(tm, D_MODEL), _res((D_FF, D_MODEL)), _tok(tm, D_FF), _tok(tm, D_FF)],
        out_specs=[_tok(tm, D_FF)] * 2,
        out_shape=[jax.ShapeDtypeStruct((T, D_FF), BF16)] * 2,
        compiler_params=_params(1),
    )(df, w_down, gate, up)


def _ffn_in_bwd(dgate, dup, wg_t, wu_t, x2, mix, dy, g3, g2, tm):
    T = x2.shape[0]

    def body(dg_ref, du_ref, wg_ref, wu_ref, x2_ref, mix_ref, dy_ref, g3_ref, g2_ref,
             dx2_ref, dmix_ref, dg3_ref, dg2_ref):
        @pl.when(pl.program_id(0) == 0)
        def _():
            dg3_ref[...] = jnp.zeros_like(dg3_ref)
            dg2_ref[...] = jnp.zeros_like(dg2_ref)

        dh2 = _dot(dg_ref[...], wg_ref[...]) + _dot(du_ref[...], wu_ref[...])
        dn, dg3 = _rms_bwd(x2_ref[...], g3_ref[...], dh2)
        dx2 = dy_ref[...] + dn
        dx2_ref[...] = dx2
        dg3_ref[...] += dg3
        dmix, dg2 = _rms_bwd(mix_ref[...], g2_ref[...], dx2)
        dg2_ref[...] += dg2
        dmix_ref[...] = dmix.astype(BF16)

    return pl.pallas_call(
        body, name="ffn_in_bwd", grid=(T // tm,),
        in_specs=[_tok(tm, D_FF), _tok(tm, D_FF), _res((D_FF, D_MODEL)), _res((D_FF, D_MODEL)),
                  _tok(tm, D_MODEL), _tok(tm, D_MODEL), _tok(tm, D_MODEL), _res((1, D_MODEL)), _res((1, D_MODEL))],
        out_specs=[_tok(tm, D_MODEL), _tok(tm, D_MODEL), _res((1, D_MODEL)), _res((1, D_MODEL))],
        out_shape=[jax.ShapeDtypeStruct((T, D_MODEL), F32), jax.ShapeDtypeStruct((T, D_MODEL), BF16),
                   jax.ShapeDtypeStruct((1, D_MODEL), F32), jax.ShapeDtypeStruct((1, D_MODEL), F32)],
        compiler_params=_params(1),
    )(dgate, dup, wg_t, wu_t, x2, mix, dy, g3, g2)


def _mix_bwd(dmix, w_out, tm):
    T = dmix.shape[0]

    def body(d_ref, w_ref, dp_ref, da_ref):
        dcat = _dot_nt(d_ref[...], w_ref[...])
        dp_ref[...] = dcat[:, :POOL_W].astype(BF16)
        da_ref[...] = dcat[:, POOL_W:].astype(BF16)

    return pl.pallas_call(
        body, name="mix_bwd", grid=(T // tm,),
        in_specs=[_tok(tm, D_MODEL), _res((D_MODEL, D_MODEL))],
        out_specs=[_tok(tm, POOL_W), _tok(tm, ATTN_W)],
        out_shape=[jax.ShapeDtypeStruct((T, POOL_W), BF16), jax.ShapeDtypeStruct((T, ATTN_W), BF16)],
        compiler_params=_params(1),
    )(dmix, w_out)


def _attn_bwd(q, k, v, dout, out, lse, dil):
    T = q.shape[0]
    L = T // dil
    nb = L // BLK

    def body(q_ref, kc_ref, kp_ref, vc_ref, vp_ref, do_ref, o_ref, lse_ref,
             dq_ref, dkc_ref, dkp_ref, dvc_ref, dvp_ref):
        valid = _attn_mask(pl.program_id(1))
        lane = lax.broadcasted_iota(jnp.int32, (BLK, LANES), 1)
        lo = lane < HEAD_DIM
        lse_tile = lse_ref[...]
        for g in range(N_GROUPS):
            sl = slice(g * LANES, (g + 1) * LANES)
            qg = q_ref[:, sl] * 0.125
            dog = do_ref[:, sl]
            kcat = jnp.concatenate([kp_ref[:, sl], kc_ref[:, sl]], axis=0)
            vcat = jnp.concatenate([vp_ref[:, sl], vc_ref[:, sl]], axis=0)
            prod = dog.astype(F32) * o_ref[:, sl].astype(F32)
            dk = jnp.zeros((2 * BLK, LANES), F32)
            dv = jnp.zeros((2 * BLK, LANES), F32)
            dq = None
            for hh in range(2):
                hm = lo if hh == 0 else jnp.logical_not(lo)
                qa = jnp.where(hm, qg, jnp.zeros_like(qg))
                doa = jnp.where(hm, dog, jnp.zeros_like(dog))
                s = jnp.where(valid, _dot_nt(qa, kcat), NEG)
                p = jnp.exp(s - _head_col(lse_tile, lane, 2 * g + hh))
                dsum = jnp.sum(jnp.where(hm, prod, 0.0), axis=1, keepdims=True)
                ds = (p * (_dot_nt(doa, vcat) - dsum)).astype(BF16)
                dv = dv + _dot_tn(p.astype(BF16), doa)
                dk = dk + _dot_tn(ds, qa)
                dqh = _dot(ds, kcat) * 0.125
                dq = dqh if hh == 0 else jnp.where(lo, dq, dqh)
            dq_ref[:, sl] = dq.astype(BF16)
            dkp_ref[:, sl] = dk[:BLK].astype(BF16)
            dkc_ref[:, sl] = dk[BLK:].astype(BF16)
            dvp_ref[:, sl] = dv[:BLK].astype(BF16)
            dvc_ref[:, sl] = dv[BLK:].astype(BF16)

    cur = lambda r, n: (n, r)
    prv = lambda r, n: (jnp.maximum(n - 1, 0), r)
    prv_out = lambda r, n: ((n + nb - 1) % nb, r)
    wide = lambda f: pl.BlockSpec((BLK, ATTN_W), f)
    view = lambda a: a.reshape(L, dil * a.shape[1])
    outs = pl.pallas_call(
        body, name=f"attn_bwd_d{dil}", grid=(dil, nb),
        in_specs=[wide(cur), wide(cur), wide(prv), wide(cur), wide(prv), wide(cur), wide(cur),
                  pl.BlockSpec((BLK, LANES), cur)],
        out_specs=[wide(cur), wide(cur), wide(prv_out), wide(cur), wide(prv_out)],
        out_shape=[jax.ShapeDtypeStruct((L, dil * ATTN_W), BF16)] * 5,
        compiler_params=_params(2),
    )(*[view(a) for a in (q, k, k, v, v, dout, out, lse)])
    return [o.reshape(T, ATTN_W) for o in outs]


def _pool_bwd(u, dy, wbd, scale, tm):
    T = u.shape[0]
    nt = T // tm
    hb = tm // POOL_HALO

    def body(u_ref, prev_ref, dy_ref, next_ref, w_ref, sc_ref, du_ref, dw_ref, dsc_ref):
        i = pl.program_id(0)

        @pl.when(i == 0)
        def _():
            dw_ref[...] = jnp.zeros_like(dw_ref)
            dsc_ref[...] = jnp.zeros_like(dsc_ref)

        w = w_ref[...]
        sc = sc_ref[...]
        d = _pool_delta(u_ref[...], prev_ref[...], i, tm).astype(BF16)
        dyc = dy_ref[...].astype(F32)
        dsc_ref[...] += jnp.sum(dyc * _dot(d, w), axis=0, keepdims=True)
        nxt = jnp.where(i < nt - 1, next_ref[...].astype(F32), 0.0)
        dypre = (jnp.concatenate([dyc, nxt], axis=0) * sc).astype(BF16)
        dw_ref[...] += _dot_tn(d, dypre[:tm])
        dd = _dot_nt(dypre, w)
        n = tm + POOL_HALO
        lane = lax.broadcasted_iota(jnp.int32, (n, POOL_W), 1)
        row = lax.broadcasted_iota(jnp.int32, (n, POOL_W), 0) + i * tm
        gx = dd / jnp.minimum(row + 1, _pool_window(lane)).astype(F32)
        a2 = gx + pltpu.roll(gx, n - 1, 0)
        a4 = a2 + pltpu.roll(a2, n - 2, 0)
        a8 = a4 + pltpu.roll(a4, n - 4, 0)
        a16 = a8 + pltpu.roll(a8, n - 8, 0)
        fs = _pool_select(lane[:tm], a2[:tm], a4[:tm], a8[:tm], a16[:tm])
        du_ref[...] = (fs - dd[:tm]).astype(BF16)

    return pl.pallas_call(
        body, name="pool_bwd", grid=(nt,),
        in_specs=[_tok(tm, POOL_W), pl.BlockSpec((POOL_HALO, POOL_W), lambda i: (jnp.maximum(i * hb - 1, 0), 0)),
                  _tok(tm, POOL_W), pl.BlockSpec((POOL_HALO, POOL_W), lambda i: (jnp.minimum((i + 1) * hb, nt * hb - 1), 0)),
                  _res((POOL_W, POOL_W)), _res((1, POOL_W))],
        out_specs=[_tok(tm, POOL_W), _res((POOL_W, POOL_W)), _res((1, POOL_W))],
        out_shape=[jax.ShapeDtypeStruct((T, POOL_W), BF16), jax.ShapeDtypeStruct((POOL_W, POOL_W), F32),
                   jax.ShapeDtypeStruct((1, POOL_W), F32)],
        compiler_params=_params(1),
    )(u, u, dy, dy, wbd, scale)


def _dproj_combine(du, dqs, dkcs, dkps, dvcs, dvps, cos, sin, tm):
    T = du.shape[0]
    n_cfg = len(dqs)

    def body(*refs):
        du_ref = refs[0]
        groups = [refs[1 + j * n_cfg:1 + (j + 1) * n_cfg] for j in range(5)]
        c_ref, s_ref, out_ref = refs[1 + 5 * n_cfg:]
        tot = lambda rs: sum(r[...].astype(F32) for r in rs)
        c = c_ref[...]
        s = s_ref[...]
        dq = _rope(tot(groups[0]), c, s, -1.0)
        dk = _rope(tot(groups[1]) + tot(groups[2]), c, s, -1.0)
        dv = tot(groups[3]) + tot(groups[4])
        out_ref[...] = jnp.concatenate([du_ref[...], dq.astype(BF16), dk.astype(BF16), dv.astype(BF16)], axis=1)

    return pl.pallas_call(
        body, name="dproj_combine", grid=(T // tm,),
        in_specs=[_tok(tm, POOL_W)] + [_tok(tm, ATTN_W)] * (5 * n_cfg) + [_tok(tm, LANES)] * 2,
        out_specs=_tok(tm, IN_W),
        out_shape=jax.ShapeDtypeStruct((T, IN_W), BF16),
        compiler_params=_params(1),
    )(du, *dqs, *dkcs, *dkps, *dvcs, *dvps, cos, sin)


def _proj_bwd(dproj, w_in_t, x, dx2, g1, tm):
    T = x.shape[0]

    def body(d_ref, w_ref, x_ref, r_ref, g_ref, dx_ref, dg_ref):
        @pl.when(pl.program_id(0) == 0)
        def _():
            dg_ref[...] = jnp.zeros_like(dg_ref)

        dn, dg = _rms_bwd(x_ref[...], g_ref[...], _dot(d_ref[...], w_ref[...]))
        dg_ref[...] += dg
        dx_ref[...] = r_ref[...] + dn

    return pl.pallas_call(
        body, name="proj_bwd", grid=(T // tm,),
        in_specs=[_tok(tm, IN_W), _res((IN_W, D_MODEL)), _tok(tm, D_MODEL), _tok(tm, D_MODEL), _res((1, D_MODEL))],
        out_specs=[_tok(tm, D_MODEL), _res((1, D_MODEL))],
        out_shape=[jax.ShapeDtypeStruct((T, D_MODEL), F32), jax.ShapeDtypeStruct((1, D_MODEL), F32)],
        compiler_params=_params(1),
    )(dproj, w_in_t, x, dx2, g1)


def _wgrad(a, b, name, tile_m, tk):
    T, M = a.shape
    N = b.shape[1]
    nk = T // tk

    def body(a_ref, b_ref, o_ref, acc_ref):
        kk = pl.program_id(1)

        @pl.when(kk == 0)
        def _():
            acc_ref[...] = jnp.zeros_like(acc_ref)

        acc_ref[...] += _dot_tn(a_ref[...], b_ref[...])

        @pl.when(kk == nk - 1)
        def _():
            o_ref[...] = acc_ref[...].astype(BF16)

    return pl.pallas_call(
        body, name=name, grid=(M // tile_m, nk),
        in_specs=[pl.BlockSpec((tk, tile_m), lambda j, kk: (kk, j)), pl.BlockSpec((tk, N), lambda j, kk: (kk, 0))],
        out_specs=pl.BlockSpec((tile_m, N), lambda j, kk: (j, 0)),
        out_shape=jax.ShapeDtypeStruct((M, N), BF16),
        scratch_shapes=[pltpu.VMEM((tile_m, N), F32)],
        compiler_params=_params(2),
    )(a, b)


def _exchange(arrs, scatter, name):
    n = len(arrs)
    out_shapes = [jax.ShapeDtypeStruct((N_DEV,) + (a.shape[1:] if sc else a.shape), a.dtype)
                  for a, sc in zip(arrs, scatter)]

    def body(*refs):
        ins, outs = refs[:n], refs[n:2 * n]
        send_sems, recv_sems, loc_sems = refs[2 * n:]
        x, y, c = lax.axis_index("x"), lax.axis_index("y"), lax.axis_index("c")
        me = 4 * x + 2 * y + c
        local, sends, recvs = [], [], []
        for i in range(n):
            own = ins[i].at[me] if scatter[i] else ins[i]
            loc = pltpu.make_async_copy(own, outs[i].at[me], loc_sems.at[i])
            loc.start()
            local.append(loc)
            for kbits in range(1, N_DEV):
                px = 1 - x if kbits & 4 else x
                py = 1 - y if kbits & 2 else y
                pc = 1 - c if kbits & 1 else c
                pid = 4 * px + 2 * py + pc
                src = ins[i].at[pid] if scatter[i] else ins[i]
                cp = pltpu.make_async_remote_copy(
                    src_ref=src, dst_ref=outs[i].at[me],
                    send_sem=send_sems.at[i, kbits - 1], recv_sem=recv_sems.at[i, kbits - 1],
                    device_id=(px, py, pc), device_id_type=pl.DeviceIdType.MESH)
                cp.start()
                sends.append(cp)
                recvs.append(pltpu.make_async_remote_copy(
                    src_ref=src, dst_ref=outs[i].at[pid],
                    send_sem=send_sems.at[i, kbits - 1], recv_sem=recv_sems.at[i, kbits - 1],
                    device_id=(px, py, pc), device_id_type=pl.DeviceIdType.MESH))
        for cp in recvs:
            cp.wait_recv()
        for cp in sends:
            cp.wait_send()
        for cp in local:
            cp.wait()

    hbm = pl.BlockSpec(memory_space=pl.ANY)
    return pl.pallas_call(
        body, name=name, in_specs=[hbm] * n, out_specs=[hbm] * n, out_shape=out_shapes,
        scratch_shapes=[pltpu.SemaphoreType.DMA((n, N_DEV - 1)), pltpu.SemaphoreType.DMA((n, N_DEV - 1)),
                        pltpu.SemaphoreType.DMA((n,))],
    )(*arrs)


def _slot_sum(parts, name, tr):
    _, R, C = parts.shape

    def body(p_ref, o_ref):
        acc = p_ref[0].astype(F32)
        for s in range(1, N_DEV):
            acc = acc + p_ref[s].astype(F32)
        o_ref[...] = acc

    return pl.pallas_call(
        body, name=name, grid=(R // tr,),
        in_specs=[pl.BlockSpec((N_DEV, tr, C), lambda i: (0, i, 0))],
        out_specs=pl.BlockSpec((tr, C), lambda i: (i, 0)),
        out_shape=jax.ShapeDtypeStruct((R, C), F32),
        compiler_params=_params(1),
    )(parts)


def _adamw(w, g, m, v, name):
    def body(w_ref, g_ref, m_ref, v_ref, d_ref, nm_ref, nv_ref):
        g = g_ref[...]
        nm = ADAM_B1 * m_ref[...] + (1.0 - ADAM_B1) * g
        nv = ADAM_B2 * v_ref[...] + (1.0 - ADAM_B2) * jnp.square(g)
        m_hat = nm / (1.0 - ADAM_B1 ** ADAM_STEP)
        v_hat = nv / (1.0 - ADAM_B2 ** ADAM_STEP)
        d_ref[...] = -ADAM_LR * (m_hat / (jnp.sqrt(v_hat) + ADAM_EPS) + ADAM_WD * w_ref[...])
        nm_ref[...] = nm
        nv_ref[...] = nv

    return pl.pallas_call(
        body, name=name, out_shape=[jax.ShapeDtypeStruct(w.shape, F32)] * 3,
        compiler_params=pltpu.CompilerParams(vmem_limit_bytes=VMEM_LIMIT),
    )(w, g, m, v)


def _rope_tables(T):
    half = HEAD_DIM // 2
    freqs = ROPE_THETA ** (-jnp.arange(half, dtype=F32) * (2.0 / HEAD_DIM))
    ang = jnp.arange(T).astype(F32)[:, None] * freqs[None, :]
    c, s = jnp.cos(ang), jnp.sin(ang)
    return jnp.concatenate([c, c, c, c], axis=1), jnp.concatenate([-s, s, -s, s], axis=1)


def _block_diag(w_pool):
    wbd = jnp.zeros((POOL_W, POOL_W), F32)
    g = POOL_W // len(POOL_WINDOWS)
    for i in range(len(POOL_WINDOWS)):
        wbd = wbd.at[i * g:(i + 1) * g, i * g:(i + 1) * g].set(w_pool[i])
    return wbd


def _pack_small(g1, w_pool, pool_scale, g2, g3, g4, extra):
    pad = lambda a: jnp.pad(a.reshape(1, -1), ((0, 0), (0, D_MODEL - a.size)))
    rows = [g1.reshape(1, -1), g2.reshape(1, -1), g3.reshape(1, -1), g4.reshape(1, -1),
            w_pool.reshape(-1, D_MODEL), pad(pool_scale), pad(extra)]
    buf = jnp.concatenate(rows, axis=0)
    return jnp.pad(buf, ((0, SMALL_ROWS - buf.shape[0]), (0, 0)))


def _unpack_small(buf):
    n_pool = len(POOL_WINDOWS) * (POOL_W // len(POOL_WINDOWS)) ** 2 // D_MODEL
    g = POOL_W // len(POOL_WINDOWS)
    return (buf[0:1], buf[4:4 + n_pool].reshape(1, len(POOL_WINDOWS), g, g), buf[4 + n_pool:5 + n_pool, :POOL_W],
            buf[1:2], buf[2:3], buf[3:4], buf[5 + n_pool])


def _local_step(x, tgt, g1, w_pool, pool_scale, g2, g3, g4, w_in_t, w_out, wg_t, wu_t, w_down):
    T = x.shape[0]
    cos, sin = _rope_tables(T)
    wbd = _block_diag(w_pool).astype(BF16)

    h1, u, q, k, v = _proj_fwd(x, g1, w_in_t, cos, sin, 512)
    pool = _pool_fwd(u, wbd, pool_scale, 512)
    prev = None
    for j, dil in enumerate(DILATIONS):
        prev = _attn_fwd(q, k, v, dil, prev, j == len(DILATIONS) - 1)
    attn, lse = prev
    cat, mix, x2, h2 = _mix_fwd(pool, attn, x, w_out, g2, g3, 512)
    gate, up, act = _ffn_up(h2, wg_t, wu_t, 256)
    df, dy, dg4, loss = _ffn_down_loss(act, w_down, x2, g4, tgt, 512)

    dgate, dup = _ffn_act_bwd(df, w_down, gate, up, 256)
    dx2, dmix, dg3, dg2 = _ffn_in_bwd(dgate, dup, wg_t, wu_t, x2, mix, dy, g3, g2, 512)
    dpool, dattn = _mix_bwd(dmix, w_out, 512)
    parts = [_attn_bwd(q, k, v, dattn, attn, lse, dil) for dil in DILATIONS]
    du, dwbd, dscale = _pool_bwd(u, dpool, wbd, pool_scale, 512)
    dproj = _dproj_combine(du, *[[p[j] for p in parts] for j in range(5)], cos, sin, 256)
    grad_x, dg1 = _proj_bwd(dproj, w_in_t, x, dx2, g1, 512)

    g = POOL_W // len(POOL_WINDOWS)
    dw_pool = jnp.stack([dwbd[i * g:(i + 1) * g, i * g:(i + 1) * g] for i in range(len(POOL_WINDOWS))])
    small = (dg1, dw_pool, dscale, dg2, dg3, dg4)
    big = (_wgrad(dproj, h1, "wgrad_in", IN_W // 2, 512), _wgrad(cat, dmix, "wgrad_out", D_MODEL, 512),
           _wgrad(dgate, h2, "wgrad_gate", D_FF // 2, 512), _wgrad(dup, h2, "wgrad_up", D_FF // 2, 512),
           _wgrad(act, df, "wgrad_down", D_FF // 2, 512))
    return loss, grad_x, small, big


def kernel(x, ln_pre_mix, w_in, w_pool, pool_scale, w_out, ln_post_mix, ln_pre_ffn, w_gate, w_up, w_down, ln_post_ffn, loss_target, m_ln_pre_mix, m_w_in, m_w_pool, m_pool_scale, m_w_out, m_ln_post_mix, m_ln_pre_ffn, m_w_gate, m_w_up, m_w_down, m_ln_post_ffn, v_ln_pre_mix, v_w_in, v_w_pool, v_pool_scale, v_w_out, v_ln_post_mix, v_ln_pre_ffn, v_w_gate, v_w_up, v_w_down, v_ln_post_ffn):
    shards = [w_in[0].T.astype(BF16), w_out[0].astype(BF16), w_gate[0].T.astype(BF16),
              w_up[0].T.astype(BF16), w_down[0].astype(BF16)]
    full = _exchange(shards, [False] * 5, "gather_weights")
    w_in_t, w_out_f, wg_t, wu_t, w_down_f = [f.reshape(-1, D_MODEL) for f in full]

    loss, grad_x, small, big = _local_step(
        x[0], loss_target[0], ln_pre_mix, w_pool[0], pool_scale, ln_post_mix, ln_pre_ffn, ln_post_ffn,
        w_in_t, w_out_f, wg_t, wu_t, w_down_f)

    small_buf = _pack_small(small[0], small[1], small[2], small[3], small[4], small[5], loss)
    blocks = [b.reshape(N_DEV, -1, D_MODEL) for b in big]
    got = _exchange(blocks + [small_buf], [True] * 5 + [False], "exchange_grads")
    sums = [_slot_sum(got[i], f"sum_grad_{i}", got[i].shape[1] // 2) for i in range(5)]
    small_sum = _slot_sum(got[5], "sum_small", SMALL_ROWS)

    g_in, g_out, g_gate, g_up, g_down = sums[0].T, sums[1], sums[2].T, sums[3].T, sums[4]
    upd = [_adamw(w[0], g, m[0], v[0], f"adamw_{nm}") for nm, w, g, m, v in (
        ("in", w_in, g_in, m_w_in, v_w_in), ("out", w_out, g_out, m_w_out, v_w_out),
        ("gate", w_gate, g_gate, m_w_gate, v_w_gate), ("up", w_up, g_up, m_w_up, v_w_up),
        ("down", w_down, g_down, m_w_down, v_w_down))]
    pack = lambda a, b, c, d, e, f: _pack_small(a, b[0], c, d, e, f, jnp.zeros((1,), F32))
    small_upd = _adamw(
        pack(ln_pre_mix, w_pool, pool_scale, ln_post_mix, ln_pre_ffn, ln_post_ffn), small_sum,
        pack(m_ln_pre_mix, m_w_pool, m_pool_scale, m_ln_post_mix, m_ln_pre_ffn, m_ln_post_ffn),
        pack(v_ln_pre_mix, v_w_pool, v_pool_scale, v_ln_post_mix, v_ln_pre_ffn, v_ln_post_ffn), "adamw_small")

    def tree(small6, big5):
        s1, spool, sscale, s2, s3, s4 = small6
        b_in, b_out, b_gate, b_up, b_down = [b[None] for b in big5]
        return [s1, b_in, spool, sscale, b_out, s2, s3, b_gate, b_up, b_down, s4]

    g_small = _unpack_small(small_sum)
    outs = [g_small[6][0], grad_x[None]]
    outs += tree(g_small[:6], [g_in, g_out, g_gate, g_up, g_down])
    for j in range(3):
        outs += tree(_unpack_small(small_upd[j])[:6], [u[j] for u in upd])
    return tuple(outs)
```

```python
import jax
import jax.numpy as jnp
from jax import lax
from jax.experimental import pallas as pl
from jax.experimental.pallas import tpu as pltpu

F32 = jnp.float32
BF16 = jnp.bfloat16

D_MODEL = 1024
POOL_W = 256
ATTN_W = 768
IN_W = 2560
D_FF = 2816
POOL_WINDOWS = (2, 4, 8, 16)
POOL_HALO = 16
DILATIONS = (1, 4, 16)
BLK = 128
LANES = 128
HEAD_DIM = 64
N_GROUPS = ATTN_W // LANES
ROPE_THETA = 10000.0
EPS = 1e-6
NEG = -1e30
N_DEV = 8
SMALL_ROWS = 24

ADAM_LR = 0.001
ADAM_B1 = 0.9
ADAM_B2 = 0.999
ADAM_EPS = 1e-08
ADAM_WD = 0.01
ADAM_STEP = 10

VMEM_LIMIT = 56 * 1024 * 1024


def _dot(a, b):
    return jnp.dot(a, b, preferred_element_type=F32)


def _dot_nt(a, b):
    return lax.dot_general(a, b, (((1,), (1,)), ((), ())), preferred_element_type=F32)


def _dot_tn(a, b):
    return lax.dot_general(a, b, (((0,), (0,)), ((), ())), preferred_element_type=F32)


def _params(n_grid):
    return pltpu.CompilerParams(dimension_semantics=("arbitrary",) * n_grid, vmem_limit_bytes=VMEM_LIMIT)


def _tok(tm, c):
    return pl.BlockSpec((tm, c), lambda i: (i, 0))


def _res(shape):
    return pl.BlockSpec(shape, lambda i: (0,) * len(shape))


def _rms_fwd(x, g):
    r = lax.rsqrt(jnp.mean(x * x, axis=-1, keepdims=True) + EPS)
    return x * r * g


def _rms_bwd(x, g, dy):
    r = lax.rsqrt(jnp.mean(x * x, axis=-1, keepdims=True) + EPS)
    xh = x * r
    gd = dy * g
    dx = r * (gd - xh * jnp.mean(gd * xh, axis=-1, keepdims=True))
    return dx, jnp.sum(dy * xh, axis=0, keepdims=True)


def _rope(x, c, s, sign):
    lane = lax.broadcasted_iota(jnp.int32, (x.shape[0], LANES), 1)
    first = (lane % HEAD_DIM) < (HEAD_DIM // 2)
    outs = []
    for g in range(x.shape[1] // LANES):
        xg = x[:, g * LANES:(g + 1) * LANES]
        rot = jnp.where(first, pltpu.roll(xg, LANES - HEAD_DIM // 2, 1), pltpu.roll(xg, HEAD_DIM // 2, 1))
        outs.append(xg * c + sign * (rot * s))
    return jnp.concatenate(outs, axis=1)


def _proj_fwd(x, g1, w_in_t, cos, sin, tm):
    T = x.shape[0]

    def body(x_ref, g_ref, w_ref, c_ref, s_ref, h_ref, u_ref, q_ref, k_ref, v_ref):
        h = _rms_fwd(x_ref[...], g_ref[...]).astype(BF16)
        h_ref[...] = h
        proj = _dot_nt(h, w_ref[...])
        c = c_ref[...]
        s = s_ref[...]
        u_ref[...] = proj[:, :POOL_W]
        _store_packed(q_ref, _rope(proj[:, POOL_W:POOL_W + ATTN_W], c, s, 1.0))
        _store_packed(k_ref, _rope(proj[:, POOL_W + ATTN_W:POOL_W + 2 * ATTN_W], c, s, 1.0))
        _store_packed(v_ref, proj[:, POOL_W + 2 * ATTN_W:])

    return pl.pallas_call(
        body, name="proj_fwd", grid=(T // tm,),
        in_specs=[_tok(tm, D_MODEL), _res((1, D_MODEL)), _res((IN_W, D_MODEL)), _tok(tm, LANES), _tok(tm, LANES)],
        out_specs=[_tok(tm, D_MODEL), _tok(tm, POOL_W)] + [_tok_packed(tm, ATTN_W)] * 3,
        out_shape=[jax.ShapeDtypeStruct((T, D_MODEL), BF16), jax.ShapeDtypeStruct((T, POOL_W), F32)]
        + [_packed(T, ATTN_W)] * 3,
        compiler_params=_params(1),
    )(x, g1, w_in_t, cos, sin)


def _pool_window(lane):
    return jnp.where(lane < 64, 2, jnp.where(lane < 128, 4, jnp.where(lane < 192, 8, 16)))


def _pool_select(lane, a2, a4, a8, a16):
    return jnp.where(lane < 64, a2, jnp.where(lane < 128, a4, jnp.where(lane < 192, a8, a16)))


def _pool_delta(cur, prev, i, tm):
    prev = jnp.where(i > 0, prev, 0.0)
    ext = jnp.concatenate([prev, cur], axis=0)
    s2 = ext + pltpu.roll(ext, 1, 0)
    s4 = s2 + pltpu.roll(s2, 2, 0)
    s8 = s4 + pltpu.roll(s4, 4, 0)
    s16 = s8 + pltpu.roll(s8, 8, 0)
    lane = lax.broadcasted_iota(jnp.int32, (tm, POOL_W), 1)
    row = lax.broadcasted_iota(jnp.int32, (tm, POOL_W), 0) + i * tm
    ws = _pool_select(lane, s2[POOL_HALO:], s4[POOL_HALO:], s8[POOL_HALO:], s16[POOL_HALO:])
    cnt = jnp.minimum(row + 1, _pool_window(lane)).astype(F32)
    return ws / cnt - cur


def _pool_fwd(u, wbd, scale, tm):
    T = u.shape[0]
    hb = tm // POOL_HALO

    def body(u_ref, prev_ref, w_ref, sc_ref, o_ref):
        d = _pool_delta(u_ref[...], prev_ref[...], pl.program_id(0), tm)
        o_ref[...] = (_dot(d.astype(BF16), w_ref[...]) * sc_ref[...]).astype(BF16)

    return pl.pallas_call(
        body, name="pool_fwd", grid=(T // tm,),
        in_specs=[_tok(tm, POOL_W), pl.BlockSpec((POOL_HALO, POOL_W), lambda i: (jnp.maximum(i * hb - 1, 0), 0)),
                  _res((POOL_W, POOL_W)), _res((1, POOL_W))],
        out_specs=_tok(tm, POOL_W),
        out_shape=jax.ShapeDtypeStruct((T, POOL_W), BF16),
        compiler_params=_params(1),
    )(u, u, wbd, scale)


def _attn_mask(n):
    qi = lax.broadcasted_iota(jnp.int32, (BLK, 2 * BLK), 0)
    kj = lax.broadcasted_iota(jnp.int32, (BLK, 2 * BLK), 1)
    dist = qi + BLK - kj
    return (dist >= 0) & (dist <= BLK) & ((kj >= BLK) | (n > 0))


def _head_col(tile, lane, h):
    return jnp.sum(jnp.where(lane == h, tile, 0.0), axis=1, keepdims=True)


def _attn_cols(dil):
    return ATTN_W // 2 if dil >= 16 else ATTN_W


def _attn_specs(dil, nb):
    cw = _attn_cols(dil)
    ch = BLK * dil
    wide = lambda f: pl.BlockSpec((cw // LANES, ch // 2, LANES), f)
    full = pl.BlockSpec((cw // LANES, ch, LANES), lambda n, j, r: (j, n, 0))
    cur = lambda n, j, r: (j, n, 0)
    prv = lambda n, j, r: (j, jnp.maximum(n - 1, 0), 0)
    prv_out = lambda n, j, r: (j, (n + nb - 1) % nb, 0)
    heads = pl.BlockSpec((ch, LANES), lambda n, j, r: (n, 0))
    return cw, wide(cur), wide(prv), wide(prv_out), heads, full


HIGH_HALF = 0xFFFF0000


def _pack(x):
    return pltpu.bitcast(x.astype(BF16), F32)


def _unpack(words):
    return pltpu.bitcast(words, BF16)


def _packed(rows, cols):
    return jax.ShapeDtypeStruct((cols // LANES, rows // 2, LANES), F32)


def _tok_packed(tm, cols):
    return pl.BlockSpec((cols // LANES, tm // 2, LANES), lambda i: (0, i, 0))


def _store_packed(ref, x):
    for g in range(x.shape[1] // LANES):
        ref[g] = _pack(x[:, g * LANES:(g + 1) * LANES])


def _load_packed(ref):
    return jnp.concatenate([_unpack(ref[g]) for g in range(ref.shape[0])], axis=1)


def _load_streams(ref, dil, r2, sl):
    if dil == 1:
        return [_unpack(ref[sl])]
    words = lax.bitcast_convert_type(ref.at[sl][pl.ds(r2, BLK, stride=dil // 2), :], jnp.uint32)
    even = lax.bitcast_convert_type(words << 16, F32).astype(BF16)
    odd = lax.bitcast_convert_type(words & jnp.uint32(HIGH_HALF), F32).astype(BF16)
    return [even, odd]


def _load_streams_f32(ref, dil, r2, sl):
    ref = ref if sl is None else ref.at[sl]
    if dil == 1:
        return [ref[...]]
    return [ref[pl.ds(2 * r2 + e, BLK, stride=dil), :] for e in range(2)]


def _store_streams_f32(ref, dil, r2, sl, tiles):
    ref = ref if sl is None else ref.at[sl]
    if dil == 1:
        ref[...] = tiles[0]
    else:
        for e, t in enumerate(tiles):
            ref[pl.ds(2 * r2 + e, BLK, stride=dil), :] = t


def _store_streams(ref, dil, r2, sl, tiles):
    if dil == 1:
        ref[sl] = _pack(tiles[0])
    else:
        even, odd = [lax.bitcast_convert_type(t.astype(BF16).astype(F32), jnp.uint32) for t in tiles]
        words = (odd & jnp.uint32(HIGH_HALF)) | (even >> 16)
        ref.at[sl][pl.ds(r2, BLK, stride=dil // 2), :] = lax.bitcast_convert_type(words, F32)


def _attn_fwd(q, k, v, dil, prev, last):
    T = 2 * q.shape[1]
    nb = T // (BLK * dil)
    first = prev is None
    cw, cur, prv, _, heads, full = _attn_specs(dil, nb)
    ncb = ATTN_W // cw
    heads_per_step = cw // HEAD_DIM
    n_str = min(dil, 2)
    everything = None

    def body(*refs):
        if first:
            q_ref, kc_ref, kp_ref, vc_ref, vp_ref, acc_ref, lse_ref = refs
        else:
            q_ref, kc_ref, kp_ref, vc_ref, vp_ref, acc_in, lse_in, acc_ref, lse_ref = refs
        j = pl.program_id(1)
        r2 = pl.program_id(2)
        valid = _attn_mask(pl.program_id(0))
        lane = lax.broadcasted_iota(jnp.int32, (BLK, LANES), 1)
        lo = lane < HEAD_DIM
        lse_tiles = [jnp.zeros((BLK, LANES), F32) for _ in range(n_str)]
        if not first:
            lse_old_tiles = _load_streams_f32(lse_in, dil, r2, everything)
        for g in range(cw // LANES):
            sl = g
            qs, kcs, kps, vcs, vps = [_load_streams(r, dil, r2, sl) for r in (q_ref, kc_ref, kp_ref, vc_ref, vp_ref)]
            if not first:
                olds = _load_streams_f32(acc_in, dil, r2, sl)
            pairs = []
            for e in range(n_str):
                qg = qs[e] * 0.125
                kcat = jnp.concatenate([kps[e], kcs[e]], axis=0)
                vcat = jnp.concatenate([vps[e], vcs[e]], axis=0)
                pair = None
                for hh in range(2):
                    h = j * heads_per_step + 2 * g + hh
                    hm = lo if hh == 0 else jnp.logical_not(lo)
                    s = _dot_nt(jnp.where(hm, qg, jnp.zeros_like(qg)), kcat)
                    s = jnp.where(valid, s, NEG)
                    m = jnp.max(s, axis=1, keepdims=True)
                    p = jnp.exp(s - m)
                    den = jnp.sum(p, axis=1, keepdims=True)
                    o = _dot(p.astype(BF16), vcat) / den
                    lse = m + jnp.log(den)
                    if not first:
                        lse_old = _head_col(lse_old_tiles[e], lane, h)
                        mx = jnp.maximum(lse_old, lse)
                        new = mx + jnp.log(jnp.exp(lse_old - mx) + jnp.exp(lse - mx))
                        o = olds[e] * jnp.exp(lse_old - new) + o * jnp.exp(lse - new)
                        lse = new
                    pair = o if hh == 0 else jnp.where(lo, pair, o)
                    lse_tiles[e] = jnp.where(lane == h, lse, lse_tiles[e])
                pairs.append(pair)
            (_store_streams if last else _store_streams_f32)(acc_ref, dil, r2, sl, pairs)
        if ncb == 1:
            _store_streams_f32(lse_ref, dil, r2, everything, lse_tiles)
        else:
            @pl.when(j == 0)
            def _():
                _store_streams_f32(lse_ref, dil, r2, everything, lse_tiles)

            @pl.when(j > 0)
            def _():
                before = _load_streams_f32(lse_ref, dil, r2, everything)
                _store_streams_f32(lse_ref, dil, r2, everything, [a + b for a, b in zip(before, lse_tiles)])

    ins = [q, k, k, v, v]
    in_specs = [cur, cur, prv, cur, prv]
    if not first:
        ins += [prev[0], prev[1]]
        in_specs += [full, heads]
    return pl.pallas_call(
        body, name=f"attn_fwd_d{dil}", grid=(nb, ncb, max(dil // 2, 1)),
        in_specs=in_specs, out_specs=[cur if last else full, heads],
        out_shape=[_packed(T, ATTN_W) if last else jax.ShapeDtypeStruct((N_GROUPS, T, LANES), F32),
                   jax.ShapeDtypeStruct((T, LANES), F32)],
        compiler_params=_params(3),
    )(*ins)


def _mix_fwd(pool, attn, x, w_out, g2, g3, tm):
    T = x.shape[0]

    def body(p_ref, a_ref, x_ref, w_ref, g2_ref, g3_ref, cat_ref, mix_ref, x2_ref, h2_ref):
        p = p_ref[...]
        a = _load_packed(a_ref)
        cat_ref[...] = jnp.concatenate([p, a], axis=1)
        mix = _dot(p, w_ref[:POOL_W, :]) + _dot(a, w_ref[POOL_W:, :])
        mix_ref[...] = mix
        x2 = x_ref[...] + _rms_fwd(mix, g2_ref[...])
        x2_ref[...] = x2
        h2_ref[...] = _rms_fwd(x2, g3_ref[...]).astype(BF16)

    return pl.pallas_call(
        body, name="mix_fwd", grid=(T // tm,),
        in_specs=[_tok(tm, POOL_W), _tok_packed(tm, ATTN_W), _tok(tm, D_MODEL), _res((D_MODEL, D_MODEL)),
                  _res((1, D_MODEL)), _res((1, D_MODEL))],
        out_specs=[_tok(tm, D_MODEL)] * 4,
        out_shape=[jax.ShapeDtypeStruct((T, D_MODEL), BF16), jax.ShapeDtypeStruct((T, D_MODEL), F32),
                   jax.ShapeDtypeStruct((T, D_MODEL), F32), jax.ShapeDtypeStruct((T, D_MODEL), BF16)],
        compiler_params=_params(1),
    )(pool, attn, x, w_out, g2, g3)


def _ffn_up(h2, wg_t, wu_t, tm):
    T = h2.shape[0]

    def body(h_ref, wg_ref, wu_ref, g_ref, u_ref, a_ref):
        h = h_ref[...]
        gate = _dot_nt(h, wg_ref[...])
        up = _dot_nt(h, wu_ref[...])
        g_ref[...] = gate.astype(BF16)
        u_ref[...] = up.astype(BF16)
        a_ref[...] = (gate * (1.0 / (1.0 + jnp.exp(-gate))) * up).astype(BF16)

    return pl.pallas_call(
        body, name="ffn_up", grid=(T // tm,),
        in_specs=[_tok(tm, D_MODEL), _res((D_FF, D_MODEL)), _res((D_FF, D_MODEL))],
        out_specs=[_tok(tm, D_FF)] * 3,
        out_shape=[jax.ShapeDtypeStruct((T, D_FF), BF16)] * 3,
        compiler_params=_params(1),
    )(h2, wg_t, wu_t)


def _ffn_down_loss(act, w_down, x2, g4, tgt, tm):
    T = act.shape[0]

    def body(a_ref, w_ref, x2_ref, g_ref, t_ref, df_ref, dy_ref, dg_ref, loss_ref):
        i = pl.program_id(0)

        @pl.when(i == 0)
        def _():
            dg_ref[...] = jnp.zeros_like(dg_ref)
            loss_ref[...] = jnp.zeros_like(loss_ref)

        f = _dot(a_ref[...], w_ref[...])
        g = g_ref[...]
        err = x2_ref[...] + _rms_fwd(f, g) - t_ref[...]
        loss_ref[...] += 0.5 * jnp.sum(jnp.mean(err * err, axis=-1, keepdims=True), axis=0, keepdims=True)
        dy = err * (1.0 / D_MODEL)
        dy_ref[...] = dy
        df, dg = _rms_bwd(f, g, dy)
        dg_ref[...] += dg
        df_ref[...] = df.astype(BF16)

    return pl.pallas_call(
        body, name="ffn_down_loss", grid=(T // tm,),
        in_specs=[_tok(tm, D_FF), _res((D_FF, D_MODEL)), _tok(tm, D_MODEL), _res((1, D_MODEL)), _tok(tm, D_MODEL)],
        out_specs=[_tok(tm, D_MODEL), _tok(tm, D_MODEL), _res((1, D_MODEL)), _res((1, 1))],
        out_shape=[jax.ShapeDtypeStruct((T, D_MODEL), BF16), jax.ShapeDtypeStruct((T, D_MODEL), F32),
                   jax.ShapeDtypeStruct((1, D_MODEL), F32), jax.ShapeDtypeStruct((1, 1), F32)],
        compiler_params=_params(1),
    )(act, w_down, x2, g4, tgt)


def _ffn_act_bwd(df, w_down, gate, up, tm):
    T = df.shape[0]

    def body(df_ref, w_ref, g_ref, u_ref, dg_ref, du_ref):
        dact = _dot_nt(df_ref[...], w_ref[...])
        g = g_ref[...].astype(F32)
        u = u_ref[...].astype(F32)
        sg = 1.0 / (1.0 + jnp.exp(-g))
        dg_ref[...] = (dact * u * (sg * (1.0 + g * (1.0 - sg)))).astype(BF16)
        du_ref[...] = (dact * (g * sg)).astype(BF16)

    return pl.pallas_call(
        body, name="ffn_act_bwd", grid=(T // tm,),
        in_specs=[_tok(tm, D_MODEL), _res((D_FF, D_MODEL)), _tok(tm, D_FF), _tok(tm, D_FF)],
        out_specs=[_tok(tm, D_FF)] * 2,
        out_shape=[jax.ShapeDtypeStruct((T, D_FF), BF16)] * 2,
        compiler_params=_params(1),
    )(df, w_down, gate, up)


def _ffn_in_bwd(dgate, dup, wg_t, wu_t, x2, mix, dy, g3, g2, tm):
    T = x2.shape[0]

    def body(dg_ref, du_ref, wg_ref, wu_ref, x2_ref, mix_ref, dy_ref, g3_ref, g2_ref,
             dx2_ref, dmix_ref, dg3_ref, dg2_ref):
        @pl.when(pl.program_id(0) == 0)
        def _():
            dg3_ref[...] = jnp.zeros_like(dg3_ref)
            dg2_ref[...] = jnp.zeros_like(dg2_ref)

        dh2 = _dot(dg_ref[...], wg_ref[...]) + _dot(du_ref[...], wu_ref[...])
        dn, dg3 = _rms_bwd(x2_ref[...], g3_ref[...], dh2)
        dx2 = dy_ref[...] + dn
        dx2_ref[...] = dx2
        dg3_ref[...] += dg3
        dmix, dg2 = _rms_bwd(mix_ref[...], g2_ref[...], dx2)
        dg2_ref[...] += dg2
        dmix_ref[...] = dmix.astype(BF16)

    return pl.pallas_call(
        body, name="ffn_in_bwd", grid=(T // tm,),
        in_specs=[_tok(tm, D_FF), _tok(tm, D_FF), _res((D_FF, D_MODEL)), _res((D_FF, D_MODEL)),
                  _tok(tm, D_MODEL), _tok(tm, D_MODEL), _tok(tm, D_MODEL), _res((1, D_MODEL)), _res((1, D_MODEL))],
        out_specs=[_tok(tm, D_MODEL), _tok(tm, D_MODEL), _res((1, D_MODEL)), _res((1, D_MODEL))],
        out_shape=[jax.ShapeDtypeStruct((T, D_MODEL), F32), jax.ShapeDtypeStruct((T, D_MODEL), BF16),
                   jax.ShapeDtypeStruct((1, D_MODEL), F32), jax.ShapeDtypeStruct((1, D_MODEL), F32)],
        compiler_params=_params(1),
    )(dgate, dup, wg_t, wu_t, x2, mix, dy, g3, g2)


def _mix_bwd(dmix, w_out, tm):
    T = dmix.shape[0]

    def body(d_ref, w_ref, dp_ref, da_ref):
        dcat = _dot_nt(d_ref[...], w_ref[...])
        dp_ref[...] = dcat[:, :POOL_W].astype(BF16)
        _store_packed(da_ref, dcat[:, POOL_W:])

    return pl.pallas_call(
        body, name="mix_bwd", grid=(T // tm,),
        in_specs=[_tok(tm, D_MODEL), _res((D_MODEL, D_MODEL))],
        out_specs=[_tok(tm, POOL_W), _tok_packed(tm, ATTN_W)],
        out_shape=[jax.ShapeDtypeStruct((T, POOL_W), BF16), _packed(T, ATTN_W)],
        compiler_params=_params(1),
    )(dmix, w_out)


def _attn_bwd(q, k, v, dout, out, lse, dil):
    T = 2 * q.shape[1]
    nb = T // (BLK * dil)
    cw, cur, prv, prv_out, heads, _ = _attn_specs(dil, nb)
    ncb = ATTN_W // cw
    heads_per_step = cw // HEAD_DIM
    n_str = min(dil, 2)

    def body(q_ref, kc_ref, kp_ref, vc_ref, vp_ref, do_ref, o_ref, lse_ref,
             dq_ref, dkc_ref, dkp_ref, dvc_ref, dvp_ref):
        j = pl.program_id(1)
        r2 = pl.program_id(2)
        valid = _attn_mask(pl.program_id(0))
        lane = lax.broadcasted_iota(jnp.int32, (BLK, LANES), 1)
        lo = lane < HEAD_DIM
        lse_tiles = _load_streams_f32(lse_ref, dil, r2, None)
        for g in range(cw // LANES):
            sl = g
            qs, kcs, kps, vcs, vps, dos, os_ = [
                _load_streams(r, dil, r2, sl) for r in (q_ref, kc_ref, kp_ref, vc_ref, vp_ref, do_ref, o_ref)]
            dqs, dks, dvs = [], [], []
            for e in range(n_str):
                qg = qs[e] * 0.125
                dog = dos[e]
                kcat = jnp.concatenate([kps[e], kcs[e]], axis=0)
                vcat = jnp.concatenate([vps[e], vcs[e]], axis=0)
                prod = dog.astype(F32) * os_[e].astype(F32)
                dk = jnp.zeros((2 * BLK, LANES), F32)
                dv = jnp.zeros((2 * BLK, LANES), F32)
                dq = None
                for hh in range(2):
                    hm = lo if hh == 0 else jnp.logical_not(lo)
                    qa = jnp.where(hm, qg, jnp.zeros_like(qg))
                    doa = jnp.where(hm, dog, jnp.zeros_like(dog))
                    s = jnp.where(valid, _dot_nt(qa, kcat), NEG)
                    p = jnp.exp(s - _head_col(lse_tiles[e], lane, j * heads_per_step + 2 * g + hh))
                    dsum = jnp.sum(jnp.where(hm, prod, 0.0), axis=1, keepdims=True)
                    ds = (p * (_dot_nt(doa, vcat) - dsum)).astype(BF16)
                    dv = dv + _dot_tn(p.astype(BF16), doa)
                    dk = dk + _dot_tn(ds, qa)
                    dqh = _dot(ds, kcat) * 0.125
                    dq = dqh if hh == 0 else jnp.where(lo, dq, dqh)
                dqs.append(dq)
                dks.append(dk)
                dvs.append(dv)
            _store_streams(dq_ref, dil, r2, sl, dqs)
            _store_streams(dkp_ref, dil, r2, sl, [t[:BLK] for t in dks])
            _store_streams(dkc_ref, dil, r2, sl, [t[BLK:] for t in dks])
            _store_streams(dvp_ref, dil, r2, sl, [t[:BLK] for t in dvs])
            _store_streams(dvc_ref, dil, r2, sl, [t[BLK:] for t in dvs])

    return pl.pallas_call(
        body, name=f"attn_bwd_d{dil}", grid=(nb, ncb, max(dil // 2, 1)),
        in_specs=[cur, cur, prv, cur, prv, cur, cur, heads],
        out_specs=[cur, cur, prv_out, cur, prv_out],
        out_shape=[_packed(T, ATTN_W)] * 5,
        compiler_params=_params(3),
    )(q, k, k, v, v, dout, out, lse)


def _pool_bwd(u, dy, wbd, scale, tm):
    T = u.shape[0]
    nt = T // tm
    hb = tm // POOL_HALO

    def body(u_ref, prev_ref, dy_ref, next_ref, w_ref, sc_ref, du_ref, dw_ref, dsc_ref):
        i = pl.program_id(0)

        @pl.when(i == 0)
        def _():
            dw_ref[...] = jnp.zeros_like(dw_ref)
            dsc_ref[...] = jnp.zeros_like(dsc_ref)

        w = w_ref[...]
        sc = sc_ref[...]
        d = _pool_delta(u_ref[...], prev_ref[...], i, tm).astype(BF16)
        dyc = dy_ref[...].astype(F32)
        dsc_ref[...] += jnp.sum(dyc * _dot(d, w), axis=0, keepdims=True)
        nxt = jnp.where(i < nt - 1, next_ref[...].astype(F32), 0.0)
        dypre = (jnp.concatenate([dyc, nxt], axis=0) * sc).astype(BF16)
        dw_ref[...] += _dot_tn(d, dypre[:tm])
        dd = _dot_nt(dypre, w)
        n = tm + POOL_HALO
        lane = lax.broadcasted_iota(jnp.int32, (n, POOL_W), 1)
        row = lax.broadcasted_iota(jnp.int32, (n, POOL_W), 0) + i * tm
        gx = dd / jnp.minimum(row + 1, _pool_window(lane)).astype(F32)
        a2 = gx + pltpu.roll(gx, n - 1, 0)
        a4 = a2 + pltpu.roll(a2, n - 2, 0)
        a8 = a4 + pltpu.roll(a4, n - 4, 0)
        a16 = a8 + pltpu.roll(a8, n - 8, 0)
        fs = _pool_select(lane[:tm], a2[:tm], a4[:tm], a8[:tm], a16[:tm])
        du_ref[...] = (fs - dd[:tm]).astype(BF16)

    return pl.pallas_call(
        body, name="pool_bwd", grid=(nt,),
        in_specs=[_tok(tm, POOL_W), pl.BlockSpec((POOL_HALO, POOL_W), lambda i: (jnp.maximum(i * hb - 1, 0), 0)),
                  _tok(tm, POOL_W), pl.BlockSpec((POOL_HALO, POOL_W), lambda i: (jnp.minimum((i + 1) * hb, nt * hb - 1), 0)),
                  _res((POOL_W, POOL_W)), _res((1, POOL_W))],
        out_specs=[_tok(tm, POOL_W), _res((POOL_W, POOL_W)), _res((1, POOL_W))],
        out_shape=[jax.ShapeDtypeStruct((T, POOL_W), BF16), jax.ShapeDtypeStruct((POOL_W, POOL_W), F32),
                   jax.ShapeDtypeStruct((1, POOL_W), F32)],
        compiler_params=_params(1),
    )(u, u, dy, dy, wbd, scale)


def _dproj_combine(du, dqs, dkcs, dkps, dvcs, dvps, cos, sin, tm):
    T = du.shape[0]
    n_cfg = len(dqs)

    def body(*refs):
        du_ref = refs[0]
        groups = [refs[1 + j * n_cfg:1 + (j + 1) * n_cfg] for j in range(5)]
        c_ref, s_ref, out_ref = refs[1 + 5 * n_cfg:]
        tot = lambda rs: sum(_load_packed(r).astype(F32) for r in rs)
        c = c_ref[...]
        s = s_ref[...]
        dq = _rope(tot(groups[0]), c, s, -1.0)
        dk = _rope(tot(groups[1]) + tot(groups[2]), c, s, -1.0)
        dv = tot(groups[3]) + tot(groups[4])
        out_ref[...] = jnp.concatenate([du_ref[...], dq.astype(BF16), dk.astype(BF16), dv.astype(BF16)], axis=1)

    return pl.pallas_call(
        body, name="dproj_combine", grid=(T // tm,),
        in_specs=[_tok(tm, POOL_W)] + [_tok_packed(tm, ATTN_W)] * (5 * n_cfg) + [_tok(tm, LANES)] * 2,
        out_specs=_tok(tm, IN_W),
        out_shape=jax.ShapeDtypeStruct((T, IN_W), BF16),
        compiler_params=_params(1),
    )(du, *dqs, *dkcs, *dkps, *dvcs, *dvps, cos, sin)


def _proj_bwd(dproj, w_in_t, x, dx2, g1, tm):
    T = x.shape[0]

    def body(d_ref, w_ref, x_ref, r_ref, g_ref, dx_ref, dg_ref):
        @pl.when(pl.program_id(0) == 0)
        def _():
            dg_ref[...] = jnp.zeros_like(dg_ref)

        dn, dg = _rms_bwd(x_ref[...], g_ref[...], _dot(d_ref[...], w_ref[...]))
        dg_ref[...] += dg
        dx_ref[...] = r_ref[...] + dn

    return pl.pallas_call(
        body, name="proj_bwd", grid=(T // tm,),
        in_specs=[_tok(tm, IN_W), _res((IN_W, D_MODEL)), _tok(tm, D_MODEL), _tok(tm, D_MODEL), _res((1, D_MODEL))],
        out_specs=[_tok(tm, D_MODEL), _res((1, D_MODEL))],
        out_shape=[jax.ShapeDtypeStruct((T, D_MODEL), F32), jax.ShapeDtypeStruct((1, D_MODEL), F32)],
        compiler_params=_params(1),
    )(dproj, w_in_t, x, dx2, g1)


def _wgrad(a, b, name, tile_m, tk):
    T, M = a.shape
    N = b.shape[1]
    nk = T // tk

    def body(a_ref, b_ref, o_ref, acc_ref):
        kk = pl.program_id(1)

        @pl.when(kk == 0)
        def _():
            acc_ref[...] = jnp.zeros_like(acc_ref)

        acc_ref[...] += _dot_tn(a_ref[...], b_ref[...])

        @pl.when(kk == nk - 1)
        def _():
            o_ref[...] = acc_ref[...].astype(BF16)

    return pl.pallas_call(
        body, name=name, grid=(M // tile_m, nk),
        in_specs=[pl.BlockSpec((tk, tile_m), lambda j, kk: (kk, j)), pl.BlockSpec((tk, N), lambda j, kk: (kk, 0))],
        out_specs=pl.BlockSpec((tile_m, N), lambda j, kk: (j, 0)),
        out_shape=jax.ShapeDtypeStruct((M, N), BF16),
        scratch_shapes=[pltpu.VMEM((tile_m, N), F32)],
        compiler_params=_params(2),
    )(a, b)


def _exchange(arrs, scatter, name):
    n = len(arrs)
    out_shapes = [jax.ShapeDtypeStruct((N_DEV,) + (a.shape[1:] if sc else a.shape), a.dtype)
                  for a, sc in zip(arrs, scatter)]

    def body(*refs):
        ins, outs = refs[:n], refs[n:2 * n]
        send_sems, recv_sems, loc_sems = refs[2 * n:]
        x, y, c = lax.axis_index("x"), lax.axis_index("y"), lax.axis_index("c")
        me = 4 * x + 2 * y + c
        local, sends, recvs = [], [], []
        for i in range(n):
            own = ins[i].at[me] if scatter[i] else ins[i]
            loc = pltpu.make_async_copy(own, outs[i].at[me], loc_sems.at[i])
            loc.start()
            local.append(loc)
            for kbits in range(1, N_DEV):
                px = 1 - x if kbits & 4 else x
                py = 1 - y if kbits & 2 else y
                pc = 1 - c if kbits & 1 else c
                pid = 4 * px + 2 * py + pc
                src = ins[i].at[pid] if scatter[i] else ins[i]
                cp = pltpu.make_async_remote_copy(
                    src_ref=src, dst_ref=outs[i].at[me],
                    send_sem=send_sems.at[i, kbits - 1], recv_sem=recv_sems.at[i, kbits - 1],
                    device_id=(px, py, pc), device_id_type=pl.DeviceIdType.MESH)
                cp.start()
                sends.append(cp)
                recvs.append(pltpu.make_async_remote_copy(
                    src_ref=src, dst_ref=outs[i].at[pid],
                    send_sem=send_sems.at[i, kbits - 1], recv_sem=recv_sems.at[i, kbits - 1],
                    device_id=(px, py, pc), device_id_type=pl.DeviceIdType.MESH))
        for cp in recvs:
            cp.wait_recv()
        for cp in sends:
            cp.wait_send()
        for cp in local:
            cp.wait()

    hbm = pl.BlockSpec(memory_space=pl.ANY)
    return pl.pallas_call(
        body, name=name, in_specs=[hbm] * n, out_specs=[hbm] * n, out_shape=out_shapes,
        scratch_shapes=[pltpu.SemaphoreType.DMA((n, N_DEV - 1)), pltpu.SemaphoreType.DMA((n, N_DEV - 1)),
                        pltpu.SemaphoreType.DMA((n,))],
    )(*arrs)


def _slot_sum(parts, name, tr):
    _, R, C = parts.shape

    def body(p_ref, o_ref):
        acc = p_ref[0].astype(F32)
        for s in range(1, N_DEV):
            acc = acc + p_ref[s].astype(F32)
        o_ref[...] = acc

    return pl.pallas_call(
        body, name=name, grid=(R // tr,),
        in_specs=[pl.BlockSpec((N_DEV, tr, C), lambda i: (0, i, 0))],
        out_specs=pl.BlockSpec((tr, C), lambda i: (i, 0)),
        out_shape=jax.ShapeDtypeStruct((R, C), F32),
        compiler_params=_params(1),
    )(parts)


def _adamw(w, g, m, v, name):
    def body(w_ref, g_ref, m_ref, v_ref, d_ref, nm_ref, nv_ref):
        g = g_ref[...]
        nm = ADAM_B1 * m_ref[...] + (1.0 - ADAM_B1) * g
        nv = ADAM_B2 * v_ref[...] + (1.0 - ADAM_B2) * jnp.square(g)
        m_hat = nm / (1.0 - ADAM_B1 ** ADAM_STEP)
        v_hat = nv / (1.0 - ADAM_B2 ** ADAM_STEP)
        d_ref[...] = -ADAM_LR * (m_hat / (jnp.sqrt(v_hat) + ADAM_EPS) + ADAM_WD * w_ref[...])
        nm_ref[...] = nm
        nv_ref[...] = nv

    return pl.pallas_call(
        body, name=name, out_shape=[jax.ShapeDtypeStruct(w.shape, F32)] * 3,
        compiler_params=pltpu.CompilerParams(vmem_limit_bytes=VMEM_LIMIT),
    )(w, g, m, v)


def _rope_tables(T):
    half = HEAD_DIM // 2
    freqs = ROPE_THETA ** (-jnp.arange(half, dtype=F32) * (2.0 / HEAD_DIM))
    ang = jnp.arange(T).astype(F32)[:, None] * freqs[None, :]
    c, s = jnp.cos(ang), jnp.sin(ang)
    return jnp.concatenate([c, c, c, c], axis=1), jnp.concatenate([-s, s, -s, s], axis=1)


def _block_diag(w_pool):
    wbd = jnp.zeros((POOL_W, POOL_W), F32)
    g = POOL_W // len(POOL_WINDOWS)
    for i in range(len(POOL_WINDOWS)):
        wbd = wbd.at[i * g:(i + 1) * g, i * g:(i + 1) * g].set(w_pool[i])
    return wbd


def _pack_small(g1, w_pool, pool_scale, g2, g3, g4, extra):
    pad = lambda a: jnp.pad(a.reshape(1, -1), ((0, 0), (0, D_MODEL - a.size)))
    rows = [g1.reshape(1, -1), g2.reshape(1, -1), g3.reshape(1, -1), g4.reshape(1, -1),
            w_pool.reshape(-1, D_MODEL), pad(pool_scale), pad(extra)]
    buf = jnp.concatenate(rows, axis=0)
    return jnp.pad(buf, ((0, SMALL_ROWS - buf.shape[0]), (0, 0)))


def _unpack_small(buf):
    n_pool = len(POOL_WINDOWS) * (POOL_W // len(POOL_WINDOWS)) ** 2 // D_MODEL
    g = POOL_W // len(POOL_WINDOWS)
    return (buf[0:1], buf[4:4 + n_pool].reshape(1, len(POOL_WINDOWS), g, g), buf[4 + n_pool:5 + n_pool, :POOL_W],
            buf[1:2], buf[2:3], buf[3:4], buf[5 + n_pool])


def _local_step(x, tgt, g1, w_pool, pool_scale, g2, g3, g4, w_in_t, w_out, wg_t, wu_t, w_down):
    T = x.shape[0]
    cos, sin = _rope_tables(T)
    wbd = _block_diag(w_pool).astype(BF16)

    h1, u, q, k, v = _proj_fwd(x, g1, w_in_t, cos, sin, 512)
    pool = _pool_fwd(u, wbd, pool_scale, 512)
    prev = None
    for j, dil in enumerate(DILATIONS):
        prev = _attn_fwd(q, k, v, dil, prev, j == len(DILATIONS) - 1)
    attn, lse = prev
    cat, mix, x2, h2 = _mix_fwd(pool, attn, x, w_out, g2, g3, 512)
    gate, up, act = _ffn_up(h2, wg_t, wu_t, 256)
    df, dy, dg4, loss = _ffn_down_loss(act, w_down, x2, g4, tgt, 512)

    dgate, dup = _ffn_act_bwd(df, w_down, gate, up, 256)
    dx2, dmix, dg3, dg2 = _ffn_in_bwd(dgate, dup, wg_t, wu_t, x2, mix, dy, g3, g2, 512)
    dpool, dattn = _mix_bwd(dmix, w_out, 512)
    parts = [_attn_bwd(q, k, v, dattn, attn, lse, dil) for dil in DILATIONS]
    du, dwbd, dscale = _pool_bwd(u, dpool, wbd, pool_scale, 512)
    dproj = _dproj_combine(du, *[[p[j] for p in parts] for j in range(5)], cos, sin, 256)
    grad_x, dg1 = _proj_bwd(dproj, w_in_t, x, dx2, g1, 512)

    g = POOL_W // len(POOL_WINDOWS)
    dw_pool = jnp.stack([dwbd[i * g:(i + 1) * g, i * g:(i + 1) * g] for i in range(len(POOL_WINDOWS))])
    small = (dg1, dw_pool, dscale, dg2, dg3, dg4)
    big = (_wgrad(dproj, h1, "wgrad_in", IN_W // 2, 512), _wgrad(cat, dmix, "wgrad_out", D_MODEL, 512),
           _wgrad(dgate, h2, "wgrad_gate", D_FF // 2, 512), _wgrad(dup, h2, "wgrad_up", D_FF // 2, 512),
           _wgrad(act, df, "wgrad_down", D_FF // 2, 512))
    return loss, grad_x, small, big


def kernel(x, ln_pre_mix, w_in, w_pool, pool_scale, w_out, ln_post_mix, ln_pre_ffn, w_gate, w_up, w_down, ln_post_ffn, loss_target, m_ln_pre_mix, m_w_in, m_w_pool, m_pool_scale, m_w_out, m_ln_post_mix, m_ln_pre_ffn, m_w_gate, m_w_up, m_w_down, m_ln_post_ffn, v_ln_pre_mix, v_w_in, v_w_pool, v_pool_scale, v_w_out, v_ln_post_mix, v_ln_pre_ffn, v_w_gate, v_w_up, v_w_down, v_ln_post_ffn):
    shards = [w_in[0].T.astype(BF16), w_out[0].astype(BF16), w_gate[0].T.astype(BF16),
              w_up[0].T.astype(BF16), w_down[0].astype(BF16)]
    full = _exchange(shards, [False] * 5, "gather_weights")
    w_in_t, w_out_f, wg_t, wu_t, w_down_f = [f.reshape(-1, D_MODEL) for f in full]

    loss, grad_x, small, big = _local_step(
        x[0], loss_target[0], ln_pre_mix, w_pool[0], pool_scale, ln_post_mix, ln_pre_ffn, ln_post_ffn,
        w_in_t, w_out_f, wg_t, wu_t, w_down_f)

    small_buf = _pack_small(small[0], small[1], small[2], small[3], small[4], small[5], loss)
    blocks = [b.reshape(N_DEV, -1, D_MODEL) for b in big]
    got = _exchange(blocks + [small_buf], [True] * 5 + [False], "exchange_grads")
    sums = [_slot_sum(got[i], f"sum_grad_{i}", got[i].shape[1] // 2) for i in range(5)]
    small_sum = _slot_sum(got[5], "sum_small", SMALL_ROWS)

    g_in, g_out, g_gate, g_up, g_down = sums[0].T, sums[1], sums[2].T, sums[3].T, sums[4]
    upd = [_adamw(w[0], g, m[0], v[0], f"adamw_{nm}") for nm, w, g, m, v in (
        ("in", w_in, g_in, m_w_in, v_w_in), ("out", w_out, g_out, m_w_out, v_w_out),
        ("gate", w_gate, g_gate, m_w_gate, v_w_gate), ("up", w_up, g_up, m_w_up, v_w_up),
        ("down", w_down, g_down, m_w_down, v_w_down))]
    pack = lambda a, b, c, d, e, f: _pack_small(a, b[0], c, d, e, f, jnp.zeros((1,), F32))
    small_upd = _adamw(
        pack(ln_pre_mix, w_pool, pool_scale, ln_post_mix, ln_pre_ffn, ln_post_ffn), small_sum,
        pack(m_ln_pre_mix, m_w_pool, m_pool_scale, m_ln_post_mix, m_ln_pre_ffn, m_ln_post_ffn),
        pack(v_ln_pre_mix, v_w_pool, v_pool_scale, v_ln_post_mix, v_ln_pre_ffn, v_ln_post_ffn), "adamw_small")

    def tree(small6, big5):
        s1, spool, sscale, s2, s3, s4 = small6
        b_in, b_out, b_gate, b_up, b_down = [b[None] for b in big5]
        return [s1, b_in, spool, sscale, b_out, s2, s3, b_gate, b_up, b_down, s4]

    g_small = _unpack_small(small_sum)
    outs = [g_small[6][0], grad_x[None]]
    outs += tree(g_small[:6], [g_in, g_out, g_gate, g_up, g_down])
    for j in range(3):
        outs += tree(_unpack_small(small_upd[j])[:6], [u[j] for u in upd])
    return tuple(outs)
```

```python
import jax
import jax.numpy as jnp
from jax import lax
from jax.experimental import pallas as pl
from jax.experimental.pallas import tpu as pltpu

F32 = jnp.float32
BF16 = jnp.bfloat16

D_MODEL = 1024
POOL_W = 256
ATTN_W = 768
IN_W = 2560
D_FF = 2816
POOL_WINDOWS = (2, 4, 8, 16)
POOL_HALO = 16
DILATIONS = (1, 4, 16)
BLK = 128
LANES = 128
HEAD_DIM = 64
N_GROUPS = ATTN_W // LANES
ROPE_THETA = 10000.0
EPS = 1e-6
NEG = -1e30
N_DEV = 8
SMALL_ROWS = 24

ADAM_LR = 0.001
ADAM_B1 = 0.9
ADAM_B2 = 0.999
ADAM_EPS = 1e-08
ADAM_WD = 0.01
ADAM_STEP = 10

VMEM_LIMIT = 56 * 1024 * 1024


def _dot(a, b):
    return jnp.dot(a, b, preferred_element_type=F32)


def _dot_nt(a, b):
    return lax.dot_general(a, b, (((1,), (1,)), ((), ())), preferred_element_type=F32)


def _dot_tn(a, b):
    return lax.dot_general(a, b, (((0,), (0,)), ((), ())), preferred_element_type=F32)


def _params(n_grid):
    return pltpu.CompilerParams(dimension_semantics=("arbitrary",) * n_grid, vmem_limit_bytes=VMEM_LIMIT)


def _tok(tm, c):
    return pl.BlockSpec((tm, c), lambda i: (i, 0))


def _res(shape):
    return pl.BlockSpec(shape, lambda i: (0,) * len(shape))


def _rms_fwd(x, g):
    r = lax.rsqrt(jnp.mean(x * x, axis=-1, keepdims=True) + EPS)
    return x * r * g


def _rms_bwd(x, g, dy):
    r = lax.rsqrt(jnp.mean(x * x, axis=-1, keepdims=True) + EPS)
    xh = x * r
    gd = dy * g
    dx = r * (gd - xh * jnp.mean(gd * xh, axis=-1, keepdims=True))
    return dx, jnp.sum(dy * xh, axis=0, keepdims=True)


def _rope(x, c, s, sign):
    lane = lax.broadcasted_iota(jnp.int32, (x.shape[0], LANES), 1)
    first = (lane % HEAD_DIM) < (HEAD_DIM // 2)
    outs = []
    for g in range(x.shape[1] // LANES):
        xg = x[:, g * LANES:(g + 1) * LANES]
        rot = jnp.where(first, pltpu.roll(xg, LANES - HEAD_DIM // 2, 1), pltpu.roll(xg, HEAD_DIM // 2, 1))
        outs.append(xg * c + sign * (rot * s))
    return jnp.concatenate(outs, axis=1)


def _proj_fwd(x, g1, w_in_t, cos, sin, tm):
    T = x.shape[0]

    def body(x_ref, g_ref, w_ref, c_ref, s_ref, h_ref, u_ref, q_ref, k_ref, v_ref):
        h = _rms_fwd(x_ref[...], g_ref[...]).astype(BF16)
        h_ref[...] = h
        proj = _dot_nt(h, w_ref[...])
        c = c_ref[...]
        s = s_ref[...]
        u_ref[...] = proj[:, :POOL_W]
        _store_packed(q_ref, _rope(proj[:, POOL_W:POOL_W + ATTN_W], c, s, 1.0))
        _store_packed(k_ref, _rope(proj[:, POOL_W + ATTN_W:POOL_W + 2 * ATTN_W], c, s, 1.0))
        _store_packed(v_ref, proj[:, POOL_W + 2 * ATTN_W:])

    return pl.pallas_call(
        body, name="proj_fwd", grid=(T // tm,),
        in_specs=[_tok(tm, D_MODEL), _res((1, D_MODEL)), _res((IN_W, D_MODEL)), _tok(tm, LANES), _tok(tm, LANES)],
        out_specs=[_tok(tm, D_MODEL), _tok(tm, POOL_W)] + [_tok_packed(tm, ATTN_W)] * 3,
        out_shape=[jax.ShapeDtypeStruct((T, D_MODEL), BF16), jax.ShapeDtypeStruct((T, POOL_W), F32)]
        + [_packed(T, ATTN_W)] * 3,
        compiler_params=_params(1),
    )(x, g1, w_in_t, cos, sin)


def _pool_window(lane):
    return jnp.where(lane < 64, 2, jnp.where(lane < 128, 4, jnp.where(lane < 192, 8, 16)))


def _pool_select(lane, a2, a4, a8, a16):
    return jnp.where(lane < 64, a2, jnp.where(lane < 128, a4, jnp.where(lane < 192, a8, a16)))


def _pool_delta(cur, prev, i, tm):
    prev = jnp.where(i > 0, prev, 0.0)
    ext = jnp.concatenate([prev, cur], axis=0)
    s2 = ext + pltpu.roll(ext, 1, 0)
    s4 = s2 + pltpu.roll(s2, 2, 0)
    s8 = s4 + pltpu.roll(s4, 4, 0)
    s16 = s8 + pltpu.roll(s8, 8, 0)
    lane = lax.broadcasted_iota(jnp.int32, (tm, POOL_W), 1)
    row = lax.broadcasted_iota(jnp.int32, (tm, POOL_W), 0) + i * tm
    ws = _pool_select(lane, s2[POOL_HALO:], s4[POOL_HALO:], s8[POOL_HALO:], s16[POOL_HALO:])
    cnt = jnp.minimum(row + 1, _pool_window(lane)).astype(F32)
    return ws / cnt - cur


def _pool_fwd(u, wbd, scale, tm):
    T = u.shape[0]
    hb = tm // POOL_HALO

    def body(u_ref, prev_ref, w_ref, sc_ref, o_ref):
        d = _pool_delta(u_ref[...], prev_ref[...], pl.program_id(0), tm)
        o_ref[...] = (_dot(d.astype(BF16), w_ref[...]) * sc_ref[...]).astype(BF16)

    return pl.pallas_call(
        body, name="pool_fwd", grid=(T // tm,),
        in_specs=[_tok(tm, POOL_W), pl.BlockSpec((POOL_HALO, POOL_W), lambda i: (jnp.maximum(i * hb - 1, 0), 0)),
                  _res((POOL_W, POOL_W)), _res((1, POOL_W))],
        out_specs=_tok(tm, POOL_W),
        out_shape=jax.ShapeDtypeStruct((T, POOL_W), BF16),
        compiler_params=_params(1),
    )(u, u, wbd, scale)


def _attn_mask(n):
    qi = lax.broadcasted_iota(jnp.int32, (BLK, 2 * BLK), 0)
    kj = lax.broadcasted_iota(jnp.int32, (BLK, 2 * BLK), 1)
    dist = qi + BLK - kj
    return (dist >= 0) & (dist <= BLK) & ((kj >= BLK) | (n > 0))


def _head_col(tile, lane, h):
    return jnp.sum(jnp.where(lane == h, tile, 0.0), axis=1, keepdims=True)


def _attn_cols(dil):
    return ATTN_W // 2 if dil >= 16 else ATTN_W


def _attn_specs(dil, nb):
    cw = _attn_cols(dil)
    ch = BLK * dil
    wide = lambda f: pl.BlockSpec((cw // LANES, ch // 2, LANES), f)
    full = pl.BlockSpec((cw // LANES, ch, LANES), lambda n, j, r: (j, n, 0))
    cur = lambda n, j, r: (j, n, 0)
    prv = lambda n, j, r: (j, jnp.maximum(n - 1, 0), 0)
    prv_out = lambda n, j, r: (j, (n + nb - 1) % nb, 0)
    heads = pl.BlockSpec((ch, LANES), lambda n, j, r: (n, 0))
    return cw, wide(cur), wide(prv), wide(prv_out), heads, full


HIGH_HALF = 0xFFFF0000


def _pack(x):
    return pltpu.bitcast(x.astype(BF16), F32)


def _unpack(words):
    return pltpu.bitcast(words, BF16)


def _packed(rows, cols):
    return jax.ShapeDtypeStruct((cols // LANES, rows // 2, LANES), F32)


def _tok_packed(tm, cols):
    return pl.BlockSpec((cols // LANES, tm // 2, LANES), lambda i: (0, i, 0))


def _store_packed(ref, x):
    for g in range(x.shape[1] // LANES):
        ref[g] = _pack(x[:, g * LANES:(g + 1) * LANES])


def _load_packed(ref):
    return jnp.concatenate([_unpack(ref[g]) for g in range(ref.shape[0])], axis=1)


def _load_streams(ref, dil, r2, sl):
    if dil == 1:
        return [_unpack(ref[sl])]
    words = lax.bitcast_convert_type(ref.at[sl][pl.ds(r2, BLK, stride=dil // 2), :], jnp.uint32)
    even = lax.bitcast_convert_type(words << 16, F32).astype(BF16)
    odd = lax.bitcast_convert_type(words & jnp.uint32(HIGH_HALF), F32).astype(BF16)
    return [even, odd]


def _load_streams_f32(ref, dil, r2, sl):
    ref = ref if sl is None else ref.at[sl]
    if dil == 1:
        return [ref[...]]
    return [ref[pl.ds(2 * r2 + e, BLK, stride=dil), :] for e in range(2)]


def _store_streams_f32(ref, dil, r2, sl, tiles):
    ref = ref if sl is None else ref.at[sl]
    if dil == 1:
        ref[...] = tiles[0]
    else:
        for e, t in enumerate(tiles):
            ref[pl.ds(2 * r2 + e, BLK, stride=dil), :] = t


def _store_streams(ref, dil, r2, sl, tiles):
    if dil == 1:
        ref[sl] = _pack(tiles[0])
    else:
        even, odd = [lax.bitcast_convert_type(t.astype(BF16).astype(F32), jnp.uint32) for t in tiles]
        words = (odd & jnp.uint32(HIGH_HALF)) | (even >> 16)
        ref.at[sl][pl.ds(r2, BLK, stride=dil // 2), :] = lax.bitcast_convert_type(words, F32)


def _attn_fwd(q, k, v, dil, prev, last):
    T = 2 * q.shape[1]
    nb = T // (BLK * dil)
    first = prev is None
    cw, cur, prv, _, heads, full = _attn_specs(dil, nb)
    ncb = ATTN_W // cw
    heads_per_step = cw // HEAD_DIM
    n_str = min(dil, 2)
    everything = None

    def body(*refs):
        if first:
            q_ref, kc_ref, kp_ref, vc_ref, vp_ref, acc_ref, lse_ref = refs
        else:
            q_ref, kc_ref, kp_ref, vc_ref, vp_ref, acc_in, lse_in, acc_ref, lse_ref = refs
        j = pl.program_id(1)
        r2 = pl.program_id(2)
        valid = _attn_mask(pl.program_id(0))
        lane = lax.broadcasted_iota(jnp.int32, (BLK, LANES), 1)
        lo = lane < HEAD_DIM
        lse_tiles = [jnp.zeros((BLK, LANES), F32) for _ in range(n_str)]
        if not first:
            lse_old_tiles = _load_streams_f32(lse_in, dil, r2, everything)
        for g in range(cw // LANES):
            sl = g
            qs, kcs, kps, vcs, vps = [_load_streams(r, dil, r2, sl) for r in (q_ref, kc_ref, kp_ref, vc_ref, vp_ref)]
            if not first:
                olds = _load_streams_f32(acc_in, dil, r2, sl)
            pairs = []
            for e in range(n_str):
                qg = qs[e] * 0.125
                kcat = jnp.concatenate([kps[e], kcs[e]], axis=0)
                vcat = jnp.concatenate([vps[e], vcs[e]], axis=0)
                pair = None
                for hh in range(2):
                    h = j * heads_per_step + 2 * g + hh
                    hm = lo if hh == 0 else jnp.logical_not(lo)
                    s = _dot_nt(jnp.where(hm, qg, jnp.zeros_like(qg)), kcat)
                    s = jnp.where(valid, s, NEG)
                    m = jnp.max(s, axis=1, keepdims=True)
                    p = jnp.exp(s - m)
                    den = jnp.sum(p, axis=1, keepdims=True)
                    o = _dot(p.astype(BF16), vcat) / den
                    lse = m + jnp.log(den)
                    if not first:
                        lse_old = _head_col(lse_old_tiles[e], lane, h)
                        mx = jnp.maximum(lse_old, lse)
                        new = mx + jnp.log(jnp.exp(lse_old - mx) + jnp.exp(lse - mx))
                        o = olds[e] * jnp.exp(lse_old - new) + o * jnp.exp(lse - new)
                        lse = new
                    pair = o if hh == 0 else jnp.where(lo, pair, o)
                    lse_tiles[e] = jnp.where(lane == h, lse, lse_tiles[e])
                pairs.append(pair)
            (_store_streams if last else _store_streams_f32)(acc_ref, dil, r2, sl, pairs)
        if ncb == 1:
            _store_streams_f32(lse_ref, dil, r2, everything, lse_tiles)
        else:
            @pl.when(j == 0)
            def _():
                _store_streams_f32(lse_ref, dil, r2, everything, lse_tiles)

            @pl.when(j > 0)
            def _():
                before = _load_streams_f32(lse_ref, dil, r2, everything)
                _store_streams_f32(lse_ref, dil, r2, everything, [a + b for a, b in zip(before, lse_tiles)])

    ins = [q, k, k, v, v]
    in_specs = [cur, cur, prv, cur, prv]
    if not first:
        ins += [prev[0], prev[1]]
        in_specs += [full, heads]
    return pl.pallas_call(
        body, name=f"attn_fwd_d{dil}", grid=(nb, ncb, max(dil // 2, 1)),
        in_specs=in_specs, out_specs=[cur if last else full, heads],
        out_shape=[_packed(T, ATTN_W) if last else jax.ShapeDtypeStruct((N_GROUPS, T, LANES), F32),
                   jax.ShapeDtypeStruct((T, LANES), F32)],
        compiler_params=_params(3),
    )(*ins)


def _mix_fwd(pool, attn, x, w_out, g2, g3, tm):
    T = x.shape[0]

    def body(p_ref, a_ref, x_ref, w_ref, g2_ref, g3_ref, cat_ref, mix_ref, x2_ref, h2_ref):
        p = p_ref[...]
        a = _load_packed(a_ref)
        cat_ref[...] = jnp.concatenate([p, a], axis=1)
        mix = _dot(p, w_ref[:POOL_W, :]) + _dot(a, w_ref[POOL_W:, :])
        mix_ref[...] = mix
        x2 = x_ref[...] + _rms_fwd(mix, g2_ref[...])
        x2_ref[...] = x2
        h2_ref[...] = _rms_fwd(x2, g3_ref[...]).astype(BF16)

    return pl.pallas_call(
        body, name="mix_fwd", grid=(T // tm,),
        in_specs=[_tok(tm, POOL_W), _tok_packed(tm, ATTN_W), _tok(tm, D_MODEL), _res((D_MODEL, D_MODEL)),
                  _res((1, D_MODEL)), _res((1, D_MODEL))],
        out_specs=[_tok(tm, D_MODEL)] * 4,
        out_shape=[jax.ShapeDtypeStruct((T, D_MODEL), BF16), jax.ShapeDtypeStruct((T, D_MODEL), F32),
                   jax.ShapeDtypeStruct((T, D_MODEL), F32), jax.ShapeDtypeStruct((T, D_MODEL), BF16)],
        compiler_params=_params(1),
    )(pool, attn, x, w_out, g2, g3)


def _ffn_up(h2, wg_t, wu_t, tm):
    T = h2.shape[0]

    def body(h_ref, wg_ref, wu_ref, g_ref, u_ref, a_ref):
        h = h_ref[...]
        gate = _dot_nt(h, wg_ref[...])
        up = _dot_nt(h, wu_ref[...])
        g_ref[...] = gate.astype(BF16)
        u_ref[...] = up.astype(BF16)
        a_ref[...] = (gate * (1.0 / (1.0 + jnp.exp(-gate))) * up).astype(BF16)

    return pl.pallas_call(
        body, name="ffn_up", grid=(T // tm,),
        in_specs=[_tok(tm, D_MODEL), _res((D_FF, D_MODEL)), _res((D_FF, D_MODEL))],
        out_specs=[_tok(tm, D_FF)] * 3,
        out_shape=[jax.ShapeDtypeStruct((T, D_FF), BF16)] * 3,
        compiler_params=_params(1),
    )(h2, wg_t, wu_t)


def _ffn_down_loss(act, w_down, x2, g4, tgt, tm):
    T = act.shape[0]

    def body(a_ref, w_ref, x2_ref, g_ref, t_ref, df_ref, dy_ref, dg_ref, loss_ref):
        i = pl.program_id(0)

        @pl.when(i == 0)
        def _():
            dg_ref[...] = jnp.zeros_like(dg_ref)
            loss_ref[...] = jnp.zeros_like(loss_ref)

        f = _dot(a_ref[...], w_ref[...])
        g = g_ref[...]
        err = x2_ref[...] + _rms_fwd(f, g) - t_ref[...]
        loss_ref[...] += 0.5 * jnp.sum(jnp.mean(err * err, axis=-1, keepdims=True), axis=0, keepdims=True)
        dy = err * (1.0 / D_MODEL)
        dy_ref[...] = dy
        df, dg = _rms_bwd(f, g, dy)
        dg_ref[...] += dg
        df_ref[...] = df.astype(BF16)

    return pl.pallas_call(
        body, name="ffn_down_loss", grid=(T // tm,),
        in_specs=[_tok(tm, D_FF), _res((D_FF, D_MODEL)), _tok(tm, D_MODEL), _res((1, D_MODEL)), _tok(tm, D_MODEL)],
        out_specs=[_tok(tm, D_MODEL), _tok(tm, D_MODEL), _res((1, D_MODEL)), _res((1, 1))],
        out_shape=[jax.ShapeDtypeStruct((T, D_MODEL), BF16), jax.ShapeDtypeStruct((T, D_MODEL), F32),
                   jax.ShapeDtypeStruct((1, D_MODEL), F32), jax.ShapeDtypeStruct((1, 1), F32)],
        compiler_params=_params(1),
    )(act, w_down, x2, g4, tgt)


def _ffn_act_bwd(df, w_down, gate, up, tm):
    T = df.shape[0]

    def body(df_ref, w_ref, g_ref, u_ref, dg_ref, du_ref):
        dact = _dot_nt(df_ref[...], w_ref[...])
        g = g_ref[...].astype(F32)
        u = u_ref[...].astype(F32)
        sg = 1.0 / (1.0 + jnp.exp(-g))
        dg_ref[...] = (dact * u * (sg * (1.0 + g * (1.0 - sg)))).astype(BF16)
        du_ref[...] = (dact * (g * sg)).astype(BF16)

    return pl.pallas_call(
        body, name="ffn_act_bwd", grid=(T // tm,),
        in_specs=[_tok(tm, D_MODEL), _res((D_FF, D_MODEL)), _tok(tm, D_FF), _tok(tm, D_FF)],
        out_specs=[_tok(tm, D_FF)] * 2,
        out_shape=[jax.ShapeDtypeStruct((T, D_FF), BF16)] * 2,
        compiler_params=_params(1),
    )(df, w_down, gate, up)


def _ffn_in_bwd(dgate, dup, wg_t, wu_t, x2, mix, dy, g3, g2, tm):
    T = x2.shape[0]

    def body(dg_ref, du_ref, wg_ref, wu_ref, x2_ref, mix_ref, dy_ref, g3_ref, g2_ref,
             dx2_ref, dmix_ref, dg3_ref, dg2_ref):
        @pl.when(pl.program_id(0) == 0)
        def _():
            dg3_ref[...] = jnp.zeros_like(dg3_ref)
            dg2_ref[...] = jnp.zeros_like(dg2_ref)

        dh2 = _dot(dg_ref[...], wg_ref[...]) + _dot(du_ref[...], wu_ref[...])
        dn, dg3 = _rms_bwd(x2_ref[...], g3_ref[...], dh2)
        dx2 = dy_ref[...] + dn
        dx2_ref[...] = dx2
        dg3_ref[...] += dg3
        dmix, dg2 = _rms_bwd(mix_ref[...], g2_ref[...], dx2)
        dg2_ref[...] += dg2
        dmix_ref[...] = dmix.astype(BF16)

    return pl.pallas_call(
        body, name="ffn_in_bwd", grid=(T // tm,),
        in_specs=[_tok(tm, D_FF), _tok(tm, D_FF), _res((D_FF, D_MODEL)), _res((D_FF, D_MODEL)),
                  _tok(tm, D_MODEL), _tok(tm, D_MODEL), _tok(tm, D_MODEL), _res((1, D_MODEL)), _res((1, D_MODEL))],
        out_specs=[_tok(tm, D_MODEL), _tok(tm, D_MODEL), _res((1, D_MODEL)), _res((1, D_MODEL))],
        out_shape=[jax.ShapeDtypeStruct((T, D_MODEL), F32), jax.ShapeDtypeStruct((T, D_MODEL), BF16),
                   jax.ShapeDtypeStruct((1, D_MODEL), F32), jax.ShapeDtypeStruct((1, D_MODEL), F32)],
        compiler_params=_params(1),
    )(dgate, dup, wg_t, wu_t, x2, mix, dy, g3, g2)


def _mix_bwd(dmix, w_out, tm):
    T = dmix.shape[0]

    def body(d_ref, w_ref, dp_ref, da_ref):
        dcat = _dot_nt(d_ref[...], w_ref[...])
        dp_ref[...] = dcat[:, :POOL_W].astype(BF16)
        _store_packed(da_ref, dcat[:, POOL_W:])

    return pl.pallas_call(
        body, name="mix_bwd", grid=(T // tm,),
        in_specs=[_tok(tm, D_MODEL), _res((D_MODEL, D_MODEL))],
        out_specs=[_tok(tm, POOL_W), _tok_packed(tm, ATTN_W)],
        out_shape=[jax.ShapeDtypeStruct((T, POOL_W), BF16), _packed(T, ATTN_W)],
        compiler_params=_params(1),
    )(dmix, w_out)


def _attn_bwd(q, k, v, dout, out, lse, dil):
    T = 2 * q.shape[1]
    nb = T // (BLK * dil)
    cw, cur, prv, prv_out, heads, _ = _attn_specs(dil, nb)
    ncb = ATTN_W // cw
    heads_per_step = cw // HEAD_DIM
    n_str = min(dil, 2)

    def body(q_ref, kc_ref, kp_ref, vc_ref, vp_ref, do_ref, o_ref, lse_ref,
             dq_ref, dkc_ref, dkp_ref, dvc_ref, dvp_ref):
        j = pl.program_id(1)
        r2 = pl.program_id(2)
        valid = _attn_mask(pl.program_id(0))
        lane = lax.broadcasted_iota(jnp.int32, (BLK, LANES), 1)
        lo = lane < HEAD_DIM
        lse_tiles = _load_streams_f32(lse_ref, dil, r2, None)
        for g in range(cw // LANES):
            sl = g
            qs, kcs, kps, vcs, vps, dos, os_ = [
                _load_streams(r, dil, r2, sl) for r in (q_ref, kc_ref, kp_ref, vc_ref, vp_ref, do_ref, o_ref)]
            dqs, dks, dvs = [], [], []
            for e in range(n_str):
                qg = qs[e] * 0.125
                dog = dos[e]
                kcat = jnp.concatenate([kps[e], kcs[e]], axis=0)
                vcat = jnp.concatenate([vps[e], vcs[e]], axis=0)
                prod = dog.astype(F32) * os_[e].astype(F32)
                dk = jnp.zeros((2 * BLK, LANES), F32)
                dv = jnp.zeros((2 * BLK, LANES), F32)
                dq = None
                for hh in range(2):
                    hm = lo if hh == 0 else jnp.logical_not(lo)
                    qa = jnp.where(hm, qg, jnp.zeros_like(qg))
                    doa = jnp.where(hm, dog, jnp.zeros_like(dog))
                    s = jnp.where(valid, _dot_nt(qa, kcat), NEG)
                    p = jnp.exp(s - _head_col(lse_tiles[e], lane, j * heads_per_step + 2 * g + hh))
                    dsum = jnp.sum(jnp.where(hm, prod, 0.0), axis=1, keepdims=True)
                    ds = (p * (_dot_nt(doa, vcat) - dsum)).astype(BF16)
                    dv = dv + _dot_tn(p.astype(BF16), doa)
                    dk = dk + _dot_tn(ds, qa)
                    dqh = _dot(ds, kcat) * 0.125
                    dq = dqh if hh == 0 else jnp.where(lo, dq, dqh)
                dqs.append(dq)
                dks.append(dk)
                dvs.append(dv)
            _store_streams(dq_ref, dil, r2, sl, dqs)
            _store_streams(dkp_ref, dil, r2, sl, [t[:BLK] for t in dks])
            _store_streams(dkc_ref, dil, r2, sl, [t[BLK:] for t in dks])
            _store_streams(dvp_ref, dil, r2, sl, [t[:BLK] for t in dvs])
            _store_streams(dvc_ref, dil, r2, sl, [t[BLK:] for t in dvs])

    return pl.pallas_call(
        body, name=f"attn_bwd_d{dil}", grid=(nb, ncb, max(dil // 2, 1)),
        in_specs=[cur, cur, prv, cur, prv, cur, cur, heads],
        out_specs=[cur, cur, prv_out, cur, prv_out],
        out_shape=[_packed(T, ATTN_W)] * 5,
        compiler_params=_params(3),
    )(q, k, k, v, v, dout, out, lse)


def _pool_bwd(u, dy, wbd, scale, tm):
    T = u.shape[0]
    nt = T // tm
    hb = tm // POOL_HALO

    def body(u_ref, prev_ref, dy_ref, next_ref, w_ref, sc_ref, du_ref, dw_ref, dsc_ref):
        i = pl.program_id(0)

        @pl.when(i == 0)
        def _():
            dw_ref[...] = jnp.zeros_like(dw_ref)
            dsc_ref[...] = jnp.zeros_like(dsc_ref)

        w = w_ref[...]
        sc = sc_ref[...]
        d = _pool_delta(u_ref[...], prev_ref[...], i, tm).astype(BF16)
        dyc = dy_ref[...].astype(F32)
        dsc_ref[...] += jnp.sum(dyc * _dot(d, w), axis=0, keepdims=True)
        nxt = jnp.where(i < nt - 1, next_ref[...].astype(F32), 0.0)
        dypre = (jnp.concatenate([dyc, nxt], axis=0) * sc).astype(BF16)
        dw_ref[...] += _dot_tn(d, dypre[:tm])
        dd = _dot_nt(dypre, w)
        n = tm + POOL_HALO
        lane = lax.broadcasted_iota(jnp.int32, (n, POOL_W), 1)
        row = lax.broadcasted_iota(jnp.int32, (n, POOL_W), 0) + i * tm
        gx = dd / jnp.minimum(row + 1, _pool_window(lane)).astype(F32)
        a2 = gx + pltpu.roll(gx, n - 1, 0)
        a4 = a2 + pltpu.roll(a2, n - 2, 0)
        a8 = a4 + pltpu.roll(a4, n - 4, 0)
        a16 = a8 + pltpu.roll(a8, n - 8, 0)
        fs = _pool_select(lane[:tm], a2[:tm], a4[:tm], a8[:tm], a16[:tm])
        du_ref[...] = (fs - dd[:tm]).astype(BF16)

    return pl.pallas_call(
        body, name="pool_bwd", grid=(nt,),
        in_specs=[_tok(tm, POOL_W), pl.BlockSpec((POOL_HALO, POOL_W), lambda i: (jnp.maximum(i * hb - 1, 0), 0)),
                  _tok(tm, POOL_W), pl.BlockSpec((POOL_HALO, POOL_W), lambda i: (jnp.minimum((i + 1) * hb, nt * hb - 1), 0)),
                  _res((POOL_W, POOL_W)), _res((1, POOL_W))],
        out_specs=[_tok(tm, POOL_W), _res((POOL_W, POOL_W)), _res((1, POOL_W))],
        out_shape=[jax.ShapeDtypeStruct((T, POOL_W), BF16), jax.ShapeDtypeStruct((POOL_W, POOL_W), F32),
                   jax.ShapeDtypeStruct((1, POOL_W), F32)],
        compiler_params=_params(1),
    )(u, u, dy, dy, wbd, scale)


def _dproj_combine(du, dqs, dkcs, dkps, dvcs, dvps, cos, sin, tm):
    T = du.shape[0]
    n_cfg = len(dqs)

    def body(*refs):
        du_ref = refs[0]
        groups = [refs[1 + j * n_cfg:1 + (j + 1) * n_cfg] for j in range(5)]
        c_ref, s_ref, out_ref = refs[1 + 5 * n_cfg:]
        tot = lambda rs: sum(_load_packed(r).astype(F32) for r in rs)
        c = c_ref[...]
        s = s_ref[...]
        dq = _rope(tot(groups[0]), c, s, -1.0)
        dk = _rope(tot(groups[1]) + tot(groups[2]), c, s, -1.0)
        dv = tot(groups[3]) + tot(groups[4])
        out_ref[...] = jnp.concatenate([du_ref[...], dq.astype(BF16), dk.astype(BF16), dv.astype(BF16)], axis=1)

    return pl.pallas_call(
        body, name="dproj_combine", grid=(T // tm,),
        in_specs=[_tok(tm, POOL_W)] + [_tok_packed(tm, ATTN_W)] * (5 * n_cfg) + [_tok(tm, LANES)] * 2,
        out_specs=_tok(tm, IN_W),
        out_shape=jax.ShapeDtypeStruct((T, IN_W), BF16),
        compiler_params=_params(1),
    )(du, *dqs, *dkcs, *dkps, *dvcs, *dvps, cos, sin)


def _proj_bwd(dproj, w_in_t, x, dx2, g1, tm):
    T = x.shape[0]

    def body(d_ref, w_ref, x_ref, r_ref, g_ref, dx_ref, dg_ref):
        @pl.when(pl.program_id(0) == 0)
        def _():
            dg_ref[...] = jnp.zeros_like(dg_ref)

        dn, dg = _rms_bwd(x_ref[...], g_ref[...], _dot(d_ref[...], w_ref[...]))
        dg_ref[...] += dg
        dx_ref[...] = r_ref[...] + dn

    return pl.pallas_call(
        body, name="proj_bwd", grid=(T // tm,),
        in_specs=[_tok(tm, IN_W), _res((IN_W, D_MODEL)), _tok(tm, D_MODEL), _tok(tm, D_MODEL), _res((1, D_MODEL))],
        out_specs=[_tok(tm, D_MODEL), _res((1, D_MODEL))],
        out_shape=[jax.ShapeDtypeStruct((T, D_MODEL), F32), jax.ShapeDtypeStruct((1, D_MODEL), F32)],
        compiler_params=_params(1),
    )(dproj, w_in_t, x, dx2, g1)


def _wgrad(a, b, name, tile_m, tk):
    T, M = a.shape
    N = b.shape[1]
    nk = T // tk

    def body(a_ref, b_ref, o_ref, acc_ref):
        kk = pl.program_id(1)

        @pl.when(kk == 0)
        def _():
            acc_ref[...] = jnp.zeros_like(acc_ref)

        acc_ref[...] += _dot_tn(a_ref[...], b_ref[...])

        @pl.when(kk == nk - 1)
        def _():
            o_ref[...] = acc_ref[...].astype(BF16)

    return pl.pallas_call(
        body, name=name, grid=(M // tile_m, nk),
        in_specs=[pl.BlockSpec((tk, tile_m), lambda j, kk: (kk, j)), pl.BlockSpec((tk, N), lambda j, kk: (kk, 0))],
        out_specs=pl.BlockSpec((tile_m, N), lambda j, kk: (j, 0)),
        out_shape=jax.ShapeDtypeStruct((M, N), BF16),
        scratch_shapes=[pltpu.VMEM((tile_m, N), F32)],
        compiler_params=_params(2),
    )(a, b)


def _exchange(arrs, scatter, name):
    n = len(arrs)
    out_shapes = [jax.ShapeDtypeStruct((N_DEV,) + (a.shape[1:] if sc else a.shape), a.dtype)
                  for a, sc in zip(arrs, scatter)]

    def body(*refs):
        ins, outs = refs[:n], refs[n:2 * n]
        send_sems, recv_sems, loc_sems = refs[2 * n:]
        x, y, c = lax.axis_index("x"), lax.axis_index("y"), lax.axis_index("c")
        me = 4 * x + 2 * y + c
        local, sends, recvs = [], [], []
        for i in range(n):
            own = ins[i].at[me] if scatter[i] else ins[i]
            loc = pltpu.make_async_copy(own, outs[i].at[me], loc_sems.at[i])
            loc.start()
            local.append(loc)
            for kbits in range(1, N_DEV):
                px = 1 - x if kbits & 4 else x
                py = 1 - y if kbits & 2 else y
                pc = 1 - c if kbits & 1 else c
                pid = 4 * px + 2 * py + pc
                src = ins[i].at[pid] if scatter[i] else ins[i]
                cp = pltpu.make_async_remote_copy(
                    src_ref=src, dst_ref=outs[i].at[me],
                    send_sem=send_sems.at[i, kbits - 1], recv_sem=recv_sems.at[i, kbits - 1],
                    device_id=(px, py, pc), device_id_type=pl.DeviceIdType.MESH)
                cp.start()
                sends.append(cp)
                recvs.append(pltpu.make_async_remote_copy(
                    src_ref=src, dst_ref=outs[i].at[pid],
                    send_sem=send_sems.at[i, kbits - 1], recv_sem=recv_sems.at[i, kbits - 1],
                    device_id=(px, py, pc), device_id_type=pl.DeviceIdType.MESH))
        for cp in recvs:
            cp.wait_recv()
        for cp in sends:
            cp.wait_send()
        for cp in local:
            cp.wait()

    hbm = pl.BlockSpec(memory_space=pl.ANY)
    return pl.pallas_call(
        body, name=name, in_specs=[hbm] * n, out_specs=[hbm] * n, out_shape=out_shapes,
        scratch_shapes=[pltpu.SemaphoreType.DMA((n, N_DEV - 1)), pltpu.SemaphoreType.DMA((n, N_DEV - 1)),
                        pltpu.SemaphoreType.DMA((n,))],
    )(*arrs)


def _peers(x, y, c):
    for kbits in range(1, N_DEV):
        px = 1 - x if kbits & 4 else x
        py = 1 - y if kbits & 2 else y
        pc = 1 - c if kbits & 1 else c
        yield kbits - 1, (px, py, pc), 4 * px + 2 * py + pc


def _peer_copies(ins, lands, scatter, send_sems, recv_sems, incoming):
    x, y, c = lax.axis_index("x"), lax.axis_index("y"), lax.axis_index("c")
    me = 4 * x + 2 * y + c
    copies = []
    for i in range(len(ins)):
        for k, peer, pid in _peers(x, y, c):
            slot = i * (N_DEV - 1) + k
            copies.append(pltpu.make_async_remote_copy(
                src_ref=ins[i].at[pid] if scatter[i] else ins[i], dst_ref=lands[i].at[pid if incoming else me],
                send_sem=send_sems.at[slot], recv_sem=recv_sems.at[slot],
                device_id=peer, device_id_type=pl.DeviceIdType.MESH))
    return copies


_HBM = pl.BlockSpec(memory_space=pltpu.HBM)
_SEM = pl.BlockSpec(memory_space=pltpu.SEMAPHORE)
_DATAFLOW = pltpu.SideEffectType.DATAFLOW_SIDE_EFFECTING


def _exchange_start(arrs, scatter, name):
    n = len(arrs)
    lands = [lax.empty((N_DEV,) + (a.shape[1:] if sc else a.shape), a.dtype) for a, sc in zip(arrs, scatter)]

    def body(*refs):
        ins, lz = refs[:n], refs[n:2 * n]
        send_sems, recv_sems = refs[2 * n:2 * n + 2]
        token = refs[-1]
        for cp in _peer_copies(ins, lz, scatter, send_sems, recv_sems, False):
            cp.start()
        token[...] = jnp.zeros_like(token)

    sem_shape = pltpu.SemaphoreType.DMA((n * (N_DEV - 1),))
    outs = pl.pallas_call(
        body, name=name,
        out_shape=(sem_shape, sem_shape, *[pltpu.HBM(a.shape, a.dtype) for a in arrs + lands],
                   jax.ShapeDtypeStruct((8, LANES), F32)),
        in_specs=[_HBM] * (2 * n),
        out_specs=(_SEM, _SEM, *[_HBM] * (2 * n), pl.BlockSpec(memory_space=pltpu.VMEM)),
        input_output_aliases={i: 2 + i for i in range(2 * n)},
        compiler_params=pltpu.CompilerParams(has_side_effects=_DATAFLOW),
    )(*[pltpu.with_memory_space_constraint(a, pltpu.HBM) for a in arrs + lands])
    return outs[0], outs[1], list(outs[2:2 + n]), list(outs[2 + n:2 + 2 * n]), outs[-1]


def _exchange_wait(handle, scatter, after, name):
    send_sems, recv_sems, srcs, lands, _ = handle
    n = len(srcs)

    def body(*refs):
        ins, lz = refs[:n], refs[n:2 * n]
        for cp in _peer_copies(ins, lz, scatter, refs[2 * n], refs[2 * n + 1], False):
            cp.wait_send()
        for cp in _peer_copies(ins, lz, scatter, refs[2 * n], refs[2 * n + 1], True):
            cp.wait_recv()

    outs = pl.pallas_call(
        body, name=name,
        out_shape=[pltpu.HBM(a.shape, a.dtype) for a in srcs + lands],
        in_specs=[_HBM] * (2 * n) + [_SEM, _SEM, pl.BlockSpec(memory_space=pl.ANY)],
        out_specs=[_HBM] * (2 * n),
        input_output_aliases={i: i for i in range(2 * n)},
        compiler_params=pltpu.CompilerParams(has_side_effects=_DATAFLOW),
    )(*srcs, *lands, send_sems, recv_sems, after)
    return list(outs[:n]), list(outs[n:])


def _fill_own(lands, srcs, scatter, name):
    n = len(lands)

    def body(*refs):
        ins, outs, sems = refs[n:2 * n], refs[2 * n:3 * n], refs[3 * n]
        me = 4 * lax.axis_index("x") + 2 * lax.axis_index("y") + lax.axis_index("c")
        copies = [pltpu.make_async_copy(ins[i].at[me] if scatter[i] else ins[i], outs[i].at[me], sems.at[i])
                  for i in range(n)]
        for cp in copies:
            cp.start()
        for cp in copies:
            cp.wait()

    hbm = pl.BlockSpec(memory_space=pl.ANY)
    return pl.pallas_call(
        body, name=name, in_specs=[hbm] * (2 * n), out_specs=[hbm] * n,
        out_shape=[jax.ShapeDtypeStruct(a.shape, a.dtype) for a in lands],
        input_output_aliases={i: i for i in range(n)},
        scratch_shapes=[pltpu.SemaphoreType.DMA((n,))],
    )(*lands, *srcs)


def _slot_sum(parts, name, tr):
    _, R, C = parts.shape

    def body(p_ref, o_ref):
        acc = p_ref[0].astype(F32)
        for s in range(1, N_DEV):
            acc = acc + p_ref[s].astype(F32)
        o_ref[...] = acc

    return pl.pallas_call(
        body, name=name, grid=(R // tr,),
        in_specs=[pl.BlockSpec((N_DEV, tr, C), lambda i: (0, i, 0))],
        out_specs=pl.BlockSpec((tr, C), lambda i: (i, 0)),
        out_shape=jax.ShapeDtypeStruct((R, C), F32),
        compiler_params=_params(1),
    )(parts)


def _adamw(w, g, m, v, name):
    def body(w_ref, g_ref, m_ref, v_ref, d_ref, nm_ref, nv_ref):
        g = g_ref[...]
        nm = ADAM_B1 * m_ref[...] + (1.0 - ADAM_B1) * g
        nv = ADAM_B2 * v_ref[...] + (1.0 - ADAM_B2) * jnp.square(g)
        m_hat = nm / (1.0 - ADAM_B1 ** ADAM_STEP)
        v_hat = nv / (1.0 - ADAM_B2 ** ADAM_STEP)
        d_ref[...] = -ADAM_LR * (m_hat / (jnp.sqrt(v_hat) + ADAM_EPS) + ADAM_WD * w_ref[...])
        nm_ref[...] = nm
        nv_ref[...] = nv

    return pl.pallas_call(
        body, name=name, out_shape=[jax.ShapeDtypeStruct(w.shape, F32)] * 3,
        compiler_params=pltpu.CompilerParams(vmem_limit_bytes=VMEM_LIMIT),
    )(w, g, m, v)


def _rope_tables(T):
    half = HEAD_DIM // 2
    freqs = ROPE_THETA ** (-jnp.arange(half, dtype=F32) * (2.0 / HEAD_DIM))
    ang = jnp.arange(T).astype(F32)[:, None] * freqs[None, :]
    c, s = jnp.cos(ang), jnp.sin(ang)
    return jnp.concatenate([c, c, c, c], axis=1), jnp.concatenate([-s, s, -s, s], axis=1)


def _block_diag(w_pool):
    wbd = jnp.zeros((POOL_W, POOL_W), F32)
    g = POOL_W // len(POOL_WINDOWS)
    for i in range(len(POOL_WINDOWS)):
        wbd = wbd.at[i * g:(i + 1) * g, i * g:(i + 1) * g].set(w_pool[i])
    return wbd


def _pack_small(g1, w_pool, pool_scale, g2, g3, g4, extra):
    pad = lambda a: jnp.pad(a.reshape(1, -1), ((0, 0), (0, D_MODEL - a.size)))
    rows = [g1.reshape(1, -1), g2.reshape(1, -1), g3.reshape(1, -1), g4.reshape(1, -1),
            w_pool.reshape(-1, D_MODEL), pad(pool_scale), pad(extra)]
    buf = jnp.concatenate(rows, axis=0)
    return jnp.pad(buf, ((0, SMALL_ROWS - buf.shape[0]), (0, 0)))


def _unpack_small(buf):
    n_pool = len(POOL_WINDOWS) * (POOL_W // len(POOL_WINDOWS)) ** 2 // D_MODEL
    g = POOL_W // len(POOL_WINDOWS)
    return (buf[0:1], buf[4:4 + n_pool].reshape(1, len(POOL_WINDOWS), g, g), buf[4 + n_pool:5 + n_pool, :POOL_W],
            buf[1:2], buf[2:3], buf[3:4], buf[5 + n_pool])


class _LocalStep:
    def __init__(self, x, tgt, g1, w_pool, pool_scale, g2, g3, g4):
        self.x, self.tgt, self.pool_scale = x, tgt, pool_scale
        self.g1, self.g2, self.g3, self.g4 = g1, g2, g3, g4
        self.cos, self.sin = _rope_tables(x.shape[0])
        self.wbd = _block_diag(w_pool).astype(BF16)

    def mixer_fwd(self, w_in_t, token):
        self.w_in_t = w_in_t
        self.h1, self.u, self.q, self.k, self.v = _proj_fwd(
            self.x, self.g1 + token[0, 0], w_in_t, self.cos, self.sin, 512)
        self.pool = _pool_fwd(self.u, self.wbd, self.pool_scale, 512)
        prev = None
        for j, dil in enumerate(DILATIONS):
            prev = _attn_fwd(self.q, self.k, self.v, dil, prev, j == len(DILATIONS) - 1)
        self.attn, self.lse = prev
        return self.attn

    def ffn_fwd_bwd(self, w_out, wg_t, wu_t, w_down):
        self.w_out, self.wg_t, self.wu_t = w_out, wg_t, wu_t
        self.cat, self.mix, self.x2, h2 = _mix_fwd(self.pool, self.attn, self.x, w_out, self.g2, self.g3, 512)
        gate, up, act = _ffn_up(h2, wg_t, wu_t, 256)
        df, self.dy, self.dg4, self.loss = _ffn_down_loss(act, w_down, self.x2, self.g4, self.tgt, 512)
        self.dgate, self.dup = _ffn_act_bwd(df, w_down, gate, up, 256)
        return (_wgrad(self.dgate, h2, "wgrad_gate", D_FF // 2, 512), _wgrad(self.dup, h2, "wgrad_up", D_FF // 2, 512),
                _wgrad(act, df, "wgrad_down", D_FF // 2, 512))

    def mixer_bwd(self, token):
        self.dx2, dmix, self.dg3, self.dg2 = _ffn_in_bwd(
            self.dgate, self.dup, self.wg_t, self.wu_t, self.x2, self.mix, self.dy, self.g3 + token[0, 0], self.g2, 512)
        dpool, dattn = _mix_bwd(dmix, self.w_out, 512)
        parts = [_attn_bwd(self.q, self.k, self.v, dattn, self.attn, self.lse, dil) for dil in DILATIONS]
        du, dwbd, self.dscale = _pool_bwd(self.u, dpool, self.wbd, self.pool_scale, 512)
        g = POOL_W // len(POOL_WINDOWS)
        self.dw_pool = jnp.stack([dwbd[i * g:(i + 1) * g, i * g:(i + 1) * g] for i in range(len(POOL_WINDOWS))])
        self.dproj = _dproj_combine(du, *[[p[j] for p in parts] for j in range(5)], self.cos, self.sin, 256)
        return _wgrad(self.dproj, self.h1, "wgrad_in", IN_W // 2, 512), _wgrad(self.cat, dmix, "wgrad_out", D_MODEL, 512)

    def input_bwd(self, token):
        grad_x, dg1 = _proj_bwd(self.dproj, self.w_in_t, self.x, self.dx2, self.g1 + token[0, 0], 512)
        return self.loss, grad_x, (dg1, self.dw_pool, self.dscale, self.dg2, self.dg3, self.dg4)


def _local_step(x, tgt, g1, w_pool, pool_scale, g2, g3, g4, w_in_t, w_out, wg_t, wu_t, w_down):
    zero = jnp.zeros((8, LANES), F32)
    step = _LocalStep(x, tgt, g1, w_pool, pool_scale, g2, g3, g4)
    step.mixer_fwd(w_in_t, zero)
    dw_gate, dw_up, dw_down = step.ffn_fwd_bwd(w_out, wg_t, wu_t, w_down)
    dw_in, dw_out = step.mixer_bwd(zero)
    loss, grad_x, small = step.input_bwd(zero)
    return loss, grad_x, small, (dw_in, dw_out, dw_gate, dw_up, dw_down)


def kernel(x, ln_pre_mix, w_in, w_pool, pool_scale, w_out, ln_post_mix, ln_pre_ffn, w_gate, w_up, w_down, ln_post_ffn, loss_target, m_ln_pre_mix, m_w_in, m_w_pool, m_pool_scale, m_w_out, m_ln_post_mix, m_ln_pre_ffn, m_w_gate, m_w_up, m_w_down, m_ln_post_ffn, v_ln_pre_mix, v_w_in, v_w_pool, v_pool_scale, v_w_out, v_ln_post_mix, v_ln_pre_ffn, v_w_gate, v_w_up, v_w_down, v_ln_post_ffn):
    shards = [w_in[0].T.astype(BF16), w_out[0].astype(BF16), w_gate[0].T.astype(BF16),
              w_up[0].T.astype(BF16), w_down[0].astype(BF16)]
    flat = lambda a: a.reshape(-1, D_MODEL)
    blocks = lambda a: a.reshape(N_DEV, -1, D_MODEL)
    step = _LocalStep(x[0], loss_target[0], ln_pre_mix, w_pool[0], pool_scale, ln_post_mix, ln_pre_ffn, ln_post_ffn)

    w_in_t = flat(_exchange(shards[:1], [False], "gather_w_in")[0])
    rest = _exchange_start(shards[1:], [False] * 4, "gather_rest_start")
    attn = step.mixer_fwd(w_in_t, rest[4])
    srcs, lands = _exchange_wait(rest, [False] * 4, attn, "gather_rest_wait")
    w_out_f, wg_t, wu_t, w_down_f = [flat(a) for a in _fill_own(lands, srcs, [False] * 4, "gather_rest_own")]

    ffn = _exchange_start([blocks(a) for a in step.ffn_fwd_bwd(w_out_f, wg_t, wu_t, w_down_f)], [True] * 3,
                          "grads_ffn_start")
    mixer = _exchange_start([blocks(a) for a in step.mixer_bwd(ffn[4])], [True] * 2, "grads_mixer_start")
    loss, grad_x, small = step.input_bwd(mixer[4])
    got = []
    for handle, n_arr, nm in ((mixer, 2, "grads_mixer"), (ffn, 3, "grads_ffn")):
        srcs, lands = _exchange_wait(handle, [True] * n_arr, grad_x, nm + "_wait")
        got += _fill_own(lands, srcs, [True] * n_arr, nm + "_own")
    sums = [_slot_sum(got[i], f"sum_grad_{i}", got[i].shape[1] // 2) for i in range(5)]

    small_buf = _pack_small(small[0], small[1], small[2], small[3], small[4], small[5], loss)
    small_sum = _slot_sum(_exchange([small_buf], [False], "gather_small")[0], "sum_small", SMALL_ROWS)

    g_in, g_out, g_gate, g_up, g_down = sums[0].T, sums[1], sums[2].T, sums[3].T, sums[4]
    upd = [_adamw(w[0], g, m[0], v[0], f"adamw_{nm}") for nm, w, g, m, v in (
        ("in", w_in, g_in, m_w_in, v_w_in), ("out", w_out, g_out, m_w_out, v_w_out),
        ("gate", w_gate, g_gate, m_w_gate, v_w_gate), ("up", w_up, g_up, m_w_up, v_w_up),
        ("down", w_down, g_down, m_w_down, v_w_down))]
    pack = lambda a, b, c, d, e, f: _pack_small(a, b[0], c, d, e, f, jnp.zeros((1,), F32))
    small_upd = _adamw(
        pack(ln_pre_mix, w_pool, pool_scale, ln_post_mix, ln_pre_ffn, ln_post_ffn), small_sum,
        pack(m_ln_pre_mix, m_w_pool, m_pool_scale, m_ln_post_mix, m_ln_pre_ffn, m_ln_post_ffn),
        pack(v_ln_pre_mix, v_w_pool, v_pool_scale, v_ln_post_mix, v_ln_pre_ffn, v_ln_post_ffn), "adamw_small")

    def tree(small6, big5):
        s1, spool, sscale, s2, s3, s4 = small6
        b_in, b_out, b_gate, b_up, b_down = [b[None] for b in big5]
        return [s1, b_in, spool, sscale, b_out, s2, s3, b_gate, b_up, b_down, s4]

    g_small = _unpack_small(small_sum)
    outs = [g_small[6][0], grad_x[None]]
    outs += tree(g_small[:6], [g_in, g_out, g_gate, g_up, g_down])
    for j in range(3):
        outs += tree(_unpack_small(small_upd[j])[:6], [u[j] for u in upd])
    return tuple(outs)
```

```python
import jax
import jax.numpy as jnp
from jax import lax
from jax.experimental import pallas as pl
from jax.experimental.pallas import tpu as pltpu

F32 = jnp.float32
BF16 = jnp.bfloat16

D_MODEL = 1024
POOL_W = 256
ATTN_W = 768
IN_W = 2560
D_FF = 2816
POOL_WINDOWS = (2, 4, 8, 16)
POOL_HALO = 16
DILATIONS = (1, 4, 16)
BLK = 128
LANES = 128
HEAD_DIM = 64
N_GROUPS = ATTN_W // LANES
ROPE_THETA = 10000.0
EPS = 1e-6
NEG = -1e30
N_DEV = 8
SMALL_ROWS = 24

ADAM_LR = 0.001
ADAM_B1 = 0.9
ADAM_B2 = 0.999
ADAM_EPS = 1e-08
ADAM_WD = 0.01
ADAM_STEP = 10

VMEM_LIMIT = 56 * 1024 * 1024


def _dot(a, b):
    return jnp.dot(a, b, preferred_element_type=F32)


def _dot_nt(a, b):
    return lax.dot_general(a, b, (((1,), (1,)), ((), ())), preferred_element_type=F32)


def _dot_tn(a, b):
    return lax.dot_general(a, b, (((0,), (0,)), ((), ())), preferred_element_type=F32)


def _params(n_grid):
    return pltpu.CompilerParams(dimension_semantics=("arbitrary",) * n_grid, vmem_limit_bytes=VMEM_LIMIT)


def _tok(tm, c):
    return pl.BlockSpec((tm, c), lambda i: (i, 0))


def _res(shape):
    return pl.BlockSpec(shape, lambda i: (0,) * len(shape))


def _rms_fwd(x, g):
    r = lax.rsqrt(jnp.mean(x * x, axis=-1, keepdims=True) + EPS)
    return x * r * g


def _rms_bwd(x, g, dy):
    r = lax.rsqrt(jnp.mean(x * x, axis=-1, keepdims=True) + EPS)
    xh = x * r
    gd = dy * g
    dx = r * (gd - xh * jnp.mean(gd * xh, axis=-1, keepdims=True))
    return dx, jnp.sum(dy * xh, axis=0, keepdims=True)


def _rope(x, c, s, sign):
    lane = lax.broadcasted_iota(jnp.int32, (x.shape[0], LANES), 1)
    first = (lane % HEAD_DIM) < (HEAD_DIM // 2)
    outs = []
    for g in range(x.shape[1] // LANES):
        xg = x[:, g * LANES:(g + 1) * LANES]
        rot = jnp.where(first, pltpu.roll(xg, LANES - HEAD_DIM // 2, 1), pltpu.roll(xg, HEAD_DIM // 2, 1))
        outs.append(xg * c + sign * (rot * s))
    return jnp.concatenate(outs, axis=1)


def _proj_fwd(x, g1, w_in_t, cos, sin, tm):
    T = x.shape[0]

    def body(x_ref, g_ref, w_ref, c_ref, s_ref, h_ref, u_ref, q_ref, k_ref, v_ref):
        h = _rms_fwd(x_ref[...], g_ref[...]).astype(BF16)
        h_ref[...] = h
        proj = _dot_nt(h, w_ref[...])
        c = c_ref[...]
        s = s_ref[...]
        u_ref[...] = proj[:, :POOL_W]
        _store_packed(q_ref, _rope(proj[:, POOL_W:POOL_W + ATTN_W], c, s, 1.0))
        _store_packed(k_ref, _rope(proj[:, POOL_W + ATTN_W:POOL_W + 2 * ATTN_W], c, s, 1.0))
        _store_packed(v_ref, proj[:, POOL_W + 2 * ATTN_W:])

    return pl.pallas_call(
        body, name="proj_fwd", grid=(T // tm,),
        in_specs=[_tok(tm, D_MODEL), _res((1, D_MODEL)), _res((IN_W, D_MODEL)), _tok(tm, LANES), _tok(tm, LANES)],
        out_specs=[_tok(tm, D_MODEL), _tok(tm, POOL_W)] + [_tok_packed(tm, ATTN_W)] * 3,
        out_shape=[jax.ShapeDtypeStruct((T, D_MODEL), BF16), jax.ShapeDtypeStruct((T, POOL_W), F32)]
        + [_packed(T, ATTN_W)] * 3,
        compiler_params=_params(1),
    )(x, g1, w_in_t, cos, sin)


def _pool_window(lane):
    return jnp.where(lane < 64, 2, jnp.where(lane < 128, 4, jnp.where(lane < 192, 8, 16)))


def _pool_select(lane, a2, a4, a8, a16):
    return jnp.where(lane < 64, a2, jnp.where(lane < 128, a4, jnp.where(lane < 192, a8, a16)))


def _pool_delta(cur, prev, i, tm):
    prev = jnp.where(i > 0, prev, 0.0)
    ext = jnp.concatenate([prev, cur], axis=0)
    s2 = ext + pltpu.roll(ext, 1, 0)
    s4 = s2 + pltpu.roll(s2, 2, 0)
    s8 = s4 + pltpu.roll(s4, 4, 0)
    s16 = s8 + pltpu.roll(s8, 8, 0)
    lane = lax.broadcasted_iota(jnp.int32, (tm, POOL_W), 1)
    row = lax.broadcasted_iota(jnp.int32, (tm, POOL_W), 0) + i * tm
    ws = _pool_select(lane, s2[POOL_HALO:], s4[POOL_HALO:], s8[POOL_HALO:], s16[POOL_HALO:])
    cnt = jnp.minimum(row + 1, _pool_window(lane)).astype(F32)
    return ws / cnt - cur


def _pool_fwd(u, wbd, scale, tm):
    T = u.shape[0]
    hb = tm // POOL_HALO

    def body(u_ref, prev_ref, w_ref, sc_ref, o_ref):
        d = _pool_delta(u_ref[...], prev_ref[...], pl.program_id(0), tm)
        o_ref[...] = (_dot(d.astype(BF16), w_ref[...]) * sc_ref[...]).astype(BF16)

    return pl.pallas_call(
        body, name="pool_fwd", grid=(T // tm,),
        in_specs=[_tok(tm, POOL_W), pl.BlockSpec((POOL_HALO, POOL_W), lambda i: (jnp.maximum(i * hb - 1, 0), 0)),
                  _res((POOL_W, POOL_W)), _res((1, POOL_W))],
        out_specs=_tok(tm, POOL_W),
        out_shape=jax.ShapeDtypeStruct((T, POOL_W), BF16),
        compiler_params=_params(1),
    )(u, u, wbd, scale)


def _attn_mask(n):
    qi = lax.broadcasted_iota(jnp.int32, (BLK, 2 * BLK), 0)
    kj = lax.broadcasted_iota(jnp.int32, (BLK, 2 * BLK), 1)
    dist = qi + BLK - kj
    return (dist >= 0) & (dist <= BLK) & ((kj >= BLK) | (n > 0))


def _head_col(tile, lane, h):
    return jnp.sum(jnp.where(lane == h, tile, 0.0), axis=1, keepdims=True)


def _attn_cols(dil):
    return ATTN_W // 2 if dil >= 16 else ATTN_W


def _attn_specs(dil, nb):
    cw = _attn_cols(dil)
    ch = BLK * dil
    wide = lambda f: pl.BlockSpec((cw // LANES, ch // 2, LANES), f)
    full = pl.BlockSpec((cw // LANES, ch, LANES), lambda n, j, r: (j, n, 0))
    cur = lambda n, j, r: (j, n, 0)
    prv = lambda n, j, r: (j, jnp.maximum(n - 1, 0), 0)
    prv_out = lambda n, j, r: (j, (n + nb - 1) % nb, 0)
    heads = pl.BlockSpec((ch, LANES), lambda n, j, r: (n, 0))
    return cw, wide(cur), wide(prv), wide(prv_out), heads, full


HIGH_HALF = 0xFFFF0000


def _pack(x):
    return pltpu.bitcast(x.astype(BF16), F32)


def _unpack(words):
    return pltpu.bitcast(words, BF16)


def _packed(rows, cols):
    return jax.ShapeDtypeStruct((cols // LANES, rows // 2, LANES), F32)


def _tok_packed(tm, cols):
    return pl.BlockSpec((cols // LANES, tm // 2, LANES), lambda i: (0, i, 0))


def _store_packed(ref, x):
    for g in range(x.shape[1] // LANES):
        ref[g] = _pack(x[:, g * LANES:(g + 1) * LANES])


def _load_packed(ref):
    return jnp.concatenate([_unpack(ref[g]) for g in range(ref.shape[0])], axis=1)


def _load_streams(ref, dil, r2, sl):
    if dil == 1:
        return [_unpack(ref[sl])]
    words = lax.bitcast_convert_type(ref.at[sl][pl.ds(r2, BLK, stride=dil // 2), :], jnp.uint32)
    even = lax.bitcast_convert_type(words << 16, F32).astype(BF16)
    odd = lax.bitcast_convert_type(words & jnp.uint32(HIGH_HALF), F32).astype(BF16)
    return [even, odd]


def _load_streams_f32(ref, dil, r2, sl):
    ref = ref if sl is None else ref.at[sl]
    if dil == 1:
        return [ref[...]]
    return [ref[pl.ds(2 * r2 + e, BLK, stride=dil), :] for e in range(2)]


def _store_streams_f32(ref, dil, r2, sl, tiles):
    ref = ref if sl is None else ref.at[sl]
    if dil == 1:
        ref[...] = tiles[0]
    else:
        for e, t in enumerate(tiles):
            ref[pl.ds(2 * r2 + e, BLK, stride=dil), :] = t


def _store_streams(ref, dil, r2, sl, tiles):
    if dil == 1:
        ref[sl] = _pack(tiles[0])
    else:
        even, odd = [lax.bitcast_convert_type(t.astype(BF16).astype(F32), jnp.uint32) for t in tiles]
        words = (odd & jnp.uint32(HIGH_HALF)) | (even >> 16)
        ref.at[sl][pl.ds(r2, BLK, stride=dil // 2), :] = lax.bitcast_convert_type(words, F32)


def _attn_fwd(q, k, v, dil, prev, last):
    T = 2 * q.shape[1]
    nb = T // (BLK * dil)
    first = prev is None
    cw, cur, prv, _, heads, full = _attn_specs(dil, nb)
    ncb = ATTN_W // cw
    heads_per_step = cw // HEAD_DIM
    n_str = min(dil, 2)
    everything = None

    def body(*refs):
        if first:
            q_ref, kc_ref, kp_ref, vc_ref, vp_ref, acc_ref, lse_ref = refs
        else:
            q_ref, kc_ref, kp_ref, vc_ref, vp_ref, acc_in, lse_in, acc_ref, lse_ref = refs
        j = pl.program_id(1)
        r2 = pl.program_id(2)
        valid = _attn_mask(pl.program_id(0))
        lane = lax.broadcasted_iota(jnp.int32, (BLK, LANES), 1)
        lo = lane < HEAD_DIM
        lse_tiles = [jnp.zeros((BLK, LANES), F32) for _ in range(n_str)]
        if not first:
            lse_old_tiles = _load_streams_f32(lse_in, dil, r2, everything)
        for g in range(cw // LANES):
            sl = g
            qs, kcs, kps, vcs, vps = [_load_streams(r, dil, r2, sl) for r in (q_ref, kc_ref, kp_ref, vc_ref, vp_ref)]
            if not first:
                olds = _load_streams_f32(acc_in, dil, r2, sl)
            pairs = []
            for e in range(n_str):
                qg = qs[e] * 0.125
                kcat = jnp.concatenate([kps[e], kcs[e]], axis=0)
                vcat = jnp.concatenate([vps[e], vcs[e]], axis=0)
                pair = None
                for hh in range(2):
                    h = j * heads_per_step + 2 * g + hh
                    hm = lo if hh == 0 else jnp.logical_not(lo)
                    s = _dot_nt(jnp.where(hm, qg, jnp.zeros_like(qg)), kcat)
                    s = jnp.where(valid, s, NEG)
                    m = jnp.max(s, axis=1, keepdims=True)
                    p = jnp.exp(s - m)
                    den = jnp.sum(p, axis=1, keepdims=True)
                    o = _dot(p.astype(BF16), vcat) / den
                    lse = m + jnp.log(den)
                    if not first:
                        lse_old = _head_col(lse_old_tiles[e], lane, h)
                        mx = jnp.maximum(lse_old, lse)
                        new = mx + jnp.log(jnp.exp(lse_old - mx) + jnp.exp(lse - mx))
                        o = olds[e] * jnp.exp(lse_old - new) + o * jnp.exp(lse - new)
                        lse = new
                    pair = o if hh == 0 else jnp.where(lo, pair, o)
                    lse_tiles[e] = jnp.where(lane == h, lse, lse_tiles[e])
                pairs.append(pair)
            (_store_streams if last else _store_streams_f32)(acc_ref, dil, r2, sl, pairs)
        if ncb == 1:
            _store_streams_f32(lse_ref, dil, r2, everything, lse_tiles)
        else:
            @pl.when(j == 0)
            def _():
                _store_streams_f32(lse_ref, dil, r2, everything, lse_tiles)

            @pl.when(j > 0)
            def _():
                before = _load_streams_f32(lse_ref, dil, r2, everything)
                _store_streams_f32(lse_ref, dil, r2, everything, [a + b for a, b in zip(before, lse_tiles)])

    ins = [q, k, k, v, v]
    in_specs = [cur, cur, prv, cur, prv]
    if not first:
        ins += [prev[0], prev[1]]
        in_specs += [full, heads]
    return pl.pallas_call(
        body, name=f"attn_fwd_d{dil}", grid=(nb, ncb, max(dil // 2, 1)),
        in_specs=in_specs, out_specs=[cur if last else full, heads],
        out_shape=[_packed(T, ATTN_W) if last else jax.ShapeDtypeStruct((N_GROUPS, T, LANES), F32),
                   jax.ShapeDtypeStruct((T, LANES), F32)],
        compiler_params=_params(3),
    )(*ins)


def _mix_fwd(pool, attn, x, w_out, g2, g3, tm):
    T = x.shape[0]

    def body(p_ref, a_ref, x_ref, w_ref, g2_ref, g3_ref, cat_ref, mix_ref, x2_ref, h2_ref):
        p = p_ref[...]
        a = _load_packed(a_ref)
        cat_ref[...] = jnp.concatenate([p, a], axis=1)
        mix = _dot(p, w_ref[:POOL_W, :]) + _dot(a, w_ref[POOL_W:, :])
        mix_ref[...] = mix
        x2 = x_ref[...] + _rms_fwd(mix, g2_ref[...])
        x2_ref[...] = x2
        h2_ref[...] = _rms_fwd(x2, g3_ref[...]).astype(BF16)

    return pl.pallas_call(
        body, name="mix_fwd", grid=(T // tm,),
        in_specs=[_tok(tm, POOL_W), _tok_packed(tm, ATTN_W), _tok(tm, D_MODEL), _res((D_MODEL, D_MODEL)),
                  _res((1, D_MODEL)), _res((1, D_MODEL))],
        out_specs=[_tok(tm, D_MODEL)] * 4,
        out_shape=[jax.ShapeDtypeStruct((T, D_MODEL), BF16), jax.ShapeDtypeStruct((T, D_MODEL), F32),
                   jax.ShapeDtypeStruct((T, D_MODEL), F32), jax.ShapeDtypeStruct((T, D_MODEL), BF16)],
        compiler_params=_params(1),
    )(pool, attn, x, w_out, g2, g3)


def _ffn_up(h2, wg_t, wu_t, tm):
    T = h2.shape[0]

    def body(h_ref, wg_ref, wu_ref, g_ref, u_ref, a_ref):
        h = h_ref[...]
        gate = _dot_nt(h, wg_ref[...])
        up = _dot_nt(h, wu_ref[...])
        g_ref[...] = gate.astype(BF16)
        u_ref[...] = up.astype(BF16)
        a_ref[...] = (gate * (1.0 / (1.0 + jnp.exp(-gate))) * up).astype(BF16)

    return pl.pallas_call(
        body, name="ffn_up", grid=(T // tm,),
        in_specs=[_tok(tm, D_MODEL), _res((D_FF, D_MODEL)), _res((D_FF, D_MODEL))],
        out_specs=[_tok(tm, D_FF)] * 3,
        out_shape=[jax.ShapeDtypeStruct((T, D_FF), BF16)] * 3,
        compiler_params=_params(1),
    )(h2, wg_t, wu_t)


def _ffn_down_loss(act, w_down, x2, g4, tgt, tm):
    T = act.shape[0]

    def body(a_ref, w_ref, x2_ref, g_ref, t_ref, df_ref, dy_ref, dg_ref, loss_ref):
        i = pl.program_id(0)

        @pl.when(i == 0)
        def _():
            dg_ref[...] = jnp.zeros_like(dg_ref)
            loss_ref[...] = jnp.zeros_like(loss_ref)

        f = _dot(a_ref[...], w_ref[...])
        g = g_ref[...]
        err = x2_ref[...] + _rms_fwd(f, g) - t_ref[...]
        loss_ref[...] += 0.5 * jnp.sum(jnp.mean(err * err, axis=-1, keepdims=True), axis=0, keepdims=True)
        dy = err * (1.0 / D_MODEL)
        dy_ref[...] = dy
        df, dg = _rms_bwd(f, g, dy)
        dg_ref[...] += dg
        df_ref[...] = df.astype(BF16)

    return pl.pallas_call(
        body, name="ffn_down_loss", grid=(T // tm,),
        in_specs=[_tok(tm, D_FF), _res((D_FF, D_MODEL)), _tok(tm, D_MODEL), _res((1, D_MODEL)), _tok(tm, D_MODEL)],
        out_specs=[_tok(tm, D_MODEL), _tok(tm, D_MODEL), _res((1, D_MODEL)), _res((1, 1))],
        out_shape=[jax.ShapeDtypeStruct((T, D_MODEL), BF16), jax.ShapeDtypeStruct((T, D_MODEL), F32),
                   jax.ShapeDtypeStruct((1, D_MODEL), F32), jax.ShapeDtypeStruct((1, 1), F32)],
        compiler_params=_params(1),
    )(act, w_down, x2, g4, tgt)


def _ffn_act_bwd(df, w_down, gate, up, tm):
    T = df.shape[0]

    def body(df_ref, w_ref, g_ref, u_ref, dg_ref, du_ref):
        dact = _dot_nt(df_ref[...], w_ref[...])
        g = g_ref[...].astype(F32)
        u = u_ref[...].astype(F32)
        sg = 1.0 / (1.0 + jnp.exp(-g))
        dg_ref[...] = (dact * u * (sg * (1.0 + g * (1.0 - sg)))).astype(BF16)
        du_ref[...] = (dact * (g * sg)).astype(BF16)

    return pl.pallas_call(
        body, name="ffn_act_bwd", grid=(T // tm,),
        in_specs=[_tok(tm, D_MODEL), _res((D_FF, D_MODEL)), _tok(tm, D_FF), _tok(tm, D_FF)],
        out_specs=[_tok(tm, D_FF)] * 2,
        out_shape=[jax.ShapeDtypeStruct((T, D_FF), BF16)] * 2,
        compiler_params=_params(1),
    )(df, w_down, gate, up)


def _ffn_in_bwd(dgate, dup, wg_t, wu_t, x2, mix, dy, g3, g2, tm):
    T = x2.shape[0]

    def body(dg_ref, du_ref, wg_ref, wu_ref, x2_ref, mix_ref, dy_ref, g3_ref, g2_ref,
             dx2_ref, dmix_ref, dg3_ref, dg2_ref):
        @pl.when(pl.program_id(0) == 0)
        def _():
            dg3_ref[...] = jnp.zeros_like(dg3_ref)
            dg2_ref[...] = jnp.zeros_like(dg2_ref)

        dh2 = _dot(dg_ref[...], wg_ref[...]) + _dot(du_ref[...], wu_ref[...])
        dn, dg3 = _rms_bwd(x2_ref[...], g3_ref[...], dh2)
        dx2 = dy_ref[...] + dn
        dx2_ref[...] = dx2
        dg3_ref[...] += dg3
        dmix, dg2 = _rms_bwd(mix_ref[...], g2_ref[...], dx2)
        dg2_ref[...] += dg2
        dmix_ref[...] = dmix.astype(BF16)

    return pl.pallas_call(
        body, name="ffn_in_bwd", grid=(T // tm,),
        in_specs=[_tok(tm, D_FF), _tok(tm, D_FF), _res((D_FF, D_MODEL)), _res((D_FF, D_MODEL)),
                  _tok(tm, D_MODEL), _tok(tm, D_MODEL), _tok(tm, D_MODEL), _res((1, D_MODEL)), _res((1, D_MODEL))],
        out_specs=[_tok(tm, D_MODEL), _tok(tm, D_MODEL), _res((1, D_MODEL)), _res((1, D_MODEL))],
        out_shape=[jax.ShapeDtypeStruct((T, D_MODEL), F32), jax.ShapeDtypeStruct((T, D_MODEL), BF16),
                   jax.ShapeDtypeStruct((1, D_MODEL), F32), jax.ShapeDtypeStruct((1, D_MODEL), F32)],
        compiler_params=_params(1),
    )(dgate, dup, wg_t, wu_t, x2, mix, dy, g3, g2)


def _mix_bwd(dmix, w_out, tm):
    T = dmix.shape[0]

    def body(d_ref, w_ref, dp_ref, da_ref):
        dcat = _dot_nt(d_ref[...], w_ref[...])
        dp_ref[...] = dcat[:, :POOL_W].astype(BF16)
        _store_packed(da_ref, dcat[:, POOL_W:])

    return pl.pallas_call(
        body, name="mix_bwd", grid=(T // tm,),
        in_specs=[_tok(tm, D_MODEL), _res((D_MODEL, D_MODEL))],
        out_specs=[_tok(tm, POOL_W), _tok_packed(tm, ATTN_W)],
        out_shape=[jax.ShapeDtypeStruct((T, POOL_W), BF16), _packed(T, ATTN_W)],
        compiler_params=_params(1),
    )(dmix, w_out)


def _attn_bwd(q, k, v, dout, out, lse, dil):
    T = 2 * q.shape[1]
    nb = T // (BLK * dil)
    cw, cur, prv, prv_out, heads, _ = _attn_specs(dil, nb)
    ncb = ATTN_W // cw
    heads_per_step = cw // HEAD_DIM
    n_str = min(dil, 2)

    def body(q_ref, kc_ref, kp_ref, vc_ref, vp_ref, do_ref, o_ref, lse_ref,
             dq_ref, dkc_ref, dkp_ref, dvc_ref, dvp_ref):
        j = pl.program_id(1)
        r2 = pl.program_id(2)
        valid = _attn_mask(pl.program_id(0))
        lane = lax.broadcasted_iota(jnp.int32, (BLK, LANES), 1)
        lo = lane < HEAD_DIM
        lse_tiles = _load_streams_f32(lse_ref, dil, r2, None)
        for g in range(cw // LANES):
            sl = g
            qs, kcs, kps, vcs, vps, dos, os_ = [
                _load_streams(r, dil, r2, sl) for r in (q_ref, kc_ref, kp_ref, vc_ref, vp_ref, do_ref, o_ref)]
            dqs, dks, dvs = [], [], []
            for e in range(n_str):
                qg = qs[e] * 0.125
                dog = dos[e]
                kcat = jnp.concatenate([kps[e], kcs[e]], axis=0)
                vcat = jnp.concatenate([vps[e], vcs[e]], axis=0)
                prod = dog.astype(F32) * os_[e].astype(F32)
                dk = jnp.zeros((2 * BLK, LANES), F32)
                dv = jnp.zeros((2 * BLK, LANES), F32)
                dq = None
                for hh in range(2):
                    hm = lo if hh == 0 else jnp.logical_not(lo)
                    qa = jnp.where(hm, qg, jnp.zeros_like(qg))
                    doa = jnp.where(hm, dog, jnp.zeros_like(dog))
                    s = jnp.where(valid, _dot_nt(qa, kcat), NEG)
                    p = jnp.exp(s - _head_col(lse_tiles[e], lane, j * heads_per_step + 2 * g + hh))
                    dsum = jnp.sum(jnp.where(hm, prod, 0.0), axis=1, keepdims=True)
                    ds = (p * (_dot_nt(doa, vcat) - dsum)).astype(BF16)
                    dv = dv + _dot_tn(p.astype(BF16), doa)
                    dk = dk + _dot_tn(ds, qa)
                    dqh = _dot(ds, kcat) * 0.125
                    dq = dqh if hh == 0 else jnp.where(lo, dq, dqh)
                dqs.append(dq)
                dks.append(dk)
                dvs.append(dv)
            _store_streams(dq_ref, dil, r2, sl, dqs)
            _store_streams(dkp_ref, dil, r2, sl, [t[:BLK] for t in dks])
            _store_streams(dkc_ref, dil, r2, sl, [t[BLK:] for t in dks])
            _store_streams(dvp_ref, dil, r2, sl, [t[:BLK] for t in dvs])
            _store_streams(dvc_ref, dil, r2, sl, [t[BLK:] for t in dvs])

    return pl.pallas_call(
        body, name=f"attn_bwd_d{dil}", grid=(nb, ncb, max(dil // 2, 1)),
        in_specs=[cur, cur, prv, cur, prv, cur, cur, heads],
        out_specs=[cur, cur, prv_out, cur, prv_out],
        out_shape=[_packed(T, ATTN_W)] * 5,
        compiler_params=_params(3),
    )(q, k, k, v, v, dout, out, lse)


def _pool_bwd(u, dy, wbd, scale, tm):
    T = u.shape[0]
    nt = T // tm
    hb = tm // POOL_HALO

    def body(u_ref, prev_ref, dy_ref, next_ref, w_ref, sc_ref, du_ref, dw_ref, dsc_ref):
        i = pl.program_id(0)

        @pl.when(i == 0)
        def _():
            dw_ref[...] = jnp.zeros_like(dw_ref)
            dsc_ref[...] = jnp.zeros_like(dsc_ref)

        w = w_ref[...]
        sc = sc_ref[...]
        d = _pool_delta(u_ref[...], prev_ref[...], i, tm).astype(BF16)
        dyc = dy_ref[...].astype(F32)
        dsc_ref[...] += jnp.sum(dyc * _dot(d, w), axis=0, keepdims=True)
        nxt = jnp.where(i < nt - 1, next_ref[...].astype(F32), 0.0)
        dypre = (jnp.concatenate([dyc, nxt], axis=0) * sc).astype(BF16)
        dw_ref[...] += _dot_tn(d, dypre[:tm])
        dd = _dot_nt(dypre, w)
        n = tm + POOL_HALO
        lane = lax.broadcasted_iota(jnp.int32, (n, POOL_W), 1)
        row = lax.broadcasted_iota(jnp.int32, (n, POOL_W), 0) + i * tm
        gx = dd / jnp.minimum(row + 1, _pool_window(lane)).astype(F32)
        a2 = gx + pltpu.roll(gx, n - 1, 0)
        a4 = a2 + pltpu.roll(a2, n - 2, 0)
        a8 = a4 + pltpu.roll(a4, n - 4, 0)
        a16 = a8 + pltpu.roll(a8, n - 8, 0)
        fs = _pool_select(lane[:tm], a2[:tm], a4[:tm], a8[:tm], a16[:tm])
        du_ref[...] = (fs - dd[:tm]).astype(BF16)

    return pl.pallas_call(
        body, name="pool_bwd", grid=(nt,),
        in_specs=[_tok(tm, POOL_W), pl.BlockSpec((POOL_HALO, POOL_W), lambda i: (jnp.maximum(i * hb - 1, 0), 0)),
                  _tok(tm, POOL_W), pl.BlockSpec((POOL_HALO, POOL_W), lambda i: (jnp.minimum((i + 1) * hb, nt * hb - 1), 0)),
                  _res((POOL_W, POOL_W)), _res((1, POOL_W))],
        out_specs=[_tok(tm, POOL_W), _res((POOL_W, POOL_W)), _res((1, POOL_W))],
        out_shape=[jax.ShapeDtypeStruct((T, POOL_W), BF16), jax.ShapeDtypeStruct((POOL_W, POOL_W), F32),
                   jax.ShapeDtypeStruct((1, POOL_W), F32)],
        compiler_params=_params(1),
    )(u, u, dy, dy, wbd, scale)


def _dproj_combine(du, dqs, dkcs, dkps, dvcs, dvps, cos, sin, tm):
    T = du.shape[0]
    n_cfg = len(dqs)

    def body(*refs):
        du_ref = refs[0]
        groups = [refs[1 + j * n_cfg:1 + (j + 1) * n_cfg] for j in range(5)]
        c_ref, s_ref, out_ref = refs[1 + 5 * n_cfg:]
        tot = lambda rs: sum(_load_packed(r).astype(F32) for r in rs)
        c = c_ref[...]
        s = s_ref[...]
        dq = _rope(tot(groups[0]), c, s, -1.0)
        dk = _rope(tot(groups[1]) + tot(groups[2]), c, s, -1.0)
        dv = tot(groups[3]) + tot(groups[4])
        out_ref[...] = jnp.concatenate([du_ref[...], dq.astype(BF16), dk.astype(BF16), dv.astype(BF16)], axis=1)

    return pl.pallas_call(
        body, name="dproj_combine", grid=(T // tm,),
        in_specs=[_tok(tm, POOL_W)] + [_tok_packed(tm, ATTN_W)] * (5 * n_cfg) + [_tok(tm, LANES)] * 2,
        out_specs=_tok(tm, IN_W),
        out_shape=jax.ShapeDtypeStruct((T, IN_W), BF16),
        compiler_params=_params(1),
    )(du, *dqs, *dkcs, *dkps, *dvcs, *dvps, cos, sin)


def _proj_bwd(dproj, w_in_t, x, dx2, g1, tm):
    T = x.shape[0]

    def body(d_ref, w_ref, x_ref, r_ref, g_ref, dx_ref, dg_ref):
        @pl.when(pl.program_id(0) == 0)
        def _():
            dg_ref[...] = jnp.zeros_like(dg_ref)

        dn, dg = _rms_bwd(x_ref[...], g_ref[...], _dot(d_ref[...], w_ref[...]))
        dg_ref[...] += dg
        dx_ref[...] = r_ref[...] + dn

    return pl.pallas_call(
        body, name="proj_bwd", grid=(T // tm,),
        in_specs=[_tok(tm, IN_W), _res((IN_W, D_MODEL)), _tok(tm, D_MODEL), _tok(tm, D_MODEL), _res((1, D_MODEL))],
        out_specs=[_tok(tm, D_MODEL), _res((1, D_MODEL))],
        out_shape=[jax.ShapeDtypeStruct((T, D_MODEL), F32), jax.ShapeDtypeStruct((1, D_MODEL), F32)],
        compiler_params=_params(1),
    )(dproj, w_in_t, x, dx2, g1)


def _wgrad(a, b, name, tile_m, tk):
    T, M = a.shape
    N = b.shape[1]
    nk = T // tk

    def body(a_ref, b_ref, o_ref, acc_ref):
        kk = pl.program_id(1)

        @pl.when(kk == 0)
        def _():
            acc_ref[...] = jnp.zeros_like(acc_ref)

        acc_ref[...] += _dot_tn(a_ref[...], b_ref[...])

        @pl.when(kk == nk - 1)
        def _():
            o_ref[...] = acc_ref[...].astype(BF16)

    return pl.pallas_call(
        body, name=name, grid=(M // tile_m, nk),
        in_specs=[pl.BlockSpec((tk, tile_m), lambda j, kk: (kk, j)), pl.BlockSpec((tk, N), lambda j, kk: (kk, 0))],
        out_specs=pl.BlockSpec((tile_m, N), lambda j, kk: (j, 0)),
        out_shape=jax.ShapeDtypeStruct((M, N), BF16),
        scratch_shapes=[pltpu.VMEM((tile_m, N), F32)],
        compiler_params=_params(2),
    )(a, b)


def _exchange(arrs, scatter, name):
    n = len(arrs)
    out_shapes = [jax.ShapeDtypeStruct((N_DEV,) + (a.shape[1:] if sc else a.shape), a.dtype)
                  for a, sc in zip(arrs, scatter)]

    def body(*refs):
        ins, outs = refs[:n], refs[n:2 * n]
        send_sems, recv_sems, loc_sems = refs[2 * n:]
        x, y, c = lax.axis_index("x"), lax.axis_index("y"), lax.axis_index("c")
        me = 4 * x + 2 * y + c
        local, sends, recvs = [], [], []
        for i in range(n):
            own = ins[i].at[me] if scatter[i] else ins[i]
            loc = pltpu.make_async_copy(own, outs[i].at[me], loc_sems.at[i])
            loc.start()
            local.append(loc)
            for kbits in range(1, N_DEV):
                px = 1 - x if kbits & 4 else x
                py = 1 - y if kbits & 2 else y
                pc = 1 - c if kbits & 1 else c
                pid = 4 * px + 2 * py + pc
                src = ins[i].at[pid] if scatter[i] else ins[i]
                cp = pltpu.make_async_remote_copy(
                    src_ref=src, dst_ref=outs[i].at[me],
                    send_sem=send_sems.at[i, kbits - 1], recv_sem=recv_sems.at[i, kbits - 1],
                    device_id=(px, py, pc), device_id_type=pl.DeviceIdType.MESH)
                cp.start()
                sends.append(cp)
                recvs.append(pltpu.make_async_remote_copy(
                    src_ref=src, dst_ref=outs[i].at[pid],
                    send_sem=send_sems.at[i, kbits - 1], recv_sem=recv_sems.at[i, kbits - 1],
                    device_id=(px, py, pc), device_id_type=pl.DeviceIdType.MESH))
        for cp in recvs:
            cp.wait_recv()
        for cp in sends:
            cp.wait_send()
        for cp in local:
            cp.wait()

    hbm = pl.BlockSpec(memory_space=pl.ANY)
    return pl.pallas_call(
        body, name=name, in_specs=[hbm] * n, out_specs=[hbm] * n, out_shape=out_shapes,
        scratch_shapes=[pltpu.SemaphoreType.DMA((n, N_DEV - 1)), pltpu.SemaphoreType.DMA((n, N_DEV - 1)),
                        pltpu.SemaphoreType.DMA((n,))],
    )(*arrs)


def _peers(x, y, c):
    for kbits in range(1, N_DEV):
        px = 1 - x if kbits & 4 else x
        py = 1 - y if kbits & 2 else y
        pc = 1 - c if kbits & 1 else c
        yield kbits - 1, (px, py, pc), 4 * px + 2 * py + pc


def _peer_copies(ins, lands, scatter, send_sems, recv_sems, incoming):
    x, y, c = lax.axis_index("x"), lax.axis_index("y"), lax.axis_index("c")
    me = 4 * x + 2 * y + c
    copies = []
    for i in range(len(ins)):
        for k, peer, pid in _peers(x, y, c):
            slot = i * (N_DEV - 1) + k
            copies.append(pltpu.make_async_remote_copy(
                src_ref=ins[i].at[pid] if scatter[i] else ins[i], dst_ref=lands[i].at[pid if incoming else me],
                send_sem=send_sems.at[slot], recv_sem=recv_sems.at[slot],
                device_id=peer, device_id_type=pl.DeviceIdType.MESH))
    return copies


_HBM = pl.BlockSpec(memory_space=pltpu.HBM)
_SEM = pl.BlockSpec(memory_space=pltpu.SEMAPHORE)
_DATAFLOW = pltpu.SideEffectType.DATAFLOW_SIDE_EFFECTING


def _exchange_start(arrs, scatter, after, name):
    n = len(arrs)
    lands = [lax.empty((N_DEV,) + (a.shape[1:] if sc else a.shape), a.dtype) for a, sc in zip(arrs, scatter)]

    def body(*refs):
        ins, lz = refs[:n], refs[n:2 * n]
        send_sems, recv_sems = refs[2 * n + 1:2 * n + 3]
        token = refs[-1]
        for cp in _peer_copies(ins, lz, scatter, send_sems, recv_sems, False):
            cp.start()
        token[...] = jnp.zeros_like(token)

    sem_shape = pltpu.SemaphoreType.DMA((n * (N_DEV - 1),))
    outs = pl.pallas_call(
        body, name=name,
        out_shape=(sem_shape, sem_shape, *[pltpu.HBM(a.shape, a.dtype) for a in arrs + lands],
                   jax.ShapeDtypeStruct((8, LANES), F32)),
        in_specs=[_HBM] * (2 * n) + [pl.BlockSpec(memory_space=pl.ANY)],
        out_specs=(_SEM, _SEM, *[_HBM] * (2 * n), pl.BlockSpec(memory_space=pltpu.VMEM)),
        input_output_aliases={i: 2 + i for i in range(2 * n)},
        compiler_params=pltpu.CompilerParams(has_side_effects=_DATAFLOW),
    )(*[pltpu.with_memory_space_constraint(a, pltpu.HBM) for a in arrs + lands], after)
    return outs[0], outs[1], list(outs[2:2 + n]), list(outs[2 + n:2 + 2 * n]), outs[-1]


def _exchange_wait(handle, scatter, after, name):
    send_sems, recv_sems, srcs, lands, _ = handle
    n = len(srcs)

    def body(*refs):
        ins, lz = refs[:n], refs[n:2 * n]
        for cp in _peer_copies(ins, lz, scatter, refs[2 * n], refs[2 * n + 1], False):
            cp.wait_send()
        for cp in _peer_copies(ins, lz, scatter, refs[2 * n], refs[2 * n + 1], True):
            cp.wait_recv()

    outs = pl.pallas_call(
        body, name=name,
        out_shape=[pltpu.HBM(a.shape, a.dtype) for a in srcs + lands],
        in_specs=[_HBM] * (2 * n) + [_SEM, _SEM, pl.BlockSpec(memory_space=pl.ANY)],
        out_specs=[_HBM] * (2 * n),
        input_output_aliases={i: i for i in range(2 * n)},
        compiler_params=pltpu.CompilerParams(has_side_effects=_DATAFLOW),
    )(*srcs, *lands, send_sems, recv_sems, after)
    return list(outs[:n]), list(outs[n:])


def _fill_own(lands, srcs, scatter):
    me = 4 * lax.axis_index("x") + 2 * lax.axis_index("y") + lax.axis_index("c")
    own = [lax.dynamic_index_in_dim(s, me, 0, keepdims=False) if sc else s for s, sc in zip(srcs, scatter)]
    return [lax.dynamic_update_index_in_dim(land, o, me, 0) for land, o in zip(lands, own)]


def _slot_sum(parts, name, tr):
    _, R, C = parts.shape

    def body(p_ref, o_ref):
        acc = p_ref[0].astype(F32)
        for s in range(1, N_DEV):
            acc = acc + p_ref[s].astype(F32)
        o_ref[...] = acc

    return pl.pallas_call(
        body, name=name, grid=(R // tr,),
        in_specs=[pl.BlockSpec((N_DEV, tr, C), lambda i: (0, i, 0))],
        out_specs=pl.BlockSpec((tr, C), lambda i: (i, 0)),
        out_shape=jax.ShapeDtypeStruct((R, C), F32),
        compiler_params=_params(1),
    )(parts)


def _adamw(w, g, m, v, name):
    def body(w_ref, g_ref, m_ref, v_ref, d_ref, nm_ref, nv_ref):
        g = g_ref[...]
        nm = ADAM_B1 * m_ref[...] + (1.0 - ADAM_B1) * g
        nv = ADAM_B2 * v_ref[...] + (1.0 - ADAM_B2) * jnp.square(g)
        m_hat = nm / (1.0 - ADAM_B1 ** ADAM_STEP)
        v_hat = nv / (1.0 - ADAM_B2 ** ADAM_STEP)
        d_ref[...] = -ADAM_LR * (m_hat / (jnp.sqrt(v_hat) + ADAM_EPS) + ADAM_WD * w_ref[...])
        nm_ref[...] = nm
        nv_ref[...] = nv

    return pl.pallas_call(
        body, name=name, out_shape=[jax.ShapeDtypeStruct(w.shape, F32)] * 3,
        compiler_params=pltpu.CompilerParams(vmem_limit_bytes=VMEM_LIMIT),
    )(w, g, m, v)


def _rope_tables(T):
    half = HEAD_DIM // 2
    freqs = ROPE_THETA ** (-jnp.arange(half, dtype=F32) * (2.0 / HEAD_DIM))
    ang = jnp.arange(T).astype(F32)[:, None] * freqs[None, :]
    c, s = jnp.cos(ang), jnp.sin(ang)
    return jnp.concatenate([c, c, c, c], axis=1), jnp.concatenate([-s, s, -s, s], axis=1)


def _block_diag(w_pool):
    wbd = jnp.zeros((POOL_W, POOL_W), F32)
    g = POOL_W // len(POOL_WINDOWS)
    for i in range(len(POOL_WINDOWS)):
        wbd = wbd.at[i * g:(i + 1) * g, i * g:(i + 1) * g].set(w_pool[i])
    return wbd


def _pack_small(g1, w_pool, pool_scale, g2, g3, g4, extra):
    pad = lambda a: jnp.pad(a.reshape(1, -1), ((0, 0), (0, D_MODEL - a.size)))
    rows = [g1.reshape(1, -1), g2.reshape(1, -1), g3.reshape(1, -1), g4.reshape(1, -1),
            w_pool.reshape(-1, D_MODEL), pad(pool_scale), pad(extra)]
    buf = jnp.concatenate(rows, axis=0)
    return jnp.pad(buf, ((0, SMALL_ROWS - buf.shape[0]), (0, 0)))


def _unpack_small(buf):
    n_pool = len(POOL_WINDOWS) * (POOL_W // len(POOL_WINDOWS)) ** 2 // D_MODEL
    g = POOL_W // len(POOL_WINDOWS)
    return (buf[0:1], buf[4:4 + n_pool].reshape(1, len(POOL_WINDOWS), g, g), buf[4 + n_pool:5 + n_pool, :POOL_W],
            buf[1:2], buf[2:3], buf[3:4], buf[5 + n_pool])


class _LocalStep:
    def __init__(self, x, tgt, g1, w_pool, pool_scale, g2, g3, g4):
        self.x, self.tgt, self.pool_scale = x, tgt, pool_scale
        self.g1, self.g2, self.g3, self.g4 = g1, g2, g3, g4
        self.cos, self.sin = _rope_tables(x.shape[0])
        self.wbd = _block_diag(w_pool).astype(BF16)

    def mixer_fwd(self, w_in_t, token):
        self.w_in_t = w_in_t
        self.h1, self.u, self.q, self.k, self.v = _proj_fwd(
            self.x, self.g1 + token[0, 0], w_in_t, self.cos, self.sin, 512)
        self.pool = _pool_fwd(self.u, self.wbd, self.pool_scale, 512)
        prev = None
        for j, dil in enumerate(DILATIONS):
            prev = _attn_fwd(self.q, self.k, self.v, dil, prev, j == len(DILATIONS) - 1)
        self.attn, self.lse = prev
        return self.attn

    def ffn_fwd_bwd(self, w_out, wg_t, wu_t, w_down):
        self.w_out, self.wg_t, self.wu_t = w_out, wg_t, wu_t
        self.cat, self.mix, self.x2, h2 = _mix_fwd(self.pool, self.attn, self.x, w_out, self.g2, self.g3, 512)
        gate, up, act = _ffn_up(h2, wg_t, wu_t, 256)
        df, self.dy, self.dg4, self.loss = _ffn_down_loss(act, w_down, self.x2, self.g4, self.tgt, 512)
        self.dgate, self.dup = _ffn_act_bwd(df, w_down, gate, up, 256)
        return (_wgrad(self.dgate, h2, "wgrad_gate", D_FF // 2, 512), _wgrad(self.dup, h2, "wgrad_up", D_FF // 2, 512),
                _wgrad(act, df, "wgrad_down", D_FF // 2, 512))

    def mixer_bwd(self, token):
        self.dx2, dmix, self.dg3, self.dg2 = _ffn_in_bwd(
            self.dgate, self.dup, self.wg_t, self.wu_t, self.x2, self.mix, self.dy, self.g3 + token[0, 0], self.g2, 512)
        dpool, dattn = _mix_bwd(dmix, self.w_out, 512)
        parts = [_attn_bwd(self.q, self.k, self.v, dattn, self.attn, self.lse, dil) for dil in DILATIONS]
        du, dwbd, self.dscale = _pool_bwd(self.u, dpool, self.wbd, self.pool_scale, 512)
        g = POOL_W // len(POOL_WINDOWS)
        self.dw_pool = jnp.stack([dwbd[i * g:(i + 1) * g, i * g:(i + 1) * g] for i in range(len(POOL_WINDOWS))])
        self.dproj = _dproj_combine(du, *[[p[j] for p in parts] for j in range(5)], self.cos, self.sin, 256)
        return _wgrad(self.dproj, self.h1, "wgrad_in", IN_W // 2, 512), _wgrad(self.cat, dmix, "wgrad_out", D_MODEL, 512)

    def input_bwd(self, token):
        grad_x, dg1 = _proj_bwd(self.dproj, self.w_in_t, self.x, self.dx2, self.g1 + token[0, 0], 512)
        return self.loss, grad_x, (dg1, self.dw_pool, self.dscale, self.dg2, self.dg3, self.dg4)


def _local_step(x, tgt, g1, w_pool, pool_scale, g2, g3, g4, w_in_t, w_out, wg_t, wu_t, w_down):
    zero = jnp.zeros((8, LANES), F32)
    step = _LocalStep(x, tgt, g1, w_pool, pool_scale, g2, g3, g4)
    step.mixer_fwd(w_in_t, zero)
    dw_gate, dw_up, dw_down = step.ffn_fwd_bwd(w_out, wg_t, wu_t, w_down)
    dw_in, dw_out = step.mixer_bwd(zero)
    loss, grad_x, small = step.input_bwd(zero)
    return loss, grad_x, small, (dw_in, dw_out, dw_gate, dw_up, dw_down)


def kernel(x, ln_pre_mix, w_in, w_pool, pool_scale, w_out, ln_post_mix, ln_pre_ffn, w_gate, w_up, w_down, ln_post_ffn, loss_target, m_ln_pre_mix, m_w_in, m_w_pool, m_pool_scale, m_w_out, m_ln_post_mix, m_ln_pre_ffn, m_w_gate, m_w_up, m_w_down, m_ln_post_ffn, v_ln_pre_mix, v_w_in, v_w_pool, v_pool_scale, v_w_out, v_ln_post_mix, v_ln_pre_ffn, v_w_gate, v_w_up, v_w_down, v_ln_post_ffn):
    shards = [w_in[0].T.astype(BF16), w_out[0].astype(BF16), w_gate[0].T.astype(BF16),
              w_up[0].T.astype(BF16), w_down[0].astype(BF16)]
    flat = lambda a: a.reshape(-1, D_MODEL)
    blocks = lambda a: a.reshape(N_DEV, -1, D_MODEL)
    step = _LocalStep(x[0], loss_target[0], ln_pre_mix, w_pool[0], pool_scale, ln_post_mix, ln_pre_ffn, ln_post_ffn)

    w_in_t = flat(_exchange(shards[:1], [False], "gather_w_in")[0])
    rest = _exchange_start(shards[1:], [False] * 4, w_in_t, "gather_rest_start")
    attn = step.mixer_fwd(w_in_t, rest[4])
    srcs, lands = _exchange_wait(rest, [False] * 4, attn, "gather_rest_wait")
    w_out_f, wg_t, wu_t, w_down_f = [flat(a) for a in _fill_own(lands, srcs, [False] * 4)]

    ffn = _exchange_start([blocks(a) for a in step.ffn_fwd_bwd(w_out_f, wg_t, wu_t, w_down_f)], [True] * 3,
                          step.dgate, "grads_ffn_start")
    mixer = _exchange_start([blocks(a) for a in step.mixer_bwd(ffn[4])], [True] * 2, step.dproj, "grads_mixer_start")
    loss, grad_x, small = step.input_bwd(mixer[4])
    got = []
    for handle, n_arr, nm in ((mixer, 2, "grads_mixer"), (ffn, 3, "grads_ffn")):
        srcs, lands = _exchange_wait(handle, [True] * n_arr, grad_x, nm + "_wait")
        got += _fill_own(lands, srcs, [True] * n_arr)
    sums = [_slot_sum(got[i], f"sum_grad_{i}", got[i].shape[1] // 2) for i in range(5)]

    small_buf = _pack_small(small[0], small[1], small[2], small[3], small[4], small[5], loss)
    small_sum = _slot_sum(_exchange([small_buf], [False], "gather_small")[0], "sum_small", SMALL_ROWS)

    g_in, g_out, g_gate, g_up, g_down = sums[0].T, sums[1], sums[2].T, sums[3].T, sums[4]
    upd = [_adamw(w[0], g, m[0], v[0], f"adamw_{nm}") for nm, w, g, m, v in (
        ("in", w_in, g_in, m_w_in, v_w_in), ("out", w_out, g_out, m_w_out, v_w_out),
        ("gate", w_gate, g_gate, m_w_gate, v_w_gate), ("up", w_up, g_up, m_w_up, v_w_up),
        ("down", w_down, g_down, m_w_down, v_w_down))]
    pack = lambda a, b, c, d, e, f: _pack_small(a, b[0], c, d, e, f, jnp.zeros((1,), F32))
    small_upd = _adamw(
        pack(ln_pre_mix, w_pool, pool_scale, ln_post_mix, ln_pre_ffn, ln_post_ffn), small_sum,
        pack(m_ln_pre_mix, m_w_pool, m_pool_scale, m_ln_post_mix, m_ln_pre_ffn, m_ln_post_ffn),
        pack(v_ln_pre_mix, v_w_pool, v_pool_scale, v_ln_post_mix, v_ln_pre_ffn, v_ln_post_ffn), "adamw_small")

    def tree(small6, big5):
        s1, spool, sscale, s2, s3, s4 = small6
        b_in, b_out, b_gate, b_up, b_down = [b[None] for b in big5]
        return [s1, b_in, spool, sscale, b_out, s2, s3, b_gate, b_up, b_down, s4]

    g_small = _unpack_small(small_sum)
    outs = [g_small[6][0], grad_x[None]]
    outs += tree(g_small[:6], [g_in, g_out, g_gate, g_up, g_down])
    for j in range(3):
        outs += tree(_unpack_small(small_upd[j])[:6], [u[j] for u in upd])
    return tuple(outs)
```

```python
import jax
import jax.numpy as jnp
from jax import lax
from jax.experimental import pallas as pl
from jax.experimental.pallas import tpu as pltpu

F32 = jnp.float32
BF16 = jnp.bfloat16

D_MODEL = 1024
POOL_W = 256
ATTN_W = 768
IN_W = 2560
D_FF = 2816
POOL_WINDOWS = (2, 4, 8, 16)
POOL_HALO = 16
DILATIONS = (1, 4, 16)
BLK = 128
LANES = 128
HEAD_DIM = 64
N_GROUPS = ATTN_W // LANES
ROPE_THETA = 10000.0
EPS = 1e-6
NEG = -1e30
N_DEV = 8
SMALL_ROWS = 24

ADAM_LR = 0.001
ADAM_B1 = 0.9
ADAM_B2 = 0.999
ADAM_EPS = 1e-08
ADAM_WD = 0.01
ADAM_STEP = 10

VMEM_LIMIT = 56 * 1024 * 1024


def _dot(a, b):
    return jnp.dot(a, b, preferred_element_type=F32)


def _dot_nt(a, b):
    return lax.dot_general(a, b, (((1,), (1,)), ((), ())), preferred_element_type=F32)


def _dot_tn(a, b):
    return lax.dot_general(a, b, (((0,), (0,)), ((), ())), preferred_element_type=F32)


def _params(n_grid):
    return pltpu.CompilerParams(dimension_semantics=("arbitrary",) * n_grid, vmem_limit_bytes=VMEM_LIMIT)


def _tok(tm, c):
    return pl.BlockSpec((tm, c), lambda i: (i, 0))


def _res(shape):
    return pl.BlockSpec(shape, lambda i: (0,) * len(shape))


def _rms_fwd(x, g):
    r = lax.rsqrt(jnp.mean(x * x, axis=-1, keepdims=True) + EPS)
    return x * r * g


def _rms_bwd(x, g, dy):
    r = lax.rsqrt(jnp.mean(x * x, axis=-1, keepdims=True) + EPS)
    xh = x * r
    gd = dy * g
    dx = r * (gd - xh * jnp.mean(gd * xh, axis=-1, keepdims=True))
    return dx, jnp.sum(dy * xh, axis=0, keepdims=True)


def _rope(x, c, s, sign):
    lane = lax.broadcasted_iota(jnp.int32, (x.shape[0], LANES), 1)
    first = (lane % HEAD_DIM) < (HEAD_DIM // 2)
    outs = []
    for g in range(x.shape[1] // LANES):
        xg = x[:, g * LANES:(g + 1) * LANES]
        rot = jnp.where(first, pltpu.roll(xg, LANES - HEAD_DIM // 2, 1), pltpu.roll(xg, HEAD_DIM // 2, 1))
        outs.append(xg * c + sign * (rot * s))
    return jnp.concatenate(outs, axis=1)


def _proj_fwd(x, g1, w_in_t, cos, sin, tm):
    T = x.shape[0]

    def body(x_ref, g_ref, w_ref, c_ref, s_ref, h_ref, u_ref, q_ref, k_ref, v_ref):
        h = _rms_fwd(x_ref[...], g_ref[...]).astype(BF16)
        h_ref[...] = h
        proj = _dot_nt(h, w_ref[...])
        c = c_ref[...]
        s = s_ref[...]
        u_ref[...] = proj[:, :POOL_W]
        _store_packed(q_ref, _rope(proj[:, POOL_W:POOL_W + ATTN_W], c, s, 1.0))
        _store_packed(k_ref, _rope(proj[:, POOL_W + ATTN_W:POOL_W + 2 * ATTN_W], c, s, 1.0))
        _store_packed(v_ref, proj[:, POOL_W + 2 * ATTN_W:])

    return pl.pallas_call(
        body, name="proj_fwd", grid=(T // tm,),
        in_specs=[_tok(tm, D_MODEL), _res((1, D_MODEL)), _res((IN_W, D_MODEL)), _tok(tm, LANES), _tok(tm, LANES)],
        out_specs=[_tok(tm, D_MODEL), _tok(tm, POOL_W)] + [_tok_packed(tm, ATTN_W)] * 3,
        out_shape=[jax.ShapeDtypeStruct((T, D_MODEL), BF16), jax.ShapeDtypeStruct((T, POOL_W), F32)]
        + [_packed(T, ATTN_W)] * 3,
        compiler_params=_params(1),
    )(x, g1, w_in_t, cos, sin)


def _pool_window(lane):
    return jnp.where(lane < 64, 2, jnp.where(lane < 128, 4, jnp.where(lane < 192, 8, 16)))


def _pool_select(lane, a2, a4, a8, a16):
    return jnp.where(lane < 64, a2, jnp.where(lane < 128, a4, jnp.where(lane < 192, a8, a16)))


def _pool_delta(cur, prev, i, tm):
    prev = jnp.where(i > 0, prev, 0.0)
    ext = jnp.concatenate([prev, cur], axis=0)
    s2 = ext + pltpu.roll(ext, 1, 0)
    s4 = s2 + pltpu.roll(s2, 2, 0)
    s8 = s4 + pltpu.roll(s4, 4, 0)
    s16 = s8 + pltpu.roll(s8, 8, 0)
    lane = lax.broadcasted_iota(jnp.int32, (tm, POOL_W), 1)
    row = lax.broadcasted_iota(jnp.int32, (tm, POOL_W), 0) + i * tm
    ws = _pool_select(lane, s2[POOL_HALO:], s4[POOL_HALO:], s8[POOL_HALO:], s16[POOL_HALO:])
    cnt = jnp.minimum(row + 1, _pool_window(lane)).astype(F32)
    return ws / cnt - cur


def _pool_fwd(u, wbd, scale, tm):
    T = u.shape[0]
    hb = tm // POOL_HALO

    def body(u_ref, prev_ref, w_ref, sc_ref, o_ref):
        d = _pool_delta(u_ref[...], prev_ref[...], pl.program_id(0), tm)
        o_ref[...] = (_dot(d.astype(BF16), w_ref[...]) * sc_ref[...]).astype(BF16)

    return pl.pallas_call(
        body, name="pool_fwd", grid=(T // tm,),
        in_specs=[_tok(tm, POOL_W), pl.BlockSpec((POOL_HALO, POOL_W), lambda i: (jnp.maximum(i * hb - 1, 0), 0)),
                  _res((POOL_W, POOL_W)), _res((1, POOL_W))],
        out_specs=_tok(tm, POOL_W),
        out_shape=jax.ShapeDtypeStruct((T, POOL_W), BF16),
        compiler_params=_params(1),
    )(u, u, wbd, scale)


def _attn_mask(n):
    qi = lax.broadcasted_iota(jnp.int32, (BLK, 2 * BLK), 0)
    kj = lax.broadcasted_iota(jnp.int32, (BLK, 2 * BLK), 1)
    dist = qi + BLK - kj
    return (dist >= 0) & (dist <= BLK) & ((kj >= BLK) | (n > 0))


def _head_col(tile, lane, h):
    return jnp.sum(jnp.where(lane == h, tile, 0.0), axis=1, keepdims=True)


def _attn_cols(dil):
    return ATTN_W // 2 if dil >= 16 else ATTN_W


def _attn_specs(dil, nb):
    cw = _attn_cols(dil)
    ch = BLK * dil
    wide = lambda f: pl.BlockSpec((cw // LANES, ch // 2, LANES), f)
    full = pl.BlockSpec((cw // LANES, ch, LANES), lambda n, j, r: (j, n, 0))
    cur = lambda n, j, r: (j, n, 0)
    prv = lambda n, j, r: (j, jnp.maximum(n - 1, 0), 0)
    prv_out = lambda n, j, r: (j, (n + nb - 1) % nb, 0)
    heads = pl.BlockSpec((ch, LANES), lambda n, j, r: (n, 0))
    return cw, wide(cur), wide(prv), wide(prv_out), heads, full


HIGH_HALF = 0xFFFF0000


def _pack(x):
    return pltpu.bitcast(x.astype(BF16), F32)


def _unpack(words):
    return pltpu.bitcast(words, BF16)


def _packed(rows, cols):
    return jax.ShapeDtypeStruct((cols // LANES, rows // 2, LANES), F32)


def _tok_packed(tm, cols):
    return pl.BlockSpec((cols // LANES, tm // 2, LANES), lambda i: (0, i, 0))


def _store_packed(ref, x):
    for g in range(x.shape[1] // LANES):
        ref[g] = _pack(x[:, g * LANES:(g + 1) * LANES])


def _load_packed(ref):
    return jnp.concatenate([_unpack(ref[g]) for g in range(ref.shape[0])], axis=1)


def _load_streams(ref, dil, r2, sl):
    if dil == 1:
        return [_unpack(ref[sl])]
    words = lax.bitcast_convert_type(ref.at[sl][pl.ds(r2, BLK, stride=dil // 2), :], jnp.uint32)
    even = lax.bitcast_convert_type(words << 16, F32).astype(BF16)
    odd = lax.bitcast_convert_type(words & jnp.uint32(HIGH_HALF), F32).astype(BF16)
    return [even, odd]


def _load_streams_f32(ref, dil, r2, sl):
    ref = ref if sl is None else ref.at[sl]
    if dil == 1:
        return [ref[...]]
    return [ref[pl.ds(2 * r2 + e, BLK, stride=dil), :] for e in range(2)]


def _store_streams_f32(ref, dil, r2, sl, tiles):
    ref = ref if sl is None else ref.at[sl]
    if dil == 1:
        ref[...] = tiles[0]
    else:
        for e, t in enumerate(tiles):
            ref[pl.ds(2 * r2 + e, BLK, stride=dil), :] = t


def _store_streams(ref, dil, r2, sl, tiles):
    if dil == 1:
        ref[sl] = _pack(tiles[0])
    else:
        even, odd = [lax.bitcast_convert_type(t.astype(BF16).astype(F32), jnp.uint32) for t in tiles]
        words = (odd & jnp.uint32(HIGH_HALF)) | (even >> 16)
        ref.at[sl][pl.ds(r2, BLK, stride=dil // 2), :] = lax.bitcast_convert_type(words, F32)


def _attn_fwd(q, k, v, dil, prev, last):
    T = 2 * q.shape[1]
    nb = T // (BLK * dil)
    first = prev is None
    cw, cur, prv, _, heads, full = _attn_specs(dil, nb)
    ncb = ATTN_W // cw
    heads_per_step = cw // HEAD_DIM
    n_str = min(dil, 2)
    everything = None

    def body(*refs):
        if first:
            q_ref, kc_ref, kp_ref, vc_ref, vp_ref, acc_ref, lse_ref = refs
        else:
            q_ref, kc_ref, kp_ref, vc_ref, vp_ref, acc_in, lse_in, acc_ref, lse_ref = refs
        j = pl.program_id(1)
        r2 = pl.program_id(2)
        valid = _attn_mask(pl.program_id(0))
        lane = lax.broadcasted_iota(jnp.int32, (BLK, LANES), 1)
        lo = lane < HEAD_DIM
        lse_tiles = [jnp.zeros((BLK, LANES), F32) for _ in range(n_str)]
        if not first:
            lse_old_tiles = _load_streams_f32(lse_in, dil, r2, everything)
        for g in range(cw // LANES):
            sl = g
            qs, kcs, kps, vcs, vps = [_load_streams(r, dil, r2, sl) for r in (q_ref, kc_ref, kp_ref, vc_ref, vp_ref)]
            if not first:
                olds = _load_streams_f32(acc_in, dil, r2, sl)
            pairs = []
            for e in range(n_str):
                qg = qs[e] * 0.125
                kcat = jnp.concatenate([kps[e], kcs[e]], axis=0)
                vcat = jnp.concatenate([vps[e], vcs[e]], axis=0)
                pair = None
                for hh in range(2):
                    h = j * heads_per_step + 2 * g + hh
                    hm = lo if hh == 0 else jnp.logical_not(lo)
                    s = _dot_nt(jnp.where(hm, qg, jnp.zeros_like(qg)), kcat)
                    s = jnp.where(valid, s, NEG)
                    m = jnp.max(s, axis=1, keepdims=True)
                    p = jnp.exp(s - m)
                    den = jnp.sum(p, axis=1, keepdims=True)
                    o = _dot(p.astype(BF16), vcat) / den
                    lse = m + jnp.log(den)
                    if not first:
                        lse_old = _head_col(lse_old_tiles[e], lane, h)
                        mx = jnp.maximum(lse_old, lse)
                        new = mx + jnp.log(jnp.exp(lse_old - mx) + jnp.exp(lse - mx))
                        o = olds[e] * jnp.exp(lse_old - new) + o * jnp.exp(lse - new)
                        lse = new
                    pair = o if hh == 0 else jnp.where(lo, pair, o)
                    lse_tiles[e] = jnp.where(lane == h, lse, lse_tiles[e])
                pairs.append(pair)
            (_store_streams if last else _store_streams_f32)(acc_ref, dil, r2, sl, pairs)
        if ncb == 1:
            _store_streams_f32(lse_ref, dil, r2, everything, lse_tiles)
        else:
            @pl.when(j == 0)
            def _():
                _store_streams_f32(lse_ref, dil, r2, everything, lse_tiles)

            @pl.when(j > 0)
            def _():
                before = _load_streams_f32(lse_ref, dil, r2, everything)
                _store_streams_f32(lse_ref, dil, r2, everything, [a + b for a, b in zip(before, lse_tiles)])

    ins = [q, k, k, v, v]
    in_specs = [cur, cur, prv, cur, prv]
    if not first:
        ins += [prev[0], prev[1]]
        in_specs += [full, heads]
    return pl.pallas_call(
        body, name=f"attn_fwd_d{dil}", grid=(nb, ncb, max(dil // 2, 1)),
        in_specs=in_specs, out_specs=[cur if last else full, heads],
        out_shape=[_packed(T, ATTN_W) if last else jax.ShapeDtypeStruct((N_GROUPS, T, LANES), F32),
                   jax.ShapeDtypeStruct((T, LANES), F32)],
        compiler_params=_params(3),
    )(*ins)


def _mix_fwd(pool, attn, x, w_out, g2, g3, tm):
    T = x.shape[0]

    def body(p_ref, a_ref, x_ref, w_ref, g2_ref, g3_ref, cat_ref, mix_ref, x2_ref, h2_ref):
        p = p_ref[...]
        a = _load_packed(a_ref)
        cat_ref[...] = jnp.concatenate([p, a], axis=1)
        mix = _dot(p, w_ref[:POOL_W, :]) + _dot(a, w_ref[POOL_W:, :])
        mix_ref[...] = mix
        x2 = x_ref[...] + _rms_fwd(mix, g2_ref[...])
        x2_ref[...] = x2
        h2_ref[...] = _rms_fwd(x2, g3_ref[...]).astype(BF16)

    return pl.pallas_call(
        body, name="mix_fwd", grid=(T // tm,),
        in_specs=[_tok(tm, POOL_W), _tok_packed(tm, ATTN_W), _tok(tm, D_MODEL), _res((D_MODEL, D_MODEL)),
                  _res((1, D_MODEL)), _res((1, D_MODEL))],
        out_specs=[_tok(tm, D_MODEL)] * 4,
        out_shape=[jax.ShapeDtypeStruct((T, D_MODEL), BF16), jax.ShapeDtypeStruct((T, D_MODEL), F32),
                   jax.ShapeDtypeStruct((T, D_MODEL), F32), jax.ShapeDtypeStruct((T, D_MODEL), BF16)],
        compiler_params=_params(1),
    )(pool, attn, x, w_out, g2, g3)


def _ffn_up(h2, wg_t, wu_t, tm):
    T = h2.shape[0]

    def body(h_ref, wg_ref, wu_ref, dg_ref, du_ref, a_ref):
        h = h_ref[...]
        gate = _dot_nt(h, wg_ref[...])
        up = _dot_nt(h, wu_ref[...])
        sg = 1.0 / (1.0 + jnp.exp(-gate))
        silu = gate * sg
        a_ref[...] = (silu * up).astype(BF16)
        dg_ref[...] = (up * (sg * (1.0 + gate * (1.0 - sg)))).astype(BF16)
        du_ref[...] = silu.astype(BF16)

    return pl.pallas_call(
        body, name="ffn_up", grid=(T // tm,),
        in_specs=[_tok(tm, D_MODEL), _res((D_FF, D_MODEL)), _res((D_FF, D_MODEL))],
        out_specs=[_tok(tm, D_FF)] * 3,
        out_shape=[jax.ShapeDtypeStruct((T, D_FF), BF16)] * 3,
        compiler_params=_params(1),
    )(h2, wg_t, wu_t)


def _ffn_down_loss(act, w_down, x2, g4, tgt, tm):
    T = act.shape[0]

    def body(a_ref, w_ref, x2_ref, g_ref, t_ref, df_ref, dy_ref, dg_ref, loss_ref):
        i = pl.program_id(0)

        @pl.when(i == 0)
        def _():
            dg_ref[...] = jnp.zeros_like(dg_ref)
            loss_ref[...] = jnp.zeros_like(loss_ref)

        f = _dot(a_ref[...], w_ref[...])
        g = g_ref[...]
        err = x2_ref[...] + _rms_fwd(f, g) - t_ref[...]
        loss_ref[...] += 0.5 * jnp.sum(jnp.mean(err * err, axis=-1, keepdims=True), axis=0, keepdims=True)
        dy = err * (1.0 / D_MODEL)
        dy_ref[...] = dy
        df, dg = _rms_bwd(f, g, dy)
        dg_ref[...] += dg
        df_ref[...] = df.astype(BF16)

    return pl.pallas_call(
        body, name="ffn_down_loss", grid=(T // tm,),
        in_specs=[_tok(tm, D_FF), _res((D_FF, D_MODEL)), _tok(tm, D_MODEL), _res((1, D_MODEL)), _tok(tm, D_MODEL)],
        out_specs=[_tok(tm, D_MODEL), _tok(tm, D_MODEL), _res((1, D_MODEL)), _res((1, 1))],
        out_shape=[jax.ShapeDtypeStruct((T, D_MODEL), BF16), jax.ShapeDtypeStruct((T, D_MODEL), F32),
                   jax.ShapeDtypeStruct((1, D_MODEL), F32), jax.ShapeDtypeStruct((1, 1), F32)],
        compiler_params=_params(1),
    )(act, w_down, x2, g4, tgt)


def _ffn_act_bwd(df, w_down, act_dgate, act_dup, tm):
    T = df.shape[0]

    def body(df_ref, w_ref, ag_ref, au_ref, dg_ref, du_ref):
        dact = _dot_nt(df_ref[...], w_ref[...])
        dg_ref[...] = (dact * ag_ref[...].astype(F32)).astype(BF16)
        du_ref[...] = (dact * au_ref[...].astype(F32)).astype(BF16)

    return pl.pallas_call(
        body, name="ffn_act_bwd", grid=(T // tm,),
        in_specs=[_tok(tm, D_MODEL), _res((D_FF, D_MODEL)), _tok(tm, D_FF), _tok(tm, D_FF)],
        out_specs=[_tok(tm, D_FF)] * 2,
        out_shape=[jax.ShapeDtypeStruct((T, D_FF), BF16)] * 2,
        compiler_params=_params(1),
    )(df, w_down, act_dgate, act_dup)


def _ffn_in_bwd(dgate, dup, wg_t, wu_t, x2, mix, dy, g3, g2, tm):
    T = x2.shape[0]

    def body(dg_ref, du_ref, wg_ref, wu_ref, x2_ref, mix_ref, dy_ref, g3_ref, g2_ref,
             dx2_ref, dmix_ref, dg3_ref, dg2_ref):
        @pl.when(pl.program_id(0) == 0)
        def _():
            dg3_ref[...] = jnp.zeros_like(dg3_ref)
            dg2_ref[...] = jnp.zeros_like(dg2_ref)

        dh2 = _dot(dg_ref[...], wg_ref[...]) + _dot(du_ref[...], wu_ref[...])
        dn, dg3 = _rms_bwd(x2_ref[...], g3_ref[...], dh2)
        dx2 = dy_ref[...] + dn
        dx2_ref[...] = dx2
        dg3_ref[...] += dg3
        dmix, dg2 = _rms_bwd(mix_ref[...], g2_ref[...], dx2)
        dg2_ref[...] += dg2
        dmix_ref[...] = dmix.astype(BF16)

    return pl.pallas_call(
        body, name="ffn_in_bwd", grid=(T // tm,),
        in_specs=[_tok(tm, D_FF), _tok(tm, D_FF), _res((D_FF, D_MODEL)), _res((D_FF, D_MODEL)),
                  _tok(tm, D_MODEL), _tok(tm, D_MODEL), _tok(tm, D_MODEL), _res((1, D_MODEL)), _res((1, D_MODEL))],
        out_specs=[_tok(tm, D_MODEL), _tok(tm, D_MODEL), _res((1, D_MODEL)), _res((1, D_MODEL))],
        out_shape=[jax.ShapeDtypeStruct((T, D_MODEL), F32), jax.ShapeDtypeStruct((T, D_MODEL), BF16),
                   jax.ShapeDtypeStruct((1, D_MODEL), F32), jax.ShapeDtypeStruct((1, D_MODEL), F32)],
        compiler_params=_params(1),
    )(dgate, dup, wg_t, wu_t, x2, mix, dy, g3, g2)


def _mix_bwd(dmix, w_out, tm):
    T = dmix.shape[0]

    def body(d_ref, w_ref, dp_ref, da_ref):
        dcat = _dot_nt(d_ref[...], w_ref[...])
        dp_ref[...] = dcat[:, :POOL_W].astype(BF16)
        _store_packed(da_ref, dcat[:, POOL_W:])

    return pl.pallas_call(
        body, name="mix_bwd", grid=(T // tm,),
        in_specs=[_tok(tm, D_MODEL), _res((D_MODEL, D_MODEL))],
        out_specs=[_tok(tm, POOL_W), _tok_packed(tm, ATTN_W)],
        out_shape=[jax.ShapeDtypeStruct((T, POOL_W), BF16), _packed(T, ATTN_W)],
        compiler_params=_params(1),
    )(dmix, w_out)


def _attn_bwd(q, k, v, dout, out, lse, dil):
    T = 2 * q.shape[1]
    nb = T // (BLK * dil)
    cw, cur, prv, prv_out, heads, _ = _attn_specs(dil, nb)
    ncb = ATTN_W // cw
    heads_per_step = cw // HEAD_DIM
    n_str = min(dil, 2)

    def body(q_ref, kc_ref, kp_ref, vc_ref, vp_ref, do_ref, o_ref, lse_ref,
             dq_ref, dkc_ref, dkp_ref, dvc_ref, dvp_ref):
        j = pl.program_id(1)
        r2 = pl.program_id(2)
        valid = _attn_mask(pl.program_id(0))
        lane = lax.broadcasted_iota(jnp.int32, (BLK, LANES), 1)
        lo = lane < HEAD_DIM
        lse_tiles = _load_streams_f32(lse_ref, dil, r2, None)
        for g in range(cw // LANES):
            sl = g
            qs, kcs, kps, vcs, vps, dos, os_ = [
                _load_streams(r, dil, r2, sl) for r in (q_ref, kc_ref, kp_ref, vc_ref, vp_ref, do_ref, o_ref)]
            dqs, dks, dvs = [], [], []
            for e in range(n_str):
                qg = qs[e] * 0.125
                dog = dos[e]
                kcat = jnp.concatenate([kps[e], kcs[e]], axis=0)
                vcat = jnp.concatenate([vps[e], vcs[e]], axis=0)
                prod = dog.astype(F32) * os_[e].astype(F32)
                qg_t = qg.astype(F32).T.astype(BF16)
                dog_t = dog.astype(F32).T.astype(BF16)
                dk_t, dv_t = [], []
                dq = None
                for hh in range(2):
                    hm = lo if hh == 0 else jnp.logical_not(lo)
                    feat = slice(hh * HEAD_DIM, (hh + 1) * HEAD_DIM)
                    qa = jnp.where(hm, qg, jnp.zeros_like(qg))
                    doa = jnp.where(hm, dog, jnp.zeros_like(dog))
                    s = jnp.where(valid, _dot_nt(qa, kcat), NEG)
                    p = jnp.exp(s - _head_col(lse_tiles[e], lane, j * heads_per_step + 2 * g + hh))
                    dsum = jnp.sum(jnp.where(hm, prod, 0.0), axis=1, keepdims=True)
                    ds = (p * (_dot_nt(doa, vcat) - dsum)).astype(BF16)
                    dv_t.append(_dot(dog_t[feat], p.astype(BF16)))
                    dk_t.append(_dot(qg_t[feat], ds))
                    dqh = _dot(ds, kcat) * 0.125
                    dq = dqh if hh == 0 else jnp.where(lo, dq, dqh)
                dqs.append(dq)
                dks.append(jnp.concatenate(dk_t, axis=0).T)
                dvs.append(jnp.concatenate(dv_t, axis=0).T)
            _store_streams(dq_ref, dil, r2, sl, dqs)
            _store_streams(dkp_ref, dil, r2, sl, [t[:BLK] for t in dks])
            _store_streams(dkc_ref, dil, r2, sl, [t[BLK:] for t in dks])
            _store_streams(dvp_ref, dil, r2, sl, [t[:BLK] for t in dvs])
            _store_streams(dvc_ref, dil, r2, sl, [t[BLK:] for t in dvs])

    return pl.pallas_call(
        body, name=f"attn_bwd_d{dil}", grid=(nb, ncb, max(dil // 2, 1)),
        in_specs=[cur, cur, prv, cur, prv, cur, cur, heads],
        out_specs=[cur, cur, prv_out, cur, prv_out],
        out_shape=[_packed(T, ATTN_W)] * 5,
        compiler_params=_params(3),
    )(q, k, k, v, v, dout, out, lse)


def _pool_bwd(u, dy, wbd, scale, tm):
    T = u.shape[0]
    nt = T // tm
    hb = tm // POOL_HALO

    def body(u_ref, prev_ref, dy_ref, next_ref, w_ref, sc_ref, du_ref, dw_ref, dsc_ref):
        i = pl.program_id(0)

        @pl.when(i == 0)
        def _():
            dw_ref[...] = jnp.zeros_like(dw_ref)
            dsc_ref[...] = jnp.zeros_like(dsc_ref)

        w = w_ref[...]
        sc = sc_ref[...]
        d = _pool_delta(u_ref[...], prev_ref[...], i, tm).astype(BF16)
        dyc = dy_ref[...].astype(F32)
        dsc_ref[...] += jnp.sum(dyc * _dot(d, w), axis=0, keepdims=True)
        nxt = jnp.where(i < nt - 1, next_ref[...].astype(F32), 0.0)
        dypre = (jnp.concatenate([dyc, nxt], axis=0) * sc).astype(BF16)
        dw_ref[...] += _dot_tn(d, dypre[:tm])
        dd = _dot_nt(dypre, w)
        n = tm + POOL_HALO
        lane = lax.broadcasted_iota(jnp.int32, (n, POOL_W), 1)
        row = lax.broadcasted_iota(jnp.int32, (n, POOL_W), 0) + i * tm
        gx = dd / jnp.minimum(row + 1, _pool_window(lane)).astype(F32)
        a2 = gx + pltpu.roll(gx, n - 1, 0)
        a4 = a2 + pltpu.roll(a2, n - 2, 0)
        a8 = a4 + pltpu.roll(a4, n - 4, 0)
        a16 = a8 + pltpu.roll(a8, n - 8, 0)
        fs = _pool_select(lane[:tm], a2[:tm], a4[:tm], a8[:tm], a16[:tm])
        du_ref[...] = (fs - dd[:tm]).astype(BF16)

    return pl.pallas_call(
        body, name="pool_bwd", grid=(nt,),
        in_specs=[_tok(tm, POOL_W), pl.BlockSpec((POOL_HALO, POOL_W), lambda i: (jnp.maximum(i * hb - 1, 0), 0)),
                  _tok(tm, POOL_W), pl.BlockSpec((POOL_HALO, POOL_W), lambda i: (jnp.minimum((i + 1) * hb, nt * hb - 1), 0)),
                  _res((POOL_W, POOL_W)), _res((1, POOL_W))],
        out_specs=[_tok(tm, POOL_W), _res((POOL_W, POOL_W)), _res((1, POOL_W))],
        out_shape=[jax.ShapeDtypeStruct((T, POOL_W), BF16), jax.ShapeDtypeStruct((POOL_W, POOL_W), F32),
                   jax.ShapeDtypeStruct((1, POOL_W), F32)],
        compiler_params=_params(1),
    )(u, u, dy, dy, wbd, scale)


def _dproj_combine(du, dqs, dkcs, dkps, dvcs, dvps, cos, sin, tm):
    T = du.shape[0]
    n_cfg = len(dqs)

    def body(*refs):
        du_ref = refs[0]
        groups = [refs[1 + j * n_cfg:1 + (j + 1) * n_cfg] for j in range(5)]
        c_ref, s_ref, out_ref = refs[1 + 5 * n_cfg:]
        tot = lambda rs: sum(_load_packed(r).astype(F32) for r in rs)
        c = c_ref[...]
        s = s_ref[...]
        dq = _rope(tot(groups[0]), c, s, -1.0)
        dk = _rope(tot(groups[1]) + tot(groups[2]), c, s, -1.0)
        dv = tot(groups[3]) + tot(groups[4])
        out_ref[...] = jnp.concatenate([du_ref[...], dq.astype(BF16), dk.astype(BF16), dv.astype(BF16)], axis=1)

    return pl.pallas_call(
        body, name="dproj_combine", grid=(T // tm,),
        in_specs=[_tok(tm, POOL_W)] + [_tok_packed(tm, ATTN_W)] * (5 * n_cfg) + [_tok(tm, LANES)] * 2,
        out_specs=_tok(tm, IN_W),
        out_shape=jax.ShapeDtypeStruct((T, IN_W), BF16),
        compiler_params=_params(1),
    )(du, *dqs, *dkcs, *dkps, *dvcs, *dvps, cos, sin)


def _proj_bwd(dproj, w_in_t, x, dx2, g1, tm):
    T = x.shape[0]

    def body(d_ref, w_ref, x_ref, r_ref, g_ref, dx_ref, dg_ref):
        @pl.when(pl.program_id(0) == 0)
        def _():
            dg_ref[...] = jnp.zeros_like(dg_ref)

        dn, dg = _rms_bwd(x_ref[...], g_ref[...], _dot(d_ref[...], w_ref[...]))
        dg_ref[...] += dg
        dx_ref[...] = r_ref[...] + dn

    return pl.pallas_call(
        body, name="proj_bwd", grid=(T // tm,),
        in_specs=[_tok(tm, IN_W), _res((IN_W, D_MODEL)), _tok(tm, D_MODEL), _tok(tm, D_MODEL), _res((1, D_MODEL))],
        out_specs=[_tok(tm, D_MODEL), _res((1, D_MODEL))],
        out_shape=[jax.ShapeDtypeStruct((T, D_MODEL), F32), jax.ShapeDtypeStruct((1, D_MODEL), F32)],
        compiler_params=_params(1),
    )(dproj, w_in_t, x, dx2, g1)


def _wgrad(a, b, name, tile_m, tk):
    T, M = a.shape
    N = b.shape[1]
    nk = T // tk

    def body(a_ref, b_ref, o_ref, acc_ref):
        kk = pl.program_id(1)

        @pl.when(kk == 0)
        def _():
            acc_ref[...] = jnp.zeros_like(acc_ref)

        acc_ref[...] += _dot_tn(a_ref[...], b_ref[...])

        @pl.when(kk == nk - 1)
        def _():
            o_ref[...] = acc_ref[...].astype(BF16)

    return pl.pallas_call(
        body, name=name, grid=(M // tile_m, nk),
        in_specs=[pl.BlockSpec((tk, tile_m), lambda j, kk: (kk, j)), pl.BlockSpec((tk, N), lambda j, kk: (kk, 0))],
        out_specs=pl.BlockSpec((tile_m, N), lambda j, kk: (j, 0)),
        out_shape=jax.ShapeDtypeStruct((M, N), BF16),
        scratch_shapes=[pltpu.VMEM((tile_m, N), F32)],
        compiler_params=_params(2),
    )(a, b)


def _exchange(arrs, scatter, name):
    n = len(arrs)
    out_shapes = [jax.ShapeDtypeStruct((N_DEV,) + (a.shape[1:] if sc else a.shape), a.dtype)
                  for a, sc in zip(arrs, scatter)]

    def body(*refs):
        ins, outs = refs[:n], refs[n:2 * n]
        send_sems, recv_sems, loc_sems = refs[2 * n:]
        x, y, c = lax.axis_index("x"), lax.axis_index("y"), lax.axis_index("c")
        me = 4 * x + 2 * y + c
        local, sends, recvs = [], [], []
        for i in range(n):
            own = ins[i].at[me] if scatter[i] else ins[i]
            loc = pltpu.make_async_copy(own, outs[i].at[me], loc_sems.at[i])
            loc.start()
            local.append(loc)
            for kbits in range(1, N_DEV):
                px = 1 - x if kbits & 4 else x
                py = 1 - y if kbits & 2 else y
                pc = 1 - c if kbits & 1 else c
                pid = 4 * px + 2 * py + pc
                src = ins[i].at[pid] if scatter[i] else ins[i]
                cp = pltpu.make_async_remote_copy(
                    src_ref=src, dst_ref=outs[i].at[me],
                    send_sem=send_sems.at[i, kbits - 1], recv_sem=recv_sems.at[i, kbits - 1],
                    device_id=(px, py, pc), device_id_type=pl.DeviceIdType.MESH)
                cp.start()
                sends.append(cp)
                recvs.append(pltpu.make_async_remote_copy(
                    src_ref=src, dst_ref=outs[i].at[pid],
                    send_sem=send_sems.at[i, kbits - 1], recv_sem=recv_sems.at[i, kbits - 1],
                    device_id=(px, py, pc), device_id_type=pl.DeviceIdType.MESH))
        for cp in recvs:
            cp.wait_recv()
        for cp in sends:
            cp.wait_send()
        for cp in local:
            cp.wait()

    hbm = pl.BlockSpec(memory_space=pl.ANY)
    return pl.pallas_call(
        body, name=name, in_specs=[hbm] * n, out_specs=[hbm] * n, out_shape=out_shapes,
        scratch_shapes=[pltpu.SemaphoreType.DMA((n, N_DEV - 1)), pltpu.SemaphoreType.DMA((n, N_DEV - 1)),
                        pltpu.SemaphoreType.DMA((n,))],
    )(*arrs)


def _peers(x, y, c):
    for kbits in range(1, N_DEV):
        px = 1 - x if kbits & 4 else x
        py = 1 - y if kbits & 2 else y
        pc = 1 - c if kbits & 1 else c
        yield kbits - 1, (px, py, pc), 4 * px + 2 * py + pc


def _peer_copies(ins, lands, scatter, send_sems, recv_sems, incoming):
    x, y, c = lax.axis_index("x"), lax.axis_index("y"), lax.axis_index("c")
    me = 4 * x + 2 * y + c
    copies = []
    for i in range(len(ins)):
        for k, peer, pid in _peers(x, y, c):
            slot = i * (N_DEV - 1) + k
            copies.append(pltpu.make_async_remote_copy(
                src_ref=ins[i].at[pid] if scatter[i] else ins[i], dst_ref=lands[i].at[pid if incoming else me],
                send_sem=send_sems.at[slot], recv_sem=recv_sems.at[slot],
                device_id=peer, device_id_type=pl.DeviceIdType.MESH))
    return copies


_HBM = pl.BlockSpec(memory_space=pltpu.HBM)
_SEM = pl.BlockSpec(memory_space=pltpu.SEMAPHORE)
_DATAFLOW = pltpu.SideEffectType.DATAFLOW_SIDE_EFFECTING


def _exchange_start(arrs, scatter, after, name):
    n = len(arrs)
    lands = [lax.empty((N_DEV,) + (a.shape[1:] if sc else a.shape), a.dtype) for a, sc in zip(arrs, scatter)]

    def body(*refs):
        ins, lz = refs[:n], refs[n:2 * n]
        send_sems, recv_sems = refs[2 * n + 1:2 * n + 3]
        token = refs[-1]
        for cp in _peer_copies(ins, lz, scatter, send_sems, recv_sems, False):
            cp.start()
        token[...] = jnp.zeros_like(token)

    sem_shape = pltpu.SemaphoreType.DMA((n * (N_DEV - 1),))
    outs = pl.pallas_call(
        body, name=name,
        out_shape=(sem_shape, sem_shape, *[pltpu.HBM(a.shape, a.dtype) for a in arrs + lands],
                   jax.ShapeDtypeStruct((8, LANES), F32)),
        in_specs=[_HBM] * (2 * n) + [pl.BlockSpec(memory_space=pl.ANY)],
        out_specs=(_SEM, _SEM, *[_HBM] * (2 * n), pl.BlockSpec(memory_space=pltpu.VMEM)),
        input_output_aliases={i: 2 + i for i in range(2 * n)},
        compiler_params=pltpu.CompilerParams(has_side_effects=_DATAFLOW),
    )(*[pltpu.with_memory_space_constraint(a, pltpu.HBM) for a in arrs + lands], after)
    return outs[0], outs[1], list(outs[2:2 + n]), list(outs[2 + n:2 + 2 * n]), outs[-1]


def _exchange_wait(handle, scatter, after, name):
    send_sems, recv_sems, srcs, lands, _ = handle
    n = len(srcs)

    def body(*refs):
        ins, lz = refs[:n], refs[n:2 * n]
        for cp in _peer_copies(ins, lz, scatter, refs[2 * n], refs[2 * n + 1], False):
            cp.wait_send()
        for cp in _peer_copies(ins, lz, scatter, refs[2 * n], refs[2 * n + 1], True):
            cp.wait_recv()

    outs = pl.pallas_call(
        body, name=name,
        out_shape=[pltpu.HBM(a.shape, a.dtype) for a in srcs + lands],
        in_specs=[_HBM] * (2 * n) + [_SEM, _SEM, pl.BlockSpec(memory_space=pl.ANY)],
        out_specs=[_HBM] * (2 * n),
        input_output_aliases={i: i for i in range(2 * n)},
        compiler_params=pltpu.CompilerParams(has_side_effects=_DATAFLOW),
    )(*srcs, *lands, send_sems, recv_sems, after)
    return list(outs[:n]), list(outs[n:])


def _fill_own(lands, srcs, scatter):
    me = 4 * lax.axis_index("x") + 2 * lax.axis_index("y") + lax.axis_index("c")
    own = [lax.dynamic_index_in_dim(s, me, 0, keepdims=False) if sc else s for s, sc in zip(srcs, scatter)]
    return [lax.dynamic_update_index_in_dim(land, o, me, 0) for land, o in zip(lands, own)]


def _slot_sum(parts, name, tr):
    _, R, C = parts.shape

    def body(p_ref, o_ref):
        acc = p_ref[0].astype(F32)
        for s in range(1, N_DEV):
            acc = acc + p_ref[s].astype(F32)
        o_ref[...] = acc

    return pl.pallas_call(
        body, name=name, grid=(R // tr,),
        in_specs=[pl.BlockSpec((N_DEV, tr, C), lambda i: (0, i, 0))],
        out_specs=pl.BlockSpec((tr, C), lambda i: (i, 0)),
        out_shape=jax.ShapeDtypeStruct((R, C), F32),
        compiler_params=_params(1),
    )(parts)


def _adamw(w, g, m, v, name):
    def body(w_ref, g_ref, m_ref, v_ref, d_ref, nm_ref, nv_ref):
        g = g_ref[...]
        nm = ADAM_B1 * m_ref[...] + (1.0 - ADAM_B1) * g
        nv = ADAM_B2 * v_ref[...] + (1.0 - ADAM_B2) * jnp.square(g)
        m_hat = nm / (1.0 - ADAM_B1 ** ADAM_STEP)
        v_hat = nv / (1.0 - ADAM_B2 ** ADAM_STEP)
        d_ref[...] = -ADAM_LR * (m_hat / (jnp.sqrt(v_hat) + ADAM_EPS) + ADAM_WD * w_ref[...])
        nm_ref[...] = nm
        nv_ref[...] = nv

    return pl.pallas_call(
        body, name=name, out_shape=[jax.ShapeDtypeStruct(w.shape, F32)] * 3,
        compiler_params=pltpu.CompilerParams(vmem_limit_bytes=VMEM_LIMIT),
    )(w, g, m, v)


def _rope_tables(T):
    half = HEAD_DIM // 2
    freqs = ROPE_THETA ** (-jnp.arange(half, dtype=F32) * (2.0 / HEAD_DIM))
    ang = jnp.arange(T).astype(F32)[:, None] * freqs[None, :]
    c, s = jnp.cos(ang), jnp.sin(ang)
    return jnp.concatenate([c, c, c, c], axis=1), jnp.concatenate([-s, s, -s, s], axis=1)


def _block_diag(w_pool):
    wbd = jnp.zeros((POOL_W, POOL_W), F32)
    g = POOL_W // len(POOL_WINDOWS)
    for i in range(len(POOL_WINDOWS)):
        wbd = wbd.at[i * g:(i + 1) * g, i * g:(i + 1) * g].set(w_pool[i])
    return wbd


def _pack_small(g1, w_pool, pool_scale, g2, g3, g4, extra):
    pad = lambda a: jnp.pad(a.reshape(1, -1), ((0, 0), (0, D_MODEL - a.size)))
    rows = [g1.reshape(1, -1), g2.reshape(1, -1), g3.reshape(1, -1), g4.reshape(1, -1),
            w_pool.reshape(-1, D_MODEL), pad(pool_scale), pad(extra)]
    buf = jnp.concatenate(rows, axis=0)
    return jnp.pad(buf, ((0, SMALL_ROWS - buf.shape[0]), (0, 0)))


def _unpack_small(buf):
    n_pool = len(POOL_WINDOWS) * (POOL_W // len(POOL_WINDOWS)) ** 2 // D_MODEL
    g = POOL_W // len(POOL_WINDOWS)
    return (buf[0:1], buf[4:4 + n_pool].reshape(1, len(POOL_WINDOWS), g, g), buf[4 + n_pool:5 + n_pool, :POOL_W],
            buf[1:2], buf[2:3], buf[3:4], buf[5 + n_pool])


class _LocalStep:
    def __init__(self, x, tgt, g1, w_pool, pool_scale, g2, g3, g4):
        self.x, self.tgt, self.pool_scale = x, tgt, pool_scale
        self.g1, self.g2, self.g3, self.g4 = g1, g2, g3, g4
        self.cos, self.sin = _rope_tables(x.shape[0])
        self.wbd = _block_diag(w_pool).astype(BF16)

    def mixer_fwd(self, w_in_t, token):
        self.w_in_t = w_in_t
        self.h1, self.u, self.q, self.k, self.v = _proj_fwd(
            self.x, self.g1 + token[0, 0], w_in_t, self.cos, self.sin, 512)
        self.pool = _pool_fwd(self.u, self.wbd, self.pool_scale, 512)
        prev = None
        for j, dil in enumerate(DILATIONS):
            prev = _attn_fwd(self.q, self.k, self.v, dil, prev, j == len(DILATIONS) - 1)
        self.attn, self.lse = prev
        return self.attn

    def ffn_fwd_bwd(self, w_out, wg_t, wu_t, w_down):
        self.w_out, self.wg_t, self.wu_t = w_out, wg_t, wu_t
        self.cat, self.mix, self.x2, h2 = _mix_fwd(self.pool, self.attn, self.x, w_out, self.g2, self.g3, 512)
        act_dgate, act_dup, act = _ffn_up(h2, wg_t, wu_t, 256)
        df, self.dy, self.dg4, self.loss = _ffn_down_loss(act, w_down, self.x2, self.g4, self.tgt, 512)
        self.dgate, self.dup = _ffn_act_bwd(df, w_down, act_dgate, act_dup, 256)
        return (_wgrad(self.dgate, h2, "wgrad_gate", D_FF // 2, 1024), _wgrad(self.dup, h2, "wgrad_up", D_FF // 2, 1024),
                _wgrad(act, df, "wgrad_down", D_FF // 2, 1024))

    def mixer_bwd(self, token):
        self.dx2, dmix, self.dg3, self.dg2 = _ffn_in_bwd(
            self.dgate, self.dup, self.wg_t, self.wu_t, self.x2, self.mix, self.dy, self.g3 + token[0, 0], self.g2, 512)
        dpool, dattn = _mix_bwd(dmix, self.w_out, 512)
        parts = [_attn_bwd(self.q, self.k, self.v, dattn, self.attn, self.lse, dil) for dil in DILATIONS]
        du, dwbd, self.dscale = _pool_bwd(self.u, dpool, self.wbd, self.pool_scale, 512)
        g = POOL_W // len(POOL_WINDOWS)
        self.dw_pool = jnp.stack([dwbd[i * g:(i + 1) * g, i * g:(i + 1) * g] for i in range(len(POOL_WINDOWS))])
        self.dproj = _dproj_combine(du, *[[p[j] for p in parts] for j in range(5)], self.cos, self.sin, 256)
        return _wgrad(self.dproj, self.h1, "wgrad_in", IN_W // 2, 1024), _wgrad(self.cat, dmix, "wgrad_out", D_MODEL, 1024)

    def input_bwd(self, token):
        grad_x, dg1 = _proj_bwd(self.dproj, self.w_in_t, self.x, self.dx2, self.g1 + token[0, 0], 512)
        return self.loss, grad_x, (dg1, self.dw_pool, self.dscale, self.dg2, self.dg3, self.dg4)


def _local_step(x, tgt, g1, w_pool, pool_scale, g2, g3, g4, w_in_t, w_out, wg_t, wu_t, w_down):
    zero = jnp.zeros((8, LANES), F32)
    step = _LocalStep(x, tgt, g1, w_pool, pool_scale, g2, g3, g4)
    step.mixer_fwd(w_in_t, zero)
    dw_gate, dw_up, dw_down = step.ffn_fwd_bwd(w_out, wg_t, wu_t, w_down)
    dw_in, dw_out = step.mixer_bwd(zero)
    loss, grad_x, small = step.input_bwd(zero)
    return loss, grad_x, small, (dw_in, dw_out, dw_gate, dw_up, dw_down)


def kernel(x, ln_pre_mix, w_in, w_pool, pool_scale, w_out, ln_post_mix, ln_pre_ffn, w_gate, w_up, w_down, ln_post_ffn, loss_target, m_ln_pre_mix, m_w_in, m_w_pool, m_pool_scale, m_w_out, m_ln_post_mix, m_ln_pre_ffn, m_w_gate, m_w_up, m_w_down, m_ln_post_ffn, v_ln_pre_mix, v_w_in, v_w_pool, v_pool_scale, v_w_out, v_ln_post_mix, v_ln_pre_ffn, v_w_gate, v_w_up, v_w_down, v_ln_post_ffn):
    shards = [w_in[0].T.astype(BF16), w_out[0].astype(BF16), w_gate[0].T.astype(BF16),
              w_up[0].T.astype(BF16), w_down[0].astype(BF16)]
    flat = lambda a: a.reshape(-1, D_MODEL)
    blocks = lambda a: a.reshape(N_DEV, -1, D_MODEL)
    step = _LocalStep(x[0], loss_target[0], ln_pre_mix, w_pool[0], pool_scale, ln_post_mix, ln_pre_ffn, ln_post_ffn)

    w_in_t = flat(_exchange(shards[:1], [False], "gather_w_in")[0])
    rest = _exchange_start(shards[1:], [False] * 4, w_in_t, "gather_rest_start")
    attn = step.mixer_fwd(w_in_t, rest[4])
    srcs, lands = _exchange_wait(rest, [False] * 4, attn, "gather_rest_wait")
    w_out_f, wg_t, wu_t, w_down_f = [flat(a) for a in _fill_own(lands, srcs, [False] * 4)]

    ffn = _exchange_start([blocks(a) for a in step.ffn_fwd_bwd(w_out_f, wg_t, wu_t, w_down_f)], [True] * 3,
                          step.dgate, "grads_ffn_start")
    mixer = _exchange_start([blocks(a) for a in step.mixer_bwd(ffn[4])], [True] * 2, step.dproj, "grads_mixer_start")
    loss, grad_x, small = step.input_bwd(mixer[4])
    got = []
    for handle, n_arr, nm in ((mixer, 2, "grads_mixer"), (ffn, 3, "grads_ffn")):
        srcs, lands = _exchange_wait(handle, [True] * n_arr, grad_x, nm + "_wait")
        got += _fill_own(lands, srcs, [True] * n_arr)
    sums = [_slot_sum(got[i], f"sum_grad_{i}", got[i].shape[1] // 2) for i in range(5)]

    small_buf = _pack_small(small[0], small[1], small[2], small[3], small[4], small[5], loss)
    small_sum = _slot_sum(_exchange([small_buf], [False], "gather_small")[0], "sum_small", SMALL_ROWS)

    g_in, g_out, g_gate, g_up, g_down = sums[0].T, sums[1], sums[2].T, sums[3].T, sums[4]
    upd = [_adamw(w[0], g, m[0], v[0], f"adamw_{nm}") for nm, w, g, m, v in (
        ("in", w_in, g_in, m_w_in, v_w_in), ("out", w_out, g_out, m_w_out, v_w_out),
        ("gate", w_gate, g_gate, m_w_gate, v_w_gate), ("up", w_up, g_up, m_w_up, v_w_up),
        ("down", w_down, g_down, m_w_down, v_w_down))]
    pack = lambda a, b, c, d, e, f: _pack_small(a, b[0], c, d, e, f, jnp.zeros((1,), F32))
    small_upd = _adamw(
        pack(ln_pre_mix, w_pool, pool_scale, ln_post_mix, ln_pre_ffn, ln_post_ffn), small_sum,
        pack(m_ln_pre_mix, m_w_pool, m_pool_scale, m_ln_post_mix, m_ln_pre_ffn, m_ln_post_ffn),
        pack(v_ln_pre_mix, v_w_pool, v_pool_scale, v_ln_post_mix, v_ln_pre_ffn, v_ln_post_ffn), "adamw_small")

    def tree(small6, big5):
        s1, spool, sscale, s2, s3, s4 = small6
        b_in, b_out, b_gate, b_up, b_down = [b[None] for b in big5]
        return [s1, b_in, spool, sscale, b_out, s2, s3, b_gate, b_up, b_down, s4]

    g_small = _unpack_small(small_sum)
    outs = [g_small[6][0], grad_x[None]]
    outs += tree(g_small[:6], [g_in, g_out, g_gate, g_up, g_down])
    for j in range(3):
        outs += tree(_unpack_small(small_upd[j])[:6], [u[j] for u in upd])
    return tuple(outs)
```

```python
import jax
import jax.numpy as jnp
from jax import lax
from jax.experimental import pallas as pl
from jax.experimental.pallas import tpu as pltpu

F32 = jnp.float32
BF16 = jnp.bfloat16

D_MODEL = 1024
POOL_W = 256
ATTN_W = 768
IN_W = 2560
D_FF = 2816
POOL_WINDOWS = (2, 4, 8, 16)
POOL_HALO = 16
DILATIONS = (1, 4, 16)
BLK = 128
LANES = 128
HEAD_DIM = 64
N_GROUPS = ATTN_W // LANES
ROPE_THETA = 10000.0
EPS = 1e-6
NEG = -1e30
N_DEV = 8
SMALL_ROWS = 24

ADAM_LR = 0.001
ADAM_B1 = 0.9
ADAM_B2 = 0.999
ADAM_EPS = 1e-08
ADAM_WD = 0.01
ADAM_STEP = 10

VMEM_LIMIT = 56 * 1024 * 1024


def _dot(a, b):
    return jnp.dot(a, b, preferred_element_type=F32)


def _dot_nt(a, b):
    return lax.dot_general(a, b, (((1,), (1,)), ((), ())), preferred_element_type=F32)


def _dot_tn(a, b):
    return lax.dot_general(a, b, (((0,), (0,)), ((), ())), preferred_element_type=F32)


def _params(n_grid):
    return pltpu.CompilerParams(dimension_semantics=("arbitrary",) * n_grid, vmem_limit_bytes=VMEM_LIMIT)


def _tok(tm, c):
    return pl.BlockSpec((tm, c), lambda i: (i, 0))


def _res(shape):
    return pl.BlockSpec(shape, lambda i: (0,) * len(shape))


def _rms_fwd(x, g):
    r = lax.rsqrt(jnp.mean(x * x, axis=-1, keepdims=True) + EPS)
    return x * r * g


def _rms_bwd(x, g, dy):
    r = lax.rsqrt(jnp.mean(x * x, axis=-1, keepdims=True) + EPS)
    xh = x * r
    gd = dy * g
    dx = r * (gd - xh * jnp.mean(gd * xh, axis=-1, keepdims=True))
    return dx, jnp.sum(dy * xh, axis=0, keepdims=True)


def _rope(x, c, s, sign):
    lane = lax.broadcasted_iota(jnp.int32, (x.shape[0], LANES), 1)
    first = (lane % HEAD_DIM) < (HEAD_DIM // 2)
    outs = []
    for g in range(x.shape[1] // LANES):
        xg = x[:, g * LANES:(g + 1) * LANES]
        rot = jnp.where(first, pltpu.roll(xg, LANES - HEAD_DIM // 2, 1), pltpu.roll(xg, HEAD_DIM // 2, 1))
        outs.append(xg * c + sign * (rot * s))
    return jnp.concatenate(outs, axis=1)


def _proj_fwd(x, g1, w_in_t, cos, sin, tm):
    T = x.shape[0]

    def body(x_ref, g_ref, w_ref, c_ref, s_ref, h_ref, u_ref, q_ref, k_ref, v_ref):
        h = _rms_fwd(x_ref[...], g_ref[...]).astype(BF16)
        h_ref[...] = h
        proj = _dot_nt(h, w_ref[...])
        c = c_ref[...]
        s = s_ref[...]
        u_ref[...] = proj[:, :POOL_W]
        _store_packed(q_ref, _rope(proj[:, POOL_W:POOL_W + ATTN_W], c, s, 1.0))
        _store_packed(k_ref, _rope(proj[:, POOL_W + ATTN_W:POOL_W + 2 * ATTN_W], c, s, 1.0))
        _store_packed(v_ref, proj[:, POOL_W + 2 * ATTN_W:])

    return pl.pallas_call(
        body, name="proj_fwd", grid=(T // tm,),
        in_specs=[_tok(tm, D_MODEL), _res((1, D_MODEL)), _res((IN_W, D_MODEL)), _tok(tm, LANES), _tok(tm, LANES)],
        out_specs=[_tok(tm, D_MODEL), _tok(tm, POOL_W)] + [_tok_packed(tm, ATTN_W)] * 3,
        out_shape=[jax.ShapeDtypeStruct((T, D_MODEL), BF16), jax.ShapeDtypeStruct((T, POOL_W), F32)]
        + [_packed(T, ATTN_W)] * 3,
        compiler_params=_params(1),
    )(x, g1, w_in_t, cos, sin)


def _pool_window(lane):
    return jnp.where(lane < 64, 2, jnp.where(lane < 128, 4, jnp.where(lane < 192, 8, 16)))


def _pool_select(lane, a2, a4, a8, a16):
    return jnp.where(lane < 64, a2, jnp.where(lane < 128, a4, jnp.where(lane < 192, a8, a16)))


def _pool_delta(cur, prev, i, tm):
    prev = jnp.where(i > 0, prev, 0.0)
    ext = jnp.concatenate([prev, cur], axis=0)
    s2 = ext + pltpu.roll(ext, 1, 0)
    s4 = s2 + pltpu.roll(s2, 2, 0)
    s8 = s4 + pltpu.roll(s4, 4, 0)
    s16 = s8 + pltpu.roll(s8, 8, 0)
    lane = lax.broadcasted_iota(jnp.int32, (tm, POOL_W), 1)
    row = lax.broadcasted_iota(jnp.int32, (tm, POOL_W), 0) + i * tm
    ws = _pool_select(lane, s2[POOL_HALO:], s4[POOL_HALO:], s8[POOL_HALO:], s16[POOL_HALO:])
    cnt = jnp.minimum(row + 1, _pool_window(lane)).astype(F32)
    return ws / cnt - cur


def _pool_fwd(u, wbd, scale, tm):
    T = u.shape[0]
    hb = tm // POOL_HALO

    def body(u_ref, prev_ref, w_ref, sc_ref, o_ref):
        d = _pool_delta(u_ref[...], prev_ref[...], pl.program_id(0), tm)
        o_ref[...] = (_dot(d.astype(BF16), w_ref[...]) * sc_ref[...]).astype(BF16)

    return pl.pallas_call(
        body, name="pool_fwd", grid=(T // tm,),
        in_specs=[_tok(tm, POOL_W), pl.BlockSpec((POOL_HALO, POOL_W), lambda i: (jnp.maximum(i * hb - 1, 0), 0)),
                  _res((POOL_W, POOL_W)), _res((1, POOL_W))],
        out_specs=_tok(tm, POOL_W),
        out_shape=jax.ShapeDtypeStruct((T, POOL_W), BF16),
        compiler_params=_params(1),
    )(u, u, wbd, scale)


def _attn_mask(n):
    qi = lax.broadcasted_iota(jnp.int32, (BLK, 2 * BLK), 0)
    kj = lax.broadcasted_iota(jnp.int32, (BLK, 2 * BLK), 1)
    dist = qi + BLK - kj
    return (dist >= 0) & (dist <= BLK) & ((kj >= BLK) | (n > 0))


def _stack_heads(x, lo):
    zero = jnp.zeros_like(x)
    return jnp.concatenate([jnp.where(lo, x, zero), jnp.where(lo, zero, x)], axis=0)


def _head_col(tile, lane, h):
    return jnp.sum(jnp.where(lane == h, tile, 0.0), axis=1, keepdims=True)


def _attn_cols(dil):
    return ATTN_W // 2 if dil >= 16 else ATTN_W


def _attn_specs(dil, nb):
    cw = _attn_cols(dil)
    ch = BLK * dil
    wide = lambda f: pl.BlockSpec((cw // LANES, ch // 2, LANES), f)
    full = pl.BlockSpec((cw // LANES, ch, LANES), lambda n, j, r: (j, n, 0))
    cur = lambda n, j, r: (j, n, 0)
    prv = lambda n, j, r: (j, jnp.maximum(n - 1, 0), 0)
    prv_out = lambda n, j, r: (j, (n + nb - 1) % nb, 0)
    heads = pl.BlockSpec((ch, LANES), lambda n, j, r: (n, 0))
    return cw, wide(cur), wide(prv), wide(prv_out), heads, full


HIGH_HALF = 0xFFFF0000


def _pack(x):
    return pltpu.bitcast(x.astype(BF16), F32)


def _unpack(words):
    return pltpu.bitcast(words, BF16)


def _packed(rows, cols):
    return jax.ShapeDtypeStruct((cols // LANES, rows // 2, LANES), F32)


def _tok_packed(tm, cols):
    return pl.BlockSpec((cols // LANES, tm // 2, LANES), lambda i: (0, i, 0))


def _store_packed(ref, x):
    for g in range(x.shape[1] // LANES):
        ref[g] = _pack(x[:, g * LANES:(g + 1) * LANES])


def _load_packed(ref):
    return jnp.concatenate([_unpack(ref[g]) for g in range(ref.shape[0])], axis=1)


def _load_streams(ref, dil, r2, sl):
    if dil == 1:
        return [_unpack(ref[sl])]
    words = lax.bitcast_convert_type(ref.at[sl][pl.ds(r2, BLK, stride=dil // 2), :], jnp.uint32)
    even = lax.bitcast_convert_type(words << 16, F32).astype(BF16)
    odd = lax.bitcast_convert_type(words & jnp.uint32(HIGH_HALF), F32).astype(BF16)
    return [even, odd]


def _load_streams_f32(ref, dil, r2, sl):
    ref = ref if sl is None else ref.at[sl]
    if dil == 1:
        return [ref[...]]
    return [ref[pl.ds(2 * r2 + e, BLK, stride=dil), :] for e in range(2)]


def _store_streams_f32(ref, dil, r2, sl, tiles):
    ref = ref if sl is None else ref.at[sl]
    if dil == 1:
        ref[...] = tiles[0]
    else:
        for e, t in enumerate(tiles):
            ref[pl.ds(2 * r2 + e, BLK, stride=dil), :] = t


def _store_streams(ref, dil, r2, sl, tiles):
    if dil == 1:
        ref[sl] = _pack(tiles[0])
    else:
        even, odd = [lax.bitcast_convert_type(t.astype(BF16).astype(F32), jnp.uint32) for t in tiles]
        words = (odd & jnp.uint32(HIGH_HALF)) | (even >> 16)
        ref.at[sl][pl.ds(r2, BLK, stride=dil // 2), :] = lax.bitcast_convert_type(words, F32)


def _attn_fwd(q, k, v, dil, prev, last):
    T = 2 * q.shape[1]
    nb = T // (BLK * dil)
    first = prev is None
    cw, cur, prv, _, heads, full = _attn_specs(dil, nb)
    ncb = ATTN_W // cw
    heads_per_step = cw // HEAD_DIM
    n_str = min(dil, 2)
    everything = None

    def body(*refs):
        if first:
            q_ref, kc_ref, kp_ref, vc_ref, vp_ref, acc_ref, lse_ref = refs
        else:
            q_ref, kc_ref, kp_ref, vc_ref, vp_ref, acc_in, lse_in, acc_ref, lse_ref = refs
        j = pl.program_id(1)
        r2 = pl.program_id(2)
        valid = _attn_mask(pl.program_id(0))
        lane = lax.broadcasted_iota(jnp.int32, (BLK, LANES), 1)
        lo = lane < HEAD_DIM
        lse_tiles = [jnp.zeros((BLK, LANES), F32) for _ in range(n_str)]
        store_acc = _store_streams if last else _store_streams_f32
        own = []
        for g in range(cw // LANES):
            qs, kcs, kps, vcs, vps = [_load_streams(r, dil, r2, g) for r in (q_ref, kc_ref, kp_ref, vc_ref, vp_ref)]
            pairs = []
            for e in range(n_str):
                qg = qs[e] * 0.125
                kcat = jnp.concatenate([kps[e], kcs[e]], axis=0)
                vcat = jnp.concatenate([vps[e], vcs[e]], axis=0)
                pair = None
                for hh in range(2):
                    h = j * heads_per_step + 2 * g + hh
                    hm = lo if hh == 0 else jnp.logical_not(lo)
                    s = _dot_nt(jnp.where(hm, qg, jnp.zeros_like(qg)), kcat)
                    s = jnp.where(valid, s, NEG)
                    m = jnp.max(s, axis=1, keepdims=True)
                    p = jnp.exp(s - m)
                    den = jnp.sum(p, axis=1, keepdims=True)
                    o = _dot(p.astype(BF16), vcat) / den
                    pair = o if hh == 0 else jnp.where(lo, pair, o)
                    lse_tiles[e] = jnp.where(lane == h, m + jnp.log(den), lse_tiles[e])
                pairs.append(pair)
            if first:
                store_acc(acc_ref, dil, r2, g, pairs)
            else:
                own.append(pairs)
        if not first:
            mine = (lane >= j * heads_per_step) & (lane < (j + 1) * heads_per_step)
            before = _load_streams_f32(lse_in, dil, r2, everything)
            w_before, w_own = [], []
            for e in range(n_str):
                mx = jnp.maximum(before[e], lse_tiles[e])
                total = mx + jnp.log(jnp.exp(before[e] - mx) + jnp.exp(lse_tiles[e] - mx))
                w_before.append(jnp.exp(before[e] - total))
                w_own.append(jnp.exp(lse_tiles[e] - total))
                lse_tiles[e] = jnp.where(mine, total, 0.0)
            for g in range(cw // LANES):
                h0 = j * heads_per_step + 2 * g
                spread = lambda w: jnp.where(lo, _head_col(w, lane, h0), _head_col(w, lane, h0 + 1))
                olds = _load_streams_f32(acc_in, dil, r2, g)
                store_acc(acc_ref, dil, r2, g, [olds[e] * spread(w_before[e]) + own[g][e] * spread(w_own[e])
                                                for e in range(n_str)])
        if ncb == 1:
            _store_streams_f32(lse_ref, dil, r2, everything, lse_tiles)
        else:
            @pl.when(j == 0)
            def _():
                _store_streams_f32(lse_ref, dil, r2, everything, lse_tiles)

            @pl.when(j > 0)
            def _():
                before = _load_streams_f32(lse_ref, dil, r2, everything)
                _store_streams_f32(lse_ref, dil, r2, everything, [a + b for a, b in zip(before, lse_tiles)])

    ins = [q, k, k, v, v]
    in_specs = [cur, cur, prv, cur, prv]
    if not first:
        ins += [prev[0], prev[1]]
        in_specs += [full, heads]
    return pl.pallas_call(
        body, name=f"attn_fwd_d{dil}", grid=(nb, ncb, max(dil // 2, 1)),
        in_specs=in_specs, out_specs=[cur if last else full, heads],
        out_shape=[_packed(T, ATTN_W) if last else jax.ShapeDtypeStruct((N_GROUPS, T, LANES), F32),
                   jax.ShapeDtypeStruct((T, LANES), F32)],
        compiler_params=_params(3),
    )(*ins)


def _mix_fwd(pool, attn, x, w_out, g2, g3, tm):
    T = x.shape[0]

    def body(p_ref, a_ref, x_ref, w_ref, g2_ref, g3_ref, cat_ref, mix_ref, x2_ref, h2_ref):
        p = p_ref[...]
        a = _load_packed(a_ref)
        cat_ref[...] = jnp.concatenate([p, a], axis=1)
        mix = _dot(p, w_ref[:POOL_W, :]) + _dot(a, w_ref[POOL_W:, :])
        mix_ref[...] = mix
        x2 = x_ref[...] + _rms_fwd(mix, g2_ref[...])
        x2_ref[...] = x2
        h2_ref[...] = _rms_fwd(x2, g3_ref[...]).astype(BF16)

    return pl.pallas_call(
        body, name="mix_fwd", grid=(T // tm,),
        in_specs=[_tok(tm, POOL_W), _tok_packed(tm, ATTN_W), _tok(tm, D_MODEL), _res((D_MODEL, D_MODEL)),
                  _res((1, D_MODEL)), _res((1, D_MODEL))],
        out_specs=[_tok(tm, D_MODEL)] * 4,
        out_shape=[jax.ShapeDtypeStruct((T, D_MODEL), BF16), jax.ShapeDtypeStruct((T, D_MODEL), F32),
                   jax.ShapeDtypeStruct((T, D_MODEL), F32), jax.ShapeDtypeStruct((T, D_MODEL), BF16)],
        compiler_params=_params(1),
    )(pool, attn, x, w_out, g2, g3)


def _ffn_up(h2, wg_t, wu_t, tm):
    T = h2.shape[0]

    def body(h_ref, wg_ref, wu_ref, dg_ref, du_ref, a_ref):
        h = h_ref[...]
        gate = _dot_nt(h, wg_ref[...])
        up = _dot_nt(h, wu_ref[...])
        sg = 1.0 / (1.0 + jnp.exp(-gate))
        silu = gate * sg
        a_ref[...] = (silu * up).astype(BF16)
        dg_ref[...] = (up * (sg * (1.0 + gate * (1.0 - sg)))).astype(BF16)
        du_ref[...] = silu.astype(BF16)

    return pl.pallas_call(
        body, name="ffn_up", grid=(T // tm,),
        in_specs=[_tok(tm, D_MODEL), _res((D_FF, D_MODEL)), _res((D_FF, D_MODEL))],
        out_specs=[_tok(tm, D_FF)] * 3,
        out_shape=[jax.ShapeDtypeStruct((T, D_FF), BF16)] * 3,
        compiler_params=_params(1),
    )(h2, wg_t, wu_t)


def _ffn_down_loss(act, w_down, x2, g4, tgt, tm):
    T = act.shape[0]

    def body(a_ref, w_ref, x2_ref, g_ref, t_ref, df_ref, dy_ref, dg_ref, loss_ref):
        i = pl.program_id(0)

        @pl.when(i == 0)
        def _():
            dg_ref[...] = jnp.zeros_like(dg_ref)
            loss_ref[...] = jnp.zeros_like(loss_ref)

        f = _dot(a_ref[...], w_ref[...])
        g = g_ref[...]
        err = x2_ref[...] + _rms_fwd(f, g) - t_ref[...]
        loss_ref[...] += 0.5 * jnp.sum(jnp.mean(err * err, axis=-1, keepdims=True), axis=0, keepdims=True)
        dy = err * (1.0 / D_MODEL)
        dy_ref[...] = dy
        df, dg = _rms_bwd(f, g, dy)
        dg_ref[...] += dg
        df_ref[...] = df.astype(BF16)

    return pl.pallas_call(
        body, name="ffn_down_loss", grid=(T // tm,),
        in_specs=[_tok(tm, D_FF), _res((D_FF, D_MODEL)), _tok(tm, D_MODEL), _res((1, D_MODEL)), _tok(tm, D_MODEL)],
        out_specs=[_tok(tm, D_MODEL), _tok(tm, D_MODEL), _res((1, D_MODEL)), _res((1, 1))],
        out_shape=[jax.ShapeDtypeStruct((T, D_MODEL), BF16), jax.ShapeDtypeStruct((T, D_MODEL), F32),
                   jax.ShapeDtypeStruct((1, D_MODEL), F32), jax.ShapeDtypeStruct((1, 1), F32)],
        compiler_params=_params(1),
    )(act, w_down, x2, g4, tgt)


def _ffn_act_bwd(df, w_down, act_dgate, act_dup, tm):
    T = df.shape[0]

    def body(df_ref, w_ref, ag_ref, au_ref, dg_ref, du_ref):
        dact = _dot_nt(df_ref[...], w_ref[...])
        dg_ref[...] = (dact * ag_ref[...].astype(F32)).astype(BF16)
        du_ref[...] = (dact * au_ref[...].astype(F32)).astype(BF16)

    return pl.pallas_call(
        body, name="ffn_act_bwd", grid=(T // tm,),
        in_specs=[_tok(tm, D_MODEL), _res((D_FF, D_MODEL)), _tok(tm, D_FF), _tok(tm, D_FF)],
        out_specs=[_tok(tm, D_FF)] * 2,
        out_shape=[jax.ShapeDtypeStruct((T, D_FF), BF16)] * 2,
        compiler_params=_params(1),
    )(df, w_down, act_dgate, act_dup)


def _ffn_in_bwd(dgate, dup, wg_t, wu_t, x2, mix, dy, g3, g2, tm):
    T = x2.shape[0]

    def body(dg_ref, du_ref, wg_ref, wu_ref, x2_ref, mix_ref, dy_ref, g3_ref, g2_ref,
             dx2_ref, dmix_ref, dg3_ref, dg2_ref):
        @pl.when(pl.program_id(0) == 0)
        def _():
            dg3_ref[...] = jnp.zeros_like(dg3_ref)
            dg2_ref[...] = jnp.zeros_like(dg2_ref)

        dh2 = _dot(dg_ref[...], wg_ref[...]) + _dot(du_ref[...], wu_ref[...])
        dn, dg3 = _rms_bwd(x2_ref[...], g3_ref[...], dh2)
        dx2 = dy_ref[...] + dn
        dx2_ref[...] = dx2
        dg3_ref[...] += dg3
        dmix, dg2 = _rms_bwd(mix_ref[...], g2_ref[...], dx2)
        dg2_ref[...] += dg2
        dmix_ref[...] = dmix.astype(BF16)

    return pl.pallas_call(
        body, name="ffn_in_bwd", grid=(T // tm,),
        in_specs=[_tok(tm, D_FF), _tok(tm, D_FF), _res((D_FF, D_MODEL)), _res((D_FF, D_MODEL)),
                  _tok(tm, D_MODEL), _tok(tm, D_MODEL), _tok(tm, D_MODEL), _res((1, D_MODEL)), _res((1, D_MODEL))],
        out_specs=[_tok(tm, D_MODEL), _tok(tm, D_MODEL), _res((1, D_MODEL)), _res((1, D_MODEL))],
        out_shape=[jax.ShapeDtypeStruct((T, D_MODEL), F32), jax.ShapeDtypeStruct((T, D_MODEL), BF16),
                   jax.ShapeDtypeStruct((1, D_MODEL), F32), jax.ShapeDtypeStruct((1, D_MODEL), F32)],
        compiler_params=_params(1),
    )(dgate, dup, wg_t, wu_t, x2, mix, dy, g3, g2)


def _mix_bwd(dmix, w_out, tm):
    T = dmix.shape[0]

    def body(d_ref, w_ref, dp_ref, da_ref):
        dcat = _dot_nt(d_ref[...], w_ref[...])
        dp_ref[...] = dcat[:, :POOL_W].astype(BF16)
        _store_packed(da_ref, dcat[:, POOL_W:])

    return pl.pallas_call(
        body, name="mix_bwd", grid=(T // tm,),
        in_specs=[_tok(tm, D_MODEL), _res((D_MODEL, D_MODEL))],
        out_specs=[_tok(tm, POOL_W), _tok_packed(tm, ATTN_W)],
        out_shape=[jax.ShapeDtypeStruct((T, POOL_W), BF16), _packed(T, ATTN_W)],
        compiler_params=_params(1),
    )(dmix, w_out)


def _attn_bwd(q, k, v, dout, out, lse, dil):
    T = 2 * q.shape[1]
    nb = T // (BLK * dil)
    cw, cur, prv, prv_out, heads, _ = _attn_specs(dil, nb)
    ncb = ATTN_W // cw
    heads_per_step = cw // HEAD_DIM
    n_str = min(dil, 2)

    def body(q_ref, kc_ref, kp_ref, vc_ref, vp_ref, do_ref, o_ref, lse_ref,
             dq_ref, dkc_ref, dkp_ref, dvc_ref, dvp_ref):
        j = pl.program_id(1)
        r2 = pl.program_id(2)
        valid = _attn_mask(pl.program_id(0))
        lane = lax.broadcasted_iota(jnp.int32, (BLK, LANES), 1)
        lo = lane < HEAD_DIM
        valid2 = jnp.concatenate([valid, valid], axis=0)
        lse_tiles = _load_streams_f32(lse_ref, dil, r2, None)
        for g in range(cw // LANES):
            sl = g
            qs, kcs, kps, vcs, vps, dos, os_ = [
                _load_streams(r, dil, r2, sl) for r in (q_ref, kc_ref, kp_ref, vc_ref, vp_ref, do_ref, o_ref)]
            dqs, dks, dvs = [], [], []
            for e in range(n_str):
                qg = qs[e] * 0.125
                dog = dos[e]
                kcat = jnp.concatenate([kps[e], kcs[e]], axis=0)
                vcat = jnp.concatenate([vps[e], vcs[e]], axis=0)
                prod = dog.astype(F32) * os_[e].astype(F32)
                h0 = j * heads_per_step + 2 * g
                q2 = _stack_heads(qg, lo)
                do2 = _stack_heads(dog, lo)
                lse2 = jnp.concatenate([_head_col(lse_tiles[e], lane, h0), _head_col(lse_tiles[e], lane, h0 + 1)], axis=0)
                dsum2 = jnp.concatenate([jnp.sum(jnp.where(lo, prod, 0.0), axis=1, keepdims=True),
                                         jnp.sum(jnp.where(lo, 0.0, prod), axis=1, keepdims=True)], axis=0)
                p = jnp.exp(jnp.where(valid2, _dot_nt(q2, kcat), NEG) - lse2)
                ds = (p * (_dot_nt(do2, vcat) - dsum2)).astype(BF16)
                dvs.append(_dot_tn(p.astype(BF16), do2))
                dks.append(_dot_tn(ds, q2))
                dq2 = _dot(ds, kcat) * 0.125
                dqs.append(jnp.where(lo, dq2[:BLK], dq2[BLK:]))
            _store_streams(dq_ref, dil, r2, sl, dqs)
            _store_streams(dkp_ref, dil, r2, sl, [t[:BLK] for t in dks])
            _store_streams(dkc_ref, dil, r2, sl, [t[BLK:] for t in dks])
            _store_streams(dvp_ref, dil, r2, sl, [t[:BLK] for t in dvs])
            _store_streams(dvc_ref, dil, r2, sl, [t[BLK:] for t in dvs])

    return pl.pallas_call(
        body, name=f"attn_bwd_d{dil}", grid=(nb, ncb, max(dil // 2, 1)),
        in_specs=[cur, cur, prv, cur, prv, cur, cur, heads],
        out_specs=[cur, cur, prv_out, cur, prv_out],
        out_shape=[_packed(T, ATTN_W)] * 5,
        compiler_params=_params(3),
    )(q, k, k, v, v, dout, out, lse)


def _pool_bwd(u, dy, wbd, scale, tm):
    T = u.shape[0]
    nt = T // tm
    hb = tm // POOL_HALO

    def body(u_ref, prev_ref, dy_ref, next_ref, w_ref, sc_ref, du_ref, dw_ref, dsc_ref):
        i = pl.program_id(0)

        @pl.when(i == 0)
        def _():
            dw_ref[...] = jnp.zeros_like(dw_ref)
            dsc_ref[...] = jnp.zeros_like(dsc_ref)

        w = w_ref[...]
        sc = sc_ref[...]
        d = _pool_delta(u_ref[...], prev_ref[...], i, tm).astype(BF16)
        dyc = dy_ref[...].astype(F32)
        dsc_ref[...] += jnp.sum(dyc * _dot(d, w), axis=0, keepdims=True)
        nxt = jnp.where(i < nt - 1, next_ref[...].astype(F32), 0.0)
        dypre = (jnp.concatenate([dyc, nxt], axis=0) * sc).astype(BF16)
        dw_ref[...] += _dot_tn(d, dypre[:tm])
        dd = _dot_nt(dypre, w)
        n = tm + POOL_HALO
        lane = lax.broadcasted_iota(jnp.int32, (n, POOL_W), 1)
        row = lax.broadcasted_iota(jnp.int32, (n, POOL_W), 0) + i * tm
        gx = dd / jnp.minimum(row + 1, _pool_window(lane)).astype(F32)
        a2 = gx + pltpu.roll(gx, n - 1, 0)
        a4 = a2 + pltpu.roll(a2, n - 2, 0)
        a8 = a4 + pltpu.roll(a4, n - 4, 0)
        a16 = a8 + pltpu.roll(a8, n - 8, 0)
        fs = _pool_select(lane[:tm], a2[:tm], a4[:tm], a8[:tm], a16[:tm])
        du_ref[...] = (fs - dd[:tm]).astype(BF16)

    return pl.pallas_call(
        body, name="pool_bwd", grid=(nt,),
        in_specs=[_tok(tm, POOL_W), pl.BlockSpec((POOL_HALO, POOL_W), lambda i: (jnp.maximum(i * hb - 1, 0), 0)),
                  _tok(tm, POOL_W), pl.BlockSpec((POOL_HALO, POOL_W), lambda i: (jnp.minimum((i + 1) * hb, nt * hb - 1), 0)),
                  _res((POOL_W, POOL_W)), _res((1, POOL_W))],
        out_specs=[_tok(tm, POOL_W), _res((POOL_W, POOL_W)), _res((1, POOL_W))],
        out_shape=[jax.ShapeDtypeStruct((T, POOL_W), BF16), jax.ShapeDtypeStruct((POOL_W, POOL_W), F32),
                   jax.ShapeDtypeStruct((1, POOL_W), F32)],
        compiler_params=_params(1),
    )(u, u, dy, dy, wbd, scale)


def _dproj_combine(du, dqs, dkcs, dkps, dvcs, dvps, cos, sin, tm):
    T = du.shape[0]
    n_cfg = len(dqs)

    def body(*refs):
        du_ref = refs[0]
        groups = [refs[1 + j * n_cfg:1 + (j + 1) * n_cfg] for j in range(5)]
        c_ref, s_ref, out_ref = refs[1 + 5 * n_cfg:]
        tot = lambda rs: sum(_load_packed(r).astype(F32) for r in rs)
        c = c_ref[...]
        s = s_ref[...]
        dq = _rope(tot(groups[0]), c, s, -1.0)
        dk = _rope(tot(groups[1]) + tot(groups[2]), c, s, -1.0)
        dv = tot(groups[3]) + tot(groups[4])
        out_ref[...] = jnp.concatenate([du_ref[...], dq.astype(BF16), dk.astype(BF16), dv.astype(BF16)], axis=1)

    return pl.pallas_call(
        body, name="dproj_combine", grid=(T // tm,),
        in_specs=[_tok(tm, POOL_W)] + [_tok_packed(tm, ATTN_W)] * (5 * n_cfg) + [_tok(tm, LANES)] * 2,
        out_specs=_tok(tm, IN_W),
        out_shape=jax.ShapeDtypeStruct((T, IN_W), BF16),
        compiler_params=_params(1),
    )(du, *dqs, *dkcs, *dkps, *dvcs, *dvps, cos, sin)


def _proj_bwd(dproj, w_in_t, x, dx2, g1, tm):
    T = x.shape[0]

    def body(d_ref, w_ref, x_ref, r_ref, g_ref, dx_ref, dg_ref):
        @pl.when(pl.program_id(0) == 0)
        def _():
            dg_ref[...] = jnp.zeros_like(dg_ref)

        dn, dg = _rms_bwd(x_ref[...], g_ref[...], _dot(d_ref[...], w_ref[...]))
        dg_ref[...] += dg
        dx_ref[...] = r_ref[...] + dn

    return pl.pallas_call(
        body, name="proj_bwd", grid=(T // tm,),
        in_specs=[_tok(tm, IN_W), _res((IN_W, D_MODEL)), _tok(tm, D_MODEL), _tok(tm, D_MODEL), _res((1, D_MODEL))],
        out_specs=[_tok(tm, D_MODEL), _res((1, D_MODEL))],
        out_shape=[jax.ShapeDtypeStruct((T, D_MODEL), F32), jax.ShapeDtypeStruct((1, D_MODEL), F32)],
        compiler_params=_params(1),
    )(dproj, w_in_t, x, dx2, g1)


def _wgrad(a, b, name, tile_m, tk):
    T, M = a.shape
    N = b.shape[1]
    nk = T // tk

    def body(a_ref, b_ref, o_ref, acc_ref):
        kk = pl.program_id(1)

        @pl.when(kk == 0)
        def _():
            acc_ref[...] = jnp.zeros_like(acc_ref)

        acc_ref[...] += _dot_tn(a_ref[...], b_ref[...])

        @pl.when(kk == nk - 1)
        def _():
            o_ref[...] = acc_ref[...].astype(BF16)

    return pl.pallas_call(
        body, name=name, grid=(M // tile_m, nk),
        in_specs=[pl.BlockSpec((tk, tile_m), lambda j, kk: (kk, j)), pl.BlockSpec((tk, N), lambda j, kk: (kk, 0))],
        out_specs=pl.BlockSpec((tile_m, N), lambda j, kk: (j, 0)),
        out_shape=jax.ShapeDtypeStruct((M, N), BF16),
        scratch_shapes=[pltpu.VMEM((tile_m, N), F32)],
        compiler_params=_params(2),
    )(a, b)


def _exchange(arrs, scatter, name):
    n = len(arrs)
    out_shapes = [jax.ShapeDtypeStruct((N_DEV,) + (a.shape[1:] if sc else a.shape), a.dtype)
                  for a, sc in zip(arrs, scatter)]

    def body(*refs):
        ins, outs = refs[:n], refs[n:2 * n]
        send_sems, recv_sems, loc_sems = refs[2 * n:]
        x, y, c = lax.axis_index("x"), lax.axis_index("y"), lax.axis_index("c")
        me = 4 * x + 2 * y + c
        local, sends, recvs = [], [], []
        for i in range(n):
            own = ins[i].at[me] if scatter[i] else ins[i]
            loc = pltpu.make_async_copy(own, outs[i].at[me], loc_sems.at[i])
            loc.start()
            local.append(loc)
            for kbits in range(1, N_DEV):
                px = 1 - x if kbits & 4 else x
                py = 1 - y if kbits & 2 else y
                pc = 1 - c if kbits & 1 else c
                pid = 4 * px + 2 * py + pc
                src = ins[i].at[pid] if scatter[i] else ins[i]
                cp = pltpu.make_async_remote_copy(
                    src_ref=src, dst_ref=outs[i].at[me],
                    send_sem=send_sems.at[i, kbits - 1], recv_sem=recv_sems.at[i, kbits - 1],
                    device_id=(px, py, pc), device_id_type=pl.DeviceIdType.MESH)
                cp.start()
                sends.append(cp)
                recvs.append(pltpu.make_async_remote_copy(
                    src_ref=src, dst_ref=outs[i].at[pid],
                    send_sem=send_sems.at[i, kbits - 1], recv_sem=recv_sems.at[i, kbits - 1],
                    device_id=(px, py, pc), device_id_type=pl.DeviceIdType.MESH))
        for cp in recvs:
            cp.wait_recv()
        for cp in sends:
            cp.wait_send()
        for cp in local:
            cp.wait()

    hbm = pl.BlockSpec(memory_space=pl.ANY)
    return pl.pallas_call(
        body, name=name, in_specs=[hbm] * n, out_specs=[hbm] * n, out_shape=out_shapes,
        scratch_shapes=[pltpu.SemaphoreType.DMA((n, N_DEV - 1)), pltpu.SemaphoreType.DMA((n, N_DEV - 1)),
                        pltpu.SemaphoreType.DMA((n,))],
    )(*arrs)


def _peers(x, y, c):
    for kbits in range(1, N_DEV):
        px = 1 - x if kbits & 4 else x
        py = 1 - y if kbits & 2 else y
        pc = 1 - c if kbits & 1 else c
        yield kbits - 1, (px, py, pc), 4 * px + 2 * py + pc


def _peer_copies(ins, lands, scatter, send_sems, recv_sems, incoming):
    x, y, c = lax.axis_index("x"), lax.axis_index("y"), lax.axis_index("c")
    me = 4 * x + 2 * y + c
    copies = []
    for i in range(len(ins)):
        for k, peer, pid in _peers(x, y, c):
            slot = i * (N_DEV - 1) + k
            copies.append(pltpu.make_async_remote_copy(
                src_ref=ins[i].at[pid] if scatter[i] else ins[i], dst_ref=lands[i].at[pid if incoming else me],
                send_sem=send_sems.at[slot], recv_sem=recv_sems.at[slot],
                device_id=peer, device_id_type=pl.DeviceIdType.MESH))
    return copies


_HBM = pl.BlockSpec(memory_space=pltpu.HBM)
_SEM = pl.BlockSpec(memory_space=pltpu.SEMAPHORE)
_DATAFLOW = pltpu.SideEffectType.DATAFLOW_SIDE_EFFECTING


def _exchange_start(arrs, scatter, after, name):
    n = len(arrs)
    lands = [lax.empty((N_DEV,) + (a.shape[1:] if sc else a.shape), a.dtype) for a, sc in zip(arrs, scatter)]

    def body(*refs):
        ins, lz = refs[:n], refs[n:2 * n]
        send_sems, recv_sems = refs[2 * n + 1:2 * n + 3]
        token = refs[-1]
        for cp in _peer_copies(ins, lz, scatter, send_sems, recv_sems, False):
            cp.start()
        token[...] = jnp.zeros_like(token)

    sem_shape = pltpu.SemaphoreType.DMA((n * (N_DEV - 1),))
    outs = pl.pallas_call(
        body, name=name,
        out_shape=(sem_shape, sem_shape, *[pltpu.HBM(a.shape, a.dtype) for a in arrs + lands],
                   jax.ShapeDtypeStruct((8, LANES), F32)),
        in_specs=[_HBM] * (2 * n) + [pl.BlockSpec(memory_space=pl.ANY)],
        out_specs=(_SEM, _SEM, *[_HBM] * (2 * n), pl.BlockSpec(memory_space=pltpu.VMEM)),
        input_output_aliases={i: 2 + i for i in range(2 * n)},
        compiler_params=pltpu.CompilerParams(has_side_effects=_DATAFLOW),
    )(*[pltpu.with_memory_space_constraint(a, pltpu.HBM) for a in arrs + lands], after)
    return outs[0], outs[1], list(outs[2:2 + n]), list(outs[2 + n:2 + 2 * n]), outs[-1]


def _exchange_wait(handle, scatter, after, name):
    send_sems, recv_sems, srcs, lands, _ = handle
    n = len(srcs)

    def body(*refs):
        ins, lz = refs[:n], refs[n:2 * n]
        for cp in _peer_copies(ins, lz, scatter, refs[2 * n], refs[2 * n + 1], False):
            cp.wait_send()
        for cp in _peer_copies(ins, lz, scatter, refs[2 * n], refs[2 * n + 1], True):
            cp.wait_recv()

    outs = pl.pallas_call(
        body, name=name,
        out_shape=[pltpu.HBM(a.shape, a.dtype) for a in srcs + lands],
        in_specs=[_HBM] * (2 * n) + [_SEM, _SEM, pl.BlockSpec(memory_space=pl.ANY)],
        out_specs=[_HBM] * (2 * n),
        input_output_aliases={i: i for i in range(2 * n)},
        compiler_params=pltpu.CompilerParams(has_side_effects=_DATAFLOW),
    )(*srcs, *lands, send_sems, recv_sems, after)
    return list(outs[:n]), list(outs[n:])


def _fill_own(lands, srcs, scatter):
    me = 4 * lax.axis_index("x") + 2 * lax.axis_index("y") + lax.axis_index("c")
    own = [lax.dynamic_index_in_dim(s, me, 0, keepdims=False) if sc else s for s, sc in zip(srcs, scatter)]
    return [lax.dynamic_update_index_in_dim(land, o, me, 0) for land, o in zip(lands, own)]


def _slot_sum(parts, name, tr):
    _, R, C = parts.shape

    def body(p_ref, o_ref):
        acc = p_ref[0].astype(F32)
        for s in range(1, N_DEV):
            acc = acc + p_ref[s].astype(F32)
        o_ref[...] = acc

    return pl.pallas_call(
        body, name=name, grid=(R // tr,),
        in_specs=[pl.BlockSpec((N_DEV, tr, C), lambda i: (0, i, 0))],
        out_specs=pl.BlockSpec((tr, C), lambda i: (i, 0)),
        out_shape=jax.ShapeDtypeStruct((R, C), F32),
        compiler_params=_params(1),
    )(parts)


def _adamw(w, g, m, v, name):
    def body(w_ref, g_ref, m_ref, v_ref, d_ref, nm_ref, nv_ref):
        g = g_ref[...]
        nm = ADAM_B1 * m_ref[...] + (1.0 - ADAM_B1) * g
        nv = ADAM_B2 * v_ref[...] + (1.0 - ADAM_B2) * jnp.square(g)
        m_hat = nm / (1.0 - ADAM_B1 ** ADAM_STEP)
        v_hat = nv / (1.0 - ADAM_B2 ** ADAM_STEP)
        d_ref[...] = -ADAM_LR * (m_hat / (jnp.sqrt(v_hat) + ADAM_EPS) + ADAM_WD * w_ref[...])
        nm_ref[...] = nm
        nv_ref[...] = nv

    return pl.pallas_call(
        body, name=name, out_shape=[jax.ShapeDtypeStruct(w.shape, F32)] * 3,
        compiler_params=pltpu.CompilerParams(vmem_limit_bytes=VMEM_LIMIT),
    )(w, g, m, v)


def _rope_tables(T):
    half = HEAD_DIM // 2
    freqs = ROPE_THETA ** (-jnp.arange(half, dtype=F32) * (2.0 / HEAD_DIM))
    ang = jnp.arange(T).astype(F32)[:, None] * freqs[None, :]
    c, s = jnp.cos(ang), jnp.sin(ang)
    return jnp.concatenate([c, c, c, c], axis=1), jnp.concatenate([-s, s, -s, s], axis=1)


def _block_diag(w_pool):
    wbd = jnp.zeros((POOL_W, POOL_W), F32)
    g = POOL_W // len(POOL_WINDOWS)
    for i in range(len(POOL_WINDOWS)):
        wbd = wbd.at[i * g:(i + 1) * g, i * g:(i + 1) * g].set(w_pool[i])
    return wbd


def _pack_small(g1, w_pool, pool_scale, g2, g3, g4, extra):
    pad = lambda a: jnp.pad(a.reshape(1, -1), ((0, 0), (0, D_MODEL - a.size)))
    rows = [g1.reshape(1, -1), g2.reshape(1, -1), g3.reshape(1, -1), g4.reshape(1, -1),
            w_pool.reshape(-1, D_MODEL), pad(pool_scale), pad(extra)]
    buf = jnp.concatenate(rows, axis=0)
    return jnp.pad(buf, ((0, SMALL_ROWS - buf.shape[0]), (0, 0)))


def _unpack_small(buf):
    n_pool = len(POOL_WINDOWS) * (POOL_W // len(POOL_WINDOWS)) ** 2 // D_MODEL
    g = POOL_W // len(POOL_WINDOWS)
    return (buf[0:1], buf[4:4 + n_pool].reshape(1, len(POOL_WINDOWS), g, g), buf[4 + n_pool:5 + n_pool, :POOL_W],
            buf[1:2], buf[2:3], buf[3:4], buf[5 + n_pool])


class _LocalStep:
    def __init__(self, x, tgt, g1, w_pool, pool_scale, g2, g3, g4):
        self.x, self.tgt, self.pool_scale = x, tgt, pool_scale
        self.g1, self.g2, self.g3, self.g4 = g1, g2, g3, g4
        self.cos, self.sin = _rope_tables(x.shape[0])
        self.wbd = _block_diag(w_pool).astype(BF16)

    def mixer_fwd(self, w_in_t, token):
        self.w_in_t = w_in_t
        self.h1, self.u, self.q, self.k, self.v = _proj_fwd(
            self.x, self.g1 + token[0, 0], w_in_t, self.cos, self.sin, 512)
        self.pool = _pool_fwd(self.u, self.wbd, self.pool_scale, 512)
        prev = None
        for j, dil in enumerate(DILATIONS):
            prev = _attn_fwd(self.q, self.k, self.v, dil, prev, j == len(DILATIONS) - 1)
        self.attn, self.lse = prev
        return self.attn

    def ffn_fwd_bwd(self, w_out, wg_t, wu_t, w_down):
        self.w_out, self.wg_t, self.wu_t = w_out, wg_t, wu_t
        self.cat, self.mix, self.x2, h2 = _mix_fwd(self.pool, self.attn, self.x, w_out, self.g2, self.g3, 512)
        act_dgate, act_dup, act = _ffn_up(h2, wg_t, wu_t, 256)
        df, self.dy, self.dg4, self.loss = _ffn_down_loss(act, w_down, self.x2, self.g4, self.tgt, 512)
        self.dgate, self.dup = _ffn_act_bwd(df, w_down, act_dgate, act_dup, 256)
        return (_wgrad(self.dgate, h2, "wgrad_gate", D_FF // 2, 1024), _wgrad(self.dup, h2, "wgrad_up", D_FF // 2, 1024),
                _wgrad(act, df, "wgrad_down", D_FF // 2, 1024))

    def mixer_bwd(self, token):
        self.dx2, dmix, self.dg3, self.dg2 = _ffn_in_bwd(
            self.dgate, self.dup, self.wg_t, self.wu_t, self.x2, self.mix, self.dy, self.g3 + token[0, 0], self.g2, 512)
        dpool, dattn = _mix_bwd(dmix, self.w_out, 512)
        parts = [_attn_bwd(self.q, self.k, self.v, dattn, self.attn, self.lse, dil) for dil in DILATIONS]
        du, dwbd, self.dscale = _pool_bwd(self.u, dpool, self.wbd, self.pool_scale, 512)
        g = POOL_W // len(POOL_WINDOWS)
        self.dw_pool = jnp.stack([dwbd[i * g:(i + 1) * g, i * g:(i + 1) * g] for i in range(len(POOL_WINDOWS))])
        self.dproj = _dproj_combine(du, *[[p[j] for p in parts] for j in range(5)], self.cos, self.sin, 256)
        return _wgrad(self.dproj, self.h1, "wgrad_in", IN_W // 2, 1024), _wgrad(self.cat, dmix, "wgrad_out", D_MODEL, 1024)

    def input_bwd(self, token):
        grad_x, dg1 = _proj_bwd(self.dproj, self.w_in_t, self.x, self.dx2, self.g1 + token[0, 0], 512)
        return self.loss, grad_x, (dg1, self.dw_pool, self.dscale, self.dg2, self.dg3, self.dg4)


def _local_step(x, tgt, g1, w_pool, pool_scale, g2, g3, g4, w_in_t, w_out, wg_t, wu_t, w_down):
    zero = jnp.zeros((8, LANES), F32)
    step = _LocalStep(x, tgt, g1, w_pool, pool_scale, g2, g3, g4)
    step.mixer_fwd(w_in_t, zero)
    dw_gate, dw_up, dw_down = step.ffn_fwd_bwd(w_out, wg_t, wu_t, w_down)
    dw_in, dw_out = step.mixer_bwd(zero)
    loss, grad_x, small = step.input_bwd(zero)
    return loss, grad_x, small, (dw_in, dw_out, dw_gate, dw_up, dw_down)


def kernel(x, ln_pre_mix, w_in, w_pool, pool_scale, w_out, ln_post_mix, ln_pre_ffn, w_gate, w_up, w_down, ln_post_ffn, loss_target, m_ln_pre_mix, m_w_in, m_w_pool, m_pool_scale, m_w_out, m_ln_post_mix, m_ln_pre_ffn, m_w_gate, m_w_up, m_w_down, m_ln_post_ffn, v_ln_pre_mix, v_w_in, v_w_pool, v_pool_scale, v_w_out, v_ln_post_mix, v_ln_pre_ffn, v_w_gate, v_w_up, v_w_down, v_ln_post_ffn):
    shards = [w_in[0].T.astype(BF16), w_out[0].astype(BF16), w_gate[0].T.astype(BF16),
              w_up[0].T.astype(BF16), w_down[0].astype(BF16)]
    flat = lambda a: a.reshape(-1, D_MODEL)
    blocks = lambda a: a.reshape(N_DEV, -1, D_MODEL)
    step = _LocalStep(x[0], loss_target[0], ln_pre_mix, w_pool[0], pool_scale, ln_post_mix, ln_pre_ffn, ln_post_ffn)

    w_in_t = flat(_exchange(shards[:1], [False], "gather_w_in")[0])
    rest = _exchange_start(shards[1:], [False] * 4, w_in_t, "gather_rest_start")
    attn = step.mixer_fwd(w_in_t, rest[4])
    srcs, lands = _exchange_wait(rest, [False] * 4, attn, "gather_rest_wait")
    w_out_f, wg_t, wu_t, w_down_f = [flat(a) for a in _fill_own(lands, srcs, [False] * 4)]

    ffn = _exchange_start([blocks(a) for a in step.ffn_fwd_bwd(w_out_f, wg_t, wu_t, w_down_f)], [True] * 3,
                          step.dgate, "grads_ffn_start")
    mixer = _exchange_start([blocks(a) for a in step.mixer_bwd(ffn[4])], [True] * 2, step.dproj, "grads_mixer_start")
    loss, grad_x, small = step.input_bwd(mixer[4])
    got = []
    for handle, n_arr, nm in ((mixer, 2, "grads_mixer"), (ffn, 3, "grads_ffn")):
        srcs, lands = _exchange_wait(handle, [True] * n_arr, grad_x, nm + "_wait")
        got += _fill_own(lands, srcs, [True] * n_arr)
    sums = [_slot_sum(got[i], f"sum_grad_{i}", got[i].shape[1] // 2) for i in range(5)]

    small_buf = _pack_small(small[0], small[1], small[2], small[3], small[4], small[5], loss)
    small_sum = _slot_sum(_exchange([small_buf], [False], "gather_small")[0], "sum_small", SMALL_ROWS)

    g_in, g_out, g_gate, g_up, g_down = sums[0].T, sums[1], sums[2].T, sums[3].T, sums[4]
    upd = [_adamw(w[0], g, m[0], v[0], f"adamw_{nm}") for nm, w, g, m, v in (
        ("in", w_in, g_in, m_w_in, v_w_in), ("out", w_out, g_out, m_w_out, v_w_out),
        ("gate", w_gate, g_gate, m_w_gate, v_w_gate), ("up", w_up, g_up, m_w_up, v_w_up),
        ("down", w_down, g_down, m_w_down, v_w_down))]
    pack = lambda a, b, c, d, e, f: _pack_small(a, b[0], c, d, e, f, jnp.zeros((1,), F32))
    small_upd = _adamw(
        pack(ln_pre_mix, w_pool, pool_scale, ln_post_mix, ln_pre_ffn, ln_post_ffn), small_sum,
        pack(m_ln_pre_mix, m_w_pool, m_pool_scale, m_ln_post_mix, m_ln_pre_ffn, m_ln_post_ffn),
        pack(v_ln_pre_mix, v_w_pool, v_pool_scale, v_ln_post_mix, v_ln_pre_ffn, v_ln_post_ffn), "adamw_small")

    def tree(small6, big5):
        s1, spool, sscale, s2, s3, s4 = small6
        b_in, b_out, b_gate, b_up, b_down = [b[None] for b in big5]
        return [s1, b_in, spool, sscale, b_out, s2, s3, b_gate, b_up, b_down, s4]

    g_small = _unpack_small(small_sum)
    outs = [g_small[6][0], grad_x[None]]
    outs += tree(g_small[:6], [g_in, g_out, g_gate, g_up, g_down])
    for j in range(3):
        outs += tree(_unpack_small(small_upd[j])[:6], [u[j] for u in upd])
    return tuple(outs)
```

```python
import jax
import jax.numpy as jnp
from jax import lax
from jax.experimental import pallas as pl
from jax.experimental.pallas import tpu as pltpu

F32 = jnp.float32
BF16 = jnp.bfloat16

D_MODEL = 1024
POOL_W = 256
ATTN_W = 768
IN_W = 2560
D_FF = 2816
POOL_WINDOWS = (2, 4, 8, 16)
POOL_HALO = 16
DILATIONS = (1, 4, 16)
BLK = 128
LANES = 128
HEAD_DIM = 64
N_GROUPS = ATTN_W // LANES
ROPE_THETA = 10000.0
EPS = 1e-6
NEG = -1e30
N_DEV = 8
SMALL_ROWS = 24

ADAM_LR = 0.001
ADAM_B1 = 0.9
ADAM_B2 = 0.999
ADAM_EPS = 1e-08
ADAM_WD = 0.01
ADAM_STEP = 10

VMEM_LIMIT = 56 * 1024 * 1024


def _dot(a, b):
    return jnp.dot(a, b, preferred_element_type=F32)


def _dot_nt(a, b):
    return lax.dot_general(a, b, (((1,), (1,)), ((), ())), preferred_element_type=F32)


def _dot_tn(a, b):
    return lax.dot_general(a, b, (((0,), (0,)), ((), ())), preferred_element_type=F32)


def _params(n_grid):
    return pltpu.CompilerParams(dimension_semantics=("arbitrary",) * n_grid, vmem_limit_bytes=VMEM_LIMIT)


def _tok(tm, c):
    return pl.BlockSpec((tm, c), lambda i: (i, 0))


def _res(shape):
    return pl.BlockSpec(shape, lambda i: (0,) * len(shape), pipeline_mode=pl.Buffered(1))


ROW_CHUNK = 128


def _row_chunks(tm):
    return [pl.ds(r, ROW_CHUNK) for r in range(0, tm, ROW_CHUNK)]


def _acc(shape):
    return pl.BlockSpec(shape, lambda i: (0,) * len(shape))


def _rms_fwd(x, g):
    r = lax.rsqrt(jnp.mean(x * x, axis=-1, keepdims=True) + EPS)
    return x * r * g


def _rms_bwd(x, g, dy):
    r = lax.rsqrt(jnp.mean(x * x, axis=-1, keepdims=True) + EPS)
    xh = x * r
    gd = dy * g
    dx = r * (gd - xh * jnp.mean(gd * xh, axis=-1, keepdims=True))
    return dx, jnp.sum(dy * xh, axis=0, keepdims=True)


def _rope(x, c, s, sign):
    lane = lax.broadcasted_iota(jnp.int32, (x.shape[0], LANES), 1)
    first = (lane % HEAD_DIM) < (HEAD_DIM // 2)
    outs = []
    for g in range(x.shape[1] // LANES):
        xg = x[:, g * LANES:(g + 1) * LANES]
        rot = jnp.where(first, pltpu.roll(xg, LANES - HEAD_DIM // 2, 1), pltpu.roll(xg, HEAD_DIM // 2, 1))
        outs.append(xg * c + sign * (rot * s))
    return jnp.concatenate(outs, axis=1)


def _proj_fwd(x, g1, w_in_t, cos, sin, tm):
    T = x.shape[0]

    def body(x_ref, g_ref, w_ref, c_ref, s_ref, h_ref, u_ref, q_ref, k_ref, v_ref):
        h = _rms_fwd(x_ref[...], g_ref[...]).astype(BF16)
        h_ref[...] = h
        proj = _dot_nt(h, w_ref[...])
        c = c_ref[...]
        s = s_ref[...]
        u_ref[...] = proj[:, :POOL_W]
        _store_packed(q_ref, _rope(proj[:, POOL_W:POOL_W + ATTN_W], c, s, 1.0))
        _store_packed(k_ref, _rope(proj[:, POOL_W + ATTN_W:POOL_W + 2 * ATTN_W], c, s, 1.0))
        _store_packed(v_ref, proj[:, POOL_W + 2 * ATTN_W:])

    return pl.pallas_call(
        body, name="proj_fwd", grid=(T // tm,),
        in_specs=[_tok(tm, D_MODEL), _res((1, D_MODEL)), _res((IN_W, D_MODEL)), _tok(tm, LANES), _tok(tm, LANES)],
        out_specs=[_tok(tm, D_MODEL), _tok(tm, POOL_W)] + [_tok_packed(tm, ATTN_W)] * 3,
        out_shape=[jax.ShapeDtypeStruct((T, D_MODEL), BF16), jax.ShapeDtypeStruct((T, POOL_W), F32)]
        + [_packed(T, ATTN_W)] * 3,
        compiler_params=_params(1),
    )(x, g1, w_in_t, cos, sin)


def _pool_window(lane):
    return jnp.where(lane < 64, 2, jnp.where(lane < 128, 4, jnp.where(lane < 192, 8, 16)))


def _pool_select(lane, a2, a4, a8, a16):
    return jnp.where(lane < 64, a2, jnp.where(lane < 128, a4, jnp.where(lane < 192, a8, a16)))


def _pool_delta(cur, prev, i, tm):
    prev = jnp.where(i > 0, prev, 0.0)
    ext = jnp.concatenate([prev, cur], axis=0)
    s2 = ext + pltpu.roll(ext, 1, 0)
    s4 = s2 + pltpu.roll(s2, 2, 0)
    s8 = s4 + pltpu.roll(s4, 4, 0)
    s16 = s8 + pltpu.roll(s8, 8, 0)
    lane = lax.broadcasted_iota(jnp.int32, (tm, POOL_W), 1)
    row = lax.broadcasted_iota(jnp.int32, (tm, POOL_W), 0) + i * tm
    ws = _pool_select(lane, s2[POOL_HALO:], s4[POOL_HALO:], s8[POOL_HALO:], s16[POOL_HALO:])
    cnt = jnp.minimum(row + 1, _pool_window(lane)).astype(F32)
    return ws / cnt - cur


def _pool_fwd(u, wbd, scale, tm):
    T = u.shape[0]
    hb = tm // POOL_HALO

    def body(u_ref, prev_ref, w_ref, sc_ref, o_ref):
        d = _pool_delta(u_ref[...], prev_ref[...], pl.program_id(0), tm)
        o_ref[...] = (_dot(d.astype(BF16), w_ref[...]) * sc_ref[...]).astype(BF16)

    return pl.pallas_call(
        body, name="pool_fwd", grid=(T // tm,),
        in_specs=[_tok(tm, POOL_W), pl.BlockSpec((POOL_HALO, POOL_W), lambda i: (jnp.maximum(i * hb - 1, 0), 0)),
                  _res((POOL_W, POOL_W)), _res((1, POOL_W))],
        out_specs=_tok(tm, POOL_W),
        out_shape=jax.ShapeDtypeStruct((T, POOL_W), BF16),
        compiler_params=_params(1),
    )(u, u, wbd, scale)


def _attn_mask(n):
    qi = lax.broadcasted_iota(jnp.int32, (BLK, 2 * BLK), 0)
    kj = lax.broadcasted_iota(jnp.int32, (BLK, 2 * BLK), 1)
    dist = qi + BLK - kj
    return (dist >= 0) & (dist <= BLK) & ((kj >= BLK) | (n > 0))


def _stack_heads(x, lo):
    zero = jnp.zeros_like(x)
    return jnp.concatenate([jnp.where(lo, x, zero), jnp.where(lo, zero, x)], axis=0)


def _head_col(tile, lane, h):
    return jnp.sum(jnp.where(lane == h, tile, 0.0), axis=1, keepdims=True)


def _attn_cols(dil):
    return ATTN_W // 2 if dil >= 16 else ATTN_W


def _attn_specs(dil, nb):
    cw = _attn_cols(dil)
    ch = BLK * dil
    wide = lambda f: pl.BlockSpec((cw // LANES, ch // 2, LANES), f)
    full = pl.BlockSpec((cw // LANES, ch, LANES), lambda n, j, r: (j, n, 0))
    cur = lambda n, j, r: (j, n, 0)
    prv = lambda n, j, r: (j, jnp.maximum(n - 1, 0), 0)
    prv_out = lambda n, j, r: (j, (n + nb - 1) % nb, 0)
    heads = pl.BlockSpec((ch, LANES), lambda n, j, r: (n, 0))
    return cw, wide(cur), wide(prv), wide(prv_out), heads, full


HIGH_HALF = 0xFFFF0000


def _pack(x):
    return pltpu.bitcast(x.astype(BF16), F32)


def _unpack(words):
    return pltpu.bitcast(words, BF16)


def _packed(rows, cols):
    return jax.ShapeDtypeStruct((cols // LANES, rows // 2, LANES), F32)


def _tok_packed(tm, cols):
    return pl.BlockSpec((cols // LANES, tm // 2, LANES), lambda i: (0, i, 0))


def _store_packed(ref, x):
    for g in range(x.shape[1] // LANES):
        ref[g] = _pack(x[:, g * LANES:(g + 1) * LANES])


def _load_packed(ref):
    return jnp.concatenate([_unpack(ref[g]) for g in range(ref.shape[0])], axis=1)


def _load_streams(ref, dil, r2, sl):
    if dil == 1:
        return [_unpack(ref[sl])]
    words = lax.bitcast_convert_type(ref.at[sl][pl.ds(r2, BLK, stride=dil // 2), :], jnp.uint32)
    even = lax.bitcast_convert_type(words << 16, F32).astype(BF16)
    odd = lax.bitcast_convert_type(words & jnp.uint32(HIGH_HALF), F32).astype(BF16)
    return [even, odd]


def _load_streams_f32(ref, dil, r2, sl):
    ref = ref if sl is None else ref.at[sl]
    if dil == 1:
        return [ref[...]]
    return [ref[pl.ds(2 * r2 + e, BLK, stride=dil), :] for e in range(2)]


def _store_streams_f32(ref, dil, r2, sl, tiles):
    ref = ref if sl is None else ref.at[sl]
    if dil == 1:
        ref[...] = tiles[0]
    else:
        for e, t in enumerate(tiles):
            ref[pl.ds(2 * r2 + e, BLK, stride=dil), :] = t


def _store_streams(ref, dil, r2, sl, tiles):
    if dil == 1:
        ref[sl] = _pack(tiles[0])
    else:
        even, odd = [lax.bitcast_convert_type(t.astype(BF16).astype(F32), jnp.uint32) for t in tiles]
        words = (odd & jnp.uint32(HIGH_HALF)) | (even >> 16)
        ref.at[sl][pl.ds(r2, BLK, stride=dil // 2), :] = lax.bitcast_convert_type(words, F32)


def _attn_fwd(q, k, v, dil, prev, last):
    T = 2 * q.shape[1]
    nb = T // (BLK * dil)
    first = prev is None
    cw, cur, prv, _, heads, full = _attn_specs(dil, nb)
    ncb = ATTN_W // cw
    heads_per_step = cw // HEAD_DIM
    n_str = min(dil, 2)
    everything = None

    def body(*refs):
        if first:
            q_ref, kc_ref, kp_ref, vc_ref, vp_ref, acc_ref, lse_ref = refs
        else:
            q_ref, kc_ref, kp_ref, vc_ref, vp_ref, acc_in, lse_in, acc_ref, lse_ref = refs
        j = pl.program_id(1)
        r2 = pl.program_id(2)
        valid = _attn_mask(pl.program_id(0))
        lane = lax.broadcasted_iota(jnp.int32, (BLK, LANES), 1)
        lo = lane < HEAD_DIM
        lse_tiles = [jnp.zeros((BLK, LANES), F32) for _ in range(n_str)]
        store_acc = _store_streams if last else _store_streams_f32
        own = []
        for g in range(cw // LANES):
            qs, kcs, kps, vcs, vps = [_load_streams(r, dil, r2, g) for r in (q_ref, kc_ref, kp_ref, vc_ref, vp_ref)]
            pairs = []
            for e in range(n_str):
                qg = qs[e] * 0.125
                kcat = jnp.concatenate([kps[e], kcs[e]], axis=0)
                vcat = jnp.concatenate([vps[e], vcs[e]], axis=0)
                pair = None
                for hh in range(2):
                    h = j * heads_per_step + 2 * g + hh
                    hm = lo if hh == 0 else jnp.logical_not(lo)
                    s = _dot_nt(jnp.where(hm, qg, jnp.zeros_like(qg)), kcat)
                    s = jnp.where(valid, s, NEG)
                    m = jnp.max(s, axis=1, keepdims=True)
                    p = jnp.exp(s - m)
                    den = jnp.sum(p, axis=1, keepdims=True)
                    o = _dot(p.astype(BF16), vcat) / den
                    pair = o if hh == 0 else jnp.where(lo, pair, o)
                    lse_tiles[e] = jnp.where(lane == h, m + jnp.log(den), lse_tiles[e])
                pairs.append(pair)
            if first:
                store_acc(acc_ref, dil, r2, g, pairs)
            else:
                own.append(pairs)
        if not first:
            mine = (lane >= j * heads_per_step) & (lane < (j + 1) * heads_per_step)
            before = _load_streams_f32(lse_in, dil, r2, everything)
            w_before, w_own = [], []
            for e in range(n_str):
                mx = jnp.maximum(before[e], lse_tiles[e])
                total = mx + jnp.log(jnp.exp(before[e] - mx) + jnp.exp(lse_tiles[e] - mx))
                w_before.append(jnp.exp(before[e] - total))
                w_own.append(jnp.exp(lse_tiles[e] - total))
                lse_tiles[e] = jnp.where(mine, total, 0.0)
            for g in range(cw // LANES):
                h0 = j * heads_per_step + 2 * g
                spread = lambda w: jnp.where(lo, _head_col(w, lane, h0), _head_col(w, lane, h0 + 1))
                olds = _load_streams_f32(acc_in, dil, r2, g)
                store_acc(acc_ref, dil, r2, g, [olds[e] * spread(w_before[e]) + own[g][e] * spread(w_own[e])
                                                for e in range(n_str)])
        if ncb == 1:
            _store_streams_f32(lse_ref, dil, r2, everything, lse_tiles)
        else:
            @pl.when(j == 0)
            def _():
                _store_streams_f32(lse_ref, dil, r2, everything, lse_tiles)

            @pl.when(j > 0)
            def _():
                before = _load_streams_f32(lse_ref, dil, r2, everything)
                _store_streams_f32(lse_ref, dil, r2, everything, [a + b for a, b in zip(before, lse_tiles)])

    ins = [q, k, k, v, v]
    in_specs = [cur, cur, prv, cur, prv]
    if not first:
        ins += [prev[0], prev[1]]
        in_specs += [full, heads]
    return pl.pallas_call(
        body, name=f"attn_fwd_d{dil}", grid=(nb, ncb, max(dil // 2, 1)),
        in_specs=in_specs, out_specs=[cur if last else full, heads],
        out_shape=[_packed(T, ATTN_W) if last else jax.ShapeDtypeStruct((N_GROUPS, T, LANES), F32),
                   jax.ShapeDtypeStruct((T, LANES), F32)],
        compiler_params=_params(3),
    )(*ins)


def _mix_fwd(pool, attn, x, w_out, g2, g3, tm):
    T = x.shape[0]

    def body(p_ref, a_ref, x_ref, w_ref, g2_ref, g3_ref, cat_ref, mix_ref, x2_ref, h2_ref):
        p = p_ref[...]
        a = _load_packed(a_ref)
        cat_ref[...] = jnp.concatenate([p, a], axis=1)
        mix = _dot(p, w_ref[:POOL_W, :]) + _dot(a, w_ref[POOL_W:, :])
        mix_ref[...] = mix
        x2 = x_ref[...] + _rms_fwd(mix, g2_ref[...])
        x2_ref[...] = x2
        h2_ref[...] = _rms_fwd(x2, g3_ref[...]).astype(BF16)

    return pl.pallas_call(
        body, name="mix_fwd", grid=(T // tm,),
        in_specs=[_tok(tm, POOL_W), _tok_packed(tm, ATTN_W), _tok(tm, D_MODEL), _res((D_MODEL, D_MODEL)),
                  _res((1, D_MODEL)), _res((1, D_MODEL))],
        out_specs=[_tok(tm, D_MODEL)] * 4,
        out_shape=[jax.ShapeDtypeStruct((T, D_MODEL), BF16), jax.ShapeDtypeStruct((T, D_MODEL), F32),
                   jax.ShapeDtypeStruct((T, D_MODEL), F32), jax.ShapeDtypeStruct((T, D_MODEL), BF16)],
        compiler_params=_params(1),
    )(pool, attn, x, w_out, g2, g3)


def _ffn_up(h2, wg_t, wu_t, tm):
    T = h2.shape[0]

    def body(h_ref, wg_ref, wu_ref, dg_ref, du_ref, a_ref):
        h = h_ref[...]
        gate = _dot_nt(h, wg_ref[...])
        up = _dot_nt(h, wu_ref[...])
        sg = 1.0 / (1.0 + jnp.exp(-gate))
        silu = gate * sg
        a_ref[...] = (silu * up).astype(BF16)
        dg_ref[...] = (up * (sg * (1.0 + gate * (1.0 - sg)))).astype(BF16)
        du_ref[...] = silu.astype(BF16)

    return pl.pallas_call(
        body, name="ffn_up", grid=(T // tm,),
        in_specs=[_tok(tm, D_MODEL), _res((D_FF, D_MODEL)), _res((D_FF, D_MODEL))],
        out_specs=[_tok(tm, D_FF)] * 3,
        out_shape=[jax.ShapeDtypeStruct((T, D_FF), BF16)] * 3,
        compiler_params=_params(1),
    )(h2, wg_t, wu_t)


def _ffn_down_loss(act, w_down, x2, g4, tgt, tm):
    T = act.shape[0]

    def body(a_ref, w_ref, x2_ref, g_ref, t_ref, df_ref, dy_ref, dg_ref, loss_ref):
        i = pl.program_id(0)

        @pl.when(i == 0)
        def _():
            dg_ref[...] = jnp.zeros_like(dg_ref)
            loss_ref[...] = jnp.zeros_like(loss_ref)

        g = g_ref[...]
        for rows in _row_chunks(tm):
            f = _dot(a_ref[rows, :], w_ref[...])
            err = x2_ref[rows, :] + _rms_fwd(f, g) - t_ref[rows, :]
            loss_ref[...] += 0.5 * jnp.sum(jnp.mean(err * err, axis=-1, keepdims=True), axis=0, keepdims=True)
            dy = err * (1.0 / D_MODEL)
            dy_ref[rows, :] = dy
            df, dg = _rms_bwd(f, g, dy)
            dg_ref[...] += dg
            df_ref[rows, :] = df.astype(BF16)

    return pl.pallas_call(
        body, name="ffn_down_loss", grid=(T // tm,),
        in_specs=[_tok(tm, D_FF), _res((D_FF, D_MODEL)), _tok(tm, D_MODEL), _res((1, D_MODEL)), _tok(tm, D_MODEL)],
        out_specs=[_tok(tm, D_MODEL), _tok(tm, D_MODEL), _acc((1, D_MODEL)), _acc((1, 1))],
        out_shape=[jax.ShapeDtypeStruct((T, D_MODEL), BF16), jax.ShapeDtypeStruct((T, D_MODEL), F32),
                   jax.ShapeDtypeStruct((1, D_MODEL), F32), jax.ShapeDtypeStruct((1, 1), F32)],
        compiler_params=_params(1),
    )(act, w_down, x2, g4, tgt)


def _ffn_act_bwd(df, w_down, act_dgate, act_dup, tm):
    T = df.shape[0]

    def body(df_ref, w_ref, ag_ref, au_ref, dg_ref, du_ref):
        dact = _dot_nt(df_ref[...], w_ref[...])
        dg_ref[...] = (dact * ag_ref[...].astype(F32)).astype(BF16)
        du_ref[...] = (dact * au_ref[...].astype(F32)).astype(BF16)

    return pl.pallas_call(
        body, name="ffn_act_bwd", grid=(T // tm,),
        in_specs=[_tok(tm, D_MODEL), _res((D_FF, D_MODEL)), _tok(tm, D_FF), _tok(tm, D_FF)],
        out_specs=[_tok(tm, D_FF)] * 2,
        out_shape=[jax.ShapeDtypeStruct((T, D_FF), BF16)] * 2,
        compiler_params=_params(1),
    )(df, w_down, act_dgate, act_dup)


def _ffn_in_bwd(dgate, dup, wg_t, wu_t, x2, mix, dy, g3, g2, tm):
    T = x2.shape[0]

    def body(dg_ref, du_ref, wg_ref, wu_ref, x2_ref, mix_ref, dy_ref, g3_ref, g2_ref,
             dx2_ref, dmix_ref, dg3_ref, dg2_ref):
        @pl.when(pl.program_id(0) == 0)
        def _():
            dg3_ref[...] = jnp.zeros_like(dg3_ref)
            dg2_ref[...] = jnp.zeros_like(dg2_ref)

        for rows in _row_chunks(tm):
            dh2 = _dot(dg_ref[rows, :], wg_ref[...]) + _dot(du_ref[rows, :], wu_ref[...])
            dn, dg3 = _rms_bwd(x2_ref[rows, :], g3_ref[...], dh2)
            dx2 = dy_ref[rows, :] + dn
            dx2_ref[rows, :] = dx2
            dg3_ref[...] += dg3
            dmix, dg2 = _rms_bwd(mix_ref[rows, :], g2_ref[...], dx2)
            dg2_ref[...] += dg2
            dmix_ref[rows, :] = dmix.astype(BF16)

    return pl.pallas_call(
        body, name="ffn_in_bwd", grid=(T // tm,),
        in_specs=[_tok(tm, D_FF), _tok(tm, D_FF), _res((D_FF, D_MODEL)), _res((D_FF, D_MODEL)),
                  _tok(tm, D_MODEL), _tok(tm, D_MODEL), _tok(tm, D_MODEL), _res((1, D_MODEL)), _res((1, D_MODEL))],
        out_specs=[_tok(tm, D_MODEL), _tok(tm, D_MODEL), _acc((1, D_MODEL)), _acc((1, D_MODEL))],
        out_shape=[jax.ShapeDtypeStruct((T, D_MODEL), F32), jax.ShapeDtypeStruct((T, D_MODEL), BF16),
                   jax.ShapeDtypeStruct((1, D_MODEL), F32), jax.ShapeDtypeStruct((1, D_MODEL), F32)],
        compiler_params=_params(1),
    )(dgate, dup, wg_t, wu_t, x2, mix, dy, g3, g2)


def _mix_bwd(dmix, w_out, tm):
    T = dmix.shape[0]

    def body(d_ref, w_ref, dp_ref, da_ref):
        dcat = _dot_nt(d_ref[...], w_ref[...])
        dp_ref[...] = dcat[:, :POOL_W].astype(BF16)
        _store_packed(da_ref, dcat[:, POOL_W:])

    return pl.pallas_call(
        body, name="mix_bwd", grid=(T // tm,),
        in_specs=[_tok(tm, D_MODEL), _res((D_MODEL, D_MODEL))],
        out_specs=[_tok(tm, POOL_W), _tok_packed(tm, ATTN_W)],
        out_shape=[jax.ShapeDtypeStruct((T, POOL_W), BF16), _packed(T, ATTN_W)],
        compiler_params=_params(1),
    )(dmix, w_out)


def _attn_bwd(q, k, v, dout, out, lse, dil):
    T = 2 * q.shape[1]
    nb = T // (BLK * dil)
    cw, cur, prv, prv_out, heads, _ = _attn_specs(dil, nb)
    ncb = ATTN_W // cw
    heads_per_step = cw // HEAD_DIM
    n_str = min(dil, 2)

    def body(q_ref, kc_ref, kp_ref, vc_ref, vp_ref, do_ref, o_ref, lse_ref,
             dq_ref, dkc_ref, dkp_ref, dvc_ref, dvp_ref):
        j = pl.program_id(1)
        r2 = pl.program_id(2)
        valid = _attn_mask(pl.program_id(0))
        lane = lax.broadcasted_iota(jnp.int32, (BLK, LANES), 1)
        lo = lane < HEAD_DIM
        valid2 = jnp.concatenate([valid, valid], axis=0)
        lse_tiles = _load_streams_f32(lse_ref, dil, r2, None)
        for g in range(cw // LANES):
            sl = g
            qs, kcs, kps, vcs, vps, dos, os_ = [
                _load_streams(r, dil, r2, sl) for r in (q_ref, kc_ref, kp_ref, vc_ref, vp_ref, do_ref, o_ref)]
            dqs, dks, dvs = [], [], []
            for e in range(n_str):
                qg = qs[e] * 0.125
                dog = dos[e]
                kcat = jnp.concatenate([kps[e], kcs[e]], axis=0)
                vcat = jnp.concatenate([vps[e], vcs[e]], axis=0)
                prod = dog.astype(F32) * os_[e].astype(F32)
                h0 = j * heads_per_step + 2 * g
                q2 = _stack_heads(qg, lo)
                do2 = _stack_heads(dog, lo)
                lse2 = jnp.concatenate([_head_col(lse_tiles[e], lane, h0), _head_col(lse_tiles[e], lane, h0 + 1)], axis=0)
                dsum2 = jnp.concatenate([jnp.sum(jnp.where(lo, prod, 0.0), axis=1, keepdims=True),
                                         jnp.sum(jnp.where(lo, 0.0, prod), axis=1, keepdims=True)], axis=0)
                p = jnp.exp(jnp.where(valid2, _dot_nt(q2, kcat), NEG) - lse2)
                ds = (p * (_dot_nt(do2, vcat) - dsum2)).astype(BF16)
                dvs.append(_dot_tn(p.astype(BF16), do2))
                dks.append(_dot_tn(ds, q2))
                dq2 = _dot(ds, kcat) * 0.125
                dqs.append(jnp.where(lo, dq2[:BLK], dq2[BLK:]))
            _store_streams(dq_ref, dil, r2, sl, dqs)
            _store_streams(dkp_ref, dil, r2, sl, [t[:BLK] for t in dks])
            _store_streams(dkc_ref, dil, r2, sl, [t[BLK:] for t in dks])
            _store_streams(dvp_ref, dil, r2, sl, [t[:BLK] for t in dvs])
            _store_streams(dvc_ref, dil, r2, sl, [t[BLK:] for t in dvs])

    return pl.pallas_call(
        body, name=f"attn_bwd_d{dil}", grid=(nb, ncb, max(dil // 2, 1)),
        in_specs=[cur, cur, prv, cur, prv, cur, cur, heads],
        out_specs=[cur, cur, prv_out, cur, prv_out],
        out_shape=[_packed(T, ATTN_W)] * 5,
        compiler_params=_params(3),
    )(q, k, k, v, v, dout, out, lse)


def _pool_bwd(u, dy, wbd, scale, tm):
    T = u.shape[0]
    nt = T // tm
    hb = tm // POOL_HALO

    def body(u_ref, prev_ref, dy_ref, next_ref, w_ref, sc_ref, du_ref, dw_ref, dsc_ref):
        i = pl.program_id(0)

        @pl.when(i == 0)
        def _():
            dw_ref[...] = jnp.zeros_like(dw_ref)
            dsc_ref[...] = jnp.zeros_like(dsc_ref)

        w = w_ref[...]
        sc = sc_ref[...]
        d = _pool_delta(u_ref[...], prev_ref[...], i, tm).astype(BF16)
        dyc = dy_ref[...].astype(F32)
        dsc_ref[...] += jnp.sum(dyc * _dot(d, w), axis=0, keepdims=True)
        nxt = jnp.where(i < nt - 1, next_ref[...].astype(F32), 0.0)
        dypre = (jnp.concatenate([dyc, nxt], axis=0) * sc).astype(BF16)
        dw_ref[...] += _dot_tn(d, dypre[:tm])
        dd = _dot_nt(dypre, w)
        n = tm + POOL_HALO
        lane = lax.broadcasted_iota(jnp.int32, (n, POOL_W), 1)
        row = lax.broadcasted_iota(jnp.int32, (n, POOL_W), 0) + i * tm
        gx = dd / jnp.minimum(row + 1, _pool_window(lane)).astype(F32)
        a2 = gx + pltpu.roll(gx, n - 1, 0)
        a4 = a2 + pltpu.roll(a2, n - 2, 0)
        a8 = a4 + pltpu.roll(a4, n - 4, 0)
        a16 = a8 + pltpu.roll(a8, n - 8, 0)
        fs = _pool_select(lane[:tm], a2[:tm], a4[:tm], a8[:tm], a16[:tm])
        du_ref[...] = (fs - dd[:tm]).astype(BF16)

    return pl.pallas_call(
        body, name="pool_bwd", grid=(nt,),
        in_specs=[_tok(tm, POOL_W), pl.BlockSpec((POOL_HALO, POOL_W), lambda i: (jnp.maximum(i * hb - 1, 0), 0)),
                  _tok(tm, POOL_W), pl.BlockSpec((POOL_HALO, POOL_W), lambda i: (jnp.minimum((i + 1) * hb, nt * hb - 1), 0)),
                  _res((POOL_W, POOL_W)), _res((1, POOL_W))],
        out_specs=[_tok(tm, POOL_W), _acc((POOL_W, POOL_W)), _acc((1, POOL_W))],
        out_shape=[jax.ShapeDtypeStruct((T, POOL_W), BF16), jax.ShapeDtypeStruct((POOL_W, POOL_W), F32),
                   jax.ShapeDtypeStruct((1, POOL_W), F32)],
        compiler_params=_params(1),
    )(u, u, dy, dy, wbd, scale)


def _dproj_combine(du, dqs, dkcs, dkps, dvcs, dvps, cos, sin, tm):
    T = du.shape[0]
    n_cfg = len(dqs)

    def body(*refs):
        du_ref = refs[0]
        groups = [refs[1 + j * n_cfg:1 + (j + 1) * n_cfg] for j in range(5)]
        c_ref, s_ref, out_ref = refs[1 + 5 * n_cfg:]
        tot = lambda rs: sum(_load_packed(r).astype(F32) for r in rs)
        c = c_ref[...]
        s = s_ref[...]
        dq = _rope(tot(groups[0]), c, s, -1.0)
        dk = _rope(tot(groups[1]) + tot(groups[2]), c, s, -1.0)
        dv = tot(groups[3]) + tot(groups[4])
        out_ref[...] = jnp.concatenate([du_ref[...], dq.astype(BF16), dk.astype(BF16), dv.astype(BF16)], axis=1)

    return pl.pallas_call(
        body, name="dproj_combine", grid=(T // tm,),
        in_specs=[_tok(tm, POOL_W)] + [_tok_packed(tm, ATTN_W)] * (5 * n_cfg) + [_tok(tm, LANES)] * 2,
        out_specs=_tok(tm, IN_W),
        out_shape=jax.ShapeDtypeStruct((T, IN_W), BF16),
        compiler_params=_params(1),
    )(du, *dqs, *dkcs, *dkps, *dvcs, *dvps, cos, sin)


def _proj_bwd(dproj, w_in_t, x, dx2, g1, tm):
    T = x.shape[0]

    def body(d_ref, w_ref, x_ref, r_ref, g_ref, dx_ref, dg_ref):
        @pl.when(pl.program_id(0) == 0)
        def _():
            dg_ref[...] = jnp.zeros_like(dg_ref)

        for rows in _row_chunks(tm):
            dn, dg = _rms_bwd(x_ref[rows, :], g_ref[...], _dot(d_ref[rows, :], w_ref[...]))
            dg_ref[...] += dg
            dx_ref[rows, :] = r_ref[rows, :] + dn

    return pl.pallas_call(
        body, name="proj_bwd", grid=(T // tm,),
        in_specs=[_tok(tm, IN_W), _res((IN_W, D_MODEL)), _tok(tm, D_MODEL), _tok(tm, D_MODEL), _res((1, D_MODEL))],
        out_specs=[_tok(tm, D_MODEL), _acc((1, D_MODEL))],
        out_shape=[jax.ShapeDtypeStruct((T, D_MODEL), F32), jax.ShapeDtypeStruct((1, D_MODEL), F32)],
        compiler_params=_params(1),
    )(dproj, w_in_t, x, dx2, g1)


def _wgrad(a, b, name, tile_m, tk):
    T, M = a.shape
    N = b.shape[1]
    nk = T // tk

    def body(a_ref, b_ref, o_ref, acc_ref):
        kk = pl.program_id(1)

        @pl.when(kk == 0)
        def _():
            acc_ref[...] = jnp.zeros_like(acc_ref)

        acc_ref[...] += _dot_tn(a_ref[...], b_ref[...])

        @pl.when(kk == nk - 1)
        def _():
            o_ref[...] = acc_ref[...].astype(BF16)

    return pl.pallas_call(
        body, name=name, grid=(M // tile_m, nk),
        in_specs=[pl.BlockSpec((tk, tile_m), lambda j, kk: (kk, j)), pl.BlockSpec((tk, N), lambda j, kk: (kk, 0))],
        out_specs=pl.BlockSpec((tile_m, N), lambda j, kk: (j, 0)),
        out_shape=jax.ShapeDtypeStruct((M, N), BF16),
        scratch_shapes=[pltpu.VMEM((tile_m, N), F32)],
        compiler_params=_params(2),
    )(a, b)


def _exchange(arrs, scatter, name):
    n = len(arrs)
    out_shapes = [jax.ShapeDtypeStruct((N_DEV,) + (a.shape[1:] if sc else a.shape), a.dtype)
                  for a, sc in zip(arrs, scatter)]

    def body(*refs):
        ins, outs = refs[:n], refs[n:2 * n]
        send_sems, recv_sems, loc_sems = refs[2 * n:]
        x, y, c = lax.axis_index("x"), lax.axis_index("y"), lax.axis_index("c")
        me = 4 * x + 2 * y + c
        local, sends, recvs = [], [], []
        for i in range(n):
            own = ins[i].at[me] if scatter[i] else ins[i]
            loc = pltpu.make_async_copy(own, outs[i].at[me], loc_sems.at[i])
            loc.start()
            local.append(loc)
            for kbits in range(1, N_DEV):
                px = 1 - x if kbits & 4 else x
                py = 1 - y if kbits & 2 else y
                pc = 1 - c if kbits & 1 else c
                pid = 4 * px + 2 * py + pc
                src = ins[i].at[pid] if scatter[i] else ins[i]
                cp = pltpu.make_async_remote_copy(
                    src_ref=src, dst_ref=outs[i].at[me],
                    send_sem=send_sems.at[i, kbits - 1], recv_sem=recv_sems.at[i, kbits - 1],
                    device_id=(px, py, pc), device_id_type=pl.DeviceIdType.MESH)
                cp.start()
                sends.append(cp)
                recvs.append(pltpu.make_async_remote_copy(
                    src_ref=src, dst_ref=outs[i].at[pid],
                    send_sem=send_sems.at[i, kbits - 1], recv_sem=recv_sems.at[i, kbits - 1],
                    device_id=(px, py, pc), device_id_type=pl.DeviceIdType.MESH))
        for cp in recvs:
            cp.wait_recv()
        for cp in sends:
            cp.wait_send()
        for cp in local:
            cp.wait()

    hbm = pl.BlockSpec(memory_space=pl.ANY)
    return pl.pallas_call(
        body, name=name, in_specs=[hbm] * n, out_specs=[hbm] * n, out_shape=out_shapes,
        scratch_shapes=[pltpu.SemaphoreType.DMA((n, N_DEV - 1)), pltpu.SemaphoreType.DMA((n, N_DEV - 1)),
                        pltpu.SemaphoreType.DMA((n,))],
    )(*arrs)


def _gather_two_level(arr, name):
    def body(x_ref, out_ref, send_sems, recv_sems, local_sem):
        x, y, c = lax.axis_index("x"), lax.axis_index("y"), lax.axis_index("c")
        me, sibling = (x, y, c), (x, y, 1 - c)
        chips = [(1 - x, y), (x, 1 - y), (1 - x, 1 - y)]
        slot = lambda px, py, pc: out_ref.at[4 * px + 2 * py + pc]

        def copy(k, block, to, src=None):
            return pltpu.make_async_remote_copy(
                src_ref=slot(*block) if src is None else src, dst_ref=slot(*block),
                send_sem=send_sems.at[k], recv_sem=recv_sems.at[k],
                device_id=to, device_id_type=pl.DeviceIdType.MESH)

        mine = pltpu.make_async_copy(x_ref, slot(*me), local_sem)
        mine.start()
        first = [copy(0, me, sibling, src=x_ref)]
        first += [copy(1 + i, me, (*chip, c), src=x_ref) for i, chip in enumerate(chips)]
        for cp in first:
            cp.start()
        passed = [copy(4 + i, (*chip, c), sibling) for i, chip in enumerate(chips)]
        for i, chip in enumerate(chips):
            copy(1 + i, (*chip, c), me).wait_recv()
            passed[i].start()
        copy(0, sibling, me).wait_recv()
        for i, chip in enumerate(chips):
            copy(4 + i, (*chip, 1 - c), me).wait_recv()
        for cp in first + passed:
            cp.wait_send()
        mine.wait()

    hbm = pl.BlockSpec(memory_space=pl.ANY)
    return pl.pallas_call(
        body, name=name, in_specs=[hbm], out_specs=hbm,
        out_shape=jax.ShapeDtypeStruct((N_DEV,) + arr.shape, arr.dtype),
        scratch_shapes=[pltpu.SemaphoreType.DMA((N_DEV - 1,)), pltpu.SemaphoreType.DMA((N_DEV - 1,)),
                        pltpu.SemaphoreType.DMA],
    )(arr)


def _peers(x, y, c):
    for kbits in range(1, N_DEV):
        px = 1 - x if kbits & 4 else x
        py = 1 - y if kbits & 2 else y
        pc = 1 - c if kbits & 1 else c
        yield kbits - 1, (px, py, pc), 4 * px + 2 * py + pc


def _peer_copies(ins, lands, scatter, send_sems, recv_sems, incoming):
    x, y, c = lax.axis_index("x"), lax.axis_index("y"), lax.axis_index("c")
    me = 4 * x + 2 * y + c
    copies = []
    for i in range(len(ins)):
        for k, peer, pid in _peers(x, y, c):
            slot = i * (N_DEV - 1) + k
            copies.append(pltpu.make_async_remote_copy(
                src_ref=ins[i].at[pid] if scatter[i] else ins[i], dst_ref=lands[i].at[pid if incoming else me],
                send_sem=send_sems.at[slot], recv_sem=recv_sems.at[slot],
                device_id=peer, device_id_type=pl.DeviceIdType.MESH))
    return copies


_HBM = pl.BlockSpec(memory_space=pltpu.HBM)
_SEM = pl.BlockSpec(memory_space=pltpu.SEMAPHORE)
_DATAFLOW = pltpu.SideEffectType.DATAFLOW_SIDE_EFFECTING


def _exchange_start(arrs, scatter, after, name):
    n = len(arrs)
    lands = [lax.empty((N_DEV,) + (a.shape[1:] if sc else a.shape), a.dtype) for a, sc in zip(arrs, scatter)]

    def body(*refs):
        ins, lz = refs[:n], refs[n:2 * n]
        send_sems, recv_sems = refs[2 * n + 1:2 * n + 3]
        token = refs[-1]
        for cp in _peer_copies(ins, lz, scatter, send_sems, recv_sems, False):
            cp.start()
        token[...] = jnp.zeros_like(token)

    sem_shape = pltpu.SemaphoreType.DMA((n * (N_DEV - 1),))
    outs = pl.pallas_call(
        body, name=name,
        out_shape=(sem_shape, sem_shape, *[pltpu.HBM(a.shape, a.dtype) for a in arrs + lands],
                   jax.ShapeDtypeStruct((8, LANES), F32)),
        in_specs=[_HBM] * (2 * n) + [pl.BlockSpec(memory_space=pl.ANY)],
        out_specs=(_SEM, _SEM, *[_HBM] * (2 * n), pl.BlockSpec(memory_space=pltpu.VMEM)),
        input_output_aliases={i: 2 + i for i in range(2 * n)},
        compiler_params=pltpu.CompilerParams(has_side_effects=_DATAFLOW),
    )(*[pltpu.with_memory_space_constraint(a, pltpu.HBM) for a in arrs + lands], after)
    return outs[0], outs[1], list(outs[2:2 + n]), list(outs[2 + n:2 + 2 * n]), outs[-1]


def _exchange_wait(handle, scatter, after, name):
    send_sems, recv_sems, srcs, lands, _ = handle
    n = len(srcs)

    def body(*refs):
        ins, lz = refs[:n], refs[n:2 * n]
        for cp in _peer_copies(ins, lz, scatter, refs[2 * n], refs[2 * n + 1], False):
            cp.wait_send()
        for cp in _peer_copies(ins, lz, scatter, refs[2 * n], refs[2 * n + 1], True):
            cp.wait_recv()

    outs = pl.pallas_call(
        body, name=name,
        out_shape=[pltpu.HBM(a.shape, a.dtype) for a in srcs + lands],
        in_specs=[_HBM] * (2 * n) + [_SEM, _SEM, pl.BlockSpec(memory_space=pl.ANY)],
        out_specs=[_HBM] * (2 * n),
        input_output_aliases={i: i for i in range(2 * n)},
        compiler_params=pltpu.CompilerParams(has_side_effects=_DATAFLOW),
    )(*srcs, *lands, send_sems, recv_sems, after)
    return list(outs[:n]), list(outs[n:])


def _fill_own(lands, srcs, scatter):
    me = 4 * lax.axis_index("x") + 2 * lax.axis_index("y") + lax.axis_index("c")
    own = [lax.dynamic_index_in_dim(s, me, 0, keepdims=False) if sc else s for s, sc in zip(srcs, scatter)]
    return [lax.dynamic_update_index_in_dim(land, o, me, 0) for land, o in zip(lands, own)]


def _slot_sum(parts, name, tr):
    _, R, C = parts.shape

    def body(p_ref, o_ref):
        acc = p_ref[0].astype(F32)
        for s in range(1, N_DEV):
            acc = acc + p_ref[s].astype(F32)
        o_ref[...] = acc

    return pl.pallas_call(
        body, name=name, grid=(R // tr,),
        in_specs=[pl.BlockSpec((N_DEV, tr, C), lambda i: (0, i, 0))],
        out_specs=pl.BlockSpec((tr, C), lambda i: (i, 0)),
        out_shape=jax.ShapeDtypeStruct((R, C), F32),
        compiler_params=_params(1),
    )(parts)


def _adamw(w, g, m, v, name):
    def body(w_ref, g_ref, m_ref, v_ref, d_ref, nm_ref, nv_ref):
        g = g_ref[...]
        nm = ADAM_B1 * m_ref[...] + (1.0 - ADAM_B1) * g
        nv = ADAM_B2 * v_ref[...] + (1.0 - ADAM_B2) * jnp.square(g)
        m_hat = nm / (1.0 - ADAM_B1 ** ADAM_STEP)
        v_hat = nv / (1.0 - ADAM_B2 ** ADAM_STEP)
        d_ref[...] = -ADAM_LR * (m_hat / (jnp.sqrt(v_hat) + ADAM_EPS) + ADAM_WD * w_ref[...])
        nm_ref[...] = nm
        nv_ref[...] = nv

    return pl.pallas_call(
        body, name=name, out_shape=[jax.ShapeDtypeStruct(w.shape, F32)] * 3,
        compiler_params=pltpu.CompilerParams(vmem_limit_bytes=VMEM_LIMIT),
    )(w, g, m, v)


def _rope_tables(T):
    half = HEAD_DIM // 2
    freqs = ROPE_THETA ** (-jnp.arange(half, dtype=F32) * (2.0 / HEAD_DIM))
    ang = jnp.arange(T).astype(F32)[:, None] * jnp.tile(freqs, LANES // half)[None, :]
    sign = jnp.tile(jnp.concatenate([-jnp.ones((half,), F32), jnp.ones((half,), F32)]), LANES // HEAD_DIM)
    return jnp.cos(ang), jnp.sin(ang) * sign[None, :]


def _block_diag(w_pool):
    wbd = jnp.zeros((POOL_W, POOL_W), F32)
    g = POOL_W // len(POOL_WINDOWS)
    for i in range(len(POOL_WINDOWS)):
        wbd = wbd.at[i * g:(i + 1) * g, i * g:(i + 1) * g].set(w_pool[i])
    return wbd


def _pack_small(g1, w_pool, pool_scale, g2, g3, g4, extra):
    pad = lambda a: jnp.pad(a.reshape(1, -1), ((0, 0), (0, D_MODEL - a.size)))
    rows = [g1.reshape(1, -1), g2.reshape(1, -1), g3.reshape(1, -1), g4.reshape(1, -1),
            w_pool.reshape(-1, D_MODEL), pad(pool_scale), pad(extra)]
    buf = jnp.concatenate(rows, axis=0)
    return jnp.pad(buf, ((0, SMALL_ROWS - buf.shape[0]), (0, 0)))


def _unpack_small(buf):
    n_pool = len(POOL_WINDOWS) * (POOL_W // len(POOL_WINDOWS)) ** 2 // D_MODEL
    g = POOL_W // len(POOL_WINDOWS)
    return (buf[0:1], buf[4:4 + n_pool].reshape(1, len(POOL_WINDOWS), g, g), buf[4 + n_pool:5 + n_pool, :POOL_W],
            buf[1:2], buf[2:3], buf[3:4], buf[5 + n_pool])


class _LocalStep:
    def __init__(self, x, tgt, g1, w_pool, pool_scale, g2, g3, g4):
        self.x, self.tgt, self.pool_scale = x, tgt, pool_scale
        self.g1, self.g2, self.g3, self.g4 = g1, g2, g3, g4
        self.cos, self.sin = _rope_tables(x.shape[0])
        self.wbd = _block_diag(w_pool).astype(BF16)

    def mixer_fwd(self, w_in_t, token):
        self.w_in_t = w_in_t
        self.h1, self.u, self.q, self.k, self.v = _proj_fwd(
            self.x, self.g1 + token[0, 0], w_in_t, self.cos, self.sin, 512)
        self.pool = _pool_fwd(self.u, self.wbd, self.pool_scale, 512)
        prev = None
        for j, dil in enumerate(DILATIONS):
            prev = _attn_fwd(self.q, self.k, self.v, dil, prev, j == len(DILATIONS) - 1)
        self.attn, self.lse = prev
        return self.attn

    def ffn_fwd_bwd(self, w_out, wg_t, wu_t, w_down):
        self.w_out, self.wg_t, self.wu_t = w_out, wg_t, wu_t
        self.cat, self.mix, self.x2, h2 = _mix_fwd(self.pool, self.attn, self.x, w_out, self.g2, self.g3, 512)
        act_dgate, act_dup, act = _ffn_up(h2, wg_t, wu_t, 512)
        df, self.dy, self.dg4, self.loss = _ffn_down_loss(act, w_down, self.x2, self.g4, self.tgt, 1024)
        self.dgate, self.dup = _ffn_act_bwd(df, w_down, act_dgate, act_dup, 512)
        return (_wgrad(self.dgate, h2, "wgrad_gate", D_FF // 2, 1024), _wgrad(self.dup, h2, "wgrad_up", D_FF // 2, 1024),
                _wgrad(act, df, "wgrad_down", D_FF // 2, 1024))

    def mixer_bwd(self, token):
        self.dx2, dmix, self.dg3, self.dg2 = _ffn_in_bwd(
            self.dgate, self.dup, self.wg_t, self.wu_t, self.x2, self.mix, self.dy, self.g3 + token[0, 0], self.g2, 512)
        dpool, dattn = _mix_bwd(dmix, self.w_out, 512)
        parts = [_attn_bwd(self.q, self.k, self.v, dattn, self.attn, self.lse, dil) for dil in DILATIONS]
        du, dwbd, self.dscale = _pool_bwd(self.u, dpool, self.wbd, self.pool_scale, 512)
        g = POOL_W // len(POOL_WINDOWS)
        self.dw_pool = jnp.stack([dwbd[i * g:(i + 1) * g, i * g:(i + 1) * g] for i in range(len(POOL_WINDOWS))])
        self.dproj = _dproj_combine(du, *[[p[j] for p in parts] for j in range(5)], self.cos, self.sin, 256)
        return _wgrad(self.dproj, self.h1, "wgrad_in", IN_W // 2, 1024), _wgrad(self.cat, dmix, "wgrad_out", D_MODEL, 1024)

    def input_bwd(self, token):
        grad_x, dg1 = _proj_bwd(self.dproj, self.w_in_t, self.x, self.dx2, self.g1 + token[0, 0], 512)
        return self.loss, grad_x, (dg1, self.dw_pool, self.dscale, self.dg2, self.dg3, self.dg4)


def _local_step(x, tgt, g1, w_pool, pool_scale, g2, g3, g4, w_in_t, w_out, wg_t, wu_t, w_down):
    zero = jnp.zeros((8, LANES), F32)
    step = _LocalStep(x, tgt, g1, w_pool, pool_scale, g2, g3, g4)
    step.mixer_fwd(w_in_t, zero)
    dw_gate, dw_up, dw_down = step.ffn_fwd_bwd(w_out, wg_t, wu_t, w_down)
    dw_in, dw_out = step.mixer_bwd(zero)
    loss, grad_x, small = step.input_bwd(zero)
    return loss, grad_x, small, (dw_in, dw_out, dw_gate, dw_up, dw_down)


def kernel(x, ln_pre_mix, w_in, w_pool, pool_scale, w_out, ln_post_mix, ln_pre_ffn, w_gate, w_up, w_down, ln_post_ffn, loss_target, m_ln_pre_mix, m_w_in, m_w_pool, m_pool_scale, m_w_out, m_ln_post_mix, m_ln_pre_ffn, m_w_gate, m_w_up, m_w_down, m_ln_post_ffn, v_ln_pre_mix, v_w_in, v_w_pool, v_pool_scale, v_w_out, v_ln_post_mix, v_ln_pre_ffn, v_w_gate, v_w_up, v_w_down, v_ln_post_ffn):
    shards = [w_in[0].T.astype(BF16), w_out[0].astype(BF16), w_gate[0].T.astype(BF16),
              w_up[0].T.astype(BF16), w_down[0].astype(BF16)]
    flat = lambda a: a.reshape(-1, D_MODEL)
    blocks = lambda a: a.reshape(N_DEV, -1, D_MODEL)
    step = _LocalStep(x[0], loss_target[0], ln_pre_mix, w_pool[0], pool_scale, ln_post_mix, ln_pre_ffn, ln_post_ffn)

    w_in_t = flat(_gather_two_level(shards[0], "gather_w_in"))
    rest = _exchange_start(shards[1:], [False] * 4, w_in_t, "gather_rest_start")
    attn = step.mixer_fwd(w_in_t, rest[4])
    srcs, lands = _exchange_wait(rest, [False] * 4, attn, "gather_rest_wait")
    w_out_f, wg_t, wu_t, w_down_f = [flat(a) for a in _fill_own(lands, srcs, [False] * 4)]

    ffn = _exchange_start([blocks(a) for a in step.ffn_fwd_bwd(w_out_f, wg_t, wu_t, w_down_f)], [True] * 3,
                          step.dgate, "grads_ffn_start")
    mixer = _exchange_start([blocks(a) for a in step.mixer_bwd(ffn[4])], [True] * 2, step.dproj, "grads_mixer_start")
    loss, grad_x, small = step.input_bwd(mixer[4])
    got = []
    for handle, n_arr, nm in ((mixer, 2, "grads_mixer"), (ffn, 3, "grads_ffn")):
        srcs, lands = _exchange_wait(handle, [True] * n_arr, grad_x, nm + "_wait")
        got += _fill_own(lands, srcs, [True] * n_arr)
    sums = [_slot_sum(got[i], f"sum_grad_{i}", got[i].shape[1] // 2) for i in range(5)]

    small_buf = _pack_small(small[0], small[1], small[2], small[3], small[4], small[5], loss)
    small_sum = _slot_sum(_exchange([small_buf], [False], "gather_small")[0], "sum_small", SMALL_ROWS)

    g_in, g_out, g_gate, g_up, g_down = sums[0].T, sums[1], sums[2].T, sums[3].T, sums[4]
    upd = [_adamw(w[0], g, m[0], v[0], f"adamw_{nm}") for nm, w, g, m, v in (
        ("in", w_in, g_in, m_w_in, v_w_in), ("out", w_out, g_out, m_w_out, v_w_out),
        ("gate", w_gate, g_gate, m_w_gate, v_w_gate), ("up", w_up, g_up, m_w_up, v_w_up),
        ("down", w_down, g_down, m_w_down, v_w_down))]
    pack = lambda a, b, c, d, e, f: _pack_small(a, b[0], c, d, e, f, jnp.zeros((1,), F32))
    small_upd = _adamw(
        pack(ln_pre_mix, w_pool, pool_scale, ln_post_mix, ln_pre_ffn, ln_post_ffn), small_sum,
        pack(m_ln_pre_mix, m_w_pool, m_pool_scale, m_ln_post_mix, m_ln_pre_ffn, m_ln_post_ffn),
        pack(v_ln_pre_mix, v_w_pool, v_pool_scale, v_ln_post_mix, v_ln_pre_ffn, v_ln_post_ffn), "adamw_small")

    def tree(small6, big5):
        s1, spool, sscale, s2, s3, s4 = small6
        b_in, b_out, b_gate, b_up, b_down = [b[None] for b in big5]
        return [s1, b_in, spool, sscale, b_out, s2, s3, b_gate, b_up, b_down, s4]

    g_small = _unpack_small(small_sum)
    outs = [g_small[6][0], grad_x[None]]
    outs += tree(g_small[:6], [g_in, g_out, g_gate, g_up, g_down])
    for j in range(3):
        outs += tree(_unpack_small(small_upd[j])[:6], [u[j] for u in upd])
    return tuple(outs)
```

```python
import jax
import jax.numpy as jnp
from jax import lax
from jax.experimental import pallas as pl
from jax.experimental.pallas import tpu as pltpu

F32 = jnp.float32
BF16 = jnp.bfloat16

D_MODEL = 1024
POOL_W = 256
ATTN_W = 768
IN_W = 2560
D_FF = 2816
POOL_WINDOWS = (2, 4, 8, 16)
POOL_HALO = 16
DILATIONS = (1, 4, 16)
BLK = 128
LANES = 128
HEAD_DIM = 64
N_GROUPS = ATTN_W // LANES
ROPE_THETA = 10000.0
EPS = 1e-6
NEG = -1e30
N_DEV = 8
SMALL_ROWS = 24

ADAM_LR = 0.001
ADAM_B1 = 0.9
ADAM_B2 = 0.999
ADAM_EPS = 1e-08
ADAM_WD = 0.01
ADAM_STEP = 10

VMEM_LIMIT = 56 * 1024 * 1024


def _dot(a, b):
    return jnp.dot(a, b, preferred_element_type=F32)


def _dot_nt(a, b):
    return lax.dot_general(a, b, (((1,), (1,)), ((), ())), preferred_element_type=F32)


def _dot_tn(a, b):
    return lax.dot_general(a, b, (((0,), (0,)), ((), ())), preferred_element_type=F32)


def _params(n_grid):
    return pltpu.CompilerParams(dimension_semantics=("arbitrary",) * n_grid, vmem_limit_bytes=VMEM_LIMIT)


def _tok(tm, c):
    return pl.BlockSpec((tm, c), lambda i: (i, 0))


def _res(shape):
    return pl.BlockSpec(shape, lambda i: (0,) * len(shape), pipeline_mode=pl.Buffered(1))


def _acc(shape):
    return pl.BlockSpec(shape, lambda i: (0,) * len(shape))


def _rms_fwd(x, g):
    r = lax.rsqrt(jnp.mean(x * x, axis=-1, keepdims=True) + EPS)
    return x * r * g


def _rms_bwd(x, g, dy):
    r = lax.rsqrt(jnp.mean(x * x, axis=-1, keepdims=True) + EPS)
    xh = x * r
    gd = dy * g
    dx = r * (gd - xh * jnp.mean(gd * xh, axis=-1, keepdims=True))
    return dx, jnp.sum(dy * xh, axis=0, keepdims=True)


def _rope(x, c, s, sign):
    lane = lax.broadcasted_iota(jnp.int32, (x.shape[0], LANES), 1)
    first = (lane % HEAD_DIM) < (HEAD_DIM // 2)
    outs = []
    for g in range(x.shape[1] // LANES):
        xg = x[:, g * LANES:(g + 1) * LANES]
        rot = jnp.where(first, pltpu.roll(xg, LANES - HEAD_DIM // 2, 1), pltpu.roll(xg, HEAD_DIM // 2, 1))
        outs.append(xg * c + sign * (rot * s))
    return jnp.concatenate(outs, axis=1)


def _proj_fwd(x, g1, w_in_t, cos, sin, tm):
    T = x.shape[0]

    def body(x_ref, g_ref, w_ref, c_ref, s_ref, h_ref, u_ref, q_ref, k_ref, v_ref):
        h = _rms_fwd(x_ref[...], g_ref[...]).astype(BF16)
        h_ref[...] = h
        proj = _dot_nt(h, w_ref[...])
        c = c_ref[...]
        s = s_ref[...]
        u_ref[...] = proj[:, :POOL_W]
        _store_packed(q_ref, _rope(proj[:, POOL_W:POOL_W + ATTN_W], c, s, 1.0))
        _store_packed(k_ref, _rope(proj[:, POOL_W + ATTN_W:POOL_W + 2 * ATTN_W], c, s, 1.0))
        _store_packed(v_ref, proj[:, POOL_W + 2 * ATTN_W:])

    return pl.pallas_call(
        body, name="proj_fwd", grid=(T // tm,),
        in_specs=[_tok(tm, D_MODEL), _res((1, D_MODEL)), _res((IN_W, D_MODEL)), _tok(tm, LANES), _tok(tm, LANES)],
        out_specs=[_tok(tm, D_MODEL), _tok(tm, POOL_W)] + [_tok_packed(tm, ATTN_W)] * 3,
        out_shape=[jax.ShapeDtypeStruct((T, D_MODEL), BF16), jax.ShapeDtypeStruct((T, POOL_W), F32)]
        + [_packed(T, ATTN_W)] * 3,
        compiler_params=_params(1),
    )(x, g1, w_in_t, cos, sin)


def _pool_window(lane):
    return jnp.where(lane < 64, 2, jnp.where(lane < 128, 4, jnp.where(lane < 192, 8, 16)))


def _pool_select(lane, a2, a4, a8, a16):
    return jnp.where(lane < 64, a2, jnp.where(lane < 128, a4, jnp.where(lane < 192, a8, a16)))


def _pool_delta(cur, prev, i, tm):
    prev = jnp.where(i > 0, prev, 0.0)
    ext = jnp.concatenate([prev, cur], axis=0)
    s2 = ext + pltpu.roll(ext, 1, 0)
    s4 = s2 + pltpu.roll(s2, 2, 0)
    s8 = s4 + pltpu.roll(s4, 4, 0)
    s16 = s8 + pltpu.roll(s8, 8, 0)
    lane = lax.broadcasted_iota(jnp.int32, (tm, POOL_W), 1)
    row = lax.broadcasted_iota(jnp.int32, (tm, POOL_W), 0) + i * tm
    ws = _pool_select(lane, s2[POOL_HALO:], s4[POOL_HALO:], s8[POOL_HALO:], s16[POOL_HALO:])
    cnt = jnp.minimum(row + 1, _pool_window(lane)).astype(F32)
    return ws / cnt - cur


def _pool_fwd(u, wbd, scale, tm):
    T = u.shape[0]
    hb = tm // POOL_HALO

    def body(u_ref, prev_ref, w_ref, sc_ref, o_ref):
        d = _pool_delta(u_ref[...], prev_ref[...], pl.program_id(0), tm)
        o_ref[...] = (_dot(d.astype(BF16), w_ref[...]) * sc_ref[...]).astype(BF16)

    return pl.pallas_call(
        body, name="pool_fwd", grid=(T // tm,),
        in_specs=[_tok(tm, POOL_W), pl.BlockSpec((POOL_HALO, POOL_W), lambda i: (jnp.maximum(i * hb - 1, 0), 0)),
                  _res((POOL_W, POOL_W)), _res((1, POOL_W))],
        out_specs=_tok(tm, POOL_W),
        out_shape=jax.ShapeDtypeStruct((T, POOL_W), BF16),
        compiler_params=_params(1),
    )(u, u, wbd, scale)


def _attn_mask(n):
    qi = lax.broadcasted_iota(jnp.int32, (BLK, 2 * BLK), 0)
    kj = lax.broadcasted_iota(jnp.int32, (BLK, 2 * BLK), 1)
    dist = qi + BLK - kj
    return (dist >= 0) & (dist <= BLK) & ((kj >= BLK) | (n > 0))


def _stack_heads(x, lo):
    zero = jnp.zeros_like(x)
    return jnp.concatenate([jnp.where(lo, x, zero), jnp.where(lo, zero, x)], axis=0)


def _head_col(tile, lane, h):
    return jnp.sum(jnp.where(lane == h, tile, 0.0), axis=1, keepdims=True)


def _attn_cols(dil):
    return ATTN_W // 2 if dil >= 16 else ATTN_W


def _stream_pairs_per_step(dil):
    return 2 if dil >= 16 else 1


def _attn_specs(dil, nb):
    cw = _attn_cols(dil)
    ch = BLK * dil
    wide = lambda f: pl.BlockSpec((cw // LANES, ch // 2, LANES), f)
    full = pl.BlockSpec((cw // LANES, ch, LANES), lambda n, j, r: (j, n, 0))
    cur = lambda n, j, r: (j, n, 0)
    prv = lambda n, j, r: (j, jnp.maximum(n - 1, 0), 0)
    prv_out = lambda n, j, r: (j, (n + nb - 1) % nb, 0)
    heads = pl.BlockSpec((ch, LANES), lambda n, j, r: (n, 0))
    return cw, wide(cur), wide(prv), wide(prv_out), heads, full


HIGH_HALF = 0xFFFF0000


def _pack(x):
    return pltpu.bitcast(x.astype(BF16), F32)


def _unpack(words):
    return pltpu.bitcast(words, BF16)


def _packed(rows, cols):
    return jax.ShapeDtypeStruct((cols // LANES, rows // 2, LANES), F32)


def _tok_packed(tm, cols):
    return pl.BlockSpec((cols // LANES, tm // 2, LANES), lambda i: (0, i, 0))


def _store_packed(ref, x):
    for g in range(x.shape[1] // LANES):
        ref[g] = _pack(x[:, g * LANES:(g + 1) * LANES])


def _load_packed(ref):
    return jnp.concatenate([_unpack(ref[g]) for g in range(ref.shape[0])], axis=1)


def _load_streams(ref, dil, r2, sl):
    if dil == 1:
        return [_unpack(ref[sl])]
    words = lax.bitcast_convert_type(ref.at[sl][pl.ds(r2, BLK, stride=dil // 2), :], jnp.uint32)
    even = lax.bitcast_convert_type(words << 16, F32).astype(BF16)
    odd = lax.bitcast_convert_type(words & jnp.uint32(HIGH_HALF), F32).astype(BF16)
    return [even, odd]


def _load_streams_f32(ref, dil, r2, sl):
    ref = ref if sl is None else ref.at[sl]
    if dil == 1:
        return [ref[...]]
    return [ref[pl.ds(2 * r2 + e, BLK, stride=dil), :] for e in range(2)]


def _store_streams_f32(ref, dil, r2, sl, tiles):
    ref = ref if sl is None else ref.at[sl]
    if dil == 1:
        ref[...] = tiles[0]
    else:
        for e, t in enumerate(tiles):
            ref[pl.ds(2 * r2 + e, BLK, stride=dil), :] = t


def _store_streams(ref, dil, r2, sl, tiles):
    if dil == 1:
        ref[sl] = _pack(tiles[0])
    else:
        even, odd = [lax.bitcast_convert_type(t.astype(BF16).astype(F32), jnp.uint32) for t in tiles]
        words = (odd & jnp.uint32(HIGH_HALF)) | (even >> 16)
        ref.at[sl][pl.ds(r2, BLK, stride=dil // 2), :] = lax.bitcast_convert_type(words, F32)


def _attn_fwd(q, k, v, dil, prev, last):
    T = 2 * q.shape[1]
    nb = T // (BLK * dil)
    first = prev is None
    cw, cur, prv, _, heads, full = _attn_specs(dil, nb)
    ncb = ATTN_W // cw
    heads_per_step = cw // HEAD_DIM
    n_str = min(dil, 2)
    reps = _stream_pairs_per_step(dil)
    everything = None

    def body(*refs):
        if first:
            q_ref, kc_ref, kp_ref, vc_ref, vp_ref, acc_ref, lse_ref = refs
        else:
            q_ref, kc_ref, kp_ref, vc_ref, vp_ref, acc_in, lse_in, acc_ref, lse_ref = refs
        j = pl.program_id(1)
        valid = _attn_mask(pl.program_id(0))
        lane = lax.broadcasted_iota(jnp.int32, (BLK, LANES), 1)
        lo = lane < HEAD_DIM
        store_acc = _store_streams if last else _store_streams_f32

        def stream_pair(r2):
            lse_tiles = [jnp.zeros((BLK, LANES), F32) for _ in range(n_str)]
            own = []
            for g in range(cw // LANES):
                qs, kcs, kps, vcs, vps = [_load_streams(r, dil, r2, g)
                                          for r in (q_ref, kc_ref, kp_ref, vc_ref, vp_ref)]
                pairs = []
                for e in range(n_str):
                    qg = qs[e] * 0.125
                    kcat = jnp.concatenate([kps[e], kcs[e]], axis=0)
                    vcat = jnp.concatenate([vps[e], vcs[e]], axis=0)
                    pair = None
                    for hh in range(2):
                        h = j * heads_per_step + 2 * g + hh
                        hm = lo if hh == 0 else jnp.logical_not(lo)
                        s = _dot_nt(jnp.where(hm, qg, jnp.zeros_like(qg)), kcat)
                        s = jnp.where(valid, s, NEG)
                        m = jnp.max(s, axis=1, keepdims=True)
                        p = jnp.exp(s - m)
                        den = jnp.sum(p, axis=1, keepdims=True)
                        o = _dot(p.astype(BF16), vcat) / den
                        pair = o if hh == 0 else jnp.where(lo, pair, o)
                        lse_tiles[e] = jnp.where(lane == h, m + jnp.log(den), lse_tiles[e])
                    pairs.append(pair)
                if first:
                    store_acc(acc_ref, dil, r2, g, pairs)
                else:
                    own.append(pairs)
            if not first:
                mine = (lane >= j * heads_per_step) & (lane < (j + 1) * heads_per_step)
                before = _load_streams_f32(lse_in, dil, r2, everything)
                w_before, w_own = [], []
                for e in range(n_str):
                    mx = jnp.maximum(before[e], lse_tiles[e])
                    total = mx + jnp.log(jnp.exp(before[e] - mx) + jnp.exp(lse_tiles[e] - mx))
                    w_before.append(jnp.exp(before[e] - total))
                    w_own.append(jnp.exp(lse_tiles[e] - total))
                    lse_tiles[e] = jnp.where(mine, total, 0.0)
                for g in range(cw // LANES):
                    h0 = j * heads_per_step + 2 * g
                    spread = lambda w: jnp.where(lo, _head_col(w, lane, h0), _head_col(w, lane, h0 + 1))
                    olds = _load_streams_f32(acc_in, dil, r2, g)
                    store_acc(acc_ref, dil, r2, g, [olds[e] * spread(w_before[e]) + own[g][e] * spread(w_own[e])
                                                    for e in range(n_str)])
            if ncb == 1:
                _store_streams_f32(lse_ref, dil, r2, everything, lse_tiles)
            else:
                @pl.when(j == 0)
                def _():
                    _store_streams_f32(lse_ref, dil, r2, everything, lse_tiles)

                @pl.when(j > 0)
                def _():
                    before = _load_streams_f32(lse_ref, dil, r2, everything)
                    _store_streams_f32(lse_ref, dil, r2, everything, [a + b for a, b in zip(before, lse_tiles)])

        for rep in range(reps):
            stream_pair(pl.program_id(2) * reps + rep)

    ins = [q, k, k, v, v]
    in_specs = [cur, cur, prv, cur, prv]
    if not first:
        ins += [prev[0], prev[1]]
        in_specs += [full, heads]
    return pl.pallas_call(
        body, name=f"attn_fwd_d{dil}", grid=(nb, ncb, max(dil // 2, 1) // reps),
        in_specs=in_specs, out_specs=[cur if last else full, heads],
        out_shape=[_packed(T, ATTN_W) if last else jax.ShapeDtypeStruct((N_GROUPS, T, LANES), F32),
                   jax.ShapeDtypeStruct((T, LANES), F32)],
        compiler_params=_params(3),
    )(*ins)


def _mix_fwd(pool, attn, x, w_out, g2, g3, tm):
    T = x.shape[0]

    def body(p_ref, a_ref, x_ref, w_ref, g2_ref, g3_ref, cat_ref, mix_ref, x2_ref, h2_ref):
        p = p_ref[...]
        a = _load_packed(a_ref)
        cat_ref[...] = jnp.concatenate([p, a], axis=1)
        mix = _dot(p, w_ref[:POOL_W, :]) + _dot(a, w_ref[POOL_W:, :])
        mix_ref[...] = mix
        x2 = x_ref[...] + _rms_fwd(mix, g2_ref[...])
        x2_ref[...] = x2
        h2_ref[...] = _rms_fwd(x2, g3_ref[...]).astype(BF16)

    return pl.pallas_call(
        body, name="mix_fwd", grid=(T // tm,),
        in_specs=[_tok(tm, POOL_W), _tok_packed(tm, ATTN_W), _tok(tm, D_MODEL), _res((D_MODEL, D_MODEL)),
                  _res((1, D_MODEL)), _res((1, D_MODEL))],
        out_specs=[_tok(tm, D_MODEL)] * 4,
        out_shape=[jax.ShapeDtypeStruct((T, D_MODEL), BF16), jax.ShapeDtypeStruct((T, D_MODEL), F32),
                   jax.ShapeDtypeStruct((T, D_MODEL), F32), jax.ShapeDtypeStruct((T, D_MODEL), BF16)],
        compiler_params=_params(1),
    )(pool, attn, x, w_out, g2, g3)


def _ffn_up(h2, wg_t, wu_t, tm):
    T = h2.shape[0]

    def body(h_ref, wg_ref, wu_ref, dg_ref, du_ref, a_ref):
        h = h_ref[...]
        gate = _dot_nt(h, wg_ref[...])
        up = _dot_nt(h, wu_ref[...])
        sg = 1.0 / (1.0 + jnp.exp(-gate))
        silu = gate * sg
        a_ref[...] = (silu * up).astype(BF16)
        dg_ref[...] = (up * (sg * (1.0 + gate * (1.0 - sg)))).astype(BF16)
        du_ref[...] = silu.astype(BF16)

    return pl.pallas_call(
        body, name="ffn_up", grid=(T // tm,),
        in_specs=[_tok(tm, D_MODEL), _res((D_FF, D_MODEL)), _res((D_FF, D_MODEL))],
        out_specs=[_tok(tm, D_FF)] * 3,
        out_shape=[jax.ShapeDtypeStruct((T, D_FF), BF16)] * 3,
        compiler_params=_params(1),
    )(h2, wg_t, wu_t)


def _ffn_down_loss(act, w_down, x2, g4, tgt, tm):
    T = act.shape[0]

    def body(a_ref, w_ref, x2_ref, g_ref, t_ref, df_ref, dy_ref, dg_ref, loss_ref):
        i = pl.program_id(0)

        @pl.when(i == 0)
        def _():
            dg_ref[...] = jnp.zeros_like(dg_ref)
            loss_ref[...] = jnp.zeros_like(loss_ref)

        f = _dot(a_ref[...], w_ref[...])
        g = g_ref[...]
        err = x2_ref[...] + _rms_fwd(f, g) - t_ref[...]
        loss_ref[...] += 0.5 * jnp.sum(jnp.mean(err * err, axis=-1, keepdims=True), axis=0, keepdims=True)
        dy = err * (1.0 / D_MODEL)
        dy_ref[...] = dy
        df, dg = _rms_bwd(f, g, dy)
        dg_ref[...] += dg
        df_ref[...] = df.astype(BF16)

    return pl.pallas_call(
        body, name="ffn_down_loss", grid=(T // tm,),
        in_specs=[_tok(tm, D_FF), _res((D_FF, D_MODEL)), _tok(tm, D_MODEL), _res((1, D_MODEL)), _tok(tm, D_MODEL)],
        out_specs=[_tok(tm, D_MODEL), _tok(tm, D_MODEL), _acc((1, D_MODEL)), _acc((1, 1))],
        out_shape=[jax.ShapeDtypeStruct((T, D_MODEL), BF16), jax.ShapeDtypeStruct((T, D_MODEL), F32),
                   jax.ShapeDtypeStruct((1, D_MODEL), F32), jax.ShapeDtypeStruct((1, 1), F32)],
        compiler_params=_params(1),
    )(act, w_down, x2, g4, tgt)


def _ffn_act_bwd(df, w_down, act_dgate, act_dup, tm):
    T = df.shape[0]

    def body(df_ref, w_ref, ag_ref, au_ref, dg_ref, du_ref):
        dact = _dot_nt(df_ref[...], w_ref[...])
        dg_ref[...] = (dact * ag_ref[...].astype(F32)).astype(BF16)
        du_ref[...] = (dact * au_ref[...].astype(F32)).astype(BF16)

    return pl.pallas_call(
        body, name="ffn_act_bwd", grid=(T // tm,),
        in_specs=[_tok(tm, D_MODEL), _res((D_FF, D_MODEL)), _tok(tm, D_FF), _tok(tm, D_FF)],
        out_specs=[_tok(tm, D_FF)] * 2,
        out_shape=[jax.ShapeDtypeStruct((T, D_FF), BF16)] * 2,
        compiler_params=_params(1),
    )(df, w_down, act_dgate, act_dup)


def _ffn_in_bwd(dgate, dup, wg_t, wu_t, x2, mix, dy, g3, g2, tm):
    T = x2.shape[0]

    def body(dg_ref, du_ref, wg_ref, wu_ref, x2_ref, mix_ref, dy_ref, g3_ref, g2_ref,
             dx2_ref, dmix_ref, dg3_ref, dg2_ref):
        @pl.when(pl.program_id(0) == 0)
        def _():
            dg3_ref[...] = jnp.zeros_like(dg3_ref)
            dg2_ref[...] = jnp.zeros_like(dg2_ref)

        dh2 = _dot(dg_ref[...], wg_ref[...]) + _dot(du_ref[...], wu_ref[...])
        dn, dg3 = _rms_bwd(x2_ref[...], g3_ref[...], dh2)
        dx2 = dy_ref[...] + dn
        dx2_ref[...] = dx2
        dg3_ref[...] += dg3
        dmix, dg2 = _rms_bwd(mix_ref[...], g2_ref[...], dx2)
        dg2_ref[...] += dg2
        dmix_ref[...] = dmix.astype(BF16)

    return pl.pallas_call(
        body, name="ffn_in_bwd", grid=(T // tm,),
        in_specs=[_tok(tm, D_FF), _tok(tm, D_FF), _res((D_FF, D_MODEL)), _res((D_FF, D_MODEL)),
                  _tok(tm, D_MODEL), _tok(tm, D_MODEL), _tok(tm, D_MODEL), _res((1, D_MODEL)), _res((1, D_MODEL))],
        out_specs=[_tok(tm, D_MODEL), _tok(tm, D_MODEL), _acc((1, D_MODEL)), _acc((1, D_MODEL))],
        out_shape=[jax.ShapeDtypeStruct((T, D_MODEL), F32), jax.ShapeDtypeStruct((T, D_MODEL), BF16),
                   jax.ShapeDtypeStruct((1, D_MODEL), F32), jax.ShapeDtypeStruct((1, D_MODEL), F32)],
        compiler_params=_params(1),
    )(dgate, dup, wg_t, wu_t, x2, mix, dy, g3, g2)


def _mix_bwd(dmix, w_out, tm):
    T = dmix.shape[0]

    def body(d_ref, w_ref, dp_ref, da_ref):
        dcat = _dot_nt(d_ref[...], w_ref[...])
        dp_ref[...] = dcat[:, :POOL_W].astype(BF16)
        _store_packed(da_ref, dcat[:, POOL_W:])

    return pl.pallas_call(
        body, name="mix_bwd", grid=(T // tm,),
        in_specs=[_tok(tm, D_MODEL), _res((D_MODEL, D_MODEL))],
        out_specs=[_tok(tm, POOL_W), _tok_packed(tm, ATTN_W)],
        out_shape=[jax.ShapeDtypeStruct((T, POOL_W), BF16), _packed(T, ATTN_W)],
        compiler_params=_params(1),
    )(dmix, w_out)


def _attn_bwd(q, k, v, dout, out, lse, dil):
    T = 2 * q.shape[1]
    nb = T // (BLK * dil)
    cw, cur, prv, prv_out, heads, _ = _attn_specs(dil, nb)
    ncb = ATTN_W // cw
    heads_per_step = cw // HEAD_DIM
    n_str = min(dil, 2)
    reps = _stream_pairs_per_step(dil)

    def body(q_ref, kc_ref, kp_ref, vc_ref, vp_ref, do_ref, o_ref, lse_ref,
             dq_ref, dkc_ref, dkp_ref, dvc_ref, dvp_ref):
        j = pl.program_id(1)
        valid = _attn_mask(pl.program_id(0))
        lane = lax.broadcasted_iota(jnp.int32, (BLK, LANES), 1)
        lo = lane < HEAD_DIM
        valid2 = jnp.concatenate([valid, valid], axis=0)
        for rep, g in [(rep, g) for rep in range(reps) for g in range(cw // LANES)]:
            r2 = pl.program_id(2) * reps + rep
            sl = g
            lse_tiles = _load_streams_f32(lse_ref, dil, r2, None)
            qs, kcs, kps, vcs, vps, dos, os_ = [
                _load_streams(r, dil, r2, sl) for r in (q_ref, kc_ref, kp_ref, vc_ref, vp_ref, do_ref, o_ref)]
            dqs, dks, dvs = [], [], []
            for e in range(n_str):
                qg = qs[e] * 0.125
                dog = dos[e]
                kcat = jnp.concatenate([kps[e], kcs[e]], axis=0)
                vcat = jnp.concatenate([vps[e], vcs[e]], axis=0)
                prod = dog.astype(F32) * os_[e].astype(F32)
                h0 = j * heads_per_step + 2 * g
                q2 = _stack_heads(qg, lo)
                do2 = _stack_heads(dog, lo)
                lse2 = jnp.concatenate([_head_col(lse_tiles[e], lane, h0), _head_col(lse_tiles[e], lane, h0 + 1)], axis=0)
                dsum2 = jnp.concatenate([jnp.sum(jnp.where(lo, prod, 0.0), axis=1, keepdims=True),
                                         jnp.sum(jnp.where(lo, 0.0, prod), axis=1, keepdims=True)], axis=0)
                p = jnp.exp(jnp.where(valid2, _dot_nt(q2, kcat), NEG) - lse2)
                ds = (p * (_dot_nt(do2, vcat) - dsum2)).astype(BF16)
                dvs.append(_dot_tn(p.astype(BF16), do2))
                dks.append(_dot_tn(ds, q2))
                dq2 = _dot(ds, kcat) * 0.125
                dqs.append(jnp.where(lo, dq2[:BLK], dq2[BLK:]))
            _store_streams(dq_ref, dil, r2, sl, dqs)
            _store_streams(dkp_ref, dil, r2, sl, [t[:BLK] for t in dks])
            _store_streams(dkc_ref, dil, r2, sl, [t[BLK:] for t in dks])
            _store_streams(dvp_ref, dil, r2, sl, [t[:BLK] for t in dvs])
            _store_streams(dvc_ref, dil, r2, sl, [t[BLK:] for t in dvs])

    return pl.pallas_call(
        body, name=f"attn_bwd_d{dil}", grid=(nb, ncb, max(dil // 2, 1) // reps),
        in_specs=[cur, cur, prv, cur, prv, cur, cur, heads],
        out_specs=[cur, cur, prv_out, cur, prv_out],
        out_shape=[_packed(T, ATTN_W)] * 5,
        compiler_params=_params(3),
    )(q, k, k, v, v, dout, out, lse)


def _pool_bwd(u, dy, wbd, scale, tm):
    T = u.shape[0]
    nt = T // tm
    hb = tm // POOL_HALO

    def body(u_ref, prev_ref, dy_ref, next_ref, w_ref, sc_ref, du_ref, dw_ref, dsc_ref):
        i = pl.program_id(0)

        @pl.when(i == 0)
        def _():
            dw_ref[...] = jnp.zeros_like(dw_ref)
            dsc_ref[...] = jnp.zeros_like(dsc_ref)

        w = w_ref[...]
        sc = sc_ref[...]
        d = _pool_delta(u_ref[...], prev_ref[...], i, tm).astype(BF16)
        dyc = dy_ref[...].astype(F32)
        dsc_ref[...] += jnp.sum(dyc * _dot(d, w), axis=0, keepdims=True)
        nxt = jnp.where(i < nt - 1, next_ref[...].astype(F32), 0.0)
        dypre = (jnp.concatenate([dyc, nxt], axis=0) * sc).astype(BF16)
        dw_ref[...] += _dot_tn(d, dypre[:tm])
        dd = _dot_nt(dypre, w)
        n = tm + POOL_HALO
        lane = lax.broadcasted_iota(jnp.int32, (n, POOL_W), 1)
        row = lax.broadcasted_iota(jnp.int32, (n, POOL_W), 0) + i * tm
        gx = dd / jnp.minimum(row + 1, _pool_window(lane)).astype(F32)
        a2 = gx + pltpu.roll(gx, n - 1, 0)
        a4 = a2 + pltpu.roll(a2, n - 2, 0)
        a8 = a4 + pltpu.roll(a4, n - 4, 0)
        a16 = a8 + pltpu.roll(a8, n - 8, 0)
        fs = _pool_select(lane[:tm], a2[:tm], a4[:tm], a8[:tm], a16[:tm])
        du_ref[...] = (fs - dd[:tm]).astype(BF16)

    return pl.pallas_call(
        body, name="pool_bwd", grid=(nt,),
        in_specs=[_tok(tm, POOL_W), pl.BlockSpec((POOL_HALO, POOL_W), lambda i: (jnp.maximum(i * hb - 1, 0), 0)),
                  _tok(tm, POOL_W), pl.BlockSpec((POOL_HALO, POOL_W), lambda i: (jnp.minimum((i + 1) * hb, nt * hb - 1), 0)),
                  _res((POOL_W, POOL_W)), _res((1, POOL_W))],
        out_specs=[_tok(tm, POOL_W), _acc((POOL_W, POOL_W)), _acc((1, POOL_W))],
        out_shape=[jax.ShapeDtypeStruct((T, POOL_W), BF16), jax.ShapeDtypeStruct((POOL_W, POOL_W), F32),
                   jax.ShapeDtypeStruct((1, POOL_W), F32)],
        compiler_params=_params(1),
    )(u, u, dy, dy, wbd, scale)


def _dproj_combine(du, dqs, dkcs, dkps, dvcs, dvps, cos, sin, tm):
    T = du.shape[0]
    n_cfg = len(dqs)

    def body(*refs):
        du_ref = refs[0]
        groups = [refs[1 + j * n_cfg:1 + (j + 1) * n_cfg] for j in range(5)]
        c_ref, s_ref, out_ref = refs[1 + 5 * n_cfg:]
        tot = lambda rs: sum(_load_packed(r).astype(F32) for r in rs)
        c = c_ref[...]
        s = s_ref[...]
        dq = _rope(tot(groups[0]), c, s, -1.0)
        dk = _rope(tot(groups[1]) + tot(groups[2]), c, s, -1.0)
        dv = tot(groups[3]) + tot(groups[4])
        out_ref[...] = jnp.concatenate([du_ref[...], dq.astype(BF16), dk.astype(BF16), dv.astype(BF16)], axis=1)

    return pl.pallas_call(
        body, name="dproj_combine", grid=(T // tm,),
        in_specs=[_tok(tm, POOL_W)] + [_tok_packed(tm, ATTN_W)] * (5 * n_cfg) + [_tok(tm, LANES)] * 2,
        out_specs=_tok(tm, IN_W),
        out_shape=jax.ShapeDtypeStruct((T, IN_W), BF16),
        compiler_params=_params(1),
    )(du, *dqs, *dkcs, *dkps, *dvcs, *dvps, cos, sin)


def _proj_bwd(dproj, w_in_t, x, dx2, g1, tm):
    T = x.shape[0]

    def body(d_ref, w_ref, x_ref, r_ref, g_ref, dx_ref, dg_ref):
        @pl.when(pl.program_id(0) == 0)
        def _():
            dg_ref[...] = jnp.zeros_like(dg_ref)

        dn, dg = _rms_bwd(x_ref[...], g_ref[...], _dot(d_ref[...], w_ref[...]))
        dg_ref[...] += dg
        dx_ref[...] = r_ref[...] + dn

    return pl.pallas_call(
        body, name="proj_bwd", grid=(T // tm,),
        in_specs=[_tok(tm, IN_W), _res((IN_W, D_MODEL)), _tok(tm, D_MODEL), _tok(tm, D_MODEL), _res((1, D_MODEL))],
        out_specs=[_tok(tm, D_MODEL), _acc((1, D_MODEL))],
        out_shape=[jax.ShapeDtypeStruct((T, D_MODEL), F32), jax.ShapeDtypeStruct((1, D_MODEL), F32)],
        compiler_params=_params(1),
    )(dproj, w_in_t, x, dx2, g1)


def _wgrad(a, b, name, tile_m, tk):
    T, M = a.shape
    N = b.shape[1]
    nk = T // tk

    def body(a_ref, b_ref, o_ref, acc_ref):
        kk = pl.program_id(1)

        @pl.when(kk == 0)
        def _():
            acc_ref[...] = jnp.zeros_like(acc_ref)

        acc_ref[...] += _dot_tn(a_ref[...], b_ref[...])

        @pl.when(kk == nk - 1)
        def _():
            o_ref[...] = acc_ref[...].astype(BF16)

    return pl.pallas_call(
        body, name=name, grid=(M // tile_m, nk),
        in_specs=[pl.BlockSpec((tk, tile_m), lambda j, kk: (kk, j)), pl.BlockSpec((tk, N), lambda j, kk: (kk, 0))],
        out_specs=pl.BlockSpec((tile_m, N), lambda j, kk: (j, 0)),
        out_shape=jax.ShapeDtypeStruct((M, N), BF16),
        scratch_shapes=[pltpu.VMEM((tile_m, N), F32)],
        compiler_params=_params(2),
    )(a, b)


def _exchange(arrs, scatter, name):
    n = len(arrs)
    out_shapes = [jax.ShapeDtypeStruct((N_DEV,) + (a.shape[1:] if sc else a.shape), a.dtype)
                  for a, sc in zip(arrs, scatter)]

    def body(*refs):
        ins, outs = refs[:n], refs[n:2 * n]
        send_sems, recv_sems, loc_sems = refs[2 * n:]
        x, y, c = lax.axis_index("x"), lax.axis_index("y"), lax.axis_index("c")
        me = 4 * x + 2 * y + c
        local, sends, recvs = [], [], []
        for i in range(n):
            own = ins[i].at[me] if scatter[i] else ins[i]
            loc = pltpu.make_async_copy(own, outs[i].at[me], loc_sems.at[i])
            loc.start()
            local.append(loc)
            for kbits in range(1, N_DEV):
                px = 1 - x if kbits & 4 else x
                py = 1 - y if kbits & 2 else y
                pc = 1 - c if kbits & 1 else c
                pid = 4 * px + 2 * py + pc
                src = ins[i].at[pid] if scatter[i] else ins[i]
                cp = pltpu.make_async_remote_copy(
                    src_ref=src, dst_ref=outs[i].at[me],
                    send_sem=send_sems.at[i, kbits - 1], recv_sem=recv_sems.at[i, kbits - 1],
                    device_id=(px, py, pc), device_id_type=pl.DeviceIdType.MESH)
                cp.start()
                sends.append(cp)
                recvs.append(pltpu.make_async_remote_copy(
                    src_ref=src, dst_ref=outs[i].at[pid],
                    send_sem=send_sems.at[i, kbits - 1], recv_sem=recv_sems.at[i, kbits - 1],
                    device_id=(px, py, pc), device_id_type=pl.DeviceIdType.MESH))
        for cp in recvs:
            cp.wait_recv()
        for cp in sends:
            cp.wait_send()
        for cp in local:
            cp.wait()

    hbm = pl.BlockSpec(memory_space=pl.ANY)
    return pl.pallas_call(
        body, name=name, in_specs=[hbm] * n, out_specs=[hbm] * n, out_shape=out_shapes,
        scratch_shapes=[pltpu.SemaphoreType.DMA((n, N_DEV - 1)), pltpu.SemaphoreType.DMA((n, N_DEV - 1)),
                        pltpu.SemaphoreType.DMA((n,))],
    )(*arrs)


def _gather_two_level(arr, name):
    def body(x_ref, out_ref, send_sems, recv_sems, local_sem):
        x, y, c = lax.axis_index("x"), lax.axis_index("y"), lax.axis_index("c")
        me, sibling = (x, y, c), (x, y, 1 - c)
        chips = [(1 - x, y), (x, 1 - y), (1 - x, 1 - y)]
        slot = lambda px, py, pc: out_ref.at[4 * px + 2 * py + pc]

        def copy(k, block, to, src=None):
            return pltpu.make_async_remote_copy(
                src_ref=slot(*block) if src is None else src, dst_ref=slot(*block),
                send_sem=send_sems.at[k], recv_sem=recv_sems.at[k],
                device_id=to, device_id_type=pl.DeviceIdType.MESH)

        mine = pltpu.make_async_copy(x_ref, slot(*me), local_sem)
        mine.start()
        first = [copy(0, me, sibling, src=x_ref)]
        first += [copy(1 + i, me, (*chip, c), src=x_ref) for i, chip in enumerate(chips)]
        for cp in first:
            cp.start()
        passed = [copy(4 + i, (*chip, c), sibling) for i, chip in enumerate(chips)]
        for i, chip in enumerate(chips):
            copy(1 + i, (*chip, c), me).wait_recv()
            passed[i].start()
        copy(0, sibling, me).wait_recv()
        for i, chip in enumerate(chips):
            copy(4 + i, (*chip, 1 - c), me).wait_recv()
        for cp in first + passed:
            cp.wait_send()
        mine.wait()

    hbm = pl.BlockSpec(memory_space=pl.ANY)
    return pl.pallas_call(
        body, name=name, in_specs=[hbm], out_specs=hbm,
        out_shape=jax.ShapeDtypeStruct((N_DEV,) + arr.shape, arr.dtype),
        scratch_shapes=[pltpu.SemaphoreType.DMA((N_DEV - 1,)), pltpu.SemaphoreType.DMA((N_DEV - 1,)),
                        pltpu.SemaphoreType.DMA],
    )(arr)


def _peers(x, y, c):
    for kbits in range(1, N_DEV):
        px = 1 - x if kbits & 4 else x
        py = 1 - y if kbits & 2 else y
        pc = 1 - c if kbits & 1 else c
        yield kbits - 1, (px, py, pc), 4 * px + 2 * py + pc


def _peer_copies(ins, lands, scatter, send_sems, recv_sems, incoming):
    x, y, c = lax.axis_index("x"), lax.axis_index("y"), lax.axis_index("c")
    me = 4 * x + 2 * y + c
    copies = []
    for i in range(len(ins)):
        for k, peer, pid in _peers(x, y, c):
            slot = i * (N_DEV - 1) + k
            copies.append(pltpu.make_async_remote_copy(
                src_ref=ins[i].at[pid] if scatter[i] else ins[i], dst_ref=lands[i].at[pid if incoming else me],
                send_sem=send_sems.at[slot], recv_sem=recv_sems.at[slot],
                device_id=peer, device_id_type=pl.DeviceIdType.MESH))
    return copies


_HBM = pl.BlockSpec(memory_space=pltpu.HBM)
_SEM = pl.BlockSpec(memory_space=pltpu.SEMAPHORE)
_DATAFLOW = pltpu.SideEffectType.DATAFLOW_SIDE_EFFECTING


def _exchange_start(arrs, scatter, after, name):
    n = len(arrs)
    lands = [lax.empty((N_DEV,) + (a.shape[1:] if sc else a.shape), a.dtype) for a, sc in zip(arrs, scatter)]

    def body(*refs):
        ins, lz = refs[:n], refs[n:2 * n]
        send_sems, recv_sems = refs[2 * n + 1:2 * n + 3]
        token = refs[-1]
        for cp in _peer_copies(ins, lz, scatter, send_sems, recv_sems, False):
            cp.start()
        token[...] = jnp.zeros_like(token)

    sem_shape = pltpu.SemaphoreType.DMA((n * (N_DEV - 1),))
    outs = pl.pallas_call(
        body, name=name,
        out_shape=(sem_shape, sem_shape, *[pltpu.HBM(a.shape, a.dtype) for a in arrs + lands],
                   jax.ShapeDtypeStruct((8, LANES), F32)),
        in_specs=[_HBM] * (2 * n) + [pl.BlockSpec(memory_space=pl.ANY)],
        out_specs=(_SEM, _SEM, *[_HBM] * (2 * n), pl.BlockSpec(memory_space=pltpu.VMEM)),
        input_output_aliases={i: 2 + i for i in range(2 * n)},
        compiler_params=pltpu.CompilerParams(has_side_effects=_DATAFLOW),
    )(*[pltpu.with_memory_space_constraint(a, pltpu.HBM) for a in arrs + lands], after)
    return outs[0], outs[1], list(outs[2:2 + n]), list(outs[2 + n:2 + 2 * n]), outs[-1]


def _exchange_wait(handle, scatter, after, name):
    send_sems, recv_sems, srcs, lands, _ = handle
    n = len(srcs)

    def body(*refs):
        ins, lz = refs[:n], refs[n:2 * n]
        for cp in _peer_copies(ins, lz, scatter, refs[2 * n], refs[2 * n + 1], False):
            cp.wait_send()
        for cp in _peer_copies(ins, lz, scatter, refs[2 * n], refs[2 * n + 1], True):
            cp.wait_recv()

    outs = pl.pallas_call(
        body, name=name,
        out_shape=[pltpu.HBM(a.shape, a.dtype) for a in srcs + lands],
        in_specs=[_HBM] * (2 * n) + [_SEM, _SEM, pl.BlockSpec(memory_space=pl.ANY)],
        out_specs=[_HBM] * (2 * n),
        input_output_aliases={i: i for i in range(2 * n)},
        compiler_params=pltpu.CompilerParams(has_side_effects=_DATAFLOW),
    )(*srcs, *lands, send_sems, recv_sems, after)
    return list(outs[:n]), list(outs[n:])


def _fill_own(lands, srcs, scatter):
    me = 4 * lax.axis_index("x") + 2 * lax.axis_index("y") + lax.axis_index("c")
    own = [lax.dynamic_index_in_dim(s, me, 0, keepdims=False) if sc else s for s, sc in zip(srcs, scatter)]
    return [lax.dynamic_update_index_in_dim(land, o, me, 0) for land, o in zip(lands, own)]


def _slot_sum(parts, name, tr):
    _, R, C = parts.shape

    def body(p_ref, o_ref):
        acc = p_ref[0].astype(F32)
        for s in range(1, N_DEV):
            acc = acc + p_ref[s].astype(F32)
        o_ref[...] = acc

    return pl.pallas_call(
        body, name=name, grid=(R // tr,),
        in_specs=[pl.BlockSpec((N_DEV, tr, C), lambda i: (0, i, 0))],
        out_specs=pl.BlockSpec((tr, C), lambda i: (i, 0)),
        out_shape=jax.ShapeDtypeStruct((R, C), F32),
        compiler_params=_params(1),
    )(parts)


def _adamw(w, g, m, v, name):
    def body(w_ref, g_ref, m_ref, v_ref, d_ref, nm_ref, nv_ref):
        g = g_ref[...]
        nm = ADAM_B1 * m_ref[...] + (1.0 - ADAM_B1) * g
        nv = ADAM_B2 * v_ref[...] + (1.0 - ADAM_B2) * jnp.square(g)
        m_hat = nm / (1.0 - ADAM_B1 ** ADAM_STEP)
        v_hat = nv / (1.0 - ADAM_B2 ** ADAM_STEP)
        d_ref[...] = -ADAM_LR * (m_hat / (jnp.sqrt(v_hat) + ADAM_EPS) + ADAM_WD * w_ref[...])
        nm_ref[...] = nm
        nv_ref[...] = nv

    return pl.pallas_call(
        body, name=name, out_shape=[jax.ShapeDtypeStruct(w.shape, F32)] * 3,
        compiler_params=pltpu.CompilerParams(vmem_limit_bytes=VMEM_LIMIT),
    )(w, g, m, v)


def _rope_tables(T):
    half = HEAD_DIM // 2
    freqs = ROPE_THETA ** (-jnp.arange(half, dtype=F32) * (2.0 / HEAD_DIM))
    ang = jnp.arange(T).astype(F32)[:, None] * jnp.tile(freqs, LANES // half)[None, :]
    sign = jnp.tile(jnp.concatenate([-jnp.ones((half,), F32), jnp.ones((half,), F32)]), LANES // HEAD_DIM)
    return jnp.cos(ang), jnp.sin(ang) * sign[None, :]


def _block_diag(w_pool):
    wbd = jnp.zeros((POOL_W, POOL_W), F32)
    g = POOL_W // len(POOL_WINDOWS)
    for i in range(len(POOL_WINDOWS)):
        wbd = wbd.at[i * g:(i + 1) * g, i * g:(i + 1) * g].set(w_pool[i])
    return wbd


def _pack_small(g1, w_pool, pool_scale, g2, g3, g4, extra):
    pad = lambda a: jnp.pad(a.reshape(1, -1), ((0, 0), (0, D_MODEL - a.size)))
    rows = [g1.reshape(1, -1), g2.reshape(1, -1), g3.reshape(1, -1), g4.reshape(1, -1),
            w_pool.reshape(-1, D_MODEL), pad(pool_scale), pad(extra)]
    buf = jnp.concatenate(rows, axis=0)
    return jnp.pad(buf, ((0, SMALL_ROWS - buf.shape[0]), (0, 0)))


def _unpack_small(buf):
    n_pool = len(POOL_WINDOWS) * (POOL_W // len(POOL_WINDOWS)) ** 2 // D_MODEL
    g = POOL_W // len(POOL_WINDOWS)
    return (buf[0:1], buf[4:4 + n_pool].reshape(1, len(POOL_WINDOWS), g, g), buf[4 + n_pool:5 + n_pool, :POOL_W],
            buf[1:2], buf[2:3], buf[3:4], buf[5 + n_pool])


class _LocalStep:
    def __init__(self, x, tgt, g1, w_pool, pool_scale, g2, g3, g4):
        self.x, self.tgt, self.pool_scale = x, tgt, pool_scale
        self.g1, self.g2, self.g3, self.g4 = g1, g2, g3, g4
        self.cos, self.sin = _rope_tables(x.shape[0])
        self.wbd = _block_diag(w_pool).astype(BF16)

    def mixer_fwd(self, w_in_t, token):
        self.w_in_t = w_in_t
        self.h1, self.u, self.q, self.k, self.v = _proj_fwd(
            self.x, self.g1 + token[0, 0], w_in_t, self.cos, self.sin, 512)
        self.pool = _pool_fwd(self.u, self.wbd, self.pool_scale, 512)
        prev = None
        for j, dil in enumerate(DILATIONS):
            prev = _attn_fwd(self.q, self.k, self.v, dil, prev, j == len(DILATIONS) - 1)
        self.attn, self.lse = prev
        return self.attn

    def ffn_fwd_bwd(self, w_out, wg_t, wu_t, w_down):
        self.w_out, self.wg_t, self.wu_t = w_out, wg_t, wu_t
        self.cat, self.mix, self.x2, h2 = _mix_fwd(self.pool, self.attn, self.x, w_out, self.g2, self.g3, 512)
        act_dgate, act_dup, act = _ffn_up(h2, wg_t, wu_t, 256)
        df, self.dy, self.dg4, self.loss = _ffn_down_loss(act, w_down, self.x2, self.g4, self.tgt, 512)
        self.dgate, self.dup = _ffn_act_bwd(df, w_down, act_dgate, act_dup, 512)
        return (_wgrad(self.dgate, h2, "wgrad_gate", D_FF // 2, 1024), _wgrad(self.dup, h2, "wgrad_up", D_FF // 2, 1024),
                _wgrad(act, df, "wgrad_down", D_FF // 2, 1024))

    def mixer_bwd(self, token):
        self.dx2, dmix, self.dg3, self.dg2 = _ffn_in_bwd(
            self.dgate, self.dup, self.wg_t, self.wu_t, self.x2, self.mix, self.dy, self.g3 + token[0, 0], self.g2, 512)
        dpool, dattn = _mix_bwd(dmix, self.w_out, 512)
        parts = [_attn_bwd(self.q, self.k, self.v, dattn, self.attn, self.lse, dil) for dil in DILATIONS]
        du, dwbd, self.dscale = _pool_bwd(self.u, dpool, self.wbd, self.pool_scale, 512)
        g = POOL_W // len(POOL_WINDOWS)
        self.dw_pool = jnp.stack([dwbd[i * g:(i + 1) * g, i * g:(i + 1) * g] for i in range(len(POOL_WINDOWS))])
        self.dproj = _dproj_combine(du, *[[p[j] for p in parts] for j in range(5)], self.cos, self.sin, 256)
        return _wgrad(self.dproj, self.h1, "wgrad_in", IN_W // 2, 1024), _wgrad(self.cat, dmix, "wgrad_out", D_MODEL, 1024)

    def input_bwd(self, token):
        grad_x, dg1 = _proj_bwd(self.dproj, self.w_in_t, self.x, self.dx2, self.g1 + token[0, 0], 512)
        return self.loss, grad_x, (dg1, self.dw_pool, self.dscale, self.dg2, self.dg3, self.dg4)


def _local_step(x, tgt, g1, w_pool, pool_scale, g2, g3, g4, w_in_t, w_out, wg_t, wu_t, w_down):
    zero = jnp.zeros((8, LANES), F32)
    step = _LocalStep(x, tgt, g1, w_pool, pool_scale, g2, g3, g4)
    step.mixer_fwd(w_in_t, zero)
    dw_gate, dw_up, dw_down = step.ffn_fwd_bwd(w_out, wg_t, wu_t, w_down)
    dw_in, dw_out = step.mixer_bwd(zero)
    loss, grad_x, small = step.input_bwd(zero)
    return loss, grad_x, small, (dw_in, dw_out, dw_gate, dw_up, dw_down)


def kernel(x, ln_pre_mix, w_in, w_pool, pool_scale, w_out, ln_post_mix, ln_pre_ffn, w_gate, w_up, w_down, ln_post_ffn, loss_target, m_ln_pre_mix, m_w_in, m_w_pool, m_pool_scale, m_w_out, m_ln_post_mix, m_ln_pre_ffn, m_w_gate, m_w_up, m_w_down, m_ln_post_ffn, v_ln_pre_mix, v_w_in, v_w_pool, v_pool_scale, v_w_out, v_ln_post_mix, v_ln_pre_ffn, v_w_gate, v_w_up, v_w_down, v_ln_post_ffn):
    shards = [w_in[0].T.astype(BF16), w_out[0].astype(BF16), w_gate[0].T.astype(BF16),
              w_up[0].T.astype(BF16), w_down[0].astype(BF16)]
    flat = lambda a: a.reshape(-1, D_MODEL)
    blocks = lambda a: a.reshape(N_DEV, -1, D_MODEL)
    step = _LocalStep(x[0], loss_target[0], ln_pre_mix, w_pool[0], pool_scale, ln_post_mix, ln_pre_ffn, ln_post_ffn)

    w_in_t = flat(_gather_two_level(shards[0], "gather_w_in"))
    rest = _exchange_start(shards[1:], [False] * 4, w_in_t, "gather_rest_start")
    attn = step.mixer_fwd(w_in_t, rest[4])
    srcs, lands = _exchange_wait(rest, [False] * 4, attn, "gather_rest_wait")
    w_out_f, wg_t, wu_t, w_down_f = [flat(a) for a in _fill_own(lands, srcs, [False] * 4)]

    ffn = _exchange_start([blocks(a) for a in step.ffn_fwd_bwd(w_out_f, wg_t, wu_t, w_down_f)], [True] * 3,
                          step.dgate, "grads_ffn_start")
    mixer = _exchange_start([blocks(a) for a in step.mixer_bwd(ffn[4])], [True] * 2, step.dproj, "grads_mixer_start")
    loss, grad_x, small = step.input_bwd(mixer[4])
    got = []
    for handle, n_arr, nm in ((mixer, 2, "grads_mixer"), (ffn, 3, "grads_ffn")):
        srcs, lands = _exchange_wait(handle, [True] * n_arr, grad_x, nm + "_wait")
        got += _fill_own(lands, srcs, [True] * n_arr)
    sums = [_slot_sum(got[i], f"sum_grad_{i}", got[i].shape[1] // 2) for i in range(5)]

    small_buf = _pack_small(small[0], small[1], small[2], small[3], small[4], small[5], loss)
    small_sum = _slot_sum(_exchange([small_buf], [False], "gather_small")[0], "sum_small", SMALL_ROWS)

    g_in, g_out, g_gate, g_up, g_down = sums[0].T, sums[1], sums[2].T, sums[3].T, sums[4]
    upd = [_adamw(w[0], g, m[0], v[0], f"adamw_{nm}") for nm, w, g, m, v in (
        ("in", w_in, g_in, m_w_in, v_w_in), ("out", w_out, g_out, m_w_out, v_w_out),
        ("gate", w_gate, g_gate, m_w_gate, v_w_gate), ("up", w_up, g_up, m_w_up, v_w_up),
        ("down", w_down, g_down, m_w_down, v_w_down))]
    pack = lambda a, b, c, d, e, f: _pack_small(a, b[0], c, d, e, f, jnp.zeros((1,), F32))
    small_upd = _adamw(
        pack(ln_pre_mix, w_pool, pool_scale, ln_post_mix, ln_pre_ffn, ln_post_ffn), small_sum,
        pack(m_ln_pre_mix, m_w_pool, m_pool_scale, m_ln_post_mix, m_ln_pre_ffn, m_ln_post_ffn),
        pack(v_ln_pre_mix, v_w_pool, v_pool_scale, v_ln_post_mix, v_ln_pre_ffn, v_ln_post_ffn), "adamw_small")

    def tree(small6, big5):
        s1, spool, sscale, s2, s3, s4 = small6
        b_in, b_out, b_gate, b_up, b_down = [b[None] for b in big5]
        return [s1, b_in, spool, sscale, b_out, s2, s3, b_gate, b_up, b_down, s4]

    g_small = _unpack_small(small_sum)
    outs = [g_small[6][0], grad_x[None]]
    outs += tree(g_small[:6], [g_in, g_out, g_gate, g_up, g_down])
    for j in range(3):
        outs += tree(_unpack_small(small_upd[j])[:6], [u[j] for u in upd])
    return tuple(outs)
```

```python
import jax
import jax.numpy as jnp
from jax import lax
from jax.experimental import pallas as pl
from jax.experimental.pallas import tpu as pltpu

F32 = jnp.float32
BF16 = jnp.bfloat16

D_MODEL = 1024
POOL_W = 256
ATTN_W = 768
IN_W = 2560
D_FF = 2816
POOL_WINDOWS = (2, 4, 8, 16)
POOL_HALO = 16
DILATIONS = (1, 4, 16)
BLK = 128
LANES = 128
HEAD_DIM = 64
N_GROUPS = ATTN_W // LANES
ROPE_THETA = 10000.0
EPS = 1e-6
NEG = -1e30
N_DEV = 8
SMALL_ROWS = 24

ADAM_LR = 0.001
ADAM_B1 = 0.9
ADAM_B2 = 0.999
ADAM_EPS = 1e-08
ADAM_WD = 0.01
ADAM_STEP = 10

VMEM_LIMIT = 56 * 1024 * 1024


def _dot(a, b):
    return jnp.dot(a, b, preferred_element_type=F32)


def _dot_nt(a, b):
    return lax.dot_general(a, b, (((1,), (1,)), ((), ())), preferred_element_type=F32)


def _dot_tn(a, b):
    return lax.dot_general(a, b, (((0,), (0,)), ((), ())), preferred_element_type=F32)


def _params(n_grid):
    return pltpu.CompilerParams(dimension_semantics=("arbitrary",) * n_grid, vmem_limit_bytes=VMEM_LIMIT)


def _tok(tm, c):
    return pl.BlockSpec((tm, c), lambda i: (i, 0))


def _res(shape):
    return pl.BlockSpec(shape, lambda i: (0,) * len(shape), pipeline_mode=pl.Buffered(1))


def _acc(shape):
    return pl.BlockSpec(shape, lambda i: (0,) * len(shape))


def _rms_fwd(x, g):
    r = lax.rsqrt(jnp.mean(x * x, axis=-1, keepdims=True) + EPS)
    return x * r * g


def _rms_bwd(x, g, dy):
    r = lax.rsqrt(jnp.mean(x * x, axis=-1, keepdims=True) + EPS)
    xh = x * r
    gd = dy * g
    dx = r * (gd - xh * jnp.mean(gd * xh, axis=-1, keepdims=True))
    return dx, jnp.sum(dy * xh, axis=0, keepdims=True)


def _rope(x, c, s, sign):
    lane = lax.broadcasted_iota(jnp.int32, (x.shape[0], LANES), 1)
    first = (lane % HEAD_DIM) < (HEAD_DIM // 2)
    outs = []
    for g in range(x.shape[1] // LANES):
        xg = x[:, g * LANES:(g + 1) * LANES]
        rot = jnp.where(first, pltpu.roll(xg, LANES - HEAD_DIM // 2, 1), pltpu.roll(xg, HEAD_DIM // 2, 1))
        outs.append(xg * c + sign * (rot * s))
    return jnp.concatenate(outs, axis=1)


def _proj_fwd(x, g1, w_in_t, cos, sin, tm):
    T = x.shape[0]

    def body(x_ref, g_ref, w_ref, c_ref, s_ref, h_ref, u_ref, q_ref, k_ref, v_ref):
        h = _rms_fwd(x_ref[...], g_ref[...]).astype(BF16)
        h_ref[...] = h
        proj = _dot_nt(h, w_ref[...])
        c = c_ref[...]
        s = s_ref[...]
        u_ref[...] = proj[:, :POOL_W]
        _store_packed(q_ref, _rope(proj[:, POOL_W:POOL_W + ATTN_W], c, s, 1.0))
        _store_packed(k_ref, _rope(proj[:, POOL_W + ATTN_W:POOL_W + 2 * ATTN_W], c, s, 1.0))
        _store_packed(v_ref, proj[:, POOL_W + 2 * ATTN_W:])

    return pl.pallas_call(
        body, name="proj_fwd", grid=(T // tm,),
        in_specs=[_tok(tm, D_MODEL), _res((1, D_MODEL)), _res((IN_W, D_MODEL)), _tok(tm, LANES), _tok(tm, LANES)],
        out_specs=[_tok(tm, D_MODEL), _tok(tm, POOL_W)] + [_tok_packed(tm, ATTN_W)] * 3,
        out_shape=[jax.ShapeDtypeStruct((T, D_MODEL), BF16), jax.ShapeDtypeStruct((T, POOL_W), F32)]
        + [_packed(T, ATTN_W)] * 3,
        compiler_params=_params(1),
    )(x, g1, w_in_t, cos, sin)


def _pool_window(lane):
    return jnp.where(lane < 64, 2, jnp.where(lane < 128, 4, jnp.where(lane < 192, 8, 16)))


def _pool_select(lane, a2, a4, a8, a16):
    return jnp.where(lane < 64, a2, jnp.where(lane < 128, a4, jnp.where(lane < 192, a8, a16)))


def _pool_delta(cur, prev, i, tm):
    prev = jnp.where(i > 0, prev, 0.0)
    ext = jnp.concatenate([prev, cur], axis=0)
    s2 = ext + pltpu.roll(ext, 1, 0)
    s4 = s2 + pltpu.roll(s2, 2, 0)
    s8 = s4 + pltpu.roll(s4, 4, 0)
    s16 = s8 + pltpu.roll(s8, 8, 0)
    lane = lax.broadcasted_iota(jnp.int32, (tm, POOL_W), 1)
    row = lax.broadcasted_iota(jnp.int32, (tm, POOL_W), 0) + i * tm
    ws = _pool_select(lane, s2[POOL_HALO:], s4[POOL_HALO:], s8[POOL_HALO:], s16[POOL_HALO:])
    cnt = jnp.minimum(row + 1, _pool_window(lane)).astype(F32)
    return ws / cnt - cur


def _pool_fwd(u, wbd, scale, tm):
    T = u.shape[0]
    hb = tm // POOL_HALO

    def body(u_ref, prev_ref, w_ref, sc_ref, o_ref):
        d = _pool_delta(u_ref[...], prev_ref[...], pl.program_id(0), tm)
        o_ref[...] = (_dot(d.astype(BF16), w_ref[...]) * sc_ref[...]).astype(BF16)

    return pl.pallas_call(
        body, name="pool_fwd", grid=(T // tm,),
        in_specs=[_tok(tm, POOL_W), pl.BlockSpec((POOL_HALO, POOL_W), lambda i: (jnp.maximum(i * hb - 1, 0), 0)),
                  _res((POOL_W, POOL_W)), _res((1, POOL_W))],
        out_specs=_tok(tm, POOL_W),
        out_shape=jax.ShapeDtypeStruct((T, POOL_W), BF16),
        compiler_params=_params(1),
    )(u, u, wbd, scale)


def _attn_mask(n):
    qi = lax.broadcasted_iota(jnp.int32, (BLK, 2 * BLK), 0)
    kj = lax.broadcasted_iota(jnp.int32, (BLK, 2 * BLK), 1)
    dist = qi + BLK - kj
    return (dist >= 0) & (dist <= BLK) & ((kj >= BLK) | (n > 0))


def _stack_heads(x, lo):
    zero = jnp.zeros_like(x)
    return jnp.concatenate([jnp.where(lo, x, zero), jnp.where(lo, zero, x)], axis=0)


def _head_col(tile, lane, h):
    return jnp.sum(jnp.where(lane == h, tile, 0.0), axis=1, keepdims=True)


def _attn_cols(dil):
    return ATTN_W // 2 if dil >= 16 else ATTN_W


def _stream_pairs_per_step(dil):
    return {1: 1, 4: 2, 16: 4}[dil]


def _attn_specs(dil, nb):
    cw = _attn_cols(dil)
    ch = BLK * dil
    wide = lambda f: pl.BlockSpec((cw // LANES, ch // 2, LANES), f)
    full = pl.BlockSpec((cw // LANES, ch, LANES), lambda n, j, r: (j, n, 0))
    cur = lambda n, j, r: (j, n, 0)
    prv = lambda n, j, r: (j, jnp.maximum(n - 1, 0), 0)
    prv_out = lambda n, j, r: (j, (n + nb - 1) % nb, 0)
    heads = pl.BlockSpec((ch, LANES), lambda n, j, r: (n, 0))
    return cw, wide(cur), wide(prv), wide(prv_out), heads, full


HIGH_HALF = 0xFFFF0000


def _pack(x):
    return pltpu.bitcast(x.astype(BF16), F32)


def _unpack(words):
    return pltpu.bitcast(words, BF16)


def _packed(rows, cols):
    return jax.ShapeDtypeStruct((cols // LANES, rows // 2, LANES), F32)


def _tok_packed(tm, cols):
    return pl.BlockSpec((cols // LANES, tm // 2, LANES), lambda i: (0, i, 0))


def _store_packed(ref, x):
    for g in range(x.shape[1] // LANES):
        ref[g] = _pack(x[:, g * LANES:(g + 1) * LANES])


def _load_packed(ref):
    return jnp.concatenate([_unpack(ref[g]) for g in range(ref.shape[0])], axis=1)


def _load_streams(ref, dil, r2, sl):
    if dil == 1:
        return [_unpack(ref[sl])]
    words = lax.bitcast_convert_type(ref.at[sl][pl.ds(r2, BLK, stride=dil // 2), :], jnp.uint32)
    even = lax.bitcast_convert_type(words << 16, F32).astype(BF16)
    odd = lax.bitcast_convert_type(words & jnp.uint32(HIGH_HALF), F32).astype(BF16)
    return [even, odd]


def _load_streams_f32(ref, dil, r2, sl):
    ref = ref if sl is None else ref.at[sl]
    if dil == 1:
        return [ref[...]]
    return [ref[pl.ds(2 * r2 + e, BLK, stride=dil), :] for e in range(2)]


def _store_streams_f32(ref, dil, r2, sl, tiles):
    ref = ref if sl is None else ref.at[sl]
    if dil == 1:
        ref[...] = tiles[0]
    else:
        for e, t in enumerate(tiles):
            ref[pl.ds(2 * r2 + e, BLK, stride=dil), :] = t


def _store_streams(ref, dil, r2, sl, tiles):
    if dil == 1:
        ref[sl] = _pack(tiles[0])
    else:
        even, odd = [lax.bitcast_convert_type(t.astype(BF16).astype(F32), jnp.uint32) for t in tiles]
        words = (odd & jnp.uint32(HIGH_HALF)) | (even >> 16)
        ref.at[sl][pl.ds(r2, BLK, stride=dil // 2), :] = lax.bitcast_convert_type(words, F32)


def _attn_fwd(q, k, v, dil, prev, last):
    T = 2 * q.shape[1]
    nb = T // (BLK * dil)
    first = prev is None
    cw, cur, prv, _, heads, full = _attn_specs(dil, nb)
    ncb = ATTN_W // cw
    heads_per_step = cw // HEAD_DIM
    n_str = min(dil, 2)
    reps = _stream_pairs_per_step(dil)
    everything = None

    def body(*refs):
        if first:
            q_ref, kc_ref, kp_ref, vc_ref, vp_ref, acc_ref, lse_ref = refs
        else:
            q_ref, kc_ref, kp_ref, vc_ref, vp_ref, acc_in, lse_in, acc_ref, lse_ref = refs
        j = pl.program_id(1)
        valid = _attn_mask(pl.program_id(0))
        lane = lax.broadcasted_iota(jnp.int32, (BLK, LANES), 1)
        lo = lane < HEAD_DIM
        store_acc = _store_streams if last else _store_streams_f32

        def stream_pair(r2):
            lse_tiles = [jnp.zeros((BLK, LANES), F32) for _ in range(n_str)]
            own = []
            for g in range(cw // LANES):
                qs, kcs, kps, vcs, vps = [_load_streams(r, dil, r2, g)
                                          for r in (q_ref, kc_ref, kp_ref, vc_ref, vp_ref)]
                pairs = []
                for e in range(n_str):
                    qg = qs[e] * 0.125
                    kcat = jnp.concatenate([kps[e], kcs[e]], axis=0)
                    vcat = jnp.concatenate([vps[e], vcs[e]], axis=0)
                    pair = None
                    for hh in range(2):
                        h = j * heads_per_step + 2 * g + hh
                        hm = lo if hh == 0 else jnp.logical_not(lo)
                        s = _dot_nt(jnp.where(hm, qg, jnp.zeros_like(qg)), kcat)
                        s = jnp.where(valid, s, NEG)
                        m = jnp.max(s, axis=1, keepdims=True)
                        p = jnp.exp(s - m)
                        den = jnp.sum(p, axis=1, keepdims=True)
                        o = _dot(p.astype(BF16), vcat) / den
                        pair = o if hh == 0 else jnp.where(lo, pair, o)
                        lse_tiles[e] = jnp.where(lane == h, m + jnp.log(den), lse_tiles[e])
                    pairs.append(pair)
                if first:
                    store_acc(acc_ref, dil, r2, g, pairs)
                else:
                    own.append(pairs)
            if not first:
                mine = (lane >= j * heads_per_step) & (lane < (j + 1) * heads_per_step)
                before = _load_streams_f32(lse_in, dil, r2, everything)
                w_before, w_own = [], []
                for e in range(n_str):
                    mx = jnp.maximum(before[e], lse_tiles[e])
                    total = mx + jnp.log(jnp.exp(before[e] - mx) + jnp.exp(lse_tiles[e] - mx))
                    w_before.append(jnp.exp(before[e] - total))
                    w_own.append(jnp.exp(lse_tiles[e] - total))
                    lse_tiles[e] = jnp.where(mine, total, 0.0)
                for g in range(cw // LANES):
                    h0 = j * heads_per_step + 2 * g
                    spread = lambda w: jnp.where(lo, _head_col(w, lane, h0), _head_col(w, lane, h0 + 1))
                    olds = _load_streams_f32(acc_in, dil, r2, g)
                    store_acc(acc_ref, dil, r2, g, [olds[e] * spread(w_before[e]) + own[g][e] * spread(w_own[e])
                                                    for e in range(n_str)])
            if ncb == 1:
                _store_streams_f32(lse_ref, dil, r2, everything, lse_tiles)
            else:
                @pl.when(j == 0)
                def _():
                    _store_streams_f32(lse_ref, dil, r2, everything, lse_tiles)

                @pl.when(j > 0)
                def _():
                    before = _load_streams_f32(lse_ref, dil, r2, everything)
                    _store_streams_f32(lse_ref, dil, r2, everything, [a + b for a, b in zip(before, lse_tiles)])

        for rep in range(reps):
            stream_pair(pl.program_id(2) * reps + rep)

    ins = [q, k, k, v, v]
    in_specs = [cur, cur, prv, cur, prv]
    if not first:
        ins += [prev[0], prev[1]]
        in_specs += [full, heads]
    return pl.pallas_call(
        body, name=f"attn_fwd_d{dil}", grid=(nb, ncb, max(dil // 2, 1) // reps),
        in_specs=in_specs, out_specs=[cur if last else full, heads],
        out_shape=[_packed(T, ATTN_W) if last else jax.ShapeDtypeStruct((N_GROUPS, T, LANES), F32),
                   jax.ShapeDtypeStruct((T, LANES), F32)],
        compiler_params=_params(3),
    )(*ins)


def _mix_fwd(pool, attn, x, w_out, g2, g3, tm):
    T = x.shape[0]

    def body(p_ref, a_ref, x_ref, w_ref, g2_ref, g3_ref, cat_ref, mix_ref, x2_ref, h2_ref):
        p = p_ref[...]
        a = _load_packed(a_ref)
        cat_ref[...] = jnp.concatenate([p, a], axis=1)
        mix = _dot(p, w_ref[:POOL_W, :]) + _dot(a, w_ref[POOL_W:, :])
        mix_ref[...] = mix
        x2 = x_ref[...] + _rms_fwd(mix, g2_ref[...])
        x2_ref[...] = x2
        h2_ref[...] = _rms_fwd(x2, g3_ref[...]).astype(BF16)

    return pl.pallas_call(
        body, name="mix_fwd", grid=(T // tm,),
        in_specs=[_tok(tm, POOL_W), _tok_packed(tm, ATTN_W), _tok(tm, D_MODEL), _res((D_MODEL, D_MODEL)),
                  _res((1, D_MODEL)), _res((1, D_MODEL))],
        out_specs=[_tok(tm, D_MODEL)] * 4,
        out_shape=[jax.ShapeDtypeStruct((T, D_MODEL), BF16), jax.ShapeDtypeStruct((T, D_MODEL), F32),
                   jax.ShapeDtypeStruct((T, D_MODEL), F32), jax.ShapeDtypeStruct((T, D_MODEL), BF16)],
        compiler_params=_params(1),
    )(pool, attn, x, w_out, g2, g3)


def _ffn_up(h2, wg_t, wu_t, tm):
    T = h2.shape[0]

    def body(h_ref, wg_ref, wu_ref, dg_ref, du_ref, a_ref):
        h = h_ref[...]
        gate = _dot_nt(h, wg_ref[...])
        up = _dot_nt(h, wu_ref[...])
        sg = 1.0 / (1.0 + jnp.exp(-gate))
        silu = gate * sg
        a_ref[...] = (silu * up).astype(BF16)
        dg_ref[...] = (up * (sg * (1.0 + gate * (1.0 - sg)))).astype(BF16)
        du_ref[...] = silu.astype(BF16)

    return pl.pallas_call(
        body, name="ffn_up", grid=(T // tm,),
        in_specs=[_tok(tm, D_MODEL), _res((D_FF, D_MODEL)), _res((D_FF, D_MODEL))],
        out_specs=[_tok(tm, D_FF)] * 3,
        out_shape=[jax.ShapeDtypeStruct((T, D_FF), BF16)] * 3,
        compiler_params=_params(1),
    )(h2, wg_t, wu_t)


def _ffn_down_loss(act, w_down, x2, g4, tgt, tm):
    T = act.shape[0]

    def body(a_ref, w_ref, x2_ref, g_ref, t_ref, df_ref, dy_ref, dg_ref, loss_ref):
        i = pl.program_id(0)

        @pl.when(i == 0)
        def _():
            dg_ref[...] = jnp.zeros_like(dg_ref)
            loss_ref[...] = jnp.zeros_like(loss_ref)

        f = _dot(a_ref[...], w_ref[...])
        g = g_ref[...]
        err = x2_ref[...] + _rms_fwd(f, g) - t_ref[...]
        loss_ref[...] += 0.5 * jnp.sum(jnp.mean(err * err, axis=-1, keepdims=True), axis=0, keepdims=True)
        dy = err * (1.0 / D_MODEL)
        dy_ref[...] = dy
        df, dg = _rms_bwd(f, g, dy)
        dg_ref[...] += dg
        df_ref[...] = df.astype(BF16)

    return pl.pallas_call(
        body, name="ffn_down_loss", grid=(T // tm,),
        in_specs=[_tok(tm, D_FF), _res((D_FF, D_MODEL)), _tok(tm, D_MODEL), _res((1, D_MODEL)), _tok(tm, D_MODEL)],
        out_specs=[_tok(tm, D_MODEL), _tok(tm, D_MODEL), _acc((1, D_MODEL)), _acc((1, 1))],
        out_shape=[jax.ShapeDtypeStruct((T, D_MODEL), BF16), jax.ShapeDtypeStruct((T, D_MODEL), F32),
                   jax.ShapeDtypeStruct((1, D_MODEL), F32), jax.ShapeDtypeStruct((1, 1), F32)],
        compiler_params=_params(1),
    )(act, w_down, x2, g4, tgt)


def _ffn_act_bwd(df, w_down, act_dgate, act_dup, tm):
    T = df.shape[0]

    def body(df_ref, w_ref, ag_ref, au_ref, dg_ref, du_ref):
        dact = _dot_nt(df_ref[...], w_ref[...])
        dg_ref[...] = (dact * ag_ref[...].astype(F32)).astype(BF16)
        du_ref[...] = (dact * au_ref[...].astype(F32)).astype(BF16)

    return pl.pallas_call(
        body, name="ffn_act_bwd", grid=(T // tm,),
        in_specs=[_tok(tm, D_MODEL), _res((D_FF, D_MODEL)), _tok(tm, D_FF), _tok(tm, D_FF)],
        out_specs=[_tok(tm, D_FF)] * 2,
        out_shape=[jax.ShapeDtypeStruct((T, D_FF), BF16)] * 2,
        compiler_params=_params(1),
    )(df, w_down, act_dgate, act_dup)


def _ffn_in_bwd(dgate, dup, wg_t, wu_t, x2, mix, dy, g3, g2, tm):
    T = x2.shape[0]

    def body(dg_ref, du_ref, wg_ref, wu_ref, x2_ref, mix_ref, dy_ref, g3_ref, g2_ref,
             dx2_ref, dmix_ref, dg3_ref, dg2_ref):
        @pl.when(pl.program_id(0) == 0)
        def _():
            dg3_ref[...] = jnp.zeros_like(dg3_ref)
            dg2_ref[...] = jnp.zeros_like(dg2_ref)

        dh2 = _dot(dg_ref[...], wg_ref[...]) + _dot(du_ref[...], wu_ref[...])
        dn, dg3 = _rms_bwd(x2_ref[...], g3_ref[...], dh2)
        dx2 = dy_ref[...] + dn
        dx2_ref[...] = dx2
        dg3_ref[...] += dg3
        dmix, dg2 = _rms_bwd(mix_ref[...], g2_ref[...], dx2)
        dg2_ref[...] += dg2
        dmix_ref[...] = dmix.astype(BF16)

    return pl.pallas_call(
        body, name="ffn_in_bwd", grid=(T // tm,),
        in_specs=[_tok(tm, D_FF), _tok(tm, D_FF), _res((D_FF, D_MODEL)), _res((D_FF, D_MODEL)),
                  _tok(tm, D_MODEL), _tok(tm, D_MODEL), _tok(tm, D_MODEL), _res((1, D_MODEL)), _res((1, D_MODEL))],
        out_specs=[_tok(tm, D_MODEL), _tok(tm, D_MODEL), _acc((1, D_MODEL)), _acc((1, D_MODEL))],
        out_shape=[jax.ShapeDtypeStruct((T, D_MODEL), F32), jax.ShapeDtypeStruct((T, D_MODEL), BF16),
                   jax.ShapeDtypeStruct((1, D_MODEL), F32), jax.ShapeDtypeStruct((1, D_MODEL), F32)],
        compiler_params=_params(1),
    )(dgate, dup, wg_t, wu_t, x2, mix, dy, g3, g2)


def _mix_bwd(dmix, w_out, tm):
    T = dmix.shape[0]

    def body(d_ref, w_ref, dp_ref, da_ref):
        dcat = _dot_nt(d_ref[...], w_ref[...])
        dp_ref[...] = dcat[:, :POOL_W].astype(BF16)
        _store_packed(da_ref, dcat[:, POOL_W:])

    return pl.pallas_call(
        body, name="mix_bwd", grid=(T // tm,),
        in_specs=[_tok(tm, D_MODEL), _res((D_MODEL, D_MODEL))],
        out_specs=[_tok(tm, POOL_W), _tok_packed(tm, ATTN_W)],
        out_shape=[jax.ShapeDtypeStruct((T, POOL_W), BF16), _packed(T, ATTN_W)],
        compiler_params=_params(1),
    )(dmix, w_out)


def _attn_bwd(q, k, v, dout, out, lse, dil):
    T = 2 * q.shape[1]
    nb = T // (BLK * dil)
    cw, cur, prv, prv_out, heads, _ = _attn_specs(dil, nb)
    ncb = ATTN_W // cw
    heads_per_step = cw // HEAD_DIM
    n_str = min(dil, 2)
    reps = _stream_pairs_per_step(dil)

    def body(q_ref, kc_ref, kp_ref, vc_ref, vp_ref, do_ref, o_ref, lse_ref,
             dq_ref, dkc_ref, dkp_ref, dvc_ref, dvp_ref):
        j = pl.program_id(1)
        valid = _attn_mask(pl.program_id(0))
        lane = lax.broadcasted_iota(jnp.int32, (BLK, LANES), 1)
        lo = lane < HEAD_DIM
        valid2 = jnp.concatenate([valid, valid], axis=0)
        for rep, g in [(rep, g) for rep in range(reps) for g in range(cw // LANES)]:
            r2 = pl.program_id(2) * reps + rep
            sl = g
            lse_tiles = _load_streams_f32(lse_ref, dil, r2, None)
            qs, kcs, kps, vcs, vps, dos, os_ = [
                _load_streams(r, dil, r2, sl) for r in (q_ref, kc_ref, kp_ref, vc_ref, vp_ref, do_ref, o_ref)]
            dqs, dks, dvs = [], [], []
            for e in range(n_str):
                qg = qs[e] * 0.125
                dog = dos[e]
                kcat = jnp.concatenate([kps[e], kcs[e]], axis=0)
                vcat = jnp.concatenate([vps[e], vcs[e]], axis=0)
                prod = dog.astype(F32) * os_[e].astype(F32)
                h0 = j * heads_per_step + 2 * g
                q2 = _stack_heads(qg, lo)
                do2 = _stack_heads(dog, lo)
                lse2 = jnp.concatenate([_head_col(lse_tiles[e], lane, h0), _head_col(lse_tiles[e], lane, h0 + 1)], axis=0)
                dsum2 = jnp.concatenate([jnp.sum(jnp.where(lo, prod, 0.0), axis=1, keepdims=True),
                                         jnp.sum(jnp.where(lo, 0.0, prod), axis=1, keepdims=True)], axis=0)
                p = jnp.exp(jnp.where(valid2, _dot_nt(q2, kcat), NEG) - lse2)
                ds = (p * (_dot_nt(do2, vcat) - dsum2)).astype(BF16)
                dvs.append(_dot_tn(p.astype(BF16), do2))
                dks.append(_dot_tn(ds, q2))
                dq2 = _dot(ds, kcat) * 0.125
                dqs.append(jnp.where(lo, dq2[:BLK], dq2[BLK:]))
            _store_streams(dq_ref, dil, r2, sl, dqs)
            _store_streams(dkp_ref, dil, r2, sl, [t[:BLK] for t in dks])
            _store_streams(dkc_ref, dil, r2, sl, [t[BLK:] for t in dks])
            _store_streams(dvp_ref, dil, r2, sl, [t[:BLK] for t in dvs])
            _store_streams(dvc_ref, dil, r2, sl, [t[BLK:] for t in dvs])

    return pl.pallas_call(
        body, name=f"attn_bwd_d{dil}", grid=(nb, ncb, max(dil // 2, 1) // reps),
        in_specs=[cur, cur, prv, cur, prv, cur, cur, heads],
        out_specs=[cur, cur, prv_out, cur, prv_out],
        out_shape=[_packed(T, ATTN_W)] * 5,
        compiler_params=_params(3),
    )(q, k, k, v, v, dout, out, lse)


def _pool_bwd(u, dy, wbd, scale, tm):
    T = u.shape[0]
    nt = T // tm
    hb = tm // POOL_HALO

    def body(u_ref, prev_ref, dy_ref, next_ref, w_ref, sc_ref, du_ref, dw_ref, dsc_ref):
        i = pl.program_id(0)

        @pl.when(i == 0)
        def _():
            dw_ref[...] = jnp.zeros_like(dw_ref)
            dsc_ref[...] = jnp.zeros_like(dsc_ref)

        w = w_ref[...]
        sc = sc_ref[...]
        d = _pool_delta(u_ref[...], prev_ref[...], i, tm).astype(BF16)
        dyc = dy_ref[...].astype(F32)
        dsc_ref[...] += jnp.sum(dyc * _dot(d, w), axis=0, keepdims=True)
        nxt = jnp.where(i < nt - 1, next_ref[...].astype(F32), 0.0)
        dypre = (jnp.concatenate([dyc, nxt], axis=0) * sc).astype(BF16)
        dw_ref[...] += _dot_tn(d, dypre[:tm])
        dd = _dot_nt(dypre, w)
        n = tm + POOL_HALO
        lane = lax.broadcasted_iota(jnp.int32, (n, POOL_W), 1)
        row = lax.broadcasted_iota(jnp.int32, (n, POOL_W), 0) + i * tm
        gx = dd / jnp.minimum(row + 1, _pool_window(lane)).astype(F32)
        a2 = gx + pltpu.roll(gx, n - 1, 0)
        a4 = a2 + pltpu.roll(a2, n - 2, 0)
        a8 = a4 + pltpu.roll(a4, n - 4, 0)
        a16 = a8 + pltpu.roll(a8, n - 8, 0)
        fs = _pool_select(lane[:tm], a2[:tm], a4[:tm], a8[:tm], a16[:tm])
        du_ref[...] = (fs - dd[:tm]).astype(BF16)

    return pl.pallas_call(
        body, name="pool_bwd", grid=(nt,),
        in_specs=[_tok(tm, POOL_W), pl.BlockSpec((POOL_HALO, POOL_W), lambda i: (jnp.maximum(i * hb - 1, 0), 0)),
                  _tok(tm, POOL_W), pl.BlockSpec((POOL_HALO, POOL_W), lambda i: (jnp.minimum((i + 1) * hb, nt * hb - 1), 0)),
                  _res((POOL_W, POOL_W)), _res((1, POOL_W))],
        out_specs=[_tok(tm, POOL_W), _acc((POOL_W, POOL_W)), _acc((1, POOL_W))],
        out_shape=[jax.ShapeDtypeStruct((T, POOL_W), BF16), jax.ShapeDtypeStruct((POOL_W, POOL_W), F32),
                   jax.ShapeDtypeStruct((1, POOL_W), F32)],
        compiler_params=_params(1),
    )(u, u, dy, dy, wbd, scale)


def _dproj_combine(du, dqs, dkcs, dkps, dvcs, dvps, cos, sin, tm):
    T = du.shape[0]
    n_cfg = len(dqs)

    def body(*refs):
        du_ref = refs[0]
        groups = [refs[1 + j * n_cfg:1 + (j + 1) * n_cfg] for j in range(5)]
        c_ref, s_ref, out_ref = refs[1 + 5 * n_cfg:]
        tot = lambda rs: sum(_load_packed(r).astype(F32) for r in rs)
        c = c_ref[...]
        s = s_ref[...]
        dq = _rope(tot(groups[0]), c, s, -1.0)
        dk = _rope(tot(groups[1]) + tot(groups[2]), c, s, -1.0)
        dv = tot(groups[3]) + tot(groups[4])
        out_ref[...] = jnp.concatenate([du_ref[...], dq.astype(BF16), dk.astype(BF16), dv.astype(BF16)], axis=1)

    return pl.pallas_call(
        body, name="dproj_combine", grid=(T // tm,),
        in_specs=[_tok(tm, POOL_W)] + [_tok_packed(tm, ATTN_W)] * (5 * n_cfg) + [_tok(tm, LANES)] * 2,
        out_specs=_tok(tm, IN_W),
        out_shape=jax.ShapeDtypeStruct((T, IN_W), BF16),
        compiler_params=_params(1),
    )(du, *dqs, *dkcs, *dkps, *dvcs, *dvps, cos, sin)


def _proj_bwd(dproj, w_in_t, x, dx2, g1, tm):
    T = x.shape[0]

    def body(d_ref, w_ref, x_ref, r_ref, g_ref, dx_ref, dg_ref):
        @pl.when(pl.program_id(0) == 0)
        def _():
            dg_ref[...] = jnp.zeros_like(dg_ref)

        dn, dg = _rms_bwd(x_ref[...], g_ref[...], _dot(d_ref[...], w_ref[...]))
        dg_ref[...] += dg
        dx_ref[...] = r_ref[...] + dn

    return pl.pallas_call(
        body, name="proj_bwd", grid=(T // tm,),
        in_specs=[_tok(tm, IN_W), _res((IN_W, D_MODEL)), _tok(tm, D_MODEL), _tok(tm, D_MODEL), _res((1, D_MODEL))],
        out_specs=[_tok(tm, D_MODEL), _acc((1, D_MODEL))],
        out_shape=[jax.ShapeDtypeStruct((T, D_MODEL), F32), jax.ShapeDtypeStruct((1, D_MODEL), F32)],
        compiler_params=_params(1),
    )(dproj, w_in_t, x, dx2, g1)


def _wgrad(a, b, name, tile_m, tk):
    T, M = a.shape
    N = b.shape[1]
    nk = T // tk

    def body(a_ref, b_ref, o_ref, acc_ref):
        kk = pl.program_id(1)

        @pl.when(kk == 0)
        def _():
            acc_ref[...] = jnp.zeros_like(acc_ref)

        acc_ref[...] += _dot_tn(a_ref[...], b_ref[...])

        @pl.when(kk == nk - 1)
        def _():
            o_ref[...] = acc_ref[...].astype(BF16)

    return pl.pallas_call(
        body, name=name, grid=(M // tile_m, nk),
        in_specs=[pl.BlockSpec((tk, tile_m), lambda j, kk: (kk, j)), pl.BlockSpec((tk, N), lambda j, kk: (kk, 0))],
        out_specs=pl.BlockSpec((tile_m, N), lambda j, kk: (j, 0)),
        out_shape=jax.ShapeDtypeStruct((M, N), BF16),
        scratch_shapes=[pltpu.VMEM((tile_m, N), F32)],
        compiler_params=_params(2),
    )(a, b)


def _exchange(arrs, scatter, name):
    n = len(arrs)
    out_shapes = [jax.ShapeDtypeStruct((N_DEV,) + (a.shape[1:] if sc else a.shape), a.dtype)
                  for a, sc in zip(arrs, scatter)]

    def body(*refs):
        ins, outs = refs[:n], refs[n:2 * n]
        send_sems, recv_sems, loc_sems = refs[2 * n:]
        x, y, c = lax.axis_index("x"), lax.axis_index("y"), lax.axis_index("c")
        me = 4 * x + 2 * y + c
        local, sends, recvs = [], [], []
        for i in range(n):
            own = ins[i].at[me] if scatter[i] else ins[i]
            loc = pltpu.make_async_copy(own, outs[i].at[me], loc_sems.at[i])
            loc.start()
            local.append(loc)
            for kbits in range(1, N_DEV):
                px = 1 - x if kbits & 4 else x
                py = 1 - y if kbits & 2 else y
                pc = 1 - c if kbits & 1 else c
                pid = 4 * px + 2 * py + pc
                src = ins[i].at[pid] if scatter[i] else ins[i]
                cp = pltpu.make_async_remote_copy(
                    src_ref=src, dst_ref=outs[i].at[me],
                    send_sem=send_sems.at[i, kbits - 1], recv_sem=recv_sems.at[i, kbits - 1],
                    device_id=(px, py, pc), device_id_type=pl.DeviceIdType.MESH)
                cp.start()
                sends.append(cp)
                recvs.append(pltpu.make_async_remote_copy(
                    src_ref=src, dst_ref=outs[i].at[pid],
                    send_sem=send_sems.at[i, kbits - 1], recv_sem=recv_sems.at[i, kbits - 1],
                    device_id=(px, py, pc), device_id_type=pl.DeviceIdType.MESH))
        for cp in recvs:
            cp.wait_recv()
        for cp in sends:
            cp.wait_send()
        for cp in local:
            cp.wait()

    hbm = pl.BlockSpec(memory_space=pl.ANY)
    return pl.pallas_call(
        body, name=name, in_specs=[hbm] * n, out_specs=[hbm] * n, out_shape=out_shapes,
        scratch_shapes=[pltpu.SemaphoreType.DMA((n, N_DEV - 1)), pltpu.SemaphoreType.DMA((n, N_DEV - 1)),
                        pltpu.SemaphoreType.DMA((n,))],
    )(*arrs)


def _gather_two_level(arr, name):
    def body(x_ref, out_ref, send_sems, recv_sems, local_sem):
        x, y, c = lax.axis_index("x"), lax.axis_index("y"), lax.axis_index("c")
        me, sibling = (x, y, c), (x, y, 1 - c)
        chips = [(1 - x, y), (x, 1 - y), (1 - x, 1 - y)]
        slot = lambda px, py, pc: out_ref.at[4 * px + 2 * py + pc]

        def copy(k, block, to, src=None):
            return pltpu.make_async_remote_copy(
                src_ref=slot(*block) if src is None else src, dst_ref=slot(*block),
                send_sem=send_sems.at[k], recv_sem=recv_sems.at[k],
                device_id=to, device_id_type=pl.DeviceIdType.MESH)

        mine = pltpu.make_async_copy(x_ref, slot(*me), local_sem)
        mine.start()
        first = [copy(0, me, sibling, src=x_ref)]
        first += [copy(1 + i, me, (*chip, c), src=x_ref) for i, chip in enumerate(chips)]
        for cp in first:
            cp.start()
        passed = [copy(4 + i, (*chip, c), sibling) for i, chip in enumerate(chips)]
        for i, chip in enumerate(chips):
            copy(1 + i, (*chip, c), me).wait_recv()
            passed[i].start()
        copy(0, sibling, me).wait_recv()
        for i, chip in enumerate(chips):
            copy(4 + i, (*chip, 1 - c), me).wait_recv()
        for cp in first + passed:
            cp.wait_send()
        mine.wait()

    hbm = pl.BlockSpec(memory_space=pl.ANY)
    return pl.pallas_call(
        body, name=name, in_specs=[hbm], out_specs=hbm,
        out_shape=jax.ShapeDtypeStruct((N_DEV,) + arr.shape, arr.dtype),
        scratch_shapes=[pltpu.SemaphoreType.DMA((N_DEV - 1,)), pltpu.SemaphoreType.DMA((N_DEV - 1,)),
                        pltpu.SemaphoreType.DMA],
    )(arr)


def _peers(x, y, c):
    for kbits in range(1, N_DEV):
        px = 1 - x if kbits & 4 else x
        py = 1 - y if kbits & 2 else y
        pc = 1 - c if kbits & 1 else c
        yield kbits - 1, (px, py, pc), 4 * px + 2 * py + pc


def _peer_copies(ins, lands, scatter, send_sems, recv_sems, incoming):
    x, y, c = lax.axis_index("x"), lax.axis_index("y"), lax.axis_index("c")
    me = 4 * x + 2 * y + c
    copies = []
    for i in range(len(ins)):
        for k, peer, pid in _peers(x, y, c):
            slot = i * (N_DEV - 1) + k
            copies.append(pltpu.make_async_remote_copy(
                src_ref=ins[i].at[pid] if scatter[i] else ins[i], dst_ref=lands[i].at[pid if incoming else me],
                send_sem=send_sems.at[slot], recv_sem=recv_sems.at[slot],
                device_id=peer, device_id_type=pl.DeviceIdType.MESH))
    return copies


_HBM = pl.BlockSpec(memory_space=pltpu.HBM)
_SEM = pl.BlockSpec(memory_space=pltpu.SEMAPHORE)
_DATAFLOW = pltpu.SideEffectType.DATAFLOW_SIDE_EFFECTING


def _exchange_start(arrs, scatter, after, name):
    n = len(arrs)
    lands = [lax.empty((N_DEV,) + (a.shape[1:] if sc else a.shape), a.dtype) for a, sc in zip(arrs, scatter)]

    def body(*refs):
        ins, lz = refs[:n], refs[n:2 * n]
        send_sems, recv_sems = refs[2 * n + 1:2 * n + 3]
        token = refs[-1]
        for cp in _peer_copies(ins, lz, scatter, send_sems, recv_sems, False):
            cp.start()
        token[...] = jnp.zeros_like(token)

    sem_shape = pltpu.SemaphoreType.DMA((n * (N_DEV - 1),))
    outs = pl.pallas_call(
        body, name=name,
        out_shape=(sem_shape, sem_shape, *[pltpu.HBM(a.shape, a.dtype) for a in arrs + lands],
                   jax.ShapeDtypeStruct((8, LANES), F32)),
        in_specs=[_HBM] * (2 * n) + [pl.BlockSpec(memory_space=pl.ANY)],
        out_specs=(_SEM, _SEM, *[_HBM] * (2 * n), pl.BlockSpec(memory_space=pltpu.VMEM)),
        input_output_aliases={i: 2 + i for i in range(2 * n)},
        compiler_params=pltpu.CompilerParams(has_side_effects=_DATAFLOW),
    )(*[pltpu.with_memory_space_constraint(a, pltpu.HBM) for a in arrs + lands], after)
    return outs[0], outs[1], list(outs[2:2 + n]), list(outs[2 + n:2 + 2 * n]), outs[-1]


def _exchange_wait(handle, scatter, after, name):
    send_sems, recv_sems, srcs, lands, _ = handle
    n = len(srcs)

    def body(*refs):
        ins, lz = refs[:n], refs[n:2 * n]
        for cp in _peer_copies(ins, lz, scatter, refs[2 * n], refs[2 * n + 1], False):
            cp.wait_send()
        for cp in _peer_copies(ins, lz, scatter, refs[2 * n], refs[2 * n + 1], True):
            cp.wait_recv()

    outs = pl.pallas_call(
        body, name=name,
        out_shape=[pltpu.HBM(a.shape, a.dtype) for a in srcs + lands],
        in_specs=[_HBM] * (2 * n) + [_SEM, _SEM, pl.BlockSpec(memory_space=pl.ANY)],
        out_specs=[_HBM] * (2 * n),
        input_output_aliases={i: i for i in range(2 * n)},
        compiler_params=pltpu.CompilerParams(has_side_effects=_DATAFLOW),
    )(*srcs, *lands, send_sems, recv_sems, after)
    return list(outs[:n]), list(outs[n:])


def _fill_own(lands, srcs, scatter):
    me = 4 * lax.axis_index("x") + 2 * lax.axis_index("y") + lax.axis_index("c")
    own = [lax.dynamic_index_in_dim(s, me, 0, keepdims=False) if sc else s for s, sc in zip(srcs, scatter)]
    return [lax.dynamic_update_index_in_dim(land, o, me, 0) for land, o in zip(lands, own)]


def _slot_sum(parts, name, tr):
    _, R, C = parts.shape

    def body(p_ref, o_ref):
        acc = p_ref[0].astype(F32)
        for s in range(1, N_DEV):
            acc = acc + p_ref[s].astype(F32)
        o_ref[...] = acc

    return pl.pallas_call(
        body, name=name, grid=(R // tr,),
        in_specs=[pl.BlockSpec((N_DEV, tr, C), lambda i: (0, i, 0))],
        out_specs=pl.BlockSpec((tr, C), lambda i: (i, 0)),
        out_shape=jax.ShapeDtypeStruct((R, C), F32),
        compiler_params=_params(1),
    )(parts)


def _adamw(w, g, m, v, name):
    def body(w_ref, g_ref, m_ref, v_ref, d_ref, nm_ref, nv_ref):
        g = g_ref[...]
        nm = ADAM_B1 * m_ref[...] + (1.0 - ADAM_B1) * g
        nv = ADAM_B2 * v_ref[...] + (1.0 - ADAM_B2) * jnp.square(g)
        m_hat = nm / (1.0 - ADAM_B1 ** ADAM_STEP)
        v_hat = nv / (1.0 - ADAM_B2 ** ADAM_STEP)
        d_ref[...] = -ADAM_LR * (m_hat / (jnp.sqrt(v_hat) + ADAM_EPS) + ADAM_WD * w_ref[...])
        nm_ref[...] = nm
        nv_ref[...] = nv

    return pl.pallas_call(
        body, name=name, out_shape=[jax.ShapeDtypeStruct(w.shape, F32)] * 3,
        compiler_params=pltpu.CompilerParams(vmem_limit_bytes=VMEM_LIMIT),
    )(w, g, m, v)


def _rope_tables(T):
    half = HEAD_DIM // 2
    freqs = ROPE_THETA ** (-jnp.arange(half, dtype=F32) * (2.0 / HEAD_DIM))
    ang = jnp.arange(T).astype(F32)[:, None] * jnp.tile(freqs, LANES // half)[None, :]
    sign = jnp.tile(jnp.concatenate([-jnp.ones((half,), F32), jnp.ones((half,), F32)]), LANES // HEAD_DIM)
    return jnp.cos(ang), jnp.sin(ang) * sign[None, :]


def _block_diag(w_pool):
    wbd = jnp.zeros((POOL_W, POOL_W), F32)
    g = POOL_W // len(POOL_WINDOWS)
    for i in range(len(POOL_WINDOWS)):
        wbd = wbd.at[i * g:(i + 1) * g, i * g:(i + 1) * g].set(w_pool[i])
    return wbd


def _pack_small(g1, w_pool, pool_scale, g2, g3, g4, extra):
    pad = lambda a: jnp.pad(a.reshape(1, -1), ((0, 0), (0, D_MODEL - a.size)))
    rows = [g1.reshape(1, -1), g2.reshape(1, -1), g3.reshape(1, -1), g4.reshape(1, -1),
            w_pool.reshape(-1, D_MODEL), pad(pool_scale), pad(extra)]
    buf = jnp.concatenate(rows, axis=0)
    return jnp.pad(buf, ((0, SMALL_ROWS - buf.shape[0]), (0, 0)))


def _unpack_small(buf):
    n_pool = len(POOL_WINDOWS) * (POOL_W // len(POOL_WINDOWS)) ** 2 // D_MODEL
    g = POOL_W // len(POOL_WINDOWS)
    return (buf[0:1], buf[4:4 + n_pool].reshape(1, len(POOL_WINDOWS), g, g), buf[4 + n_pool:5 + n_pool, :POOL_W],
            buf[1:2], buf[2:3], buf[3:4], buf[5 + n_pool])


class _LocalStep:
    def __init__(self, x, tgt, g1, w_pool, pool_scale, g2, g3, g4):
        self.x, self.tgt, self.pool_scale = x, tgt, pool_scale
        self.g1, self.g2, self.g3, self.g4 = g1, g2, g3, g4
        self.cos, self.sin = _rope_tables(x.shape[0])
        self.wbd = _block_diag(w_pool).astype(BF16)

    def mixer_fwd(self, w_in_t, token):
        self.w_in_t = w_in_t
        self.h1, self.u, self.q, self.k, self.v = _proj_fwd(
            self.x, self.g1 + token[0, 0], w_in_t, self.cos, self.sin, 512)
        self.pool = _pool_fwd(self.u, self.wbd, self.pool_scale, 512)
        prev = None
        for j, dil in enumerate(DILATIONS):
            prev = _attn_fwd(self.q, self.k, self.v, dil, prev, j == len(DILATIONS) - 1)
        self.attn, self.lse = prev
        return self.attn

    def ffn_fwd_bwd(self, w_out, wg_t, wu_t, w_down):
        self.w_out, self.wg_t, self.wu_t = w_out, wg_t, wu_t
        self.cat, self.mix, self.x2, h2 = _mix_fwd(self.pool, self.attn, self.x, w_out, self.g2, self.g3, 512)
        act_dgate, act_dup, act = _ffn_up(h2, wg_t, wu_t, 256)
        df, self.dy, self.dg4, self.loss = _ffn_down_loss(act, w_down, self.x2, self.g4, self.tgt, 512)
        self.dgate, self.dup = _ffn_act_bwd(df, w_down, act_dgate, act_dup, 512)
        return (_wgrad(self.dgate, h2, "wgrad_gate", D_FF // 2, 1024), _wgrad(self.dup, h2, "wgrad_up", D_FF // 2, 1024),
                _wgrad(act, df, "wgrad_down", D_FF // 2, 1024))

    def mixer_bwd(self, token):
        self.dx2, dmix, self.dg3, self.dg2 = _ffn_in_bwd(
            self.dgate, self.dup, self.wg_t, self.wu_t, self.x2, self.mix, self.dy, self.g3 + token[0, 0], self.g2, 512)
        dpool, dattn = _mix_bwd(dmix, self.w_out, 512)
        parts = [_attn_bwd(self.q, self.k, self.v, dattn, self.attn, self.lse, dil) for dil in DILATIONS]
        du, dwbd, self.dscale = _pool_bwd(self.u, dpool, self.wbd, self.pool_scale, 512)
        g = POOL_W // len(POOL_WINDOWS)
        self.dw_pool = jnp.stack([dwbd[i * g:(i + 1) * g, i * g:(i + 1) * g] for i in range(len(POOL_WINDOWS))])
        self.dproj = _dproj_combine(du, *[[p[j] for p in parts] for j in range(5)], self.cos, self.sin, 256)
        return _wgrad(self.dproj, self.h1, "wgrad_in", IN_W // 2, 1024), _wgrad(self.cat, dmix, "wgrad_out", D_MODEL, 1024)

    def input_bwd(self, token):
        grad_x, dg1 = _proj_bwd(self.dproj, self.w_in_t, self.x, self.dx2, self.g1 + token[0, 0], 512)
        return self.loss, grad_x, (dg1, self.dw_pool, self.dscale, self.dg2, self.dg3, self.dg4)


def _local_step(x, tgt, g1, w_pool, pool_scale, g2, g3, g4, w_in_t, w_out, wg_t, wu_t, w_down):
    zero = jnp.zeros((8, LANES), F32)
    step = _LocalStep(x, tgt, g1, w_pool, pool_scale, g2, g3, g4)
    step.mixer_fwd(w_in_t, zero)
    dw_gate, dw_up, dw_down = step.ffn_fwd_bwd(w_out, wg_t, wu_t, w_down)
    dw_in, dw_out = step.mixer_bwd(zero)
    loss, grad_x, small = step.input_bwd(zero)
    return loss, grad_x, small, (dw_in, dw_out, dw_gate, dw_up, dw_down)


def kernel(x, ln_pre_mix, w_in, w_pool, pool_scale, w_out, ln_post_mix, ln_pre_ffn, w_gate, w_up, w_down, ln_post_ffn, loss_target, m_ln_pre_mix, m_w_in, m_w_pool, m_pool_scale, m_w_out, m_ln_post_mix, m_ln_pre_ffn, m_w_gate, m_w_up, m_w_down, m_ln_post_ffn, v_ln_pre_mix, v_w_in, v_w_pool, v_pool_scale, v_w_out, v_ln_post_mix, v_ln_pre_ffn, v_w_gate, v_w_up, v_w_down, v_ln_post_ffn):
    shards = [w_in[0].T.astype(BF16), w_out[0].astype(BF16), w_gate[0].T.astype(BF16),
              w_up[0].T.astype(BF16), w_down[0].astype(BF16)]
    flat = lambda a: a.reshape(-1, D_MODEL)
    blocks = lambda a: a.reshape(N_DEV, -1, D_MODEL)
    step = _LocalStep(x[0], loss_target[0], ln_pre_mix, w_pool[0], pool_scale, ln_post_mix, ln_pre_ffn, ln_post_ffn)

    w_in_t = flat(_gather_two_level(shards[0], "gather_w_in"))
    rest = _exchange_start(shards[1:], [False] * 4, w_in_t, "gather_rest_start")
    attn = step.mixer_fwd(w_in_t, rest[4])
    srcs, lands = _exchange_wait(rest, [False] * 4, attn, "gather_rest_wait")
    w_out_f, wg_t, wu_t, w_down_f = [flat(a) for a in _fill_own(lands, srcs, [False] * 4)]

    ffn = _exchange_start([blocks(a) for a in step.ffn_fwd_bwd(w_out_f, wg_t, wu_t, w_down_f)], [True] * 3,
                          step.dgate, "grads_ffn_start")
    mixer = _exchange_start([blocks(a) for a in step.mixer_bwd(ffn[4])], [True] * 2, step.dproj, "grads_mixer_start")
    loss, grad_x, small = step.input_bwd(mixer[4])
    got = []
    for handle, n_arr, nm in ((mixer, 2, "grads_mixer"), (ffn, 3, "grads_ffn")):
        srcs, lands = _exchange_wait(handle, [True] * n_arr, grad_x, nm + "_wait")
        got += _fill_own(lands, srcs, [True] * n_arr)
    sums = [_slot_sum(got[i], f"sum_grad_{i}", got[i].shape[1] // 2) for i in range(5)]

    small_buf = _pack_small(small[0], small[1], small[2], small[3], small[4], small[5], loss)
    small_sum = _slot_sum(_exchange([small_buf], [False], "gather_small")[0], "sum_small", SMALL_ROWS)

    g_in, g_out, g_gate, g_up, g_down = sums[0].T, sums[1], sums[2].T, sums[3].T, sums[4]
    upd = [_adamw(w[0], g, m[0], v[0], f"adamw_{nm}") for nm, w, g, m, v in (
        ("in", w_in, g_in, m_w_in, v_w_in), ("out", w_out, g_out, m_w_out, v_w_out),
        ("gate", w_gate, g_gate, m_w_gate, v_w_gate), ("up", w_up, g_up, m_w_up, v_w_up),
        ("down", w_down, g_down, m_w_down, v_w_down))]
    pack = lambda a, b, c, d, e, f: _pack_small(a, b[0], c, d, e, f, jnp.zeros((1,), F32))
    small_upd = _adamw(
        pack(ln_pre_mix, w_pool, pool_scale, ln_post_mix, ln_pre_ffn, ln_post_ffn), small_sum,
        pack(m_ln_pre_mix, m_w_pool, m_pool_scale, m_ln_post_mix, m_ln_pre_ffn, m_ln_post_ffn),
        pack(v_ln_pre_mix, v_w_pool, v_pool_scale, v_ln_post_mix, v_ln_pre_ffn, v_ln_post_ffn), "adamw_small")

    def tree(small6, big5):
        s1, spool, sscale, s2, s3, s4 = small6
        b_in, b_out, b_gate, b_up, b_down = [b[None] for b in big5]
        return [s1, b_in, spool, sscale, b_out, s2, s3, b_gate, b_up, b_down, s4]

    g_small = _unpack_small(small_sum)
    outs = [g_small[6][0], grad_x[None]]
    outs += tree(g_small[:6], [g_in, g_out, g_gate, g_up, g_down])
    for j in range(3):
        outs += tree(_unpack_small(small_upd[j])[:6], [u[j] for u in upd])
    return tuple(outs)
```

```python
import jax
import jax.numpy as jnp
from jax import lax
from jax.experimental import pallas as pl
from jax.experimental.pallas import tpu as pltpu

F32 = jnp.float32
BF16 = jnp.bfloat16

D_MODEL = 1024
POOL_W = 256
ATTN_W = 768
IN_W = 2560
D_FF = 2816
POOL_WINDOWS = (2, 4, 8, 16)
POOL_HALO = 16
DILATIONS = (1, 4, 16)
BLK = 128
LANES = 128
HEAD_DIM = 64
N_GROUPS = ATTN_W // LANES
ROPE_THETA = 10000.0
EPS = 1e-6
NEG = -1e30
N_DEV = 8
SMALL_ROWS = 24

ADAM_LR = 0.001
ADAM_B1 = 0.9
ADAM_B2 = 0.999
ADAM_EPS = 1e-08
ADAM_WD = 0.01
ADAM_STEP = 10

VMEM_LIMIT = 56 * 1024 * 1024


def _dot(a, b):
    return jnp.dot(a, b, preferred_element_type=F32)


def _dot_nt(a, b):
    return lax.dot_general(a, b, (((1,), (1,)), ((), ())), preferred_element_type=F32)


def _dot_tn(a, b):
    return lax.dot_general(a, b, (((0,), (0,)), ((), ())), preferred_element_type=F32)


def _params(n_grid):
    return pltpu.CompilerParams(dimension_semantics=("arbitrary",) * n_grid, vmem_limit_bytes=VMEM_LIMIT)


def _tok(tm, c):
    return pl.BlockSpec((tm, c), lambda i: (i, 0))


def _res(shape):
    return pl.BlockSpec(shape, lambda i: (0,) * len(shape), pipeline_mode=pl.Buffered(1))


def _acc(shape):
    return pl.BlockSpec(shape, lambda i: (0,) * len(shape))


def _rms_fwd(x, g):
    r = lax.rsqrt(jnp.mean(x * x, axis=-1, keepdims=True) + EPS)
    return x * r * g


def _rms_bwd(x, g, dy):
    r = lax.rsqrt(jnp.mean(x * x, axis=-1, keepdims=True) + EPS)
    xh = x * r
    gd = dy * g
    dx = r * (gd - xh * jnp.mean(gd * xh, axis=-1, keepdims=True))
    return dx, jnp.sum(dy * xh, axis=0, keepdims=True)


def _rope(x, c, s, sign):
    lane = lax.broadcasted_iota(jnp.int32, (x.shape[0], LANES), 1)
    first = (lane % HEAD_DIM) < (HEAD_DIM // 2)
    outs = []
    for g in range(x.shape[1] // LANES):
        xg = x[:, g * LANES:(g + 1) * LANES]
        rot = jnp.where(first, pltpu.roll(xg, LANES - HEAD_DIM // 2, 1), pltpu.roll(xg, HEAD_DIM // 2, 1))
        outs.append(xg * c + sign * (rot * s))
    return jnp.concatenate(outs, axis=1)


def _proj_fwd(x, g1, w_in_t, cos, sin, tm):
    T = x.shape[0]

    def body(x_ref, g_ref, w_ref, c_ref, s_ref, h_ref, u_ref, q_ref, k_ref, v_ref):
        h = _rms_fwd(x_ref[...], g_ref[...]).astype(BF16)
        h_ref[...] = h
        proj = _dot_nt(h, w_ref[...])
        c = c_ref[...]
        s = s_ref[...]
        u_ref[...] = proj[:, :POOL_W]
        _store_packed(q_ref, _rope(proj[:, POOL_W:POOL_W + ATTN_W], c, s, 1.0))
        _store_packed(k_ref, _rope(proj[:, POOL_W + ATTN_W:POOL_W + 2 * ATTN_W], c, s, 1.0))
        _store_packed(v_ref, proj[:, POOL_W + 2 * ATTN_W:])

    return pl.pallas_call(
        body, name="proj_fwd", grid=(T // tm,),
        in_specs=[_tok(tm, D_MODEL), _res((1, D_MODEL)), _res((IN_W, D_MODEL)), _tok(tm, LANES), _tok(tm, LANES)],
        out_specs=[_tok(tm, D_MODEL), _tok(tm, POOL_W)] + [_tok_packed(tm, ATTN_W)] * 3,
        out_shape=[jax.ShapeDtypeStruct((T, D_MODEL), BF16), jax.ShapeDtypeStruct((T, POOL_W), F32)]
        + [_packed(T, ATTN_W)] * 3,
        compiler_params=_params(1),
    )(x, g1, w_in_t, cos, sin)


def _pool_window(lane):
    return jnp.where(lane < 64, 2, jnp.where(lane < 128, 4, jnp.where(lane < 192, 8, 16)))


def _pool_select(lane, a2, a4, a8, a16):
    return jnp.where(lane < 64, a2, jnp.where(lane < 128, a4, jnp.where(lane < 192, a8, a16)))


def _pool_delta(cur, prev, i, tm):
    prev = jnp.where(i > 0, prev, 0.0)
    ext = jnp.concatenate([prev, cur], axis=0)
    s2 = ext + pltpu.roll(ext, 1, 0)
    s4 = s2 + pltpu.roll(s2, 2, 0)
    s8 = s4 + pltpu.roll(s4, 4, 0)
    s16 = s8 + pltpu.roll(s8, 8, 0)
    lane = lax.broadcasted_iota(jnp.int32, (tm, POOL_W), 1)
    row = lax.broadcasted_iota(jnp.int32, (tm, POOL_W), 0) + i * tm
    ws = _pool_select(lane, s2[POOL_HALO:], s4[POOL_HALO:], s8[POOL_HALO:], s16[POOL_HALO:])
    cnt = jnp.minimum(row + 1, _pool_window(lane)).astype(F32)
    return ws / cnt - cur


def _pool_fwd(u, wbd, scale, tm):
    T = u.shape[0]
    hb = tm // POOL_HALO

    def body(u_ref, prev_ref, w_ref, sc_ref, o_ref):
        d = _pool_delta(u_ref[...], prev_ref[...], pl.program_id(0), tm)
        o_ref[...] = (_dot(d.astype(BF16), w_ref[...]) * sc_ref[...]).astype(BF16)

    return pl.pallas_call(
        body, name="pool_fwd", grid=(T // tm,),
        in_specs=[_tok(tm, POOL_W), pl.BlockSpec((POOL_HALO, POOL_W), lambda i: (jnp.maximum(i * hb - 1, 0), 0)),
                  _res((POOL_W, POOL_W)), _res((1, POOL_W))],
        out_specs=_tok(tm, POOL_W),
        out_shape=jax.ShapeDtypeStruct((T, POOL_W), BF16),
        compiler_params=_params(1),
    )(u, u, wbd, scale)


def _attn_mask(n):
    qi = lax.broadcasted_iota(jnp.int32, (BLK, 2 * BLK), 0)
    kj = lax.broadcasted_iota(jnp.int32, (BLK, 2 * BLK), 1)
    dist = qi + BLK - kj
    return (dist >= 0) & (dist <= BLK) & ((kj >= BLK) | (n > 0))


def _stack_heads(x, lo):
    zero = jnp.zeros_like(x)
    return jnp.concatenate([jnp.where(lo, x, zero), jnp.where(lo, zero, x)], axis=0)


def _head_col(tile, lane, h):
    return jnp.sum(jnp.where(lane == h, tile, 0.0), axis=1, keepdims=True)


def _attn_cols(dil):
    return ATTN_W // 2 if dil >= 16 else ATTN_W


def _stream_pairs_per_step(dil, backward):
    return {1: 1, 4: 2 if backward else 1, 16: 4}[dil]


def _attn_specs(dil, nb):
    cw = _attn_cols(dil)
    ch = BLK * dil
    wide = lambda f: pl.BlockSpec((cw // LANES, ch // 2, LANES), f)
    full = pl.BlockSpec((cw // LANES, ch, LANES), lambda n, j, r: (j, n, 0))
    cur = lambda n, j, r: (j, n, 0)
    prv = lambda n, j, r: (j, jnp.maximum(n - 1, 0), 0)
    prv_out = lambda n, j, r: (j, (n + nb - 1) % nb, 0)
    heads = pl.BlockSpec((ch, LANES), lambda n, j, r: (n, 0))
    return cw, wide(cur), wide(prv), wide(prv_out), heads, full


HIGH_HALF = 0xFFFF0000


def _pack(x):
    return pltpu.bitcast(x.astype(BF16), F32)


def _unpack(words):
    return pltpu.bitcast(words, BF16)


def _packed(rows, cols):
    return jax.ShapeDtypeStruct((cols // LANES, rows // 2, LANES), F32)


def _tok_packed(tm, cols):
    return pl.BlockSpec((cols // LANES, tm // 2, LANES), lambda i: (0, i, 0))


def _store_packed(ref, x):
    for g in range(x.shape[1] // LANES):
        ref[g] = _pack(x[:, g * LANES:(g + 1) * LANES])


def _load_packed(ref):
    return jnp.concatenate([_unpack(ref[g]) for g in range(ref.shape[0])], axis=1)


def _load_streams(ref, dil, r2, sl):
    if dil == 1:
        return [_unpack(ref[sl])]
    words = lax.bitcast_convert_type(ref.at[sl][pl.ds(r2, BLK, stride=dil // 2), :], jnp.uint32)
    even = lax.bitcast_convert_type(words << 16, F32).astype(BF16)
    odd = lax.bitcast_convert_type(words & jnp.uint32(HIGH_HALF), F32).astype(BF16)
    return [even, odd]


def _load_streams_f32(ref, dil, r2, sl):
    ref = ref if sl is None else ref.at[sl]
    if dil == 1:
        return [ref[...]]
    return [ref[pl.ds(2 * r2 + e, BLK, stride=dil), :] for e in range(2)]


def _store_streams_f32(ref, dil, r2, sl, tiles):
    ref = ref if sl is None else ref.at[sl]
    if dil == 1:
        ref[...] = tiles[0]
    else:
        for e, t in enumerate(tiles):
            ref[pl.ds(2 * r2 + e, BLK, stride=dil), :] = t


def _store_streams(ref, dil, r2, sl, tiles):
    if dil == 1:
        ref[sl] = _pack(tiles[0])
    else:
        even, odd = [lax.bitcast_convert_type(t.astype(BF16).astype(F32), jnp.uint32) for t in tiles]
        words = (odd & jnp.uint32(HIGH_HALF)) | (even >> 16)
        ref.at[sl][pl.ds(r2, BLK, stride=dil // 2), :] = lax.bitcast_convert_type(words, F32)


def _attn_fwd(q, k, v, dil, prev, last):
    T = 2 * q.shape[1]
    nb = T // (BLK * dil)
    first = prev is None
    cw, cur, prv, _, heads, full = _attn_specs(dil, nb)
    ncb = ATTN_W // cw
    heads_per_step = cw // HEAD_DIM
    n_str = min(dil, 2)
    reps = _stream_pairs_per_step(dil, False)
    everything = None

    def body(*refs):
        if first:
            q_ref, kc_ref, kp_ref, vc_ref, vp_ref, acc_ref, lse_ref = refs
        else:
            q_ref, kc_ref, kp_ref, vc_ref, vp_ref, acc_in, lse_in, acc_ref, lse_ref = refs
        j = pl.program_id(1)
        valid = _attn_mask(pl.program_id(0))
        lane = lax.broadcasted_iota(jnp.int32, (BLK, LANES), 1)
        lo = lane < HEAD_DIM
        store_acc = _store_streams if last else _store_streams_f32

        def stream_pair(r2):
            lse_tiles = [jnp.zeros((BLK, LANES), F32) for _ in range(n_str)]
            own = []
            for g in range(cw // LANES):
                qs, kcs, kps, vcs, vps = [_load_streams(r, dil, r2, g)
                                          for r in (q_ref, kc_ref, kp_ref, vc_ref, vp_ref)]
                pairs = []
                for e in range(n_str):
                    qg = qs[e] * 0.125
                    kcat = jnp.concatenate([kps[e], kcs[e]], axis=0)
                    vcat = jnp.concatenate([vps[e], vcs[e]], axis=0)
                    pair = None
                    for hh in range(2):
                        h = j * heads_per_step + 2 * g + hh
                        hm = lo if hh == 0 else jnp.logical_not(lo)
                        s = _dot_nt(jnp.where(hm, qg, jnp.zeros_like(qg)), kcat)
                        s = jnp.where(valid, s, NEG)
                        m = jnp.max(s, axis=1, keepdims=True)
                        p = jnp.exp(s - m)
                        den = jnp.sum(p, axis=1, keepdims=True)
                        o = _dot(p.astype(BF16), vcat) / den
                        pair = o if hh == 0 else jnp.where(lo, pair, o)
                        lse_tiles[e] = jnp.where(lane == h, m + jnp.log(den), lse_tiles[e])
                    pairs.append(pair)
                if first:
                    store_acc(acc_ref, dil, r2, g, pairs)
                else:
                    own.append(pairs)
            if not first:
                mine = (lane >= j * heads_per_step) & (lane < (j + 1) * heads_per_step)
                before = _load_streams_f32(lse_in, dil, r2, everything)
                w_before, w_own = [], []
                for e in range(n_str):
                    mx = jnp.maximum(before[e], lse_tiles[e])
                    total = mx + jnp.log(jnp.exp(before[e] - mx) + jnp.exp(lse_tiles[e] - mx))
                    w_before.append(jnp.exp(before[e] - total))
                    w_own.append(jnp.exp(lse_tiles[e] - total))
                    lse_tiles[e] = jnp.where(mine, total, 0.0)
                for g in range(cw // LANES):
                    h0 = j * heads_per_step + 2 * g
                    spread = lambda w: jnp.where(lo, _head_col(w, lane, h0), _head_col(w, lane, h0 + 1))
                    olds = _load_streams_f32(acc_in, dil, r2, g)
                    store_acc(acc_ref, dil, r2, g, [olds[e] * spread(w_before[e]) + own[g][e] * spread(w_own[e])
                                                    for e in range(n_str)])
            if ncb == 1:
                _store_streams_f32(lse_ref, dil, r2, everything, lse_tiles)
            else:
                @pl.when(j == 0)
                def _():
                    _store_streams_f32(lse_ref, dil, r2, everything, lse_tiles)

                @pl.when(j > 0)
                def _():
                    before = _load_streams_f32(lse_ref, dil, r2, everything)
                    _store_streams_f32(lse_ref, dil, r2, everything, [a + b for a, b in zip(before, lse_tiles)])

        for rep in range(reps):
            stream_pair(pl.program_id(2) * reps + rep)

    ins = [q, k, k, v, v]
    in_specs = [cur, cur, prv, cur, prv]
    if not first:
        ins += [prev[0], prev[1]]
        in_specs += [full, heads]
    return pl.pallas_call(
        body, name=f"attn_fwd_d{dil}", grid=(nb, ncb, max(dil // 2, 1) // reps),
        in_specs=in_specs, out_specs=[cur if last else full, heads],
        out_shape=[_packed(T, ATTN_W) if last else jax.ShapeDtypeStruct((N_GROUPS, T, LANES), F32),
                   jax.ShapeDtypeStruct((T, LANES), F32)],
        compiler_params=_params(3),
    )(*ins)


def _mix_fwd(pool, attn, x, w_out, g2, g3, tm):
    T = x.shape[0]

    def body(p_ref, a_ref, x_ref, w_ref, g2_ref, g3_ref, cat_ref, mix_ref, x2_ref, h2_ref):
        p = p_ref[...]
        a = _load_packed(a_ref)
        cat_ref[...] = jnp.concatenate([p, a], axis=1)
        mix = _dot(p, w_ref[:POOL_W, :]) + _dot(a, w_ref[POOL_W:, :])
        mix_ref[...] = mix
        x2 = x_ref[...] + _rms_fwd(mix, g2_ref[...])
        x2_ref[...] = x2
        h2_ref[...] = _rms_fwd(x2, g3_ref[...]).astype(BF16)

    return pl.pallas_call(
        body, name="mix_fwd", grid=(T // tm,),
        in_specs=[_tok(tm, POOL_W), _tok_packed(tm, ATTN_W), _tok(tm, D_MODEL), _res((D_MODEL, D_MODEL)),
                  _res((1, D_MODEL)), _res((1, D_MODEL))],
        out_specs=[_tok(tm, D_MODEL)] * 4,
        out_shape=[jax.ShapeDtypeStruct((T, D_MODEL), BF16), jax.ShapeDtypeStruct((T, D_MODEL), F32),
                   jax.ShapeDtypeStruct((T, D_MODEL), F32), jax.ShapeDtypeStruct((T, D_MODEL), BF16)],
        compiler_params=_params(1),
    )(pool, attn, x, w_out, g2, g3)


def _ffn_up(h2, wg_t, wu_t, tm):
    T = h2.shape[0]

    def body(h_ref, wg_ref, wu_ref, dg_ref, du_ref, a_ref):
        h = h_ref[...]
        gate = _dot_nt(h, wg_ref[...])
        up = _dot_nt(h, wu_ref[...])
        sg = 1.0 / (1.0 + jnp.exp(-gate))
        silu = gate * sg
        a_ref[...] = (silu * up).astype(BF16)
        dg_ref[...] = (up * (sg * (1.0 + gate * (1.0 - sg)))).astype(BF16)
        du_ref[...] = silu.astype(BF16)

    return pl.pallas_call(
        body, name="ffn_up", grid=(T // tm,),
        in_specs=[_tok(tm, D_MODEL), _res((D_FF, D_MODEL)), _res((D_FF, D_MODEL))],
        out_specs=[_tok(tm, D_FF)] * 3,
        out_shape=[jax.ShapeDtypeStruct((T, D_FF), BF16)] * 3,
        compiler_params=_params(1),
    )(h2, wg_t, wu_t)


def _ffn_down_loss(act, w_down, x2, g4, tgt, tm):
    T = act.shape[0]

    def body(a_ref, w_ref, x2_ref, g_ref, t_ref, df_ref, dy_ref, dg_ref, loss_ref):
        i = pl.program_id(0)

        @pl.when(i == 0)
        def _():
            dg_ref[...] = jnp.zeros_like(dg_ref)
            loss_ref[...] = jnp.zeros_like(loss_ref)

        f = _dot(a_ref[...], w_ref[...])
        g = g_ref[...]
        err = x2_ref[...] + _rms_fwd(f, g) - t_ref[...]
        loss_ref[...] += 0.5 * jnp.sum(jnp.mean(err * err, axis=-1, keepdims=True), axis=0, keepdims=True)
        dy = err * (1.0 / D_MODEL)
        dy_ref[...] = dy
        df, dg = _rms_bwd(f, g, dy)
        dg_ref[...] += dg
        df_ref[...] = df.astype(BF16)

    return pl.pallas_call(
        body, name="ffn_down_loss", grid=(T // tm,),
        in_specs=[_tok(tm, D_FF), _res((D_FF, D_MODEL)), _tok(tm, D_MODEL), _res((1, D_MODEL)), _tok(tm, D_MODEL)],
        out_specs=[_tok(tm, D_MODEL), _tok(tm, D_MODEL), _acc((1, D_MODEL)), _acc((1, 1))],
        out_shape=[jax.ShapeDtypeStruct((T, D_MODEL), BF16), jax.ShapeDtypeStruct((T, D_MODEL), F32),
                   jax.ShapeDtypeStruct((1, D_MODEL), F32), jax.ShapeDtypeStruct((1, 1), F32)],
        compiler_params=_params(1),
    )(act, w_down, x2, g4, tgt)


def _ffn_act_bwd(df, w_down, act_dgate, act_dup, tm):
    T = df.shape[0]

    def body(df_ref, w_ref, ag_ref, au_ref, dg_ref, du_ref):
        dact = _dot_nt(df_ref[...], w_ref[...])
        dg_ref[...] = (dact * ag_ref[...].astype(F32)).astype(BF16)
        du_ref[...] = (dact * au_ref[...].astype(F32)).astype(BF16)

    return pl.pallas_call(
        body, name="ffn_act_bwd", grid=(T // tm,),
        in_specs=[_tok(tm, D_MODEL), _res((D_FF, D_MODEL)), _tok(tm, D_FF), _tok(tm, D_FF)],
        out_specs=[_tok(tm, D_FF)] * 2,
        out_shape=[jax.ShapeDtypeStruct((T, D_FF), BF16)] * 2,
        compiler_params=_params(1),
    )(df, w_down, act_dgate, act_dup)


def _ffn_in_bwd(dgate, dup, wg_t, wu_t, x2, mix, dy, g3, g2, tm):
    T = x2.shape[0]

    def body(dg_ref, du_ref, wg_ref, wu_ref, x2_ref, mix_ref, dy_ref, g3_ref, g2_ref,
             dx2_ref, dmix_ref, dg3_ref, dg2_ref):
        @pl.when(pl.program_id(0) == 0)
        def _():
            dg3_ref[...] = jnp.zeros_like(dg3_ref)
            dg2_ref[...] = jnp.zeros_like(dg2_ref)

        dh2 = _dot(dg_ref[...], wg_ref[...]) + _dot(du_ref[...], wu_ref[...])
        dn, dg3 = _rms_bwd(x2_ref[...], g3_ref[...], dh2)
        dx2 = dy_ref[...] + dn
        dx2_ref[...] = dx2
        dg3_ref[...] += dg3
        dmix, dg2 = _rms_bwd(mix_ref[...], g2_ref[...], dx2)
        dg2_ref[...] += dg2
        dmix_ref[...] = dmix.astype(BF16)

    return pl.pallas_call(
        body, name="ffn_in_bwd", grid=(T // tm,),
        in_specs=[_tok(tm, D_FF), _tok(tm, D_FF), _res((D_FF, D_MODEL)), _res((D_FF, D_MODEL)),
                  _tok(tm, D_MODEL), _tok(tm, D_MODEL), _tok(tm, D_MODEL), _res((1, D_MODEL)), _res((1, D_MODEL))],
        out_specs=[_tok(tm, D_MODEL), _tok(tm, D_MODEL), _acc((1, D_MODEL)), _acc((1, D_MODEL))],
        out_shape=[jax.ShapeDtypeStruct((T, D_MODEL), F32), jax.ShapeDtypeStruct((T, D_MODEL), BF16),
                   jax.ShapeDtypeStruct((1, D_MODEL), F32), jax.ShapeDtypeStruct((1, D_MODEL), F32)],
        compiler_params=_params(1),
    )(dgate, dup, wg_t, wu_t, x2, mix, dy, g3, g2)


def _mix_bwd(dmix, w_out, tm):
    T = dmix.shape[0]

    def body(d_ref, w_ref, dp_ref, da_ref):
        dcat = _dot_nt(d_ref[...], w_ref[...])
        dp_ref[...] = dcat[:, :POOL_W].astype(BF16)
        _store_packed(da_ref, dcat[:, POOL_W:])

    return pl.pallas_call(
        body, name="mix_bwd", grid=(T // tm,),
        in_specs=[_tok(tm, D_MODEL), _res((D_MODEL, D_MODEL))],
        out_specs=[_tok(tm, POOL_W), _tok_packed(tm, ATTN_W)],
        out_shape=[jax.ShapeDtypeStruct((T, POOL_W), BF16), _packed(T, ATTN_W)],
        compiler_params=_params(1),
    )(dmix, w_out)


def _attn_bwd(q, k, v, dout, out, lse, dil):
    T = 2 * q.shape[1]
    nb = T // (BLK * dil)
    cw, cur, prv, prv_out, heads, _ = _attn_specs(dil, nb)
    ncb = ATTN_W // cw
    heads_per_step = cw // HEAD_DIM
    n_str = min(dil, 2)
    reps = _stream_pairs_per_step(dil, True)

    def body(q_ref, kc_ref, kp_ref, vc_ref, vp_ref, do_ref, o_ref, lse_ref,
             dq_ref, dkc_ref, dkp_ref, dvc_ref, dvp_ref):
        j = pl.program_id(1)
        valid = _attn_mask(pl.program_id(0))
        lane = lax.broadcasted_iota(jnp.int32, (BLK, LANES), 1)
        lo = lane < HEAD_DIM
        valid2 = jnp.concatenate([valid, valid], axis=0)
        for rep, g in [(rep, g) for rep in range(reps) for g in range(cw // LANES)]:
            r2 = pl.program_id(2) * reps + rep
            sl = g
            lse_tiles = _load_streams_f32(lse_ref, dil, r2, None)
            qs, kcs, kps, vcs, vps, dos, os_ = [
                _load_streams(r, dil, r2, sl) for r in (q_ref, kc_ref, kp_ref, vc_ref, vp_ref, do_ref, o_ref)]
            dqs, dks, dvs = [], [], []
            for e in range(n_str):
                qg = qs[e] * 0.125
                dog = dos[e]
                kcat = jnp.concatenate([kps[e], kcs[e]], axis=0)
                vcat = jnp.concatenate([vps[e], vcs[e]], axis=0)
                prod = dog.astype(F32) * os_[e].astype(F32)
                h0 = j * heads_per_step + 2 * g
                q2 = _stack_heads(qg, lo)
                do2 = _stack_heads(dog, lo)
                lse2 = jnp.concatenate([_head_col(lse_tiles[e], lane, h0), _head_col(lse_tiles[e], lane, h0 + 1)], axis=0)
                dsum2 = jnp.concatenate([jnp.sum(jnp.where(lo, prod, 0.0), axis=1, keepdims=True),
                                         jnp.sum(jnp.where(lo, 0.0, prod), axis=1, keepdims=True)], axis=0)
                p = jnp.exp(jnp.where(valid2, _dot_nt(q2, kcat), NEG) - lse2)
                ds = (p * (_dot_nt(do2, vcat) - dsum2)).astype(BF16)
                dvs.append(_dot_tn(p.astype(BF16), do2))
                dks.append(_dot_tn(ds, q2))
                dq2 = _dot(ds, kcat) * 0.125
                dqs.append(jnp.where(lo, dq2[:BLK], dq2[BLK:]))
            _store_streams(dq_ref, dil, r2, sl, dqs)
            _store_streams(dkp_ref, dil, r2, sl, [t[:BLK] for t in dks])
            _store_streams(dkc_ref, dil, r2, sl, [t[BLK:] for t in dks])
            _store_streams(dvp_ref, dil, r2, sl, [t[:BLK] for t in dvs])
            _store_streams(dvc_ref, dil, r2, sl, [t[BLK:] for t in dvs])

    return pl.pallas_call(
        body, name=f"attn_bwd_d{dil}", grid=(nb, ncb, max(dil // 2, 1) // reps),
        in_specs=[cur, cur, prv, cur, prv, cur, cur, heads],
        out_specs=[cur, cur, prv_out, cur, prv_out],
        out_shape=[_packed(T, ATTN_W)] * 5,
        compiler_params=_params(3),
    )(q, k, k, v, v, dout, out, lse)


def _pool_bwd(u, dy, wbd, scale, tm):
    T = u.shape[0]
    nt = T // tm
    hb = tm // POOL_HALO

    def body(u_ref, prev_ref, dy_ref, next_ref, w_ref, sc_ref, du_ref, dw_ref, dsc_ref):
        i = pl.program_id(0)

        @pl.when(i == 0)
        def _():
            dw_ref[...] = jnp.zeros_like(dw_ref)
            dsc_ref[...] = jnp.zeros_like(dsc_ref)

        w = w_ref[...]
        sc = sc_ref[...]
        d = _pool_delta(u_ref[...], prev_ref[...], i, tm).astype(BF16)
        dyc = dy_ref[...].astype(F32)
        dsc_ref[...] += jnp.sum(dyc * _dot(d, w), axis=0, keepdims=True)
        nxt = jnp.where(i < nt - 1, next_ref[...].astype(F32), 0.0)
        dypre = (jnp.concatenate([dyc, nxt], axis=0) * sc).astype(BF16)
        dw_ref[...] += _dot_tn(d, dypre[:tm])
        dd = _dot_nt(dypre, w)
        n = tm + POOL_HALO
        lane = lax.broadcasted_iota(jnp.int32, (n, POOL_W), 1)
        row = lax.broadcasted_iota(jnp.int32, (n, POOL_W), 0) + i * tm
        gx = dd / jnp.minimum(row + 1, _pool_window(lane)).astype(F32)
        a2 = gx + pltpu.roll(gx, n - 1, 0)
        a4 = a2 + pltpu.roll(a2, n - 2, 0)
        a8 = a4 + pltpu.roll(a4, n - 4, 0)
        a16 = a8 + pltpu.roll(a8, n - 8, 0)
        fs = _pool_select(lane[:tm], a2[:tm], a4[:tm], a8[:tm], a16[:tm])
        du_ref[...] = (fs - dd[:tm]).astype(BF16)

    return pl.pallas_call(
        body, name="pool_bwd", grid=(nt,),
        in_specs=[_tok(tm, POOL_W), pl.BlockSpec((POOL_HALO, POOL_W), lambda i: (jnp.maximum(i * hb - 1, 0), 0)),
                  _tok(tm, POOL_W), pl.BlockSpec((POOL_HALO, POOL_W), lambda i: (jnp.minimum((i + 1) * hb, nt * hb - 1), 0)),
                  _res((POOL_W, POOL_W)), _res((1, POOL_W))],
        out_specs=[_tok(tm, POOL_W), _acc((POOL_W, POOL_W)), _acc((1, POOL_W))],
        out_shape=[jax.ShapeDtypeStruct((T, POOL_W), BF16), jax.ShapeDtypeStruct((POOL_W, POOL_W), F32),
                   jax.ShapeDtypeStruct((1, POOL_W), F32)],
        compiler_params=_params(1),
    )(u, u, dy, dy, wbd, scale)


def _dproj_combine(du, dqs, dkcs, dkps, dvcs, dvps, cos, sin, tm):
    T = du.shape[0]
    n_cfg = len(dqs)

    def body(*refs):
        du_ref = refs[0]
        groups = [refs[1 + j * n_cfg:1 + (j + 1) * n_cfg] for j in range(5)]
        c_ref, s_ref, out_ref = refs[1 + 5 * n_cfg:]
        tot = lambda rs: sum(_load_packed(r).astype(F32) for r in rs)
        c = c_ref[...]
        s = s_ref[...]
        dq = _rope(tot(groups[0]), c, s, -1.0)
        dk = _rope(tot(groups[1]) + tot(groups[2]), c, s, -1.0)
        dv = tot(groups[3]) + tot(groups[4])
        out_ref[...] = jnp.concatenate([du_ref[...], dq.astype(BF16), dk.astype(BF16), dv.astype(BF16)], axis=1)

    return pl.pallas_call(
        body, name="dproj_combine", grid=(T // tm,),
        in_specs=[_tok(tm, POOL_W)] + [_tok_packed(tm, ATTN_W)] * (5 * n_cfg) + [_tok(tm, LANES)] * 2,
        out_specs=_tok(tm, IN_W),
        out_shape=jax.ShapeDtypeStruct((T, IN_W), BF16),
        compiler_params=_params(1),
    )(du, *dqs, *dkcs, *dkps, *dvcs, *dvps, cos, sin)


def _proj_bwd(dproj, w_in_t, x, dx2, g1, tm):
    T = x.shape[0]

    def body(d_ref, w_ref, x_ref, r_ref, g_ref, dx_ref, dg_ref):
        @pl.when(pl.program_id(0) == 0)
        def _():
            dg_ref[...] = jnp.zeros_like(dg_ref)

        dn, dg = _rms_bwd(x_ref[...], g_ref[...], _dot(d_ref[...], w_ref[...]))
        dg_ref[...] += dg
        dx_ref[...] = r_ref[...] + dn

    return pl.pallas_call(
        body, name="proj_bwd", grid=(T // tm,),
        in_specs=[_tok(tm, IN_W), _res((IN_W, D_MODEL)), _tok(tm, D_MODEL), _tok(tm, D_MODEL), _res((1, D_MODEL))],
        out_specs=[_tok(tm, D_MODEL), _acc((1, D_MODEL))],
        out_shape=[jax.ShapeDtypeStruct((T, D_MODEL), F32), jax.ShapeDtypeStruct((1, D_MODEL), F32)],
        compiler_params=_params(1),
    )(dproj, w_in_t, x, dx2, g1)


def _wgrad(a, b, name, tile_m, tk):
    T, M = a.shape
    N = b.shape[1]
    nk = T // tk

    def body(a_ref, b_ref, o_ref, acc_ref):
        kk = pl.program_id(1)

        @pl.when(kk == 0)
        def _():
            acc_ref[...] = jnp.zeros_like(acc_ref)

        acc_ref[...] += _dot_tn(a_ref[...], b_ref[...])

        @pl.when(kk == nk - 1)
        def _():
            o_ref[...] = acc_ref[...].astype(BF16)

    return pl.pallas_call(
        body, name=name, grid=(M // tile_m, nk),
        in_specs=[pl.BlockSpec((tk, tile_m), lambda j, kk: (kk, j)), pl.BlockSpec((tk, N), lambda j, kk: (kk, 0))],
        out_specs=pl.BlockSpec((tile_m, N), lambda j, kk: (j, 0)),
        out_shape=jax.ShapeDtypeStruct((M, N), BF16),
        scratch_shapes=[pltpu.VMEM((tile_m, N), F32)],
        compiler_params=_params(2),
    )(a, b)


def _exchange(arrs, scatter, name):
    n = len(arrs)
    out_shapes = [jax.ShapeDtypeStruct((N_DEV,) + (a.shape[1:] if sc else a.shape), a.dtype)
                  for a, sc in zip(arrs, scatter)]

    def body(*refs):
        ins, outs = refs[:n], refs[n:2 * n]
        send_sems, recv_sems, loc_sems = refs[2 * n:]
        x, y, c = lax.axis_index("x"), lax.axis_index("y"), lax.axis_index("c")
        me = 4 * x + 2 * y + c
        local, sends, recvs = [], [], []
        for i in range(n):
            own = ins[i].at[me] if scatter[i] else ins[i]
            loc = pltpu.make_async_copy(own, outs[i].at[me], loc_sems.at[i])
            loc.start()
            local.append(loc)
            for kbits in range(1, N_DEV):
                px = 1 - x if kbits & 4 else x
                py = 1 - y if kbits & 2 else y
                pc = 1 - c if kbits & 1 else c
                pid = 4 * px + 2 * py + pc
                src = ins[i].at[pid] if scatter[i] else ins[i]
                cp = pltpu.make_async_remote_copy(
                    src_ref=src, dst_ref=outs[i].at[me],
                    send_sem=send_sems.at[i, kbits - 1], recv_sem=recv_sems.at[i, kbits - 1],
                    device_id=(px, py, pc), device_id_type=pl.DeviceIdType.MESH)
                cp.start()
                sends.append(cp)
                recvs.append(pltpu.make_async_remote_copy(
                    src_ref=src, dst_ref=outs[i].at[pid],
                    send_sem=send_sems.at[i, kbits - 1], recv_sem=recv_sems.at[i, kbits - 1],
                    device_id=(px, py, pc), device_id_type=pl.DeviceIdType.MESH))
        for cp in recvs:
            cp.wait_recv()
        for cp in sends:
            cp.wait_send()
        for cp in local:
            cp.wait()

    hbm = pl.BlockSpec(memory_space=pl.ANY)
    return pl.pallas_call(
        body, name=name, in_specs=[hbm] * n, out_specs=[hbm] * n, out_shape=out_shapes,
        scratch_shapes=[pltpu.SemaphoreType.DMA((n, N_DEV - 1)), pltpu.SemaphoreType.DMA((n, N_DEV - 1)),
                        pltpu.SemaphoreType.DMA((n,))],
    )(*arrs)


def _gather_two_level(arr, name):
    def body(x_ref, out_ref, send_sems, recv_sems, local_sem):
        x, y, c = lax.axis_index("x"), lax.axis_index("y"), lax.axis_index("c")
        me, sibling = (x, y, c), (x, y, 1 - c)
        chips = [(1 - x, y), (x, 1 - y), (1 - x, 1 - y)]
        slot = lambda px, py, pc: out_ref.at[4 * px + 2 * py + pc]

        def copy(k, block, to, src=None):
            return pltpu.make_async_remote_copy(
                src_ref=slot(*block) if src is None else src, dst_ref=slot(*block),
                send_sem=send_sems.at[k], recv_sem=recv_sems.at[k],
                device_id=to, device_id_type=pl.DeviceIdType.MESH)

        mine = pltpu.make_async_copy(x_ref, slot(*me), local_sem)
        mine.start()
        first = [copy(0, me, sibling, src=x_ref)]
        first += [copy(1 + i, me, (*chip, c), src=x_ref) for i, chip in enumerate(chips)]
        for cp in first:
            cp.start()
        passed = [copy(4 + i, (*chip, c), sibling) for i, chip in enumerate(chips)]
        for i, chip in enumerate(chips):
            copy(1 + i, (*chip, c), me).wait_recv()
            passed[i].start()
        copy(0, sibling, me).wait_recv()
        for i, chip in enumerate(chips):
            copy(4 + i, (*chip, 1 - c), me).wait_recv()
        for cp in first + passed:
            cp.wait_send()
        mine.wait()

    hbm = pl.BlockSpec(memory_space=pl.ANY)
    return pl.pallas_call(
        body, name=name, in_specs=[hbm], out_specs=hbm,
        out_shape=jax.ShapeDtypeStruct((N_DEV,) + arr.shape, arr.dtype),
        scratch_shapes=[pltpu.SemaphoreType.DMA((N_DEV - 1,)), pltpu.SemaphoreType.DMA((N_DEV - 1,)),
                        pltpu.SemaphoreType.DMA],
    )(arr)


def _peers(x, y, c):
    for kbits in range(1, N_DEV):
        px = 1 - x if kbits & 4 else x
        py = 1 - y if kbits & 2 else y
        pc = 1 - c if kbits & 1 else c
        yield kbits - 1, (px, py, pc), 4 * px + 2 * py + pc


def _peer_copies(ins, lands, scatter, send_sems, recv_sems, incoming):
    x, y, c = lax.axis_index("x"), lax.axis_index("y"), lax.axis_index("c")
    me = 4 * x + 2 * y + c
    copies = []
    for i in range(len(ins)):
        for k, peer, pid in _peers(x, y, c):
            slot = i * (N_DEV - 1) + k
            copies.append(pltpu.make_async_remote_copy(
                src_ref=ins[i].at[pid] if scatter[i] else ins[i], dst_ref=lands[i].at[pid if incoming else me],
                send_sem=send_sems.at[slot], recv_sem=recv_sems.at[slot],
                device_id=peer, device_id_type=pl.DeviceIdType.MESH))
    return copies


_HBM = pl.BlockSpec(memory_space=pltpu.HBM)
_SEM = pl.BlockSpec(memory_space=pltpu.SEMAPHORE)
_DATAFLOW = pltpu.SideEffectType.DATAFLOW_SIDE_EFFECTING


def _exchange_start(arrs, scatter, after, name):
    n = len(arrs)
    lands = [lax.empty((N_DEV,) + (a.shape[1:] if sc else a.shape), a.dtype) for a, sc in zip(arrs, scatter)]

    def body(*refs):
        ins, lz = refs[:n], refs[n:2 * n]
        send_sems, recv_sems = refs[2 * n + 1:2 * n + 3]
        token = refs[-1]
        for cp in _peer_copies(ins, lz, scatter, send_sems, recv_sems, False):
            cp.start()
        token[...] = jnp.zeros_like(token)

    sem_shape = pltpu.SemaphoreType.DMA((n * (N_DEV - 1),))
    outs = pl.pallas_call(
        body, name=name,
        out_shape=(sem_shape, sem_shape, *[pltpu.HBM(a.shape, a.dtype) for a in arrs + lands],
                   jax.ShapeDtypeStruct((8, LANES), F32)),
        in_specs=[_HBM] * (2 * n) + [pl.BlockSpec(memory_space=pl.ANY)],
        out_specs=(_SEM, _SEM, *[_HBM] * (2 * n), pl.BlockSpec(memory_space=pltpu.VMEM)),
        input_output_aliases={i: 2 + i for i in range(2 * n)},
        compiler_params=pltpu.CompilerParams(has_side_effects=_DATAFLOW),
    )(*[pltpu.with_memory_space_constraint(a, pltpu.HBM) for a in arrs + lands], after)
    return outs[0], outs[1], list(outs[2:2 + n]), list(outs[2 + n:2 + 2 * n]), outs[-1]


def _exchange_wait(handle, scatter, after, name):
    send_sems, recv_sems, srcs, lands, _ = handle
    n = len(srcs)

    def body(*refs):
        ins, lz = refs[:n], refs[n:2 * n]
        for cp in _peer_copies(ins, lz, scatter, refs[2 * n], refs[2 * n + 1], False):
            cp.wait_send()
        for cp in _peer_copies(ins, lz, scatter, refs[2 * n], refs[2 * n + 1], True):
            cp.wait_recv()

    outs = pl.pallas_call(
        body, name=name,
        out_shape=[pltpu.HBM(a.shape, a.dtype) for a in srcs + lands],
        in_specs=[_HBM] * (2 * n) + [_SEM, _SEM, pl.BlockSpec(memory_space=pl.ANY)],
        out_specs=[_HBM] * (2 * n),
        input_output_aliases={i: i for i in range(2 * n)},
        compiler_params=pltpu.CompilerParams(has_side_effects=_DATAFLOW),
    )(*srcs, *lands, send_sems, recv_sems, after)
    return list(outs[:n]), list(outs[n:])


def _fill_own(lands, srcs, scatter):
    me = 4 * lax.axis_index("x") + 2 * lax.axis_index("y") + lax.axis_index("c")
    own = [lax.dynamic_index_in_dim(s, me, 0, keepdims=False) if sc else s for s, sc in zip(srcs, scatter)]
    return [lax.dynamic_update_index_in_dim(land, o, me, 0) for land, o in zip(lands, own)]


def _slot_sum(parts, name, tr):
    _, R, C = parts.shape

    def body(p_ref, o_ref):
        acc = p_ref[0].astype(F32)
        for s in range(1, N_DEV):
            acc = acc + p_ref[s].astype(F32)
        o_ref[...] = acc

    return pl.pallas_call(
        body, name=name, grid=(R // tr,),
        in_specs=[pl.BlockSpec((N_DEV, tr, C), lambda i: (0, i, 0))],
        out_specs=pl.BlockSpec((tr, C), lambda i: (i, 0)),
        out_shape=jax.ShapeDtypeStruct((R, C), F32),
        compiler_params=_params(1),
    )(parts)


def _adamw(w, g, m, v, name):
    def body(w_ref, g_ref, m_ref, v_ref, d_ref, nm_ref, nv_ref):
        g = g_ref[...]
        nm = ADAM_B1 * m_ref[...] + (1.0 - ADAM_B1) * g
        nv = ADAM_B2 * v_ref[...] + (1.0 - ADAM_B2) * jnp.square(g)
        m_hat = nm / (1.0 - ADAM_B1 ** ADAM_STEP)
        v_hat = nv / (1.0 - ADAM_B2 ** ADAM_STEP)
        d_ref[...] = -ADAM_LR * (m_hat / (jnp.sqrt(v_hat) + ADAM_EPS) + ADAM_WD * w_ref[...])
        nm_ref[...] = nm
        nv_ref[...] = nv

    return pl.pallas_call(
        body, name=name, out_shape=[jax.ShapeDtypeStruct(w.shape, F32)] * 3,
        compiler_params=pltpu.CompilerParams(vmem_limit_bytes=VMEM_LIMIT),
    )(w, g, m, v)


def _rope_tables(T):
    half = HEAD_DIM // 2
    freqs = ROPE_THETA ** (-jnp.arange(half, dtype=F32) * (2.0 / HEAD_DIM))
    ang = jnp.arange(T).astype(F32)[:, None] * jnp.tile(freqs, LANES // half)[None, :]
    sign = jnp.tile(jnp.concatenate([-jnp.ones((half,), F32), jnp.ones((half,), F32)]), LANES // HEAD_DIM)
    return jnp.cos(ang), jnp.sin(ang) * sign[None, :]


def _block_diag(w_pool):
    wbd = jnp.zeros((POOL_W, POOL_W), F32)
    g = POOL_W // len(POOL_WINDOWS)
    for i in range(len(POOL_WINDOWS)):
        wbd = wbd.at[i * g:(i + 1) * g, i * g:(i + 1) * g].set(w_pool[i])
    return wbd


def _pack_small(g1, w_pool, pool_scale, g2, g3, g4, extra):
    pad = lambda a: jnp.pad(a.reshape(1, -1), ((0, 0), (0, D_MODEL - a.size)))
    rows = [g1.reshape(1, -1), g2.reshape(1, -1), g3.reshape(1, -1), g4.reshape(1, -1),
            w_pool.reshape(-1, D_MODEL), pad(pool_scale), pad(extra)]
    buf = jnp.concatenate(rows, axis=0)
    return jnp.pad(buf, ((0, SMALL_ROWS - buf.shape[0]), (0, 0)))


def _unpack_small(buf):
    n_pool = len(POOL_WINDOWS) * (POOL_W // len(POOL_WINDOWS)) ** 2 // D_MODEL
    g = POOL_W // len(POOL_WINDOWS)
    return (buf[0:1], buf[4:4 + n_pool].reshape(1, len(POOL_WINDOWS), g, g), buf[4 + n_pool:5 + n_pool, :POOL_W],
            buf[1:2], buf[2:3], buf[3:4], buf[5 + n_pool])


class _LocalStep:
    def __init__(self, x, tgt, g1, w_pool, pool_scale, g2, g3, g4):
        self.x, self.tgt, self.pool_scale = x, tgt, pool_scale
        self.g1, self.g2, self.g3, self.g4 = g1, g2, g3, g4
        self.cos, self.sin = _rope_tables(x.shape[0])
        self.wbd = _block_diag(w_pool).astype(BF16)

    def mixer_fwd(self, w_in_t, token):
        self.w_in_t = w_in_t
        self.h1, self.u, self.q, self.k, self.v = _proj_fwd(
            self.x, self.g1 + token[0, 0], w_in_t, self.cos, self.sin, 1024)
        self.pool = _pool_fwd(self.u, self.wbd, self.pool_scale, 512)
        prev = None
        for j, dil in enumerate(DILATIONS):
            prev = _attn_fwd(self.q, self.k, self.v, dil, prev, j == len(DILATIONS) - 1)
        self.attn, self.lse = prev
        return self.attn

    def ffn_fwd_bwd(self, w_out, wg_t, wu_t, w_down):
        self.w_out, self.wg_t, self.wu_t = w_out, wg_t, wu_t
        self.cat, self.mix, self.x2, h2 = _mix_fwd(self.pool, self.attn, self.x, w_out, self.g2, self.g3, 1024)
        act_dgate, act_dup, act = _ffn_up(h2, wg_t, wu_t, 256)
        df, self.dy, self.dg4, self.loss = _ffn_down_loss(act, w_down, self.x2, self.g4, self.tgt, 512)
        self.dgate, self.dup = _ffn_act_bwd(df, w_down, act_dgate, act_dup, 512)
        return (_wgrad(self.dgate, h2, "wgrad_gate", D_FF // 2, 1024), _wgrad(self.dup, h2, "wgrad_up", D_FF // 2, 1024),
                _wgrad(act, df, "wgrad_down", D_FF // 2, 1024))

    def mixer_bwd(self, token):
        self.dx2, dmix, self.dg3, self.dg2 = _ffn_in_bwd(
            self.dgate, self.dup, self.wg_t, self.wu_t, self.x2, self.mix, self.dy, self.g3 + token[0, 0], self.g2, 512)
        dpool, dattn = _mix_bwd(dmix, self.w_out, 1024)
        parts = [_attn_bwd(self.q, self.k, self.v, dattn, self.attn, self.lse, dil) for dil in DILATIONS]
        du, dwbd, self.dscale = _pool_bwd(self.u, dpool, self.wbd, self.pool_scale, 512)
        g = POOL_W // len(POOL_WINDOWS)
        self.dw_pool = jnp.stack([dwbd[i * g:(i + 1) * g, i * g:(i + 1) * g] for i in range(len(POOL_WINDOWS))])
        self.dproj = _dproj_combine(du, *[[p[j] for p in parts] for j in range(5)], self.cos, self.sin, 256)
        return _wgrad(self.dproj, self.h1, "wgrad_in", IN_W // 2, 1024), _wgrad(self.cat, dmix, "wgrad_out", D_MODEL, 1024)

    def input_bwd(self, token):
        grad_x, dg1 = _proj_bwd(self.dproj, self.w_in_t, self.x, self.dx2, self.g1 + token[0, 0], 1024)
        return self.loss, grad_x, (dg1, self.dw_pool, self.dscale, self.dg2, self.dg3, self.dg4)


def _local_step(x, tgt, g1, w_pool, pool_scale, g2, g3, g4, w_in_t, w_out, wg_t, wu_t, w_down):
    zero = jnp.zeros((8, LANES), F32)
    step = _LocalStep(x, tgt, g1, w_pool, pool_scale, g2, g3, g4)
    step.mixer_fwd(w_in_t, zero)
    dw_gate, dw_up, dw_down = step.ffn_fwd_bwd(w_out, wg_t, wu_t, w_down)
    dw_in, dw_out = step.mixer_bwd(zero)
    loss, grad_x, small = step.input_bwd(zero)
    return loss, grad_x, small, (dw_in, dw_out, dw_gate, dw_up, dw_down)


def kernel(x, ln_pre_mix, w_in, w_pool, pool_scale, w_out, ln_post_mix, ln_pre_ffn, w_gate, w_up, w_down, ln_post_ffn, loss_target, m_ln_pre_mix, m_w_in, m_w_pool, m_pool_scale, m_w_out, m_ln_post_mix, m_ln_pre_ffn, m_w_gate, m_w_up, m_w_down, m_ln_post_ffn, v_ln_pre_mix, v_w_in, v_w_pool, v_pool_scale, v_w_out, v_ln_post_mix, v_ln_pre_ffn, v_w_gate, v_w_up, v_w_down, v_ln_post_ffn):
    shards = [w_in[0].T.astype(BF16), w_out[0].astype(BF16), w_gate[0].T.astype(BF16),
              w_up[0].T.astype(BF16), w_down[0].astype(BF16)]
    flat = lambda a: a.reshape(-1, D_MODEL)
    blocks = lambda a: a.reshape(N_DEV, -1, D_MODEL)
    step = _LocalStep(x[0], loss_target[0], ln_pre_mix, w_pool[0], pool_scale, ln_post_mix, ln_pre_ffn, ln_post_ffn)

    w_in_t = flat(_gather_two_level(shards[0], "gather_w_in"))
    rest = _exchange_start(shards[1:], [False] * 4, w_in_t, "gather_rest_start")
    attn = step.mixer_fwd(w_in_t, rest[4])
    srcs, lands = _exchange_wait(rest, [False] * 4, attn, "gather_rest_wait")
    w_out_f, wg_t, wu_t, w_down_f = [flat(a) for a in _fill_own(lands, srcs, [False] * 4)]

    ffn = _exchange_start([blocks(a) for a in step.ffn_fwd_bwd(w_out_f, wg_t, wu_t, w_down_f)], [True] * 3,
                          step.dgate, "grads_ffn_start")
    mixer = _exchange_start([blocks(a) for a in step.mixer_bwd(ffn[4])], [True] * 2, step.dproj, "grads_mixer_start")
    loss, grad_x, small = step.input_bwd(mixer[4])
    got = []
    for handle, n_arr, nm in ((mixer, 2, "grads_mixer"), (ffn, 3, "grads_ffn")):
        srcs, lands = _exchange_wait(handle, [True] * n_arr, grad_x, nm + "_wait")
        got += _fill_own(lands, srcs, [True] * n_arr)
    sums = [_slot_sum(got[i], f"sum_grad_{i}", got[i].shape[1] // 2) for i in range(5)]

    small_buf = _pack_small(small[0], small[1], small[2], small[3], small[4], small[5], loss)
    small_sum = _slot_sum(_exchange([small_buf], [False], "gather_small")[0], "sum_small", SMALL_ROWS)

    g_in, g_out, g_gate, g_up, g_down = sums[0].T, sums[1], sums[2].T, sums[3].T, sums[4]
    upd = [_adamw(w[0], g, m[0], v[0], f"adamw_{nm}") for nm, w, g, m, v in (
        ("in", w_in, g_in, m_w_in, v_w_in), ("out", w_out, g_out, m_w_out, v_w_out),
        ("gate", w_gate, g_gate, m_w_gate, v_w_gate), ("up", w_up, g_up, m_w_up, v_w_up),
        ("down", w_down, g_down, m_w_down, v_w_down))]
    pack = lambda a, b, c, d, e, f: _pack_small(a, b[0], c, d, e, f, jnp.zeros((1,), F32))
    small_upd = _adamw(
        pack(ln_pre_mix, w_pool, pool_scale, ln_post_mix, ln_pre_ffn, ln_post_ffn), small_sum,
        pack(m_ln_pre_mix, m_w_pool, m_pool_scale, m_ln_post_mix, m_ln_pre_ffn, m_ln_post_ffn),
        pack(v_ln_pre_mix, v_w_pool, v_pool_scale, v_ln_post_mix, v_ln_pre_ffn, v_ln_post_ffn), "adamw_small")

    def tree(small6, big5):
        s1, spool, sscale, s2, s3, s4 = small6
        b_in, b_out, b_gate, b_up, b_down = [b[None] for b in big5]
        return [s1, b_in, spool, sscale, b_out, s2, s3, b_gate, b_up, b_down, s4]

    g_small = _unpack_small(small_sum)
    outs = [g_small[6][0], grad_x[None]]
    outs += tree(g_small[:6], [g_in, g_out, g_gate, g_up, g_down])
    for j in range(3):
        outs += tree(_unpack_small(small_upd[j])[:6], [u[j] for u in upd])
    return tuple(outs)
```

```python
import jax
import jax.numpy as jnp
from jax import lax
from jax.experimental import pallas as pl
from jax.experimental.pallas import tpu as pltpu

F32 = jnp.float32
BF16 = jnp.bfloat16

D_MODEL = 1024
POOL_W = 256
ATTN_W = 768
IN_W = 2560
D_FF = 2816
POOL_WINDOWS = (2, 4, 8, 16)
POOL_HALO = 16
DILATIONS = (1, 4, 16)
BLK = 128
LANES = 128
HEAD_DIM = 64
N_GROUPS = ATTN_W // LANES
ROPE_THETA = 10000.0
EPS = 1e-6
NEG = -1e30
N_DEV = 8
SMALL_ROWS = 24

ADAM_LR = 0.001
ADAM_B1 = 0.9
ADAM_B2 = 0.999
ADAM_EPS = 1e-08
ADAM_WD = 0.01
ADAM_STEP = 10

VMEM_LIMIT = 56 * 1024 * 1024


def _dot(a, b):
    return jnp.dot(a, b, preferred_element_type=F32)


def _dot_nt(a, b):
    return lax.dot_general(a, b, (((1,), (1,)), ((), ())), preferred_element_type=F32)


def _dot_tn(a, b):
    return lax.dot_general(a, b, (((0,), (0,)), ((), ())), preferred_element_type=F32)


def _params(n_grid):
    return pltpu.CompilerParams(dimension_semantics=("arbitrary",) * n_grid, vmem_limit_bytes=VMEM_LIMIT)


def _tok(tm, c):
    return pl.BlockSpec((tm, c), lambda i: (i, 0))


def _res(shape):
    return pl.BlockSpec(shape, lambda i: (0,) * len(shape), pipeline_mode=pl.Buffered(1))


def _acc(shape):
    return pl.BlockSpec(shape, lambda i: (0,) * len(shape))


def _rms_fwd(x, g):
    r = lax.rsqrt(jnp.mean(x * x, axis=-1, keepdims=True) + EPS)
    return x * r * g


def _rms_bwd(x, g, dy):
    r = lax.rsqrt(jnp.mean(x * x, axis=-1, keepdims=True) + EPS)
    xh = x * r
    gd = dy * g
    dx = r * (gd - xh * jnp.mean(gd * xh, axis=-1, keepdims=True))
    return dx, jnp.sum(dy * xh, axis=0, keepdims=True)


def _rope(x, c, s, sign):
    lane = lax.broadcasted_iota(jnp.int32, (x.shape[0], LANES), 1)
    first = (lane % HEAD_DIM) < (HEAD_DIM // 2)
    outs = []
    for g in range(x.shape[1] // LANES):
        xg = x[:, g * LANES:(g + 1) * LANES]
        rot = jnp.where(first, pltpu.roll(xg, LANES - HEAD_DIM // 2, 1), pltpu.roll(xg, HEAD_DIM // 2, 1))
        outs.append(xg * c + sign * (rot * s))
    return jnp.concatenate(outs, axis=1)


def _proj_fwd(x, g1, w_in_t, cos, sin, tm):
    T = x.shape[0]

    def body(x_ref, g_ref, w_ref, c_ref, s_ref, h_ref, u_ref, q_ref, k_ref, v_ref):
        h = _rms_fwd(x_ref[...], g_ref[...]).astype(BF16)
        h_ref[...] = h
        proj = _dot_nt(h, w_ref[...])
        c = c_ref[...]
        s = s_ref[...]
        u_ref[...] = proj[:, :POOL_W]
        _store_packed(q_ref, _rope(proj[:, POOL_W:POOL_W + ATTN_W], c, s, 1.0))
        _store_packed(k_ref, _rope(proj[:, POOL_W + ATTN_W:POOL_W + 2 * ATTN_W], c, s, 1.0))
        _store_packed(v_ref, proj[:, POOL_W + 2 * ATTN_W:])

    return pl.pallas_call(
        body, name="proj_fwd", grid=(T // tm,),
        in_specs=[_tok(tm, D_MODEL), _res((1, D_MODEL)), _res((IN_W, D_MODEL)), _tok(tm, LANES), _tok(tm, LANES)],
        out_specs=[_tok(tm, D_MODEL), _tok(tm, POOL_W)] + [_tok_packed(tm, ATTN_W)] * 3,
        out_shape=[jax.ShapeDtypeStruct((T, D_MODEL), BF16), jax.ShapeDtypeStruct((T, POOL_W), F32)]
        + [_packed(T, ATTN_W)] * 3,
        compiler_params=_params(1),
    )(x, g1, w_in_t, cos, sin)


def _pool_window(lane):
    return jnp.where(lane < 64, 2, jnp.where(lane < 128, 4, jnp.where(lane < 192, 8, 16)))


def _pool_select(lane, a2, a4, a8, a16):
    return jnp.where(lane < 64, a2, jnp.where(lane < 128, a4, jnp.where(lane < 192, a8, a16)))


def _pool_delta(cur, prev, i, tm):
    prev = jnp.where(i > 0, prev, 0.0)
    ext = jnp.concatenate([prev, cur], axis=0)
    s2 = ext + pltpu.roll(ext, 1, 0)
    s4 = s2 + pltpu.roll(s2, 2, 0)
    s8 = s4 + pltpu.roll(s4, 4, 0)
    s16 = s8 + pltpu.roll(s8, 8, 0)
    lane = lax.broadcasted_iota(jnp.int32, (tm, POOL_W), 1)
    row = lax.broadcasted_iota(jnp.int32, (tm, POOL_W), 0) + i * tm
    ws = _pool_select(lane, s2[POOL_HALO:], s4[POOL_HALO:], s8[POOL_HALO:], s16[POOL_HALO:])
    cnt = jnp.minimum(row + 1, _pool_window(lane)).astype(F32)
    return ws / cnt - cur


def _pool_fwd(u, wbd, scale, tm):
    T = u.shape[0]
    hb = tm // POOL_HALO

    def body(u_ref, prev_ref, w_ref, sc_ref, o_ref):
        d = _pool_delta(u_ref[...], prev_ref[...], pl.program_id(0), tm)
        o_ref[...] = (_dot(d.astype(BF16), w_ref[...]) * sc_ref[...]).astype(BF16)

    return pl.pallas_call(
        body, name="pool_fwd", grid=(T // tm,),
        in_specs=[_tok(tm, POOL_W), pl.BlockSpec((POOL_HALO, POOL_W), lambda i: (jnp.maximum(i * hb - 1, 0), 0)),
                  _res((POOL_W, POOL_W)), _res((1, POOL_W))],
        out_specs=_tok(tm, POOL_W),
        out_shape=jax.ShapeDtypeStruct((T, POOL_W), BF16),
        compiler_params=_params(1),
    )(u, u, wbd, scale)


def _attn_mask(has_prev):
    qi = lax.broadcasted_iota(jnp.int32, (BLK, 2 * BLK), 0)
    kj = lax.broadcasted_iota(jnp.int32, (BLK, 2 * BLK), 1)
    dist = qi + BLK - kj
    return (dist >= 0) & (dist <= BLK) & ((kj >= BLK) | has_prev)


def _stack_heads(x, lo):
    zero = jnp.zeros_like(x)
    return jnp.concatenate([jnp.where(lo, x, zero), jnp.where(lo, zero, x)], axis=0)


def _head_col(tile, lane, h):
    return jnp.sum(jnp.where(lane == h, tile, 0.0), axis=1, keepdims=True)


def _attn_cols(dil):
    return ATTN_W // 2 if dil >= 16 else ATTN_W


def _units_per_step(dil, backward):
    return {1: 2, 4: 2 if backward else 1, 16: 4}[dil]


def _chunk_tokens(dil, units):
    return BLK * (units if dil == 1 else dil)


def _unit_steps(dil, units):
    return 1 if dil == 1 else dil // 2 // units


def _attn_specs(dil, nb, units):
    cw = _attn_cols(dil)
    ch = _chunk_tokens(dil, units)
    wide = lambda f: pl.BlockSpec((cw // LANES, ch // 2, LANES), f)
    full = pl.BlockSpec((cw // LANES, ch, LANES), lambda n, j, r: (j, n, 0))
    cur = lambda n, j, r: (j, n, 0)
    prv = lambda n, j, r: (j, jnp.maximum(n - 1, 0), 0)
    prv_out = lambda n, j, r: (j, (n + nb - 1) % nb, 0)
    heads = pl.BlockSpec((ch, LANES), lambda n, j, r: (n, 0))
    return cw, wide(cur), wide(prv), wide(prv_out), heads, full


HIGH_HALF = 0xFFFF0000


def _pack(x):
    return pltpu.bitcast(x.astype(BF16), F32)


def _unpack(words):
    return pltpu.bitcast(words, BF16)


def _packed(rows, cols):
    return jax.ShapeDtypeStruct((cols // LANES, rows // 2, LANES), F32)


def _tok_packed(tm, cols):
    return pl.BlockSpec((cols // LANES, tm // 2, LANES), lambda i: (0, i, 0))


def _store_packed(ref, x):
    for g in range(x.shape[1] // LANES):
        ref[g] = _pack(x[:, g * LANES:(g + 1) * LANES])


def _load_packed(ref):
    return jnp.concatenate([_unpack(ref[g]) for g in range(ref.shape[0])], axis=1)


def _load_streams(ref, dil, r2, sl):
    if dil == 1:
        return [_unpack(ref.at[sl][pl.ds(r2 * (BLK // 2), BLK // 2), :])]
    words = lax.bitcast_convert_type(ref.at[sl][pl.ds(r2, BLK, stride=dil // 2), :], jnp.uint32)
    even = lax.bitcast_convert_type(words << 16, F32).astype(BF16)
    odd = lax.bitcast_convert_type(words & jnp.uint32(HIGH_HALF), F32).astype(BF16)
    return [even, odd]


def _load_prev_streams(prev_ref, cur_ref, dil, units, r2, sl):
    if dil > 1:
        return _load_streams(prev_ref, dil, r2, sl)
    return _load_streams(cur_ref, 1, r2 - 1, sl) if r2 > 0 else _load_streams(prev_ref, 1, units - 1, sl)


def _load_streams_f32(ref, dil, r2, sl):
    ref = ref if sl is None else ref.at[sl]
    if dil == 1:
        return [ref[pl.ds(r2 * BLK, BLK), :]]
    return [ref[pl.ds(2 * r2 + e, BLK, stride=dil), :] for e in range(2)]


def _store_streams_f32(ref, dil, r2, sl, tiles):
    ref = ref if sl is None else ref.at[sl]
    if dil == 1:
        ref[pl.ds(r2 * BLK, BLK), :] = tiles[0]
    else:
        for e, t in enumerate(tiles):
            ref[pl.ds(2 * r2 + e, BLK, stride=dil), :] = t


def _store_streams(ref, dil, r2, sl, tiles):
    if dil == 1:
        ref.at[sl][pl.ds(r2 * (BLK // 2), BLK // 2), :] = _pack(tiles[0])
    else:
        even, odd = [lax.bitcast_convert_type(t.astype(BF16).astype(F32), jnp.uint32) for t in tiles]
        words = (odd & jnp.uint32(HIGH_HALF)) | (even >> 16)
        ref.at[sl][pl.ds(r2, BLK, stride=dil // 2), :] = lax.bitcast_convert_type(words, F32)


def _attn_fwd(q, k, v, dil, prev, last):
    T = 2 * q.shape[1]
    reps = _units_per_step(dil, False)
    nb = T // _chunk_tokens(dil, reps)
    first = prev is None
    cw, cur, prv, _, heads, full = _attn_specs(dil, nb, reps)
    ncb = ATTN_W // cw
    heads_per_step = cw // HEAD_DIM
    n_str = min(dil, 2)
    everything = None

    def body(*refs):
        if first:
            q_ref, kc_ref, kp_ref, vc_ref, vp_ref, acc_ref, lse_ref = refs
        else:
            q_ref, kc_ref, kp_ref, vc_ref, vp_ref, acc_in, lse_in, acc_ref, lse_ref = refs
        j = pl.program_id(1)
        lane = lax.broadcasted_iota(jnp.int32, (BLK, LANES), 1)
        lo = lane < HEAD_DIM
        store_acc = _store_streams if last else _store_streams_f32

        def stream_pair(r2):
            valid = _attn_mask(True if dil == 1 and r2 > 0 else pl.program_id(0) > 0)
            lse_tiles = [jnp.zeros((BLK, LANES), F32) for _ in range(n_str)]
            own = []
            for g in range(cw // LANES):
                qs, kcs, vcs = [_load_streams(r, dil, r2, g) for r in (q_ref, kc_ref, vc_ref)]
                kps, vps = [_load_prev_streams(p, c, dil, reps, r2, g) for p, c in ((kp_ref, kc_ref), (vp_ref, vc_ref))]
                pairs = []
                for e in range(n_str):
                    qg = qs[e] * 0.125
                    kcat = jnp.concatenate([kps[e], kcs[e]], axis=0)
                    vcat = jnp.concatenate([vps[e], vcs[e]], axis=0)
                    pair = None
                    for hh in range(2):
                        h = j * heads_per_step + 2 * g + hh
                        hm = lo if hh == 0 else jnp.logical_not(lo)
                        s = _dot_nt(jnp.where(hm, qg, jnp.zeros_like(qg)), kcat)
                        s = jnp.where(valid, s, NEG)
                        m = jnp.max(s, axis=1, keepdims=True)
                        p = jnp.exp(s - m)
                        den = jnp.sum(p, axis=1, keepdims=True)
                        o = _dot(p.astype(BF16), vcat) / den
                        pair = o if hh == 0 else jnp.where(lo, pair, o)
                        lse_tiles[e] = jnp.where(lane == h, m + jnp.log(den), lse_tiles[e])
                    pairs.append(pair)
                if first:
                    store_acc(acc_ref, dil, r2, g, pairs)
                else:
                    own.append(pairs)
            if not first:
                mine = (lane >= j * heads_per_step) & (lane < (j + 1) * heads_per_step)
                before = _load_streams_f32(lse_in, dil, r2, everything)
                w_before, w_own = [], []
                for e in range(n_str):
                    mx = jnp.maximum(before[e], lse_tiles[e])
                    total = mx + jnp.log(jnp.exp(before[e] - mx) + jnp.exp(lse_tiles[e] - mx))
                    w_before.append(jnp.exp(before[e] - total))
                    w_own.append(jnp.exp(lse_tiles[e] - total))
                    lse_tiles[e] = jnp.where(mine, total, 0.0)
                for g in range(cw // LANES):
                    h0 = j * heads_per_step + 2 * g
                    spread = lambda w: jnp.where(lo, _head_col(w, lane, h0), _head_col(w, lane, h0 + 1))
                    olds = _load_streams_f32(acc_in, dil, r2, g)
                    store_acc(acc_ref, dil, r2, g, [olds[e] * spread(w_before[e]) + own[g][e] * spread(w_own[e])
                                                    for e in range(n_str)])
            if ncb == 1:
                _store_streams_f32(lse_ref, dil, r2, everything, lse_tiles)
            else:
                @pl.when(j == 0)
                def _():
                    _store_streams_f32(lse_ref, dil, r2, everything, lse_tiles)

                @pl.when(j > 0)
                def _():
                    before = _load_streams_f32(lse_ref, dil, r2, everything)
                    _store_streams_f32(lse_ref, dil, r2, everything, [a + b for a, b in zip(before, lse_tiles)])

        for rep in range(reps):
            stream_pair(rep if dil == 1 else pl.program_id(2) * reps + rep)

    ins = [q, k, k, v, v]
    in_specs = [cur, cur, prv, cur, prv]
    if not first:
        ins += [prev[0], prev[1]]
        in_specs += [full, heads]
    return pl.pallas_call(
        body, name=f"attn_fwd_d{dil}", grid=(nb, ncb, _unit_steps(dil, reps)),
        in_specs=in_specs, out_specs=[cur if last else full, heads],
        out_shape=[_packed(T, ATTN_W) if last else jax.ShapeDtypeStruct((N_GROUPS, T, LANES), F32),
                   jax.ShapeDtypeStruct((T, LANES), F32)],
        compiler_params=_params(3),
    )(*ins)


def _mix_fwd(pool, attn, x, w_out, g2, g3, tm):
    T = x.shape[0]

    def body(p_ref, a_ref, x_ref, w_ref, g2_ref, g3_ref, cat_ref, mix_ref, x2_ref, h2_ref):
        p = p_ref[...]
        a = _load_packed(a_ref)
        cat_ref[...] = jnp.concatenate([p, a], axis=1)
        mix = _dot(p, w_ref[:POOL_W, :]) + _dot(a, w_ref[POOL_W:, :])
        mix_ref[...] = mix
        x2 = x_ref[...] + _rms_fwd(mix, g2_ref[...])
        x2_ref[...] = x2
        h2_ref[...] = _rms_fwd(x2, g3_ref[...]).astype(BF16)

    return pl.pallas_call(
        body, name="mix_fwd", grid=(T // tm,),
        in_specs=[_tok(tm, POOL_W), _tok_packed(tm, ATTN_W), _tok(tm, D_MODEL), _res((D_MODEL, D_MODEL)),
                  _res((1, D_MODEL)), _res((1, D_MODEL))],
        out_specs=[_tok(tm, D_MODEL)] * 4,
        out_shape=[jax.ShapeDtypeStruct((T, D_MODEL), BF16), jax.ShapeDtypeStruct((T, D_MODEL), F32),
                   jax.ShapeDtypeStruct((T, D_MODEL), F32), jax.ShapeDtypeStruct((T, D_MODEL), BF16)],
        compiler_params=_params(1),
    )(pool, attn, x, w_out, g2, g3)


def _ffn_up(h2, wg_t, wu_t, tm):
    T = h2.shape[0]

    def body(h_ref, wg_ref, wu_ref, dg_ref, du_ref, a_ref):
        h = h_ref[...]
        gate = _dot_nt(h, wg_ref[...])
        up = _dot_nt(h, wu_ref[...])
        sg = 1.0 / (1.0 + jnp.exp(-gate))
        silu = gate * sg
        a_ref[...] = (silu * up).astype(BF16)
        dg_ref[...] = (up * (sg * (1.0 + gate * (1.0 - sg)))).astype(BF16)
        du_ref[...] = silu.astype(BF16)

    return pl.pallas_call(
        body, name="ffn_up", grid=(T // tm,),
        in_specs=[_tok(tm, D_MODEL), _res((D_FF, D_MODEL)), _res((D_FF, D_MODEL))],
        out_specs=[_tok(tm, D_FF)] * 3,
        out_shape=[jax.ShapeDtypeStruct((T, D_FF), BF16)] * 3,
        compiler_params=_params(1),
    )(h2, wg_t, wu_t)


def _ffn_down_loss(act, w_down, x2, g4, tgt, tm):
    T = act.shape[0]

    def body(a_ref, w_ref, x2_ref, g_ref, t_ref, df_ref, dy_ref, dg_ref, loss_ref):
        i = pl.program_id(0)

        @pl.when(i == 0)
        def _():
            dg_ref[...] = jnp.zeros_like(dg_ref)
            loss_ref[...] = jnp.zeros_like(loss_ref)

        f = _dot(a_ref[...], w_ref[...])
        g = g_ref[...]
        err = x2_ref[...] + _rms_fwd(f, g) - t_ref[...]
        loss_ref[...] += 0.5 * jnp.sum(jnp.mean(err * err, axis=-1, keepdims=True), axis=0, keepdims=True)
        dy = err * (1.0 / D_MODEL)
        dy_ref[...] = dy
        df, dg = _rms_bwd(f, g, dy)
        dg_ref[...] += dg
        df_ref[...] = df.astype(BF16)

    return pl.pallas_call(
        body, name="ffn_down_loss", grid=(T // tm,),
        in_specs=[_tok(tm, D_FF), _res((D_FF, D_MODEL)), _tok(tm, D_MODEL), _res((1, D_MODEL)), _tok(tm, D_MODEL)],
        out_specs=[_tok(tm, D_MODEL), _tok(tm, D_MODEL), _acc((1, D_MODEL)), _acc((1, 1))],
        out_shape=[jax.ShapeDtypeStruct((T, D_MODEL), BF16), jax.ShapeDtypeStruct((T, D_MODEL), F32),
                   jax.ShapeDtypeStruct((1, D_MODEL), F32), jax.ShapeDtypeStruct((1, 1), F32)],
        compiler_params=_params(1),
    )(act, w_down, x2, g4, tgt)


def _ffn_act_bwd(df, w_down, act_dgate, act_dup, tm):
    T = df.shape[0]

    def body(df_ref, w_ref, ag_ref, au_ref, dg_ref, du_ref):
        dact = _dot_nt(df_ref[...], w_ref[...])
        dg_ref[...] = (dact * ag_ref[...].astype(F32)).astype(BF16)
        du_ref[...] = (dact * au_ref[...].astype(F32)).astype(BF16)

    return pl.pallas_call(
        body, name="ffn_act_bwd", grid=(T // tm,),
        in_specs=[_tok(tm, D_MODEL), _res((D_FF, D_MODEL)), _tok(tm, D_FF), _tok(tm, D_FF)],
        out_specs=[_tok(tm, D_FF)] * 2,
        out_shape=[jax.ShapeDtypeStruct((T, D_FF), BF16)] * 2,
        compiler_params=_params(1),
    )(df, w_down, act_dgate, act_dup)


def _ffn_in_bwd(dgate, dup, wg_t, wu_t, x2, mix, dy, g3, g2, tm):
    T = x2.shape[0]

    def body(dg_ref, du_ref, wg_ref, wu_ref, x2_ref, mix_ref, dy_ref, g3_ref, g2_ref,
             dx2_ref, dmix_ref, dg3_ref, dg2_ref):
        @pl.when(pl.program_id(0) == 0)
        def _():
            dg3_ref[...] = jnp.zeros_like(dg3_ref)
            dg2_ref[...] = jnp.zeros_like(dg2_ref)

        dh2 = _dot(dg_ref[...], wg_ref[...]) + _dot(du_ref[...], wu_ref[...])
        dn, dg3 = _rms_bwd(x2_ref[...], g3_ref[...], dh2)
        dx2 = dy_ref[...] + dn
        dx2_ref[...] = dx2
        dg3_ref[...] += dg3
        dmix, dg2 = _rms_bwd(mix_ref[...], g2_ref[...], dx2)
        dg2_ref[...] += dg2
        dmix_ref[...] = dmix.astype(BF16)

    return pl.pallas_call(
        body, name="ffn_in_bwd", grid=(T // tm,),
        in_specs=[_tok(tm, D_FF), _tok(tm, D_FF), _res((D_FF, D_MODEL)), _res((D_FF, D_MODEL)),
                  _tok(tm, D_MODEL), _tok(tm, D_MODEL), _tok(tm, D_MODEL), _res((1, D_MODEL)), _res((1, D_MODEL))],
        out_specs=[_tok(tm, D_MODEL), _tok(tm, D_MODEL), _acc((1, D_MODEL)), _acc((1, D_MODEL))],
        out_shape=[jax.ShapeDtypeStruct((T, D_MODEL), F32), jax.ShapeDtypeStruct((T, D_MODEL), BF16),
                   jax.ShapeDtypeStruct((1, D_MODEL), F32), jax.ShapeDtypeStruct((1, D_MODEL), F32)],
        compiler_params=_params(1),
    )(dgate, dup, wg_t, wu_t, x2, mix, dy, g3, g2)


def _mix_bwd(dmix, w_out, tm):
    T = dmix.shape[0]

    def body(d_ref, w_ref, dp_ref, da_ref):
        dcat = _dot_nt(d_ref[...], w_ref[...])
        dp_ref[...] = dcat[:, :POOL_W].astype(BF16)
        _store_packed(da_ref, dcat[:, POOL_W:])

    return pl.pallas_call(
        body, name="mix_bwd", grid=(T // tm,),
        in_specs=[_tok(tm, D_MODEL), _res((D_MODEL, D_MODEL))],
        out_specs=[_tok(tm, POOL_W), _tok_packed(tm, ATTN_W)],
        out_shape=[jax.ShapeDtypeStruct((T, POOL_W), BF16), _packed(T, ATTN_W)],
        compiler_params=_params(1),
    )(dmix, w_out)


def _attn_bwd(q, k, v, dout, out, lse, dil):
    T = 2 * q.shape[1]
    reps = _units_per_step(dil, True)
    nb = T // _chunk_tokens(dil, reps)
    cw, cur, prv, prv_out, heads, _ = _attn_specs(dil, nb, reps)
    ncb = ATTN_W // cw
    heads_per_step = cw // HEAD_DIM
    n_str = min(dil, 2)

    def body(q_ref, kc_ref, kp_ref, vc_ref, vp_ref, do_ref, o_ref, lse_ref,
             dq_ref, dkc_ref, dkp_ref, dvc_ref, dvp_ref):
        j = pl.program_id(1)
        lane = lax.broadcasted_iota(jnp.int32, (BLK, LANES), 1)
        lo = lane < HEAD_DIM

        def unit_grads(r2, g):
            valid = _attn_mask(True if dil == 1 and r2 > 0 else pl.program_id(0) > 0)
            valid2 = jnp.concatenate([valid, valid], axis=0)
            lse_tiles = _load_streams_f32(lse_ref, dil, r2, None)
            qs, kcs, vcs, dos, os_ = [_load_streams(r, dil, r2, g) for r in (q_ref, kc_ref, vc_ref, do_ref, o_ref)]
            kps, vps = [_load_prev_streams(p, c, dil, reps, r2, g) for p, c in ((kp_ref, kc_ref), (vp_ref, vc_ref))]
            dqs, dks, dvs = [], [], []
            for e in range(n_str):
                qg = qs[e] * 0.125
                dog = dos[e]
                kcat = jnp.concatenate([kps[e], kcs[e]], axis=0)
                vcat = jnp.concatenate([vps[e], vcs[e]], axis=0)
                prod = dog.astype(F32) * os_[e].astype(F32)
                h0 = j * heads_per_step + 2 * g
                q2 = _stack_heads(qg, lo)
                do2 = _stack_heads(dog, lo)
                lse2 = jnp.concatenate([_head_col(lse_tiles[e], lane, h0), _head_col(lse_tiles[e], lane, h0 + 1)], axis=0)
                dsum2 = jnp.concatenate([jnp.sum(jnp.where(lo, prod, 0.0), axis=1, keepdims=True),
                                         jnp.sum(jnp.where(lo, 0.0, prod), axis=1, keepdims=True)], axis=0)
                p = jnp.exp(jnp.where(valid2, _dot_nt(q2, kcat), NEG) - lse2)
                ds = (p * (_dot_nt(do2, vcat) - dsum2)).astype(BF16)
                dvs.append(_dot_tn(p.astype(BF16), do2))
                dks.append(_dot_tn(ds, q2))
                dq2 = _dot(ds, kcat) * 0.125
                dqs.append(jnp.where(lo, dq2[:BLK], dq2[BLK:]))
            return dqs, dks, dvs

        for g in range(cw // LANES):
            if dil > 1:
                for rep in range(reps):
                    r2 = pl.program_id(2) * reps + rep
                    dqs, dks, dvs = unit_grads(r2, g)
                    _store_streams(dq_ref, dil, r2, g, dqs)
                    _store_streams(dkp_ref, dil, r2, g, [t[:BLK] for t in dks])
                    _store_streams(dkc_ref, dil, r2, g, [t[BLK:] for t in dks])
                    _store_streams(dvp_ref, dil, r2, g, [t[:BLK] for t in dvs])
                    _store_streams(dvc_ref, dil, r2, g, [t[BLK:] for t in dvs])
            else:
                blocks = [unit_grads(b, g) for b in range(reps)]
                for b, (dqs, dks, dvs) in enumerate(blocks):
                    _store_streams(dq_ref, 1, b, g, dqs)
                    for cur_ref, prev_ref, which in ((dkc_ref, dkp_ref, 1), (dvc_ref, dvp_ref, 2)):
                        own = blocks[b][which][0][BLK:]
                        if b + 1 < reps:
                            own = own + blocks[b + 1][which][0][:BLK]
                        _store_streams(cur_ref, 1, b, g, [own])
                        edge = blocks[0][which][0][:BLK] if b == reps - 1 else jnp.zeros((BLK, LANES), F32)
                        _store_streams(prev_ref, 1, b, g, [edge])

    return pl.pallas_call(
        body, name=f"attn_bwd_d{dil}", grid=(nb, ncb, _unit_steps(dil, reps)),
        in_specs=[cur, cur, prv, cur, prv, cur, cur, heads],
        out_specs=[cur, cur, prv_out, cur, prv_out],
        out_shape=[_packed(T, ATTN_W)] * 5,
        compiler_params=_params(3),
    )(q, k, k, v, v, dout, out, lse)


def _pool_bwd(u, dy, wbd, scale, tm):
    T = u.shape[0]
    nt = T // tm
    hb = tm // POOL_HALO

    def body(u_ref, prev_ref, dy_ref, next_ref, w_ref, sc_ref, du_ref, dw_ref, dsc_ref):
        i = pl.program_id(0)

        @pl.when(i == 0)
        def _():
            dw_ref[...] = jnp.zeros_like(dw_ref)
            dsc_ref[...] = jnp.zeros_like(dsc_ref)

        w = w_ref[...]
        sc = sc_ref[...]
        d = _pool_delta(u_ref[...], prev_ref[...], i, tm).astype(BF16)
        dyc = dy_ref[...].astype(F32)
        dsc_ref[...] += jnp.sum(dyc * _dot(d, w), axis=0, keepdims=True)
        nxt = jnp.where(i < nt - 1, next_ref[...].astype(F32), 0.0)
        dypre = (jnp.concatenate([dyc, nxt], axis=0) * sc).astype(BF16)
        dw_ref[...] += _dot_tn(d, dypre[:tm])
        dd = _dot_nt(dypre, w)
        n = tm + POOL_HALO
        lane = lax.broadcasted_iota(jnp.int32, (n, POOL_W), 1)
        row = lax.broadcasted_iota(jnp.int32, (n, POOL_W), 0) + i * tm
        gx = dd / jnp.minimum(row + 1, _pool_window(lane)).astype(F32)
        a2 = gx + pltpu.roll(gx, n - 1, 0)
        a4 = a2 + pltpu.roll(a2, n - 2, 0)
        a8 = a4 + pltpu.roll(a4, n - 4, 0)
        a16 = a8 + pltpu.roll(a8, n - 8, 0)
        fs = _pool_select(lane[:tm], a2[:tm], a4[:tm], a8[:tm], a16[:tm])
        du_ref[...] = (fs - dd[:tm]).astype(BF16)

    return pl.pallas_call(
        body, name="pool_bwd", grid=(nt,),
        in_specs=[_tok(tm, POOL_W), pl.BlockSpec((POOL_HALO, POOL_W), lambda i: (jnp.maximum(i * hb - 1, 0), 0)),
                  _tok(tm, POOL_W), pl.BlockSpec((POOL_HALO, POOL_W), lambda i: (jnp.minimum((i + 1) * hb, nt * hb - 1), 0)),
                  _res((POOL_W, POOL_W)), _res((1, POOL_W))],
        out_specs=[_tok(tm, POOL_W), _acc((POOL_W, POOL_W)), _acc((1, POOL_W))],
        out_shape=[jax.ShapeDtypeStruct((T, POOL_W), BF16), jax.ShapeDtypeStruct((POOL_W, POOL_W), F32),
                   jax.ShapeDtypeStruct((1, POOL_W), F32)],
        compiler_params=_params(1),
    )(u, u, dy, dy, wbd, scale)


def _dproj_combine(du, dqs, dkcs, dkps, dvcs, dvps, cos, sin, tm):
    T = du.shape[0]
    n_cfg = len(dqs)

    def body(*refs):
        du_ref = refs[0]
        groups = [refs[1 + j * n_cfg:1 + (j + 1) * n_cfg] for j in range(5)]
        c_ref, s_ref, out_ref = refs[1 + 5 * n_cfg:]
        tot = lambda rs: sum(_load_packed(r).astype(F32) for r in rs)
        c = c_ref[...]
        s = s_ref[...]
        dq = _rope(tot(groups[0]), c, s, -1.0)
        dk = _rope(tot(groups[1]) + tot(groups[2]), c, s, -1.0)
        dv = tot(groups[3]) + tot(groups[4])
        out_ref[...] = jnp.concatenate([du_ref[...], dq.astype(BF16), dk.astype(BF16), dv.astype(BF16)], axis=1)

    return pl.pallas_call(
        body, name="dproj_combine", grid=(T // tm,),
        in_specs=[_tok(tm, POOL_W)] + [_tok_packed(tm, ATTN_W)] * (5 * n_cfg) + [_tok(tm, LANES)] * 2,
        out_specs=_tok(tm, IN_W),
        out_shape=jax.ShapeDtypeStruct((T, IN_W), BF16),
        compiler_params=_params(1),
    )(du, *dqs, *dkcs, *dkps, *dvcs, *dvps, cos, sin)


def _proj_bwd(dproj, w_in_t, x, dx2, g1, tm):
    T = x.shape[0]

    def body(d_ref, w_ref, x_ref, r_ref, g_ref, dx_ref, dg_ref):
        @pl.when(pl.program_id(0) == 0)
        def _():
            dg_ref[...] = jnp.zeros_like(dg_ref)

        dn, dg = _rms_bwd(x_ref[...], g_ref[...], _dot(d_ref[...], w_ref[...]))
        dg_ref[...] += dg
        dx_ref[...] = r_ref[...] + dn

    return pl.pallas_call(
        body, name="proj_bwd", grid=(T // tm,),
        in_specs=[_tok(tm, IN_W), _res((IN_W, D_MODEL)), _tok(tm, D_MODEL), _tok(tm, D_MODEL), _res((1, D_MODEL))],
        out_specs=[_tok(tm, D_MODEL), _acc((1, D_MODEL))],
        out_shape=[jax.ShapeDtypeStruct((T, D_MODEL), F32), jax.ShapeDtypeStruct((1, D_MODEL), F32)],
        compiler_params=_params(1),
    )(dproj, w_in_t, x, dx2, g1)


def _wgrad(a, b, name, tile_m, tk):
    T, M = a.shape
    N = b.shape[1]
    nk = T // tk

    def body(a_ref, b_ref, o_ref, acc_ref):
        kk = pl.program_id(1)

        @pl.when(kk == 0)
        def _():
            acc_ref[...] = jnp.zeros_like(acc_ref)

        acc_ref[...] += _dot_tn(a_ref[...], b_ref[...])

        @pl.when(kk == nk - 1)
        def _():
            o_ref[...] = acc_ref[...].astype(BF16)

    return pl.pallas_call(
        body, name=name, grid=(M // tile_m, nk),
        in_specs=[pl.BlockSpec((tk, tile_m), lambda j, kk: (kk, j)), pl.BlockSpec((tk, N), lambda j, kk: (kk, 0))],
        out_specs=pl.BlockSpec((tile_m, N), lambda j, kk: (j, 0)),
        out_shape=jax.ShapeDtypeStruct((M, N), BF16),
        scratch_shapes=[pltpu.VMEM((tile_m, N), F32)],
        compiler_params=_params(2),
    )(a, b)


def _exchange(arrs, scatter, name):
    n = len(arrs)
    out_shapes = [jax.ShapeDtypeStruct((N_DEV,) + (a.shape[1:] if sc else a.shape), a.dtype)
                  for a, sc in zip(arrs, scatter)]

    def body(*refs):
        ins, outs = refs[:n], refs[n:2 * n]
        send_sems, recv_sems, loc_sems = refs[2 * n:]
        x, y, c = lax.axis_index("x"), lax.axis_index("y"), lax.axis_index("c")
        me = 4 * x + 2 * y + c
        local, sends, recvs = [], [], []
        for i in range(n):
            own = ins[i].at[me] if scatter[i] else ins[i]
            loc = pltpu.make_async_copy(own, outs[i].at[me], loc_sems.at[i])
            loc.start()
            local.append(loc)
            for kbits in range(1, N_DEV):
                px = 1 - x if kbits & 4 else x
                py = 1 - y if kbits & 2 else y
                pc = 1 - c if kbits & 1 else c
                pid = 4 * px + 2 * py + pc
                src = ins[i].at[pid] if scatter[i] else ins[i]
                cp = pltpu.make_async_remote_copy(
                    src_ref=src, dst_ref=outs[i].at[me],
                    send_sem=send_sems.at[i, kbits - 1], recv_sem=recv_sems.at[i, kbits - 1],
                    device_id=(px, py, pc), device_id_type=pl.DeviceIdType.MESH)
                cp.start()
                sends.append(cp)
                recvs.append(pltpu.make_async_remote_copy(
                    src_ref=src, dst_ref=outs[i].at[pid],
                    send_sem=send_sems.at[i, kbits - 1], recv_sem=recv_sems.at[i, kbits - 1],
                    device_id=(px, py, pc), device_id_type=pl.DeviceIdType.MESH))
        for cp in recvs:
            cp.wait_recv()
        for cp in sends:
            cp.wait_send()
        for cp in local:
            cp.wait()

    hbm = pl.BlockSpec(memory_space=pl.ANY)
    return pl.pallas_call(
        body, name=name, in_specs=[hbm] * n, out_specs=[hbm] * n, out_shape=out_shapes,
        scratch_shapes=[pltpu.SemaphoreType.DMA((n, N_DEV - 1)), pltpu.SemaphoreType.DMA((n, N_DEV - 1)),
                        pltpu.SemaphoreType.DMA((n,))],
    )(*arrs)


def _gather_two_level(arr, name):
    def body(x_ref, out_ref, send_sems, recv_sems, local_sem):
        x, y, c = lax.axis_index("x"), lax.axis_index("y"), lax.axis_index("c")
        me, sibling = (x, y, c), (x, y, 1 - c)
        chips = [(1 - x, y), (x, 1 - y), (1 - x, 1 - y)]
        slot = lambda px, py, pc: out_ref.at[4 * px + 2 * py + pc]

        def copy(k, block, to, src=None):
            return pltpu.make_async_remote_copy(
                src_ref=slot(*block) if src is None else src, dst_ref=slot(*block),
                send_sem=send_sems.at[k], recv_sem=recv_sems.at[k],
                device_id=to, device_id_type=pl.DeviceIdType.MESH)

        mine = pltpu.make_async_copy(x_ref, slot(*me), local_sem)
        mine.start()
        first = [copy(0, me, sibling, src=x_ref)]
        first += [copy(1 + i, me, (*chip, c), src=x_ref) for i, chip in enumerate(chips)]
        for cp in first:
            cp.start()
        passed = [copy(4 + i, (*chip, c), sibling) for i, chip in enumerate(chips)]
        for i, chip in enumerate(chips):
            copy(1 + i, (*chip, c), me).wait_recv()
            passed[i].start()
        copy(0, sibling, me).wait_recv()
        for i, chip in enumerate(chips):
            copy(4 + i, (*chip, 1 - c), me).wait_recv()
        for cp in first + passed:
            cp.wait_send()
        mine.wait()

    hbm = pl.BlockSpec(memory_space=pl.ANY)
    return pl.pallas_call(
        body, name=name, in_specs=[hbm], out_specs=hbm,
        out_shape=jax.ShapeDtypeStruct((N_DEV,) + arr.shape, arr.dtype),
        scratch_shapes=[pltpu.SemaphoreType.DMA((N_DEV - 1,)), pltpu.SemaphoreType.DMA((N_DEV - 1,)),
                        pltpu.SemaphoreType.DMA],
    )(arr)


def _peers(x, y, c):
    for kbits in range(1, N_DEV):
        px = 1 - x if kbits & 4 else x
        py = 1 - y if kbits & 2 else y
        pc = 1 - c if kbits & 1 else c
        yield kbits - 1, (px, py, pc), 4 * px + 2 * py + pc


def _peer_copies(ins, lands, scatter, send_sems, recv_sems, incoming):
    x, y, c = lax.axis_index("x"), lax.axis_index("y"), lax.axis_index("c")
    me = 4 * x + 2 * y + c
    copies = []
    for i in range(len(ins)):
        for k, peer, pid in _peers(x, y, c):
            slot = i * (N_DEV - 1) + k
            copies.append(pltpu.make_async_remote_copy(
                src_ref=ins[i].at[pid] if scatter[i] else ins[i], dst_ref=lands[i].at[pid if incoming else me],
                send_sem=send_sems.at[slot], recv_sem=recv_sems.at[slot],
                device_id=peer, device_id_type=pl.DeviceIdType.MESH))
    return copies


_HBM = pl.BlockSpec(memory_space=pltpu.HBM)
_SEM = pl.BlockSpec(memory_space=pltpu.SEMAPHORE)
_DATAFLOW = pltpu.SideEffectType.DATAFLOW_SIDE_EFFECTING


def _exchange_start(arrs, scatter, after, name):
    n = len(arrs)
    lands = [lax.empty((N_DEV,) + (a.shape[1:] if sc else a.shape), a.dtype) for a, sc in zip(arrs, scatter)]

    def body(*refs):
        ins, lz = refs[:n], refs[n:2 * n]
        send_sems, recv_sems = refs[2 * n + 1:2 * n + 3]
        token = refs[-1]
        for cp in _peer_copies(ins, lz, scatter, send_sems, recv_sems, False):
            cp.start()
        token[...] = jnp.zeros_like(token)

    sem_shape = pltpu.SemaphoreType.DMA((n * (N_DEV - 1),))
    outs = pl.pallas_call(
        body, name=name,
        out_shape=(sem_shape, sem_shape, *[pltpu.HBM(a.shape, a.dtype) for a in arrs + lands],
                   jax.ShapeDtypeStruct((8, LANES), F32)),
        in_specs=[_HBM] * (2 * n) + [pl.BlockSpec(memory_space=pl.ANY)],
        out_specs=(_SEM, _SEM, *[_HBM] * (2 * n), pl.BlockSpec(memory_space=pltpu.VMEM)),
        input_output_aliases={i: 2 + i for i in range(2 * n)},
        compiler_params=pltpu.CompilerParams(has_side_effects=_DATAFLOW),
    )(*[pltpu.with_memory_space_constraint(a, pltpu.HBM) for a in arrs + lands], after)
    return outs[0], outs[1], list(outs[2:2 + n]), list(outs[2 + n:2 + 2 * n]), outs[-1]


def _exchange_wait(handle, scatter, after, name):
    send_sems, recv_sems, srcs, lands, _ = handle
    n = len(srcs)

    def body(*refs):
        ins, lz = refs[:n], refs[n:2 * n]
        for cp in _peer_copies(ins, lz, scatter, refs[2 * n], refs[2 * n + 1], False):
            cp.wait_send()
        for cp in _peer_copies(ins, lz, scatter, refs[2 * n], refs[2 * n + 1], True):
            cp.wait_recv()

    outs = pl.pallas_call(
        body, name=name,
        out_shape=[pltpu.HBM(a.shape, a.dtype) for a in srcs + lands],
        in_specs=[_HBM] * (2 * n) + [_SEM, _SEM, pl.BlockSpec(memory_space=pl.ANY)],
        out_specs=[_HBM] * (2 * n),
        input_output_aliases={i: i for i in range(2 * n)},
        compiler_params=pltpu.CompilerParams(has_side_effects=_DATAFLOW),
    )(*srcs, *lands, send_sems, recv_sems, after)
    return list(outs[:n]), list(outs[n:])


def _fill_own(lands, srcs, scatter):
    me = 4 * lax.axis_index("x") + 2 * lax.axis_index("y") + lax.axis_index("c")
    own = [lax.dynamic_index_in_dim(s, me, 0, keepdims=False) if sc else s for s, sc in zip(srcs, scatter)]
    return [lax.dynamic_update_index_in_dim(land, o, me, 0) for land, o in zip(lands, own)]


def _slot_sum(parts, name, tr):
    _, R, C = parts.shape

    def body(p_ref, o_ref):
        acc = p_ref[0].astype(F32)
        for s in range(1, N_DEV):
            acc = acc + p_ref[s].astype(F32)
        o_ref[...] = acc

    return pl.pallas_call(
        body, name=name, grid=(R // tr,),
        in_specs=[pl.BlockSpec((N_DEV, tr, C), lambda i: (0, i, 0))],
        out_specs=pl.BlockSpec((tr, C), lambda i: (i, 0)),
        out_shape=jax.ShapeDtypeStruct((R, C), F32),
        compiler_params=_params(1),
    )(parts)


def _adamw(w, g, m, v, name):
    def body(w_ref, g_ref, m_ref, v_ref, d_ref, nm_ref, nv_ref):
        g = g_ref[...]
        nm = ADAM_B1 * m_ref[...] + (1.0 - ADAM_B1) * g
        nv = ADAM_B2 * v_ref[...] + (1.0 - ADAM_B2) * jnp.square(g)
        m_hat = nm / (1.0 - ADAM_B1 ** ADAM_STEP)
        v_hat = nv / (1.0 - ADAM_B2 ** ADAM_STEP)
        d_ref[...] = -ADAM_LR * (m_hat / (jnp.sqrt(v_hat) + ADAM_EPS) + ADAM_WD * w_ref[...])
        nm_ref[...] = nm
        nv_ref[...] = nv

    return pl.pallas_call(
        body, name=name, out_shape=[jax.ShapeDtypeStruct(w.shape, F32)] * 3,
        compiler_params=pltpu.CompilerParams(vmem_limit_bytes=VMEM_LIMIT),
    )(w, g, m, v)


def _rope_tables(T):
    half = HEAD_DIM // 2
    freqs = ROPE_THETA ** (-jnp.arange(half, dtype=F32) * (2.0 / HEAD_DIM))
    ang = jnp.arange(T).astype(F32)[:, None] * jnp.tile(freqs, LANES // half)[None, :]
    sign = jnp.tile(jnp.concatenate([-jnp.ones((half,), F32), jnp.ones((half,), F32)]), LANES // HEAD_DIM)
    return jnp.cos(ang), jnp.sin(ang) * sign[None, :]


def _block_diag(w_pool):
    wbd = jnp.zeros((POOL_W, POOL_W), F32)
    g = POOL_W // len(POOL_WINDOWS)
    for i in range(len(POOL_WINDOWS)):
        wbd = wbd.at[i * g:(i + 1) * g, i * g:(i + 1) * g].set(w_pool[i])
    return wbd


def _pack_small(g1, w_pool, pool_scale, g2, g3, g4, extra):
    pad = lambda a: jnp.pad(a.reshape(1, -1), ((0, 0), (0, D_MODEL - a.size)))
    rows = [g1.reshape(1, -1), g2.reshape(1, -1), g3.reshape(1, -1), g4.reshape(1, -1),
            w_pool.reshape(-1, D_MODEL), pad(pool_scale), pad(extra)]
    buf = jnp.concatenate(rows, axis=0)
    return jnp.pad(buf, ((0, SMALL_ROWS - buf.shape[0]), (0, 0)))


def _unpack_small(buf):
    n_pool = len(POOL_WINDOWS) * (POOL_W // len(POOL_WINDOWS)) ** 2 // D_MODEL
    g = POOL_W // len(POOL_WINDOWS)
    return (buf[0:1], buf[4:4 + n_pool].reshape(1, len(POOL_WINDOWS), g, g), buf[4 + n_pool:5 + n_pool, :POOL_W],
            buf[1:2], buf[2:3], buf[3:4], buf[5 + n_pool])


class _LocalStep:
    def __init__(self, x, tgt, g1, w_pool, pool_scale, g2, g3, g4):
        self.x, self.tgt, self.pool_scale = x, tgt, pool_scale
        self.g1, self.g2, self.g3, self.g4 = g1, g2, g3, g4
        self.cos, self.sin = _rope_tables(x.shape[0])
        self.wbd = _block_diag(w_pool).astype(BF16)

    def mixer_fwd(self, w_in_t, token):
        self.w_in_t = w_in_t
        self.h1, self.u, self.q, self.k, self.v = _proj_fwd(
            self.x, self.g1 + token[0, 0], w_in_t, self.cos, self.sin, 1024)
        self.pool = _pool_fwd(self.u, self.wbd, self.pool_scale, 512)
        prev = None
        for j, dil in enumerate(DILATIONS):
            prev = _attn_fwd(self.q, self.k, self.v, dil, prev, j == len(DILATIONS) - 1)
        self.attn, self.lse = prev
        return self.attn

    def ffn_fwd_bwd(self, w_out, wg_t, wu_t, w_down):
        self.w_out, self.wg_t, self.wu_t = w_out, wg_t, wu_t
        self.cat, self.mix, self.x2, h2 = _mix_fwd(self.pool, self.attn, self.x, w_out, self.g2, self.g3, 1024)
        act_dgate, act_dup, act = _ffn_up(h2, wg_t, wu_t, 256)
        df, self.dy, self.dg4, self.loss = _ffn_down_loss(act, w_down, self.x2, self.g4, self.tgt, 512)
        self.dgate, self.dup = _ffn_act_bwd(df, w_down, act_dgate, act_dup, 512)
        return (_wgrad(self.dgate, h2, "wgrad_gate", D_FF // 2, 1024), _wgrad(self.dup, h2, "wgrad_up", D_FF // 2, 1024),
                _wgrad(act, df, "wgrad_down", D_FF // 2, 1024))

    def mixer_bwd(self, token):
        self.dx2, dmix, self.dg3, self.dg2 = _ffn_in_bwd(
            self.dgate, self.dup, self.wg_t, self.wu_t, self.x2, self.mix, self.dy, self.g3 + token[0, 0], self.g2, 512)
        dpool, dattn = _mix_bwd(dmix, self.w_out, 1024)
        parts = [_attn_bwd(self.q, self.k, self.v, dattn, self.attn, self.lse, dil) for dil in DILATIONS]
        du, dwbd, self.dscale = _pool_bwd(self.u, dpool, self.wbd, self.pool_scale, 512)
        g = POOL_W // len(POOL_WINDOWS)
        self.dw_pool = jnp.stack([dwbd[i * g:(i + 1) * g, i * g:(i + 1) * g] for i in range(len(POOL_WINDOWS))])
        self.dproj = _dproj_combine(du, *[[p[j] for p in parts] for j in range(5)], self.cos, self.sin, 256)
        return _wgrad(self.dproj, self.h1, "wgrad_in", IN_W // 2, 1024), _wgrad(self.cat, dmix, "wgrad_out", D_MODEL, 1024)

    def input_bwd(self, token):
        grad_x, dg1 = _proj_bwd(self.dproj, self.w_in_t, self.x, self.dx2, self.g1 + token[0, 0], 1024)
        return self.loss, grad_x, (dg1, self.dw_pool, self.dscale, self.dg2, self.dg3, self.dg4)


def _local_step(x, tgt, g1, w_pool, pool_scale, g2, g3, g4, w_in_t, w_out, wg_t, wu_t, w_down):
    zero = jnp.zeros((8, LANES), F32)
    step = _LocalStep(x, tgt, g1, w_pool, pool_scale, g2, g3, g4)
    step.mixer_fwd(w_in_t, zero)
    dw_gate, dw_up, dw_down = step.ffn_fwd_bwd(w_out, wg_t, wu_t, w_down)
    dw_in, dw_out = step.mixer_bwd(zero)
    loss, grad_x, small = step.input_bwd(zero)
    return loss, grad_x, small, (dw_in, dw_out, dw_gate, dw_up, dw_down)


def kernel(x, ln_pre_mix, w_in, w_pool, pool_scale, w_out, ln_post_mix, ln_pre_ffn, w_gate, w_up, w_down, ln_post_ffn, loss_target, m_ln_pre_mix, m_w_in, m_w_pool, m_pool_scale, m_w_out, m_ln_post_mix, m_ln_pre_ffn, m_w_gate, m_w_up, m_w_down, m_ln_post_ffn, v_ln_pre_mix, v_w_in, v_w_pool, v_pool_scale, v_w_out, v_ln_post_mix, v_ln_pre_ffn, v_w_gate, v_w_up, v_w_down, v_ln_post_ffn):
    shards = [w_in[0].T.astype(BF16), w_out[0].astype(BF16), w_gate[0].T.astype(BF16),
              w_up[0].T.astype(BF16), w_down[0].astype(BF16)]
    flat = lambda a: a.reshape(-1, D_MODEL)
    blocks = lambda a: a.reshape(N_DEV, -1, D_MODEL)
    step = _LocalStep(x[0], loss_target[0], ln_pre_mix, w_pool[0], pool_scale, ln_post_mix, ln_pre_ffn, ln_post_ffn)

    w_in_t = flat(_gather_two_level(shards[0], "gather_w_in"))
    rest = _exchange_start(shards[1:], [False] * 4, w_in_t, "gather_rest_start")
    attn = step.mixer_fwd(w_in_t, rest[4])
    srcs, lands = _exchange_wait(rest, [False] * 4, attn, "gather_rest_wait")
    w_out_f, wg_t, wu_t, w_down_f = [flat(a) for a in _fill_own(lands, srcs, [False] * 4)]

    ffn = _exchange_start([blocks(a) for a in step.ffn_fwd_bwd(w_out_f, wg_t, wu_t, w_down_f)], [True] * 3,
                          step.dgate, "grads_ffn_start")
    mixer = _exchange_start([blocks(a) for a in step.mixer_bwd(ffn[4])], [True] * 2, step.dproj, "grads_mixer_start")
    loss, grad_x, small = step.input_bwd(mixer[4])
    got = []
    for handle, n_arr, nm in ((mixer, 2, "grads_mixer"), (ffn, 3, "grads_ffn")):
        srcs, lands = _exchange_wait(handle, [True] * n_arr, grad_x, nm + "_wait")
        got += _fill_own(lands, srcs, [True] * n_arr)
    sums = [_slot_sum(got[i], f"sum_grad_{i}", got[i].shape[1] // 2) for i in range(5)]

    small_buf = _pack_small(small[0], small[1], small[2], small[3], small[4], small[5], loss)
    small_sum = _slot_sum(_exchange([small_buf], [False], "gather_small")[0], "sum_small", SMALL_ROWS)

    g_in, g_out, g_gate, g_up, g_down = sums[0].T, sums[1], sums[2].T, sums[3].T, sums[4]
    upd = [_adamw(w[0], g, m[0], v[0], f"adamw_{nm}") for nm, w, g, m, v in (
        ("in", w_in, g_in, m_w_in, v_w_in), ("out", w_out, g_out, m_w_out, v_w_out),
        ("gate", w_gate, g_gate, m_w_gate, v_w_gate), ("up", w_up, g_up, m_w_up, v_w_up),
        ("down", w_down, g_down, m_w_down, v_w_down))]
    pack = lambda a, b, c, d, e, f: _pack_small(a, b[0], c, d, e, f, jnp.zeros((1,), F32))
    small_upd = _adamw(
        pack(ln_pre_mix, w_pool, pool_scale, ln_post_mix, ln_pre_ffn, ln_post_ffn), small_sum,
        pack(m_ln_pre_mix, m_w_pool, m_pool_scale, m_ln_post_mix, m_ln_pre_ffn, m_ln_post_ffn),
        pack(v_ln_pre_mix, v_w_pool, v_pool_scale, v_ln_post_mix, v_ln_pre_ffn, v_ln_post_ffn), "adamw_small")

    def tree(small6, big5):
        s1, spool, sscale, s2, s3, s4 = small6
        b_in, b_out, b_gate, b_up, b_down = [b[None] for b in big5]
        return [s1, b_in, spool, sscale, b_out, s2, s3, b_gate, b_up, b_down, s4]

    g_small = _unpack_small(small_sum)
    outs = [g_small[6][0], grad_x[None]]
    outs += tree(g_small[:6], [g_in, g_out, g_gate, g_up, g_down])
    for j in range(3):
        outs += tree(_unpack_small(small_upd[j])[:6], [u[j] for u in upd])
    return tuple(outs)
```

```python
import jax
import jax.numpy as jnp
from jax import lax
from jax.experimental import pallas as pl
from jax.experimental.pallas import tpu as pltpu

F32 = jnp.float32
BF16 = jnp.bfloat16

D_MODEL = 1024
POOL_W = 256
ATTN_W = 768
IN_W = 2560
D_FF = 2816
POOL_WINDOWS = (2, 4, 8, 16)
POOL_HALO = 16
DILATIONS = (1, 4, 16)
BLK = 128
LANES = 128
HEAD_DIM = 64
N_GROUPS = ATTN_W // LANES
ROPE_THETA = 10000.0
EPS = 1e-6
NEG = -1e30
N_DEV = 8
SMALL_ROWS = 24

ADAM_LR = 0.001
ADAM_B1 = 0.9
ADAM_B2 = 0.999
ADAM_EPS = 1e-08
ADAM_WD = 0.01
ADAM_STEP = 10

VMEM_LIMIT = 56 * 1024 * 1024


def _dot(a, b):
    return jnp.dot(a, b, preferred_element_type=F32)


def _dot_nt(a, b):
    return lax.dot_general(a, b, (((1,), (1,)), ((), ())), preferred_element_type=F32)


def _dot_tn(a, b):
    return lax.dot_general(a, b, (((0,), (0,)), ((), ())), preferred_element_type=F32)


def _params(n_grid):
    return pltpu.CompilerParams(dimension_semantics=("arbitrary",) * n_grid, vmem_limit_bytes=VMEM_LIMIT)


def _tok(tm, c):
    return pl.BlockSpec((tm, c), lambda i: (i, 0))


def _res(shape):
    return pl.BlockSpec(shape, lambda i: (0,) * len(shape), pipeline_mode=pl.Buffered(1))


def _acc(shape):
    return pl.BlockSpec(shape, lambda i: (0,) * len(shape))


def _rms_fwd(x, g):
    r = lax.rsqrt(jnp.mean(x * x, axis=-1, keepdims=True) + EPS)
    return x * r * g


def _rms_bwd(x, g, dy):
    r = lax.rsqrt(jnp.mean(x * x, axis=-1, keepdims=True) + EPS)
    xh = x * r
    gd = dy * g
    dx = r * (gd - xh * jnp.mean(gd * xh, axis=-1, keepdims=True))
    return dx, jnp.sum(dy * xh, axis=0, keepdims=True)


def _rope(x, c, s, sign):
    lane = lax.broadcasted_iota(jnp.int32, (x.shape[0], LANES), 1)
    first = (lane % HEAD_DIM) < (HEAD_DIM // 2)
    outs = []
    for g in range(x.shape[1] // LANES):
        xg = x[:, g * LANES:(g + 1) * LANES]
        rot = jnp.where(first, pltpu.roll(xg, LANES - HEAD_DIM // 2, 1), pltpu.roll(xg, HEAD_DIM // 2, 1))
        outs.append(xg * c + sign * (rot * s))
    return jnp.concatenate(outs, axis=1)


def _proj_fwd(x, g1, w_in_t, cos, sin, tm):
    T = x.shape[0]

    def body(x_ref, g_ref, w_ref, c_ref, s_ref, h_ref, u_ref, q_ref, k_ref, v_ref):
        h = _rms_fwd(x_ref[...], g_ref[...]).astype(BF16)
        h_ref[...] = h
        proj = _dot_nt(h, w_ref[...])
        c = c_ref[...]
        s = s_ref[...]
        u_ref[...] = proj[:, :POOL_W]
        _store_packed(q_ref, _rope(proj[:, POOL_W:POOL_W + ATTN_W], c, s, 1.0))
        _store_packed(k_ref, _rope(proj[:, POOL_W + ATTN_W:POOL_W + 2 * ATTN_W], c, s, 1.0))
        _store_packed(v_ref, proj[:, POOL_W + 2 * ATTN_W:])

    return pl.pallas_call(
        body, name="proj_fwd", grid=(T // tm,),
        in_specs=[_tok(tm, D_MODEL), _res((1, D_MODEL)), _res((IN_W, D_MODEL)), _tok(tm, LANES), _tok(tm, LANES)],
        out_specs=[_tok(tm, D_MODEL), _tok(tm, POOL_W)] + [_tok_packed(tm, ATTN_W)] * 3,
        out_shape=[jax.ShapeDtypeStruct((T, D_MODEL), BF16), jax.ShapeDtypeStruct((T, POOL_W), F32)]
        + [_packed(T, ATTN_W)] * 3,
        compiler_params=_params(1),
    )(x, g1, w_in_t, cos, sin)


def _pool_window(lane):
    return jnp.where(lane < 64, 2, jnp.where(lane < 128, 4, jnp.where(lane < 192, 8, 16)))


def _pool_select(lane, a2, a4, a8, a16):
    return jnp.where(lane < 64, a2, jnp.where(lane < 128, a4, jnp.where(lane < 192, a8, a16)))


def _pool_delta(cur, prev, i, tm):
    prev = jnp.where(i > 0, prev, 0.0)
    ext = jnp.concatenate([prev, cur], axis=0)
    s2 = ext + pltpu.roll(ext, 1, 0)
    s4 = s2 + pltpu.roll(s2, 2, 0)
    s8 = s4 + pltpu.roll(s4, 4, 0)
    s16 = s8 + pltpu.roll(s8, 8, 0)
    lane = lax.broadcasted_iota(jnp.int32, (tm, POOL_W), 1)
    row = lax.broadcasted_iota(jnp.int32, (tm, POOL_W), 0) + i * tm
    ws = _pool_select(lane, s2[POOL_HALO:], s4[POOL_HALO:], s8[POOL_HALO:], s16[POOL_HALO:])
    cnt = jnp.minimum(row + 1, _pool_window(lane)).astype(F32)
    return ws / cnt - cur


def _pool_fwd(u, wbd, scale, tm):
    T = u.shape[0]
    hb = tm // POOL_HALO

    def body(u_ref, prev_ref, w_ref, sc_ref, o_ref):
        d = _pool_delta(u_ref[...], prev_ref[...], pl.program_id(0), tm)
        o_ref[...] = (_dot(d.astype(BF16), w_ref[...]) * sc_ref[...]).astype(BF16)

    return pl.pallas_call(
        body, name="pool_fwd", grid=(T // tm,),
        in_specs=[_tok(tm, POOL_W), pl.BlockSpec((POOL_HALO, POOL_W), lambda i: (jnp.maximum(i * hb - 1, 0), 0)),
                  _res((POOL_W, POOL_W)), _res((1, POOL_W))],
        out_specs=_tok(tm, POOL_W),
        out_shape=jax.ShapeDtypeStruct((T, POOL_W), BF16),
        compiler_params=_params(1),
    )(u, u, wbd, scale)


def _attn_mask(has_prev):
    qi = lax.broadcasted_iota(jnp.int32, (BLK, 2 * BLK), 0)
    kj = lax.broadcasted_iota(jnp.int32, (BLK, 2 * BLK), 1)
    dist = qi + BLK - kj
    return (dist >= 0) & (dist <= BLK) & ((kj >= BLK) | has_prev)


def _stack_heads(x, lo):
    zero = jnp.zeros_like(x)
    return jnp.concatenate([jnp.where(lo, x, zero), jnp.where(lo, zero, x)], axis=0)


def _head_col(tile, lane, h):
    return jnp.sum(jnp.where(lane == h, tile, 0.0), axis=1, keepdims=True)


def _attn_cols(dil):
    return ATTN_W // 2 if dil >= 16 else ATTN_W


def _units_per_step(dil, backward):
    return {1: 4, 4: 2 if backward else 1, 16: 4}[dil]


def _chunk_tokens(dil, units):
    return BLK * (units if dil == 1 else dil)


def _unit_steps(dil, units):
    return 1 if dil == 1 else dil // 2 // units


def _attn_specs(dil, nb, units):
    cw = _attn_cols(dil)
    ch = _chunk_tokens(dil, units)
    wide = lambda f: pl.BlockSpec((cw // LANES, ch // 2, LANES), f)
    full = pl.BlockSpec((cw // LANES, ch, LANES), lambda n, j, r: (j, n, 0))
    cur = lambda n, j, r: (j, n, 0)
    prv = lambda n, j, r: (j, jnp.maximum(n - 1, 0), 0)
    prv_out = lambda n, j, r: (j, (n + nb - 1) % nb, 0)
    heads = pl.BlockSpec((ch, LANES), lambda n, j, r: (n, 0))
    return cw, wide(cur), wide(prv), wide(prv_out), heads, full


HIGH_HALF = 0xFFFF0000


def _pack(x):
    return pltpu.bitcast(x.astype(BF16), F32)


def _unpack(words):
    return pltpu.bitcast(words, BF16)


def _packed(rows, cols):
    return jax.ShapeDtypeStruct((cols // LANES, rows // 2, LANES), F32)


def _tok_packed(tm, cols):
    return pl.BlockSpec((cols // LANES, tm // 2, LANES), lambda i: (0, i, 0))


def _store_packed(ref, x):
    for g in range(x.shape[1] // LANES):
        ref[g] = _pack(x[:, g * LANES:(g + 1) * LANES])


def _load_packed(ref):
    return jnp.concatenate([_unpack(ref[g]) for g in range(ref.shape[0])], axis=1)


def _load_streams(ref, dil, r2, sl):
    if dil == 1:
        return [_unpack(ref.at[sl][pl.ds(r2 * (BLK // 2), BLK // 2), :])]
    words = lax.bitcast_convert_type(ref.at[sl][pl.ds(r2, BLK, stride=dil // 2), :], jnp.uint32)
    even = lax.bitcast_convert_type(words << 16, F32).astype(BF16)
    odd = lax.bitcast_convert_type(words & jnp.uint32(HIGH_HALF), F32).astype(BF16)
    return [even, odd]


def _load_prev_streams(prev_ref, cur_ref, dil, units, r2, sl):
    if dil > 1:
        return _load_streams(prev_ref, dil, r2, sl)
    return _load_streams(cur_ref, 1, r2 - 1, sl) if r2 > 0 else _load_streams(prev_ref, 1, units - 1, sl)


def _load_streams_f32(ref, dil, r2, sl):
    ref = ref if sl is None else ref.at[sl]
    if dil == 1:
        return [ref[pl.ds(r2 * BLK, BLK), :]]
    return [ref[pl.ds(2 * r2 + e, BLK, stride=dil), :] for e in range(2)]


def _store_streams_f32(ref, dil, r2, sl, tiles):
    ref = ref if sl is None else ref.at[sl]
    if dil == 1:
        ref[pl.ds(r2 * BLK, BLK), :] = tiles[0]
    else:
        for e, t in enumerate(tiles):
            ref[pl.ds(2 * r2 + e, BLK, stride=dil), :] = t


def _store_streams(ref, dil, r2, sl, tiles):
    if dil == 1:
        ref.at[sl][pl.ds(r2 * (BLK // 2), BLK // 2), :] = _pack(tiles[0])
    else:
        even, odd = [lax.bitcast_convert_type(t.astype(BF16).astype(F32), jnp.uint32) for t in tiles]
        words = (odd & jnp.uint32(HIGH_HALF)) | (even >> 16)
        ref.at[sl][pl.ds(r2, BLK, stride=dil // 2), :] = lax.bitcast_convert_type(words, F32)


def _attn_fwd(q, k, v, dil, prev, last):
    T = 2 * q.shape[1]
    reps = _units_per_step(dil, False)
    nb = T // _chunk_tokens(dil, reps)
    first = prev is None
    cw, cur, prv, _, heads, full = _attn_specs(dil, nb, reps)
    ncb = ATTN_W // cw
    heads_per_step = cw // HEAD_DIM
    n_str = min(dil, 2)
    everything = None

    def body(*refs):
        if first:
            q_ref, kc_ref, kp_ref, vc_ref, vp_ref, acc_ref, lse_ref = refs
        else:
            q_ref, kc_ref, kp_ref, vc_ref, vp_ref, acc_in, lse_in, acc_ref, lse_ref = refs
        j = pl.program_id(1)
        lane = lax.broadcasted_iota(jnp.int32, (BLK, LANES), 1)
        lo = lane < HEAD_DIM
        store_acc = _store_streams if last else _store_streams_f32

        def stream_pair(r2):
            valid = _attn_mask(True if dil == 1 and r2 > 0 else pl.program_id(0) > 0)
            lse_tiles = [jnp.zeros((BLK, LANES), F32) for _ in range(n_str)]
            own = []
            for g in range(cw // LANES):
                qs, kcs, vcs = [_load_streams(r, dil, r2, g) for r in (q_ref, kc_ref, vc_ref)]
                kps, vps = [_load_prev_streams(p, c, dil, reps, r2, g) for p, c in ((kp_ref, kc_ref), (vp_ref, vc_ref))]
                pairs = []
                for e in range(n_str):
                    qg = qs[e] * 0.125
                    kcat = jnp.concatenate([kps[e], kcs[e]], axis=0)
                    vcat = jnp.concatenate([vps[e], vcs[e]], axis=0)
                    pair = None
                    for hh in range(2):
                        h = j * heads_per_step + 2 * g + hh
                        hm = lo if hh == 0 else jnp.logical_not(lo)
                        s = _dot_nt(jnp.where(hm, qg, jnp.zeros_like(qg)), kcat)
                        s = jnp.where(valid, s, NEG)
                        m = jnp.max(s, axis=1, keepdims=True)
                        p = jnp.exp(s - m)
                        den = jnp.sum(p, axis=1, keepdims=True)
                        o = _dot(p.astype(BF16), vcat) / den
                        pair = o if hh == 0 else jnp.where(lo, pair, o)
                        lse_tiles[e] = jnp.where(lane == h, m + jnp.log(den), lse_tiles[e])
                    pairs.append(pair)
                if first:
                    store_acc(acc_ref, dil, r2, g, pairs)
                else:
                    own.append(pairs)
            if not first:
                mine = (lane >= j * heads_per_step) & (lane < (j + 1) * heads_per_step)
                before = _load_streams_f32(lse_in, dil, r2, everything)
                w_before, w_own = [], []
                for e in range(n_str):
                    mx = jnp.maximum(before[e], lse_tiles[e])
                    total = mx + jnp.log(jnp.exp(before[e] - mx) + jnp.exp(lse_tiles[e] - mx))
                    w_before.append(jnp.exp(before[e] - total))
                    w_own.append(jnp.exp(lse_tiles[e] - total))
                    lse_tiles[e] = jnp.where(mine, total, 0.0)
                for g in range(cw // LANES):
                    h0 = j * heads_per_step + 2 * g
                    spread = lambda w: jnp.where(lo, _head_col(w, lane, h0), _head_col(w, lane, h0 + 1))
                    olds = _load_streams_f32(acc_in, dil, r2, g)
                    store_acc(acc_ref, dil, r2, g, [olds[e] * spread(w_before[e]) + own[g][e] * spread(w_own[e])
                                                    for e in range(n_str)])
            if ncb == 1:
                _store_streams_f32(lse_ref, dil, r2, everything, lse_tiles)
            else:
                @pl.when(j == 0)
                def _():
                    _store_streams_f32(lse_ref, dil, r2, everything, lse_tiles)

                @pl.when(j > 0)
                def _():
                    before = _load_streams_f32(lse_ref, dil, r2, everything)
                    _store_streams_f32(lse_ref, dil, r2, everything, [a + b for a, b in zip(before, lse_tiles)])

        for rep in range(reps):
            stream_pair(rep if dil == 1 else pl.program_id(2) * reps + rep)

    ins = [q, k, k, v, v]
    in_specs = [cur, cur, prv, cur, prv]
    if not first:
        ins += [prev[0], prev[1]]
        in_specs += [full, heads]
    return pl.pallas_call(
        body, name=f"attn_fwd_d{dil}", grid=(nb, ncb, _unit_steps(dil, reps)),
        in_specs=in_specs, out_specs=[cur if last else full, heads],
        out_shape=[_packed(T, ATTN_W) if last else jax.ShapeDtypeStruct((N_GROUPS, T, LANES), F32),
                   jax.ShapeDtypeStruct((T, LANES), F32)],
        compiler_params=_params(3),
    )(*ins)


def _mix_fwd(pool, attn, x, w_out, g2, g3, tm):
    T = x.shape[0]

    def body(p_ref, a_ref, x_ref, w_ref, g2_ref, g3_ref, cat_ref, mix_ref, x2_ref, h2_ref):
        p = p_ref[...]
        a = _load_packed(a_ref)
        cat_ref[...] = jnp.concatenate([p, a], axis=1)
        mix = _dot(p, w_ref[:POOL_W, :]) + _dot(a, w_ref[POOL_W:, :])
        mix_ref[...] = mix
        x2 = x_ref[...] + _rms_fwd(mix, g2_ref[...])
        x2_ref[...] = x2
        h2_ref[...] = _rms_fwd(x2, g3_ref[...]).astype(BF16)

    return pl.pallas_call(
        body, name="mix_fwd", grid=(T // tm,),
        in_specs=[_tok(tm, POOL_W), _tok_packed(tm, ATTN_W), _tok(tm, D_MODEL), _res((D_MODEL, D_MODEL)),
                  _res((1, D_MODEL)), _res((1, D_MODEL))],
        out_specs=[_tok(tm, D_MODEL)] * 4,
        out_shape=[jax.ShapeDtypeStruct((T, D_MODEL), BF16), jax.ShapeDtypeStruct((T, D_MODEL), F32),
                   jax.ShapeDtypeStruct((T, D_MODEL), F32), jax.ShapeDtypeStruct((T, D_MODEL), BF16)],
        compiler_params=_params(1),
    )(pool, attn, x, w_out, g2, g3)


def _ffn_up(h2, wg_t, wu_t, tm):
    T = h2.shape[0]

    def body(h_ref, wg_ref, wu_ref, dg_ref, du_ref, a_ref):
        h = h_ref[...]
        gate = _dot_nt(h, wg_ref[...])
        up = _dot_nt(h, wu_ref[...])
        sg = 1.0 / (1.0 + jnp.exp(-gate))
        silu = gate * sg
        a_ref[...] = (silu * up).astype(BF16)
        dg_ref[...] = (up * (sg * (1.0 + gate * (1.0 - sg)))).astype(BF16)
        du_ref[...] = silu.astype(BF16)

    return pl.pallas_call(
        body, name="ffn_up", grid=(T // tm,),
        in_specs=[_tok(tm, D_MODEL), _res((D_FF, D_MODEL)), _res((D_FF, D_MODEL))],
        out_specs=[_tok(tm, D_FF)] * 3,
        out_shape=[jax.ShapeDtypeStruct((T, D_FF), BF16)] * 3,
        compiler_params=_params(1),
    )(h2, wg_t, wu_t)


def _ffn_down_loss(act, w_down, x2, g4, tgt, tm):
    T = act.shape[0]

    def body(a_ref, w_ref, x2_ref, g_ref, t_ref, df_ref, dy_ref, dg_ref, loss_ref):
        i = pl.program_id(0)

        @pl.when(i == 0)
        def _():
            dg_ref[...] = jnp.zeros_like(dg_ref)
            loss_ref[...] = jnp.zeros_like(loss_ref)

        f = _dot(a_ref[...], w_ref[...])
        g = g_ref[...]
        err = x2_ref[...] + _rms_fwd(f, g) - t_ref[...]
        loss_ref[...] += 0.5 * jnp.sum(jnp.mean(err * err, axis=-1, keepdims=True), axis=0, keepdims=True)
        dy = err * (1.0 / D_MODEL)
        dy_ref[...] = dy
        df, dg = _rms_bwd(f, g, dy)
        dg_ref[...] += dg
        df_ref[...] = df.astype(BF16)

    return pl.pallas_call(
        body, name="ffn_down_loss", grid=(T // tm,),
        in_specs=[_tok(tm, D_FF), _res((D_FF, D_MODEL)), _tok(tm, D_MODEL), _res((1, D_MODEL)), _tok(tm, D_MODEL)],
        out_specs=[_tok(tm, D_MODEL), _tok(tm, D_MODEL), _acc((1, D_MODEL)), _acc((1, 1))],
        out_shape=[jax.ShapeDtypeStruct((T, D_MODEL), BF16), jax.ShapeDtypeStruct((T, D_MODEL), F32),
                   jax.ShapeDtypeStruct((1, D_MODEL), F32), jax.ShapeDtypeStruct((1, 1), F32)],
        compiler_params=_params(1),
    )(act, w_down, x2, g4, tgt)


def _ffn_act_bwd(df, w_down, act_dgate, act_dup, tm):
    T = df.shape[0]

    def body(df_ref, w_ref, ag_ref, au_ref, dg_ref, du_ref):
        dact = _dot_nt(df_ref[...], w_ref[...])
        dg_ref[...] = (dact * ag_ref[...].astype(F32)).astype(BF16)
        du_ref[...] = (dact * au_ref[...].astype(F32)).astype(BF16)

    return pl.pallas_call(
        body, name="ffn_act_bwd", grid=(T // tm,),
        in_specs=[_tok(tm, D_MODEL), _res((D_FF, D_MODEL)), _tok(tm, D_FF), _tok(tm, D_FF)],
        out_specs=[_tok(tm, D_FF)] * 2,
        out_shape=[jax.ShapeDtypeStruct((T, D_FF), BF16)] * 2,
        compiler_params=_params(1),
    )(df, w_down, act_dgate, act_dup)


def _ffn_in_bwd(dgate, dup, wg_t, wu_t, x2, mix, dy, g3, g2, tm):
    T = x2.shape[0]

    def body(dg_ref, du_ref, wg_ref, wu_ref, x2_ref, mix_ref, dy_ref, g3_ref, g2_ref,
             dx2_ref, dmix_ref, dg3_ref, dg2_ref):
        @pl.when(pl.program_id(0) == 0)
        def _():
            dg3_ref[...] = jnp.zeros_like(dg3_ref)
            dg2_ref[...] = jnp.zeros_like(dg2_ref)

        dh2 = _dot(dg_ref[...], wg_ref[...]) + _dot(du_ref[...], wu_ref[...])
        dn, dg3 = _rms_bwd(x2_ref[...], g3_ref[...], dh2)
        dx2 = dy_ref[...] + dn
        dx2_ref[...] = dx2
        dg3_ref[...] += dg3
        dmix, dg2 = _rms_bwd(mix_ref[...], g2_ref[...], dx2)
        dg2_ref[...] += dg2
        dmix_ref[...] = dmix.astype(BF16)

    return pl.pallas_call(
        body, name="ffn_in_bwd", grid=(T // tm,),
        in_specs=[_tok(tm, D_FF), _tok(tm, D_FF), _res((D_FF, D_MODEL)), _res((D_FF, D_MODEL)),
                  _tok(tm, D_MODEL), _tok(tm, D_MODEL), _tok(tm, D_MODEL), _res((1, D_MODEL)), _res((1, D_MODEL))],
        out_specs=[_tok(tm, D_MODEL), _tok(tm, D_MODEL), _acc((1, D_MODEL)), _acc((1, D_MODEL))],
        out_shape=[jax.ShapeDtypeStruct((T, D_MODEL), F32), jax.ShapeDtypeStruct((T, D_MODEL), BF16),
                   jax.ShapeDtypeStruct((1, D_MODEL), F32), jax.ShapeDtypeStruct((1, D_MODEL), F32)],
        compiler_params=_params(1),
    )(dgate, dup, wg_t, wu_t, x2, mix, dy, g3, g2)


def _mix_bwd(dmix, w_out, tm):
    T = dmix.shape[0]

    def body(d_ref, w_ref, dp_ref, da_ref):
        dcat = _dot_nt(d_ref[...], w_ref[...])
        dp_ref[...] = dcat[:, :POOL_W].astype(BF16)
        _store_packed(da_ref, dcat[:, POOL_W:])

    return pl.pallas_call(
        body, name="mix_bwd", grid=(T // tm,),
        in_specs=[_tok(tm, D_MODEL), _res((D_MODEL, D_MODEL))],
        out_specs=[_tok(tm, POOL_W), _tok_packed(tm, ATTN_W)],
        out_shape=[jax.ShapeDtypeStruct((T, POOL_W), BF16), _packed(T, ATTN_W)],
        compiler_params=_params(1),
    )(dmix, w_out)


def _attn_bwd(q, k, v, dout, out, lse, dil):
    T = 2 * q.shape[1]
    reps = _units_per_step(dil, True)
    nb = T // _chunk_tokens(dil, reps)
    cw, cur, prv, prv_out, heads, _ = _attn_specs(dil, nb, reps)
    ncb = ATTN_W // cw
    heads_per_step = cw // HEAD_DIM
    n_str = min(dil, 2)

    def body(q_ref, kc_ref, kp_ref, vc_ref, vp_ref, do_ref, o_ref, lse_ref,
             dq_ref, dkc_ref, dkp_ref, dvc_ref, dvp_ref):
        j = pl.program_id(1)
        lane = lax.broadcasted_iota(jnp.int32, (BLK, LANES), 1)
        lo = lane < HEAD_DIM

        def unit_grads(r2, g):
            valid = _attn_mask(True if dil == 1 and r2 > 0 else pl.program_id(0) > 0)
            valid2 = jnp.concatenate([valid, valid], axis=0)
            lse_tiles = _load_streams_f32(lse_ref, dil, r2, None)
            qs, kcs, vcs, dos, os_ = [_load_streams(r, dil, r2, g) for r in (q_ref, kc_ref, vc_ref, do_ref, o_ref)]
            kps, vps = [_load_prev_streams(p, c, dil, reps, r2, g) for p, c in ((kp_ref, kc_ref), (vp_ref, vc_ref))]
            dqs, dks, dvs = [], [], []
            for e in range(n_str):
                qg = qs[e] * 0.125
                dog = dos[e]
                kcat = jnp.concatenate([kps[e], kcs[e]], axis=0)
                vcat = jnp.concatenate([vps[e], vcs[e]], axis=0)
                prod = dog.astype(F32) * os_[e].astype(F32)
                h0 = j * heads_per_step + 2 * g
                q2 = _stack_heads(qg, lo)
                do2 = _stack_heads(dog, lo)
                lse2 = jnp.concatenate([_head_col(lse_tiles[e], lane, h0), _head_col(lse_tiles[e], lane, h0 + 1)], axis=0)
                dsum2 = jnp.concatenate([jnp.sum(jnp.where(lo, prod, 0.0), axis=1, keepdims=True),
                                         jnp.sum(jnp.where(lo, 0.0, prod), axis=1, keepdims=True)], axis=0)
                p = jnp.exp(jnp.where(valid2, _dot_nt(q2, kcat), NEG) - lse2)
                ds = (p * (_dot_nt(do2, vcat) - dsum2)).astype(BF16)
                dvs.append(_dot_tn(p.astype(BF16), do2))
                dks.append(_dot_tn(ds, q2))
                dq2 = _dot(ds, kcat) * 0.125
                dqs.append(jnp.where(lo, dq2[:BLK], dq2[BLK:]))
            return dqs, dks, dvs

        for g in range(cw // LANES):
            if dil > 1:
                for rep in range(reps):
                    r2 = pl.program_id(2) * reps + rep
                    dqs, dks, dvs = unit_grads(r2, g)
                    _store_streams(dq_ref, dil, r2, g, dqs)
                    _store_streams(dkp_ref, dil, r2, g, [t[:BLK] for t in dks])
                    _store_streams(dkc_ref, dil, r2, g, [t[BLK:] for t in dks])
                    _store_streams(dvp_ref, dil, r2, g, [t[:BLK] for t in dvs])
                    _store_streams(dvc_ref, dil, r2, g, [t[BLK:] for t in dvs])
            else:
                blocks = [unit_grads(b, g) for b in range(reps)]
                for b, (dqs, dks, dvs) in enumerate(blocks):
                    _store_streams(dq_ref, 1, b, g, dqs)
                    for cur_ref, prev_ref, which in ((dkc_ref, dkp_ref, 1), (dvc_ref, dvp_ref, 2)):
                        own = blocks[b][which][0][BLK:]
                        if b + 1 < reps:
                            own = own + blocks[b + 1][which][0][:BLK]
                        _store_streams(cur_ref, 1, b, g, [own])
                        edge = blocks[0][which][0][:BLK] if b == reps - 1 else jnp.zeros((BLK, LANES), F32)
                        _store_streams(prev_ref, 1, b, g, [edge])

    return pl.pallas_call(
        body, name=f"attn_bwd_d{dil}", grid=(nb, ncb, _unit_steps(dil, reps)),
        in_specs=[cur, cur, prv, cur, prv, cur, cur, heads],
        out_specs=[cur, cur, prv_out, cur, prv_out],
        out_shape=[_packed(T, ATTN_W)] * 5,
        compiler_params=_params(3),
    )(q, k, k, v, v, dout, out, lse)


def _pool_bwd(u, dy, wbd, scale, tm):
    T = u.shape[0]
    nt = T // tm
    hb = tm // POOL_HALO

    def body(u_ref, prev_ref, dy_ref, next_ref, w_ref, sc_ref, du_ref, dw_ref, dsc_ref):
        i = pl.program_id(0)

        @pl.when(i == 0)
        def _():
            dw_ref[...] = jnp.zeros_like(dw_ref)
            dsc_ref[...] = jnp.zeros_like(dsc_ref)

        w = w_ref[...]
        sc = sc_ref[...]
        d = _pool_delta(u_ref[...], prev_ref[...], i, tm).astype(BF16)
        dyc = dy_ref[...].astype(F32)
        dsc_ref[...] += jnp.sum(dyc * _dot(d, w), axis=0, keepdims=True)
        nxt = jnp.where(i < nt - 1, next_ref[...].astype(F32), 0.0)
        dypre = (jnp.concatenate([dyc, nxt], axis=0) * sc).astype(BF16)
        dw_ref[...] += _dot_tn(d, dypre[:tm])
        dd = _dot_nt(dypre, w)
        n = tm + POOL_HALO
        lane = lax.broadcasted_iota(jnp.int32, (n, POOL_W), 1)
        row = lax.broadcasted_iota(jnp.int32, (n, POOL_W), 0) + i * tm
        gx = dd / jnp.minimum(row + 1, _pool_window(lane)).astype(F32)
        a2 = gx + pltpu.roll(gx, n - 1, 0)
        a4 = a2 + pltpu.roll(a2, n - 2, 0)
        a8 = a4 + pltpu.roll(a4, n - 4, 0)
        a16 = a8 + pltpu.roll(a8, n - 8, 0)
        fs = _pool_select(lane[:tm], a2[:tm], a4[:tm], a8[:tm], a16[:tm])
        du_ref[...] = (fs - dd[:tm]).astype(BF16)

    return pl.pallas_call(
        body, name="pool_bwd", grid=(nt,),
        in_specs=[_tok(tm, POOL_W), pl.BlockSpec((POOL_HALO, POOL_W), lambda i: (jnp.maximum(i * hb - 1, 0), 0)),
                  _tok(tm, POOL_W), pl.BlockSpec((POOL_HALO, POOL_W), lambda i: (jnp.minimum((i + 1) * hb, nt * hb - 1), 0)),
                  _res((POOL_W, POOL_W)), _res((1, POOL_W))],
        out_specs=[_tok(tm, POOL_W), _acc((POOL_W, POOL_W)), _acc((1, POOL_W))],
        out_shape=[jax.ShapeDtypeStruct((T, POOL_W), BF16), jax.ShapeDtypeStruct((POOL_W, POOL_W), F32),
                   jax.ShapeDtypeStruct((1, POOL_W), F32)],
        compiler_params=_params(1),
    )(u, u, dy, dy, wbd, scale)


def _dproj_combine(du, dqs, dkcs, dkps, dvcs, dvps, cos, sin, tm):
    T = du.shape[0]
    n_cfg = len(dqs)

    def body(*refs):
        du_ref = refs[0]
        groups = [refs[1 + j * n_cfg:1 + (j + 1) * n_cfg] for j in range(5)]
        c_ref, s_ref, out_ref = refs[1 + 5 * n_cfg:]
        tot = lambda rs: sum(_load_packed(r).astype(F32) for r in rs)
        c = c_ref[...]
        s = s_ref[...]
        dq = _rope(tot(groups[0]), c, s, -1.0)
        dk = _rope(tot(groups[1]) + tot(groups[2]), c, s, -1.0)
        dv = tot(groups[3]) + tot(groups[4])
        out_ref[...] = jnp.concatenate([du_ref[...], dq.astype(BF16), dk.astype(BF16), dv.astype(BF16)], axis=1)

    return pl.pallas_call(
        body, name="dproj_combine", grid=(T // tm,),
        in_specs=[_tok(tm, POOL_W)] + [_tok_packed(tm, ATTN_W)] * (5 * n_cfg) + [_tok(tm, LANES)] * 2,
        out_specs=_tok(tm, IN_W),
        out_shape=jax.ShapeDtypeStruct((T, IN_W), BF16),
        compiler_params=_params(1),
    )(du, *dqs, *dkcs, *dkps, *dvcs, *dvps, cos, sin)


def _proj_bwd(dproj, w_in_t, x, dx2, g1, tm):
    T = x.shape[0]

    def body(d_ref, w_ref, x_ref, r_ref, g_ref, dx_ref, dg_ref):
        @pl.when(pl.program_id(0) == 0)
        def _():
            dg_ref[...] = jnp.zeros_like(dg_ref)

        dn, dg = _rms_bwd(x_ref[...], g_ref[...], _dot(d_ref[...], w_ref[...]))
        dg_ref[...] += dg
        dx_ref[...] = r_ref[...] + dn

    return pl.pallas_call(
        body, name="proj_bwd", grid=(T // tm,),
        in_specs=[_tok(tm, IN_W), _res((IN_W, D_MODEL)), _tok(tm, D_MODEL), _tok(tm, D_MODEL), _res((1, D_MODEL))],
        out_specs=[_tok(tm, D_MODEL), _acc((1, D_MODEL))],
        out_shape=[jax.ShapeDtypeStruct((T, D_MODEL), F32), jax.ShapeDtypeStruct((1, D_MODEL), F32)],
        compiler_params=_params(1),
    )(dproj, w_in_t, x, dx2, g1)


def _wgrad(a, b, name, tile_m, tk):
    T, M = a.shape
    N = b.shape[1]
    nk = T // tk

    def body(a_ref, b_ref, o_ref, acc_ref):
        kk = pl.program_id(1)

        @pl.when(kk == 0)
        def _():
            acc_ref[...] = jnp.zeros_like(acc_ref)

        acc_ref[...] += _dot_tn(a_ref[...], b_ref[...])

        @pl.when(kk == nk - 1)
        def _():
            o_ref[...] = acc_ref[...].astype(BF16)

    return pl.pallas_call(
        body, name=name, grid=(M // tile_m, nk),
        in_specs=[pl.BlockSpec((tk, tile_m), lambda j, kk: (kk, j)), pl.BlockSpec((tk, N), lambda j, kk: (kk, 0))],
        out_specs=pl.BlockSpec((tile_m, N), lambda j, kk: (j, 0)),
        out_shape=jax.ShapeDtypeStruct((M, N), BF16),
        scratch_shapes=[pltpu.VMEM((tile_m, N), F32)],
        compiler_params=_params(2),
    )(a, b)


def _exchange(arrs, scatter, name):
    n = len(arrs)
    out_shapes = [jax.ShapeDtypeStruct((N_DEV,) + (a.shape[1:] if sc else a.shape), a.dtype)
                  for a, sc in zip(arrs, scatter)]

    def body(*refs):
        ins, outs = refs[:n], refs[n:2 * n]
        send_sems, recv_sems, loc_sems = refs[2 * n:]
        x, y, c = lax.axis_index("x"), lax.axis_index("y"), lax.axis_index("c")
        me = 4 * x + 2 * y + c
        local, sends, recvs = [], [], []
        for i in range(n):
            own = ins[i].at[me] if scatter[i] else ins[i]
            loc = pltpu.make_async_copy(own, outs[i].at[me], loc_sems.at[i])
            loc.start()
            local.append(loc)
            for kbits in range(1, N_DEV):
                px = 1 - x if kbits & 4 else x
                py = 1 - y if kbits & 2 else y
                pc = 1 - c if kbits & 1 else c
                pid = 4 * px + 2 * py + pc
                src = ins[i].at[pid] if scatter[i] else ins[i]
                cp = pltpu.make_async_remote_copy(
                    src_ref=src, dst_ref=outs[i].at[me],
                    send_sem=send_sems.at[i, kbits - 1], recv_sem=recv_sems.at[i, kbits - 1],
                    device_id=(px, py, pc), device_id_type=pl.DeviceIdType.MESH)
                cp.start()
                sends.append(cp)
                recvs.append(pltpu.make_async_remote_copy(
                    src_ref=src, dst_ref=outs[i].at[pid],
                    send_sem=send_sems.at[i, kbits - 1], recv_sem=recv_sems.at[i, kbits - 1],
                    device_id=(px, py, pc), device_id_type=pl.DeviceIdType.MESH))
        for cp in recvs:
            cp.wait_recv()
        for cp in sends:
            cp.wait_send()
        for cp in local:
            cp.wait()

    hbm = pl.BlockSpec(memory_space=pl.ANY)
    return pl.pallas_call(
        body, name=name, in_specs=[hbm] * n, out_specs=[hbm] * n, out_shape=out_shapes,
        scratch_shapes=[pltpu.SemaphoreType.DMA((n, N_DEV - 1)), pltpu.SemaphoreType.DMA((n, N_DEV - 1)),
                        pltpu.SemaphoreType.DMA((n,))],
    )(*arrs)


def _gather_two_level(arr, name):
    def body(x_ref, out_ref, send_sems, recv_sems, local_sem):
        x, y, c = lax.axis_index("x"), lax.axis_index("y"), lax.axis_index("c")
        me, sibling = (x, y, c), (x, y, 1 - c)
        chips = [(1 - x, y), (x, 1 - y), (1 - x, 1 - y)]
        slot = lambda px, py, pc: out_ref.at[4 * px + 2 * py + pc]

        def copy(k, block, to, src=None):
            return pltpu.make_async_remote_copy(
                src_ref=slot(*block) if src is None else src, dst_ref=slot(*block),
                send_sem=send_sems.at[k], recv_sem=recv_sems.at[k],
                device_id=to, device_id_type=pl.DeviceIdType.MESH)

        mine = pltpu.make_async_copy(x_ref, slot(*me), local_sem)
        mine.start()
        first = [copy(0, me, sibling, src=x_ref)]
        first += [copy(1 + i, me, (*chip, c), src=x_ref) for i, chip in enumerate(chips)]
        for cp in first:
            cp.start()
        passed = [copy(4 + i, (*chip, c), sibling) for i, chip in enumerate(chips)]
        for i, chip in enumerate(chips):
            copy(1 + i, (*chip, c), me).wait_recv()
            passed[i].start()
        copy(0, sibling, me).wait_recv()
        for i, chip in enumerate(chips):
            copy(4 + i, (*chip, 1 - c), me).wait_recv()
        for cp in first + passed:
            cp.wait_send()
        mine.wait()

    hbm = pl.BlockSpec(memory_space=pl.ANY)
    return pl.pallas_call(
        body, name=name, in_specs=[hbm], out_specs=hbm,
        out_shape=jax.ShapeDtypeStruct((N_DEV,) + arr.shape, arr.dtype),
        scratch_shapes=[pltpu.SemaphoreType.DMA((N_DEV - 1,)), pltpu.SemaphoreType.DMA((N_DEV - 1,)),
                        pltpu.SemaphoreType.DMA],
    )(arr)


def _peers(x, y, c):
    for kbits in range(1, N_DEV):
        px = 1 - x if kbits & 4 else x
        py = 1 - y if kbits & 2 else y
        pc = 1 - c if kbits & 1 else c
        yield kbits - 1, (px, py, pc), 4 * px + 2 * py + pc


def _peer_copies(ins, lands, scatter, send_sems, recv_sems, incoming):
    x, y, c = lax.axis_index("x"), lax.axis_index("y"), lax.axis_index("c")
    me = 4 * x + 2 * y + c
    copies = []
    for i in range(len(ins)):
        for k, peer, pid in _peers(x, y, c):
            slot = i * (N_DEV - 1) + k
            copies.append(pltpu.make_async_remote_copy(
                src_ref=ins[i].at[pid] if scatter[i] else ins[i], dst_ref=lands[i].at[pid if incoming else me],
                send_sem=send_sems.at[slot], recv_sem=recv_sems.at[slot],
                device_id=peer, device_id_type=pl.DeviceIdType.MESH))
    return copies


_HBM = pl.BlockSpec(memory_space=pltpu.HBM)
_SEM = pl.BlockSpec(memory_space=pltpu.SEMAPHORE)
_DATAFLOW = pltpu.SideEffectType.DATAFLOW_SIDE_EFFECTING


def _exchange_start(arrs, scatter, after, name):
    n = len(arrs)
    lands = [lax.empty((N_DEV,) + (a.shape[1:] if sc else a.shape), a.dtype) for a, sc in zip(arrs, scatter)]

    def body(*refs):
        ins, lz = refs[:n], refs[n:2 * n]
        send_sems, recv_sems = refs[2 * n + 1:2 * n + 3]
        token = refs[-1]
        for cp in _peer_copies(ins, lz, scatter, send_sems, recv_sems, False):
            cp.start()
        token[...] = jnp.zeros_like(token)

    sem_shape = pltpu.SemaphoreType.DMA((n * (N_DEV - 1),))
    outs = pl.pallas_call(
        body, name=name,
        out_shape=(sem_shape, sem_shape, *[pltpu.HBM(a.shape, a.dtype) for a in arrs + lands],
                   jax.ShapeDtypeStruct((8, LANES), F32)),
        in_specs=[_HBM] * (2 * n) + [pl.BlockSpec(memory_space=pl.ANY)],
        out_specs=(_SEM, _SEM, *[_HBM] * (2 * n), pl.BlockSpec(memory_space=pltpu.VMEM)),
        input_output_aliases={i: 2 + i for i in range(2 * n)},
        compiler_params=pltpu.CompilerParams(has_side_effects=_DATAFLOW),
    )(*[pltpu.with_memory_space_constraint(a, pltpu.HBM) for a in arrs + lands], after)
    return outs[0], outs[1], list(outs[2:2 + n]), list(outs[2 + n:2 + 2 * n]), outs[-1]


def _exchange_wait(handle, scatter, after, name):
    send_sems, recv_sems, srcs, lands, _ = handle
    n = len(srcs)

    def body(*refs):
        ins, lz = refs[:n], refs[n:2 * n]
        for cp in _peer_copies(ins, lz, scatter, refs[2 * n], refs[2 * n + 1], False):
            cp.wait_send()
        for cp in _peer_copies(ins, lz, scatter, refs[2 * n], refs[2 * n + 1], True):
            cp.wait_recv()

    outs = pl.pallas_call(
        body, name=name,
        out_shape=[pltpu.HBM(a.shape, a.dtype) for a in srcs + lands],
        in_specs=[_HBM] * (2 * n) + [_SEM, _SEM, pl.BlockSpec(memory_space=pl.ANY)],
        out_specs=[_HBM] * (2 * n),
        input_output_aliases={i: i for i in range(2 * n)},
        compiler_params=pltpu.CompilerParams(has_side_effects=_DATAFLOW),
    )(*srcs, *lands, send_sems, recv_sems, after)
    return list(outs[:n]), list(outs[n:])


def _fill_own(lands, srcs, scatter):
    me = 4 * lax.axis_index("x") + 2 * lax.axis_index("y") + lax.axis_index("c")
    own = [lax.dynamic_index_in_dim(s, me, 0, keepdims=False) if sc else s for s, sc in zip(srcs, scatter)]
    return [lax.dynamic_update_index_in_dim(land, o, me, 0) for land, o in zip(lands, own)]


def _slot_sum(parts, name, tr):
    _, R, C = parts.shape

    def body(p_ref, o_ref):
        acc = p_ref[0].astype(F32)
        for s in range(1, N_DEV):
            acc = acc + p_ref[s].astype(F32)
        o_ref[...] = acc

    return pl.pallas_call(
        body, name=name, grid=(R // tr,),
        in_specs=[pl.BlockSpec((N_DEV, tr, C), lambda i: (0, i, 0))],
        out_specs=pl.BlockSpec((tr, C), lambda i: (i, 0)),
        out_shape=jax.ShapeDtypeStruct((R, C), F32),
        compiler_params=_params(1),
    )(parts)


def _adamw(w, g, m, v, name):
    def body(w_ref, g_ref, m_ref, v_ref, d_ref, nm_ref, nv_ref):
        g = g_ref[...]
        nm = ADAM_B1 * m_ref[...] + (1.0 - ADAM_B1) * g
        nv = ADAM_B2 * v_ref[...] + (1.0 - ADAM_B2) * jnp.square(g)
        m_hat = nm / (1.0 - ADAM_B1 ** ADAM_STEP)
        v_hat = nv / (1.0 - ADAM_B2 ** ADAM_STEP)
        d_ref[...] = -ADAM_LR * (m_hat / (jnp.sqrt(v_hat) + ADAM_EPS) + ADAM_WD * w_ref[...])
        nm_ref[...] = nm
        nv_ref[...] = nv

    return pl.pallas_call(
        body, name=name, out_shape=[jax.ShapeDtypeStruct(w.shape, F32)] * 3,
        compiler_params=pltpu.CompilerParams(vmem_limit_bytes=VMEM_LIMIT),
    )(w, g, m, v)


def _rope_tables(T):
    half = HEAD_DIM // 2
    freqs = ROPE_THETA ** (-jnp.arange(half, dtype=F32) * (2.0 / HEAD_DIM))
    ang = jnp.arange(T).astype(F32)[:, None] * jnp.tile(freqs, LANES // half)[None, :]
    sign = jnp.tile(jnp.concatenate([-jnp.ones((half,), F32), jnp.ones((half,), F32)]), LANES // HEAD_DIM)
    return jnp.cos(ang), jnp.sin(ang) * sign[None, :]


def _block_diag(w_pool):
    wbd = jnp.zeros((POOL_W, POOL_W), F32)
    g = POOL_W // len(POOL_WINDOWS)
    for i in range(len(POOL_WINDOWS)):
        wbd = wbd.at[i * g:(i + 1) * g, i * g:(i + 1) * g].set(w_pool[i])
    return wbd


def _pack_small(g1, w_pool, pool_scale, g2, g3, g4, extra):
    pad = lambda a: jnp.pad(a.reshape(1, -1), ((0, 0), (0, D_MODEL - a.size)))
    rows = [g1.reshape(1, -1), g2.reshape(1, -1), g3.reshape(1, -1), g4.reshape(1, -1),
            w_pool.reshape(-1, D_MODEL), pad(pool_scale), pad(extra)]
    buf = jnp.concatenate(rows, axis=0)
    return jnp.pad(buf, ((0, SMALL_ROWS - buf.shape[0]), (0, 0)))


def _unpack_small(buf):
    n_pool = len(POOL_WINDOWS) * (POOL_W // len(POOL_WINDOWS)) ** 2 // D_MODEL
    g = POOL_W // len(POOL_WINDOWS)
    return (buf[0:1], buf[4:4 + n_pool].reshape(1, len(POOL_WINDOWS), g, g), buf[4 + n_pool:5 + n_pool, :POOL_W],
            buf[1:2], buf[2:3], buf[3:4], buf[5 + n_pool])


class _LocalStep:
    def __init__(self, x, tgt, g1, w_pool, pool_scale, g2, g3, g4):
        self.x, self.tgt, self.pool_scale = x, tgt, pool_scale
        self.g1, self.g2, self.g3, self.g4 = g1, g2, g3, g4
        self.cos, self.sin = _rope_tables(x.shape[0])
        self.wbd = _block_diag(w_pool).astype(BF16)

    def mixer_fwd(self, w_in_t, token):
        self.w_in_t = w_in_t
        self.h1, self.u, self.q, self.k, self.v = _proj_fwd(
            self.x, self.g1 + token[0, 0], w_in_t, self.cos, self.sin, 1024)
        self.pool = _pool_fwd(self.u, self.wbd, self.pool_scale, 1024)
        prev = None
        for j, dil in enumerate(DILATIONS):
            prev = _attn_fwd(self.q, self.k, self.v, dil, prev, j == len(DILATIONS) - 1)
        self.attn, self.lse = prev
        return self.attn

    def ffn_fwd_bwd(self, w_out, wg_t, wu_t, w_down):
        self.w_out, self.wg_t, self.wu_t = w_out, wg_t, wu_t
        self.cat, self.mix, self.x2, h2 = _mix_fwd(self.pool, self.attn, self.x, w_out, self.g2, self.g3, 1024)
        act_dgate, act_dup, act = _ffn_up(h2, wg_t, wu_t, 256)
        df, self.dy, self.dg4, self.loss = _ffn_down_loss(act, w_down, self.x2, self.g4, self.tgt, 512)
        self.dgate, self.dup = _ffn_act_bwd(df, w_down, act_dgate, act_dup, 512)
        return (_wgrad(self.dgate, h2, "wgrad_gate", D_FF // 2, 1024), _wgrad(self.dup, h2, "wgrad_up", D_FF // 2, 1024),
                _wgrad(act, df, "wgrad_down", D_FF // 2, 1024))

    def mixer_bwd(self, token):
        self.dx2, dmix, self.dg3, self.dg2 = _ffn_in_bwd(
            self.dgate, self.dup, self.wg_t, self.wu_t, self.x2, self.mix, self.dy, self.g3 + token[0, 0], self.g2, 512)
        dpool, dattn = _mix_bwd(dmix, self.w_out, 1024)
        parts = [_attn_bwd(self.q, self.k, self.v, dattn, self.attn, self.lse, dil) for dil in DILATIONS]
        du, dwbd, self.dscale = _pool_bwd(self.u, dpool, self.wbd, self.pool_scale, 1024)
        g = POOL_W // len(POOL_WINDOWS)
        self.dw_pool = jnp.stack([dwbd[i * g:(i + 1) * g, i * g:(i + 1) * g] for i in range(len(POOL_WINDOWS))])
        self.dproj = _dproj_combine(du, *[[p[j] for p in parts] for j in range(5)], self.cos, self.sin, 512)
        return _wgrad(self.dproj, self.h1, "wgrad_in", IN_W // 2, 1024), _wgrad(self.cat, dmix, "wgrad_out", D_MODEL, 1024)

    def input_bwd(self, token):
        grad_x, dg1 = _proj_bwd(self.dproj, self.w_in_t, self.x, self.dx2, self.g1 + token[0, 0], 1024)
        return self.loss, grad_x, (dg1, self.dw_pool, self.dscale, self.dg2, self.dg3, self.dg4)


def _local_step(x, tgt, g1, w_pool, pool_scale, g2, g3, g4, w_in_t, w_out, wg_t, wu_t, w_down):
    zero = jnp.zeros((8, LANES), F32)
    step = _LocalStep(x, tgt, g1, w_pool, pool_scale, g2, g3, g4)
    step.mixer_fwd(w_in_t, zero)
    dw_gate, dw_up, dw_down = step.ffn_fwd_bwd(w_out, wg_t, wu_t, w_down)
    dw_in, dw_out = step.mixer_bwd(zero)
    loss, grad_x, small = step.input_bwd(zero)
    return loss, grad_x, small, (dw_in, dw_out, dw_gate, dw_up, dw_down)


def kernel(x, ln_pre_mix, w_in, w_pool, pool_scale, w_out, ln_post_mix, ln_pre_ffn, w_gate, w_up, w_down, ln_post_ffn, loss_target, m_ln_pre_mix, m_w_in, m_w_pool, m_pool_scale, m_w_out, m_ln_post_mix, m_ln_pre_ffn, m_w_gate, m_w_up, m_w_down, m_ln_post_ffn, v_ln_pre_mix, v_w_in, v_w_pool, v_pool_scale, v_w_out, v_ln_post_mix, v_ln_pre_ffn, v_w_gate, v_w_up, v_w_down, v_ln_post_ffn):
    shards = [w_in[0].T.astype(BF16), w_out[0].astype(BF16), w_gate[0].T.astype(BF16),
              w_up[0].T.astype(BF16), w_down[0].astype(BF16)]
    flat = lambda a: a.reshape(-1, D_MODEL)
    blocks = lambda a: a.reshape(N_DEV, -1, D_MODEL)
    step = _LocalStep(x[0], loss_target[0], ln_pre_mix, w_pool[0], pool_scale, ln_post_mix, ln_pre_ffn, ln_post_ffn)

    w_in_t = flat(_gather_two_level(shards[0], "gather_w_in"))
    rest = _exchange_start(shards[1:], [False] * 4, w_in_t, "gather_rest_start")
    attn = step.mixer_fwd(w_in_t, rest[4])
    srcs, lands = _exchange_wait(rest, [False] * 4, attn, "gather_rest_wait")
    w_out_f, wg_t, wu_t, w_down_f = [flat(a) for a in _fill_own(lands, srcs, [False] * 4)]

    ffn = _exchange_start([blocks(a) for a in step.ffn_fwd_bwd(w_out_f, wg_t, wu_t, w_down_f)], [True] * 3,
                          step.dgate, "grads_ffn_start")
    mixer = _exchange_start([blocks(a) for a in step.mixer_bwd(ffn[4])], [True] * 2, step.dproj, "grads_mixer_start")
    loss, grad_x, small = step.input_bwd(mixer[4])
    got = []
    for handle, n_arr, nm in ((mixer, 2, "grads_mixer"), (ffn, 3, "grads_ffn")):
        srcs, lands = _exchange_wait(handle, [True] * n_arr, grad_x, nm + "_wait")
        got += _fill_own(lands, srcs, [True] * n_arr)
    sums = [_slot_sum(got[i], f"sum_grad_{i}", got[i].shape[1] // 2) for i in range(5)]

    small_buf = _pack_small(small[0], small[1], small[2], small[3], small[4], small[5], loss)
    small_sum = _slot_sum(_exchange([small_buf], [False], "gather_small")[0], "sum_small", SMALL_ROWS)

    g_in, g_out, g_gate, g_up, g_down = sums[0].T, sums[1], sums[2].T, sums[3].T, sums[4]
    upd = [_adamw(w[0], g, m[0], v[0], f"adamw_{nm}") for nm, w, g, m, v in (
        ("in", w_in, g_in, m_w_in, v_w_in), ("out", w_out, g_out, m_w_out, v_w_out),
        ("gate", w_gate, g_gate, m_w_gate, v_w_gate), ("up", w_up, g_up, m_w_up, v_w_up),
        ("down", w_down, g_down, m_w_down, v_w_down))]
    pack = lambda a, b, c, d, e, f: _pack_small(a, b[0], c, d, e, f, jnp.zeros((1,), F32))
    small_upd = _adamw(
        pack(ln_pre_mix, w_pool, pool_scale, ln_post_mix, ln_pre_ffn, ln_post_ffn), small_sum,
        pack(m_ln_pre_mix, m_w_pool, m_pool_scale, m_ln_post_mix, m_ln_pre_ffn, m_ln_post_ffn),
        pack(v_ln_pre_mix, v_w_pool, v_pool_scale, v_ln_post_mix, v_ln_pre_ffn, v_ln_post_ffn), "adamw_small")

    def tree(small6, big5):
        s1, spool, sscale, s2, s3, s4 = small6
        b_in, b_out, b_gate, b_up, b_down = [b[None] for b in big5]
        return [s1, b_in, spool, sscale, b_out, s2, s3, b_gate, b_up, b_down, s4]

    g_small = _unpack_small(small_sum)
    outs = [g_small[6][0], grad_x[None]]
    outs += tree(g_small[:6], [g_in, g_out, g_gate, g_up, g_down])
    for j in range(3):
        outs += tree(_unpack_small(small_upd[j])[:6], [u[j] for u in upd])
    return tuple(outs)
```

```python
import jax
import jax.numpy as jnp
from jax import lax
from jax.experimental import pallas as pl
from jax.experimental.pallas import tpu as pltpu

F32 = jnp.float32
BF16 = jnp.bfloat16

D_MODEL = 1024
POOL_W = 256
ATTN_W = 768
IN_W = 2560
D_FF = 2816
POOL_WINDOWS = (2, 4, 8, 16)
POOL_HALO = 16
DILATIONS = (1, 4, 16)
BLK = 128
LANES = 128
HEAD_DIM = 64
N_GROUPS = ATTN_W // LANES
ROPE_THETA = 10000.0
EPS = 1e-6
NEG = -1e30
N_DEV = 8
SMALL_ROWS = 24

ADAM_LR = 0.001
ADAM_B1 = 0.9
ADAM_B2 = 0.999
ADAM_EPS = 1e-08
ADAM_WD = 0.01
ADAM_STEP = 10

VMEM_LIMIT = 56 * 1024 * 1024


def _dot(a, b):
    return jnp.dot(a, b, preferred_element_type=F32)


def _dot_nt(a, b):
    return lax.dot_general(a, b, (((1,), (1,)), ((), ())), preferred_element_type=F32)


def _dot_tn(a, b):
    return lax.dot_general(a, b, (((0,), (0,)), ((), ())), preferred_element_type=F32)


def _params(n_grid):
    return pltpu.CompilerParams(dimension_semantics=("arbitrary",) * n_grid, vmem_limit_bytes=VMEM_LIMIT)


def _tok(tm, c):
    return pl.BlockSpec((tm, c), lambda i: (i, 0))


def _res(shape):
    return pl.BlockSpec(shape, lambda i: (0,) * len(shape), pipeline_mode=pl.Buffered(1))


def _acc(shape):
    return pl.BlockSpec(shape, lambda i: (0,) * len(shape))


def _rms_fwd(x, g):
    r = lax.rsqrt(jnp.mean(x * x, axis=-1, keepdims=True) + EPS)
    return x * r * g


def _rms_bwd(x, g, dy):
    r = lax.rsqrt(jnp.mean(x * x, axis=-1, keepdims=True) + EPS)
    xh = x * r
    gd = dy * g
    dx = r * (gd - xh * jnp.mean(gd * xh, axis=-1, keepdims=True))
    return dx, jnp.sum(dy * xh, axis=0, keepdims=True)


def _rope(x, c, s, sign):
    lane = lax.broadcasted_iota(jnp.int32, (x.shape[0], LANES), 1)
    first = (lane % HEAD_DIM) < (HEAD_DIM // 2)
    outs = []
    for g in range(x.shape[1] // LANES):
        xg = x[:, g * LANES:(g + 1) * LANES]
        rot = jnp.where(first, pltpu.roll(xg, LANES - HEAD_DIM // 2, 1), pltpu.roll(xg, HEAD_DIM // 2, 1))
        outs.append(xg * c + sign * (rot * s))
    return jnp.concatenate(outs, axis=1)


def _proj_fwd(x, g1, w_in_t, cos, sin, tm):
    T = x.shape[0]

    def body(x_ref, g_ref, w_ref, c_ref, s_ref, h_ref, u_ref, q_ref, k_ref, v_ref):
        h = _rms_fwd(x_ref[...], g_ref[...]).astype(BF16)
        h_ref[...] = h
        proj = _dot_nt(h, w_ref[...])
        c = c_ref[...]
        s = s_ref[...]
        u_ref[...] = proj[:, :POOL_W]
        _store_packed(q_ref, _rope(proj[:, POOL_W:POOL_W + ATTN_W], c, s, 1.0))
        _store_packed(k_ref, _rope(proj[:, POOL_W + ATTN_W:POOL_W + 2 * ATTN_W], c, s, 1.0))
        _store_packed(v_ref, proj[:, POOL_W + 2 * ATTN_W:])

    return pl.pallas_call(
        body, name="proj_fwd", grid=(T // tm,),
        in_specs=[_tok(tm, D_MODEL), _res((1, D_MODEL)), _res((IN_W, D_MODEL)), _tok(tm, LANES), _tok(tm, LANES)],
        out_specs=[_tok(tm, D_MODEL), _tok(tm, POOL_W)] + [_tok_packed(tm, ATTN_W)] * 3,
        out_shape=[jax.ShapeDtypeStruct((T, D_MODEL), BF16), jax.ShapeDtypeStruct((T, POOL_W), F32)]
        + [_packed(T, ATTN_W)] * 3,
        compiler_params=_params(1),
    )(x, g1, w_in_t, cos, sin)


def _pool_window(lane):
    return jnp.where(lane < 64, 2, jnp.where(lane < 128, 4, jnp.where(lane < 192, 8, 16)))


def _pool_select(lane, a2, a4, a8, a16):
    return jnp.where(lane < 64, a2, jnp.where(lane < 128, a4, jnp.where(lane < 192, a8, a16)))


def _pool_delta(cur, prev, i, tm):
    prev = jnp.where(i > 0, prev, 0.0)
    ext = jnp.concatenate([prev, cur], axis=0)
    s2 = ext + pltpu.roll(ext, 1, 0)
    s4 = s2 + pltpu.roll(s2, 2, 0)
    s8 = s4 + pltpu.roll(s4, 4, 0)
    s16 = s8 + pltpu.roll(s8, 8, 0)
    lane = lax.broadcasted_iota(jnp.int32, (tm, POOL_W), 1)
    row = lax.broadcasted_iota(jnp.int32, (tm, POOL_W), 0) + i * tm
    ws = _pool_select(lane, s2[POOL_HALO:], s4[POOL_HALO:], s8[POOL_HALO:], s16[POOL_HALO:])
    cnt = jnp.minimum(row + 1, _pool_window(lane)).astype(F32)
    return ws / cnt - cur


def _pool_fwd(u, wbd, scale, tm):
    T = u.shape[0]
    hb = tm // POOL_HALO

    def body(u_ref, prev_ref, w_ref, sc_ref, o_ref):
        d = _pool_delta(u_ref[...], prev_ref[...], pl.program_id(0), tm)
        o_ref[...] = (_dot(d.astype(BF16), w_ref[...]) * sc_ref[...]).astype(BF16)

    return pl.pallas_call(
        body, name="pool_fwd", grid=(T // tm,),
        in_specs=[_tok(tm, POOL_W), pl.BlockSpec((POOL_HALO, POOL_W), lambda i: (jnp.maximum(i * hb - 1, 0), 0)),
                  _res((POOL_W, POOL_W)), _res((1, POOL_W))],
        out_specs=_tok(tm, POOL_W),
        out_shape=jax.ShapeDtypeStruct((T, POOL_W), BF16),
        compiler_params=_params(1),
    )(u, u, wbd, scale)


def _attn_mask(has_prev):
    qi = lax.broadcasted_iota(jnp.int32, (BLK, 2 * BLK), 0)
    kj = lax.broadcasted_iota(jnp.int32, (BLK, 2 * BLK), 1)
    dist = qi + BLK - kj
    return (dist >= 0) & (dist <= BLK) & ((kj >= BLK) | has_prev)


def _stack_heads(x, lo):
    zero = jnp.zeros_like(x)
    return jnp.concatenate([jnp.where(lo, x, zero), jnp.where(lo, zero, x)], axis=0)


def _head_col(tile, lane, h):
    return jnp.sum(jnp.where(lane == h, tile, 0.0), axis=1, keepdims=True)


def _attn_cols(dil):
    return ATTN_W // 2 if dil >= 16 else ATTN_W


def _units_per_step(dil, backward):
    return {1: 4, 4: 2, 16: 4}[dil]


def _chunk_tokens(dil, units):
    return BLK * (units if dil == 1 else dil)


def _unit_steps(dil, units):
    return 1 if dil == 1 else dil // 2 // units


def _attn_specs(dil, nb, units):
    cw = _attn_cols(dil)
    ch = _chunk_tokens(dil, units)
    wide = lambda f: pl.BlockSpec((cw // LANES, ch // 2, LANES), f)
    full = pl.BlockSpec((cw // LANES, ch, LANES), lambda n, j, r: (j, n, 0))
    cur = lambda n, j, r: (j, n, 0)
    prv = lambda n, j, r: (j, jnp.maximum(n - 1, 0), 0)
    prv_out = lambda n, j, r: (j, (n + nb - 1) % nb, 0)
    heads = pl.BlockSpec((ch, LANES), lambda n, j, r: (n, 0))
    return cw, wide(cur), wide(prv), wide(prv_out), heads, full


HIGH_HALF = 0xFFFF0000


def _pack(x):
    return pltpu.bitcast(x.astype(BF16), F32)


def _unpack(words):
    return pltpu.bitcast(words, BF16)


def _packed(rows, cols):
    return jax.ShapeDtypeStruct((cols // LANES, rows // 2, LANES), F32)


def _tok_packed(tm, cols):
    return pl.BlockSpec((cols // LANES, tm // 2, LANES), lambda i: (0, i, 0))


def _store_packed(ref, x):
    for g in range(x.shape[1] // LANES):
        ref[g] = _pack(x[:, g * LANES:(g + 1) * LANES])


def _load_packed(ref):
    return jnp.concatenate([_unpack(ref[g]) for g in range(ref.shape[0])], axis=1)


def _load_streams(ref, dil, r2, sl):
    if dil == 1:
        return [_unpack(ref.at[sl][pl.ds(r2 * (BLK // 2), BLK // 2), :])]
    words = lax.bitcast_convert_type(ref.at[sl][pl.ds(r2, BLK, stride=dil // 2), :], jnp.uint32)
    even = lax.bitcast_convert_type(words << 16, F32).astype(BF16)
    odd = lax.bitcast_convert_type(words & jnp.uint32(HIGH_HALF), F32).astype(BF16)
    return [even, odd]


def _load_prev_streams(prev_ref, cur_ref, dil, units, r2, sl):
    if dil > 1:
        return _load_streams(prev_ref, dil, r2, sl)
    return _load_streams(cur_ref, 1, r2 - 1, sl) if r2 > 0 else _load_streams(prev_ref, 1, units - 1, sl)


def _load_streams_f32(ref, dil, r2, sl):
    ref = ref if sl is None else ref.at[sl]
    if dil == 1:
        return [ref[pl.ds(r2 * BLK, BLK), :]]
    return [ref[pl.ds(2 * r2 + e, BLK, stride=dil), :] for e in range(2)]


def _store_streams_f32(ref, dil, r2, sl, tiles):
    ref = ref if sl is None else ref.at[sl]
    if dil == 1:
        ref[pl.ds(r2 * BLK, BLK), :] = tiles[0]
    else:
        for e, t in enumerate(tiles):
            ref[pl.ds(2 * r2 + e, BLK, stride=dil), :] = t


def _store_streams(ref, dil, r2, sl, tiles):
    if dil == 1:
        ref.at[sl][pl.ds(r2 * (BLK // 2), BLK // 2), :] = _pack(tiles[0])
    else:
        even, odd = [lax.bitcast_convert_type(t.astype(BF16).astype(F32), jnp.uint32) for t in tiles]
        words = (odd & jnp.uint32(HIGH_HALF)) | (even >> 16)
        ref.at[sl][pl.ds(r2, BLK, stride=dil // 2), :] = lax.bitcast_convert_type(words, F32)


def _attn_fwd(q, k, v, dil, others, last):
    T = 2 * q.shape[1]
    reps = _units_per_step(dil, False)
    nb = T // _chunk_tokens(dil, reps)
    first = not others
    cw, cur, prv, _, heads, full = _attn_specs(dil, nb, reps)
    ncb = ATTN_W // cw
    heads_per_step = cw // HEAD_DIM
    n_str = min(dil, 2)
    everything = None

    def body(*refs):
        q_ref, kc_ref, kp_ref, vc_ref, vp_ref = refs[:5]
        acc_ins, lse_ins = refs[5:5 + 2 * len(others):2], refs[6:6 + 2 * len(others):2]
        acc_ref, lse_ref = refs[-2:]
        j = pl.program_id(1)
        lane = lax.broadcasted_iota(jnp.int32, (BLK, LANES), 1)
        lo = lane < HEAD_DIM
        store_acc = _store_streams if last else _store_streams_f32

        def stream_pair(r2):
            valid = _attn_mask(True if dil == 1 and r2 > 0 else pl.program_id(0) > 0)
            lse_tiles = [jnp.zeros((BLK, LANES), F32) for _ in range(n_str)]
            own = []
            for g in range(cw // LANES):
                qs, kcs, vcs = [_load_streams(r, dil, r2, g) for r in (q_ref, kc_ref, vc_ref)]
                kps, vps = [_load_prev_streams(p, c, dil, reps, r2, g) for p, c in ((kp_ref, kc_ref), (vp_ref, vc_ref))]
                pairs = []
                for e in range(n_str):
                    qg = qs[e] * 0.125
                    kcat = jnp.concatenate([kps[e], kcs[e]], axis=0)
                    vcat = jnp.concatenate([vps[e], vcs[e]], axis=0)
                    pair = None
                    for hh in range(2):
                        h = j * heads_per_step + 2 * g + hh
                        hm = lo if hh == 0 else jnp.logical_not(lo)
                        s = _dot_nt(jnp.where(hm, qg, jnp.zeros_like(qg)), kcat)
                        s = jnp.where(valid, s, NEG)
                        m = jnp.max(s, axis=1, keepdims=True)
                        p = jnp.exp(s - m)
                        den = jnp.sum(p, axis=1, keepdims=True)
                        o = _dot(p.astype(BF16), vcat) / den
                        pair = o if hh == 0 else jnp.where(lo, pair, o)
                        lse_tiles[e] = jnp.where(lane == h, m + jnp.log(den), lse_tiles[e])
                    pairs.append(pair)
                if first:
                    store_acc(acc_ref, dil, r2, g, pairs)
                else:
                    own.append(pairs)
            if not first:
                mine = (lane >= j * heads_per_step) & (lane < (j + 1) * heads_per_step)
                theirs = [_load_streams_f32(ref, dil, r2, everything) for ref in lse_ins]
                w_theirs, w_own = [[] for _ in others], []
                for e in range(n_str):
                    parts = [t[e] for t in theirs] + [lse_tiles[e]]
                    mx = parts[0]
                    for part in parts[1:]:
                        mx = jnp.maximum(mx, part)
                    total = mx + jnp.log(sum(jnp.exp(part - mx) for part in parts))
                    for i, t in enumerate(theirs):
                        w_theirs[i].append(jnp.exp(t[e] - total))
                    w_own.append(jnp.exp(lse_tiles[e] - total))
                    lse_tiles[e] = jnp.where(mine, total, 0.0)
                for g in range(cw // LANES):
                    h0 = j * heads_per_step + 2 * g
                    spread = lambda w: jnp.where(lo, _head_col(w, lane, h0), _head_col(w, lane, h0 + 1))
                    olds = [_load_streams_f32(ref, dil, r2, g) for ref in acc_ins]
                    store_acc(acc_ref, dil, r2, g, [
                        sum(olds[i][e] * spread(w_theirs[i][e]) for i in range(len(others)))
                        + own[g][e] * spread(w_own[e]) for e in range(n_str)])
            if ncb == 1:
                _store_streams_f32(lse_ref, dil, r2, everything, lse_tiles)
            else:
                @pl.when(j == 0)
                def _():
                    _store_streams_f32(lse_ref, dil, r2, everything, lse_tiles)

                @pl.when(j > 0)
                def _():
                    before = _load_streams_f32(lse_ref, dil, r2, everything)
                    _store_streams_f32(lse_ref, dil, r2, everything, [a + b for a, b in zip(before, lse_tiles)])

        for rep in range(reps):
            stream_pair(rep if dil == 1 else pl.program_id(2) * reps + rep)

    ins = [q, k, k, v, v]
    in_specs = [cur, cur, prv, cur, prv]
    for acc, lse in others:
        ins += [acc, lse]
        in_specs += [full, heads]
    return pl.pallas_call(
        body, name=f"attn_fwd_d{dil}", grid=(nb, ncb, _unit_steps(dil, reps)),
        in_specs=in_specs, out_specs=[cur if last else full, heads],
        out_shape=[_packed(T, ATTN_W) if last else jax.ShapeDtypeStruct((N_GROUPS, T, LANES), F32),
                   jax.ShapeDtypeStruct((T, LANES), F32)],
        compiler_params=_params(3),
    )(*ins)


def _mix_fwd(pool, attn, x, w_out, g2, g3, tm):
    T = x.shape[0]

    def body(p_ref, a_ref, x_ref, w_ref, g2_ref, g3_ref, cat_ref, mix_ref, x2_ref, h2_ref):
        p = p_ref[...]
        a = _load_packed(a_ref)
        cat_ref[...] = jnp.concatenate([p, a], axis=1)
        mix = _dot(p, w_ref[:POOL_W, :]) + _dot(a, w_ref[POOL_W:, :])
        mix_ref[...] = mix
        x2 = x_ref[...] + _rms_fwd(mix, g2_ref[...])
        x2_ref[...] = x2
        h2_ref[...] = _rms_fwd(x2, g3_ref[...]).astype(BF16)

    return pl.pallas_call(
        body, name="mix_fwd", grid=(T // tm,),
        in_specs=[_tok(tm, POOL_W), _tok_packed(tm, ATTN_W), _tok(tm, D_MODEL), _res((D_MODEL, D_MODEL)),
                  _res((1, D_MODEL)), _res((1, D_MODEL))],
        out_specs=[_tok(tm, D_MODEL)] * 4,
        out_shape=[jax.ShapeDtypeStruct((T, D_MODEL), BF16), jax.ShapeDtypeStruct((T, D_MODEL), F32),
                   jax.ShapeDtypeStruct((T, D_MODEL), F32), jax.ShapeDtypeStruct((T, D_MODEL), BF16)],
        compiler_params=_params(1),
    )(pool, attn, x, w_out, g2, g3)


def _ffn_up(h2, wg_t, wu_t, tm):
    T = h2.shape[0]

    def body(h_ref, wg_ref, wu_ref, dg_ref, du_ref, a_ref):
        h = h_ref[...]
        gate = _dot_nt(h, wg_ref[...])
        up = _dot_nt(h, wu_ref[...])
        sg = 1.0 / (1.0 + jnp.exp(-gate))
        silu = gate * sg
        a_ref[...] = (silu * up).astype(BF16)
        dg_ref[...] = (up * (sg * (1.0 + gate * (1.0 - sg)))).astype(BF16)
        du_ref[...] = silu.astype(BF16)

    return pl.pallas_call(
        body, name="ffn_up", grid=(T // tm,),
        in_specs=[_tok(tm, D_MODEL), _res((D_FF, D_MODEL)), _res((D_FF, D_MODEL))],
        out_specs=[_tok(tm, D_FF)] * 3,
        out_shape=[jax.ShapeDtypeStruct((T, D_FF), BF16)] * 3,
        compiler_params=_params(1),
    )(h2, wg_t, wu_t)


def _ffn_down_loss(act, w_down, x2, g4, tgt, tm):
    T = act.shape[0]

    def body(a_ref, w_ref, x2_ref, g_ref, t_ref, df_ref, dy_ref, dg_ref, loss_ref):
        i = pl.program_id(0)

        @pl.when(i == 0)
        def _():
            dg_ref[...] = jnp.zeros_like(dg_ref)
            loss_ref[...] = jnp.zeros_like(loss_ref)

        f = _dot(a_ref[...], w_ref[...])
        g = g_ref[...]
        err = x2_ref[...] + _rms_fwd(f, g) - t_ref[...]
        loss_ref[...] += 0.5 * jnp.sum(jnp.mean(err * err, axis=-1, keepdims=True), axis=0, keepdims=True)
        dy = err * (1.0 / D_MODEL)
        dy_ref[...] = dy
        df, dg = _rms_bwd(f, g, dy)
        dg_ref[...] += dg
        df_ref[...] = df.astype(BF16)

    return pl.pallas_call(
        body, name="ffn_down_loss", grid=(T // tm,),
        in_specs=[_tok(tm, D_FF), _res((D_FF, D_MODEL)), _tok(tm, D_MODEL), _res((1, D_MODEL)), _tok(tm, D_MODEL)],
        out_specs=[_tok(tm, D_MODEL), _tok(tm, D_MODEL), _acc((1, D_MODEL)), _acc((1, 1))],
        out_shape=[jax.ShapeDtypeStruct((T, D_MODEL), BF16), jax.ShapeDtypeStruct((T, D_MODEL), F32),
                   jax.ShapeDtypeStruct((1, D_MODEL), F32), jax.ShapeDtypeStruct((1, 1), F32)],
        compiler_params=_params(1),
    )(act, w_down, x2, g4, tgt)


def _ffn_act_bwd(df, w_down, act_dgate, act_dup, tm):
    T = df.shape[0]

    def body(df_ref, w_ref, ag_ref, au_ref, dg_ref, du_ref):
        dact = _dot_nt(df_ref[...], w_ref[...])
        dg_ref[...] = (dact * ag_ref[...].astype(F32)).astype(BF16)
        du_ref[...] = (dact * au_ref[...].astype(F32)).astype(BF16)

    return pl.pallas_call(
        body, name="ffn_act_bwd", grid=(T // tm,),
        in_specs=[_tok(tm, D_MODEL), _res((D_FF, D_MODEL)), _tok(tm, D_FF), _tok(tm, D_FF)],
        out_specs=[_tok(tm, D_FF)] * 2,
        out_shape=[jax.ShapeDtypeStruct((T, D_FF), BF16)] * 2,
        compiler_params=_params(1),
    )(df, w_down, act_dgate, act_dup)


def _ffn_in_bwd(dgate, dup, wg_t, wu_t, x2, mix, dy, g3, g2, tm):
    T = x2.shape[0]

    def body(dg_ref, du_ref, wg_ref, wu_ref, x2_ref, mix_ref, dy_ref, g3_ref, g2_ref,
             dx2_ref, dmix_ref, dg3_ref, dg2_ref):
        @pl.when(pl.program_id(0) == 0)
        def _():
            dg3_ref[...] = jnp.zeros_like(dg3_ref)
            dg2_ref[...] = jnp.zeros_like(dg2_ref)

        dh2 = _dot(dg_ref[...], wg_ref[...]) + _dot(du_ref[...], wu_ref[...])
        dn, dg3 = _rms_bwd(x2_ref[...], g3_ref[...], dh2)
        dx2 = dy_ref[...] + dn
        dx2_ref[...] = dx2
        dg3_ref[...] += dg3
        dmix, dg2 = _rms_bwd(mix_ref[...], g2_ref[...], dx2)
        dg2_ref[...] += dg2
        dmix_ref[...] = dmix.astype(BF16)

    return pl.pallas_call(
        body, name="ffn_in_bwd", grid=(T // tm,),
        in_specs=[_tok(tm, D_FF), _tok(tm, D_FF), _res((D_FF, D_MODEL)), _res((D_FF, D_MODEL)),
                  _tok(tm, D_MODEL), _tok(tm, D_MODEL), _tok(tm, D_MODEL), _res((1, D_MODEL)), _res((1, D_MODEL))],
        out_specs=[_tok(tm, D_MODEL), _tok(tm, D_MODEL), _acc((1, D_MODEL)), _acc((1, D_MODEL))],
        out_shape=[jax.ShapeDtypeStruct((T, D_MODEL), F32), jax.ShapeDtypeStruct((T, D_MODEL), BF16),
                   jax.ShapeDtypeStruct((1, D_MODEL), F32), jax.ShapeDtypeStruct((1, D_MODEL), F32)],
        compiler_params=_params(1),
    )(dgate, dup, wg_t, wu_t, x2, mix, dy, g3, g2)


def _mix_bwd(dmix, w_out, tm):
    T = dmix.shape[0]

    def body(d_ref, w_ref, dp_ref, da_ref):
        dcat = _dot_nt(d_ref[...], w_ref[...])
        dp_ref[...] = dcat[:, :POOL_W].astype(BF16)
        _store_packed(da_ref, dcat[:, POOL_W:])

    return pl.pallas_call(
        body, name="mix_bwd", grid=(T // tm,),
        in_specs=[_tok(tm, D_MODEL), _res((D_MODEL, D_MODEL))],
        out_specs=[_tok(tm, POOL_W), _tok_packed(tm, ATTN_W)],
        out_shape=[jax.ShapeDtypeStruct((T, POOL_W), BF16), _packed(T, ATTN_W)],
        compiler_params=_params(1),
    )(dmix, w_out)


def _attn_bwd(q, k, v, dout, out, lse, dil):
    T = 2 * q.shape[1]
    reps = _units_per_step(dil, True)
    nb = T // _chunk_tokens(dil, reps)
    cw, cur, prv, prv_out, heads, _ = _attn_specs(dil, nb, reps)
    ncb = ATTN_W // cw
    heads_per_step = cw // HEAD_DIM
    n_str = min(dil, 2)

    def body(q_ref, kc_ref, kp_ref, vc_ref, vp_ref, do_ref, o_ref, lse_ref,
             dq_ref, dkc_ref, dkp_ref, dvc_ref, dvp_ref):
        j = pl.program_id(1)
        lane = lax.broadcasted_iota(jnp.int32, (BLK, LANES), 1)
        lo = lane < HEAD_DIM

        def unit_grads(r2, g):
            valid = _attn_mask(True if dil == 1 and r2 > 0 else pl.program_id(0) > 0)
            valid2 = jnp.concatenate([valid, valid], axis=0)
            lse_tiles = _load_streams_f32(lse_ref, dil, r2, None)
            qs, kcs, vcs, dos, os_ = [_load_streams(r, dil, r2, g) for r in (q_ref, kc_ref, vc_ref, do_ref, o_ref)]
            kps, vps = [_load_prev_streams(p, c, dil, reps, r2, g) for p, c in ((kp_ref, kc_ref), (vp_ref, vc_ref))]
            dqs, dks, dvs = [], [], []
            for e in range(n_str):
                qg = qs[e] * 0.125
                dog = dos[e]
                kcat = jnp.concatenate([kps[e], kcs[e]], axis=0)
                vcat = jnp.concatenate([vps[e], vcs[e]], axis=0)
                prod = dog.astype(F32) * os_[e].astype(F32)
                h0 = j * heads_per_step + 2 * g
                q2 = _stack_heads(qg, lo)
                do2 = _stack_heads(dog, lo)
                lse2 = jnp.concatenate([_head_col(lse_tiles[e], lane, h0), _head_col(lse_tiles[e], lane, h0 + 1)], axis=0)
                dsum2 = jnp.concatenate([jnp.sum(jnp.where(lo, prod, 0.0), axis=1, keepdims=True),
                                         jnp.sum(jnp.where(lo, 0.0, prod), axis=1, keepdims=True)], axis=0)
                p = jnp.exp(jnp.where(valid2, _dot_nt(q2, kcat), NEG) - lse2)
                ds = (p * (_dot_nt(do2, vcat) - dsum2)).astype(BF16)
                dvs.append(_dot_tn(p.astype(BF16), do2))
                dks.append(_dot_tn(ds, q2))
                dq2 = _dot(ds, kcat) * 0.125
                dqs.append(jnp.where(lo, dq2[:BLK], dq2[BLK:]))
            return dqs, dks, dvs

        for g in range(cw // LANES):
            if dil > 1:
                for rep in range(reps):
                    r2 = pl.program_id(2) * reps + rep
                    dqs, dks, dvs = unit_grads(r2, g)
                    _store_streams(dq_ref, dil, r2, g, dqs)
                    _store_streams(dkp_ref, dil, r2, g, [t[:BLK] for t in dks])
                    _store_streams(dkc_ref, dil, r2, g, [t[BLK:] for t in dks])
                    _store_streams(dvp_ref, dil, r2, g, [t[:BLK] for t in dvs])
                    _store_streams(dvc_ref, dil, r2, g, [t[BLK:] for t in dvs])
            else:
                blocks = [unit_grads(b, g) for b in range(reps)]
                for b, (dqs, dks, dvs) in enumerate(blocks):
                    _store_streams(dq_ref, 1, b, g, dqs)
                    for cur_ref, prev_ref, which in ((dkc_ref, dkp_ref, 1), (dvc_ref, dvp_ref, 2)):
                        own = blocks[b][which][0][BLK:]
                        if b + 1 < reps:
                            own = own + blocks[b + 1][which][0][:BLK]
                        _store_streams(cur_ref, 1, b, g, [own])
                        edge = blocks[0][which][0][:BLK] if b == reps - 1 else jnp.zeros((BLK, LANES), F32)
                        _store_streams(prev_ref, 1, b, g, [edge])

    return pl.pallas_call(
        body, name=f"attn_bwd_d{dil}", grid=(nb, ncb, _unit_steps(dil, reps)),
        in_specs=[cur, cur, prv, cur, prv, cur, cur, heads],
        out_specs=[cur, cur, prv_out, cur, prv_out],
        out_shape=[_packed(T, ATTN_W)] * 5,
        compiler_params=_params(3),
    )(q, k, k, v, v, dout, out, lse)


def _pool_bwd(u, dy, wbd, scale, tm):
    T = u.shape[0]
    nt = T // tm
    hb = tm // POOL_HALO

    def body(u_ref, prev_ref, dy_ref, next_ref, w_ref, sc_ref, du_ref, dw_ref, dsc_ref):
        i = pl.program_id(0)

        @pl.when(i == 0)
        def _():
            dw_ref[...] = jnp.zeros_like(dw_ref)
            dsc_ref[...] = jnp.zeros_like(dsc_ref)

        w = w_ref[...]
        sc = sc_ref[...]
        d = _pool_delta(u_ref[...], prev_ref[...], i, tm).astype(BF16)
        dyc = dy_ref[...].astype(F32)
        dsc_ref[...] += jnp.sum(dyc * _dot(d, w), axis=0, keepdims=True)
        nxt = jnp.where(i < nt - 1, next_ref[...].astype(F32), 0.0)
        dypre = (jnp.concatenate([dyc, nxt], axis=0) * sc).astype(BF16)
        dw_ref[...] += _dot_tn(d, dypre[:tm])
        dd = _dot_nt(dypre, w)
        n = tm + POOL_HALO
        lane = lax.broadcasted_iota(jnp.int32, (n, POOL_W), 1)
        row = lax.broadcasted_iota(jnp.int32, (n, POOL_W), 0) + i * tm
        gx = dd / jnp.minimum(row + 1, _pool_window(lane)).astype(F32)
        a2 = gx + pltpu.roll(gx, n - 1, 0)
        a4 = a2 + pltpu.roll(a2, n - 2, 0)
        a8 = a4 + pltpu.roll(a4, n - 4, 0)
        a16 = a8 + pltpu.roll(a8, n - 8, 0)
        fs = _pool_select(lane[:tm], a2[:tm], a4[:tm], a8[:tm], a16[:tm])
        du_ref[...] = (fs - dd[:tm]).astype(BF16)

    return pl.pallas_call(
        body, name="pool_bwd", grid=(nt,),
        in_specs=[_tok(tm, POOL_W), pl.BlockSpec((POOL_HALO, POOL_W), lambda i: (jnp.maximum(i * hb - 1, 0), 0)),
                  _tok(tm, POOL_W), pl.BlockSpec((POOL_HALO, POOL_W), lambda i: (jnp.minimum((i + 1) * hb, nt * hb - 1), 0)),
                  _res((POOL_W, POOL_W)), _res((1, POOL_W))],
        out_specs=[_tok(tm, POOL_W), _acc((POOL_W, POOL_W)), _acc((1, POOL_W))],
        out_shape=[jax.ShapeDtypeStruct((T, POOL_W), BF16), jax.ShapeDtypeStruct((POOL_W, POOL_W), F32),
                   jax.ShapeDtypeStruct((1, POOL_W), F32)],
        compiler_params=_params(1),
    )(u, u, dy, dy, wbd, scale)


def _dproj_combine(du, dqs, dkcs, dkps, dvcs, dvps, cos, sin, tm):
    T = du.shape[0]
    n_cfg = len(dqs)

    def body(*refs):
        du_ref = refs[0]
        groups = [refs[1 + j * n_cfg:1 + (j + 1) * n_cfg] for j in range(5)]
        c_ref, s_ref, out_ref = refs[1 + 5 * n_cfg:]
        tot = lambda rs: sum(_load_packed(r).astype(F32) for r in rs)
        c = c_ref[...]
        s = s_ref[...]
        dq = _rope(tot(groups[0]), c, s, -1.0)
        dk = _rope(tot(groups[1]) + tot(groups[2]), c, s, -1.0)
        dv = tot(groups[3]) + tot(groups[4])
        out_ref[...] = jnp.concatenate([du_ref[...], dq.astype(BF16), dk.astype(BF16), dv.astype(BF16)], axis=1)

    return pl.pallas_call(
        body, name="dproj_combine", grid=(T // tm,),
        in_specs=[_tok(tm, POOL_W)] + [_tok_packed(tm, ATTN_W)] * (5 * n_cfg) + [_tok(tm, LANES)] * 2,
        out_specs=_tok(tm, IN_W),
        out_shape=jax.ShapeDtypeStruct((T, IN_W), BF16),
        compiler_params=_params(1),
    )(du, *dqs, *dkcs, *dkps, *dvcs, *dvps, cos, sin)


def _proj_bwd(dproj, w_in_t, x, dx2, g1, tm):
    T = x.shape[0]

    def body(d_ref, w_ref, x_ref, r_ref, g_ref, dx_ref, dg_ref):
        @pl.when(pl.program_id(0) == 0)
        def _():
            dg_ref[...] = jnp.zeros_like(dg_ref)

        dn, dg = _rms_bwd(x_ref[...], g_ref[...], _dot(d_ref[...], w_ref[...]))
        dg_ref[...] += dg
        dx_ref[...] = r_ref[...] + dn

    return pl.pallas_call(
        body, name="proj_bwd", grid=(T // tm,),
        in_specs=[_tok(tm, IN_W), _res((IN_W, D_MODEL)), _tok(tm, D_MODEL), _tok(tm, D_MODEL), _res((1, D_MODEL))],
        out_specs=[_tok(tm, D_MODEL), _acc((1, D_MODEL))],
        out_shape=[jax.ShapeDtypeStruct((T, D_MODEL), F32), jax.ShapeDtypeStruct((1, D_MODEL), F32)],
        compiler_params=_params(1),
    )(dproj, w_in_t, x, dx2, g1)


def _wgrad(a, b, name, tile_m, tk):
    T, M = a.shape
    N = b.shape[1]
    nk = T // tk

    def body(a_ref, b_ref, o_ref, acc_ref):
        kk = pl.program_id(1)

        @pl.when(kk == 0)
        def _():
            acc_ref[...] = jnp.zeros_like(acc_ref)

        acc_ref[...] += _dot_tn(a_ref[...], b_ref[...])

        @pl.when(kk == nk - 1)
        def _():
            o_ref[...] = acc_ref[...].astype(BF16)

    return pl.pallas_call(
        body, name=name, grid=(M // tile_m, nk),
        in_specs=[pl.BlockSpec((tk, tile_m), lambda j, kk: (kk, j)), pl.BlockSpec((tk, N), lambda j, kk: (kk, 0))],
        out_specs=pl.BlockSpec((tile_m, N), lambda j, kk: (j, 0)),
        out_shape=jax.ShapeDtypeStruct((M, N), BF16),
        scratch_shapes=[pltpu.VMEM((tile_m, N), F32)],
        compiler_params=_params(2),
    )(a, b)


def _exchange(arrs, scatter, name):
    n = len(arrs)
    out_shapes = [jax.ShapeDtypeStruct((N_DEV,) + (a.shape[1:] if sc else a.shape), a.dtype)
                  for a, sc in zip(arrs, scatter)]

    def body(*refs):
        ins, outs = refs[:n], refs[n:2 * n]
        send_sems, recv_sems, loc_sems = refs[2 * n:]
        x, y, c = lax.axis_index("x"), lax.axis_index("y"), lax.axis_index("c")
        me = 4 * x + 2 * y + c
        local, sends, recvs = [], [], []
        for i in range(n):
            own = ins[i].at[me] if scatter[i] else ins[i]
            loc = pltpu.make_async_copy(own, outs[i].at[me], loc_sems.at[i])
            loc.start()
            local.append(loc)
            for kbits in range(1, N_DEV):
                px = 1 - x if kbits & 4 else x
                py = 1 - y if kbits & 2 else y
                pc = 1 - c if kbits & 1 else c
                pid = 4 * px + 2 * py + pc
                src = ins[i].at[pid] if scatter[i] else ins[i]
                cp = pltpu.make_async_remote_copy(
                    src_ref=src, dst_ref=outs[i].at[me],
                    send_sem=send_sems.at[i, kbits - 1], recv_sem=recv_sems.at[i, kbits - 1],
                    device_id=(px, py, pc), device_id_type=pl.DeviceIdType.MESH)
                cp.start()
                sends.append(cp)
                recvs.append(pltpu.make_async_remote_copy(
                    src_ref=src, dst_ref=outs[i].at[pid],
                    send_sem=send_sems.at[i, kbits - 1], recv_sem=recv_sems.at[i, kbits - 1],
                    device_id=(px, py, pc), device_id_type=pl.DeviceIdType.MESH))
        for cp in recvs:
            cp.wait_recv()
        for cp in sends:
            cp.wait_send()
        for cp in local:
            cp.wait()

    hbm = pl.BlockSpec(memory_space=pl.ANY)
    return pl.pallas_call(
        body, name=name, in_specs=[hbm] * n, out_specs=[hbm] * n, out_shape=out_shapes,
        scratch_shapes=[pltpu.SemaphoreType.DMA((n, N_DEV - 1)), pltpu.SemaphoreType.DMA((n, N_DEV - 1)),
                        pltpu.SemaphoreType.DMA((n,))],
    )(*arrs)


def _gather_two_level(arr, name):
    def body(x_ref, out_ref, send_sems, recv_sems, local_sem):
        x, y, c = lax.axis_index("x"), lax.axis_index("y"), lax.axis_index("c")
        me, sibling = (x, y, c), (x, y, 1 - c)
        chips = [(1 - x, y), (x, 1 - y), (1 - x, 1 - y)]
        slot = lambda px, py, pc: out_ref.at[4 * px + 2 * py + pc]

        def copy(k, block, to, src=None):
            return pltpu.make_async_remote_copy(
                src_ref=slot(*block) if src is None else src, dst_ref=slot(*block),
                send_sem=send_sems.at[k], recv_sem=recv_sems.at[k],
                device_id=to, device_id_type=pl.DeviceIdType.MESH)

        mine = pltpu.make_async_copy(x_ref, slot(*me), local_sem)
        mine.start()
        first = [copy(0, me, sibling, src=x_ref)]
        first += [copy(1 + i, me, (*chip, c), src=x_ref) for i, chip in enumerate(chips)]
        for cp in first:
            cp.start()
        passed = [copy(4 + i, (*chip, c), sibling) for i, chip in enumerate(chips)]
        for i, chip in enumerate(chips):
            copy(1 + i, (*chip, c), me).wait_recv()
            passed[i].start()
        copy(0, sibling, me).wait_recv()
        for i, chip in enumerate(chips):
            copy(4 + i, (*chip, 1 - c), me).wait_recv()
        for cp in first + passed:
            cp.wait_send()
        mine.wait()

    hbm = pl.BlockSpec(memory_space=pl.ANY)
    return pl.pallas_call(
        body, name=name, in_specs=[hbm], out_specs=hbm,
        out_shape=jax.ShapeDtypeStruct((N_DEV,) + arr.shape, arr.dtype),
        scratch_shapes=[pltpu.SemaphoreType.DMA((N_DEV - 1,)), pltpu.SemaphoreType.DMA((N_DEV - 1,)),
                        pltpu.SemaphoreType.DMA],
    )(arr)


def _peers(x, y, c):
    for kbits in range(1, N_DEV):
        px = 1 - x if kbits & 4 else x
        py = 1 - y if kbits & 2 else y
        pc = 1 - c if kbits & 1 else c
        yield kbits - 1, (px, py, pc), 4 * px + 2 * py + pc


def _peer_copies(ins, lands, scatter, send_sems, recv_sems, incoming):
    x, y, c = lax.axis_index("x"), lax.axis_index("y"), lax.axis_index("c")
    me = 4 * x + 2 * y + c
    copies = []
    for i in range(len(ins)):
        for k, peer, pid in _peers(x, y, c):
            slot = i * (N_DEV - 1) + k
            copies.append(pltpu.make_async_remote_copy(
                src_ref=ins[i].at[pid] if scatter[i] else ins[i], dst_ref=lands[i].at[pid if incoming else me],
                send_sem=send_sems.at[slot], recv_sem=recv_sems.at[slot],
                device_id=peer, device_id_type=pl.DeviceIdType.MESH))
    return copies


_HBM = pl.BlockSpec(memory_space=pltpu.HBM)
_SEM = pl.BlockSpec(memory_space=pltpu.SEMAPHORE)
_DATAFLOW = pltpu.SideEffectType.DATAFLOW_SIDE_EFFECTING


def _exchange_start(arrs, scatter, after, name):
    n = len(arrs)
    lands = [lax.empty((N_DEV,) + (a.shape[1:] if sc else a.shape), a.dtype) for a, sc in zip(arrs, scatter)]

    def body(*refs):
        ins, lz = refs[:n], refs[n:2 * n]
        send_sems, recv_sems = refs[2 * n + 1:2 * n + 3]
        token = refs[-1]
        for cp in _peer_copies(ins, lz, scatter, send_sems, recv_sems, False):
            cp.start()
        token[...] = jnp.zeros_like(token)

    sem_shape = pltpu.SemaphoreType.DMA((n * (N_DEV - 1),))
    outs = pl.pallas_call(
        body, name=name,
        out_shape=(sem_shape, sem_shape, *[pltpu.HBM(a.shape, a.dtype) for a in arrs + lands],
                   jax.ShapeDtypeStruct((8, LANES), F32)),
        in_specs=[_HBM] * (2 * n) + [pl.BlockSpec(memory_space=pl.ANY)],
        out_specs=(_SEM, _SEM, *[_HBM] * (2 * n), pl.BlockSpec(memory_space=pltpu.VMEM)),
        input_output_aliases={i: 2 + i for i in range(2 * n)},
        compiler_params=pltpu.CompilerParams(has_side_effects=_DATAFLOW),
    )(*[pltpu.with_memory_space_constraint(a, pltpu.HBM) for a in arrs + lands], after)
    return outs[0], outs[1], list(outs[2:2 + n]), list(outs[2 + n:2 + 2 * n]), outs[-1]


def _exchange_wait(handle, scatter, after, name):
    send_sems, recv_sems, srcs, lands, _ = handle
    n = len(srcs)

    def body(*refs):
        ins, lz = refs[:n], refs[n:2 * n]
        for cp in _peer_copies(ins, lz, scatter, refs[2 * n], refs[2 * n + 1], False):
            cp.wait_send()
        for cp in _peer_copies(ins, lz, scatter, refs[2 * n], refs[2 * n + 1], True):
            cp.wait_recv()

    outs = pl.pallas_call(
        body, name=name,
        out_shape=[pltpu.HBM(a.shape, a.dtype) for a in srcs + lands],
        in_specs=[_HBM] * (2 * n) + [_SEM, _SEM, pl.BlockSpec(memory_space=pl.ANY)],
        out_specs=[_HBM] * (2 * n),
        input_output_aliases={i: i for i in range(2 * n)},
        compiler_params=pltpu.CompilerParams(has_side_effects=_DATAFLOW),
    )(*srcs, *lands, send_sems, recv_sems, after)
    return list(outs[:n]), list(outs[n:])


def _fill_own(lands, srcs, scatter):
    me = 4 * lax.axis_index("x") + 2 * lax.axis_index("y") + lax.axis_index("c")
    own = [lax.dynamic_index_in_dim(s, me, 0, keepdims=False) if sc else s for s, sc in zip(srcs, scatter)]
    return [lax.dynamic_update_index_in_dim(land, o, me, 0) for land, o in zip(lands, own)]


def _slot_sum(parts, name, tr):
    _, R, C = parts.shape

    def body(p_ref, o_ref):
        acc = p_ref[0].astype(F32)
        for s in range(1, N_DEV):
            acc = acc + p_ref[s].astype(F32)
        o_ref[...] = acc

    return pl.pallas_call(
        body, name=name, grid=(R // tr,),
        in_specs=[pl.BlockSpec((N_DEV, tr, C), lambda i: (0, i, 0))],
        out_specs=pl.BlockSpec((tr, C), lambda i: (i, 0)),
        out_shape=jax.ShapeDtypeStruct((R, C), F32),
        compiler_params=_params(1),
    )(parts)


def _adamw(w, g, m, v, name):
    def body(w_ref, g_ref, m_ref, v_ref, d_ref, nm_ref, nv_ref):
        g = g_ref[...]
        nm = ADAM_B1 * m_ref[...] + (1.0 - ADAM_B1) * g
        nv = ADAM_B2 * v_ref[...] + (1.0 - ADAM_B2) * jnp.square(g)
        m_hat = nm / (1.0 - ADAM_B1 ** ADAM_STEP)
        v_hat = nv / (1.0 - ADAM_B2 ** ADAM_STEP)
        d_ref[...] = -ADAM_LR * (m_hat / (jnp.sqrt(v_hat) + ADAM_EPS) + ADAM_WD * w_ref[...])
        nm_ref[...] = nm
        nv_ref[...] = nv

    return pl.pallas_call(
        body, name=name, out_shape=[jax.ShapeDtypeStruct(w.shape, F32)] * 3,
        compiler_params=pltpu.CompilerParams(vmem_limit_bytes=VMEM_LIMIT),
    )(w, g, m, v)


def _rope_tables(T):
    half = HEAD_DIM // 2
    freqs = ROPE_THETA ** (-jnp.arange(half, dtype=F32) * (2.0 / HEAD_DIM))
    ang = jnp.arange(T).astype(F32)[:, None] * jnp.tile(freqs, LANES // half)[None, :]
    sign = jnp.tile(jnp.concatenate([-jnp.ones((half,), F32), jnp.ones((half,), F32)]), LANES // HEAD_DIM)
    return jnp.cos(ang), jnp.sin(ang) * sign[None, :]


def _block_diag(w_pool):
    wbd = jnp.zeros((POOL_W, POOL_W), F32)
    g = POOL_W // len(POOL_WINDOWS)
    for i in range(len(POOL_WINDOWS)):
        wbd = wbd.at[i * g:(i + 1) * g, i * g:(i + 1) * g].set(w_pool[i])
    return wbd


def _pack_small(g1, w_pool, pool_scale, g2, g3, g4, extra):
    pad = lambda a: jnp.pad(a.reshape(1, -1), ((0, 0), (0, D_MODEL - a.size)))
    rows = [g1.reshape(1, -1), g2.reshape(1, -1), g3.reshape(1, -1), g4.reshape(1, -1),
            w_pool.reshape(-1, D_MODEL), pad(pool_scale), pad(extra)]
    buf = jnp.concatenate(rows, axis=0)
    return jnp.pad(buf, ((0, SMALL_ROWS - buf.shape[0]), (0, 0)))


def _unpack_small(buf):
    n_pool = len(POOL_WINDOWS) * (POOL_W // len(POOL_WINDOWS)) ** 2 // D_MODEL
    g = POOL_W // len(POOL_WINDOWS)
    return (buf[0:1], buf[4:4 + n_pool].reshape(1, len(POOL_WINDOWS), g, g), buf[4 + n_pool:5 + n_pool, :POOL_W],
            buf[1:2], buf[2:3], buf[3:4], buf[5 + n_pool])


class _LocalStep:
    def __init__(self, x, tgt, g1, w_pool, pool_scale, g2, g3, g4):
        self.x, self.tgt, self.pool_scale = x, tgt, pool_scale
        self.g1, self.g2, self.g3, self.g4 = g1, g2, g3, g4
        self.cos, self.sin = _rope_tables(x.shape[0])
        self.wbd = _block_diag(w_pool).astype(BF16)

    def mixer_fwd(self, w_in_t, token):
        self.w_in_t = w_in_t
        self.h1, self.u, self.q, self.k, self.v = _proj_fwd(
            self.x, self.g1 + token[0, 0], w_in_t, self.cos, self.sin, 1024)
        self.pool = _pool_fwd(self.u, self.wbd, self.pool_scale, 1024)
        alone = [_attn_fwd(self.q, self.k, self.v, dil, [], False) for dil in DILATIONS[:-1]]
        self.attn, self.lse = _attn_fwd(self.q, self.k, self.v, DILATIONS[-1], alone, True)
        return self.attn

    def ffn_fwd_bwd(self, w_out, wg_t, wu_t, w_down):
        self.w_out, self.wg_t, self.wu_t = w_out, wg_t, wu_t
        self.cat, self.mix, self.x2, h2 = _mix_fwd(self.pool, self.attn, self.x, w_out, self.g2, self.g3, 1024)
        act_dgate, act_dup, act = _ffn_up(h2, wg_t, wu_t, 256)
        df, self.dy, self.dg4, self.loss = _ffn_down_loss(act, w_down, self.x2, self.g4, self.tgt, 512)
        self.dgate, self.dup = _ffn_act_bwd(df, w_down, act_dgate, act_dup, 512)
        return (_wgrad(self.dgate, h2, "wgrad_gate", D_FF // 2, 1024), _wgrad(self.dup, h2, "wgrad_up", D_FF // 2, 1024),
                _wgrad(act, df, "wgrad_down", D_FF // 2, 1024))

    def mixer_bwd(self, token):
        self.dx2, dmix, self.dg3, self.dg2 = _ffn_in_bwd(
            self.dgate, self.dup, self.wg_t, self.wu_t, self.x2, self.mix, self.dy, self.g3 + token[0, 0], self.g2, 512)
        dpool, dattn = _mix_bwd(dmix, self.w_out, 1024)
        parts = [_attn_bwd(self.q, self.k, self.v, dattn, self.attn, self.lse, dil) for dil in DILATIONS]
        du, dwbd, self.dscale = _pool_bwd(self.u, dpool, self.wbd, self.pool_scale, 1024)
        g = POOL_W // len(POOL_WINDOWS)
        self.dw_pool = jnp.stack([dwbd[i * g:(i + 1) * g, i * g:(i + 1) * g] for i in range(len(POOL_WINDOWS))])
        self.dproj = _dproj_combine(du, *[[p[j] for p in parts] for j in range(5)], self.cos, self.sin, 512)
        return _wgrad(self.dproj, self.h1, "wgrad_in", IN_W // 2, 1024), _wgrad(self.cat, dmix, "wgrad_out", D_MODEL, 1024)

    def input_bwd(self, token):
        grad_x, dg1 = _proj_bwd(self.dproj, self.w_in_t, self.x, self.dx2, self.g1 + token[0, 0], 1024)
        return self.loss, grad_x, (dg1, self.dw_pool, self.dscale, self.dg2, self.dg3, self.dg4)


def _local_step(x, tgt, g1, w_pool, pool_scale, g2, g3, g4, w_in_t, w_out, wg_t, wu_t, w_down):
    zero = jnp.zeros((8, LANES), F32)
    step = _LocalStep(x, tgt, g1, w_pool, pool_scale, g2, g3, g4)
    step.mixer_fwd(w_in_t, zero)
    dw_gate, dw_up, dw_down = step.ffn_fwd_bwd(w_out, wg_t, wu_t, w_down)
    dw_in, dw_out = step.mixer_bwd(zero)
    loss, grad_x, small = step.input_bwd(zero)
    return loss, grad_x, small, (dw_in, dw_out, dw_gate, dw_up, dw_down)


def kernel(x, ln_pre_mix, w_in, w_pool, pool_scale, w_out, ln_post_mix, ln_pre_ffn, w_gate, w_up, w_down, ln_post_ffn, loss_target, m_ln_pre_mix, m_w_in, m_w_pool, m_pool_scale, m_w_out, m_ln_post_mix, m_ln_pre_ffn, m_w_gate, m_w_up, m_w_down, m_ln_post_ffn, v_ln_pre_mix, v_w_in, v_w_pool, v_pool_scale, v_w_out, v_ln_post_mix, v_ln_pre_ffn, v_w_gate, v_w_up, v_w_down, v_ln_post_ffn):
    shards = [w_in[0].T.astype(BF16), w_out[0].astype(BF16), w_gate[0].T.astype(BF16),
              w_up[0].T.astype(BF16), w_down[0].astype(BF16)]
    flat = lambda a: a.reshape(-1, D_MODEL)
    blocks = lambda a: a.reshape(N_DEV, -1, D_MODEL)
    step = _LocalStep(x[0], loss_target[0], ln_pre_mix, w_pool[0], pool_scale, ln_post_mix, ln_pre_ffn, ln_post_ffn)

    w_in_t = flat(_gather_two_level(shards[0], "gather_w_in"))
    rest = _exchange_start(shards[1:], [False] * 4, w_in_t, "gather_rest_start")
    attn = step.mixer_fwd(w_in_t, rest[4])
    srcs, lands = _exchange_wait(rest, [False] * 4, attn, "gather_rest_wait")
    w_out_f, wg_t, wu_t, w_down_f = [flat(a) for a in _fill_own(lands, srcs, [False] * 4)]

    ffn = _exchange_start([blocks(a) for a in step.ffn_fwd_bwd(w_out_f, wg_t, wu_t, w_down_f)], [True] * 3,
                          step.dgate, "grads_ffn_start")
    mixer = _exchange_start([blocks(a) for a in step.mixer_bwd(ffn[4])], [True] * 2, step.dproj, "grads_mixer_start")
    loss, grad_x, small = step.input_bwd(mixer[4])
    got = []
    for handle, n_arr, nm in ((mixer, 2, "grads_mixer"), (ffn, 3, "grads_ffn")):
        srcs, lands = _exchange_wait(handle, [True] * n_arr, grad_x, nm + "_wait")
        got += _fill_own(lands, srcs, [True] * n_arr)
    sums = [_slot_sum(got[i], f"sum_grad_{i}", got[i].shape[1] // 2) for i in range(5)]

    small_buf = _pack_small(small[0], small[1], small[2], small[3], small[4], small[5], loss)
    small_sum = _slot_sum(_exchange([small_buf], [False], "gather_small")[0], "sum_small", SMALL_ROWS)

    g_in, g_out, g_gate, g_up, g_down = sums[0].T, sums[1], sums[2].T, sums[3].T, sums[4]
    upd = [_adamw(w[0], g, m[0], v[0], f"adamw_{nm}") for nm, w, g, m, v in (
        ("in", w_in, g_in, m_w_in, v_w_in), ("out", w_out, g_out, m_w_out, v_w_out),
        ("gate", w_gate, g_gate, m_w_gate, v_w_gate), ("up", w_up, g_up, m_w_up, v_w_up),
        ("down", w_down, g_down, m_w_down, v_w_down))]
    pack = lambda a, b, c, d, e, f: _pack_small(a, b[0], c, d, e, f, jnp.zeros((1,), F32))
    small_upd = _adamw(
        pack(ln_pre_mix, w_pool, pool_scale, ln_post_mix, ln_pre_ffn, ln_post_ffn), small_sum,
        pack(m_ln_pre_mix, m_w_pool, m_pool_scale, m_ln_post_mix, m_ln_pre_ffn, m_ln_post_ffn),
        pack(v_ln_pre_mix, v_w_pool, v_pool_scale, v_ln_post_mix, v_ln_pre_ffn, v_ln_post_ffn), "adamw_small")

    def tree(small6, big5):
        s1, spool, sscale, s2, s3, s4 = small6
        b_in, b_out, b_gate, b_up, b_down = [b[None] for b in big5]
        return [s1, b_in, spool, sscale, b_out, s2, s3, b_gate, b_up, b_down, s4]

    g_small = _unpack_small(small_sum)
    outs = [g_small[6][0], grad_x[None]]
    outs += tree(g_small[:6], [g_in, g_out, g_gate, g_up, g_down])
    for j in range(3):
        outs += tree(_unpack_small(small_upd[j])[:6], [u[j] for u in upd])
    return tuple(outs)
```

```python
import jax
import jax.numpy as jnp
from jax import lax
from jax.experimental import pallas as pl
from jax.experimental.pallas import tpu as pltpu

F32 = jnp.float32
BF16 = jnp.bfloat16

D_MODEL = 1024
POOL_W = 256
ATTN_W = 768
IN_W = 2560
D_FF = 2816
POOL_WINDOWS = (2, 4, 8, 16)
POOL_HALO = 16
DILATIONS = (1, 4, 16)
BLK = 128
LANES = 128
HEAD_DIM = 64
N_GROUPS = ATTN_W // LANES
ROPE_THETA = 10000.0
EPS = 1e-6
NEG = -1e30
N_DEV = 8
SMALL_ROWS = 24

ADAM_LR = 0.001
ADAM_B1 = 0.9
ADAM_B2 = 0.999
ADAM_EPS = 1e-08
ADAM_WD = 0.01
ADAM_STEP = 10

VMEM_LIMIT = 56 * 1024 * 1024


def _dot(a, b):
    return jnp.dot(a, b, preferred_element_type=F32)


def _dot_nt(a, b):
    return lax.dot_general(a, b, (((1,), (1,)), ((), ())), preferred_element_type=F32)


def _dot_tn(a, b):
    return lax.dot_general(a, b, (((0,), (0,)), ((), ())), preferred_element_type=F32)


def _params(n_grid):
    return pltpu.CompilerParams(dimension_semantics=("arbitrary",) * n_grid, vmem_limit_bytes=VMEM_LIMIT)


def _tok(tm, c):
    return pl.BlockSpec((tm, c), lambda i: (i, 0))


def _res(shape):
    return pl.BlockSpec(shape, lambda i: (0,) * len(shape), pipeline_mode=pl.Buffered(1))


def _acc(shape):
    return pl.BlockSpec(shape, lambda i: (0,) * len(shape))


def _rms_fwd(x, g):
    r = lax.rsqrt(jnp.mean(x * x, axis=-1, keepdims=True) + EPS)
    return x * r * g


def _rms_bwd(x, g, dy):
    r = lax.rsqrt(jnp.mean(x * x, axis=-1, keepdims=True) + EPS)
    xh = x * r
    gd = dy * g
    dx = r * (gd - xh * jnp.mean(gd * xh, axis=-1, keepdims=True))
    return dx, jnp.sum(dy * xh, axis=0, keepdims=True)


def _rope(x, c, s, sign):
    lane = lax.broadcasted_iota(jnp.int32, (x.shape[0], LANES), 1)
    first = (lane % HEAD_DIM) < (HEAD_DIM // 2)
    outs = []
    for g in range(x.shape[1] // LANES):
        xg = x[:, g * LANES:(g + 1) * LANES]
        rot = jnp.where(first, pltpu.roll(xg, LANES - HEAD_DIM // 2, 1), pltpu.roll(xg, HEAD_DIM // 2, 1))
        outs.append(xg * c + sign * (rot * s))
    return jnp.concatenate(outs, axis=1)


def _proj_fwd(x, g1, w_in_t, cos, sin, tm):
    T = x.shape[0]

    def body(x_ref, g_ref, w_ref, c_ref, s_ref, h_ref, u_ref, q_ref, k_ref, v_ref):
        h = _rms_fwd(x_ref[...], g_ref[...]).astype(BF16)
        h_ref[...] = h
        proj = _dot_nt(h, w_ref[...])
        c = c_ref[...]
        s = s_ref[...]
        u_ref[...] = proj[:, :POOL_W]
        _store_packed(q_ref, _rope(proj[:, POOL_W:POOL_W + ATTN_W], c, s, 1.0))
        _store_packed(k_ref, _rope(proj[:, POOL_W + ATTN_W:POOL_W + 2 * ATTN_W], c, s, 1.0))
        _store_packed(v_ref, proj[:, POOL_W + 2 * ATTN_W:])

    return pl.pallas_call(
        body, name="proj_fwd", grid=(T // tm,),
        in_specs=[_tok(tm, D_MODEL), _res((1, D_MODEL)), _res((IN_W, D_MODEL)), _tok(tm, LANES), _tok(tm, LANES)],
        out_specs=[_tok(tm, D_MODEL), _tok(tm, POOL_W)] + [_tok_packed(tm, ATTN_W)] * 3,
        out_shape=[jax.ShapeDtypeStruct((T, D_MODEL), BF16), jax.ShapeDtypeStruct((T, POOL_W), F32)]
        + [_packed(T, ATTN_W)] * 3,
        compiler_params=_params(1),
    )(x, g1, w_in_t, cos, sin)


def _pool_window(lane):
    return jnp.where(lane < 64, 2, jnp.where(lane < 128, 4, jnp.where(lane < 192, 8, 16)))


def _pool_select(lane, a2, a4, a8, a16):
    return jnp.where(lane < 64, a2, jnp.where(lane < 128, a4, jnp.where(lane < 192, a8, a16)))


def _pool_delta(cur, prev, i, tm):
    prev = jnp.where(i > 0, prev, 0.0)
    ext = jnp.concatenate([prev, cur], axis=0)
    s2 = ext + pltpu.roll(ext, 1, 0)
    s4 = s2 + pltpu.roll(s2, 2, 0)
    s8 = s4 + pltpu.roll(s4, 4, 0)
    s16 = s8 + pltpu.roll(s8, 8, 0)
    lane = lax.broadcasted_iota(jnp.int32, (tm, POOL_W), 1)
    row = lax.broadcasted_iota(jnp.int32, (tm, POOL_W), 0) + i * tm
    ws = _pool_select(lane, s2[POOL_HALO:], s4[POOL_HALO:], s8[POOL_HALO:], s16[POOL_HALO:])
    cnt = jnp.minimum(row + 1, _pool_window(lane)).astype(F32)
    return ws / cnt - cur


def _pool_fwd(u, wbd, scale, tm):
    T = u.shape[0]
    hb = tm // POOL_HALO

    def body(u_ref, prev_ref, w_ref, sc_ref, o_ref):
        d = _pool_delta(u_ref[...], prev_ref[...], pl.program_id(0), tm)
        o_ref[...] = (_dot(d.astype(BF16), w_ref[...]) * sc_ref[...]).astype(BF16)

    return pl.pallas_call(
        body, name="pool_fwd", grid=(T // tm,),
        in_specs=[_tok(tm, POOL_W), pl.BlockSpec((POOL_HALO, POOL_W), lambda i: (jnp.maximum(i * hb - 1, 0), 0)),
                  _res((POOL_W, POOL_W)), _res((1, POOL_W))],
        out_specs=_tok(tm, POOL_W),
        out_shape=jax.ShapeDtypeStruct((T, POOL_W), BF16),
        compiler_params=_params(1),
    )(u, u, wbd, scale)


def _attn_mask(has_prev):
    qi = lax.broadcasted_iota(jnp.int32, (BLK, 2 * BLK), 0)
    kj = lax.broadcasted_iota(jnp.int32, (BLK, 2 * BLK), 1)
    dist = qi + BLK - kj
    return (dist >= 0) & (dist <= BLK) & ((kj >= BLK) | has_prev)


def _stack_heads(x, lo):
    zero = jnp.zeros_like(x)
    return jnp.concatenate([jnp.where(lo, x, zero), jnp.where(lo, zero, x)], axis=0)


def _head_col(tile, lane, h):
    return jnp.sum(jnp.where(lane == h, tile, 0.0), axis=1, keepdims=True)


def _attn_cols(dil):
    return ATTN_W // 2 if dil >= 16 else ATTN_W


def _units_per_step(dil, backward):
    return {1: 4, 4: 2, 16: 4}[dil]


def _chunk_tokens(dil, units):
    return BLK * (units if dil == 1 else dil)


def _unit_steps(dil, units):
    return 1 if dil == 1 else dil // 2 // units


def _attn_specs(dil, nb, units):
    cw = _attn_cols(dil)
    ch = _chunk_tokens(dil, units)
    wide = lambda f: pl.BlockSpec((cw // LANES, ch // 2, LANES), f)
    cur = lambda n, j, r: (j, n, 0)
    prv = lambda n, j, r: (j, jnp.maximum(n - 1, 0), 0)
    prv_out = lambda n, j, r: (j, (n + nb - 1) % nb, 0)
    heads = pl.BlockSpec((ch, LANES), lambda n, j, r: (n, 0))
    return cw, wide(cur), wide(prv), wide(prv_out), heads


HIGH_HALF = 0xFFFF0000


def _pack(x):
    return pltpu.bitcast(x.astype(BF16), F32)


def _unpack(words):
    return pltpu.bitcast(words, BF16)


def _packed(rows, cols):
    return jax.ShapeDtypeStruct((cols // LANES, rows // 2, LANES), F32)


def _tok_packed(tm, cols):
    return pl.BlockSpec((cols // LANES, tm // 2, LANES), lambda i: (0, i, 0))


def _store_packed(ref, x):
    for g in range(x.shape[1] // LANES):
        ref[g] = _pack(x[:, g * LANES:(g + 1) * LANES])


def _load_packed(ref):
    return jnp.concatenate([_unpack(ref[g]) for g in range(ref.shape[0])], axis=1)


def _load_streams(ref, dil, r2, sl):
    if dil == 1:
        return [_unpack(ref.at[sl][pl.ds(r2 * (BLK // 2), BLK // 2), :])]
    words = lax.bitcast_convert_type(ref.at[sl][pl.ds(r2, BLK, stride=dil // 2), :], jnp.uint32)
    even = lax.bitcast_convert_type(words << 16, F32).astype(BF16)
    odd = lax.bitcast_convert_type(words & jnp.uint32(HIGH_HALF), F32).astype(BF16)
    return [even, odd]


def _load_prev_streams(prev_ref, cur_ref, dil, units, r2, sl):
    if dil > 1:
        return _load_streams(prev_ref, dil, r2, sl)
    return _load_streams(cur_ref, 1, r2 - 1, sl) if r2 > 0 else _load_streams(prev_ref, 1, units - 1, sl)


def _load_streams_f32(ref, dil, r2, sl):
    ref = ref if sl is None else ref.at[sl]
    if dil == 1:
        return [ref[pl.ds(r2 * BLK, BLK), :]]
    return [ref[pl.ds(2 * r2 + e, BLK, stride=dil), :] for e in range(2)]


def _store_streams_f32(ref, dil, r2, sl, tiles):
    ref = ref if sl is None else ref.at[sl]
    if dil == 1:
        ref[pl.ds(r2 * BLK, BLK), :] = tiles[0]
    else:
        for e, t in enumerate(tiles):
            ref[pl.ds(2 * r2 + e, BLK, stride=dil), :] = t


def _store_streams(ref, dil, r2, sl, tiles):
    if dil == 1:
        ref.at[sl][pl.ds(r2 * (BLK // 2), BLK // 2), :] = _pack(tiles[0])
    else:
        even, odd = [lax.bitcast_convert_type(t.astype(BF16).astype(F32), jnp.uint32) for t in tiles]
        words = (odd & jnp.uint32(HIGH_HALF)) | (even >> 16)
        ref.at[sl][pl.ds(r2, BLK, stride=dil // 2), :] = lax.bitcast_convert_type(words, F32)


def _attn_fwd(q, k, v, dil, others):
    T = 2 * q.shape[1]
    reps = _units_per_step(dil, False)
    nb = T // _chunk_tokens(dil, reps)
    first = not others
    cw, cur, prv, _, heads = _attn_specs(dil, nb, reps)
    ncb = ATTN_W // cw
    heads_per_step = cw // HEAD_DIM
    n_str = min(dil, 2)
    everything = None

    def body(*refs):
        q_ref, kc_ref, kp_ref, vc_ref, vp_ref = refs[:5]
        acc_ins, lse_ins = refs[5:5 + 2 * len(others):2], refs[6:6 + 2 * len(others):2]
        acc_ref, lse_ref = refs[-2:]
        j = pl.program_id(1)
        lane = lax.broadcasted_iota(jnp.int32, (BLK, LANES), 1)
        lo = lane < HEAD_DIM

        def stream_pair(r2):
            valid = _attn_mask(True if dil == 1 and r2 > 0 else pl.program_id(0) > 0)
            lse_tiles = [jnp.zeros((BLK, LANES), F32) for _ in range(n_str)]
            own = []
            for g in range(cw // LANES):
                qs, kcs, vcs = [_load_streams(r, dil, r2, g) for r in (q_ref, kc_ref, vc_ref)]
                kps, vps = [_load_prev_streams(p, c, dil, reps, r2, g) for p, c in ((kp_ref, kc_ref), (vp_ref, vc_ref))]
                pairs = []
                for e in range(n_str):
                    qg = qs[e] * 0.125
                    kcat = jnp.concatenate([kps[e], kcs[e]], axis=0)
                    vcat = jnp.concatenate([vps[e], vcs[e]], axis=0)
                    pair = None
                    for hh in range(2):
                        h = j * heads_per_step + 2 * g + hh
                        hm = lo if hh == 0 else jnp.logical_not(lo)
                        s = _dot_nt(jnp.where(hm, qg, jnp.zeros_like(qg)), kcat)
                        s = jnp.where(valid, s, NEG)
                        m = jnp.max(s, axis=1, keepdims=True)
                        p = jnp.exp(s - m)
                        den = jnp.sum(p, axis=1, keepdims=True)
                        o = _dot(p.astype(BF16), vcat) / den
                        pair = o if hh == 0 else jnp.where(lo, pair, o)
                        lse_tiles[e] = jnp.where(lane == h, m + jnp.log(den), lse_tiles[e])
                    pairs.append(pair)
                if first:
                    _store_streams(acc_ref, dil, r2, g, pairs)
                else:
                    own.append(pairs)
            if not first:
                mine = (lane >= j * heads_per_step) & (lane < (j + 1) * heads_per_step)
                theirs = [_load_streams_f32(ref, dil, r2, everything) for ref in lse_ins]
                w_theirs, w_own = [[] for _ in others], []
                for e in range(n_str):
                    parts = [t[e] for t in theirs] + [lse_tiles[e]]
                    mx = parts[0]
                    for part in parts[1:]:
                        mx = jnp.maximum(mx, part)
                    total = mx + jnp.log(sum(jnp.exp(part - mx) for part in parts))
                    for i, t in enumerate(theirs):
                        w_theirs[i].append(jnp.exp(t[e] - total))
                    w_own.append(jnp.exp(lse_tiles[e] - total))
                    lse_tiles[e] = jnp.where(mine, total, 0.0)
                for g in range(cw // LANES):
                    h0 = j * heads_per_step + 2 * g
                    spread = lambda w: jnp.where(lo, _head_col(w, lane, h0), _head_col(w, lane, h0 + 1))
                    olds = [_load_streams(ref, dil, r2, g) for ref in acc_ins]
                    _store_streams(acc_ref, dil, r2, g, [
                        sum(olds[i][e].astype(F32) * spread(w_theirs[i][e]) for i in range(len(others)))
                        + own[g][e] * spread(w_own[e]) for e in range(n_str)])
            if ncb == 1:
                _store_streams_f32(lse_ref, dil, r2, everything, lse_tiles)
            else:
                @pl.when(j == 0)
                def _():
                    _store_streams_f32(lse_ref, dil, r2, everything, lse_tiles)

                @pl.when(j > 0)
                def _():
                    before = _load_streams_f32(lse_ref, dil, r2, everything)
                    _store_streams_f32(lse_ref, dil, r2, everything, [a + b for a, b in zip(before, lse_tiles)])

        for rep in range(reps):
            stream_pair(rep if dil == 1 else pl.program_id(2) * reps + rep)

    ins = [q, k, k, v, v]
    in_specs = [cur, cur, prv, cur, prv]
    for acc, lse in others:
        ins += [acc, lse]
        in_specs += [cur, heads]
    return pl.pallas_call(
        body, name=f"attn_fwd_d{dil}", grid=(nb, ncb, _unit_steps(dil, reps)),
        in_specs=in_specs, out_specs=[cur, heads],
        out_shape=[_packed(T, ATTN_W), jax.ShapeDtypeStruct((T, LANES), F32)],
        compiler_params=_params(3),
    )(*ins)


def _mix_fwd(pool, attn, x, w_out, g2, g3, tm):
    T = x.shape[0]

    def body(p_ref, a_ref, x_ref, w_ref, g2_ref, g3_ref, cat_ref, mix_ref, x2_ref, h2_ref):
        p = p_ref[...]
        a = _load_packed(a_ref)
        cat_ref[...] = jnp.concatenate([p, a], axis=1)
        mix = _dot(p, w_ref[:POOL_W, :]) + _dot(a, w_ref[POOL_W:, :])
        mix_ref[...] = mix
        x2 = x_ref[...] + _rms_fwd(mix, g2_ref[...])
        x2_ref[...] = x2
        h2_ref[...] = _rms_fwd(x2, g3_ref[...]).astype(BF16)

    return pl.pallas_call(
        body, name="mix_fwd", grid=(T // tm,),
        in_specs=[_tok(tm, POOL_W), _tok_packed(tm, ATTN_W), _tok(tm, D_MODEL), _res((D_MODEL, D_MODEL)),
                  _res((1, D_MODEL)), _res((1, D_MODEL))],
        out_specs=[_tok(tm, D_MODEL)] * 4,
        out_shape=[jax.ShapeDtypeStruct((T, D_MODEL), BF16), jax.ShapeDtypeStruct((T, D_MODEL), F32),
                   jax.ShapeDtypeStruct((T, D_MODEL), F32), jax.ShapeDtypeStruct((T, D_MODEL), BF16)],
        compiler_params=_params(1),
    )(pool, attn, x, w_out, g2, g3)


def _ffn_up(h2, wg_t, wu_t, tm):
    T = h2.shape[0]

    def body(h_ref, wg_ref, wu_ref, dg_ref, du_ref, a_ref):
        h = h_ref[...]
        gate = _dot_nt(h, wg_ref[...])
        up = _dot_nt(h, wu_ref[...])
        sg = 1.0 / (1.0 + jnp.exp(-gate))
        silu = gate * sg
        a_ref[...] = (silu * up).astype(BF16)
        dg_ref[...] = (up * (sg * (1.0 + gate * (1.0 - sg)))).astype(BF16)
        du_ref[...] = silu.astype(BF16)

    return pl.pallas_call(
        body, name="ffn_up", grid=(T // tm,),
        in_specs=[_tok(tm, D_MODEL), _res((D_FF, D_MODEL)), _res((D_FF, D_MODEL))],
        out_specs=[_tok(tm, D_FF)] * 3,
        out_shape=[jax.ShapeDtypeStruct((T, D_FF), BF16)] * 3,
        compiler_params=_params(1),
    )(h2, wg_t, wu_t)


def _ffn_down_loss(act, w_down, x2, g4, tgt, tm):
    T = act.shape[0]

    def body(a_ref, w_ref, x2_ref, g_ref, t_ref, df_ref, dy_ref, dg_ref, loss_ref):
        i = pl.program_id(0)

        @pl.when(i == 0)
        def _():
            dg_ref[...] = jnp.zeros_like(dg_ref)
            loss_ref[...] = jnp.zeros_like(loss_ref)

        f = _dot(a_ref[...], w_ref[...])
        g = g_ref[...]
        err = x2_ref[...] + _rms_fwd(f, g) - t_ref[...]
        loss_ref[...] += 0.5 * jnp.sum(jnp.mean(err * err, axis=-1, keepdims=True), axis=0, keepdims=True)
        dy = err * (1.0 / D_MODEL)
        dy_ref[...] = dy
        df, dg = _rms_bwd(f, g, dy)
        dg_ref[...] += dg
        df_ref[...] = df.astype(BF16)

    return pl.pallas_call(
        body, name="ffn_down_loss", grid=(T // tm,),
        in_specs=[_tok(tm, D_FF), _res((D_FF, D_MODEL)), _tok(tm, D_MODEL), _res((1, D_MODEL)), _tok(tm, D_MODEL)],
        out_specs=[_tok(tm, D_MODEL), _tok(tm, D_MODEL), _acc((1, D_MODEL)), _acc((1, 1))],
        out_shape=[jax.ShapeDtypeStruct((T, D_MODEL), BF16), jax.ShapeDtypeStruct((T, D_MODEL), F32),
                   jax.ShapeDtypeStruct((1, D_MODEL), F32), jax.ShapeDtypeStruct((1, 1), F32)],
        compiler_params=_params(1),
    )(act, w_down, x2, g4, tgt)


def _ffn_act_bwd(df, w_down, act_dgate, act_dup, tm):
    T = df.shape[0]

    def body(df_ref, w_ref, ag_ref, au_ref, dg_ref, du_ref):
        dact = _dot_nt(df_ref[...], w_ref[...])
        dg_ref[...] = (dact * ag_ref[...].astype(F32)).astype(BF16)
        du_ref[...] = (dact * au_ref[...].astype(F32)).astype(BF16)

    return pl.pallas_call(
        body, name="ffn_act_bwd", grid=(T // tm,),
        in_specs=[_tok(tm, D_MODEL), _res((D_FF, D_MODEL)), _tok(tm, D_FF), _tok(tm, D_FF)],
        out_specs=[_tok(tm, D_FF)] * 2,
        out_shape=[jax.ShapeDtypeStruct((T, D_FF), BF16)] * 2,
        compiler_params=_params(1),
    )(df, w_down, act_dgate, act_dup)


def _ffn_in_bwd(dgate, dup, wg_t, wu_t, x2, mix, dy, g3, g2, tm):
    T = x2.shape[0]

    def body(dg_ref, du_ref, wg_ref, wu_ref, x2_ref, mix_ref, dy_ref, g3_ref, g2_ref,
             dx2_ref, dmix_ref, dg3_ref, dg2_ref):
        @pl.when(pl.program_id(0) == 0)
        def _():
            dg3_ref[...] = jnp.zeros_like(dg3_ref)
            dg2_ref[...] = jnp.zeros_like(dg2_ref)

        dh2 = _dot(dg_ref[...], wg_ref[...]) + _dot(du_ref[...], wu_ref[...])
        dn, dg3 = _rms_bwd(x2_ref[...], g3_ref[...], dh2)
        dx2 = dy_ref[...] + dn
        dx2_ref[...] = dx2
        dg3_ref[...] += dg3
        dmix, dg2 = _rms_bwd(mix_ref[...], g2_ref[...], dx2)
        dg2_ref[...] += dg2
        dmix_ref[...] = dmix.astype(BF16)

    return pl.pallas_call(
        body, name="ffn_in_bwd", grid=(T // tm,),
        in_specs=[_tok(tm, D_FF), _tok(tm, D_FF), _res((D_FF, D_MODEL)), _res((D_FF, D_MODEL)),
                  _tok(tm, D_MODEL), _tok(tm, D_MODEL), _tok(tm, D_MODEL), _res((1, D_MODEL)), _res((1, D_MODEL))],
        out_specs=[_tok(tm, D_MODEL), _tok(tm, D_MODEL), _acc((1, D_MODEL)), _acc((1, D_MODEL))],
        out_shape=[jax.ShapeDtypeStruct((T, D_MODEL), F32), jax.ShapeDtypeStruct((T, D_MODEL), BF16),
                   jax.ShapeDtypeStruct((1, D_MODEL), F32), jax.ShapeDtypeStruct((1, D_MODEL), F32)],
        compiler_params=_params(1),
    )(dgate, dup, wg_t, wu_t, x2, mix, dy, g3, g2)


def _mix_bwd(dmix, w_out, tm):
    T = dmix.shape[0]

    def body(d_ref, w_ref, dp_ref, da_ref):
        dcat = _dot_nt(d_ref[...], w_ref[...])
        dp_ref[...] = dcat[:, :POOL_W].astype(BF16)
        _store_packed(da_ref, dcat[:, POOL_W:])

    return pl.pallas_call(
        body, name="mix_bwd", grid=(T // tm,),
        in_specs=[_tok(tm, D_MODEL), _res((D_MODEL, D_MODEL))],
        out_specs=[_tok(tm, POOL_W), _tok_packed(tm, ATTN_W)],
        out_shape=[jax.ShapeDtypeStruct((T, POOL_W), BF16), _packed(T, ATTN_W)],
        compiler_params=_params(1),
    )(dmix, w_out)


def _attn_bwd(q, k, v, dout, out, lse, dil):
    T = 2 * q.shape[1]
    reps = _units_per_step(dil, True)
    nb = T // _chunk_tokens(dil, reps)
    cw, cur, prv, prv_out, heads = _attn_specs(dil, nb, reps)
    ncb = ATTN_W // cw
    heads_per_step = cw // HEAD_DIM
    n_str = min(dil, 2)

    def body(q_ref, kc_ref, kp_ref, vc_ref, vp_ref, do_ref, o_ref, lse_ref,
             dq_ref, dkc_ref, dkp_ref, dvc_ref, dvp_ref):
        j = pl.program_id(1)
        lane = lax.broadcasted_iota(jnp.int32, (BLK, LANES), 1)
        lo = lane < HEAD_DIM

        def unit_grads(r2, g):
            valid = _attn_mask(True if dil == 1 and r2 > 0 else pl.program_id(0) > 0)
            valid2 = jnp.concatenate([valid, valid], axis=0)
            lse_tiles = _load_streams_f32(lse_ref, dil, r2, None)
            qs, kcs, vcs, dos, os_ = [_load_streams(r, dil, r2, g) for r in (q_ref, kc_ref, vc_ref, do_ref, o_ref)]
            kps, vps = [_load_prev_streams(p, c, dil, reps, r2, g) for p, c in ((kp_ref, kc_ref), (vp_ref, vc_ref))]
            dqs, dks, dvs = [], [], []
            for e in range(n_str):
                qg = qs[e] * 0.125
                dog = dos[e]
                kcat = jnp.concatenate([kps[e], kcs[e]], axis=0)
                vcat = jnp.concatenate([vps[e], vcs[e]], axis=0)
                prod = dog.astype(F32) * os_[e].astype(F32)
                h0 = j * heads_per_step + 2 * g
                q2 = _stack_heads(qg, lo)
                do2 = _stack_heads(dog, lo)
                lse2 = jnp.concatenate([_head_col(lse_tiles[e], lane, h0), _head_col(lse_tiles[e], lane, h0 + 1)], axis=0)
                dsum2 = jnp.concatenate([jnp.sum(jnp.where(lo, prod, 0.0), axis=1, keepdims=True),
                                         jnp.sum(jnp.where(lo, 0.0, prod), axis=1, keepdims=True)], axis=0)
                p = jnp.exp(jnp.where(valid2, _dot_nt(q2, kcat), NEG) - lse2)
                ds = (p * (_dot_nt(do2, vcat) - dsum2)).astype(BF16)
                dvs.append(_dot_tn(p.astype(BF16), do2))
                dks.append(_dot_tn(ds, q2))
                dq2 = _dot(ds, kcat) * 0.125
                dqs.append(jnp.where(lo, dq2[:BLK], dq2[BLK:]))
            return dqs, dks, dvs

        for g in range(cw // LANES):
            if dil > 1:
                for rep in range(reps):
                    r2 = pl.program_id(2) * reps + rep
                    dqs, dks, dvs = unit_grads(r2, g)
                    _store_streams(dq_ref, dil, r2, g, dqs)
                    _store_streams(dkp_ref, dil, r2, g, [t[:BLK] for t in dks])
                    _store_streams(dkc_ref, dil, r2, g, [t[BLK:] for t in dks])
                    _store_streams(dvp_ref, dil, r2, g, [t[:BLK] for t in dvs])
                    _store_streams(dvc_ref, dil, r2, g, [t[BLK:] for t in dvs])
            else:
                blocks = [unit_grads(b, g) for b in range(reps)]
                for b, (dqs, dks, dvs) in enumerate(blocks):
                    _store_streams(dq_ref, 1, b, g, dqs)
                    for cur_ref, prev_ref, which in ((dkc_ref, dkp_ref, 1), (dvc_ref, dvp_ref, 2)):
                        own = blocks[b][which][0][BLK:]
                        if b + 1 < reps:
                            own = own + blocks[b + 1][which][0][:BLK]
                        _store_streams(cur_ref, 1, b, g, [own])
                        edge = blocks[0][which][0][:BLK] if b == reps - 1 else jnp.zeros((BLK, LANES), F32)
                        _store_streams(prev_ref, 1, b, g, [edge])

    return pl.pallas_call(
        body, name=f"attn_bwd_d{dil}", grid=(nb, ncb, _unit_steps(dil, reps)),
        in_specs=[cur, cur, prv, cur, prv, cur, cur, heads],
        out_specs=[cur, cur, prv_out, cur, prv_out],
        out_shape=[_packed(T, ATTN_W)] * 5,
        compiler_params=_params(3),
    )(q, k, k, v, v, dout, out, lse)


def _pool_bwd(u, dy, wbd, scale, tm):
    T = u.shape[0]
    nt = T // tm
    hb = tm // POOL_HALO

    def body(u_ref, prev_ref, dy_ref, next_ref, w_ref, sc_ref, du_ref, dw_ref, dsc_ref):
        i = pl.program_id(0)

        @pl.when(i == 0)
        def _():
            dw_ref[...] = jnp.zeros_like(dw_ref)
            dsc_ref[...] = jnp.zeros_like(dsc_ref)

        w = w_ref[...]
        sc = sc_ref[...]
        d = _pool_delta(u_ref[...], prev_ref[...], i, tm).astype(BF16)
        dyc = dy_ref[...].astype(F32)
        dsc_ref[...] += jnp.sum(dyc * _dot(d, w), axis=0, keepdims=True)
        nxt = jnp.where(i < nt - 1, next_ref[...].astype(F32), 0.0)
        dypre = (jnp.concatenate([dyc, nxt], axis=0) * sc).astype(BF16)
        dw_ref[...] += _dot_tn(d, dypre[:tm])
        dd = _dot_nt(dypre, w)
        n = tm + POOL_HALO
        lane = lax.broadcasted_iota(jnp.int32, (n, POOL_W), 1)
        row = lax.broadcasted_iota(jnp.int32, (n, POOL_W), 0) + i * tm
        gx = dd / jnp.minimum(row + 1, _pool_window(lane)).astype(F32)
        a2 = gx + pltpu.roll(gx, n - 1, 0)
        a4 = a2 + pltpu.roll(a2, n - 2, 0)
        a8 = a4 + pltpu.roll(a4, n - 4, 0)
        a16 = a8 + pltpu.roll(a8, n - 8, 0)
        fs = _pool_select(lane[:tm], a2[:tm], a4[:tm], a8[:tm], a16[:tm])
        du_ref[...] = (fs - dd[:tm]).astype(BF16)

    return pl.pallas_call(
        body, name="pool_bwd", grid=(nt,),
        in_specs=[_tok(tm, POOL_W), pl.BlockSpec((POOL_HALO, POOL_W), lambda i: (jnp.maximum(i * hb - 1, 0), 0)),
                  _tok(tm, POOL_W), pl.BlockSpec((POOL_HALO, POOL_W), lambda i: (jnp.minimum((i + 1) * hb, nt * hb - 1), 0)),
                  _res((POOL_W, POOL_W)), _res((1, POOL_W))],
        out_specs=[_tok(tm, POOL_W), _acc((POOL_W, POOL_W)), _acc((1, POOL_W))],
        out_shape=[jax.ShapeDtypeStruct((T, POOL_W), BF16), jax.ShapeDtypeStruct((POOL_W, POOL_W), F32),
                   jax.ShapeDtypeStruct((1, POOL_W), F32)],
        compiler_params=_params(1),
    )(u, u, dy, dy, wbd, scale)


def _dproj_combine(du, dqs, dkcs, dkps, dvcs, dvps, cos, sin, tm):
    T = du.shape[0]
    n_cfg = len(dqs)

    def body(*refs):
        du_ref = refs[0]
        groups = [refs[1 + j * n_cfg:1 + (j + 1) * n_cfg] for j in range(5)]
        c_ref, s_ref, out_ref = refs[1 + 5 * n_cfg:]
        tot = lambda rs: sum(_load_packed(r).astype(F32) for r in rs)
        c = c_ref[...]
        s = s_ref[...]
        dq = _rope(tot(groups[0]), c, s, -1.0)
        dk = _rope(tot(groups[1]) + tot(groups[2]), c, s, -1.0)
        dv = tot(groups[3]) + tot(groups[4])
        out_ref[...] = jnp.concatenate([du_ref[...], dq.astype(BF16), dk.astype(BF16), dv.astype(BF16)], axis=1)

    return pl.pallas_call(
        body, name="dproj_combine", grid=(T // tm,),
        in_specs=[_tok(tm, POOL_W)] + [_tok_packed(tm, ATTN_W)] * (5 * n_cfg) + [_tok(tm, LANES)] * 2,
        out_specs=_tok(tm, IN_W),
        out_shape=jax.ShapeDtypeStruct((T, IN_W), BF16),
        compiler_params=_params(1),
    )(du, *dqs, *dkcs, *dkps, *dvcs, *dvps, cos, sin)


def _proj_bwd(dproj, w_in_t, x, dx2, g1, tm):
    T = x.shape[0]

    def body(d_ref, w_ref, x_ref, r_ref, g_ref, dx_ref, dg_ref):
        @pl.when(pl.program_id(0) == 0)
        def _():
            dg_ref[...] = jnp.zeros_like(dg_ref)

        dn, dg = _rms_bwd(x_ref[...], g_ref[...], _dot(d_ref[...], w_ref[...]))
        dg_ref[...] += dg
        dx_ref[...] = r_ref[...] + dn

    return pl.pallas_call(
        body, name="proj_bwd", grid=(T // tm,),
        in_specs=[_tok(tm, IN_W), _res((IN_W, D_MODEL)), _tok(tm, D_MODEL), _tok(tm, D_MODEL), _res((1, D_MODEL))],
        out_specs=[_tok(tm, D_MODEL), _acc((1, D_MODEL))],
        out_shape=[jax.ShapeDtypeStruct((T, D_MODEL), F32), jax.ShapeDtypeStruct((1, D_MODEL), F32)],
        compiler_params=_params(1),
    )(dproj, w_in_t, x, dx2, g1)


def _wgrad(a, b, name, tile_m, tk):
    T, M = a.shape
    N = b.shape[1]
    nk = T // tk

    def body(a_ref, b_ref, o_ref, acc_ref):
        kk = pl.program_id(1)

        @pl.when(kk == 0)
        def _():
            acc_ref[...] = jnp.zeros_like(acc_ref)

        acc_ref[...] += _dot_tn(a_ref[...], b_ref[...])

        @pl.when(kk == nk - 1)
        def _():
            o_ref[...] = acc_ref[...].astype(BF16)

    return pl.pallas_call(
        body, name=name, grid=(M // tile_m, nk),
        in_specs=[pl.BlockSpec((tk, tile_m), lambda j, kk: (kk, j)), pl.BlockSpec((tk, N), lambda j, kk: (kk, 0))],
        out_specs=pl.BlockSpec((tile_m, N), lambda j, kk: (j, 0)),
        out_shape=jax.ShapeDtypeStruct((M, N), BF16),
        scratch_shapes=[pltpu.VMEM((tile_m, N), F32)],
        compiler_params=_params(2),
    )(a, b)


def _exchange(arrs, scatter, name):
    n = len(arrs)
    out_shapes = [jax.ShapeDtypeStruct((N_DEV,) + (a.shape[1:] if sc else a.shape), a.dtype)
                  for a, sc in zip(arrs, scatter)]

    def body(*refs):
        ins, outs = refs[:n], refs[n:2 * n]
        send_sems, recv_sems, loc_sems = refs[2 * n:]
        x, y, c = lax.axis_index("x"), lax.axis_index("y"), lax.axis_index("c")
        me = 4 * x + 2 * y + c
        local, sends, recvs = [], [], []
        for i in range(n):
            own = ins[i].at[me] if scatter[i] else ins[i]
            loc = pltpu.make_async_copy(own, outs[i].at[me], loc_sems.at[i])
            loc.start()
            local.append(loc)
            for kbits in range(1, N_DEV):
                px = 1 - x if kbits & 4 else x
                py = 1 - y if kbits & 2 else y
                pc = 1 - c if kbits & 1 else c
                pid = 4 * px + 2 * py + pc
                src = ins[i].at[pid] if scatter[i] else ins[i]
                cp = pltpu.make_async_remote_copy(
                    src_ref=src, dst_ref=outs[i].at[me],
                    send_sem=send_sems.at[i, kbits - 1], recv_sem=recv_sems.at[i, kbits - 1],
                    device_id=(px, py, pc), device_id_type=pl.DeviceIdType.MESH)
                cp.start()
                sends.append(cp)
                recvs.append(pltpu.make_async_remote_copy(
                    src_ref=src, dst_ref=outs[i].at[pid],
                    send_sem=send_sems.at[i, kbits - 1], recv_sem=recv_sems.at[i, kbits - 1],
                    device_id=(px, py, pc), device_id_type=pl.DeviceIdType.MESH))
        for cp in recvs:
            cp.wait_recv()
        for cp in sends:
            cp.wait_send()
        for cp in local:
            cp.wait()

    hbm = pl.BlockSpec(memory_space=pl.ANY)
    return pl.pallas_call(
        body, name=name, in_specs=[hbm] * n, out_specs=[hbm] * n, out_shape=out_shapes,
        scratch_shapes=[pltpu.SemaphoreType.DMA((n, N_DEV - 1)), pltpu.SemaphoreType.DMA((n, N_DEV - 1)),
                        pltpu.SemaphoreType.DMA((n,))],
    )(*arrs)


def _gather_two_level(arr, name):
    def body(x_ref, out_ref, send_sems, recv_sems, local_sem):
        x, y, c = lax.axis_index("x"), lax.axis_index("y"), lax.axis_index("c")
        me, sibling = (x, y, c), (x, y, 1 - c)
        chips = [(1 - x, y), (x, 1 - y), (1 - x, 1 - y)]
        slot = lambda px, py, pc: out_ref.at[4 * px + 2 * py + pc]

        def copy(k, block, to, src=None):
            return pltpu.make_async_remote_copy(
                src_ref=slot(*block) if src is None else src, dst_ref=slot(*block),
                send_sem=send_sems.at[k], recv_sem=recv_sems.at[k],
                device_id=to, device_id_type=pl.DeviceIdType.MESH)

        mine = pltpu.make_async_copy(x_ref, slot(*me), local_sem)
        mine.start()
        first = [copy(0, me, sibling, src=x_ref)]
        first += [copy(1 + i, me, (*chip, c), src=x_ref) for i, chip in enumerate(chips)]
        for cp in first:
            cp.start()
        passed = [copy(4 + i, (*chip, c), sibling) for i, chip in enumerate(chips)]
        for i, chip in enumerate(chips):
            copy(1 + i, (*chip, c), me).wait_recv()
            passed[i].start()
        copy(0, sibling, me).wait_recv()
        for i, chip in enumerate(chips):
            copy(4 + i, (*chip, 1 - c), me).wait_recv()
        for cp in first + passed:
            cp.wait_send()
        mine.wait()

    hbm = pl.BlockSpec(memory_space=pl.ANY)
    return pl.pallas_call(
        body, name=name, in_specs=[hbm], out_specs=hbm,
        out_shape=jax.ShapeDtypeStruct((N_DEV,) + arr.shape, arr.dtype),
        scratch_shapes=[pltpu.SemaphoreType.DMA((N_DEV - 1,)), pltpu.SemaphoreType.DMA((N_DEV - 1,)),
                        pltpu.SemaphoreType.DMA],
    )(arr)


def _peers(x, y, c):
    for kbits in range(1, N_DEV):
        px = 1 - x if kbits & 4 else x
        py = 1 - y if kbits & 2 else y
        pc = 1 - c if kbits & 1 else c
        yield kbits - 1, (px, py, pc), 4 * px + 2 * py + pc


def _peer_copies(ins, lands, scatter, send_sems, recv_sems, incoming):
    x, y, c = lax.axis_index("x"), lax.axis_index("y"), lax.axis_index("c")
    me = 4 * x + 2 * y + c
    copies = []
    for i in range(len(ins)):
        for k, peer, pid in _peers(x, y, c):
            slot = i * (N_DEV - 1) + k
            copies.append(pltpu.make_async_remote_copy(
                src_ref=ins[i].at[pid] if scatter[i] else ins[i], dst_ref=lands[i].at[pid if incoming else me],
                send_sem=send_sems.at[slot], recv_sem=recv_sems.at[slot],
                device_id=peer, device_id_type=pl.DeviceIdType.MESH))
    return copies


_HBM = pl.BlockSpec(memory_space=pltpu.HBM)
_SEM = pl.BlockSpec(memory_space=pltpu.SEMAPHORE)
_DATAFLOW = pltpu.SideEffectType.DATAFLOW_SIDE_EFFECTING


def _exchange_start(arrs, scatter, after, name):
    n = len(arrs)
    lands = [lax.empty((N_DEV,) + (a.shape[1:] if sc else a.shape), a.dtype) for a, sc in zip(arrs, scatter)]

    def body(*refs):
        ins, lz = refs[:n], refs[n:2 * n]
        send_sems, recv_sems = refs[2 * n + 1:2 * n + 3]
        token = refs[-1]
        for cp in _peer_copies(ins, lz, scatter, send_sems, recv_sems, False):
            cp.start()
        token[...] = jnp.zeros_like(token)

    sem_shape = pltpu.SemaphoreType.DMA((n * (N_DEV - 1),))
    outs = pl.pallas_call(
        body, name=name,
        out_shape=(sem_shape, sem_shape, *[pltpu.HBM(a.shape, a.dtype) for a in arrs + lands],
                   jax.ShapeDtypeStruct((8, LANES), F32)),
        in_specs=[_HBM] * (2 * n) + [pl.BlockSpec(memory_space=pl.ANY)],
        out_specs=(_SEM, _SEM, *[_HBM] * (2 * n), pl.BlockSpec(memory_space=pltpu.VMEM)),
        input_output_aliases={i: 2 + i for i in range(2 * n)},
        compiler_params=pltpu.CompilerParams(has_side_effects=_DATAFLOW),
    )(*[pltpu.with_memory_space_constraint(a, pltpu.HBM) for a in arrs + lands], after)
    return outs[0], outs[1], list(outs[2:2 + n]), list(outs[2 + n:2 + 2 * n]), outs[-1]


def _exchange_wait(handle, scatter, after, name):
    send_sems, recv_sems, srcs, lands, _ = handle
    n = len(srcs)

    def body(*refs):
        ins, lz = refs[:n], refs[n:2 * n]
        for cp in _peer_copies(ins, lz, scatter, refs[2 * n], refs[2 * n + 1], False):
            cp.wait_send()
        for cp in _peer_copies(ins, lz, scatter, refs[2 * n], refs[2 * n + 1], True):
            cp.wait_recv()

    outs = pl.pallas_call(
        body, name=name,
        out_shape=[pltpu.HBM(a.shape, a.dtype) for a in srcs + lands],
        in_specs=[_HBM] * (2 * n) + [_SEM, _SEM, pl.BlockSpec(memory_space=pl.ANY)],
        out_specs=[_HBM] * (2 * n),
        input_output_aliases={i: i for i in range(2 * n)},
        compiler_params=pltpu.CompilerParams(has_side_effects=_DATAFLOW),
    )(*srcs, *lands, send_sems, recv_sems, after)
    return list(outs[:n]), list(outs[n:])


def _fill_own(lands, srcs, scatter):
    me = 4 * lax.axis_index("x") + 2 * lax.axis_index("y") + lax.axis_index("c")
    own = [lax.dynamic_index_in_dim(s, me, 0, keepdims=False) if sc else s for s, sc in zip(srcs, scatter)]
    return [lax.dynamic_update_index_in_dim(land, o, me, 0) for land, o in zip(lands, own)]


def _slot_sum(parts, name, tr):
    _, R, C = parts.shape

    def body(p_ref, o_ref):
        acc = p_ref[0].astype(F32)
        for s in range(1, N_DEV):
            acc = acc + p_ref[s].astype(F32)
        o_ref[...] = acc

    return pl.pallas_call(
        body, name=name, grid=(R // tr,),
        in_specs=[pl.BlockSpec((N_DEV, tr, C), lambda i: (0, i, 0))],
        out_specs=pl.BlockSpec((tr, C), lambda i: (i, 0)),
        out_shape=jax.ShapeDtypeStruct((R, C), F32),
        compiler_params=_params(1),
    )(parts)


def _adamw(w, g, m, v, name):
    def body(w_ref, g_ref, m_ref, v_ref, d_ref, nm_ref, nv_ref):
        g = g_ref[...]
        nm = ADAM_B1 * m_ref[...] + (1.0 - ADAM_B1) * g
        nv = ADAM_B2 * v_ref[...] + (1.0 - ADAM_B2) * jnp.square(g)
        m_hat = nm / (1.0 - ADAM_B1 ** ADAM_STEP)
        v_hat = nv / (1.0 - ADAM_B2 ** ADAM_STEP)
        d_ref[...] = -ADAM_LR * (m_hat / (jnp.sqrt(v_hat) + ADAM_EPS) + ADAM_WD * w_ref[...])
        nm_ref[...] = nm
        nv_ref[...] = nv

    return pl.pallas_call(
        body, name=name, out_shape=[jax.ShapeDtypeStruct(w.shape, F32)] * 3,
        compiler_params=pltpu.CompilerParams(vmem_limit_bytes=VMEM_LIMIT),
    )(w, g, m, v)


def _rope_tables(T):
    half = HEAD_DIM // 2
    freqs = ROPE_THETA ** (-jnp.arange(half, dtype=F32) * (2.0 / HEAD_DIM))
    ang = jnp.arange(T).astype(F32)[:, None] * jnp.tile(freqs, LANES // half)[None, :]
    sign = jnp.tile(jnp.concatenate([-jnp.ones((half,), F32), jnp.ones((half,), F32)]), LANES // HEAD_DIM)
    return jnp.cos(ang), jnp.sin(ang) * sign[None, :]


def _block_diag(w_pool):
    wbd = jnp.zeros((POOL_W, POOL_W), F32)
    g = POOL_W // len(POOL_WINDOWS)
    for i in range(len(POOL_WINDOWS)):
        wbd = wbd.at[i * g:(i + 1) * g, i * g:(i + 1) * g].set(w_pool[i])
    return wbd


def _pack_small(g1, w_pool, pool_scale, g2, g3, g4, extra):
    pad = lambda a: jnp.pad(a.reshape(1, -1), ((0, 0), (0, D_MODEL - a.size)))
    rows = [g1.reshape(1, -1), g2.reshape(1, -1), g3.reshape(1, -1), g4.reshape(1, -1),
            w_pool.reshape(-1, D_MODEL), pad(pool_scale), pad(extra)]
    buf = jnp.concatenate(rows, axis=0)
    return jnp.pad(buf, ((0, SMALL_ROWS - buf.shape[0]), (0, 0)))


def _unpack_small(buf):
    n_pool = len(POOL_WINDOWS) * (POOL_W // len(POOL_WINDOWS)) ** 2 // D_MODEL
    g = POOL_W // len(POOL_WINDOWS)
    return (buf[0:1], buf[4:4 + n_pool].reshape(1, len(POOL_WINDOWS), g, g), buf[4 + n_pool:5 + n_pool, :POOL_W],
            buf[1:2], buf[2:3], buf[3:4], buf[5 + n_pool])


class _LocalStep:
    def __init__(self, x, tgt, g1, w_pool, pool_scale, g2, g3, g4):
        self.x, self.tgt, self.pool_scale = x, tgt, pool_scale
        self.g1, self.g2, self.g3, self.g4 = g1, g2, g3, g4
        self.cos, self.sin = _rope_tables(x.shape[0])
        self.wbd = _block_diag(w_pool).astype(BF16)

    def mixer_fwd(self, w_in_t, token):
        self.w_in_t = w_in_t
        self.h1, self.u, self.q, self.k, self.v = _proj_fwd(
            self.x, self.g1 + token[0, 0], w_in_t, self.cos, self.sin, 1024)
        self.pool = _pool_fwd(self.u, self.wbd, self.pool_scale, 1024)
        alone = [_attn_fwd(self.q, self.k, self.v, dil, []) for dil in DILATIONS[:-1]]
        self.attn, self.lse = _attn_fwd(self.q, self.k, self.v, DILATIONS[-1], alone)
        return self.attn

    def ffn_fwd_bwd(self, w_out, wg_t, wu_t, w_down):
        self.w_out, self.wg_t, self.wu_t = w_out, wg_t, wu_t
        self.cat, self.mix, self.x2, h2 = _mix_fwd(self.pool, self.attn, self.x, w_out, self.g2, self.g3, 1024)
        act_dgate, act_dup, act = _ffn_up(h2, wg_t, wu_t, 256)
        df, self.dy, self.dg4, self.loss = _ffn_down_loss(act, w_down, self.x2, self.g4, self.tgt, 512)
        self.dgate, self.dup = _ffn_act_bwd(df, w_down, act_dgate, act_dup, 512)
        return (_wgrad(self.dgate, h2, "wgrad_gate", D_FF // 2, 1024), _wgrad(self.dup, h2, "wgrad_up", D_FF // 2, 1024),
                _wgrad(act, df, "wgrad_down", D_FF // 2, 1024))

    def mixer_bwd(self, token):
        self.dx2, dmix, self.dg3, self.dg2 = _ffn_in_bwd(
            self.dgate, self.dup, self.wg_t, self.wu_t, self.x2, self.mix, self.dy, self.g3 + token[0, 0], self.g2, 512)
        dpool, dattn = _mix_bwd(dmix, self.w_out, 1024)
        parts = [_attn_bwd(self.q, self.k, self.v, dattn, self.attn, self.lse, dil) for dil in DILATIONS]
        du, dwbd, self.dscale = _pool_bwd(self.u, dpool, self.wbd, self.pool_scale, 1024)
        g = POOL_W // len(POOL_WINDOWS)
        self.dw_pool = jnp.stack([dwbd[i * g:(i + 1) * g, i * g:(i + 1) * g] for i in range(len(POOL_WINDOWS))])
        self.dproj = _dproj_combine(du, *[[p[j] for p in parts] for j in range(5)], self.cos, self.sin, 512)
        return _wgrad(self.dproj, self.h1, "wgrad_in", IN_W // 2, 1024), _wgrad(self.cat, dmix, "wgrad_out", D_MODEL, 1024)

    def input_bwd(self, token):
        grad_x, dg1 = _proj_bwd(self.dproj, self.w_in_t, self.x, self.dx2, self.g1 + token[0, 0], 1024)
        return self.loss, grad_x, (dg1, self.dw_pool, self.dscale, self.dg2, self.dg3, self.dg4)


def _local_step(x, tgt, g1, w_pool, pool_scale, g2, g3, g4, w_in_t, w_out, wg_t, wu_t, w_down):
    zero = jnp.zeros((8, LANES), F32)
    step = _LocalStep(x, tgt, g1, w_pool, pool_scale, g2, g3, g4)
    step.mixer_fwd(w_in_t, zero)
    dw_gate, dw_up, dw_down = step.ffn_fwd_bwd(w_out, wg_t, wu_t, w_down)
    dw_in, dw_out = step.mixer_bwd(zero)
    loss, grad_x, small = step.input_bwd(zero)
    return loss, grad_x, small, (dw_in, dw_out, dw_gate, dw_up, dw_down)


def kernel(x, ln_pre_mix, w_in, w_pool, pool_scale, w_out, ln_post_mix, ln_pre_ffn, w_gate, w_up, w_down, ln_post_ffn, loss_target, m_ln_pre_mix, m_w_in, m_w_pool, m_pool_scale, m_w_out, m_ln_post_mix, m_ln_pre_ffn, m_w_gate, m_w_up, m_w_down, m_ln_post_ffn, v_ln_pre_mix, v_w_in, v_w_pool, v_pool_scale, v_w_out, v_ln_post_mix, v_ln_pre_ffn, v_w_gate, v_w_up, v_w_down, v_ln_post_ffn):
    shards = [w_in[0].T.astype(BF16), w_out[0].astype(BF16), w_gate[0].T.astype(BF16),
              w_up[0].T.astype(BF16), w_down[0].astype(BF16)]
    flat = lambda a: a.reshape(-1, D_MODEL)
    blocks = lambda a: a.reshape(N_DEV, -1, D_MODEL)
    step = _LocalStep(x[0], loss_target[0], ln_pre_mix, w_pool[0], pool_scale, ln_post_mix, ln_pre_ffn, ln_post_ffn)

    w_in_t = flat(_gather_two_level(shards[0], "gather_w_in"))
    rest = _exchange_start(shards[1:], [False] * 4, w_in_t, "gather_rest_start")
    attn = step.mixer_fwd(w_in_t, rest[4])
    srcs, lands = _exchange_wait(rest, [False] * 4, attn, "gather_rest_wait")
    w_out_f, wg_t, wu_t, w_down_f = [flat(a) for a in _fill_own(lands, srcs, [False] * 4)]

    ffn = _exchange_start([blocks(a) for a in step.ffn_fwd_bwd(w_out_f, wg_t, wu_t, w_down_f)], [True] * 3,
                          step.dgate, "grads_ffn_start")
    mixer = _exchange_start([blocks(a) for a in step.mixer_bwd(ffn[4])], [True] * 2, step.dproj, "grads_mixer_start")
    loss, grad_x, small = step.input_bwd(mixer[4])
    got = []
    for handle, n_arr, nm in ((mixer, 2, "grads_mixer"), (ffn, 3, "grads_ffn")):
        srcs, lands = _exchange_wait(handle, [True] * n_arr, grad_x, nm + "_wait")
        got += _fill_own(lands, srcs, [True] * n_arr)
    sums = [_slot_sum(got[i], f"sum_grad_{i}", got[i].shape[1] // 2) for i in range(5)]

    small_buf = _pack_small(small[0], small[1], small[2], small[3], small[4], small[5], loss)
    small_sum = _slot_sum(_exchange([small_buf], [False], "gather_small")[0], "sum_small", SMALL_ROWS)

    g_in, g_out, g_gate, g_up, g_down = sums[0].T, sums[1], sums[2].T, sums[3].T, sums[4]
    upd = [_adamw(w[0], g, m[0], v[0], f"adamw_{nm}") for nm, w, g, m, v in (
        ("in", w_in, g_in, m_w_in, v_w_in), ("out", w_out, g_out, m_w_out, v_w_out),
        ("gate", w_gate, g_gate, m_w_gate, v_w_gate), ("up", w_up, g_up, m_w_up, v_w_up),
        ("down", w_down, g_down, m_w_down, v_w_down))]
    pack = lambda a, b, c, d, e, f: _pack_small(a, b[0], c, d, e, f, jnp.zeros((1,), F32))
    small_upd = _adamw(
        pack(ln_pre_mix, w_pool, pool_scale, ln_post_mix, ln_pre_ffn, ln_post_ffn), small_sum,
        pack(m_ln_pre_mix, m_w_pool, m_pool_scale, m_ln_post_mix, m_ln_pre_ffn, m_ln_post_ffn),
        pack(v_ln_pre_mix, v_w_pool, v_pool_scale, v_ln_post_mix, v_ln_pre_ffn, v_ln_post_ffn), "adamw_small")

    def tree(small6, big5):
        s1, spool, sscale, s2, s3, s4 = small6
        b_in, b_out, b_gate, b_up, b_down = [b[None] for b in big5]
        return [s1, b_in, spool, sscale, b_out, s2, s3, b_gate, b_up, b_down, s4]

    g_small = _unpack_small(small_sum)
    outs = [g_small[6][0], grad_x[None]]
    outs += tree(g_small[:6], [g_in, g_out, g_gate, g_up, g_down])
    for j in range(3):
        outs += tree(_unpack_small(small_upd[j])[:6], [u[j] for u in upd])
    return tuple(outs)
```

```python
import jax
import jax.numpy as jnp
from jax import lax
from jax.experimental import pallas as pl
from jax.experimental.pallas import tpu as pltpu

F32 = jnp.float32
BF16 = jnp.bfloat16

D_MODEL = 1024
POOL_W = 256
ATTN_W = 768
IN_W = 2560
D_FF = 2816
POOL_WINDOWS = (2, 4, 8, 16)
POOL_HALO = 16
DILATIONS = (1, 4, 16)
BLK = 128
LANES = 128
HEAD_DIM = 64
N_GROUPS = ATTN_W // LANES
ROPE_THETA = 10000.0
EPS = 1e-6
NEG = -1e30
N_DEV = 8
SMALL_ROWS = 24

ADAM_LR = 0.001
ADAM_B1 = 0.9
ADAM_B2 = 0.999
ADAM_EPS = 1e-08
ADAM_WD = 0.01
ADAM_STEP = 10

VMEM_LIMIT = 56 * 1024 * 1024


def _dot(a, b):
    return jnp.dot(a, b, preferred_element_type=F32)


def _dot_nt(a, b):
    return lax.dot_general(a, b, (((1,), (1,)), ((), ())), preferred_element_type=F32)


def _dot_tn(a, b):
    return lax.dot_general(a, b, (((0,), (0,)), ((), ())), preferred_element_type=F32)


def _params(n_grid):
    return pltpu.CompilerParams(dimension_semantics=("arbitrary",) * n_grid, vmem_limit_bytes=VMEM_LIMIT)


def _tok(tm, c):
    return pl.BlockSpec((tm, c), lambda i: (i, 0))


def _res(shape):
    return pl.BlockSpec(shape, lambda i: (0,) * len(shape), pipeline_mode=pl.Buffered(1))


def _acc(shape):
    return pl.BlockSpec(shape, lambda i: (0,) * len(shape))


def _rms_fwd(x, g):
    r = lax.rsqrt(jnp.mean(x * x, axis=-1, keepdims=True) + EPS)
    return x * r * g


def _rms_bwd(x, g, dy):
    r = lax.rsqrt(jnp.mean(x * x, axis=-1, keepdims=True) + EPS)
    xh = x * r
    gd = dy * g
    dx = r * (gd - xh * jnp.mean(gd * xh, axis=-1, keepdims=True))
    return dx, jnp.sum(dy * xh, axis=0, keepdims=True)


def _rope(x, c, s, sign):
    lane = lax.broadcasted_iota(jnp.int32, (x.shape[0], LANES), 1)
    first = (lane % HEAD_DIM) < (HEAD_DIM // 2)
    outs = []
    for g in range(x.shape[1] // LANES):
        xg = x[:, g * LANES:(g + 1) * LANES]
        rot = jnp.where(first, pltpu.roll(xg, LANES - HEAD_DIM // 2, 1), pltpu.roll(xg, HEAD_DIM // 2, 1))
        outs.append(xg * c + sign * (rot * s))
    return jnp.concatenate(outs, axis=1)


def _proj_fwd(x, g1, w_in_t, cos, sin, tm):
    T = x.shape[0]

    def body(x_ref, g_ref, w_ref, c_ref, s_ref, h_ref, u_ref, q_ref, k_ref, v_ref):
        h = _rms_fwd(x_ref[...], g_ref[...]).astype(BF16)
        h_ref[...] = h
        proj = _dot_nt(h, w_ref[...])
        c = c_ref[...]
        s = s_ref[...]
        u_ref[...] = proj[:, :POOL_W]
        _store_packed(q_ref, _rope(proj[:, POOL_W:POOL_W + ATTN_W], c, s, 1.0))
        _store_packed(k_ref, _rope(proj[:, POOL_W + ATTN_W:POOL_W + 2 * ATTN_W], c, s, 1.0))
        _store_packed(v_ref, proj[:, POOL_W + 2 * ATTN_W:])

    return pl.pallas_call(
        body, name="proj_fwd", grid=(T // tm,),
        in_specs=[_tok(tm, D_MODEL), _res((1, D_MODEL)), _res((IN_W, D_MODEL)), _tok(tm, LANES), _tok(tm, LANES)],
        out_specs=[_tok(tm, D_MODEL), _tok(tm, POOL_W)] + [_tok_packed(tm, ATTN_W)] * 3,
        out_shape=[jax.ShapeDtypeStruct((T, D_MODEL), BF16), jax.ShapeDtypeStruct((T, POOL_W), F32)]
        + [_packed(T, ATTN_W)] * 3,
        compiler_params=_params(1),
    )(x, g1, w_in_t, cos, sin)


def _pool_window(lane):
    return jnp.where(lane < 64, 2, jnp.where(lane < 128, 4, jnp.where(lane < 192, 8, 16)))


def _pool_select(lane, a2, a4, a8, a16):
    return jnp.where(lane < 64, a2, jnp.where(lane < 128, a4, jnp.where(lane < 192, a8, a16)))


def _pool_delta(cur, prev, i, tm):
    prev = jnp.where(i > 0, prev, 0.0)
    ext = jnp.concatenate([prev, cur], axis=0)
    s2 = ext + pltpu.roll(ext, 1, 0)
    s4 = s2 + pltpu.roll(s2, 2, 0)
    s8 = s4 + pltpu.roll(s4, 4, 0)
    s16 = s8 + pltpu.roll(s8, 8, 0)
    lane = lax.broadcasted_iota(jnp.int32, (tm, POOL_W), 1)
    row = lax.broadcasted_iota(jnp.int32, (tm, POOL_W), 0) + i * tm
    ws = _pool_select(lane, s2[POOL_HALO:], s4[POOL_HALO:], s8[POOL_HALO:], s16[POOL_HALO:])
    cnt = jnp.minimum(row + 1, _pool_window(lane)).astype(F32)
    return ws / cnt - cur


def _pool_fwd(u, wbd, scale, tm):
    T = u.shape[0]
    hb = tm // POOL_HALO

    def body(u_ref, prev_ref, w_ref, sc_ref, o_ref):
        d = _pool_delta(u_ref[...], prev_ref[...], pl.program_id(0), tm)
        o_ref[...] = (_dot(d.astype(BF16), w_ref[...]) * sc_ref[...]).astype(BF16)

    return pl.pallas_call(
        body, name="pool_fwd", grid=(T // tm,),
        in_specs=[_tok(tm, POOL_W), pl.BlockSpec((POOL_HALO, POOL_W), lambda i: (jnp.maximum(i * hb - 1, 0), 0)),
                  _res((POOL_W, POOL_W)), _res((1, POOL_W))],
        out_specs=_tok(tm, POOL_W),
        out_shape=jax.ShapeDtypeStruct((T, POOL_W), BF16),
        compiler_params=_params(1),
    )(u, u, wbd, scale)


def _attn_mask(has_prev):
    qi = lax.broadcasted_iota(jnp.int32, (BLK, 2 * BLK), 0)
    kj = lax.broadcasted_iota(jnp.int32, (BLK, 2 * BLK), 1)
    dist = qi + BLK - kj
    return (dist >= 0) & (dist <= BLK) & ((kj >= BLK) | has_prev)


def _stack_heads(x, lo):
    zero = jnp.zeros_like(x)
    return jnp.concatenate([jnp.where(lo, x, zero), jnp.where(lo, zero, x)], axis=0)


def _head_col(tile, lane, h):
    return jnp.sum(jnp.where(lane == h, tile, 0.0), axis=1, keepdims=True)


def _attn_cols(dil):
    return ATTN_W // 2 if dil >= 16 else ATTN_W


def _units_per_step(dil, backward):
    return {1: 4, 4: 2, 16: 4}[dil]


def _chunk_tokens(dil, units):
    return BLK * (units if dil == 1 else dil)


def _unit_steps(dil, units):
    return 1 if dil == 1 else dil // 2 // units


def _attn_specs(dil, nb, units):
    cw = _attn_cols(dil)
    ch = _chunk_tokens(dil, units)
    wide = lambda f: pl.BlockSpec((cw // LANES, ch // 2, LANES), f)
    cur = lambda n, j, r: (j, n, 0)
    prv = lambda n, j, r: (j, jnp.maximum(n - 1, 0), 0)
    prv_out = lambda n, j, r: (j, (n + nb - 1) % nb, 0)
    heads = pl.BlockSpec((ch, LANES), lambda n, j, r: (n, 0))
    return cw, wide(cur), wide(prv), wide(prv_out), heads


HIGH_HALF = 0xFFFF0000


def _pack(x):
    return pltpu.bitcast(x.astype(BF16), F32)


def _unpack(words):
    return pltpu.bitcast(words, BF16)


def _packed(rows, cols):
    return jax.ShapeDtypeStruct((cols // LANES, rows // 2, LANES), F32)


def _tok_packed(tm, cols):
    return pl.BlockSpec((cols // LANES, tm // 2, LANES), lambda i: (0, i, 0))


def _store_packed(ref, x):
    for g in range(x.shape[1] // LANES):
        ref[g] = _pack(x[:, g * LANES:(g + 1) * LANES])


def _load_packed(ref):
    return jnp.concatenate([_unpack(ref[g]) for g in range(ref.shape[0])], axis=1)


def _load_streams(ref, dil, r2, sl):
    if dil == 1:
        return [_unpack(ref.at[sl][pl.ds(r2 * (BLK // 2), BLK // 2), :])]
    words = lax.bitcast_convert_type(ref.at[sl][pl.ds(r2, BLK, stride=dil // 2), :], jnp.uint32)
    even = lax.bitcast_convert_type(words << 16, F32).astype(BF16)
    odd = lax.bitcast_convert_type(words & jnp.uint32(HIGH_HALF), F32).astype(BF16)
    return [even, odd]


def _load_prev_streams(prev_ref, cur_ref, dil, units, r2, sl):
    if dil > 1:
        return _load_streams(prev_ref, dil, r2, sl)
    return _load_streams(cur_ref, 1, r2 - 1, sl) if r2 > 0 else _load_streams(prev_ref, 1, units - 1, sl)


def _load_streams_f32(ref, dil, r2, sl):
    ref = ref if sl is None else ref.at[sl]
    if dil == 1:
        return [ref[pl.ds(r2 * BLK, BLK), :]]
    return [ref[pl.ds(2 * r2 + e, BLK, stride=dil), :] for e in range(2)]


def _store_streams_f32(ref, dil, r2, sl, tiles):
    ref = ref if sl is None else ref.at[sl]
    if dil == 1:
        ref[pl.ds(r2 * BLK, BLK), :] = tiles[0]
    else:
        for e, t in enumerate(tiles):
            ref[pl.ds(2 * r2 + e, BLK, stride=dil), :] = t


def _store_streams(ref, dil, r2, sl, tiles):
    if dil == 1:
        ref.at[sl][pl.ds(r2 * (BLK // 2), BLK // 2), :] = _pack(tiles[0])
    else:
        even, odd = [lax.bitcast_convert_type(t.astype(BF16).astype(F32), jnp.uint32) for t in tiles]
        words = (odd & jnp.uint32(HIGH_HALF)) | (even >> 16)
        ref.at[sl][pl.ds(r2, BLK, stride=dil // 2), :] = lax.bitcast_convert_type(words, F32)


def _attn_fwd(q, k, v, dil, others):
    T = 2 * q.shape[1]
    reps = _units_per_step(dil, False)
    nb = T // _chunk_tokens(dil, reps)
    first = not others
    cw, cur, prv, _, heads = _attn_specs(dil, nb, reps)
    ncb = ATTN_W // cw
    heads_per_step = cw // HEAD_DIM
    n_str = min(dil, 2)
    everything = None

    def body(*refs):
        q_ref, kc_ref, kp_ref, vc_ref, vp_ref = refs[:5]
        acc_ins, lse_ins = refs[5:5 + 2 * len(others):2], refs[6:6 + 2 * len(others):2]
        acc_ref, lse_ref = refs[-2:]
        j = pl.program_id(1)
        lane = lax.broadcasted_iota(jnp.int32, (BLK, LANES), 1)
        lo = lane < HEAD_DIM

        def stream_pair(r2):
            valid = _attn_mask(True if dil == 1 and r2 > 0 else pl.program_id(0) > 0)
            lse_tiles = [jnp.zeros((BLK, LANES), F32) for _ in range(n_str)]
            own = []
            for g in range(cw // LANES):
                qs, kcs, vcs = [_load_streams(r, dil, r2, g) for r in (q_ref, kc_ref, vc_ref)]
                kps, vps = [_load_prev_streams(p, c, dil, reps, r2, g) for p, c in ((kp_ref, kc_ref), (vp_ref, vc_ref))]
                pairs = []
                for e in range(n_str):
                    qg = qs[e] * 0.125
                    kcat = jnp.concatenate([kps[e], kcs[e]], axis=0)
                    vcat = jnp.concatenate([vps[e], vcs[e]], axis=0)
                    pair = None
                    for hh in range(2):
                        h = j * heads_per_step + 2 * g + hh
                        hm = lo if hh == 0 else jnp.logical_not(lo)
                        s = _dot_nt(jnp.where(hm, qg, jnp.zeros_like(qg)), kcat)
                        s = jnp.where(valid, s, NEG)
                        m = jnp.max(s, axis=1, keepdims=True)
                        p = jnp.exp(s - m)
                        den = jnp.sum(p, axis=1, keepdims=True)
                        o = _dot(p.astype(BF16), vcat) / den
                        pair = o if hh == 0 else jnp.where(lo, pair, o)
                        lse_tiles[e] = jnp.where(lane == h, m + jnp.log(den), lse_tiles[e])
                    pairs.append(pair)
                if first:
                    _store_streams(acc_ref, dil, r2, g, pairs)
                else:
                    own.append(pairs)
            if not first:
                mine = (lane >= j * heads_per_step) & (lane < (j + 1) * heads_per_step)
                theirs = [_load_streams_f32(ref, dil, r2, everything) for ref in lse_ins]
                w_theirs, w_own = [[] for _ in others], []
                for e in range(n_str):
                    parts = [t[e] for t in theirs] + [lse_tiles[e]]
                    mx = parts[0]
                    for part in parts[1:]:
                        mx = jnp.maximum(mx, part)
                    total = mx + jnp.log(sum(jnp.exp(part - mx) for part in parts))
                    for i, t in enumerate(theirs):
                        w_theirs[i].append(jnp.exp(t[e] - total))
                    w_own.append(jnp.exp(lse_tiles[e] - total))
                    lse_tiles[e] = jnp.where(mine, total, 0.0)
                for g in range(cw // LANES):
                    h0 = j * heads_per_step + 2 * g
                    spread = lambda w: jnp.where(lo, _head_col(w, lane, h0), _head_col(w, lane, h0 + 1))
                    olds = [_load_streams(ref, dil, r2, g) for ref in acc_ins]
                    _store_streams(acc_ref, dil, r2, g, [
                        sum(olds[i][e].astype(F32) * spread(w_theirs[i][e]) for i in range(len(others)))
                        + own[g][e] * spread(w_own[e]) for e in range(n_str)])
            if ncb == 1:
                _store_streams_f32(lse_ref, dil, r2, everything, lse_tiles)
            else:
                @pl.when(j == 0)
                def _():
                    _store_streams_f32(lse_ref, dil, r2, everything, lse_tiles)

                @pl.when(j > 0)
                def _():
                    before = _load_streams_f32(lse_ref, dil, r2, everything)
                    _store_streams_f32(lse_ref, dil, r2, everything, [a + b for a, b in zip(before, lse_tiles)])

        for rep in range(reps):
            stream_pair(rep if dil == 1 else pl.program_id(2) * reps + rep)

    ins = [q, k, k, v, v]
    in_specs = [cur, cur, prv, cur, prv]
    for acc, lse in others:
        ins += [acc, lse]
        in_specs += [cur, heads]
    return pl.pallas_call(
        body, name=f"attn_fwd_d{dil}", grid=(nb, ncb, _unit_steps(dil, reps)),
        in_specs=in_specs, out_specs=[cur, heads],
        out_shape=[_packed(T, ATTN_W), jax.ShapeDtypeStruct((T, LANES), F32)],
        compiler_params=_params(3),
    )(*ins)


def _mix_fwd(pool, attn, x, w_out, g2, g3, tm):
    T = x.shape[0]

    def body(p_ref, a_ref, x_ref, w_ref, g2_ref, g3_ref, cat_ref, mix_ref, x2_ref, h2_ref):
        p = p_ref[...]
        a = _load_packed(a_ref)
        cat_ref[...] = jnp.concatenate([p, a], axis=1)
        mix = _dot(p, w_ref[:POOL_W, :]) + _dot(a, w_ref[POOL_W:, :])
        mix_ref[...] = mix
        x2 = x_ref[...] + _rms_fwd(mix, g2_ref[...])
        x2_ref[...] = x2
        h2_ref[...] = _rms_fwd(x2, g3_ref[...]).astype(BF16)

    return pl.pallas_call(
        body, name="mix_fwd", grid=(T // tm,),
        in_specs=[_tok(tm, POOL_W), _tok_packed(tm, ATTN_W), _tok(tm, D_MODEL), _res((D_MODEL, D_MODEL)),
                  _res((1, D_MODEL)), _res((1, D_MODEL))],
        out_specs=[_tok(tm, D_MODEL)] * 4,
        out_shape=[jax.ShapeDtypeStruct((T, D_MODEL), BF16), jax.ShapeDtypeStruct((T, D_MODEL), F32),
                   jax.ShapeDtypeStruct((T, D_MODEL), F32), jax.ShapeDtypeStruct((T, D_MODEL), BF16)],
        compiler_params=_params(1),
    )(pool, attn, x, w_out, g2, g3)


def _ffn_up(h2, wg_t, wu_t, tm):
    T = h2.shape[0]

    def body(h_ref, wg_ref, wu_ref, dg_ref, du_ref, a_ref):
        h = h_ref[...]
        gate = _dot_nt(h, wg_ref[...])
        up = _dot_nt(h, wu_ref[...])
        sg = 1.0 / (1.0 + jnp.exp(-gate))
        silu = gate * sg
        a_ref[...] = (silu * up).astype(BF16)
        dg_ref[...] = (up * (sg * (1.0 + gate * (1.0 - sg)))).astype(BF16)
        du_ref[...] = silu.astype(BF16)

    return pl.pallas_call(
        body, name="ffn_up", grid=(T // tm,),
        in_specs=[_tok(tm, D_MODEL), _res((D_FF, D_MODEL)), _res((D_FF, D_MODEL))],
        out_specs=[_tok(tm, D_FF)] * 3,
        out_shape=[jax.ShapeDtypeStruct((T, D_FF), BF16)] * 3,
        compiler_params=_params(1),
    )(h2, wg_t, wu_t)


def _ffn_down_loss(act, w_down, x2, g4, tgt, tm):
    T = act.shape[0]

    def body(a_ref, w_ref, x2_ref, g_ref, t_ref, df_ref, dy_ref, dg_ref, loss_ref):
        i = pl.program_id(0)

        @pl.when(i == 0)
        def _():
            dg_ref[...] = jnp.zeros_like(dg_ref)
            loss_ref[...] = jnp.zeros_like(loss_ref)

        f = _dot(a_ref[...], w_ref[...])
        g = g_ref[...]
        err = x2_ref[...] + _rms_fwd(f, g) - t_ref[...]
        loss_ref[...] += 0.5 * jnp.sum(jnp.mean(err * err, axis=-1, keepdims=True), axis=0, keepdims=True)
        dy = err * (1.0 / D_MODEL)
        dy_ref[...] = dy
        df, dg = _rms_bwd(f, g, dy)
        dg_ref[...] += dg
        df_ref[...] = df.astype(BF16)

    return pl.pallas_call(
        body, name="ffn_down_loss", grid=(T // tm,),
        in_specs=[_tok(tm, D_FF), _res((D_FF, D_MODEL)), _tok(tm, D_MODEL), _res((1, D_MODEL)), _tok(tm, D_MODEL)],
        out_specs=[_tok(tm, D_MODEL), _tok(tm, D_MODEL), _acc((1, D_MODEL)), _acc((1, 1))],
        out_shape=[jax.ShapeDtypeStruct((T, D_MODEL), BF16), jax.ShapeDtypeStruct((T, D_MODEL), F32),
                   jax.ShapeDtypeStruct((1, D_MODEL), F32), jax.ShapeDtypeStruct((1, 1), F32)],
        compiler_params=_params(1),
    )(act, w_down, x2, g4, tgt)


def _ffn_act_bwd(df, w_down, act_dgate, act_dup, tm):
    T = df.shape[0]

    def body(df_ref, w_ref, ag_ref, au_ref, dg_ref, du_ref):
        dact = _dot_nt(df_ref[...], w_ref[...])
        dg_ref[...] = (dact * ag_ref[...].astype(F32)).astype(BF16)
        du_ref[...] = (dact * au_ref[...].astype(F32)).astype(BF16)

    return pl.pallas_call(
        body, name="ffn_act_bwd", grid=(T // tm,),
        in_specs=[_tok(tm, D_MODEL), _res((D_FF, D_MODEL)), _tok(tm, D_FF), _tok(tm, D_FF)],
        out_specs=[_tok(tm, D_FF)] * 2,
        out_shape=[jax.ShapeDtypeStruct((T, D_FF), BF16)] * 2,
        compiler_params=_params(1),
    )(df, w_down, act_dgate, act_dup)


def _ffn_in_bwd(dgate, dup, wg_t, wu_t, x2, mix, dy, g3, g2, tm):
    T = x2.shape[0]

    def body(dg_ref, du_ref, wg_ref, wu_ref, x2_ref, mix_ref, dy_ref, g3_ref, g2_ref,
             dx2_ref, dmix_ref, dg3_ref, dg2_ref):
        @pl.when(pl.program_id(0) == 0)
        def _():
            dg3_ref[...] = jnp.zeros_like(dg3_ref)
            dg2_ref[...] = jnp.zeros_like(dg2_ref)

        dh2 = _dot(dg_ref[...], wg_ref[...]) + _dot(du_ref[...], wu_ref[...])
        dn, dg3 = _rms_bwd(x2_ref[...], g3_ref[...], dh2)
        dx2 = dy_ref[...] + dn
        dx2_ref[...] = dx2
        dg3_ref[...] += dg3
        dmix, dg2 = _rms_bwd(mix_ref[...], g2_ref[...], dx2)
        dg2_ref[...] += dg2
        dmix_ref[...] = dmix.astype(BF16)

    return pl.pallas_call(
        body, name="ffn_in_bwd", grid=(T // tm,),
        in_specs=[_tok(tm, D_FF), _tok(tm, D_FF), _res((D_FF, D_MODEL)), _res((D_FF, D_MODEL)),
                  _tok(tm, D_MODEL), _tok(tm, D_MODEL), _tok(tm, D_MODEL), _res((1, D_MODEL)), _res((1, D_MODEL))],
        out_specs=[_tok(tm, D_MODEL), _tok(tm, D_MODEL), _acc((1, D_MODEL)), _acc((1, D_MODEL))],
        out_shape=[jax.ShapeDtypeStruct((T, D_MODEL), F32), jax.ShapeDtypeStruct((T, D_MODEL), BF16),
                   jax.ShapeDtypeStruct((1, D_MODEL), F32), jax.ShapeDtypeStruct((1, D_MODEL), F32)],
        compiler_params=_params(1),
    )(dgate, dup, wg_t, wu_t, x2, mix, dy, g3, g2)


def _mix_bwd(dmix, w_out, attn, tm):
    T = dmix.shape[0]

    def body(d_ref, w_ref, a_ref, dp_ref, da_ref, ds_ref):
        dcat = _dot_nt(d_ref[...], w_ref[...])
        dp_ref[...] = dcat[:, :POOL_W].astype(BF16)
        dattn = dcat[:, POOL_W:].astype(BF16)
        _store_packed(da_ref, dattn)
        prod = dattn.astype(F32) * _load_packed(a_ref).astype(F32)
        lane = lax.broadcasted_iota(jnp.int32, (tm, LANES), 1)
        lo = lane < HEAD_DIM
        dsum = jnp.zeros((tm, LANES), F32)
        for g in range(N_GROUPS):
            pg = prod[:, g * LANES:(g + 1) * LANES]
            dsum = jnp.where(lane == 2 * g, jnp.sum(jnp.where(lo, pg, 0.0), axis=1, keepdims=True), dsum)
            dsum = jnp.where(lane == 2 * g + 1, jnp.sum(jnp.where(lo, 0.0, pg), axis=1, keepdims=True), dsum)
        ds_ref[...] = dsum

    return pl.pallas_call(
        body, name="mix_bwd", grid=(T // tm,),
        in_specs=[_tok(tm, D_MODEL), _res((D_MODEL, D_MODEL)), _tok_packed(tm, ATTN_W)],
        out_specs=[_tok(tm, POOL_W), _tok_packed(tm, ATTN_W), _tok(tm, LANES)],
        out_shape=[jax.ShapeDtypeStruct((T, POOL_W), BF16), _packed(T, ATTN_W), jax.ShapeDtypeStruct((T, LANES), F32)],
        compiler_params=_params(1),
    )(dmix, w_out, attn)


def _attn_bwd(q, k, v, dout, dsum, lse, dil):
    T = 2 * q.shape[1]
    reps = _units_per_step(dil, True)
    nb = T // _chunk_tokens(dil, reps)
    cw, cur, prv, prv_out, heads = _attn_specs(dil, nb, reps)
    ncb = ATTN_W // cw
    heads_per_step = cw // HEAD_DIM
    n_str = min(dil, 2)

    def body(q_ref, kc_ref, kp_ref, vc_ref, vp_ref, do_ref, dsum_ref, lse_ref,
             dq_ref, dkc_ref, dkp_ref, dvc_ref, dvp_ref):
        j = pl.program_id(1)
        lane = lax.broadcasted_iota(jnp.int32, (BLK, LANES), 1)
        lo = lane < HEAD_DIM
        unit = lambda rep: rep if dil == 1 else pl.program_id(2) * reps + rep
        stats = [[_load_streams_f32(ref, dil, unit(rep), None) for ref in (lse_ref, dsum_ref)] for rep in range(reps)]

        def unit_grads(rep, g):
            r2 = unit(rep)
            valid = _attn_mask(True if dil == 1 and r2 > 0 else pl.program_id(0) > 0)
            valid2 = jnp.concatenate([valid, valid], axis=0)
            lse_tiles, dsum_tiles = stats[rep]
            qs, kcs, vcs, dos = [_load_streams(r, dil, r2, g) for r in (q_ref, kc_ref, vc_ref, do_ref)]
            kps, vps = [_load_prev_streams(p, c, dil, reps, r2, g) for p, c in ((kp_ref, kc_ref), (vp_ref, vc_ref))]
            dqs, dks, dvs = [], [], []
            for e in range(n_str):
                qg = qs[e] * 0.125
                dog = dos[e]
                kcat = jnp.concatenate([kps[e], kcs[e]], axis=0)
                vcat = jnp.concatenate([vps[e], vcs[e]], axis=0)
                h0 = j * heads_per_step + 2 * g
                q2 = _stack_heads(qg, lo)
                do2 = _stack_heads(dog, lo)
                both = lambda tile: jnp.concatenate([_head_col(tile, lane, h0), _head_col(tile, lane, h0 + 1)], axis=0)
                lse2, dsum2 = both(lse_tiles[e]), both(dsum_tiles[e])
                p = jnp.exp(jnp.where(valid2, _dot_nt(q2, kcat), NEG) - lse2)
                ds = (p * (_dot_nt(do2, vcat) - dsum2)).astype(BF16)
                dvs.append(_dot_tn(p.astype(BF16), do2))
                dks.append(_dot_tn(ds, q2))
                dq2 = _dot(ds, kcat) * 0.125
                dqs.append(jnp.where(lo, dq2[:BLK], dq2[BLK:]))
            return dqs, dks, dvs

        for g in range(cw // LANES):
            if dil > 1:
                for rep in range(reps):
                    r2 = unit(rep)
                    dqs, dks, dvs = unit_grads(rep, g)
                    _store_streams(dq_ref, dil, r2, g, dqs)
                    _store_streams(dkp_ref, dil, r2, g, [t[:BLK] for t in dks])
                    _store_streams(dkc_ref, dil, r2, g, [t[BLK:] for t in dks])
                    _store_streams(dvp_ref, dil, r2, g, [t[:BLK] for t in dvs])
                    _store_streams(dvc_ref, dil, r2, g, [t[BLK:] for t in dvs])
            else:
                blocks = [unit_grads(b, g) for b in range(reps)]
                for b, (dqs, dks, dvs) in enumerate(blocks):
                    _store_streams(dq_ref, 1, b, g, dqs)
                    for cur_ref, prev_ref, which in ((dkc_ref, dkp_ref, 1), (dvc_ref, dvp_ref, 2)):
                        own = blocks[b][which][0][BLK:]
                        if b + 1 < reps:
                            own = own + blocks[b + 1][which][0][:BLK]
                        _store_streams(cur_ref, 1, b, g, [own])
                        edge = blocks[0][which][0][:BLK] if b == reps - 1 else jnp.zeros((BLK, LANES), F32)
                        _store_streams(prev_ref, 1, b, g, [edge])

    return pl.pallas_call(
        body, name=f"attn_bwd_d{dil}", grid=(nb, ncb, _unit_steps(dil, reps)),
        in_specs=[cur, cur, prv, cur, prv, cur, heads, heads],
        out_specs=[cur, cur, prv_out, cur, prv_out],
        out_shape=[_packed(T, ATTN_W)] * 5,
        compiler_params=_params(3),
    )(q, k, k, v, v, dout, dsum, lse)


def _pool_bwd(u, dy, wbd, scale, tm):
    T = u.shape[0]
    nt = T // tm
    hb = tm // POOL_HALO

    def body(u_ref, prev_ref, dy_ref, next_ref, w_ref, sc_ref, du_ref, dw_ref, dsc_ref):
        i = pl.program_id(0)

        @pl.when(i == 0)
        def _():
            dw_ref[...] = jnp.zeros_like(dw_ref)
            dsc_ref[...] = jnp.zeros_like(dsc_ref)

        w = w_ref[...]
        sc = sc_ref[...]
        d = _pool_delta(u_ref[...], prev_ref[...], i, tm).astype(BF16)
        dyc = dy_ref[...].astype(F32)
        dsc_ref[...] += jnp.sum(dyc * _dot(d, w), axis=0, keepdims=True)
        nxt = jnp.where(i < nt - 1, next_ref[...].astype(F32), 0.0)
        dypre = (jnp.concatenate([dyc, nxt], axis=0) * sc).astype(BF16)
        dw_ref[...] += _dot_tn(d, dypre[:tm])
        dd = _dot_nt(dypre, w)
        n = tm + POOL_HALO
        lane = lax.broadcasted_iota(jnp.int32, (n, POOL_W), 1)
        row = lax.broadcasted_iota(jnp.int32, (n, POOL_W), 0) + i * tm
        gx = dd / jnp.minimum(row + 1, _pool_window(lane)).astype(F32)
        a2 = gx + pltpu.roll(gx, n - 1, 0)
        a4 = a2 + pltpu.roll(a2, n - 2, 0)
        a8 = a4 + pltpu.roll(a4, n - 4, 0)
        a16 = a8 + pltpu.roll(a8, n - 8, 0)
        fs = _pool_select(lane[:tm], a2[:tm], a4[:tm], a8[:tm], a16[:tm])
        du_ref[...] = (fs - dd[:tm]).astype(BF16)

    return pl.pallas_call(
        body, name="pool_bwd", grid=(nt,),
        in_specs=[_tok(tm, POOL_W), pl.BlockSpec((POOL_HALO, POOL_W), lambda i: (jnp.maximum(i * hb - 1, 0), 0)),
                  _tok(tm, POOL_W), pl.BlockSpec((POOL_HALO, POOL_W), lambda i: (jnp.minimum((i + 1) * hb, nt * hb - 1), 0)),
                  _res((POOL_W, POOL_W)), _res((1, POOL_W))],
        out_specs=[_tok(tm, POOL_W), _acc((POOL_W, POOL_W)), _acc((1, POOL_W))],
        out_shape=[jax.ShapeDtypeStruct((T, POOL_W), BF16), jax.ShapeDtypeStruct((POOL_W, POOL_W), F32),
                   jax.ShapeDtypeStruct((1, POOL_W), F32)],
        compiler_params=_params(1),
    )(u, u, dy, dy, wbd, scale)


def _dproj_combine(du, dqs, dkcs, dkps, dvcs, dvps, cos, sin, tm):
    T = du.shape[0]
    n_cfg = len(dqs)

    def body(*refs):
        du_ref = refs[0]
        groups = [refs[1 + j * n_cfg:1 + (j + 1) * n_cfg] for j in range(5)]
        c_ref, s_ref, out_ref = refs[1 + 5 * n_cfg:]
        tot = lambda rs: sum(_load_packed(r).astype(F32) for r in rs)
        c = c_ref[...]
        s = s_ref[...]
        dq = _rope(tot(groups[0]), c, s, -1.0)
        dk = _rope(tot(groups[1]) + tot(groups[2]), c, s, -1.0)
        dv = tot(groups[3]) + tot(groups[4])
        out_ref[...] = jnp.concatenate([du_ref[...], dq.astype(BF16), dk.astype(BF16), dv.astype(BF16)], axis=1)

    return pl.pallas_call(
        body, name="dproj_combine", grid=(T // tm,),
        in_specs=[_tok(tm, POOL_W)] + [_tok_packed(tm, ATTN_W)] * (5 * n_cfg) + [_tok(tm, LANES)] * 2,
        out_specs=_tok(tm, IN_W),
        out_shape=jax.ShapeDtypeStruct((T, IN_W), BF16),
        compiler_params=_params(1),
    )(du, *dqs, *dkcs, *dkps, *dvcs, *dvps, cos, sin)


def _proj_bwd(dproj, w_in_t, x, dx2, g1, tm):
    T = x.shape[0]

    def body(d_ref, w_ref, x_ref, r_ref, g_ref, dx_ref, dg_ref):
        @pl.when(pl.program_id(0) == 0)
        def _():
            dg_ref[...] = jnp.zeros_like(dg_ref)

        dn, dg = _rms_bwd(x_ref[...], g_ref[...], _dot(d_ref[...], w_ref[...]))
        dg_ref[...] += dg
        dx_ref[...] = r_ref[...] + dn

    return pl.pallas_call(
        body, name="proj_bwd", grid=(T // tm,),
        in_specs=[_tok(tm, IN_W), _res((IN_W, D_MODEL)), _tok(tm, D_MODEL), _tok(tm, D_MODEL), _res((1, D_MODEL))],
        out_specs=[_tok(tm, D_MODEL), _acc((1, D_MODEL))],
        out_shape=[jax.ShapeDtypeStruct((T, D_MODEL), F32), jax.ShapeDtypeStruct((1, D_MODEL), F32)],
        compiler_params=_params(1),
    )(dproj, w_in_t, x, dx2, g1)


def _wgrad(a, b, name, tile_m, tk):
    T, M = a.shape
    N = b.shape[1]
    nk = T // tk

    def body(a_ref, b_ref, o_ref, acc_ref):
        kk = pl.program_id(1)

        @pl.when(kk == 0)
        def _():
            acc_ref[...] = jnp.zeros_like(acc_ref)

        acc_ref[...] += _dot_tn(a_ref[...], b_ref[...])

        @pl.when(kk == nk - 1)
        def _():
            o_ref[...] = acc_ref[...].astype(BF16)

    return pl.pallas_call(
        body, name=name, grid=(M // tile_m, nk),
        in_specs=[pl.BlockSpec((tk, tile_m), lambda j, kk: (kk, j)), pl.BlockSpec((tk, N), lambda j, kk: (kk, 0))],
        out_specs=pl.BlockSpec((tile_m, N), lambda j, kk: (j, 0)),
        out_shape=jax.ShapeDtypeStruct((M, N), BF16),
        scratch_shapes=[pltpu.VMEM((tile_m, N), F32)],
        compiler_params=_params(2),
    )(a, b)


def _exchange(arrs, scatter, name):
    n = len(arrs)
    out_shapes = [jax.ShapeDtypeStruct((N_DEV,) + (a.shape[1:] if sc else a.shape), a.dtype)
                  for a, sc in zip(arrs, scatter)]

    def body(*refs):
        ins, outs = refs[:n], refs[n:2 * n]
        send_sems, recv_sems, loc_sems = refs[2 * n:]
        x, y, c = lax.axis_index("x"), lax.axis_index("y"), lax.axis_index("c")
        me = 4 * x + 2 * y + c
        local, sends, recvs = [], [], []
        for i in range(n):
            own = ins[i].at[me] if scatter[i] else ins[i]
            loc = pltpu.make_async_copy(own, outs[i].at[me], loc_sems.at[i])
            loc.start()
            local.append(loc)
            for kbits in range(1, N_DEV):
                px = 1 - x if kbits & 4 else x
                py = 1 - y if kbits & 2 else y
                pc = 1 - c if kbits & 1 else c
                pid = 4 * px + 2 * py + pc
                src = ins[i].at[pid] if scatter[i] else ins[i]
                cp = pltpu.make_async_remote_copy(
                    src_ref=src, dst_ref=outs[i].at[me],
                    send_sem=send_sems.at[i, kbits - 1], recv_sem=recv_sems.at[i, kbits - 1],
                    device_id=(px, py, pc), device_id_type=pl.DeviceIdType.MESH)
                cp.start()
                sends.append(cp)
                recvs.append(pltpu.make_async_remote_copy(
                    src_ref=src, dst_ref=outs[i].at[pid],
                    send_sem=send_sems.at[i, kbits - 1], recv_sem=recv_sems.at[i, kbits - 1],
                    device_id=(px, py, pc), device_id_type=pl.DeviceIdType.MESH))
        for cp in recvs:
            cp.wait_recv()
        for cp in sends:
            cp.wait_send()
        for cp in local:
            cp.wait()

    hbm = pl.BlockSpec(memory_space=pl.ANY)
    return pl.pallas_call(
        body, name=name, in_specs=[hbm] * n, out_specs=[hbm] * n, out_shape=out_shapes,
        scratch_shapes=[pltpu.SemaphoreType.DMA((n, N_DEV - 1)), pltpu.SemaphoreType.DMA((n, N_DEV - 1)),
                        pltpu.SemaphoreType.DMA((n,))],
    )(*arrs)


def _gather_two_level(arr, name):
    def body(x_ref, out_ref, send_sems, recv_sems, local_sem):
        x, y, c = lax.axis_index("x"), lax.axis_index("y"), lax.axis_index("c")
        me, sibling = (x, y, c), (x, y, 1 - c)
        chips = [(1 - x, y), (x, 1 - y), (1 - x, 1 - y)]
        slot = lambda px, py, pc: out_ref.at[4 * px + 2 * py + pc]

        def copy(k, block, to, src=None):
            return pltpu.make_async_remote_copy(
                src_ref=slot(*block) if src is None else src, dst_ref=slot(*block),
                send_sem=send_sems.at[k], recv_sem=recv_sems.at[k],
                device_id=to, device_id_type=pl.DeviceIdType.MESH)

        mine = pltpu.make_async_copy(x_ref, slot(*me), local_sem)
        mine.start()
        first = [copy(0, me, sibling, src=x_ref)]
        first += [copy(1 + i, me, (*chip, c), src=x_ref) for i, chip in enumerate(chips)]
        for cp in first:
            cp.start()
        passed = [copy(4 + i, (*chip, c), sibling) for i, chip in enumerate(chips)]
        for i, chip in enumerate(chips):
            copy(1 + i, (*chip, c), me).wait_recv()
            passed[i].start()
        copy(0, sibling, me).wait_recv()
        for i, chip in enumerate(chips):
            copy(4 + i, (*chip, 1 - c), me).wait_recv()
        for cp in first + passed:
            cp.wait_send()
        mine.wait()

    hbm = pl.BlockSpec(memory_space=pl.ANY)
    return pl.pallas_call(
        body, name=name, in_specs=[hbm], out_specs=hbm,
        out_shape=jax.ShapeDtypeStruct((N_DEV,) + arr.shape, arr.dtype),
        scratch_shapes=[pltpu.SemaphoreType.DMA((N_DEV - 1,)), pltpu.SemaphoreType.DMA((N_DEV - 1,)),
                        pltpu.SemaphoreType.DMA],
    )(arr)


def _peers(x, y, c):
    for kbits in range(1, N_DEV):
        px = 1 - x if kbits & 4 else x
        py = 1 - y if kbits & 2 else y
        pc = 1 - c if kbits & 1 else c
        yield kbits - 1, (px, py, pc), 4 * px + 2 * py + pc


def _peer_copies(ins, lands, scatter, send_sems, recv_sems, incoming):
    x, y, c = lax.axis_index("x"), lax.axis_index("y"), lax.axis_index("c")
    me = 4 * x + 2 * y + c
    copies = []
    for i in range(len(ins)):
        for k, peer, pid in _peers(x, y, c):
            slot = i * (N_DEV - 1) + k
            copies.append(pltpu.make_async_remote_copy(
                src_ref=ins[i].at[pid] if scatter[i] else ins[i], dst_ref=lands[i].at[pid if incoming else me],
                send_sem=send_sems.at[slot], recv_sem=recv_sems.at[slot],
                device_id=peer, device_id_type=pl.DeviceIdType.MESH))
    return copies


_HBM = pl.BlockSpec(memory_space=pltpu.HBM)
_SEM = pl.BlockSpec(memory_space=pltpu.SEMAPHORE)
_DATAFLOW = pltpu.SideEffectType.DATAFLOW_SIDE_EFFECTING


def _exchange_start(arrs, scatter, after, name):
    n = len(arrs)
    lands = [lax.empty((N_DEV,) + (a.shape[1:] if sc else a.shape), a.dtype) for a, sc in zip(arrs, scatter)]

    def body(*refs):
        ins, lz = refs[:n], refs[n:2 * n]
        send_sems, recv_sems = refs[2 * n + 1:2 * n + 3]
        token = refs[-1]
        for cp in _peer_copies(ins, lz, scatter, send_sems, recv_sems, False):
            cp.start()
        token[...] = jnp.zeros_like(token)

    sem_shape = pltpu.SemaphoreType.DMA((n * (N_DEV - 1),))
    outs = pl.pallas_call(
        body, name=name,
        out_shape=(sem_shape, sem_shape, *[pltpu.HBM(a.shape, a.dtype) for a in arrs + lands],
                   jax.ShapeDtypeStruct((8, LANES), F32)),
        in_specs=[_HBM] * (2 * n) + [pl.BlockSpec(memory_space=pl.ANY)],
        out_specs=(_SEM, _SEM, *[_HBM] * (2 * n), pl.BlockSpec(memory_space=pltpu.VMEM)),
        input_output_aliases={i: 2 + i for i in range(2 * n)},
        compiler_params=pltpu.CompilerParams(has_side_effects=_DATAFLOW),
    )(*[pltpu.with_memory_space_constraint(a, pltpu.HBM) for a in arrs + lands], after)
    return outs[0], outs[1], list(outs[2:2 + n]), list(outs[2 + n:2 + 2 * n]), outs[-1]


def _exchange_wait(handle, scatter, after, name):
    send_sems, recv_sems, srcs, lands, _ = handle
    n = len(srcs)

    def body(*refs):
        ins, lz = refs[:n], refs[n:2 * n]
        for cp in _peer_copies(ins, lz, scatter, refs[2 * n], refs[2 * n + 1], False):
            cp.wait_send()
        for cp in _peer_copies(ins, lz, scatter, refs[2 * n], refs[2 * n + 1], True):
            cp.wait_recv()

    outs = pl.pallas_call(
        body, name=name,
        out_shape=[pltpu.HBM(a.shape, a.dtype) for a in srcs + lands],
        in_specs=[_HBM] * (2 * n) + [_SEM, _SEM, pl.BlockSpec(memory_space=pl.ANY)],
        out_specs=[_HBM] * (2 * n),
        input_output_aliases={i: i for i in range(2 * n)},
        compiler_params=pltpu.CompilerParams(has_side_effects=_DATAFLOW),
    )(*srcs, *lands, send_sems, recv_sems, after)
    return list(outs[:n]), list(outs[n:])


def _fill_own(lands, srcs, scatter):
    me = 4 * lax.axis_index("x") + 2 * lax.axis_index("y") + lax.axis_index("c")
    own = [lax.dynamic_index_in_dim(s, me, 0, keepdims=False) if sc else s for s, sc in zip(srcs, scatter)]
    return [lax.dynamic_update_index_in_dim(land, o, me, 0) for land, o in zip(lands, own)]


def _slot_sum(parts, name, tr):
    _, R, C = parts.shape

    def body(p_ref, o_ref):
        acc = p_ref[0].astype(F32)
        for s in range(1, N_DEV):
            acc = acc + p_ref[s].astype(F32)
        o_ref[...] = acc

    return pl.pallas_call(
        body, name=name, grid=(R // tr,),
        in_specs=[pl.BlockSpec((N_DEV, tr, C), lambda i: (0, i, 0))],
        out_specs=pl.BlockSpec((tr, C), lambda i: (i, 0)),
        out_shape=jax.ShapeDtypeStruct((R, C), F32),
        compiler_params=_params(1),
    )(parts)


def _adamw(w, g, m, v, name):
    def body(w_ref, g_ref, m_ref, v_ref, d_ref, nm_ref, nv_ref):
        g = g_ref[...]
        nm = ADAM_B1 * m_ref[...] + (1.0 - ADAM_B1) * g
        nv = ADAM_B2 * v_ref[...] + (1.0 - ADAM_B2) * jnp.square(g)
        m_hat = nm / (1.0 - ADAM_B1 ** ADAM_STEP)
        v_hat = nv / (1.0 - ADAM_B2 ** ADAM_STEP)
        d_ref[...] = -ADAM_LR * (m_hat / (jnp.sqrt(v_hat) + ADAM_EPS) + ADAM_WD * w_ref[...])
        nm_ref[...] = nm
        nv_ref[...] = nv

    return pl.pallas_call(
        body, name=name, out_shape=[jax.ShapeDtypeStruct(w.shape, F32)] * 3,
        compiler_params=pltpu.CompilerParams(vmem_limit_bytes=VMEM_LIMIT),
    )(w, g, m, v)


def _rope_tables(T):
    half = HEAD_DIM // 2
    freqs = ROPE_THETA ** (-jnp.arange(half, dtype=F32) * (2.0 / HEAD_DIM))
    ang = jnp.arange(T).astype(F32)[:, None] * jnp.tile(freqs, LANES // half)[None, :]
    sign = jnp.tile(jnp.concatenate([-jnp.ones((half,), F32), jnp.ones((half,), F32)]), LANES // HEAD_DIM)
    return jnp.cos(ang), jnp.sin(ang) * sign[None, :]


def _block_diag(w_pool):
    wbd = jnp.zeros((POOL_W, POOL_W), F32)
    g = POOL_W // len(POOL_WINDOWS)
    for i in range(len(POOL_WINDOWS)):
        wbd = wbd.at[i * g:(i + 1) * g, i * g:(i + 1) * g].set(w_pool[i])
    return wbd


def _pack_small(g1, w_pool, pool_scale, g2, g3, g4, extra):
    pad = lambda a: jnp.pad(a.reshape(1, -1), ((0, 0), (0, D_MODEL - a.size)))
    rows = [g1.reshape(1, -1), g2.reshape(1, -1), g3.reshape(1, -1), g4.reshape(1, -1),
            w_pool.reshape(-1, D_MODEL), pad(pool_scale), pad(extra)]
    buf = jnp.concatenate(rows, axis=0)
    return jnp.pad(buf, ((0, SMALL_ROWS - buf.shape[0]), (0, 0)))


def _unpack_small(buf):
    n_pool = len(POOL_WINDOWS) * (POOL_W // len(POOL_WINDOWS)) ** 2 // D_MODEL
    g = POOL_W // len(POOL_WINDOWS)
    return (buf[0:1], buf[4:4 + n_pool].reshape(1, len(POOL_WINDOWS), g, g), buf[4 + n_pool:5 + n_pool, :POOL_W],
            buf[1:2], buf[2:3], buf[3:4], buf[5 + n_pool])


class _LocalStep:
    def __init__(self, x, tgt, g1, w_pool, pool_scale, g2, g3, g4):
        self.x, self.tgt, self.pool_scale = x, tgt, pool_scale
        self.g1, self.g2, self.g3, self.g4 = g1, g2, g3, g4
        self.cos, self.sin = _rope_tables(x.shape[0])
        self.wbd = _block_diag(w_pool).astype(BF16)

    def mixer_fwd(self, w_in_t, token):
        self.w_in_t = w_in_t
        self.h1, self.u, self.q, self.k, self.v = _proj_fwd(
            self.x, self.g1 + token[0, 0], w_in_t, self.cos, self.sin, 1024)
        self.pool = _pool_fwd(self.u, self.wbd, self.pool_scale, 1024)
        alone = [_attn_fwd(self.q, self.k, self.v, dil, []) for dil in DILATIONS[:-1]]
        self.attn, self.lse = _attn_fwd(self.q, self.k, self.v, DILATIONS[-1], alone)
        return self.attn

    def ffn_fwd_bwd(self, w_out, wg_t, wu_t, w_down):
        self.w_out, self.wg_t, self.wu_t = w_out, wg_t, wu_t
        self.cat, self.mix, self.x2, h2 = _mix_fwd(self.pool, self.attn, self.x, w_out, self.g2, self.g3, 1024)
        act_dgate, act_dup, act = _ffn_up(h2, wg_t, wu_t, 256)
        df, self.dy, self.dg4, self.loss = _ffn_down_loss(act, w_down, self.x2, self.g4, self.tgt, 512)
        self.dgate, self.dup = _ffn_act_bwd(df, w_down, act_dgate, act_dup, 512)
        return (_wgrad(self.dgate, h2, "wgrad_gate", D_FF // 2, 1024), _wgrad(self.dup, h2, "wgrad_up", D_FF // 2, 1024),
                _wgrad(act, df, "wgrad_down", D_FF // 2, 1024))

    def mixer_bwd(self, token):
        self.dx2, dmix, self.dg3, self.dg2 = _ffn_in_bwd(
            self.dgate, self.dup, self.wg_t, self.wu_t, self.x2, self.mix, self.dy, self.g3 + token[0, 0], self.g2, 512)
        dpool, dattn, dsum = _mix_bwd(dmix, self.w_out, self.attn, 1024)
        parts = [_attn_bwd(self.q, self.k, self.v, dattn, dsum, self.lse, dil) for dil in DILATIONS]
        du, dwbd, self.dscale = _pool_bwd(self.u, dpool, self.wbd, self.pool_scale, 1024)
        g = POOL_W // len(POOL_WINDOWS)
        self.dw_pool = jnp.stack([dwbd[i * g:(i + 1) * g, i * g:(i + 1) * g] for i in range(len(POOL_WINDOWS))])
        self.dproj = _dproj_combine(du, *[[p[j] for p in parts] for j in range(5)], self.cos, self.sin, 512)
        return _wgrad(self.dproj, self.h1, "wgrad_in", IN_W // 2, 1024), _wgrad(self.cat, dmix, "wgrad_out", D_MODEL, 1024)

    def input_bwd(self, token):
        grad_x, dg1 = _proj_bwd(self.dproj, self.w_in_t, self.x, self.dx2, self.g1 + token[0, 0], 1024)
        return self.loss, grad_x, (dg1, self.dw_pool, self.dscale, self.dg2, self.dg3, self.dg4)


def _local_step(x, tgt, g1, w_pool, pool_scale, g2, g3, g4, w_in_t, w_out, wg_t, wu_t, w_down):
    zero = jnp.zeros((8, LANES), F32)
    step = _LocalStep(x, tgt, g1, w_pool, pool_scale, g2, g3, g4)
    step.mixer_fwd(w_in_t, zero)
    dw_gate, dw_up, dw_down = step.ffn_fwd_bwd(w_out, wg_t, wu_t, w_down)
    dw_in, dw_out = step.mixer_bwd(zero)
    loss, grad_x, small = step.input_bwd(zero)
    return loss, grad_x, small, (dw_in, dw_out, dw_gate, dw_up, dw_down)


def kernel(x, ln_pre_mix, w_in, w_pool, pool_scale, w_out, ln_post_mix, ln_pre_ffn, w_gate, w_up, w_down, ln_post_ffn, loss_target, m_ln_pre_mix, m_w_in, m_w_pool, m_pool_scale, m_w_out, m_ln_post_mix, m_ln_pre_ffn, m_w_gate, m_w_up, m_w_down, m_ln_post_ffn, v_ln_pre_mix, v_w_in, v_w_pool, v_pool_scale, v_w_out, v_ln_post_mix, v_ln_pre_ffn, v_w_gate, v_w_up, v_w_down, v_ln_post_ffn):
    shards = [w_in[0].T.astype(BF16), w_out[0].astype(BF16), w_gate[0].T.astype(BF16),
              w_up[0].T.astype(BF16), w_down[0].astype(BF16)]
    flat = lambda a: a.reshape(-1, D_MODEL)
    blocks = lambda a: a.reshape(N_DEV, -1, D_MODEL)
    step = _LocalStep(x[0], loss_target[0], ln_pre_mix, w_pool[0], pool_scale, ln_post_mix, ln_pre_ffn, ln_post_ffn)

    w_in_t = flat(_gather_two_level(shards[0], "gather_w_in"))
    rest = _exchange_start(shards[1:], [False] * 4, w_in_t, "gather_rest_start")
    attn = step.mixer_fwd(w_in_t, rest[4])
    srcs, lands = _exchange_wait(rest, [False] * 4, attn, "gather_rest_wait")
    w_out_f, wg_t, wu_t, w_down_f = [flat(a) for a in _fill_own(lands, srcs, [False] * 4)]

    ffn = _exchange_start([blocks(a) for a in step.ffn_fwd_bwd(w_out_f, wg_t, wu_t, w_down_f)], [True] * 3,
                          step.dgate, "grads_ffn_start")
    mixer = _exchange_start([blocks(a) for a in step.mixer_bwd(ffn[4])], [True] * 2, step.dproj, "grads_mixer_start")
    loss, grad_x, small = step.input_bwd(mixer[4])
    got = []
    for handle, n_arr, nm in ((mixer, 2, "grads_mixer"), (ffn, 3, "grads_ffn")):
        srcs, lands = _exchange_wait(handle, [True] * n_arr, grad_x, nm + "_wait")
        got += _fill_own(lands, srcs, [True] * n_arr)
    sums = [_slot_sum(got[i], f"sum_grad_{i}", got[i].shape[1] // 2) for i in range(5)]

    small_buf = _pack_small(small[0], small[1], small[2], small[3], small[4], small[5], loss)
    small_sum = _slot_sum(_exchange([small_buf], [False], "gather_small")[0], "sum_small", SMALL_ROWS)

    g_in, g_out, g_gate, g_up, g_down = sums[0].T, sums[1], sums[2].T, sums[3].T, sums[4]
    upd = [_adamw(w[0], g, m[0], v[0], f"adamw_{nm}") for nm, w, g, m, v in (
        ("in", w_in, g_in, m_w_in, v_w_in), ("out", w_out, g_out, m_w_out, v_w_out),
        ("gate", w_gate, g_gate, m_w_gate, v_w_gate), ("up", w_up, g_up, m_w_up, v_w_up),
        ("down", w_down, g_down, m_w_down, v_w_down))]
    pack = lambda a, b, c, d, e, f: _pack_small(a, b[0], c, d, e, f, jnp.zeros((1,), F32))
    small_upd = _adamw(
        pack(ln_pre_mix, w_pool, pool_scale, ln_post_mix, ln_pre_ffn, ln_post_ffn), small_sum,
        pack(m_ln_pre_mix, m_w_pool, m_pool_scale, m_ln_post_mix, m_ln_pre_ffn, m_ln_post_ffn),
        pack(v_ln_pre_mix, v_w_pool, v_pool_scale, v_ln_post_mix, v_ln_pre_ffn, v_ln_post_ffn), "adamw_small")

    def tree(small6, big5):
        s1, spool, sscale, s2, s3, s4 = small6
        b_in, b_out, b_gate, b_up, b_down = [b[None] for b in big5]
        return [s1, b_in, spool, sscale, b_out, s2, s3, b_gate, b_up, b_down, s4]

    g_small = _unpack_small(small_sum)
    outs = [g_small[6][0], grad_x[None]]
    outs += tree(g_small[:6], [g_in, g_out, g_gate, g_up, g_down])
    for j in range(3):
        outs += tree(_unpack_small(small_upd[j])[:6], [u[j] for u in upd])
    return tuple(outs)
```

```python
import jax
import jax.numpy as jnp
from jax import lax
from jax.experimental import pallas as pl
from jax.experimental.pallas import tpu as pltpu

F32 = jnp.float32
BF16 = jnp.bfloat16

D_MODEL = 1024
POOL_W = 256
ATTN_W = 768
IN_W = 2560
D_FF = 2816
POOL_WINDOWS = (2, 4, 8, 16)
POOL_HALO = 16
DILATIONS = (1, 4, 16)
BLK = 128
LANES = 128
HEAD_DIM = 64
N_GROUPS = ATTN_W // LANES
ROPE_THETA = 10000.0
EPS = 1e-6
NEG = -1e30
N_DEV = 8
SMALL_ROWS = 24

ADAM_LR = 0.001
ADAM_B1 = 0.9
ADAM_B2 = 0.999
ADAM_EPS = 1e-08
ADAM_WD = 0.01
ADAM_STEP = 10

VMEM_LIMIT = 56 * 1024 * 1024


def _dot(a, b):
    return jnp.dot(a, b, preferred_element_type=F32)


def _dot_nt(a, b):
    return lax.dot_general(a, b, (((1,), (1,)), ((), ())), preferred_element_type=F32)


def _dot_tn(a, b):
    return lax.dot_general(a, b, (((0,), (0,)), ((), ())), preferred_element_type=F32)


def _params(n_grid):
    return pltpu.CompilerParams(dimension_semantics=("arbitrary",) * n_grid, vmem_limit_bytes=VMEM_LIMIT)


def _tok(tm, c):
    return pl.BlockSpec((tm, c), lambda i: (i, 0))


def _res(shape):
    return pl.BlockSpec(shape, lambda i: (0,) * len(shape), pipeline_mode=pl.Buffered(1))


def _acc(shape):
    return pl.BlockSpec(shape, lambda i: (0,) * len(shape))


def _rms_fwd(x, g):
    r = lax.rsqrt(jnp.mean(x * x, axis=-1, keepdims=True) + EPS)
    return x * r * g


def _rms_bwd(x, g, dy):
    r = lax.rsqrt(jnp.mean(x * x, axis=-1, keepdims=True) + EPS)
    xh = x * r
    gd = dy * g
    dx = r * (gd - xh * jnp.mean(gd * xh, axis=-1, keepdims=True))
    return dx, jnp.sum(dy * xh, axis=0, keepdims=True)


def _rope(x, c, s, sign):
    lane = lax.broadcasted_iota(jnp.int32, (x.shape[0], LANES), 1)
    first = (lane % HEAD_DIM) < (HEAD_DIM // 2)
    outs = []
    for g in range(x.shape[1] // LANES):
        xg = x[:, g * LANES:(g + 1) * LANES]
        rot = jnp.where(first, pltpu.roll(xg, LANES - HEAD_DIM // 2, 1), pltpu.roll(xg, HEAD_DIM // 2, 1))
        outs.append(xg * c + sign * (rot * s))
    return jnp.concatenate(outs, axis=1)


def _proj_fwd(x, g1, w_in_t, cos, sin, tm):
    T = x.shape[0]

    def body(x_ref, g_ref, w_ref, c_ref, s_ref, h_ref, u_ref, q_ref, k_ref, v_ref):
        h = _rms_fwd(x_ref[...], g_ref[...]).astype(BF16)
        h_ref[...] = h
        proj = _dot_nt(h, w_ref[...])
        c = c_ref[...]
        s = s_ref[...]
        u_ref[...] = proj[:, :POOL_W]
        _store_packed(q_ref, _rope(proj[:, POOL_W:POOL_W + ATTN_W], c, s, 1.0))
        _store_packed(k_ref, _rope(proj[:, POOL_W + ATTN_W:POOL_W + 2 * ATTN_W], c, s, 1.0))
        _store_packed(v_ref, proj[:, POOL_W + 2 * ATTN_W:])

    return pl.pallas_call(
        body, name="proj_fwd", grid=(T // tm,),
        in_specs=[_tok(tm, D_MODEL), _res((1, D_MODEL)), _res((IN_W, D_MODEL)), _tok(tm, LANES), _tok(tm, LANES)],
        out_specs=[_tok(tm, D_MODEL), _tok(tm, POOL_W)] + [_tok_packed(tm, ATTN_W)] * 3,
        out_shape=[jax.ShapeDtypeStruct((T, D_MODEL), BF16), jax.ShapeDtypeStruct((T, POOL_W), F32)]
        + [_packed(T, ATTN_W)] * 3,
        compiler_params=_params(1),
    )(x, g1, w_in_t, cos, sin)


def _pool_window(lane):
    return jnp.where(lane < 64, 2, jnp.where(lane < 128, 4, jnp.where(lane < 192, 8, 16)))


def _pool_select(lane, a2, a4, a8, a16):
    return jnp.where(lane < 64, a2, jnp.where(lane < 128, a4, jnp.where(lane < 192, a8, a16)))


def _pool_delta(cur, prev, i, tm):
    prev = jnp.where(i > 0, prev, 0.0)
    ext = jnp.concatenate([prev, cur], axis=0)
    s2 = ext + pltpu.roll(ext, 1, 0)
    s4 = s2 + pltpu.roll(s2, 2, 0)
    s8 = s4 + pltpu.roll(s4, 4, 0)
    s16 = s8 + pltpu.roll(s8, 8, 0)
    lane = lax.broadcasted_iota(jnp.int32, (tm, POOL_W), 1)
    row = lax.broadcasted_iota(jnp.int32, (tm, POOL_W), 0) + i * tm
    ws = _pool_select(lane, s2[POOL_HALO:], s4[POOL_HALO:], s8[POOL_HALO:], s16[POOL_HALO:])
    cnt = jnp.minimum(row + 1, _pool_window(lane)).astype(F32)
    return ws / cnt - cur


def _pool_fwd(u, wbd, scale, tm):
    T = u.shape[0]
    hb = tm // POOL_HALO

    def body(u_ref, prev_ref, w_ref, sc_ref, o_ref):
        d = _pool_delta(u_ref[...], prev_ref[...], pl.program_id(0), tm)
        o_ref[...] = (_dot(d.astype(BF16), w_ref[...]) * sc_ref[...]).astype(BF16)

    return pl.pallas_call(
        body, name="pool_fwd", grid=(T // tm,),
        in_specs=[_tok(tm, POOL_W), pl.BlockSpec((POOL_HALO, POOL_W), lambda i: (jnp.maximum(i * hb - 1, 0), 0)),
                  _res((POOL_W, POOL_W)), _res((1, POOL_W))],
        out_specs=_tok(tm, POOL_W),
        out_shape=jax.ShapeDtypeStruct((T, POOL_W), BF16),
        compiler_params=_params(1),
    )(u, u, wbd, scale)


def _attn_mask(has_prev):
    qi = lax.broadcasted_iota(jnp.int32, (BLK, 2 * BLK), 0)
    kj = lax.broadcasted_iota(jnp.int32, (BLK, 2 * BLK), 1)
    dist = qi + BLK - kj
    return (dist >= 0) & (dist <= BLK) & ((kj >= BLK) | has_prev)


def _stack_heads(x, lo):
    zero = jnp.zeros_like(x)
    return jnp.concatenate([jnp.where(lo, x, zero), jnp.where(lo, zero, x)], axis=0)


def _head_col(tile, lane, h):
    return jnp.sum(jnp.where(lane == h, tile, 0.0), axis=1, keepdims=True)


def _attn_cols(dil):
    return ATTN_W // 2 if dil >= 16 else ATTN_W


def _units_per_step(dil, backward):
    return {1: 8, 4: 2, 16: 8}[dil]


def _chunk_tokens(dil, units):
    return BLK * (units if dil == 1 else dil)


def _unit_steps(dil, units):
    return 1 if dil == 1 else dil // 2 // units


def _attn_specs(dil, nb, units):
    cw = _attn_cols(dil)
    ch = _chunk_tokens(dil, units)
    wide = lambda f: pl.BlockSpec((cw // LANES, ch // 2, LANES), f)
    cur = lambda n, j, r: (j, n, 0)
    prv = lambda n, j, r: (j, jnp.maximum(n - 1, 0), 0)
    prv_out = lambda n, j, r: (j, (n + nb - 1) % nb, 0)
    heads = pl.BlockSpec((ch, LANES), lambda n, j, r: (n, 0))
    return cw, wide(cur), wide(prv), wide(prv_out), heads


HIGH_HALF = 0xFFFF0000


def _pack(x):
    return pltpu.bitcast(x.astype(BF16), F32)


def _unpack(words):
    return pltpu.bitcast(words, BF16)


def _packed(rows, cols):
    return jax.ShapeDtypeStruct((cols // LANES, rows // 2, LANES), F32)


def _tok_packed(tm, cols):
    return pl.BlockSpec((cols // LANES, tm // 2, LANES), lambda i: (0, i, 0))


def _store_packed(ref, x):
    for g in range(x.shape[1] // LANES):
        ref[g] = _pack(x[:, g * LANES:(g + 1) * LANES])


def _load_packed(ref):
    return jnp.concatenate([_unpack(ref[g]) for g in range(ref.shape[0])], axis=1)


def _load_streams(ref, dil, r2, sl):
    if dil == 1:
        return [_unpack(ref.at[sl][pl.ds(r2 * (BLK // 2), BLK // 2), :])]
    words = lax.bitcast_convert_type(ref.at[sl][pl.ds(r2, BLK, stride=dil // 2), :], jnp.uint32)
    even = lax.bitcast_convert_type(words << 16, F32).astype(BF16)
    odd = lax.bitcast_convert_type(words & jnp.uint32(HIGH_HALF), F32).astype(BF16)
    return [even, odd]


def _load_prev_streams(prev_ref, cur_ref, dil, units, r2, sl):
    if dil > 1:
        return _load_streams(prev_ref, dil, r2, sl)
    return _load_streams(cur_ref, 1, r2 - 1, sl) if r2 > 0 else _load_streams(prev_ref, 1, units - 1, sl)


def _load_streams_f32(ref, dil, r2, sl):
    ref = ref if sl is None else ref.at[sl]
    if dil == 1:
        return [ref[pl.ds(r2 * BLK, BLK), :]]
    return [ref[pl.ds(2 * r2 + e, BLK, stride=dil), :] for e in range(2)]


def _store_streams_f32(ref, dil, r2, sl, tiles):
    ref = ref if sl is None else ref.at[sl]
    if dil == 1:
        ref[pl.ds(r2 * BLK, BLK), :] = tiles[0]
    else:
        for e, t in enumerate(tiles):
            ref[pl.ds(2 * r2 + e, BLK, stride=dil), :] = t


def _store_streams(ref, dil, r2, sl, tiles):
    if dil == 1:
        ref.at[sl][pl.ds(r2 * (BLK // 2), BLK // 2), :] = _pack(tiles[0])
    else:
        even, odd = [lax.bitcast_convert_type(t.astype(BF16).astype(F32), jnp.uint32) for t in tiles]
        words = (odd & jnp.uint32(HIGH_HALF)) | (even >> 16)
        ref.at[sl][pl.ds(r2, BLK, stride=dil // 2), :] = lax.bitcast_convert_type(words, F32)


def _attn_fwd(q, k, v, dil, others):
    T = 2 * q.shape[1]
    reps = _units_per_step(dil, False)
    nb = T // _chunk_tokens(dil, reps)
    first = not others
    cw, cur, prv, _, heads = _attn_specs(dil, nb, reps)
    ncb = ATTN_W // cw
    heads_per_step = cw // HEAD_DIM
    n_str = min(dil, 2)
    everything = None

    def body(*refs):
        q_ref, kc_ref, kp_ref, vc_ref, vp_ref = refs[:5]
        acc_ins, lse_ins = refs[5:5 + 2 * len(others):2], refs[6:6 + 2 * len(others):2]
        acc_ref, lse_ref = refs[-2:]
        j = pl.program_id(1)
        lane = lax.broadcasted_iota(jnp.int32, (BLK, LANES), 1)
        lo = lane < HEAD_DIM

        def stream_pair(r2):
            valid = _attn_mask(True if dil == 1 and r2 > 0 else pl.program_id(0) > 0)
            lse_tiles = [jnp.zeros((BLK, LANES), F32) for _ in range(n_str)]
            own = []
            for g in range(cw // LANES):
                qs, kcs, vcs = [_load_streams(r, dil, r2, g) for r in (q_ref, kc_ref, vc_ref)]
                kps, vps = [_load_prev_streams(p, c, dil, reps, r2, g) for p, c in ((kp_ref, kc_ref), (vp_ref, vc_ref))]
                pairs = []
                for e in range(n_str):
                    qg = qs[e] * 0.125
                    kcat = jnp.concatenate([kps[e], kcs[e]], axis=0)
                    vcat = jnp.concatenate([vps[e], vcs[e]], axis=0)
                    pair = None
                    for hh in range(2):
                        h = j * heads_per_step + 2 * g + hh
                        hm = lo if hh == 0 else jnp.logical_not(lo)
                        s = _dot_nt(jnp.where(hm, qg, jnp.zeros_like(qg)), kcat)
                        s = jnp.where(valid, s, NEG)
                        m = jnp.max(s, axis=1, keepdims=True)
                        p = jnp.exp(s - m)
                        den = jnp.sum(p, axis=1, keepdims=True)
                        o = _dot(p.astype(BF16), vcat) / den
                        pair = o if hh == 0 else jnp.where(lo, pair, o)
                        lse_tiles[e] = jnp.where(lane == h, m + jnp.log(den), lse_tiles[e])
                    pairs.append(pair)
                if first:
                    _store_streams(acc_ref, dil, r2, g, pairs)
                else:
                    own.append(pairs)
            if not first:
                mine = (lane >= j * heads_per_step) & (lane < (j + 1) * heads_per_step)
                theirs = [_load_streams_f32(ref, dil, r2, everything) for ref in lse_ins]
                w_theirs, w_own = [[] for _ in others], []
                for e in range(n_str):
                    parts = [t[e] for t in theirs] + [lse_tiles[e]]
                    mx = parts[0]
                    for part in parts[1:]:
                        mx = jnp.maximum(mx, part)
                    total = mx + jnp.log(sum(jnp.exp(part - mx) for part in parts))
                    for i, t in enumerate(theirs):
                        w_theirs[i].append(jnp.exp(t[e] - total))
                    w_own.append(jnp.exp(lse_tiles[e] - total))
                    lse_tiles[e] = jnp.where(mine, total, 0.0)
                for g in range(cw // LANES):
                    h0 = j * heads_per_step + 2 * g
                    spread = lambda w: jnp.where(lo, _head_col(w, lane, h0), _head_col(w, lane, h0 + 1))
                    olds = [_load_streams(ref, dil, r2, g) for ref in acc_ins]
                    _store_streams(acc_ref, dil, r2, g, [
                        sum(olds[i][e].astype(F32) * spread(w_theirs[i][e]) for i in range(len(others)))
                        + own[g][e] * spread(w_own[e]) for e in range(n_str)])
            if ncb == 1:
                _store_streams_f32(lse_ref, dil, r2, everything, lse_tiles)
            else:
                @pl.when(j == 0)
                def _():
                    _store_streams_f32(lse_ref, dil, r2, everything, lse_tiles)

                @pl.when(j > 0)
                def _():
                    before = _load_streams_f32(lse_ref, dil, r2, everything)
                    _store_streams_f32(lse_ref, dil, r2, everything, [a + b for a, b in zip(before, lse_tiles)])

        for rep in range(reps):
            stream_pair(rep if dil == 1 else pl.program_id(2) * reps + rep)

    ins = [q, k, k, v, v]
    in_specs = [cur, cur, prv, cur, prv]
    for acc, lse in others:
        ins += [acc, lse]
        in_specs += [cur, heads]
    return pl.pallas_call(
        body, name=f"attn_fwd_d{dil}", grid=(nb, ncb, _unit_steps(dil, reps)),
        in_specs=in_specs, out_specs=[cur, heads],
        out_shape=[_packed(T, ATTN_W), jax.ShapeDtypeStruct((T, LANES), F32)],
        compiler_params=_params(3),
    )(*ins)


def _mix_fwd(pool, attn, x, w_out, g2, g3, tm):
    T = x.shape[0]

    def body(p_ref, a_ref, x_ref, w_ref, g2_ref, g3_ref, cat_ref, mix_ref, x2_ref, h2_ref):
        p = p_ref[...]
        a = _load_packed(a_ref)
        cat_ref[...] = jnp.concatenate([p, a], axis=1)
        mix = _dot(p, w_ref[:POOL_W, :]) + _dot(a, w_ref[POOL_W:, :])
        mix_ref[...] = mix
        x2 = x_ref[...] + _rms_fwd(mix, g2_ref[...])
        x2_ref[...] = x2
        h2_ref[...] = _rms_fwd(x2, g3_ref[...]).astype(BF16)

    return pl.pallas_call(
        body, name="mix_fwd", grid=(T // tm,),
        in_specs=[_tok(tm, POOL_W), _tok_packed(tm, ATTN_W), _tok(tm, D_MODEL), _res((D_MODEL, D_MODEL)),
                  _res((1, D_MODEL)), _res((1, D_MODEL))],
        out_specs=[_tok(tm, D_MODEL)] * 4,
        out_shape=[jax.ShapeDtypeStruct((T, D_MODEL), BF16), jax.ShapeDtypeStruct((T, D_MODEL), F32),
                   jax.ShapeDtypeStruct((T, D_MODEL), F32), jax.ShapeDtypeStruct((T, D_MODEL), BF16)],
        compiler_params=_params(1),
    )(pool, attn, x, w_out, g2, g3)


def _ffn_up(h2, wg_t, wu_t, tm):
    T = h2.shape[0]

    def body(h_ref, wg_ref, wu_ref, dg_ref, du_ref, a_ref):
        h = h_ref[...]
        gate = _dot_nt(h, wg_ref[...])
        up = _dot_nt(h, wu_ref[...])
        sg = 1.0 / (1.0 + jnp.exp(-gate))
        silu = gate * sg
        a_ref[...] = (silu * up).astype(BF16)
        dg_ref[...] = (up * (sg * (1.0 + gate * (1.0 - sg)))).astype(BF16)
        du_ref[...] = silu.astype(BF16)

    return pl.pallas_call(
        body, name="ffn_up", grid=(T // tm,),
        in_specs=[_tok(tm, D_MODEL), _res((D_FF, D_MODEL)), _res((D_FF, D_MODEL))],
        out_specs=[_tok(tm, D_FF)] * 3,
        out_shape=[jax.ShapeDtypeStruct((T, D_FF), BF16)] * 3,
        compiler_params=_params(1),
    )(h2, wg_t, wu_t)


def _ffn_down_loss(act, w_down, x2, g4, tgt, tm):
    T = act.shape[0]

    def body(a_ref, w_ref, x2_ref, g_ref, t_ref, df_ref, dy_ref, dg_ref, loss_ref):
        i = pl.program_id(0)

        @pl.when(i == 0)
        def _():
            dg_ref[...] = jnp.zeros_like(dg_ref)
            loss_ref[...] = jnp.zeros_like(loss_ref)

        f = _dot(a_ref[...], w_ref[...])
        g = g_ref[...]
        err = x2_ref[...] + _rms_fwd(f, g) - t_ref[...]
        loss_ref[...] += 0.5 * jnp.sum(jnp.mean(err * err, axis=-1, keepdims=True), axis=0, keepdims=True)
        dy = err * (1.0 / D_MODEL)
        dy_ref[...] = dy
        df, dg = _rms_bwd(f, g, dy)
        dg_ref[...] += dg
        df_ref[...] = df.astype(BF16)

    return pl.pallas_call(
        body, name="ffn_down_loss", grid=(T // tm,),
        in_specs=[_tok(tm, D_FF), _res((D_FF, D_MODEL)), _tok(tm, D_MODEL), _res((1, D_MODEL)), _tok(tm, D_MODEL)],
        out_specs=[_tok(tm, D_MODEL), _tok(tm, D_MODEL), _acc((1, D_MODEL)), _acc((1, 1))],
        out_shape=[jax.ShapeDtypeStruct((T, D_MODEL), BF16), jax.ShapeDtypeStruct((T, D_MODEL), F32),
                   jax.ShapeDtypeStruct((1, D_MODEL), F32), jax.ShapeDtypeStruct((1, 1), F32)],
        compiler_params=_params(1),
    )(act, w_down, x2, g4, tgt)


def _ffn_act_bwd(df, w_down, act_dgate, act_dup, tm):
    T = df.shape[0]

    def body(df_ref, w_ref, ag_ref, au_ref, dg_ref, du_ref):
        dact = _dot_nt(df_ref[...], w_ref[...])
        dg_ref[...] = (dact * ag_ref[...].astype(F32)).astype(BF16)
        du_ref[...] = (dact * au_ref[...].astype(F32)).astype(BF16)

    return pl.pallas_call(
        body, name="ffn_act_bwd", grid=(T // tm,),
        in_specs=[_tok(tm, D_MODEL), _res((D_FF, D_MODEL)), _tok(tm, D_FF), _tok(tm, D_FF)],
        out_specs=[_tok(tm, D_FF)] * 2,
        out_shape=[jax.ShapeDtypeStruct((T, D_FF), BF16)] * 2,
        compiler_params=_params(1),
    )(df, w_down, act_dgate, act_dup)


def _ffn_in_bwd(dgate, dup, wg_t, wu_t, x2, mix, dy, g3, g2, tm):
    T = x2.shape[0]

    def body(dg_ref, du_ref, wg_ref, wu_ref, x2_ref, mix_ref, dy_ref, g3_ref, g2_ref,
             dx2_ref, dmix_ref, dg3_ref, dg2_ref):
        @pl.when(pl.program_id(0) == 0)
        def _():
            dg3_ref[...] = jnp.zeros_like(dg3_ref)
            dg2_ref[...] = jnp.zeros_like(dg2_ref)

        dh2 = _dot(dg_ref[...], wg_ref[...]) + _dot(du_ref[...], wu_ref[...])
        dn, dg3 = _rms_bwd(x2_ref[...], g3_ref[...], dh2)
        dx2 = dy_ref[...] + dn
        dx2_ref[...] = dx2
        dg3_ref[...] += dg3
        dmix, dg2 = _rms_bwd(mix_ref[...], g2_ref[...], dx2)
        dg2_ref[...] += dg2
        dmix_ref[...] = dmix.astype(BF16)

    return pl.pallas_call(
        body, name="ffn_in_bwd", grid=(T // tm,),
        in_specs=[_tok(tm, D_FF), _tok(tm, D_FF), _res((D_FF, D_MODEL)), _res((D_FF, D_MODEL)),
                  _tok(tm, D_MODEL), _tok(tm, D_MODEL), _tok(tm, D_MODEL), _res((1, D_MODEL)), _res((1, D_MODEL))],
        out_specs=[_tok(tm, D_MODEL), _tok(tm, D_MODEL), _acc((1, D_MODEL)), _acc((1, D_MODEL))],
        out_shape=[jax.ShapeDtypeStruct((T, D_MODEL), F32), jax.ShapeDtypeStruct((T, D_MODEL), BF16),
                   jax.ShapeDtypeStruct((1, D_MODEL), F32), jax.ShapeDtypeStruct((1, D_MODEL), F32)],
        compiler_params=_params(1),
    )(dgate, dup, wg_t, wu_t, x2, mix, dy, g3, g2)


def _mix_bwd(dmix, w_out, attn, tm):
    T = dmix.shape[0]

    def body(d_ref, w_ref, a_ref, dp_ref, da_ref, ds_ref):
        dcat = _dot_nt(d_ref[...], w_ref[...])
        dp_ref[...] = dcat[:, :POOL_W].astype(BF16)
        dattn = dcat[:, POOL_W:].astype(BF16)
        _store_packed(da_ref, dattn)
        prod = dattn.astype(F32) * _load_packed(a_ref).astype(F32)
        lane = lax.broadcasted_iota(jnp.int32, (tm, LANES), 1)
        lo = lane < HEAD_DIM
        dsum = jnp.zeros((tm, LANES), F32)
        for g in range(N_GROUPS):
            pg = prod[:, g * LANES:(g + 1) * LANES]
            dsum = jnp.where(lane == 2 * g, jnp.sum(jnp.where(lo, pg, 0.0), axis=1, keepdims=True), dsum)
            dsum = jnp.where(lane == 2 * g + 1, jnp.sum(jnp.where(lo, 0.0, pg), axis=1, keepdims=True), dsum)
        ds_ref[...] = dsum

    return pl.pallas_call(
        body, name="mix_bwd", grid=(T // tm,),
        in_specs=[_tok(tm, D_MODEL), _res((D_MODEL, D_MODEL)), _tok_packed(tm, ATTN_W)],
        out_specs=[_tok(tm, POOL_W), _tok_packed(tm, ATTN_W), _tok(tm, LANES)],
        out_shape=[jax.ShapeDtypeStruct((T, POOL_W), BF16), _packed(T, ATTN_W), jax.ShapeDtypeStruct((T, LANES), F32)],
        compiler_params=_params(1),
    )(dmix, w_out, attn)


def _attn_bwd(q, k, v, dout, dsum, lse, dil):
    T = 2 * q.shape[1]
    reps = _units_per_step(dil, True)
    nb = T // _chunk_tokens(dil, reps)
    cw, cur, prv, prv_out, heads = _attn_specs(dil, nb, reps)
    ncb = ATTN_W // cw
    heads_per_step = cw // HEAD_DIM
    n_str = min(dil, 2)

    def body(q_ref, kc_ref, kp_ref, vc_ref, vp_ref, do_ref, dsum_ref, lse_ref,
             dq_ref, dkc_ref, dkp_ref, dvc_ref, dvp_ref):
        j = pl.program_id(1)
        lane = lax.broadcasted_iota(jnp.int32, (BLK, LANES), 1)
        lo = lane < HEAD_DIM
        unit = lambda rep: rep if dil == 1 else pl.program_id(2) * reps + rep
        stats = [[_load_streams_f32(ref, dil, unit(rep), None) for ref in (lse_ref, dsum_ref)] for rep in range(reps)]

        def unit_grads(rep, g):
            r2 = unit(rep)
            valid = _attn_mask(True if dil == 1 and r2 > 0 else pl.program_id(0) > 0)
            valid2 = jnp.concatenate([valid, valid], axis=0)
            lse_tiles, dsum_tiles = stats[rep]
            qs, kcs, vcs, dos = [_load_streams(r, dil, r2, g) for r in (q_ref, kc_ref, vc_ref, do_ref)]
            kps, vps = [_load_prev_streams(p, c, dil, reps, r2, g) for p, c in ((kp_ref, kc_ref), (vp_ref, vc_ref))]
            dqs, dks, dvs = [], [], []
            for e in range(n_str):
                qg = qs[e] * 0.125
                dog = dos[e]
                kcat = jnp.concatenate([kps[e], kcs[e]], axis=0)
                vcat = jnp.concatenate([vps[e], vcs[e]], axis=0)
                h0 = j * heads_per_step + 2 * g
                q2 = _stack_heads(qg, lo)
                do2 = _stack_heads(dog, lo)
                both = lambda tile: jnp.concatenate([_head_col(tile, lane, h0), _head_col(tile, lane, h0 + 1)], axis=0)
                lse2, dsum2 = both(lse_tiles[e]), both(dsum_tiles[e])
                p = jnp.exp(jnp.where(valid2, _dot_nt(q2, kcat), NEG) - lse2)
                ds = (p * (_dot_nt(do2, vcat) - dsum2)).astype(BF16)
                dvs.append(_dot_tn(p.astype(BF16), do2))
                dks.append(_dot_tn(ds, q2))
                dq2 = _dot(ds, kcat) * 0.125
                dqs.append(jnp.where(lo, dq2[:BLK], dq2[BLK:]))
            return dqs, dks, dvs

        for g in range(cw // LANES):
            if dil > 1:
                for rep in range(reps):
                    r2 = unit(rep)
                    dqs, dks, dvs = unit_grads(rep, g)
                    _store_streams(dq_ref, dil, r2, g, dqs)
                    _store_streams(dkp_ref, dil, r2, g, [t[:BLK] for t in dks])
                    _store_streams(dkc_ref, dil, r2, g, [t[BLK:] for t in dks])
                    _store_streams(dvp_ref, dil, r2, g, [t[:BLK] for t in dvs])
                    _store_streams(dvc_ref, dil, r2, g, [t[BLK:] for t in dvs])
            else:
                blocks = [unit_grads(b, g) for b in range(reps)]
                for b, (dqs, dks, dvs) in enumerate(blocks):
                    _store_streams(dq_ref, 1, b, g, dqs)
                    for cur_ref, prev_ref, which in ((dkc_ref, dkp_ref, 1), (dvc_ref, dvp_ref, 2)):
                        own = blocks[b][which][0][BLK:]
                        if b + 1 < reps:
                            own = own + blocks[b + 1][which][0][:BLK]
                        _store_streams(cur_ref, 1, b, g, [own])
                        edge = blocks[0][which][0][:BLK] if b == reps - 1 else jnp.zeros((BLK, LANES), F32)
                        _store_streams(prev_ref, 1, b, g, [edge])

    return pl.pallas_call(
        body, name=f"attn_bwd_d{dil}", grid=(nb, ncb, _unit_steps(dil, reps)),
        in_specs=[cur, cur, prv, cur, prv, cur, heads, heads],
        out_specs=[cur, cur, prv_out, cur, prv_out],
        out_shape=[_packed(T, ATTN_W)] * 5,
        compiler_params=_params(3),
    )(q, k, k, v, v, dout, dsum, lse)


def _pool_bwd(u, dy, wbd, scale, tm):
    T = u.shape[0]
    nt = T // tm
    hb = tm // POOL_HALO

    def body(u_ref, prev_ref, dy_ref, next_ref, w_ref, sc_ref, du_ref, dw_ref, dsc_ref):
        i = pl.program_id(0)

        @pl.when(i == 0)
        def _():
            dw_ref[...] = jnp.zeros_like(dw_ref)
            dsc_ref[...] = jnp.zeros_like(dsc_ref)

        w = w_ref[...]
        sc = sc_ref[...]
        d = _pool_delta(u_ref[...], prev_ref[...], i, tm).astype(BF16)
        dyc = dy_ref[...].astype(F32)
        dsc_ref[...] += jnp.sum(dyc * _dot(d, w), axis=0, keepdims=True)
        nxt = jnp.where(i < nt - 1, next_ref[...].astype(F32), 0.0)
        dypre = (jnp.concatenate([dyc, nxt], axis=0) * sc).astype(BF16)
        dw_ref[...] += _dot_tn(d, dypre[:tm])
        dd = _dot_nt(dypre, w)
        n = tm + POOL_HALO
        lane = lax.broadcasted_iota(jnp.int32, (n, POOL_W), 1)
        row = lax.broadcasted_iota(jnp.int32, (n, POOL_W), 0) + i * tm
        gx = dd / jnp.minimum(row + 1, _pool_window(lane)).astype(F32)
        a2 = gx + pltpu.roll(gx, n - 1, 0)
        a4 = a2 + pltpu.roll(a2, n - 2, 0)
        a8 = a4 + pltpu.roll(a4, n - 4, 0)
        a16 = a8 + pltpu.roll(a8, n - 8, 0)
        fs = _pool_select(lane[:tm], a2[:tm], a4[:tm], a8[:tm], a16[:tm])
        du_ref[...] = (fs - dd[:tm]).astype(BF16)

    return pl.pallas_call(
        body, name="pool_bwd", grid=(nt,),
        in_specs=[_tok(tm, POOL_W), pl.BlockSpec((POOL_HALO, POOL_W), lambda i: (jnp.maximum(i * hb - 1, 0), 0)),
                  _tok(tm, POOL_W), pl.BlockSpec((POOL_HALO, POOL_W), lambda i: (jnp.minimum((i + 1) * hb, nt * hb - 1), 0)),
                  _res((POOL_W, POOL_W)), _res((1, POOL_W))],
        out_specs=[_tok(tm, POOL_W), _acc((POOL_W, POOL_W)), _acc((1, POOL_W))],
        out_shape=[jax.ShapeDtypeStruct((T, POOL_W), BF16), jax.ShapeDtypeStruct((POOL_W, POOL_W), F32),
                   jax.ShapeDtypeStruct((1, POOL_W), F32)],
        compiler_params=_params(1),
    )(u, u, dy, dy, wbd, scale)


def _dproj_combine(du, dqs, dkcs, dkps, dvcs, dvps, cos, sin, tm):
    T = du.shape[0]
    n_cfg = len(dqs)

    def body(*refs):
        du_ref = refs[0]
        groups = [refs[1 + j * n_cfg:1 + (j + 1) * n_cfg] for j in range(5)]
        c_ref, s_ref, out_ref = refs[1 + 5 * n_cfg:]
        tot = lambda rs: sum(_load_packed(r).astype(F32) for r in rs)
        c = c_ref[...]
        s = s_ref[...]
        dq = _rope(tot(groups[0]), c, s, -1.0)
        dk = _rope(tot(groups[1]) + tot(groups[2]), c, s, -1.0)
        dv = tot(groups[3]) + tot(groups[4])
        out_ref[...] = jnp.concatenate([du_ref[...], dq.astype(BF16), dk.astype(BF16), dv.astype(BF16)], axis=1)

    return pl.pallas_call(
        body, name="dproj_combine", grid=(T // tm,),
        in_specs=[_tok(tm, POOL_W)] + [_tok_packed(tm, ATTN_W)] * (5 * n_cfg) + [_tok(tm, LANES)] * 2,
        out_specs=_tok(tm, IN_W),
        out_shape=jax.ShapeDtypeStruct((T, IN_W), BF16),
        compiler_params=_params(1),
    )(du, *dqs, *dkcs, *dkps, *dvcs, *dvps, cos, sin)


def _proj_bwd(dproj, w_in_t, x, dx2, g1, tm):
    T = x.shape[0]

    def body(d_ref, w_ref, x_ref, r_ref, g_ref, dx_ref, dg_ref):
        @pl.when(pl.program_id(0) == 0)
        def _():
            dg_ref[...] = jnp.zeros_like(dg_ref)

        dn, dg = _rms_bwd(x_ref[...], g_ref[...], _dot(d_ref[...], w_ref[...]))
        dg_ref[...] += dg
        dx_ref[...] = r_ref[...] + dn

    return pl.pallas_call(
        body, name="proj_bwd", grid=(T // tm,),
        in_specs=[_tok(tm, IN_W), _res((IN_W, D_MODEL)), _tok(tm, D_MODEL), _tok(tm, D_MODEL), _res((1, D_MODEL))],
        out_specs=[_tok(tm, D_MODEL), _acc((1, D_MODEL))],
        out_shape=[jax.ShapeDtypeStruct((T, D_MODEL), F32), jax.ShapeDtypeStruct((1, D_MODEL), F32)],
        compiler_params=_params(1),
    )(dproj, w_in_t, x, dx2, g1)


def _wgrad(a, b, name, tile_m, tk):
    T, M = a.shape
    N = b.shape[1]
    nk = T // tk

    def body(a_ref, b_ref, o_ref, acc_ref):
        kk = pl.program_id(1)

        @pl.when(kk == 0)
        def _():
            acc_ref[...] = jnp.zeros_like(acc_ref)

        acc_ref[...] += _dot_tn(a_ref[...], b_ref[...])

        @pl.when(kk == nk - 1)
        def _():
            o_ref[...] = acc_ref[...].astype(BF16)

    return pl.pallas_call(
        body, name=name, grid=(M // tile_m, nk),
        in_specs=[pl.BlockSpec((tk, tile_m), lambda j, kk: (kk, j)), pl.BlockSpec((tk, N), lambda j, kk: (kk, 0))],
        out_specs=pl.BlockSpec((tile_m, N), lambda j, kk: (j, 0)),
        out_shape=jax.ShapeDtypeStruct((M, N), BF16),
        scratch_shapes=[pltpu.VMEM((tile_m, N), F32)],
        compiler_params=_params(2),
    )(a, b)


def _exchange(arrs, scatter, name):
    n = len(arrs)
    out_shapes = [jax.ShapeDtypeStruct((N_DEV,) + (a.shape[1:] if sc else a.shape), a.dtype)
                  for a, sc in zip(arrs, scatter)]

    def body(*refs):
        ins, outs = refs[:n], refs[n:2 * n]
        send_sems, recv_sems, loc_sems = refs[2 * n:]
        x, y, c = lax.axis_index("x"), lax.axis_index("y"), lax.axis_index("c")
        me = 4 * x + 2 * y + c
        local, sends, recvs = [], [], []
        for i in range(n):
            own = ins[i].at[me] if scatter[i] else ins[i]
            loc = pltpu.make_async_copy(own, outs[i].at[me], loc_sems.at[i])
            loc.start()
            local.append(loc)
            for kbits in range(1, N_DEV):
                px = 1 - x if kbits & 4 else x
                py = 1 - y if kbits & 2 else y
                pc = 1 - c if kbits & 1 else c
                pid = 4 * px + 2 * py + pc
                src = ins[i].at[pid] if scatter[i] else ins[i]
                cp = pltpu.make_async_remote_copy(
                    src_ref=src, dst_ref=outs[i].at[me],
                    send_sem=send_sems.at[i, kbits - 1], recv_sem=recv_sems.at[i, kbits - 1],
                    device_id=(px, py, pc), device_id_type=pl.DeviceIdType.MESH)
                cp.start()
                sends.append(cp)
                recvs.append(pltpu.make_async_remote_copy(
                    src_ref=src, dst_ref=outs[i].at[pid],
                    send_sem=send_sems.at[i, kbits - 1], recv_sem=recv_sems.at[i, kbits - 1],
                    device_id=(px, py, pc), device_id_type=pl.DeviceIdType.MESH))
        for cp in recvs:
            cp.wait_recv()
        for cp in sends:
            cp.wait_send()
        for cp in local:
            cp.wait()

    hbm = pl.BlockSpec(memory_space=pl.ANY)
    return pl.pallas_call(
        body, name=name, in_specs=[hbm] * n, out_specs=[hbm] * n, out_shape=out_shapes,
        scratch_shapes=[pltpu.SemaphoreType.DMA((n, N_DEV - 1)), pltpu.SemaphoreType.DMA((n, N_DEV - 1)),
                        pltpu.SemaphoreType.DMA((n,))],
    )(*arrs)


def _gather_two_level(arr, name):
    def body(x_ref, out_ref, send_sems, recv_sems, local_sem):
        x, y, c = lax.axis_index("x"), lax.axis_index("y"), lax.axis_index("c")
        me, sibling = (x, y, c), (x, y, 1 - c)
        chips = [(1 - x, y), (x, 1 - y), (1 - x, 1 - y)]
        slot = lambda px, py, pc: out_ref.at[4 * px + 2 * py + pc]

        def copy(k, block, to, src=None):
            return pltpu.make_async_remote_copy(
                src_ref=slot(*block) if src is None else src, dst_ref=slot(*block),
                send_sem=send_sems.at[k], recv_sem=recv_sems.at[k],
                device_id=to, device_id_type=pl.DeviceIdType.MESH)

        mine = pltpu.make_async_copy(x_ref, slot(*me), local_sem)
        mine.start()
        first = [copy(0, me, sibling, src=x_ref)]
        first += [copy(1 + i, me, (*chip, c), src=x_ref) for i, chip in enumerate(chips)]
        for cp in first:
            cp.start()
        passed = [copy(4 + i, (*chip, c), sibling) for i, chip in enumerate(chips)]
        for i, chip in enumerate(chips):
            copy(1 + i, (*chip, c), me).wait_recv()
            passed[i].start()
        copy(0, sibling, me).wait_recv()
        for i, chip in enumerate(chips):
            copy(4 + i, (*chip, 1 - c), me).wait_recv()
        for cp in first + passed:
            cp.wait_send()
        mine.wait()

    hbm = pl.BlockSpec(memory_space=pl.ANY)
    return pl.pallas_call(
        body, name=name, in_specs=[hbm], out_specs=hbm,
        out_shape=jax.ShapeDtypeStruct((N_DEV,) + arr.shape, arr.dtype),
        scratch_shapes=[pltpu.SemaphoreType.DMA((N_DEV - 1,)), pltpu.SemaphoreType.DMA((N_DEV - 1,)),
                        pltpu.SemaphoreType.DMA],
    )(arr)


def _peers(x, y, c):
    for kbits in range(1, N_DEV):
        px = 1 - x if kbits & 4 else x
        py = 1 - y if kbits & 2 else y
        pc = 1 - c if kbits & 1 else c
        yield kbits - 1, (px, py, pc), 4 * px + 2 * py + pc


def _peer_copies(ins, lands, scatter, send_sems, recv_sems, incoming):
    x, y, c = lax.axis_index("x"), lax.axis_index("y"), lax.axis_index("c")
    me = 4 * x + 2 * y + c
    copies = []
    for i in range(len(ins)):
        for k, peer, pid in _peers(x, y, c):
            slot = i * (N_DEV - 1) + k
            copies.append(pltpu.make_async_remote_copy(
                src_ref=ins[i].at[pid] if scatter[i] else ins[i], dst_ref=lands[i].at[pid if incoming else me],
                send_sem=send_sems.at[slot], recv_sem=recv_sems.at[slot],
                device_id=peer, device_id_type=pl.DeviceIdType.MESH))
    return copies


_HBM = pl.BlockSpec(memory_space=pltpu.HBM)
_SEM = pl.BlockSpec(memory_space=pltpu.SEMAPHORE)
_DATAFLOW = pltpu.SideEffectType.DATAFLOW_SIDE_EFFECTING


def _exchange_start(arrs, scatter, after, name):
    n = len(arrs)
    lands = [lax.empty((N_DEV,) + (a.shape[1:] if sc else a.shape), a.dtype) for a, sc in zip(arrs, scatter)]

    def body(*refs):
        ins, lz = refs[:n], refs[n:2 * n]
        send_sems, recv_sems = refs[2 * n + 1:2 * n + 3]
        token = refs[-1]
        for cp in _peer_copies(ins, lz, scatter, send_sems, recv_sems, False):
            cp.start()
        token[...] = jnp.zeros_like(token)

    sem_shape = pltpu.SemaphoreType.DMA((n * (N_DEV - 1),))
    outs = pl.pallas_call(
        body, name=name,
        out_shape=(sem_shape, sem_shape, *[pltpu.HBM(a.shape, a.dtype) for a in arrs + lands],
                   jax.ShapeDtypeStruct((8, LANES), F32)),
        in_specs=[_HBM] * (2 * n) + [pl.BlockSpec(memory_space=pl.ANY)],
        out_specs=(_SEM, _SEM, *[_HBM] * (2 * n), pl.BlockSpec(memory_space=pltpu.VMEM)),
        input_output_aliases={i: 2 + i for i in range(2 * n)},
        compiler_params=pltpu.CompilerParams(has_side_effects=_DATAFLOW),
    )(*[pltpu.with_memory_space_constraint(a, pltpu.HBM) for a in arrs + lands], after)
    return outs[0], outs[1], list(outs[2:2 + n]), list(outs[2 + n:2 + 2 * n]), outs[-1]


def _exchange_wait(handle, scatter, after, name):
    send_sems, recv_sems, srcs, lands, _ = handle
    n = len(srcs)

    def body(*refs):
        ins, lz = refs[:n], refs[n:2 * n]
        for cp in _peer_copies(ins, lz, scatter, refs[2 * n], refs[2 * n + 1], False):
            cp.wait_send()
        for cp in _peer_copies(ins, lz, scatter, refs[2 * n], refs[2 * n + 1], True):
            cp.wait_recv()

    outs = pl.pallas_call(
        body, name=name,
        out_shape=[pltpu.HBM(a.shape, a.dtype) for a in srcs + lands],
        in_specs=[_HBM] * (2 * n) + [_SEM, _SEM, pl.BlockSpec(memory_space=pl.ANY)],
        out_specs=[_HBM] * (2 * n),
        input_output_aliases={i: i for i in range(2 * n)},
        compiler_params=pltpu.CompilerParams(has_side_effects=_DATAFLOW),
    )(*srcs, *lands, send_sems, recv_sems, after)
    return list(outs[:n]), list(outs[n:])


def _fill_own(lands, srcs, scatter):
    me = 4 * lax.axis_index("x") + 2 * lax.axis_index("y") + lax.axis_index("c")
    own = [lax.dynamic_index_in_dim(s, me, 0, keepdims=False) if sc else s for s, sc in zip(srcs, scatter)]
    return [lax.dynamic_update_index_in_dim(land, o, me, 0) for land, o in zip(lands, own)]


def _slot_sum(parts, name, tr):
    _, R, C = parts.shape

    def body(p_ref, o_ref):
        acc = p_ref[0].astype(F32)
        for s in range(1, N_DEV):
            acc = acc + p_ref[s].astype(F32)
        o_ref[...] = acc

    return pl.pallas_call(
        body, name=name, grid=(R // tr,),
        in_specs=[pl.BlockSpec((N_DEV, tr, C), lambda i: (0, i, 0))],
        out_specs=pl.BlockSpec((tr, C), lambda i: (i, 0)),
        out_shape=jax.ShapeDtypeStruct((R, C), F32),
        compiler_params=_params(1),
    )(parts)


def _adamw(w, g, m, v, name):
    def body(w_ref, g_ref, m_ref, v_ref, d_ref, nm_ref, nv_ref):
        g = g_ref[...]
        nm = ADAM_B1 * m_ref[...] + (1.0 - ADAM_B1) * g
        nv = ADAM_B2 * v_ref[...] + (1.0 - ADAM_B2) * jnp.square(g)
        m_hat = nm / (1.0 - ADAM_B1 ** ADAM_STEP)
        v_hat = nv / (1.0 - ADAM_B2 ** ADAM_STEP)
        d_ref[...] = -ADAM_LR * (m_hat / (jnp.sqrt(v_hat) + ADAM_EPS) + ADAM_WD * w_ref[...])
        nm_ref[...] = nm
        nv_ref[...] = nv

    return pl.pallas_call(
        body, name=name, out_shape=[jax.ShapeDtypeStruct(w.shape, F32)] * 3,
        compiler_params=pltpu.CompilerParams(vmem_limit_bytes=VMEM_LIMIT),
    )(w, g, m, v)


def _rope_tables(T):
    half = HEAD_DIM // 2
    freqs = ROPE_THETA ** (-jnp.arange(half, dtype=F32) * (2.0 / HEAD_DIM))
    ang = jnp.arange(T).astype(F32)[:, None] * jnp.tile(freqs, LANES // half)[None, :]
    sign = jnp.tile(jnp.concatenate([-jnp.ones((half,), F32), jnp.ones((half,), F32)]), LANES // HEAD_DIM)
    return jnp.cos(ang), jnp.sin(ang) * sign[None, :]


def _block_diag(w_pool):
    wbd = jnp.zeros((POOL_W, POOL_W), F32)
    g = POOL_W // len(POOL_WINDOWS)
    for i in range(len(POOL_WINDOWS)):
        wbd = wbd.at[i * g:(i + 1) * g, i * g:(i + 1) * g].set(w_pool[i])
    return wbd


def _pack_small(g1, w_pool, pool_scale, g2, g3, g4, extra):
    pad = lambda a: jnp.pad(a.reshape(1, -1), ((0, 0), (0, D_MODEL - a.size)))
    rows = [g1.reshape(1, -1), g2.reshape(1, -1), g3.reshape(1, -1), g4.reshape(1, -1),
            w_pool.reshape(-1, D_MODEL), pad(pool_scale), pad(extra)]
    buf = jnp.concatenate(rows, axis=0)
    return jnp.pad(buf, ((0, SMALL_ROWS - buf.shape[0]), (0, 0)))


def _unpack_small(buf):
    n_pool = len(POOL_WINDOWS) * (POOL_W // len(POOL_WINDOWS)) ** 2 // D_MODEL
    g = POOL_W // len(POOL_WINDOWS)
    return (buf[0:1], buf[4:4 + n_pool].reshape(1, len(POOL_WINDOWS), g, g), buf[4 + n_pool:5 + n_pool, :POOL_W],
            buf[1:2], buf[2:3], buf[3:4], buf[5 + n_pool])


class _LocalStep:
    def __init__(self, x, tgt, g1, w_pool, pool_scale, g2, g3, g4):
        self.x, self.tgt, self.pool_scale = x, tgt, pool_scale
        self.g1, self.g2, self.g3, self.g4 = g1, g2, g3, g4
        self.cos, self.sin = _rope_tables(x.shape[0])
        self.wbd = _block_diag(w_pool).astype(BF16)

    def mixer_fwd(self, w_in_t, token):
        self.w_in_t = w_in_t
        self.h1, self.u, self.q, self.k, self.v = _proj_fwd(
            self.x, self.g1 + token[0, 0], w_in_t, self.cos, self.sin, 1024)
        self.pool = _pool_fwd(self.u, self.wbd, self.pool_scale, 1024)
        alone = [_attn_fwd(self.q, self.k, self.v, dil, []) for dil in DILATIONS[:-1]]
        self.attn, self.lse = _attn_fwd(self.q, self.k, self.v, DILATIONS[-1], alone)
        return self.attn

    def ffn_fwd_bwd(self, w_out, wg_t, wu_t, w_down):
        self.w_out, self.wg_t, self.wu_t = w_out, wg_t, wu_t
        self.cat, self.mix, self.x2, h2 = _mix_fwd(self.pool, self.attn, self.x, w_out, self.g2, self.g3, 1024)
        act_dgate, act_dup, act = _ffn_up(h2, wg_t, wu_t, 256)
        df, self.dy, self.dg4, self.loss = _ffn_down_loss(act, w_down, self.x2, self.g4, self.tgt, 512)
        self.dgate, self.dup = _ffn_act_bwd(df, w_down, act_dgate, act_dup, 512)
        return (_wgrad(self.dgate, h2, "wgrad_gate", D_FF // 2, 1024), _wgrad(self.dup, h2, "wgrad_up", D_FF // 2, 1024),
                _wgrad(act, df, "wgrad_down", D_FF // 2, 1024))

    def mixer_bwd(self, token):
        self.dx2, dmix, self.dg3, self.dg2 = _ffn_in_bwd(
            self.dgate, self.dup, self.wg_t, self.wu_t, self.x2, self.mix, self.dy, self.g3 + token[0, 0], self.g2, 512)
        dpool, dattn, dsum = _mix_bwd(dmix, self.w_out, self.attn, 1024)
        parts = [_attn_bwd(self.q, self.k, self.v, dattn, dsum, self.lse, dil) for dil in DILATIONS]
        du, dwbd, self.dscale = _pool_bwd(self.u, dpool, self.wbd, self.pool_scale, 1024)
        g = POOL_W // len(POOL_WINDOWS)
        self.dw_pool = jnp.stack([dwbd[i * g:(i + 1) * g, i * g:(i + 1) * g] for i in range(len(POOL_WINDOWS))])
        self.dproj = _dproj_combine(du, *[[p[j] for p in parts] for j in range(5)], self.cos, self.sin, 512)
        return _wgrad(self.dproj, self.h1, "wgrad_in", IN_W // 2, 1024), _wgrad(self.cat, dmix, "wgrad_out", D_MODEL, 1024)

    def input_bwd(self, token):
        grad_x, dg1 = _proj_bwd(self.dproj, self.w_in_t, self.x, self.dx2, self.g1 + token[0, 0], 1024)
        return self.loss, grad_x, (dg1, self.dw_pool, self.dscale, self.dg2, self.dg3, self.dg4)


def _local_step(x, tgt, g1, w_pool, pool_scale, g2, g3, g4, w_in_t, w_out, wg_t, wu_t, w_down):
    zero = jnp.zeros((8, LANES), F32)
    step = _LocalStep(x, tgt, g1, w_pool, pool_scale, g2, g3, g4)
    step.mixer_fwd(w_in_t, zero)
    dw_gate, dw_up, dw_down = step.ffn_fwd_bwd(w_out, wg_t, wu_t, w_down)
    dw_in, dw_out = step.mixer_bwd(zero)
    loss, grad_x, small = step.input_bwd(zero)
    return loss, grad_x, small, (dw_in, dw_out, dw_gate, dw_up, dw_down)


def kernel(x, ln_pre_mix, w_in, w_pool, pool_scale, w_out, ln_post_mix, ln_pre_ffn, w_gate, w_up, w_down, ln_post_ffn, loss_target, m_ln_pre_mix, m_w_in, m_w_pool, m_pool_scale, m_w_out, m_ln_post_mix, m_ln_pre_ffn, m_w_gate, m_w_up, m_w_down, m_ln_post_ffn, v_ln_pre_mix, v_w_in, v_w_pool, v_pool_scale, v_w_out, v_ln_post_mix, v_ln_pre_ffn, v_w_gate, v_w_up, v_w_down, v_ln_post_ffn):
    shards = [w_in[0].T.astype(BF16), w_out[0].astype(BF16), w_gate[0].T.astype(BF16),
              w_up[0].T.astype(BF16), w_down[0].astype(BF16)]
    flat = lambda a: a.reshape(-1, D_MODEL)
    blocks = lambda a: a.reshape(N_DEV, -1, D_MODEL)
    step = _LocalStep(x[0], loss_target[0], ln_pre_mix, w_pool[0], pool_scale, ln_post_mix, ln_pre_ffn, ln_post_ffn)

    w_in_t = flat(_gather_two_level(shards[0], "gather_w_in"))
    rest = _exchange_start(shards[1:], [False] * 4, w_in_t, "gather_rest_start")
    attn = step.mixer_fwd(w_in_t, rest[4])
    srcs, lands = _exchange_wait(rest, [False] * 4, attn, "gather_rest_wait")
    w_out_f, wg_t, wu_t, w_down_f = [flat(a) for a in _fill_own(lands, srcs, [False] * 4)]

    ffn = _exchange_start([blocks(a) for a in step.ffn_fwd_bwd(w_out_f, wg_t, wu_t, w_down_f)], [True] * 3,
                          step.dgate, "grads_ffn_start")
    mixer = _exchange_start([blocks(a) for a in step.mixer_bwd(ffn[4])], [True] * 2, step.dproj, "grads_mixer_start")
    loss, grad_x, small = step.input_bwd(mixer[4])
    got = []
    for handle, n_arr, nm in ((mixer, 2, "grads_mixer"), (ffn, 3, "grads_ffn")):
        srcs, lands = _exchange_wait(handle, [True] * n_arr, grad_x, nm + "_wait")
        got += _fill_own(lands, srcs, [True] * n_arr)
    sums = [_slot_sum(got[i], f"sum_grad_{i}", got[i].shape[1] // 2) for i in range(5)]

    small_buf = _pack_small(small[0], small[1], small[2], small[3], small[4], small[5], loss)
    small_sum = _slot_sum(_exchange([small_buf], [False], "gather_small")[0], "sum_small", SMALL_ROWS)

    g_in, g_out, g_gate, g_up, g_down = sums[0].T, sums[1], sums[2].T, sums[3].T, sums[4]
    upd = [_adamw(w[0], g, m[0], v[0], f"adamw_{nm}") for nm, w, g, m, v in (
        ("in", w_in, g_in, m_w_in, v_w_in), ("out", w_out, g_out, m_w_out, v_w_out),
        ("gate", w_gate, g_gate, m_w_gate, v_w_gate), ("up", w_up, g_up, m_w_up, v_w_up),
        ("down", w_down, g_down, m_w_down, v_w_down))]
    pack = lambda a, b, c, d, e, f: _pack_small(a, b[0], c, d, e, f, jnp.zeros((1,), F32))
    small_upd = _adamw(
        pack(ln_pre_mix, w_pool, pool_scale, ln_post_mix, ln_pre_ffn, ln_post_ffn), small_sum,
        pack(m_ln_pre_mix, m_w_pool, m_pool_scale, m_ln_post_mix, m_ln_pre_ffn, m_ln_post_ffn),
        pack(v_ln_pre_mix, v_w_pool, v_pool_scale, v_ln_post_mix, v_ln_pre_ffn, v_ln_post_ffn), "adamw_small")

    def tree(small6, big5):
        s1, spool, sscale, s2, s3, s4 = small6
        b_in, b_out, b_gate, b_up, b_down = [b[None] for b in big5]
        return [s1, b_in, spool, sscale, b_out, s2, s3, b_gate, b_up, b_down, s4]

    g_small = _unpack_small(small_sum)
    outs = [g_small[6][0], grad_x[None]]
    outs += tree(g_small[:6], [g_in, g_out, g_gate, g_up, g_down])
    for j in range(3):
        outs += tree(_unpack_small(small_upd[j])[:6], [u[j] for u in upd])
    return tuple(outs)
```

```python
import jax
import jax.numpy as jnp
from jax import lax
from jax.experimental import pallas as pl
from jax.experimental.pallas import tpu as pltpu

F32 = jnp.float32
BF16 = jnp.bfloat16

D_MODEL = 1024
POOL_W = 256
ATTN_W = 768
IN_W = 2560
D_FF = 2816
POOL_WINDOWS = (2, 4, 8, 16)
POOL_HALO = 16
DILATIONS = (1, 4, 16)
BLK = 128
LANES = 128
HEAD_DIM = 64
N_GROUPS = ATTN_W // LANES
ROPE_THETA = 10000.0
EPS = 1e-6
NEG = -1e30
N_DEV = 8
SMALL_ROWS = 24

ADAM_LR = 0.001
ADAM_B1 = 0.9
ADAM_B2 = 0.999
ADAM_EPS = 1e-08
ADAM_WD = 0.01
ADAM_STEP = 10

VMEM_LIMIT = 56 * 1024 * 1024


def _dot(a, b):
    return jnp.dot(a, b, preferred_element_type=F32)


def _dot_nt(a, b):
    return lax.dot_general(a, b, (((1,), (1,)), ((), ())), preferred_element_type=F32)


def _dot_tn(a, b):
    return lax.dot_general(a, b, (((0,), (0,)), ((), ())), preferred_element_type=F32)


def _params(n_grid):
    return pltpu.CompilerParams(dimension_semantics=("arbitrary",) * n_grid, vmem_limit_bytes=VMEM_LIMIT)


def _tok(tm, c):
    return pl.BlockSpec((tm, c), lambda i: (i, 0))


def _res(shape):
    return pl.BlockSpec(shape, lambda i: (0,) * len(shape), pipeline_mode=pl.Buffered(1))


def _acc(shape):
    return pl.BlockSpec(shape, lambda i: (0,) * len(shape))


def _rms_fwd(x, g):
    r = lax.rsqrt(jnp.mean(x * x, axis=-1, keepdims=True) + EPS)
    return x * r * g


def _rms_bwd(x, g, dy):
    r = lax.rsqrt(jnp.mean(x * x, axis=-1, keepdims=True) + EPS)
    xh = x * r
    gd = dy * g
    dx = r * (gd - xh * jnp.mean(gd * xh, axis=-1, keepdims=True))
    return dx, jnp.sum(dy * xh, axis=0, keepdims=True)


def _rope(x, c, s, sign):
    lane = lax.broadcasted_iota(jnp.int32, (x.shape[0], LANES), 1)
    first = (lane % HEAD_DIM) < (HEAD_DIM // 2)
    outs = []
    for g in range(x.shape[1] // LANES):
        xg = x[:, g * LANES:(g + 1) * LANES]
        rot = jnp.where(first, pltpu.roll(xg, LANES - HEAD_DIM // 2, 1), pltpu.roll(xg, HEAD_DIM // 2, 1))
        outs.append(xg * c + sign * (rot * s))
    return jnp.concatenate(outs, axis=1)


def _proj_fwd(x, g1, w_in_t, cos, sin, tm):
    T = x.shape[0]

    def body(x_ref, g_ref, w_ref, c_ref, s_ref, h_ref, u_ref, q_ref, k_ref, v_ref):
        h = _rms_fwd(x_ref[...], g_ref[...]).astype(BF16)
        h_ref[...] = h
        proj = _dot_nt(h, w_ref[...])
        c = c_ref[...]
        s = s_ref[...]
        u_ref[...] = proj[:, :POOL_W]
        _store_packed(q_ref, _rope(proj[:, POOL_W:POOL_W + ATTN_W], c, s, 1.0))
        _store_packed(k_ref, _rope(proj[:, POOL_W + ATTN_W:POOL_W + 2 * ATTN_W], c, s, 1.0))
        _store_packed(v_ref, proj[:, POOL_W + 2 * ATTN_W:])

    return pl.pallas_call(
        body, name="proj_fwd", grid=(T // tm,),
        in_specs=[_tok(tm, D_MODEL), _res((1, D_MODEL)), _res((IN_W, D_MODEL)), _tok(tm, LANES), _tok(tm, LANES)],
        out_specs=[_tok(tm, D_MODEL), _tok(tm, POOL_W)] + [_tok_packed(tm, ATTN_W)] * 3,
        out_shape=[jax.ShapeDtypeStruct((T, D_MODEL), BF16), jax.ShapeDtypeStruct((T, POOL_W), F32)]
        + [_packed(T, ATTN_W)] * 3,
        compiler_params=_params(1),
    )(x, g1, w_in_t, cos, sin)


def _pool_window(lane):
    return jnp.where(lane < 64, 2, jnp.where(lane < 128, 4, jnp.where(lane < 192, 8, 16)))


def _pool_select(lane, a2, a4, a8, a16):
    return jnp.where(lane < 64, a2, jnp.where(lane < 128, a4, jnp.where(lane < 192, a8, a16)))


def _pool_delta(cur, prev, i, tm):
    prev = jnp.where(i > 0, prev, 0.0)
    ext = jnp.concatenate([prev, cur], axis=0)
    s2 = ext + pltpu.roll(ext, 1, 0)
    s4 = s2 + pltpu.roll(s2, 2, 0)
    s8 = s4 + pltpu.roll(s4, 4, 0)
    s16 = s8 + pltpu.roll(s8, 8, 0)
    lane = lax.broadcasted_iota(jnp.int32, (tm, POOL_W), 1)
    row = lax.broadcasted_iota(jnp.int32, (tm, POOL_W), 0) + i * tm
    ws = _pool_select(lane, s2[POOL_HALO:], s4[POOL_HALO:], s8[POOL_HALO:], s16[POOL_HALO:])
    cnt = jnp.minimum(row + 1, _pool_window(lane)).astype(F32)
    return ws / cnt - cur


def _pool_fwd(u, wbd, scale, tm):
    T = u.shape[0]
    hb = tm // POOL_HALO

    def body(u_ref, prev_ref, w_ref, sc_ref, o_ref):
        d = _pool_delta(u_ref[...], prev_ref[...], pl.program_id(0), tm)
        o_ref[...] = (_dot(d.astype(BF16), w_ref[...]) * sc_ref[...]).astype(BF16)

    return pl.pallas_call(
        body, name="pool_fwd", grid=(T // tm,),
        in_specs=[_tok(tm, POOL_W), pl.BlockSpec((POOL_HALO, POOL_W), lambda i: (jnp.maximum(i * hb - 1, 0), 0)),
                  _res((POOL_W, POOL_W)), _res((1, POOL_W))],
        out_specs=_tok(tm, POOL_W),
        out_shape=jax.ShapeDtypeStruct((T, POOL_W), BF16),
        compiler_params=_params(1),
    )(u, u, wbd, scale)


def _attn_mask(has_prev):
    qi = lax.broadcasted_iota(jnp.int32, (BLK, 2 * BLK), 0)
    kj = lax.broadcasted_iota(jnp.int32, (BLK, 2 * BLK), 1)
    dist = qi + BLK - kj
    return (dist >= 0) & (dist <= BLK) & ((kj >= BLK) | has_prev)


def _stack_heads(x, lo):
    zero = jnp.zeros_like(x)
    return jnp.concatenate([jnp.where(lo, x, zero), jnp.where(lo, zero, x)], axis=0)


def _head_col(tile, lane, h):
    return jnp.sum(jnp.where(lane == h, tile, 0.0), axis=1, keepdims=True)


def _attn_cols(dil):
    return ATTN_W // 2 if dil >= 16 else ATTN_W


def _units_per_step(dil, backward):
    return {1: 4, 4: 2, 16: 8}[dil]


def _chunk_tokens(dil, units):
    return BLK * (units if dil == 1 else dil)


def _unit_steps(dil, units):
    return 1 if dil == 1 else dil // 2 // units


def _attn_specs(dil, nb, units):
    cw = _attn_cols(dil)
    ch = _chunk_tokens(dil, units)
    wide = lambda f: pl.BlockSpec((cw // LANES, ch // 2, LANES), f)
    cur = lambda n, j, r: (j, n, 0)
    prv = lambda n, j, r: (j, jnp.maximum(n - 1, 0), 0)
    prv_out = lambda n, j, r: (j, (n + nb - 1) % nb, 0)
    heads = pl.BlockSpec((ch, LANES), lambda n, j, r: (n, 0))
    return cw, wide(cur), wide(prv), wide(prv_out), heads


HIGH_HALF = 0xFFFF0000


def _pack(x):
    return pltpu.bitcast(x.astype(BF16), F32)


def _unpack(words):
    return pltpu.bitcast(words, BF16)


def _packed(rows, cols):
    return jax.ShapeDtypeStruct((cols // LANES, rows // 2, LANES), F32)


def _tok_packed(tm, cols):
    return pl.BlockSpec((cols // LANES, tm // 2, LANES), lambda i: (0, i, 0))


def _store_packed(ref, x):
    for g in range(x.shape[1] // LANES):
        ref[g] = _pack(x[:, g * LANES:(g + 1) * LANES])


def _load_packed(ref):
    return jnp.concatenate([_unpack(ref[g]) for g in range(ref.shape[0])], axis=1)


def _load_streams(ref, dil, r2, sl):
    if dil == 1:
        return [_unpack(ref.at[sl][pl.ds(r2 * (BLK // 2), BLK // 2), :])]
    words = lax.bitcast_convert_type(ref.at[sl][pl.ds(r2, BLK, stride=dil // 2), :], jnp.uint32)
    even = lax.bitcast_convert_type(words << 16, F32).astype(BF16)
    odd = lax.bitcast_convert_type(words & jnp.uint32(HIGH_HALF), F32).astype(BF16)
    return [even, odd]


def _load_prev_streams(prev_ref, cur_ref, dil, units, r2, sl):
    if dil > 1:
        return _load_streams(prev_ref, dil, r2, sl)
    return _load_streams(cur_ref, 1, r2 - 1, sl) if r2 > 0 else _load_streams(prev_ref, 1, units - 1, sl)


def _load_streams_f32(ref, dil, r2, sl):
    ref = ref if sl is None else ref.at[sl]
    if dil == 1:
        return [ref[pl.ds(r2 * BLK, BLK), :]]
    return [ref[pl.ds(2 * r2 + e, BLK, stride=dil), :] for e in range(2)]


def _store_streams_f32(ref, dil, r2, sl, tiles):
    ref = ref if sl is None else ref.at[sl]
    if dil == 1:
        ref[pl.ds(r2 * BLK, BLK), :] = tiles[0]
    else:
        for e, t in enumerate(tiles):
            ref[pl.ds(2 * r2 + e, BLK, stride=dil), :] = t


def _store_streams(ref, dil, r2, sl, tiles):
    if dil == 1:
        ref.at[sl][pl.ds(r2 * (BLK // 2), BLK // 2), :] = _pack(tiles[0])
    else:
        even, odd = [lax.bitcast_convert_type(t.astype(BF16).astype(F32), jnp.uint32) for t in tiles]
        words = (odd & jnp.uint32(HIGH_HALF)) | (even >> 16)
        ref.at[sl][pl.ds(r2, BLK, stride=dil // 2), :] = lax.bitcast_convert_type(words, F32)


def _attn_fwd(q, k, v, dil, others):
    T = 2 * q.shape[1]
    reps = _units_per_step(dil, False)
    nb = T // _chunk_tokens(dil, reps)
    first = not others
    cw, cur, prv, _, heads = _attn_specs(dil, nb, reps)
    ncb = ATTN_W // cw
    heads_per_step = cw // HEAD_DIM
    n_str = min(dil, 2)
    everything = None

    def body(*refs):
        q_ref, kc_ref, kp_ref, vc_ref, vp_ref = refs[:5]
        acc_ins, lse_ins = refs[5:5 + 2 * len(others):2], refs[6:6 + 2 * len(others):2]
        acc_ref, lse_ref = refs[-2:]
        j = pl.program_id(1)
        lane = lax.broadcasted_iota(jnp.int32, (BLK, LANES), 1)
        lo = lane < HEAD_DIM

        def stream_pair(r2):
            valid = _attn_mask(True if dil == 1 and r2 > 0 else pl.program_id(0) > 0)
            lse_tiles = [jnp.zeros((BLK, LANES), F32) for _ in range(n_str)]
            own = []
            for g in range(cw // LANES):
                qs, kcs, vcs = [_load_streams(r, dil, r2, g) for r in (q_ref, kc_ref, vc_ref)]
                kps, vps = [_load_prev_streams(p, c, dil, reps, r2, g) for p, c in ((kp_ref, kc_ref), (vp_ref, vc_ref))]
                pairs = []
                for e in range(n_str):
                    qg = qs[e] * 0.125
                    kcat = jnp.concatenate([kps[e], kcs[e]], axis=0)
                    vcat = jnp.concatenate([vps[e], vcs[e]], axis=0)
                    pair = None
                    for hh in range(2):
                        h = j * heads_per_step + 2 * g + hh
                        hm = lo if hh == 0 else jnp.logical_not(lo)
                        s = _dot_nt(jnp.where(hm, qg, jnp.zeros_like(qg)), kcat)
                        s = jnp.where(valid, s, NEG)
                        m = jnp.max(s, axis=1, keepdims=True)
                        p = jnp.exp(s - m)
                        den = jnp.sum(p, axis=1, keepdims=True)
                        o = _dot(p.astype(BF16), vcat) / den
                        pair = o if hh == 0 else jnp.where(lo, pair, o)
                        lse_tiles[e] = jnp.where(lane == h, m + jnp.log(den), lse_tiles[e])
                    pairs.append(pair)
                if first:
                    _store_streams(acc_ref, dil, r2, g, pairs)
                else:
                    own.append(pairs)
            if not first:
                mine = (lane >= j * heads_per_step) & (lane < (j + 1) * heads_per_step)
                theirs = [_load_streams_f32(ref, dil, r2, everything) for ref in lse_ins]
                w_theirs, w_own = [[] for _ in others], []
                for e in range(n_str):
                    parts = [t[e] for t in theirs] + [lse_tiles[e]]
                    mx = parts[0]
                    for part in parts[1:]:
                        mx = jnp.maximum(mx, part)
                    total = mx + jnp.log(sum(jnp.exp(part - mx) for part in parts))
                    for i, t in enumerate(theirs):
                        w_theirs[i].append(jnp.exp(t[e] - total))
                    w_own.append(jnp.exp(lse_tiles[e] - total))
                    lse_tiles[e] = jnp.where(mine, total, 0.0)
                for g in range(cw // LANES):
                    h0 = j * heads_per_step + 2 * g
                    spread = lambda w: jnp.where(lo, _head_col(w, lane, h0), _head_col(w, lane, h0 + 1))
                    olds = [_load_streams(ref, dil, r2, g) for ref in acc_ins]
                    _store_streams(acc_ref, dil, r2, g, [
                        sum(olds[i][e].astype(F32) * spread(w_theirs[i][e]) for i in range(len(others)))
                        + own[g][e] * spread(w_own[e]) for e in range(n_str)])
            if ncb == 1:
                _store_streams_f32(lse_ref, dil, r2, everything, lse_tiles)
            else:
                @pl.when(j == 0)
                def _():
                    _store_streams_f32(lse_ref, dil, r2, everything, lse_tiles)

                @pl.when(j > 0)
                def _():
                    before = _load_streams_f32(lse_ref, dil, r2, everything)
                    _store_streams_f32(lse_ref, dil, r2, everything, [a + b for a, b in zip(before, lse_tiles)])

        for rep in range(reps):
            stream_pair(rep if dil == 1 else pl.program_id(2) * reps + rep)

    ins = [q, k, k, v, v]
    in_specs = [cur, cur, prv, cur, prv]
    for acc, lse in others:
        ins += [acc, lse]
        in_specs += [cur, heads]
    return pl.pallas_call(
        body, name=f"attn_fwd_d{dil}", grid=(nb, ncb, _unit_steps(dil, reps)),
        in_specs=in_specs, out_specs=[cur, heads],
        out_shape=[_packed(T, ATTN_W), jax.ShapeDtypeStruct((T, LANES), F32)],
        compiler_params=_params(3),
    )(*ins)


def _mix_fwd(pool, attn, x, w_out, g2, g3, tm):
    T = x.shape[0]

    def body(p_ref, a_ref, x_ref, w_ref, g2_ref, g3_ref, cat_ref, mix_ref, x2_ref, h2_ref):
        p = p_ref[...]
        a = _load_packed(a_ref)
        cat_ref[...] = jnp.concatenate([p, a], axis=1)
        mix = _dot(p, w_ref[:POOL_W, :]) + _dot(a, w_ref[POOL_W:, :])
        mix_ref[...] = mix
        x2 = x_ref[...] + _rms_fwd(mix, g2_ref[...])
        x2_ref[...] = x2
        h2_ref[...] = _rms_fwd(x2, g3_ref[...]).astype(BF16)

    return pl.pallas_call(
        body, name="mix_fwd", grid=(T // tm,),
        in_specs=[_tok(tm, POOL_W), _tok_packed(tm, ATTN_W), _tok(tm, D_MODEL), _res((D_MODEL, D_MODEL)),
                  _res((1, D_MODEL)), _res((1, D_MODEL))],
        out_specs=[_tok(tm, D_MODEL)] * 4,
        out_shape=[jax.ShapeDtypeStruct((T, D_MODEL), BF16), jax.ShapeDtypeStruct((T, D_MODEL), F32),
                   jax.ShapeDtypeStruct((T, D_MODEL), F32), jax.ShapeDtypeStruct((T, D_MODEL), BF16)],
        compiler_params=_params(1),
    )(pool, attn, x, w_out, g2, g3)


def _ffn_up(h2, wg_t, wu_t, tm):
    T = h2.shape[0]

    def body(h_ref, wg_ref, wu_ref, dg_ref, du_ref, a_ref):
        h = h_ref[...]
        gate = _dot_nt(h, wg_ref[...])
        up = _dot_nt(h, wu_ref[...])
        sg = 1.0 / (1.0 + jnp.exp(-gate))
        silu = gate * sg
        a_ref[...] = (silu * up).astype(BF16)
        dg_ref[...] = (up * (sg * (1.0 + gate * (1.0 - sg)))).astype(BF16)
        du_ref[...] = silu.astype(BF16)

    return pl.pallas_call(
        body, name="ffn_up", grid=(T // tm,),
        in_specs=[_tok(tm, D_MODEL), _res((D_FF, D_MODEL)), _res((D_FF, D_MODEL))],
        out_specs=[_tok(tm, D_FF)] * 3,
        out_shape=[jax.ShapeDtypeStruct((T, D_FF), BF16)] * 3,
        compiler_params=_params(1),
    )(h2, wg_t, wu_t)


def _ffn_down_loss(act, w_down, x2, g4, tgt, tm):
    T = act.shape[0]

    def body(a_ref, w_ref, x2_ref, g_ref, t_ref, df_ref, dy_ref, dg_ref, loss_ref):
        i = pl.program_id(0)

        @pl.when(i == 0)
        def _():
            dg_ref[...] = jnp.zeros_like(dg_ref)
            loss_ref[...] = jnp.zeros_like(loss_ref)

        f = _dot(a_ref[...], w_ref[...])
        g = g_ref[...]
        err = x2_ref[...] + _rms_fwd(f, g) - t_ref[...]
        loss_ref[...] += 0.5 * jnp.sum(jnp.mean(err * err, axis=-1, keepdims=True), axis=0, keepdims=True)
        dy = err * (1.0 / D_MODEL)
        dy_ref[...] = dy
        df, dg = _rms_bwd(f, g, dy)
        dg_ref[...] += dg
        df_ref[...] = df.astype(BF16)

    return pl.pallas_call(
        body, name="ffn_down_loss", grid=(T // tm,),
        in_specs=[_tok(tm, D_FF), _res((D_FF, D_MODEL)), _tok(tm, D_MODEL), _res((1, D_MODEL)), _tok(tm, D_MODEL)],
        out_specs=[_tok(tm, D_MODEL), _tok(tm, D_MODEL), _acc((1, D_MODEL)), _acc((1, 1))],
        out_shape=[jax.ShapeDtypeStruct((T, D_MODEL), BF16), jax.ShapeDtypeStruct((T, D_MODEL), F32),
                   jax.ShapeDtypeStruct((1, D_MODEL), F32), jax.ShapeDtypeStruct((1, 1), F32)],
        compiler_params=_params(1),
    )(act, w_down, x2, g4, tgt)


def _ffn_bwd(df, w_down, act_dgate, act_dup, wg_t, wu_t, x2, mix, dy, g3, g2, tm):
    T = x2.shape[0]

    def body(df_ref, wd_ref, ag_ref, au_ref, wg_ref, wu_ref, x2_ref, mix_ref, dy_ref, g3_ref, g2_ref,
             dgate_ref, dup_ref, dx2_ref, dmix_ref, dg3_ref, dg2_ref):
        @pl.when(pl.program_id(0) == 0)
        def _():
            dg3_ref[...] = jnp.zeros_like(dg3_ref)
            dg2_ref[...] = jnp.zeros_like(dg2_ref)

        dact = _dot_nt(df_ref[...], wd_ref[...])
        dgate = (dact * ag_ref[...].astype(F32)).astype(BF16)
        dup = (dact * au_ref[...].astype(F32)).astype(BF16)
        dgate_ref[...] = dgate
        dup_ref[...] = dup
        dh2 = _dot(dgate, wg_ref[...]) + _dot(dup, wu_ref[...])
        dn, dg3 = _rms_bwd(x2_ref[...], g3_ref[...], dh2)
        dx2 = dy_ref[...] + dn
        dx2_ref[...] = dx2
        dg3_ref[...] += dg3
        dmix, dg2 = _rms_bwd(mix_ref[...], g2_ref[...], dx2)
        dg2_ref[...] += dg2
        dmix_ref[...] = dmix.astype(BF16)

    wide, narrow, weight, gain = _tok(tm, D_FF), _tok(tm, D_MODEL), _res((D_FF, D_MODEL)), _res((1, D_MODEL))
    return pl.pallas_call(
        body, name="ffn_bwd", grid=(T // tm,),
        in_specs=[narrow, weight, wide, wide, weight, weight, narrow, narrow, narrow, gain, gain],
        out_specs=[wide, wide, narrow, narrow, _acc((1, D_MODEL)), _acc((1, D_MODEL))],
        out_shape=[jax.ShapeDtypeStruct((T, D_FF), BF16), jax.ShapeDtypeStruct((T, D_FF), BF16),
                   jax.ShapeDtypeStruct((T, D_MODEL), F32), jax.ShapeDtypeStruct((T, D_MODEL), BF16),
                   jax.ShapeDtypeStruct((1, D_MODEL), F32), jax.ShapeDtypeStruct((1, D_MODEL), F32)],
        compiler_params=_params(1),
    )(df, w_down, act_dgate, act_dup, wg_t, wu_t, x2, mix, dy, g3, g2)


def _mix_bwd(dmix, w_out, attn, after, tm):
    T = dmix.shape[0]

    def body(d_ref, w_ref, a_ref, after_ref, dp_ref, da_ref, ds_ref):
        dcat = _dot_nt(d_ref[...], w_ref[...])
        dp_ref[...] = dcat[:, :POOL_W].astype(BF16)
        dattn = dcat[:, POOL_W:].astype(BF16)
        _store_packed(da_ref, dattn)
        prod = dattn.astype(F32) * _load_packed(a_ref).astype(F32)
        lane = lax.broadcasted_iota(jnp.int32, (tm, LANES), 1)
        lo = lane < HEAD_DIM
        dsum = jnp.zeros((tm, LANES), F32)
        for g in range(N_GROUPS):
            pg = prod[:, g * LANES:(g + 1) * LANES]
            dsum = jnp.where(lane == 2 * g, jnp.sum(jnp.where(lo, pg, 0.0), axis=1, keepdims=True), dsum)
            dsum = jnp.where(lane == 2 * g + 1, jnp.sum(jnp.where(lo, 0.0, pg), axis=1, keepdims=True), dsum)
        ds_ref[...] = dsum

    return pl.pallas_call(
        body, name="mix_bwd", grid=(T // tm,),
        in_specs=[_tok(tm, D_MODEL), _res((D_MODEL, D_MODEL)), _tok_packed(tm, ATTN_W), pl.BlockSpec(memory_space=pl.ANY)],
        out_specs=[_tok(tm, POOL_W), _tok_packed(tm, ATTN_W), _tok(tm, LANES)],
        out_shape=[jax.ShapeDtypeStruct((T, POOL_W), BF16), _packed(T, ATTN_W), jax.ShapeDtypeStruct((T, LANES), F32)],
        compiler_params=_params(1),
    )(dmix, w_out, attn, after)


def _attn_bwd(q, k, v, dout, dsum, lse, dil):
    T = 2 * q.shape[1]
    reps = _units_per_step(dil, True)
    nb = T // _chunk_tokens(dil, reps)
    cw, cur, prv, prv_out, heads = _attn_specs(dil, nb, reps)
    ncb = ATTN_W // cw
    heads_per_step = cw // HEAD_DIM
    n_str = min(dil, 2)

    def body(q_ref, kc_ref, kp_ref, vc_ref, vp_ref, do_ref, dsum_ref, lse_ref,
             dq_ref, dkc_ref, dkp_ref, dvc_ref, dvp_ref):
        j = pl.program_id(1)
        lane = lax.broadcasted_iota(jnp.int32, (BLK, LANES), 1)
        lo = lane < HEAD_DIM
        unit = lambda rep: rep if dil == 1 else pl.program_id(2) * reps + rep
        stats = [[_load_streams_f32(ref, dil, unit(rep), None) for ref in (lse_ref, dsum_ref)] for rep in range(reps)]

        def unit_grads(rep, g):
            r2 = unit(rep)
            valid = _attn_mask(True if dil == 1 and r2 > 0 else pl.program_id(0) > 0)
            valid2 = jnp.concatenate([valid, valid], axis=0)
            lse_tiles, dsum_tiles = stats[rep]
            qs, kcs, vcs, dos = [_load_streams(r, dil, r2, g) for r in (q_ref, kc_ref, vc_ref, do_ref)]
            kps, vps = [_load_prev_streams(p, c, dil, reps, r2, g) for p, c in ((kp_ref, kc_ref), (vp_ref, vc_ref))]
            dqs, dks, dvs = [], [], []
            for e in range(n_str):
                qg = qs[e] * 0.125
                dog = dos[e]
                kcat = jnp.concatenate([kps[e], kcs[e]], axis=0)
                vcat = jnp.concatenate([vps[e], vcs[e]], axis=0)
                h0 = j * heads_per_step + 2 * g
                q2 = _stack_heads(qg, lo)
                do2 = _stack_heads(dog, lo)
                both = lambda tile: jnp.concatenate([_head_col(tile, lane, h0), _head_col(tile, lane, h0 + 1)], axis=0)
                lse2, dsum2 = both(lse_tiles[e]), both(dsum_tiles[e])
                p = jnp.exp(jnp.where(valid2, _dot_nt(q2, kcat), NEG) - lse2)
                ds = (p * (_dot_nt(do2, vcat) - dsum2)).astype(BF16)
                dvs.append(_dot_tn(p.astype(BF16), do2))
                dks.append(_dot_tn(ds, q2))
                dq2 = _dot(ds, kcat) * 0.125
                dqs.append(jnp.where(lo, dq2[:BLK], dq2[BLK:]))
            return dqs, dks, dvs

        for g in range(cw // LANES):
            if dil > 1:
                for rep in range(reps):
                    r2 = unit(rep)
                    dqs, dks, dvs = unit_grads(rep, g)
                    _store_streams(dq_ref, dil, r2, g, dqs)
                    _store_streams(dkp_ref, dil, r2, g, [t[:BLK] for t in dks])
                    _store_streams(dkc_ref, dil, r2, g, [t[BLK:] for t in dks])
                    _store_streams(dvp_ref, dil, r2, g, [t[:BLK] for t in dvs])
                    _store_streams(dvc_ref, dil, r2, g, [t[BLK:] for t in dvs])
            else:
                blocks = [unit_grads(b, g) for b in range(reps)]
                for b, (dqs, dks, dvs) in enumerate(blocks):
                    _store_streams(dq_ref, 1, b, g, dqs)
                    for cur_ref, prev_ref, which in ((dkc_ref, dkp_ref, 1), (dvc_ref, dvp_ref, 2)):
                        own = blocks[b][which][0][BLK:]
                        if b + 1 < reps:
                            own = own + blocks[b + 1][which][0][:BLK]
                        _store_streams(cur_ref, 1, b, g, [own])
                        edge = blocks[0][which][0][:BLK] if b == reps - 1 else jnp.zeros((BLK, LANES), F32)
                        _store_streams(prev_ref, 1, b, g, [edge])

    return pl.pallas_call(
        body, name=f"attn_bwd_d{dil}", grid=(nb, ncb, _unit_steps(dil, reps)),
        in_specs=[cur, cur, prv, cur, prv, cur, heads, heads],
        out_specs=[cur, cur, prv_out, cur, prv_out],
        out_shape=[_packed(T, ATTN_W)] * 5,
        compiler_params=_params(3),
    )(q, k, k, v, v, dout, dsum, lse)


def _pool_bwd(u, dy, wbd, scale, tm):
    T = u.shape[0]
    nt = T // tm
    hb = tm // POOL_HALO

    def body(u_ref, prev_ref, dy_ref, next_ref, w_ref, sc_ref, du_ref, dw_ref, dsc_ref):
        i = pl.program_id(0)

        @pl.when(i == 0)
        def _():
            dw_ref[...] = jnp.zeros_like(dw_ref)
            dsc_ref[...] = jnp.zeros_like(dsc_ref)

        w = w_ref[...]
        sc = sc_ref[...]
        d = _pool_delta(u_ref[...], prev_ref[...], i, tm).astype(BF16)
        dyc = dy_ref[...].astype(F32)
        dsc_ref[...] += jnp.sum(dyc * _dot(d, w), axis=0, keepdims=True)
        nxt = jnp.where(i < nt - 1, next_ref[...].astype(F32), 0.0)
        dypre = (jnp.concatenate([dyc, nxt], axis=0) * sc).astype(BF16)
        dw_ref[...] += _dot_tn(d, dypre[:tm])
        dd = _dot_nt(dypre, w)
        n = tm + POOL_HALO
        lane = lax.broadcasted_iota(jnp.int32, (n, POOL_W), 1)
        row = lax.broadcasted_iota(jnp.int32, (n, POOL_W), 0) + i * tm
        gx = dd / jnp.minimum(row + 1, _pool_window(lane)).astype(F32)
        a2 = gx + pltpu.roll(gx, n - 1, 0)
        a4 = a2 + pltpu.roll(a2, n - 2, 0)
        a8 = a4 + pltpu.roll(a4, n - 4, 0)
        a16 = a8 + pltpu.roll(a8, n - 8, 0)
        fs = _pool_select(lane[:tm], a2[:tm], a4[:tm], a8[:tm], a16[:tm])
        du_ref[...] = (fs - dd[:tm]).astype(BF16)

    return pl.pallas_call(
        body, name="pool_bwd", grid=(nt,),
        in_specs=[_tok(tm, POOL_W), pl.BlockSpec((POOL_HALO, POOL_W), lambda i: (jnp.maximum(i * hb - 1, 0), 0)),
                  _tok(tm, POOL_W), pl.BlockSpec((POOL_HALO, POOL_W), lambda i: (jnp.minimum((i + 1) * hb, nt * hb - 1), 0)),
                  _res((POOL_W, POOL_W)), _res((1, POOL_W))],
        out_specs=[_tok(tm, POOL_W), _acc((POOL_W, POOL_W)), _acc((1, POOL_W))],
        out_shape=[jax.ShapeDtypeStruct((T, POOL_W), BF16), jax.ShapeDtypeStruct((POOL_W, POOL_W), F32),
                   jax.ShapeDtypeStruct((1, POOL_W), F32)],
        compiler_params=_params(1),
    )(u, u, dy, dy, wbd, scale)


def _dproj_combine(du, dqs, dkcs, dkps, dvcs, dvps, cos, sin, tm):
    T = du.shape[0]
    n_cfg = len(dqs)

    def body(*refs):
        du_ref = refs[0]
        groups = [refs[1 + j * n_cfg:1 + (j + 1) * n_cfg] for j in range(5)]
        c_ref, s_ref, out_ref = refs[1 + 5 * n_cfg:]
        tot = lambda rs: sum(_load_packed(r).astype(F32) for r in rs)
        c = c_ref[...]
        s = s_ref[...]
        dq = _rope(tot(groups[0]), c, s, -1.0)
        dk = _rope(tot(groups[1]) + tot(groups[2]), c, s, -1.0)
        dv = tot(groups[3]) + tot(groups[4])
        out_ref[...] = jnp.concatenate([du_ref[...], dq.astype(BF16), dk.astype(BF16), dv.astype(BF16)], axis=1)

    return pl.pallas_call(
        body, name="dproj_combine", grid=(T // tm,),
        in_specs=[_tok(tm, POOL_W)] + [_tok_packed(tm, ATTN_W)] * (5 * n_cfg) + [_tok(tm, LANES)] * 2,
        out_specs=_tok(tm, IN_W),
        out_shape=jax.ShapeDtypeStruct((T, IN_W), BF16),
        compiler_params=_params(1),
    )(du, *dqs, *dkcs, *dkps, *dvcs, *dvps, cos, sin)


def _proj_bwd(dproj, w_in_t, x, dx2, g1, tm):
    T = x.shape[0]

    def body(d_ref, w_ref, x_ref, r_ref, g_ref, dx_ref, dg_ref):
        @pl.when(pl.program_id(0) == 0)
        def _():
            dg_ref[...] = jnp.zeros_like(dg_ref)

        dn, dg = _rms_bwd(x_ref[...], g_ref[...], _dot(d_ref[...], w_ref[...]))
        dg_ref[...] += dg
        dx_ref[...] = r_ref[...] + dn

    return pl.pallas_call(
        body, name="proj_bwd", grid=(T // tm,),
        in_specs=[_tok(tm, IN_W), _res((IN_W, D_MODEL)), _tok(tm, D_MODEL), _tok(tm, D_MODEL), _res((1, D_MODEL))],
        out_specs=[_tok(tm, D_MODEL), _acc((1, D_MODEL))],
        out_shape=[jax.ShapeDtypeStruct((T, D_MODEL), F32), jax.ShapeDtypeStruct((1, D_MODEL), F32)],
        compiler_params=_params(1),
    )(dproj, w_in_t, x, dx2, g1)


def _wgrad(a, b, name, tile_m, tk):
    T, M = a.shape
    N = b.shape[1]
    nk = T // tk

    def body(a_ref, b_ref, o_ref, acc_ref):
        kk = pl.program_id(1)

        @pl.when(kk == 0)
        def _():
            acc_ref[...] = jnp.zeros_like(acc_ref)

        acc_ref[...] += _dot_tn(a_ref[...], b_ref[...])

        @pl.when(kk == nk - 1)
        def _():
            o_ref[...] = acc_ref[...].astype(BF16)

    return pl.pallas_call(
        body, name=name, grid=(M // tile_m, nk),
        in_specs=[pl.BlockSpec((tk, tile_m), lambda j, kk: (kk, j)), pl.BlockSpec((tk, N), lambda j, kk: (kk, 0))],
        out_specs=pl.BlockSpec((tile_m, N), lambda j, kk: (j, 0)),
        out_shape=jax.ShapeDtypeStruct((M, N), BF16),
        scratch_shapes=[pltpu.VMEM((tile_m, N), F32)],
        compiler_params=_params(2),
    )(a, b)


def _exchange(arrs, scatter, name):
    n = len(arrs)
    out_shapes = [jax.ShapeDtypeStruct((N_DEV,) + (a.shape[1:] if sc else a.shape), a.dtype)
                  for a, sc in zip(arrs, scatter)]

    def body(*refs):
        ins, outs = refs[:n], refs[n:2 * n]
        send_sems, recv_sems, loc_sems = refs[2 * n:]
        x, y, c = lax.axis_index("x"), lax.axis_index("y"), lax.axis_index("c")
        me = 4 * x + 2 * y + c
        local, sends, recvs = [], [], []
        for i in range(n):
            own = ins[i].at[me] if scatter[i] else ins[i]
            loc = pltpu.make_async_copy(own, outs[i].at[me], loc_sems.at[i])
            loc.start()
            local.append(loc)
            for kbits in range(1, N_DEV):
                px = 1 - x if kbits & 4 else x
                py = 1 - y if kbits & 2 else y
                pc = 1 - c if kbits & 1 else c
                pid = 4 * px + 2 * py + pc
                src = ins[i].at[pid] if scatter[i] else ins[i]
                cp = pltpu.make_async_remote_copy(
                    src_ref=src, dst_ref=outs[i].at[me],
                    send_sem=send_sems.at[i, kbits - 1], recv_sem=recv_sems.at[i, kbits - 1],
                    device_id=(px, py, pc), device_id_type=pl.DeviceIdType.MESH)
                cp.start()
                sends.append(cp)
                recvs.append(pltpu.make_async_remote_copy(
                    src_ref=src, dst_ref=outs[i].at[pid],
                    send_sem=send_sems.at[i, kbits - 1], recv_sem=recv_sems.at[i, kbits - 1],
                    device_id=(px, py, pc), device_id_type=pl.DeviceIdType.MESH))
        for cp in recvs:
            cp.wait_recv()
        for cp in sends:
            cp.wait_send()
        for cp in local:
            cp.wait()

    hbm = pl.BlockSpec(memory_space=pl.ANY)
    return pl.pallas_call(
        body, name=name, in_specs=[hbm] * n, out_specs=[hbm] * n, out_shape=out_shapes,
        scratch_shapes=[pltpu.SemaphoreType.DMA((n, N_DEV - 1)), pltpu.SemaphoreType.DMA((n, N_DEV - 1)),
                        pltpu.SemaphoreType.DMA((n,))],
    )(*arrs)


def _gather_two_level(arr, name):
    def body(x_ref, out_ref, send_sems, recv_sems, local_sem):
        x, y, c = lax.axis_index("x"), lax.axis_index("y"), lax.axis_index("c")
        me, sibling = (x, y, c), (x, y, 1 - c)
        chips = [(1 - x, y), (x, 1 - y), (1 - x, 1 - y)]
        slot = lambda px, py, pc: out_ref.at[4 * px + 2 * py + pc]

        def copy(k, block, to, src=None):
            return pltpu.make_async_remote_copy(
                src_ref=slot(*block) if src is None else src, dst_ref=slot(*block),
                send_sem=send_sems.at[k], recv_sem=recv_sems.at[k],
                device_id=to, device_id_type=pl.DeviceIdType.MESH)

        mine = pltpu.make_async_copy(x_ref, slot(*me), local_sem)
        mine.start()
        first = [copy(0, me, sibling, src=x_ref)]
        first += [copy(1 + i, me, (*chip, c), src=x_ref) for i, chip in enumerate(chips)]
        for cp in first:
            cp.start()
        passed = [copy(4 + i, (*chip, c), sibling) for i, chip in enumerate(chips)]
        for i, chip in enumerate(chips):
            copy(1 + i, (*chip, c), me).wait_recv()
            passed[i].start()
        copy(0, sibling, me).wait_recv()
        for i, chip in enumerate(chips):
            copy(4 + i, (*chip, 1 - c), me).wait_recv()
        for cp in first + passed:
            cp.wait_send()
        mine.wait()

    hbm = pl.BlockSpec(memory_space=pl.ANY)
    return pl.pallas_call(
        body, name=name, in_specs=[hbm], out_specs=hbm,
        out_shape=jax.ShapeDtypeStruct((N_DEV,) + arr.shape, arr.dtype),
        scratch_shapes=[pltpu.SemaphoreType.DMA((N_DEV - 1,)), pltpu.SemaphoreType.DMA((N_DEV - 1,)),
                        pltpu.SemaphoreType.DMA],
    )(arr)


def _peers(x, y, c):
    for kbits in range(1, N_DEV):
        px = 1 - x if kbits & 4 else x
        py = 1 - y if kbits & 2 else y
        pc = 1 - c if kbits & 1 else c
        yield kbits - 1, (px, py, pc), 4 * px + 2 * py + pc


def _peer_copies(ins, lands, scatter, send_sems, recv_sems, incoming):
    x, y, c = lax.axis_index("x"), lax.axis_index("y"), lax.axis_index("c")
    me = 4 * x + 2 * y + c
    copies = []
    for i in range(len(ins)):
        for k, peer, pid in _peers(x, y, c):
            slot = i * (N_DEV - 1) + k
            copies.append(pltpu.make_async_remote_copy(
                src_ref=ins[i].at[pid] if scatter[i] else ins[i], dst_ref=lands[i].at[pid if incoming else me],
                send_sem=send_sems.at[slot], recv_sem=recv_sems.at[slot],
                device_id=peer, device_id_type=pl.DeviceIdType.MESH))
    return copies


_HBM = pl.BlockSpec(memory_space=pltpu.HBM)
_SEM = pl.BlockSpec(memory_space=pltpu.SEMAPHORE)
_DATAFLOW = pltpu.SideEffectType.DATAFLOW_SIDE_EFFECTING


def _exchange_start(arrs, scatter, after, name):
    n = len(arrs)
    lands = [lax.empty((N_DEV,) + (a.shape[1:] if sc else a.shape), a.dtype) for a, sc in zip(arrs, scatter)]

    def body(*refs):
        ins, lz = refs[:n], refs[n:2 * n]
        send_sems, recv_sems = refs[2 * n + 1:2 * n + 3]
        token = refs[-1]
        for cp in _peer_copies(ins, lz, scatter, send_sems, recv_sems, False):
            cp.start()
        token[...] = jnp.zeros_like(token)

    sem_shape = pltpu.SemaphoreType.DMA((n * (N_DEV - 1),))
    outs = pl.pallas_call(
        body, name=name,
        out_shape=(sem_shape, sem_shape, *[pltpu.HBM(a.shape, a.dtype) for a in arrs + lands],
                   jax.ShapeDtypeStruct((8, LANES), F32)),
        in_specs=[_HBM] * (2 * n) + [pl.BlockSpec(memory_space=pl.ANY)],
        out_specs=(_SEM, _SEM, *[_HBM] * (2 * n), pl.BlockSpec(memory_space=pltpu.VMEM)),
        input_output_aliases={i: 2 + i for i in range(2 * n)},
        compiler_params=pltpu.CompilerParams(has_side_effects=_DATAFLOW),
    )(*[pltpu.with_memory_space_constraint(a, pltpu.HBM) for a in arrs + lands], after)
    return outs[0], outs[1], list(outs[2:2 + n]), list(outs[2 + n:2 + 2 * n]), outs[-1]


def _exchange_wait(handle, scatter, after, name):
    send_sems, recv_sems, srcs, lands, _ = handle
    n = len(srcs)

    def body(*refs):
        ins, lz = refs[:n], refs[n:2 * n]
        for cp in _peer_copies(ins, lz, scatter, refs[2 * n], refs[2 * n + 1], False):
            cp.wait_send()
        for cp in _peer_copies(ins, lz, scatter, refs[2 * n], refs[2 * n + 1], True):
            cp.wait_recv()

    outs = pl.pallas_call(
        body, name=name,
        out_shape=[pltpu.HBM(a.shape, a.dtype) for a in srcs + lands],
        in_specs=[_HBM] * (2 * n) + [_SEM, _SEM, pl.BlockSpec(memory_space=pl.ANY)],
        out_specs=[_HBM] * (2 * n),
        input_output_aliases={i: i for i in range(2 * n)},
        compiler_params=pltpu.CompilerParams(has_side_effects=_DATAFLOW),
    )(*srcs, *lands, send_sems, recv_sems, after)
    return list(outs[:n]), list(outs[n:])


def _fill_own(lands, srcs, scatter):
    me = 4 * lax.axis_index("x") + 2 * lax.axis_index("y") + lax.axis_index("c")
    own = [lax.dynamic_index_in_dim(s, me, 0, keepdims=False) if sc else s for s, sc in zip(srcs, scatter)]
    return [lax.dynamic_update_index_in_dim(land, o, me, 0) for land, o in zip(lands, own)]


def _slot_sum(parts, name, tr):
    _, R, C = parts.shape

    def body(p_ref, o_ref):
        acc = p_ref[0].astype(F32)
        for s in range(1, N_DEV):
            acc = acc + p_ref[s].astype(F32)
        o_ref[...] = acc

    return pl.pallas_call(
        body, name=name, grid=(R // tr,),
        in_specs=[pl.BlockSpec((N_DEV, tr, C), lambda i: (0, i, 0))],
        out_specs=pl.BlockSpec((tr, C), lambda i: (i, 0)),
        out_shape=jax.ShapeDtypeStruct((R, C), F32),
        compiler_params=_params(1),
    )(parts)


def _adamw(w, g, m, v, name):
    def body(w_ref, g_ref, m_ref, v_ref, d_ref, nm_ref, nv_ref):
        g = g_ref[...]
        nm = ADAM_B1 * m_ref[...] + (1.0 - ADAM_B1) * g
        nv = ADAM_B2 * v_ref[...] + (1.0 - ADAM_B2) * jnp.square(g)
        m_hat = nm / (1.0 - ADAM_B1 ** ADAM_STEP)
        v_hat = nv / (1.0 - ADAM_B2 ** ADAM_STEP)
        d_ref[...] = -ADAM_LR * (m_hat / (jnp.sqrt(v_hat) + ADAM_EPS) + ADAM_WD * w_ref[...])
        nm_ref[...] = nm
        nv_ref[...] = nv

    return pl.pallas_call(
        body, name=name, out_shape=[jax.ShapeDtypeStruct(w.shape, F32)] * 3,
        compiler_params=pltpu.CompilerParams(vmem_limit_bytes=VMEM_LIMIT),
    )(w, g, m, v)


def _rope_tables(T):
    half = HEAD_DIM // 2
    freqs = ROPE_THETA ** (-jnp.arange(half, dtype=F32) * (2.0 / HEAD_DIM))
    ang = jnp.arange(T).astype(F32)[:, None] * jnp.tile(freqs, LANES // half)[None, :]
    sign = jnp.tile(jnp.concatenate([-jnp.ones((half,), F32), jnp.ones((half,), F32)]), LANES // HEAD_DIM)
    return jnp.cos(ang), jnp.sin(ang) * sign[None, :]


def _block_diag(w_pool):
    wbd = jnp.zeros((POOL_W, POOL_W), F32)
    g = POOL_W // len(POOL_WINDOWS)
    for i in range(len(POOL_WINDOWS)):
        wbd = wbd.at[i * g:(i + 1) * g, i * g:(i + 1) * g].set(w_pool[i])
    return wbd


def _pack_small(g1, w_pool, pool_scale, g2, g3, g4, extra):
    pad = lambda a: jnp.pad(a.reshape(1, -1), ((0, 0), (0, D_MODEL - a.size)))
    rows = [g1.reshape(1, -1), g2.reshape(1, -1), g3.reshape(1, -1), g4.reshape(1, -1),
            w_pool.reshape(-1, D_MODEL), pad(pool_scale), pad(extra)]
    buf = jnp.concatenate(rows, axis=0)
    return jnp.pad(buf, ((0, SMALL_ROWS - buf.shape[0]), (0, 0)))


def _unpack_small(buf):
    n_pool = len(POOL_WINDOWS) * (POOL_W // len(POOL_WINDOWS)) ** 2 // D_MODEL
    g = POOL_W // len(POOL_WINDOWS)
    return (buf[0:1], buf[4:4 + n_pool].reshape(1, len(POOL_WINDOWS), g, g), buf[4 + n_pool:5 + n_pool, :POOL_W],
            buf[1:2], buf[2:3], buf[3:4], buf[5 + n_pool])


class _LocalStep:
    def __init__(self, x, tgt, g1, w_pool, pool_scale, g2, g3, g4):
        self.x, self.tgt, self.pool_scale = x, tgt, pool_scale
        self.g1, self.g2, self.g3, self.g4 = g1, g2, g3, g4
        self.cos, self.sin = _rope_tables(x.shape[0])
        self.wbd = _block_diag(w_pool).astype(BF16)

    def mixer_fwd(self, w_in_t, token):
        self.w_in_t = w_in_t
        self.h1, self.u, self.q, self.k, self.v = _proj_fwd(
            self.x, self.g1 + token[0, 0], w_in_t, self.cos, self.sin, 1024)
        self.pool = _pool_fwd(self.u, self.wbd, self.pool_scale, 1024)
        alone = [_attn_fwd(self.q, self.k, self.v, dil, []) for dil in DILATIONS[:-1]]
        self.attn, self.lse = _attn_fwd(self.q, self.k, self.v, DILATIONS[-1], alone)
        return self.attn

    def ffn_fwd_bwd(self, w_out, wg_t, wu_t, w_down):
        self.w_out = w_out
        self.cat, self.mix, self.x2, h2 = _mix_fwd(self.pool, self.attn, self.x, w_out, self.g2, self.g3, 1024)
        act_dgate, act_dup, act = _ffn_up(h2, wg_t, wu_t, 256)
        df, dy, self.dg4, self.loss = _ffn_down_loss(act, w_down, self.x2, self.g4, self.tgt, 512)
        self.dgate, dup, self.dx2, self.dmix, self.dg3, self.dg2 = _ffn_bwd(
            df, w_down, act_dgate, act_dup, wg_t, wu_t, self.x2, self.mix, dy, self.g3, self.g2, 256)
        return (_wgrad(self.dgate, h2, "wgrad_gate", D_FF // 2, 1024), _wgrad(dup, h2, "wgrad_up", D_FF // 2, 1024),
                _wgrad(act, df, "wgrad_down", D_FF // 2, 1024))

    def mixer_bwd(self, token):
        dmix = self.dmix
        dpool, dattn, dsum = _mix_bwd(dmix, self.w_out, self.attn, token, 1024)
        parts = [_attn_bwd(self.q, self.k, self.v, dattn, dsum, self.lse, dil) for dil in DILATIONS]
        du, dwbd, self.dscale = _pool_bwd(self.u, dpool, self.wbd, self.pool_scale, 1024)
        g = POOL_W // len(POOL_WINDOWS)
        self.dw_pool = jnp.stack([dwbd[i * g:(i + 1) * g, i * g:(i + 1) * g] for i in range(len(POOL_WINDOWS))])
        self.dproj = _dproj_combine(du, *[[p[j] for p in parts] for j in range(5)], self.cos, self.sin, 512)
        return _wgrad(self.dproj, self.h1, "wgrad_in", IN_W // 2, 1024), _wgrad(self.cat, dmix, "wgrad_out", D_MODEL, 1024)

    def input_bwd(self, token):
        grad_x, dg1 = _proj_bwd(self.dproj, self.w_in_t, self.x, self.dx2, self.g1 + token[0, 0], 1024)
        return self.loss, grad_x, (dg1, self.dw_pool, self.dscale, self.dg2, self.dg3, self.dg4)


def _local_step(x, tgt, g1, w_pool, pool_scale, g2, g3, g4, w_in_t, w_out, wg_t, wu_t, w_down):
    zero = jnp.zeros((8, LANES), F32)
    step = _LocalStep(x, tgt, g1, w_pool, pool_scale, g2, g3, g4)
    step.mixer_fwd(w_in_t, zero)
    dw_gate, dw_up, dw_down = step.ffn_fwd_bwd(w_out, wg_t, wu_t, w_down)
    dw_in, dw_out = step.mixer_bwd(zero)
    loss, grad_x, small = step.input_bwd(zero)
    return loss, grad_x, small, (dw_in, dw_out, dw_gate, dw_up, dw_down)


def kernel(x, ln_pre_mix, w_in, w_pool, pool_scale, w_out, ln_post_mix, ln_pre_ffn, w_gate, w_up, w_down, ln_post_ffn, loss_target, m_ln_pre_mix, m_w_in, m_w_pool, m_pool_scale, m_w_out, m_ln_post_mix, m_ln_pre_ffn, m_w_gate, m_w_up, m_w_down, m_ln_post_ffn, v_ln_pre_mix, v_w_in, v_w_pool, v_pool_scale, v_w_out, v_ln_post_mix, v_ln_pre_ffn, v_w_gate, v_w_up, v_w_down, v_ln_post_ffn):
    shards = [w_in[0].T.astype(BF16), w_out[0].astype(BF16), w_gate[0].T.astype(BF16),
              w_up[0].T.astype(BF16), w_down[0].astype(BF16)]
    flat = lambda a: a.reshape(-1, D_MODEL)
    blocks = lambda a: a.reshape(N_DEV, -1, D_MODEL)
    step = _LocalStep(x[0], loss_target[0], ln_pre_mix, w_pool[0], pool_scale, ln_post_mix, ln_pre_ffn, ln_post_ffn)

    w_in_t = flat(_gather_two_level(shards[0], "gather_w_in"))
    rest = _exchange_start(shards[1:], [False] * 4, w_in_t, "gather_rest_start")
    attn = step.mixer_fwd(w_in_t, rest[4])
    srcs, lands = _exchange_wait(rest, [False] * 4, attn, "gather_rest_wait")
    w_out_f, wg_t, wu_t, w_down_f = [flat(a) for a in _fill_own(lands, srcs, [False] * 4)]

    ffn = _exchange_start([blocks(a) for a in step.ffn_fwd_bwd(w_out_f, wg_t, wu_t, w_down_f)], [True] * 3,
                          step.dgate, "grads_ffn_start")
    mixer = _exchange_start([blocks(a) for a in step.mixer_bwd(ffn[4])], [True] * 2, step.dproj, "grads_mixer_start")
    loss, grad_x, small = step.input_bwd(mixer[4])
    got = []
    for handle, n_arr, nm in ((mixer, 2, "grads_mixer"), (ffn, 3, "grads_ffn")):
        srcs, lands = _exchange_wait(handle, [True] * n_arr, grad_x, nm + "_wait")
        got += _fill_own(lands, srcs, [True] * n_arr)
    sums = [_slot_sum(got[i], f"sum_grad_{i}", got[i].shape[1] // 2) for i in range(5)]

    small_buf = _pack_small(small[0], small[1], small[2], small[3], small[4], small[5], loss)
    small_sum = _slot_sum(_exchange([small_buf], [False], "gather_small")[0], "sum_small", SMALL_ROWS)

    g_in, g_out, g_gate, g_up, g_down = sums[0].T, sums[1], sums[2].T, sums[3].T, sums[4]
    upd = [_adamw(w[0], g, m[0], v[0], f"adamw_{nm}") for nm, w, g, m, v in (
        ("in", w_in, g_in, m_w_in, v_w_in), ("out", w_out, g_out, m_w_out, v_w_out),
        ("gate", w_gate, g_gate, m_w_gate, v_w_gate), ("up", w_up, g_up, m_w_up, v_w_up),
        ("down", w_down, g_down, m_w_down, v_w_down))]
    pack = lambda a, b, c, d, e, f: _pack_small(a, b[0], c, d, e, f, jnp.zeros((1,), F32))
    small_upd = _adamw(
        pack(ln_pre_mix, w_pool, pool_scale, ln_post_mix, ln_pre_ffn, ln_post_ffn), small_sum,
        pack(m_ln_pre_mix, m_w_pool, m_pool_scale, m_ln_post_mix, m_ln_pre_ffn, m_ln_post_ffn),
        pack(v_ln_pre_mix, v_w_pool, v_pool_scale, v_ln_post_mix, v_ln_pre_ffn, v_ln_post_ffn), "adamw_small")

    def tree(small6, big5):
        s1, spool, sscale, s2, s3, s4 = small6
        b_in, b_out, b_gate, b_up, b_down = [b[None] for b in big5]
        return [s1, b_in, spool, sscale, b_out, s2, s3, b_gate, b_up, b_down, s4]

    g_small = _unpack_small(small_sum)
    outs = [g_small[6][0], grad_x[None]]
    outs += tree(g_small[:6], [g_in, g_out, g_gate, g_up, g_down])
    for j in range(3):
        outs += tree(_unpack_small(small_upd[j])[:6], [u[j] for u in upd])
    return tuple(outs)
```

```python
import jax
import jax.numpy as jnp
from jax import lax
from jax.experimental import pallas as pl
from jax.experimental.pallas import tpu as pltpu

F32 = jnp.float32
BF16 = jnp.bfloat16

D_MODEL = 1024
POOL_W = 256
ATTN_W = 768
IN_W = 2560
D_FF = 2816
POOL_WINDOWS = (2, 4, 8, 16)
POOL_HALO = 16
DILATIONS = (1, 4, 16)
BLK = 128
LANES = 128
HEAD_DIM = 64
N_GROUPS = ATTN_W // LANES
ROPE_THETA = 10000.0
EPS = 1e-6
NEG = -1e30
N_DEV = 8
SMALL_ROWS = 24

ADAM_LR = 0.001
ADAM_B1 = 0.9
ADAM_B2 = 0.999
ADAM_EPS = 1e-08
ADAM_WD = 0.01
ADAM_STEP = 10

VMEM_LIMIT = 56 * 1024 * 1024


def _dot(a, b):
    return jnp.dot(a, b, preferred_element_type=F32)


def _dot_nt(a, b):
    return lax.dot_general(a, b, (((1,), (1,)), ((), ())), preferred_element_type=F32)


def _dot_tn(a, b):
    return lax.dot_general(a, b, (((0,), (0,)), ((), ())), preferred_element_type=F32)


def _params(n_grid):
    return pltpu.CompilerParams(dimension_semantics=("arbitrary",) * n_grid, vmem_limit_bytes=VMEM_LIMIT)


def _tok(tm, c):
    return pl.BlockSpec((tm, c), lambda i: (i, 0))


def _res(shape):
    return pl.BlockSpec(shape, lambda i: (0,) * len(shape), pipeline_mode=pl.Buffered(1))


def _acc(shape):
    return pl.BlockSpec(shape, lambda i: (0,) * len(shape))


def _rms_fwd(x, g):
    r = lax.rsqrt(jnp.mean(x * x, axis=-1, keepdims=True) + EPS)
    return x * r * g


def _rms_bwd(x, g, dy):
    r = lax.rsqrt(jnp.mean(x * x, axis=-1, keepdims=True) + EPS)
    xh = x * r
    gd = dy * g
    dx = r * (gd - xh * jnp.mean(gd * xh, axis=-1, keepdims=True))
    return dx, jnp.sum(dy * xh, axis=0, keepdims=True)


def _rope(x, c, s, sign):
    lane = lax.broadcasted_iota(jnp.int32, (x.shape[0], LANES), 1)
    first = (lane % HEAD_DIM) < (HEAD_DIM // 2)
    outs = []
    for g in range(x.shape[1] // LANES):
        xg = x[:, g * LANES:(g + 1) * LANES]
        rot = jnp.where(first, pltpu.roll(xg, LANES - HEAD_DIM // 2, 1), pltpu.roll(xg, HEAD_DIM // 2, 1))
        outs.append(xg * c + sign * (rot * s))
    return jnp.concatenate(outs, axis=1)


def _proj_fwd(x, g1, w_in_t, cos, sin, tm):
    T = x.shape[0]

    def body(x_ref, g_ref, w_ref, c_ref, s_ref, h_ref, u_ref, q_ref, k_ref, v_ref):
        h = _rms_fwd(x_ref[...], g_ref[...]).astype(BF16)
        h_ref[...] = h
        proj = _dot_nt(h, w_ref[...])
        c = c_ref[...]
        s = s_ref[...]
        u_ref[...] = proj[:, :POOL_W]
        _store_packed(q_ref, _rope(proj[:, POOL_W:POOL_W + ATTN_W], c, s, 1.0))
        _store_packed(k_ref, _rope(proj[:, POOL_W + ATTN_W:POOL_W + 2 * ATTN_W], c, s, 1.0))
        _store_packed(v_ref, proj[:, POOL_W + 2 * ATTN_W:])

    return pl.pallas_call(
        body, name="proj_fwd", grid=(T // tm,),
        in_specs=[_tok(tm, D_MODEL), _res((1, D_MODEL)), _res((IN_W, D_MODEL)), _tok(tm, LANES), _tok(tm, LANES)],
        out_specs=[_tok(tm, D_MODEL), _tok(tm, POOL_W)] + [_tok_packed(tm, ATTN_W)] * 3,
        out_shape=[jax.ShapeDtypeStruct((T, D_MODEL), BF16), jax.ShapeDtypeStruct((T, POOL_W), F32)]
        + [_packed(T, ATTN_W)] * 3,
        compiler_params=_params(1),
    )(x, g1, w_in_t, cos, sin)


def _pool_window(lane):
    return jnp.where(lane < 64, 2, jnp.where(lane < 128, 4, jnp.where(lane < 192, 8, 16)))


def _pool_select(lane, a2, a4, a8, a16):
    return jnp.where(lane < 64, a2, jnp.where(lane < 128, a4, jnp.where(lane < 192, a8, a16)))


def _pool_delta(cur, prev, i, tm):
    prev = jnp.where(i > 0, prev, 0.0)
    ext = jnp.concatenate([prev, cur], axis=0)
    s2 = ext + pltpu.roll(ext, 1, 0)
    s4 = s2 + pltpu.roll(s2, 2, 0)
    s8 = s4 + pltpu.roll(s4, 4, 0)
    s16 = s8 + pltpu.roll(s8, 8, 0)
    lane = lax.broadcasted_iota(jnp.int32, (tm, POOL_W), 1)
    row = lax.broadcasted_iota(jnp.int32, (tm, POOL_W), 0) + i * tm
    ws = _pool_select(lane, s2[POOL_HALO:], s4[POOL_HALO:], s8[POOL_HALO:], s16[POOL_HALO:])
    cnt = jnp.minimum(row + 1, _pool_window(lane)).astype(F32)
    return ws / cnt - cur


def _pool_fwd(u, wbd, scale, tm):
    T = u.shape[0]
    hb = tm // POOL_HALO

    def body(u_ref, prev_ref, w_ref, sc_ref, o_ref):
        d = _pool_delta(u_ref[...], prev_ref[...], pl.program_id(0), tm)
        o_ref[...] = (_dot(d.astype(BF16), w_ref[...]) * sc_ref[...]).astype(BF16)

    return pl.pallas_call(
        body, name="pool_fwd", grid=(T // tm,),
        in_specs=[_tok(tm, POOL_W), pl.BlockSpec((POOL_HALO, POOL_W), lambda i: (jnp.maximum(i * hb - 1, 0), 0)),
                  _res((POOL_W, POOL_W)), _res((1, POOL_W))],
        out_specs=_tok(tm, POOL_W),
        out_shape=jax.ShapeDtypeStruct((T, POOL_W), BF16),
        compiler_params=_params(1),
    )(u, u, wbd, scale)


def _attn_mask(has_prev):
    qi = lax.broadcasted_iota(jnp.int32, (BLK, 2 * BLK), 0)
    kj = lax.broadcasted_iota(jnp.int32, (BLK, 2 * BLK), 1)
    dist = qi + BLK - kj
    return (dist >= 0) & (dist <= BLK) & ((kj >= BLK) | has_prev)


def _stack_heads(x, lo):
    zero = jnp.zeros_like(x)
    return jnp.concatenate([jnp.where(lo, x, zero), jnp.where(lo, zero, x)], axis=0)


def _head_col(tile, lane, h):
    return jnp.sum(jnp.where(lane == h, tile, 0.0), axis=1, keepdims=True)


def _attn_cols(dil):
    return ATTN_W // 2 if dil >= 16 else ATTN_W


def _units_per_step(dil, backward):
    return {1: 4, 4: 2, 16: 8}[dil]


def _chunk_tokens(dil, units):
    return BLK * (units if dil == 1 else dil)


def _unit_steps(dil, units):
    return 1 if dil == 1 else dil // 2 // units


def _attn_specs(dil, nb, units):
    cw = _attn_cols(dil)
    ch = _chunk_tokens(dil, units)
    wide = lambda f: pl.BlockSpec((cw // LANES, ch // 2, LANES), f)
    cur = lambda n, j, r: (j, n, 0)
    prv = lambda n, j, r: (j, jnp.maximum(n - 1, 0), 0)
    prv_out = lambda n, j, r: (j, (n + nb - 1) % nb, 0)
    heads = pl.BlockSpec((ch, LANES), lambda n, j, r: (n, 0))
    return cw, wide(cur), wide(prv), wide(prv_out), heads


HIGH_HALF = 0xFFFF0000


def _pack(x):
    return pltpu.bitcast(x.astype(BF16), F32)


def _unpack(words):
    return pltpu.bitcast(words, BF16)


def _packed(rows, cols):
    return jax.ShapeDtypeStruct((cols // LANES, rows // 2, LANES), F32)


def _tok_packed(tm, cols):
    return pl.BlockSpec((cols // LANES, tm // 2, LANES), lambda i: (0, i, 0))


def _store_packed(ref, x):
    for g in range(x.shape[1] // LANES):
        ref[g] = _pack(x[:, g * LANES:(g + 1) * LANES])


def _load_packed(ref):
    return jnp.concatenate([_unpack(ref[g]) for g in range(ref.shape[0])], axis=1)


def _load_streams(ref, dil, r2, sl):
    if dil == 1:
        return [_unpack(ref.at[sl][pl.ds(r2 * (BLK // 2), BLK // 2), :])]
    words = lax.bitcast_convert_type(ref.at[sl][pl.ds(r2, BLK, stride=dil // 2), :], jnp.uint32)
    even = lax.bitcast_convert_type(words << 16, F32).astype(BF16)
    odd = lax.bitcast_convert_type(words & jnp.uint32(HIGH_HALF), F32).astype(BF16)
    return [even, odd]


def _load_prev_streams(prev_ref, cur_ref, dil, units, r2, sl):
    if dil > 1:
        return _load_streams(prev_ref, dil, r2, sl)
    return _load_streams(cur_ref, 1, r2 - 1, sl) if r2 > 0 else _load_streams(prev_ref, 1, units - 1, sl)


def _load_streams_f32(ref, dil, r2, sl):
    ref = ref if sl is None else ref.at[sl]
    if dil == 1:
        return [ref[pl.ds(r2 * BLK, BLK), :]]
    return [ref[pl.ds(2 * r2 + e, BLK, stride=dil), :] for e in range(2)]


def _store_streams_f32(ref, dil, r2, sl, tiles):
    ref = ref if sl is None else ref.at[sl]
    if dil == 1:
        ref[pl.ds(r2 * BLK, BLK), :] = tiles[0]
    else:
        for e, t in enumerate(tiles):
            ref[pl.ds(2 * r2 + e, BLK, stride=dil), :] = t


def _store_streams(ref, dil, r2, sl, tiles):
    if dil == 1:
        ref.at[sl][pl.ds(r2 * (BLK // 2), BLK // 2), :] = _pack(tiles[0])
    else:
        even, odd = [lax.bitcast_convert_type(t.astype(BF16).astype(F32), jnp.uint32) for t in tiles]
        words = (odd & jnp.uint32(HIGH_HALF)) | (even >> 16)
        ref.at[sl][pl.ds(r2, BLK, stride=dil // 2), :] = lax.bitcast_convert_type(words, F32)


def _attn_fwd(q, k, v, dil, others):
    T = 2 * q.shape[1]
    reps = _units_per_step(dil, False)
    nb = T // _chunk_tokens(dil, reps)
    first = not others
    cw, cur, prv, _, heads = _attn_specs(dil, nb, reps)
    ncb = ATTN_W // cw
    heads_per_step = cw // HEAD_DIM
    n_str = min(dil, 2)
    everything = None

    def body(*refs):
        q_ref, kc_ref, kp_ref, vc_ref, vp_ref = refs[:5]
        acc_ins, lse_ins = refs[5:5 + 2 * len(others):2], refs[6:6 + 2 * len(others):2]
        acc_ref, lse_ref = refs[-2:]
        j = pl.program_id(1)
        lane = lax.broadcasted_iota(jnp.int32, (BLK, LANES), 1)
        lo = lane < HEAD_DIM

        def stream_pair(r2):
            valid = _attn_mask(True if dil == 1 and r2 > 0 else pl.program_id(0) > 0)
            lse_tiles = [jnp.zeros((BLK, LANES), F32) for _ in range(n_str)]
            own = []
            for g in range(cw // LANES):
                qs, kcs, vcs = [_load_streams(r, dil, r2, g) for r in (q_ref, kc_ref, vc_ref)]
                kps, vps = [_load_prev_streams(p, c, dil, reps, r2, g) for p, c in ((kp_ref, kc_ref), (vp_ref, vc_ref))]
                pairs = []
                for e in range(n_str):
                    qg = qs[e] * 0.125
                    kcat = jnp.concatenate([kps[e], kcs[e]], axis=0)
                    vcat = jnp.concatenate([vps[e], vcs[e]], axis=0)
                    pair = None
                    for hh in range(2):
                        h = j * heads_per_step + 2 * g + hh
                        hm = lo if hh == 0 else jnp.logical_not(lo)
                        s = _dot_nt(jnp.where(hm, qg, jnp.zeros_like(qg)), kcat)
                        s = jnp.where(valid, s, NEG)
                        m = jnp.max(s, axis=1, keepdims=True)
                        p = jnp.exp(s - m)
                        den = jnp.sum(p, axis=1, keepdims=True)
                        o = _dot(p.astype(BF16), vcat) / den
                        pair = o if hh == 0 else jnp.where(lo, pair, o)
                        lse_tiles[e] = jnp.where(lane == h, m + jnp.log(den), lse_tiles[e])
                    pairs.append(pair)
                if first:
                    _store_streams(acc_ref, dil, r2, g, pairs)
                else:
                    own.append(pairs)
            if not first:
                mine = (lane >= j * heads_per_step) & (lane < (j + 1) * heads_per_step)
                theirs = [_load_streams_f32(ref, dil, r2, everything) for ref in lse_ins]
                w_theirs, w_own = [[] for _ in others], []
                for e in range(n_str):
                    parts = [t[e] for t in theirs] + [lse_tiles[e]]
                    mx = parts[0]
                    for part in parts[1:]:
                        mx = jnp.maximum(mx, part)
                    total = mx + jnp.log(sum(jnp.exp(part - mx) for part in parts))
                    for i, t in enumerate(theirs):
                        w_theirs[i].append(jnp.exp(t[e] - total))
                    w_own.append(jnp.exp(lse_tiles[e] - total))
                    lse_tiles[e] = jnp.where(mine, total, 0.0)
                for g in range(cw // LANES):
                    h0 = j * heads_per_step + 2 * g
                    spread = lambda w: jnp.where(lo, _head_col(w, lane, h0), _head_col(w, lane, h0 + 1))
                    olds = [_load_streams(ref, dil, r2, g) for ref in acc_ins]
                    _store_streams(acc_ref, dil, r2, g, [
                        sum(olds[i][e].astype(F32) * spread(w_theirs[i][e]) for i in range(len(others)))
                        + own[g][e] * spread(w_own[e]) for e in range(n_str)])
            if ncb == 1:
                _store_streams_f32(lse_ref, dil, r2, everything, lse_tiles)
            else:
                @pl.when(j == 0)
                def _():
                    _store_streams_f32(lse_ref, dil, r2, everything, lse_tiles)

                @pl.when(j > 0)
                def _():
                    before = _load_streams_f32(lse_ref, dil, r2, everything)
                    _store_streams_f32(lse_ref, dil, r2, everything, [a + b for a, b in zip(before, lse_tiles)])

        for rep in range(reps):
            stream_pair(rep if dil == 1 else pl.program_id(2) * reps + rep)

    ins = [q, k, k, v, v]
    in_specs = [cur, cur, prv, cur, prv]
    for acc, lse in others:
        ins += [acc, lse]
        in_specs += [cur, heads]
    return pl.pallas_call(
        body, name=f"attn_fwd_d{dil}", grid=(nb, ncb, _unit_steps(dil, reps)),
        in_specs=in_specs, out_specs=[cur, heads],
        out_shape=[_packed(T, ATTN_W), jax.ShapeDtypeStruct((T, LANES), F32)],
        compiler_params=_params(3),
    )(*ins)


def _mix_fwd(pool, attn, x, w_out, g2, g3, tm):
    T = x.shape[0]

    def body(p_ref, a_ref, x_ref, w_ref, g2_ref, g3_ref, cat_ref, mix_ref, x2_ref, h2_ref):
        p = p_ref[...]
        a = _load_packed(a_ref)
        cat_ref[...] = jnp.concatenate([p, a], axis=1)
        mix = _dot(p, w_ref[:POOL_W, :]) + _dot(a, w_ref[POOL_W:, :])
        mix_ref[...] = mix
        x2 = x_ref[...] + _rms_fwd(mix, g2_ref[...])
        x2_ref[...] = x2
        h2_ref[...] = _rms_fwd(x2, g3_ref[...]).astype(BF16)

    return pl.pallas_call(
        body, name="mix_fwd", grid=(T // tm,),
        in_specs=[_tok(tm, POOL_W), _tok_packed(tm, ATTN_W), _tok(tm, D_MODEL), _res((D_MODEL, D_MODEL)),
                  _res((1, D_MODEL)), _res((1, D_MODEL))],
        out_specs=[_tok(tm, D_MODEL)] * 4,
        out_shape=[jax.ShapeDtypeStruct((T, D_MODEL), BF16), jax.ShapeDtypeStruct((T, D_MODEL), F32),
                   jax.ShapeDtypeStruct((T, D_MODEL), F32), jax.ShapeDtypeStruct((T, D_MODEL), BF16)],
        compiler_params=_params(1),
    )(pool, attn, x, w_out, g2, g3)


def _ffn_fwd_loss(h2, wg_t, wu_t, w_down, x2, g4, tgt, tm):
    T = h2.shape[0]

    def body(h_ref, wg_ref, wu_ref, wd_ref, x2_ref, g_ref, t_ref,
             ag_ref, au_ref, a_ref, df_ref, dy_ref, dg_ref, loss_ref):
        @pl.when(pl.program_id(0) == 0)
        def _():
            dg_ref[...] = jnp.zeros_like(dg_ref)
            loss_ref[...] = jnp.zeros_like(loss_ref)

        h = h_ref[...]
        gate = _dot_nt(h, wg_ref[...])
        up = _dot_nt(h, wu_ref[...])
        sg = 1.0 / (1.0 + jnp.exp(-gate))
        silu = gate * sg
        act = (silu * up).astype(BF16)
        a_ref[...] = act
        ag_ref[...] = (up * (sg * (1.0 + gate * (1.0 - sg)))).astype(BF16)
        au_ref[...] = silu.astype(BF16)

        f = _dot(act, wd_ref[...])
        g = g_ref[...]
        err = x2_ref[...] + _rms_fwd(f, g) - t_ref[...]
        loss_ref[...] += 0.5 * jnp.sum(jnp.mean(err * err, axis=-1, keepdims=True), axis=0, keepdims=True)
        dy = err * (1.0 / D_MODEL)
        dy_ref[...] = dy
        df, dg = _rms_bwd(f, g, dy)
        dg_ref[...] += dg
        df_ref[...] = df.astype(BF16)

    wide, narrow, weight = _tok(tm, D_FF), _tok(tm, D_MODEL), _res((D_FF, D_MODEL))
    return pl.pallas_call(
        body, name="ffn_fwd_loss", grid=(T // tm,),
        in_specs=[narrow, weight, weight, weight, narrow, _res((1, D_MODEL)), narrow],
        out_specs=[wide, wide, wide, narrow, narrow, _acc((1, D_MODEL)), _acc((1, 1))],
        out_shape=[jax.ShapeDtypeStruct((T, D_FF), BF16)] * 3
        + [jax.ShapeDtypeStruct((T, D_MODEL), BF16), jax.ShapeDtypeStruct((T, D_MODEL), F32),
           jax.ShapeDtypeStruct((1, D_MODEL), F32), jax.ShapeDtypeStruct((1, 1), F32)],
        compiler_params=_params(1),
    )(h2, wg_t, wu_t, w_down, x2, g4, tgt)


def _ffn_bwd(df, w_down, act_dgate, act_dup, wg_t, wu_t, x2, mix, dy, g3, g2, tm):
    T = x2.shape[0]

    def body(df_ref, wd_ref, ag_ref, au_ref, wg_ref, wu_ref, x2_ref, mix_ref, dy_ref, g3_ref, g2_ref,
             dgate_ref, dup_ref, dx2_ref, dmix_ref, dg3_ref, dg2_ref):
        @pl.when(pl.program_id(0) == 0)
        def _():
            dg3_ref[...] = jnp.zeros_like(dg3_ref)
            dg2_ref[...] = jnp.zeros_like(dg2_ref)

        dact = _dot_nt(df_ref[...], wd_ref[...])
        dgate = (dact * ag_ref[...].astype(F32)).astype(BF16)
        dup = (dact * au_ref[...].astype(F32)).astype(BF16)
        dgate_ref[...] = dgate
        dup_ref[...] = dup
        dh2 = _dot(dgate, wg_ref[...]) + _dot(dup, wu_ref[...])
        dn, dg3 = _rms_bwd(x2_ref[...], g3_ref[...], dh2)
        dx2 = dy_ref[...] + dn
        dx2_ref[...] = dx2
        dg3_ref[...] += dg3
        dmix, dg2 = _rms_bwd(mix_ref[...], g2_ref[...], dx2)
        dg2_ref[...] += dg2
        dmix_ref[...] = dmix.astype(BF16)

    wide, narrow, weight, gain = _tok(tm, D_FF), _tok(tm, D_MODEL), _res((D_FF, D_MODEL)), _res((1, D_MODEL))
    return pl.pallas_call(
        body, name="ffn_bwd", grid=(T // tm,),
        in_specs=[narrow, weight, wide, wide, weight, weight, narrow, narrow, narrow, gain, gain],
        out_specs=[wide, wide, narrow, narrow, _acc((1, D_MODEL)), _acc((1, D_MODEL))],
        out_shape=[jax.ShapeDtypeStruct((T, D_FF), BF16), jax.ShapeDtypeStruct((T, D_FF), BF16),
                   jax.ShapeDtypeStruct((T, D_MODEL), F32), jax.ShapeDtypeStruct((T, D_MODEL), BF16),
                   jax.ShapeDtypeStruct((1, D_MODEL), F32), jax.ShapeDtypeStruct((1, D_MODEL), F32)],
        compiler_params=_params(1),
    )(df, w_down, act_dgate, act_dup, wg_t, wu_t, x2, mix, dy, g3, g2)


def _mix_bwd(dmix, w_out, attn, after, tm):
    T = dmix.shape[0]

    def body(d_ref, w_ref, a_ref, after_ref, dp_ref, da_ref, ds_ref):
        dcat = _dot_nt(d_ref[...], w_ref[...])
        dp_ref[...] = dcat[:, :POOL_W].astype(BF16)
        dattn = dcat[:, POOL_W:].astype(BF16)
        _store_packed(da_ref, dattn)
        prod = dattn.astype(F32) * _load_packed(a_ref).astype(F32)
        lane = lax.broadcasted_iota(jnp.int32, (tm, LANES), 1)
        lo = lane < HEAD_DIM
        dsum = jnp.zeros((tm, LANES), F32)
        for g in range(N_GROUPS):
            pg = prod[:, g * LANES:(g + 1) * LANES]
            dsum = jnp.where(lane == 2 * g, jnp.sum(jnp.where(lo, pg, 0.0), axis=1, keepdims=True), dsum)
            dsum = jnp.where(lane == 2 * g + 1, jnp.sum(jnp.where(lo, 0.0, pg), axis=1, keepdims=True), dsum)
        ds_ref[...] = dsum

    return pl.pallas_call(
        body, name="mix_bwd", grid=(T // tm,),
        in_specs=[_tok(tm, D_MODEL), _res((D_MODEL, D_MODEL)), _tok_packed(tm, ATTN_W), pl.BlockSpec(memory_space=pl.ANY)],
        out_specs=[_tok(tm, POOL_W), _tok_packed(tm, ATTN_W), _tok(tm, LANES)],
        out_shape=[jax.ShapeDtypeStruct((T, POOL_W), BF16), _packed(T, ATTN_W), jax.ShapeDtypeStruct((T, LANES), F32)],
        compiler_params=_params(1),
    )(dmix, w_out, attn, after)


def _attn_bwd(q, k, v, dout, dsum, lse, dil):
    T = 2 * q.shape[1]
    reps = _units_per_step(dil, True)
    nb = T // _chunk_tokens(dil, reps)
    cw, cur, prv, prv_out, heads = _attn_specs(dil, nb, reps)
    ncb = ATTN_W // cw
    heads_per_step = cw // HEAD_DIM
    n_str = min(dil, 2)

    def body(q_ref, kc_ref, kp_ref, vc_ref, vp_ref, do_ref, dsum_ref, lse_ref,
             dq_ref, dkc_ref, dkp_ref, dvc_ref, dvp_ref):
        j = pl.program_id(1)
        lane = lax.broadcasted_iota(jnp.int32, (BLK, LANES), 1)
        lo = lane < HEAD_DIM
        unit = lambda rep: rep if dil == 1 else pl.program_id(2) * reps + rep
        stats = [[_load_streams_f32(ref, dil, unit(rep), None) for ref in (lse_ref, dsum_ref)] for rep in range(reps)]

        def unit_grads(rep, g):
            r2 = unit(rep)
            valid = _attn_mask(True if dil == 1 and r2 > 0 else pl.program_id(0) > 0)
            valid2 = jnp.concatenate([valid, valid], axis=0)
            lse_tiles, dsum_tiles = stats[rep]
            qs, kcs, vcs, dos = [_load_streams(r, dil, r2, g) for r in (q_ref, kc_ref, vc_ref, do_ref)]
            kps, vps = [_load_prev_streams(p, c, dil, reps, r2, g) for p, c in ((kp_ref, kc_ref), (vp_ref, vc_ref))]
            dqs, dks, dvs = [], [], []
            for e in range(n_str):
                qg = qs[e] * 0.125
                dog = dos[e]
                kcat = jnp.concatenate([kps[e], kcs[e]], axis=0)
                vcat = jnp.concatenate([vps[e], vcs[e]], axis=0)
                h0 = j * heads_per_step + 2 * g
                q2 = _stack_heads(qg, lo)
                do2 = _stack_heads(dog, lo)
                both = lambda tile: jnp.concatenate([_head_col(tile, lane, h0), _head_col(tile, lane, h0 + 1)], axis=0)
                lse2, dsum2 = both(lse_tiles[e]), both(dsum_tiles[e])
                p = jnp.exp(jnp.where(valid2, _dot_nt(q2, kcat), NEG) - lse2)
                ds = (p * (_dot_nt(do2, vcat) - dsum2)).astype(BF16)
                dvs.append(_dot_tn(p.astype(BF16), do2))
                dks.append(_dot_tn(ds, q2))
                dq2 = _dot(ds, kcat) * 0.125
                dqs.append(jnp.where(lo, dq2[:BLK], dq2[BLK:]))
            return dqs, dks, dvs

        for g in range(cw // LANES):
            if dil > 1:
                for rep in range(reps):
                    r2 = unit(rep)
                    dqs, dks, dvs = unit_grads(rep, g)
                    _store_streams(dq_ref, dil, r2, g, dqs)
                    _store_streams(dkp_ref, dil, r2, g, [t[:BLK] for t in dks])
                    _store_streams(dkc_ref, dil, r2, g, [t[BLK:] for t in dks])
                    _store_streams(dvp_ref, dil, r2, g, [t[:BLK] for t in dvs])
                    _store_streams(dvc_ref, dil, r2, g, [t[BLK:] for t in dvs])
            else:
                blocks = [unit_grads(b, g) for b in range(reps)]
                for b, (dqs, dks, dvs) in enumerate(blocks):
                    _store_streams(dq_ref, 1, b, g, dqs)
                    for cur_ref, prev_ref, which in ((dkc_ref, dkp_ref, 1), (dvc_ref, dvp_ref, 2)):
                        own = blocks[b][which][0][BLK:]
                        if b + 1 < reps:
                            own = own + blocks[b + 1][which][0][:BLK]
                        _store_streams(cur_ref, 1, b, g, [own])
                        edge = blocks[0][which][0][:BLK] if b == reps - 1 else jnp.zeros((BLK, LANES), F32)
                        _store_streams(prev_ref, 1, b, g, [edge])

    return pl.pallas_call(
        body, name=f"attn_bwd_d{dil}", grid=(nb, ncb, _unit_steps(dil, reps)),
        in_specs=[cur, cur, prv, cur, prv, cur, heads, heads],
        out_specs=[cur, cur, prv_out, cur, prv_out],
        out_shape=[_packed(T, ATTN_W)] * 5,
        compiler_params=_params(3),
    )(q, k, k, v, v, dout, dsum, lse)


def _pool_bwd(u, dy, wbd, scale, tm):
    T = u.shape[0]
    nt = T // tm
    hb = tm // POOL_HALO

    def body(u_ref, prev_ref, dy_ref, next_ref, w_ref, sc_ref, du_ref, dw_ref, dsc_ref):
        i = pl.program_id(0)

        @pl.when(i == 0)
        def _():
            dw_ref[...] = jnp.zeros_like(dw_ref)
            dsc_ref[...] = jnp.zeros_like(dsc_ref)

        w = w_ref[...]
        sc = sc_ref[...]
        d = _pool_delta(u_ref[...], prev_ref[...], i, tm).astype(BF16)
        dyc = dy_ref[...].astype(F32)
        dsc_ref[...] += jnp.sum(dyc * _dot(d, w), axis=0, keepdims=True)
        nxt = jnp.where(i < nt - 1, next_ref[...].astype(F32), 0.0)
        dypre = (jnp.concatenate([dyc, nxt], axis=0) * sc).astype(BF16)
        dw_ref[...] += _dot_tn(d, dypre[:tm])
        dd = _dot_nt(dypre, w)
        n = tm + POOL_HALO
        lane = lax.broadcasted_iota(jnp.int32, (n, POOL_W), 1)
        row = lax.broadcasted_iota(jnp.int32, (n, POOL_W), 0) + i * tm
        gx = dd / jnp.minimum(row + 1, _pool_window(lane)).astype(F32)
        a2 = gx + pltpu.roll(gx, n - 1, 0)
        a4 = a2 + pltpu.roll(a2, n - 2, 0)
        a8 = a4 + pltpu.roll(a4, n - 4, 0)
        a16 = a8 + pltpu.roll(a8, n - 8, 0)
        fs = _pool_select(lane[:tm], a2[:tm], a4[:tm], a8[:tm], a16[:tm])
        du_ref[...] = (fs - dd[:tm]).astype(BF16)

    return pl.pallas_call(
        body, name="pool_bwd", grid=(nt,),
        in_specs=[_tok(tm, POOL_W), pl.BlockSpec((POOL_HALO, POOL_W), lambda i: (jnp.maximum(i * hb - 1, 0), 0)),
                  _tok(tm, POOL_W), pl.BlockSpec((POOL_HALO, POOL_W), lambda i: (jnp.minimum((i + 1) * hb, nt * hb - 1), 0)),
                  _res((POOL_W, POOL_W)), _res((1, POOL_W))],
        out_specs=[_tok(tm, POOL_W), _acc((POOL_W, POOL_W)), _acc((1, POOL_W))],
        out_shape=[jax.ShapeDtypeStruct((T, POOL_W), BF16), jax.ShapeDtypeStruct((POOL_W, POOL_W), F32),
                   jax.ShapeDtypeStruct((1, POOL_W), F32)],
        compiler_params=_params(1),
    )(u, u, dy, dy, wbd, scale)


def _dproj_combine(du, dqs, dkcs, dkps, dvcs, dvps, cos, sin, tm):
    T = du.shape[0]
    n_cfg = len(dqs)

    def body(*refs):
        du_ref = refs[0]
        groups = [refs[1 + j * n_cfg:1 + (j + 1) * n_cfg] for j in range(5)]
        c_ref, s_ref, out_ref = refs[1 + 5 * n_cfg:]
        tot = lambda rs: sum(_load_packed(r).astype(F32) for r in rs)
        c = c_ref[...]
        s = s_ref[...]
        dq = _rope(tot(groups[0]), c, s, -1.0)
        dk = _rope(tot(groups[1]) + tot(groups[2]), c, s, -1.0)
        dv = tot(groups[3]) + tot(groups[4])
        out_ref[...] = jnp.concatenate([du_ref[...], dq.astype(BF16), dk.astype(BF16), dv.astype(BF16)], axis=1)

    return pl.pallas_call(
        body, name="dproj_combine", grid=(T // tm,),
        in_specs=[_tok(tm, POOL_W)] + [_tok_packed(tm, ATTN_W)] * (5 * n_cfg) + [_tok(tm, LANES)] * 2,
        out_specs=_tok(tm, IN_W),
        out_shape=jax.ShapeDtypeStruct((T, IN_W), BF16),
        compiler_params=_params(1),
    )(du, *dqs, *dkcs, *dkps, *dvcs, *dvps, cos, sin)


def _proj_bwd(dproj, w_in_t, x, dx2, g1, tm):
    T = x.shape[0]

    def body(d_ref, w_ref, x_ref, r_ref, g_ref, dx_ref, dg_ref):
        @pl.when(pl.program_id(0) == 0)
        def _():
            dg_ref[...] = jnp.zeros_like(dg_ref)

        dn, dg = _rms_bwd(x_ref[...], g_ref[...], _dot(d_ref[...], w_ref[...]))
        dg_ref[...] += dg
        dx_ref[...] = r_ref[...] + dn

    return pl.pallas_call(
        body, name="proj_bwd", grid=(T // tm,),
        in_specs=[_tok(tm, IN_W), _res((IN_W, D_MODEL)), _tok(tm, D_MODEL), _tok(tm, D_MODEL), _res((1, D_MODEL))],
        out_specs=[_tok(tm, D_MODEL), _acc((1, D_MODEL))],
        out_shape=[jax.ShapeDtypeStruct((T, D_MODEL), F32), jax.ShapeDtypeStruct((1, D_MODEL), F32)],
        compiler_params=_params(1),
    )(dproj, w_in_t, x, dx2, g1)


def _wgrad(a, b, name, tile_m, tk):
    T, M = a.shape
    N = b.shape[1]
    nk = T // tk

    def body(a_ref, b_ref, o_ref, acc_ref):
        kk = pl.program_id(1)

        @pl.when(kk == 0)
        def _():
            acc_ref[...] = jnp.zeros_like(acc_ref)

        acc_ref[...] += _dot_tn(a_ref[...], b_ref[...])

        @pl.when(kk == nk - 1)
        def _():
            o_ref[...] = acc_ref[...].astype(BF16)

    return pl.pallas_call(
        body, name=name, grid=(M // tile_m, nk),
        in_specs=[pl.BlockSpec((tk, tile_m), lambda j, kk: (kk, j)), pl.BlockSpec((tk, N), lambda j, kk: (kk, 0))],
        out_specs=pl.BlockSpec((tile_m, N), lambda j, kk: (j, 0)),
        out_shape=jax.ShapeDtypeStruct((M, N), BF16),
        scratch_shapes=[pltpu.VMEM((tile_m, N), F32)],
        compiler_params=_params(2),
    )(a, b)


def _exchange(arrs, scatter, name):
    n = len(arrs)
    out_shapes = [jax.ShapeDtypeStruct((N_DEV,) + (a.shape[1:] if sc else a.shape), a.dtype)
                  for a, sc in zip(arrs, scatter)]

    def body(*refs):
        ins, outs = refs[:n], refs[n:2 * n]
        send_sems, recv_sems, loc_sems = refs[2 * n:]
        x, y, c = lax.axis_index("x"), lax.axis_index("y"), lax.axis_index("c")
        me = 4 * x + 2 * y + c
        local, sends, recvs = [], [], []
        for i in range(n):
            own = ins[i].at[me] if scatter[i] else ins[i]
            loc = pltpu.make_async_copy(own, outs[i].at[me], loc_sems.at[i])
            loc.start()
            local.append(loc)
            for kbits in range(1, N_DEV):
                px = 1 - x if kbits & 4 else x
                py = 1 - y if kbits & 2 else y
                pc = 1 - c if kbits & 1 else c
                pid = 4 * px + 2 * py + pc
                src = ins[i].at[pid] if scatter[i] else ins[i]
                cp = pltpu.make_async_remote_copy(
                    src_ref=src, dst_ref=outs[i].at[me],
                    send_sem=send_sems.at[i, kbits - 1], recv_sem=recv_sems.at[i, kbits - 1],
                    device_id=(px, py, pc), device_id_type=pl.DeviceIdType.MESH)
                cp.start()
                sends.append(cp)
                recvs.append(pltpu.make_async_remote_copy(
                    src_ref=src, dst_ref=outs[i].at[pid],
                    send_sem=send_sems.at[i, kbits - 1], recv_sem=recv_sems.at[i, kbits - 1],
                    device_id=(px, py, pc), device_id_type=pl.DeviceIdType.MESH))
        for cp in recvs:
            cp.wait_recv()
        for cp in sends:
            cp.wait_send()
        for cp in local:
            cp.wait()

    hbm = pl.BlockSpec(memory_space=pl.ANY)
    return pl.pallas_call(
        body, name=name, in_specs=[hbm] * n, out_specs=[hbm] * n, out_shape=out_shapes,
        scratch_shapes=[pltpu.SemaphoreType.DMA((n, N_DEV - 1)), pltpu.SemaphoreType.DMA((n, N_DEV - 1)),
                        pltpu.SemaphoreType.DMA((n,))],
    )(*arrs)


def _gather_two_level(arr, name):
    def body(x_ref, out_ref, send_sems, recv_sems, local_sem):
        x, y, c = lax.axis_index("x"), lax.axis_index("y"), lax.axis_index("c")
        me, sibling = (x, y, c), (x, y, 1 - c)
        chips = [(1 - x, y), (x, 1 - y), (1 - x, 1 - y)]
        slot = lambda px, py, pc: out_ref.at[4 * px + 2 * py + pc]

        def copy(k, block, to, src=None):
            return pltpu.make_async_remote_copy(
                src_ref=slot(*block) if src is None else src, dst_ref=slot(*block),
                send_sem=send_sems.at[k], recv_sem=recv_sems.at[k],
                device_id=to, device_id_type=pl.DeviceIdType.MESH)

        mine = pltpu.make_async_copy(x_ref, slot(*me), local_sem)
        mine.start()
        first = [copy(0, me, sibling, src=x_ref)]
        first += [copy(1 + i, me, (*chip, c), src=x_ref) for i, chip in enumerate(chips)]
        for cp in first:
            cp.start()
        passed = [copy(4 + i, (*chip, c), sibling) for i, chip in enumerate(chips)]
        for i, chip in enumerate(chips):
            copy(1 + i, (*chip, c), me).wait_recv()
            passed[i].start()
        copy(0, sibling, me).wait_recv()
        for i, chip in enumerate(chips):
            copy(4 + i, (*chip, 1 - c), me).wait_recv()
        for cp in first + passed:
            cp.wait_send()
        mine.wait()

    hbm = pl.BlockSpec(memory_space=pl.ANY)
    return pl.pallas_call(
        body, name=name, in_specs=[hbm], out_specs=hbm,
        out_shape=jax.ShapeDtypeStruct((N_DEV,) + arr.shape, arr.dtype),
        scratch_shapes=[pltpu.SemaphoreType.DMA((N_DEV - 1,)), pltpu.SemaphoreType.DMA((N_DEV - 1,)),
                        pltpu.SemaphoreType.DMA],
    )(arr)


def _peers(x, y, c):
    for kbits in range(1, N_DEV):
        px = 1 - x if kbits & 4 else x
        py = 1 - y if kbits & 2 else y
        pc = 1 - c if kbits & 1 else c
        yield kbits - 1, (px, py, pc), 4 * px + 2 * py + pc


def _peer_copies(ins, lands, scatter, send_sems, recv_sems, incoming):
    x, y, c = lax.axis_index("x"), lax.axis_index("y"), lax.axis_index("c")
    me = 4 * x + 2 * y + c
    copies = []
    for i in range(len(ins)):
        for k, peer, pid in _peers(x, y, c):
            slot = i * (N_DEV - 1) + k
            copies.append(pltpu.make_async_remote_copy(
                src_ref=ins[i].at[pid] if scatter[i] else ins[i], dst_ref=lands[i].at[pid if incoming else me],
                send_sem=send_sems.at[slot], recv_sem=recv_sems.at[slot],
                device_id=peer, device_id_type=pl.DeviceIdType.MESH))
    return copies


_HBM = pl.BlockSpec(memory_space=pltpu.HBM)
_SEM = pl.BlockSpec(memory_space=pltpu.SEMAPHORE)
_DATAFLOW = pltpu.SideEffectType.DATAFLOW_SIDE_EFFECTING


def _exchange_start(arrs, scatter, after, name):
    n = len(arrs)
    lands = [lax.empty((N_DEV,) + (a.shape[1:] if sc else a.shape), a.dtype) for a, sc in zip(arrs, scatter)]

    def body(*refs):
        ins, lz = refs[:n], refs[n:2 * n]
        send_sems, recv_sems = refs[2 * n + 1:2 * n + 3]
        token = refs[-1]
        for cp in _peer_copies(ins, lz, scatter, send_sems, recv_sems, False):
            cp.start()
        token[...] = jnp.zeros_like(token)

    sem_shape = pltpu.SemaphoreType.DMA((n * (N_DEV - 1),))
    outs = pl.pallas_call(
        body, name=name,
        out_shape=(sem_shape, sem_shape, *[pltpu.HBM(a.shape, a.dtype) for a in arrs + lands],
                   jax.ShapeDtypeStruct((8, LANES), F32)),
        in_specs=[_HBM] * (2 * n) + [pl.BlockSpec(memory_space=pl.ANY)],
        out_specs=(_SEM, _SEM, *[_HBM] * (2 * n), pl.BlockSpec(memory_space=pltpu.VMEM)),
        input_output_aliases={i: 2 + i for i in range(2 * n)},
        compiler_params=pltpu.CompilerParams(has_side_effects=_DATAFLOW),
    )(*[pltpu.with_memory_space_constraint(a, pltpu.HBM) for a in arrs + lands], after)
    return outs[0], outs[1], list(outs[2:2 + n]), list(outs[2 + n:2 + 2 * n]), outs[-1]


def _exchange_wait(handle, scatter, after, name):
    send_sems, recv_sems, srcs, lands, _ = handle
    n = len(srcs)

    def body(*refs):
        ins, lz = refs[:n], refs[n:2 * n]
        for cp in _peer_copies(ins, lz, scatter, refs[2 * n], refs[2 * n + 1], False):
            cp.wait_send()
        for cp in _peer_copies(ins, lz, scatter, refs[2 * n], refs[2 * n + 1], True):
            cp.wait_recv()

    outs = pl.pallas_call(
        body, name=name,
        out_shape=[pltpu.HBM(a.shape, a.dtype) for a in srcs + lands],
        in_specs=[_HBM] * (2 * n) + [_SEM, _SEM, pl.BlockSpec(memory_space=pl.ANY)],
        out_specs=[_HBM] * (2 * n),
        input_output_aliases={i: i for i in range(2 * n)},
        compiler_params=pltpu.CompilerParams(has_side_effects=_DATAFLOW),
    )(*srcs, *lands, send_sems, recv_sems, after)
    return list(outs[:n]), list(outs[n:])


def _fill_own(lands, srcs, scatter):
    me = 4 * lax.axis_index("x") + 2 * lax.axis_index("y") + lax.axis_index("c")
    own = [lax.dynamic_index_in_dim(s, me, 0, keepdims=False) if sc else s for s, sc in zip(srcs, scatter)]
    return [lax.dynamic_update_index_in_dim(land, o, me, 0) for land, o in zip(lands, own)]


def _slot_sum(parts, name, tr):
    _, R, C = parts.shape

    def body(p_ref, o_ref):
        acc = p_ref[0].astype(F32)
        for s in range(1, N_DEV):
            acc = acc + p_ref[s].astype(F32)
        o_ref[...] = acc

    return pl.pallas_call(
        body, name=name, grid=(R // tr,),
        in_specs=[pl.BlockSpec((N_DEV, tr, C), lambda i: (0, i, 0))],
        out_specs=pl.BlockSpec((tr, C), lambda i: (i, 0)),
        out_shape=jax.ShapeDtypeStruct((R, C), F32),
        compiler_params=_params(1),
    )(parts)


def _adamw(w, g, m, v, name):
    def body(w_ref, g_ref, m_ref, v_ref, d_ref, nm_ref, nv_ref):
        g = g_ref[...]
        nm = ADAM_B1 * m_ref[...] + (1.0 - ADAM_B1) * g
        nv = ADAM_B2 * v_ref[...] + (1.0 - ADAM_B2) * jnp.square(g)
        m_hat = nm / (1.0 - ADAM_B1 ** ADAM_STEP)
        v_hat = nv / (1.0 - ADAM_B2 ** ADAM_STEP)
        d_ref[...] = -ADAM_LR * (m_hat / (jnp.sqrt(v_hat) + ADAM_EPS) + ADAM_WD * w_ref[...])
        nm_ref[...] = nm
        nv_ref[...] = nv

    return pl.pallas_call(
        body, name=name, out_shape=[jax.ShapeDtypeStruct(w.shape, F32)] * 3,
        compiler_params=pltpu.CompilerParams(vmem_limit_bytes=VMEM_LIMIT),
    )(w, g, m, v)


def _rope_tables(T):
    half = HEAD_DIM // 2
    freqs = ROPE_THETA ** (-jnp.arange(half, dtype=F32) * (2.0 / HEAD_DIM))
    ang = jnp.arange(T).astype(F32)[:, None] * jnp.tile(freqs, LANES // half)[None, :]
    sign = jnp.tile(jnp.concatenate([-jnp.ones((half,), F32), jnp.ones((half,), F32)]), LANES // HEAD_DIM)
    return jnp.cos(ang), jnp.sin(ang) * sign[None, :]


def _block_diag(w_pool):
    wbd = jnp.zeros((POOL_W, POOL_W), F32)
    g = POOL_W // len(POOL_WINDOWS)
    for i in range(len(POOL_WINDOWS)):
        wbd = wbd.at[i * g:(i + 1) * g, i * g:(i + 1) * g].set(w_pool[i])
    return wbd


def _pack_small(g1, w_pool, pool_scale, g2, g3, g4, extra):
    pad = lambda a: jnp.pad(a.reshape(1, -1), ((0, 0), (0, D_MODEL - a.size)))
    rows = [g1.reshape(1, -1), g2.reshape(1, -1), g3.reshape(1, -1), g4.reshape(1, -1),
            w_pool.reshape(-1, D_MODEL), pad(pool_scale), pad(extra)]
    buf = jnp.concatenate(rows, axis=0)
    return jnp.pad(buf, ((0, SMALL_ROWS - buf.shape[0]), (0, 0)))


def _unpack_small(buf):
    n_pool = len(POOL_WINDOWS) * (POOL_W // len(POOL_WINDOWS)) ** 2 // D_MODEL
    g = POOL_W // len(POOL_WINDOWS)
    return (buf[0:1], buf[4:4 + n_pool].reshape(1, len(POOL_WINDOWS), g, g), buf[4 + n_pool:5 + n_pool, :POOL_W],
            buf[1:2], buf[2:3], buf[3:4], buf[5 + n_pool])


class _LocalStep:
    def __init__(self, x, tgt, g1, w_pool, pool_scale, g2, g3, g4):
        self.x, self.tgt, self.pool_scale = x, tgt, pool_scale
        self.g1, self.g2, self.g3, self.g4 = g1, g2, g3, g4
        self.cos, self.sin = _rope_tables(x.shape[0])
        self.wbd = _block_diag(w_pool).astype(BF16)

    def mixer_fwd(self, w_in_t, token):
        self.w_in_t = w_in_t
        self.h1, self.u, self.q, self.k, self.v = _proj_fwd(
            self.x, self.g1 + token[0, 0], w_in_t, self.cos, self.sin, 1024)
        self.pool = _pool_fwd(self.u, self.wbd, self.pool_scale, 1024)
        alone = [_attn_fwd(self.q, self.k, self.v, dil, []) for dil in DILATIONS[:-1]]
        self.attn, self.lse = _attn_fwd(self.q, self.k, self.v, DILATIONS[-1], alone)
        return self.attn

    def ffn_fwd_bwd(self, w_out, wg_t, wu_t, w_down):
        self.w_out = w_out
        self.cat, self.mix, self.x2, h2 = _mix_fwd(self.pool, self.attn, self.x, w_out, self.g2, self.g3, 1024)
        act_dgate, act_dup, act, df, dy, self.dg4, self.loss = _ffn_fwd_loss(
            h2, wg_t, wu_t, w_down, self.x2, self.g4, self.tgt, 256)
        self.dgate, dup, self.dx2, self.dmix, self.dg3, self.dg2 = _ffn_bwd(
            df, w_down, act_dgate, act_dup, wg_t, wu_t, self.x2, self.mix, dy, self.g3, self.g2, 256)
        return (_wgrad(self.dgate, h2, "wgrad_gate", D_FF // 2, 1024), _wgrad(dup, h2, "wgrad_up", D_FF // 2, 1024),
                _wgrad(act, df, "wgrad_down", D_FF // 2, 1024))

    def mixer_bwd(self, token):
        dmix = self.dmix
        dpool, dattn, dsum = _mix_bwd(dmix, self.w_out, self.attn, token, 1024)
        parts = [_attn_bwd(self.q, self.k, self.v, dattn, dsum, self.lse, dil) for dil in DILATIONS]
        du, dwbd, self.dscale = _pool_bwd(self.u, dpool, self.wbd, self.pool_scale, 1024)
        g = POOL_W // len(POOL_WINDOWS)
        self.dw_pool = jnp.stack([dwbd[i * g:(i + 1) * g, i * g:(i + 1) * g] for i in range(len(POOL_WINDOWS))])
        self.dproj = _dproj_combine(du, *[[p[j] for p in parts] for j in range(5)], self.cos, self.sin, 512)
        return _wgrad(self.dproj, self.h1, "wgrad_in", IN_W // 2, 1024), _wgrad(self.cat, dmix, "wgrad_out", D_MODEL, 1024)

    def input_bwd(self, token):
        grad_x, dg1 = _proj_bwd(self.dproj, self.w_in_t, self.x, self.dx2, self.g1 + token[0, 0], 1024)
        return self.loss, grad_x, (dg1, self.dw_pool, self.dscale, self.dg2, self.dg3, self.dg4)


def _local_step(x, tgt, g1, w_pool, pool_scale, g2, g3, g4, w_in_t, w_out, wg_t, wu_t, w_down):
    zero = jnp.zeros((8, LANES), F32)
    step = _LocalStep(x, tgt, g1, w_pool, pool_scale, g2, g3, g4)
    step.mixer_fwd(w_in_t, zero)
    dw_gate, dw_up, dw_down = step.ffn_fwd_bwd(w_out, wg_t, wu_t, w_down)
    dw_in, dw_out = step.mixer_bwd(zero)
    loss, grad_x, small = step.input_bwd(zero)
    return loss, grad_x, small, (dw_in, dw_out, dw_gate, dw_up, dw_down)


def kernel(x, ln_pre_mix, w_in, w_pool, pool_scale, w_out, ln_post_mix, ln_pre_ffn, w_gate, w_up, w_down, ln_post_ffn, loss_target, m_ln_pre_mix, m_w_in, m_w_pool, m_pool_scale, m_w_out, m_ln_post_mix, m_ln_pre_ffn, m_w_gate, m_w_up, m_w_down, m_ln_post_ffn, v_ln_pre_mix, v_w_in, v_w_pool, v_pool_scale, v_w_out, v_ln_post_mix, v_ln_pre_ffn, v_w_gate, v_w_up, v_w_down, v_ln_post_ffn):
    shards = [w_in[0].T.astype(BF16), w_out[0].astype(BF16), w_gate[0].T.astype(BF16),
              w_up[0].T.astype(BF16), w_down[0].astype(BF16)]
    flat = lambda a: a.reshape(-1, D_MODEL)
    blocks = lambda a: a.reshape(N_DEV, -1, D_MODEL)
    step = _LocalStep(x[0], loss_target[0], ln_pre_mix, w_pool[0], pool_scale, ln_post_mix, ln_pre_ffn, ln_post_ffn)

    w_in_t = flat(_gather_two_level(shards[0], "gather_w_in"))
    rest = _exchange_start(shards[1:], [False] * 4, w_in_t, "gather_rest_start")
    attn = step.mixer_fwd(w_in_t, rest[4])
    srcs, lands = _exchange_wait(rest, [False] * 4, attn, "gather_rest_wait")
    w_out_f, wg_t, wu_t, w_down_f = [flat(a) for a in _fill_own(lands, srcs, [False] * 4)]

    ffn = _exchange_start([blocks(a) for a in step.ffn_fwd_bwd(w_out_f, wg_t, wu_t, w_down_f)], [True] * 3,
                          step.dgate, "grads_ffn_start")
    mixer = _exchange_start([blocks(a) for a in step.mixer_bwd(ffn[4])], [True] * 2, step.dproj, "grads_mixer_start")
    loss, grad_x, small = step.input_bwd(mixer[4])
    got = []
    for handle, n_arr, nm in ((mixer, 2, "grads_mixer"), (ffn, 3, "grads_ffn")):
        srcs, lands = _exchange_wait(handle, [True] * n_arr, grad_x, nm + "_wait")
        got += _fill_own(lands, srcs, [True] * n_arr)
    sums = [_slot_sum(got[i], f"sum_grad_{i}", got[i].shape[1] // 2) for i in range(5)]

    small_buf = _pack_small(small[0], small[1], small[2], small[3], small[4], small[5], loss)
    small_sum = _slot_sum(_exchange([small_buf], [False], "gather_small")[0], "sum_small", SMALL_ROWS)

    g_in, g_out, g_gate, g_up, g_down = sums[0].T, sums[1], sums[2].T, sums[3].T, sums[4]
    upd = [_adamw(w[0], g, m[0], v[0], f"adamw_{nm}") for nm, w, g, m, v in (
        ("in", w_in, g_in, m_w_in, v_w_in), ("out", w_out, g_out, m_w_out, v_w_out),
        ("gate", w_gate, g_gate, m_w_gate, v_w_gate), ("up", w_up, g_up, m_w_up, v_w_up),
        ("down", w_down, g_down, m_w_down, v_w_down))]
    pack = lambda a, b, c, d, e, f: _pack_small(a, b[0], c, d, e, f, jnp.zeros((1,), F32))
    small_upd = _adamw(
        pack(ln_pre_mix, w_pool, pool_scale, ln_post_mix, ln_pre_ffn, ln_post_ffn), small_sum,
        pack(m_ln_pre_mix, m_w_pool, m_pool_scale, m_ln_post_mix, m_ln_pre_ffn, m_ln_post_ffn),
        pack(v_ln_pre_mix, v_w_pool, v_pool_scale, v_ln_post_mix, v_ln_pre_ffn, v_ln_post_ffn), "adamw_small")

    def tree(small6, big5):
        s1, spool, sscale, s2, s3, s4 = small6
        b_in, b_out, b_gate, b_up, b_down = [b[None] for b in big5]
        return [s1, b_in, spool, sscale, b_out, s2, s3, b_gate, b_up, b_down, s4]

    g_small = _unpack_small(small_sum)
    outs = [g_small[6][0], grad_x[None]]
    outs += tree(g_small[:6], [g_in, g_out, g_gate, g_up, g_down])
    for j in range(3):
        outs += tree(_unpack_small(small_upd[j])[:6], [u[j] for u in upd])
    return tuple(outs)
```

```python
import jax
import jax.numpy as jnp
from jax import lax
from jax.experimental import pallas as pl
from jax.experimental.pallas import tpu as pltpu

F32 = jnp.float32
BF16 = jnp.bfloat16

D_MODEL = 1024
POOL_W = 256
ATTN_W = 768
IN_W = 2560
D_FF = 2816
POOL_WINDOWS = (2, 4, 8, 16)
POOL_HALO = 16
DILATIONS = (1, 4, 16)
BLK = 128
LANES = 128
HEAD_DIM = 64
N_GROUPS = ATTN_W // LANES
ROPE_THETA = 10000.0
EPS = 1e-6
NEG = -1e30
N_DEV = 8
SMALL_ROWS = 24

ADAM_LR = 0.001
ADAM_B1 = 0.9
ADAM_B2 = 0.999
ADAM_EPS = 1e-08
ADAM_WD = 0.01
ADAM_STEP = 10

VMEM_LIMIT = 56 * 1024 * 1024


def _dot(a, b):
    return jnp.dot(a, b, preferred_element_type=F32)


def _dot_nt(a, b):
    return lax.dot_general(a, b, (((1,), (1,)), ((), ())), preferred_element_type=F32)


def _dot_tn(a, b):
    return lax.dot_general(a, b, (((0,), (0,)), ((), ())), preferred_element_type=F32)


def _params(n_grid):
    return pltpu.CompilerParams(dimension_semantics=("arbitrary",) * n_grid, vmem_limit_bytes=VMEM_LIMIT)


def _tok(tm, c):
    return pl.BlockSpec((tm, c), lambda i: (i, 0))


def _res(shape):
    return pl.BlockSpec(shape, lambda i: (0,) * len(shape), pipeline_mode=pl.Buffered(1))


def _acc(shape):
    return pl.BlockSpec(shape, lambda i: (0,) * len(shape))


def _rms_fwd(x, g):
    r = lax.rsqrt(jnp.mean(x * x, axis=-1, keepdims=True) + EPS)
    return x * r * g


def _rms_bwd(x, g, dy):
    r = lax.rsqrt(jnp.mean(x * x, axis=-1, keepdims=True) + EPS)
    xh = x * r
    gd = dy * g
    dx = r * (gd - xh * jnp.mean(gd * xh, axis=-1, keepdims=True))
    return dx, jnp.sum(dy * xh, axis=0, keepdims=True)


def _rope(x, c, s, sign):
    lane = lax.broadcasted_iota(jnp.int32, (x.shape[0], LANES), 1)
    first = (lane % HEAD_DIM) < (HEAD_DIM // 2)
    outs = []
    for g in range(x.shape[1] // LANES):
        xg = x[:, g * LANES:(g + 1) * LANES]
        rot = jnp.where(first, pltpu.roll(xg, LANES - HEAD_DIM // 2, 1), pltpu.roll(xg, HEAD_DIM // 2, 1))
        outs.append(xg * c + sign * (rot * s))
    return jnp.concatenate(outs, axis=1)


def _proj_fwd(x, g1, w_in_t, cos, sin, tm):
    T = x.shape[0]

    def body(x_ref, g_ref, w_ref, c_ref, s_ref, h_ref, u_ref, q_ref, k_ref, v_ref):
        h = _rms_fwd(x_ref[...], g_ref[...]).astype(BF16)
        h_ref[...] = h
        proj = _dot_nt(h, w_ref[...])
        c = c_ref[...]
        s = s_ref[...]
        u_ref[...] = proj[:, :POOL_W]
        _store_packed(q_ref, _rope(proj[:, POOL_W:POOL_W + ATTN_W], c, s, 1.0))
        _store_packed(k_ref, _rope(proj[:, POOL_W + ATTN_W:POOL_W + 2 * ATTN_W], c, s, 1.0))
        _store_packed(v_ref, proj[:, POOL_W + 2 * ATTN_W:])

    return pl.pallas_call(
        body, name="proj_fwd", grid=(T // tm,),
        in_specs=[_tok(tm, D_MODEL), _res((1, D_MODEL)), _res((IN_W, D_MODEL)), _tok(tm, LANES), _tok(tm, LANES)],
        out_specs=[_tok(tm, D_MODEL), _tok(tm, POOL_W)] + [_tok_packed(tm, ATTN_W)] * 3,
        out_shape=[jax.ShapeDtypeStruct((T, D_MODEL), BF16), jax.ShapeDtypeStruct((T, POOL_W), F32)]
        + [_packed(T, ATTN_W)] * 3,
        compiler_params=_params(1),
    )(x, g1, w_in_t, cos, sin)


def _pool_window(lane):
    return jnp.where(lane < 64, 2, jnp.where(lane < 128, 4, jnp.where(lane < 192, 8, 16)))


def _pool_select(lane, a2, a4, a8, a16):
    return jnp.where(lane < 64, a2, jnp.where(lane < 128, a4, jnp.where(lane < 192, a8, a16)))


def _pool_delta(cur, prev, i, tm):
    prev = jnp.where(i > 0, prev, 0.0)
    ext = jnp.concatenate([prev, cur], axis=0)
    s2 = ext + pltpu.roll(ext, 1, 0)
    s4 = s2 + pltpu.roll(s2, 2, 0)
    s8 = s4 + pltpu.roll(s4, 4, 0)
    s16 = s8 + pltpu.roll(s8, 8, 0)
    lane = lax.broadcasted_iota(jnp.int32, (tm, POOL_W), 1)
    row = lax.broadcasted_iota(jnp.int32, (tm, POOL_W), 0) + i * tm
    ws = _pool_select(lane, s2[POOL_HALO:], s4[POOL_HALO:], s8[POOL_HALO:], s16[POOL_HALO:])
    cnt = jnp.minimum(row + 1, _pool_window(lane)).astype(F32)
    return ws / cnt - cur


def _pool_fwd(u, wbd, scale, tm):
    T = u.shape[0]
    hb = tm // POOL_HALO

    def body(u_ref, prev_ref, w_ref, sc_ref, o_ref):
        d = _pool_delta(u_ref[...], prev_ref[...], pl.program_id(0), tm)
        o_ref[...] = (_dot(d.astype(BF16), w_ref[...]) * sc_ref[...]).astype(BF16)

    return pl.pallas_call(
        body, name="pool_fwd", grid=(T // tm,),
        in_specs=[_tok(tm, POOL_W), pl.BlockSpec((POOL_HALO, POOL_W), lambda i: (jnp.maximum(i * hb - 1, 0), 0)),
                  _res((POOL_W, POOL_W)), _res((1, POOL_W))],
        out_specs=_tok(tm, POOL_W),
        out_shape=jax.ShapeDtypeStruct((T, POOL_W), BF16),
        compiler_params=_params(1),
    )(u, u, wbd, scale)


def _attn_mask(has_prev):
    qi = lax.broadcasted_iota(jnp.int32, (BLK, 2 * BLK), 0)
    kj = lax.broadcasted_iota(jnp.int32, (BLK, 2 * BLK), 1)
    dist = qi + BLK - kj
    return (dist >= 0) & (dist <= BLK) & ((kj >= BLK) | has_prev)


def _stack_heads(x, lo):
    zero = jnp.zeros_like(x)
    return jnp.concatenate([jnp.where(lo, x, zero), jnp.where(lo, zero, x)], axis=0)


def _head_col(tile, lane, h):
    return jnp.sum(jnp.where(lane == h, tile, 0.0), axis=1, keepdims=True)


def _attn_cols(dil):
    return ATTN_W // 2 if dil >= 16 else ATTN_W


def _units_per_step(dil, backward):
    return {1: 4, 4: 2, 16: 8}[dil]


def _chunk_tokens(dil, units):
    return BLK * (units if dil == 1 else dil)


def _unit_steps(dil, units):
    return 1 if dil == 1 else dil // 2 // units


def _attn_specs(dil, nb, units):
    cw = _attn_cols(dil)
    ch = _chunk_tokens(dil, units)
    wide = lambda f: pl.BlockSpec((cw // LANES, ch // 2, LANES), f)
    cur = lambda n, j, r: (j, n, 0)
    prv = lambda n, j, r: (j, jnp.maximum(n - 1, 0), 0)
    prv_out = lambda n, j, r: (j, (n + nb - 1) % nb, 0)
    heads = pl.BlockSpec((ch, LANES), lambda n, j, r: (n, 0))
    return cw, wide(cur), wide(prv), wide(prv_out), heads


HIGH_HALF = 0xFFFF0000


def _pack(x):
    return pltpu.bitcast(x.astype(BF16), F32)


def _unpack(words):
    return pltpu.bitcast(words, BF16)


def _packed(rows, cols):
    return jax.ShapeDtypeStruct((cols // LANES, rows // 2, LANES), F32)


def _tok_packed(tm, cols):
    return pl.BlockSpec((cols // LANES, tm // 2, LANES), lambda i: (0, i, 0))


def _store_packed(ref, x):
    for g in range(x.shape[1] // LANES):
        ref[g] = _pack(x[:, g * LANES:(g + 1) * LANES])


def _load_packed(ref):
    return jnp.concatenate([_unpack(ref[g]) for g in range(ref.shape[0])], axis=1)


def _load_streams(ref, dil, r2, sl):
    if dil == 1:
        return [_unpack(ref.at[sl][pl.ds(r2 * (BLK // 2), BLK // 2), :])]
    words = lax.bitcast_convert_type(ref.at[sl][pl.ds(r2, BLK, stride=dil // 2), :], jnp.uint32)
    even = lax.bitcast_convert_type(words << 16, F32).astype(BF16)
    odd = lax.bitcast_convert_type(words & jnp.uint32(HIGH_HALF), F32).astype(BF16)
    return [even, odd]


def _load_prev_streams(prev_ref, cur_ref, dil, units, r2, sl):
    if dil > 1:
        return _load_streams(prev_ref, dil, r2, sl)
    return _load_streams(cur_ref, 1, r2 - 1, sl) if r2 > 0 else _load_streams(prev_ref, 1, units - 1, sl)


def _load_streams_f32(ref, dil, r2, sl):
    ref = ref if sl is None else ref.at[sl]
    if dil == 1:
        return [ref[pl.ds(r2 * BLK, BLK), :]]
    return [ref[pl.ds(2 * r2 + e, BLK, stride=dil), :] for e in range(2)]


def _store_streams_f32(ref, dil, r2, sl, tiles):
    ref = ref if sl is None else ref.at[sl]
    if dil == 1:
        ref[pl.ds(r2 * BLK, BLK), :] = tiles[0]
    else:
        for e, t in enumerate(tiles):
            ref[pl.ds(2 * r2 + e, BLK, stride=dil), :] = t


def _store_streams(ref, dil, r2, sl, tiles):
    if dil == 1:
        ref.at[sl][pl.ds(r2 * (BLK // 2), BLK // 2), :] = _pack(tiles[0])
    else:
        even, odd = [lax.bitcast_convert_type(t.astype(BF16).astype(F32), jnp.uint32) for t in tiles]
        words = (odd & jnp.uint32(HIGH_HALF)) | (even >> 16)
        ref.at[sl][pl.ds(r2, BLK, stride=dil // 2), :] = lax.bitcast_convert_type(words, F32)


def _attn_fwd(q, k, v, dil, others):
    T = 2 * q.shape[1]
    reps = _units_per_step(dil, False)
    nb = T // _chunk_tokens(dil, reps)
    first = not others
    cw, cur, prv, _, heads = _attn_specs(dil, nb, reps)
    ncb = ATTN_W // cw
    heads_per_step = cw // HEAD_DIM
    n_str = min(dil, 2)
    everything = None

    def body(*refs):
        q_ref, kc_ref, kp_ref, vc_ref, vp_ref = refs[:5]
        acc_ins, lse_ins = refs[5:5 + 2 * len(others):2], refs[6:6 + 2 * len(others):2]
        acc_ref, lse_ref = refs[-2:]
        j = pl.program_id(1)
        lane = lax.broadcasted_iota(jnp.int32, (BLK, LANES), 1)
        lo = lane < HEAD_DIM

        def stream_pair(r2):
            valid = _attn_mask(True if dil == 1 and r2 > 0 else pl.program_id(0) > 0)
            lse_tiles = [jnp.zeros((BLK, LANES), F32) for _ in range(n_str)]
            own = []
            for g in range(cw // LANES):
                qs, kcs, vcs = [_load_streams(r, dil, r2, g) for r in (q_ref, kc_ref, vc_ref)]
                kps, vps = [_load_prev_streams(p, c, dil, reps, r2, g) for p, c in ((kp_ref, kc_ref), (vp_ref, vc_ref))]
                pairs = []
                for e in range(n_str):
                    qg = qs[e] * 0.125
                    kcat = jnp.concatenate([kps[e], kcs[e]], axis=0)
                    vcat = jnp.concatenate([vps[e], vcs[e]], axis=0)
                    pair = None
                    for hh in range(2):
                        h = j * heads_per_step + 2 * g + hh
                        hm = lo if hh == 0 else jnp.logical_not(lo)
                        s = _dot_nt(jnp.where(hm, qg, jnp.zeros_like(qg)), kcat)
                        s = jnp.where(valid, s, NEG)
                        m = jnp.max(s, axis=1, keepdims=True)
                        p = jnp.exp(s - m)
                        den = jnp.sum(p, axis=1, keepdims=True)
                        o = _dot(p.astype(BF16), vcat) / den
                        pair = o if hh == 0 else jnp.where(lo, pair, o)
                        lse_tiles[e] = jnp.where(lane == h, m + jnp.log(den), lse_tiles[e])
                    pairs.append(pair)
                if first:
                    _store_streams(acc_ref, dil, r2, g, pairs)
                else:
                    own.append(pairs)
            if not first:
                mine = (lane >= j * heads_per_step) & (lane < (j + 1) * heads_per_step)
                theirs = [_load_streams_f32(ref, dil, r2, everything) for ref in lse_ins]
                w_theirs, w_own = [[] for _ in others], []
                for e in range(n_str):
                    parts = [t[e] for t in theirs] + [lse_tiles[e]]
                    mx = parts[0]
                    for part in parts[1:]:
                        mx = jnp.maximum(mx, part)
                    total = mx + jnp.log(sum(jnp.exp(part - mx) for part in parts))
                    for i, t in enumerate(theirs):
                        w_theirs[i].append(jnp.exp(t[e] - total))
                    w_own.append(jnp.exp(lse_tiles[e] - total))
                    lse_tiles[e] = jnp.where(mine, total, 0.0)
                for g in range(cw // LANES):
                    h0 = j * heads_per_step + 2 * g
                    spread = lambda w: jnp.where(lo, _head_col(w, lane, h0), _head_col(w, lane, h0 + 1))
                    olds = [_load_streams(ref, dil, r2, g) for ref in acc_ins]
                    _store_streams(acc_ref, dil, r2, g, [
                        sum(olds[i][e].astype(F32) * spread(w_theirs[i][e]) for i in range(len(others)))
                        + own[g][e] * spread(w_own[e]) for e in range(n_str)])
            if ncb == 1:
                _store_streams_f32(lse_ref, dil, r2, everything, lse_tiles)
            else:
                @pl.when(j == 0)
                def _():
                    _store_streams_f32(lse_ref, dil, r2, everything, lse_tiles)

                @pl.when(j > 0)
                def _():
                    before = _load_streams_f32(lse_ref, dil, r2, everything)
                    _store_streams_f32(lse_ref, dil, r2, everything, [a + b for a, b in zip(before, lse_tiles)])

        for rep in range(reps):
            stream_pair(rep if dil == 1 else pl.program_id(2) * reps + rep)

    ins = [q, k, k, v, v]
    in_specs = [cur, cur, prv, cur, prv]
    for acc, lse in others:
        ins += [acc, lse]
        in_specs += [cur, heads]
    return pl.pallas_call(
        body, name=f"attn_fwd_d{dil}", grid=(nb, ncb, _unit_steps(dil, reps)),
        in_specs=in_specs, out_specs=[cur, heads],
        out_shape=[_packed(T, ATTN_W), jax.ShapeDtypeStruct((T, LANES), F32)],
        compiler_params=_params(3),
    )(*ins)


def _mix_fwd(pool, attn, x, w_out, g2, g3, tm):
    T = x.shape[0]

    def body(p_ref, a_ref, x_ref, w_ref, g2_ref, g3_ref, cat_ref, mix_ref, x2_ref, h2_ref):
        p = p_ref[...]
        a = _load_packed(a_ref)
        cat_ref[...] = jnp.concatenate([p, a], axis=1)
        mix = _dot(p, w_ref[:POOL_W, :]) + _dot(a, w_ref[POOL_W:, :])
        mix_ref[...] = mix
        x2 = x_ref[...] + _rms_fwd(mix, g2_ref[...])
        x2_ref[...] = x2
        h2_ref[...] = _rms_fwd(x2, g3_ref[...]).astype(BF16)

    return pl.pallas_call(
        body, name="mix_fwd", grid=(T // tm,),
        in_specs=[_tok(tm, POOL_W), _tok_packed(tm, ATTN_W), _tok(tm, D_MODEL), _res((D_MODEL, D_MODEL)),
                  _res((1, D_MODEL)), _res((1, D_MODEL))],
        out_specs=[_tok(tm, D_MODEL)] * 4,
        out_shape=[jax.ShapeDtypeStruct((T, D_MODEL), BF16), jax.ShapeDtypeStruct((T, D_MODEL), F32),
                   jax.ShapeDtypeStruct((T, D_MODEL), F32), jax.ShapeDtypeStruct((T, D_MODEL), BF16)],
        compiler_params=_params(1),
    )(pool, attn, x, w_out, g2, g3)


def _ffn_up(h2, wg_t, wu_t, tm):
    T = h2.shape[0]

    def body(h_ref, wg_ref, wu_ref, dg_ref, du_ref, a_ref):
        h = h_ref[...]
        gate = _dot_nt(h, wg_ref[...])
        up = _dot_nt(h, wu_ref[...])
        sg = 1.0 / (1.0 + jnp.exp(-gate))
        silu = gate * sg
        a_ref[...] = (silu * up).astype(BF16)
        dg_ref[...] = (up * (sg * (1.0 + gate * (1.0 - sg)))).astype(BF16)
        du_ref[...] = silu.astype(BF16)

    return pl.pallas_call(
        body, name="ffn_up", grid=(T // tm,),
        in_specs=[_tok(tm, D_MODEL), _res((D_FF, D_MODEL)), _res((D_FF, D_MODEL))],
        out_specs=[_tok(tm, D_FF)] * 3,
        out_shape=[jax.ShapeDtypeStruct((T, D_FF), BF16)] * 3,
        compiler_params=_params(1),
    )(h2, wg_t, wu_t)


def _ffn_down_loss(act, w_down, x2, g4, tgt, tm):
    T = act.shape[0]

    def body(a_ref, w_ref, x2_ref, g_ref, t_ref, df_ref, dy_ref, dg_ref, loss_ref):
        i = pl.program_id(0)

        @pl.when(i == 0)
        def _():
            dg_ref[...] = jnp.zeros_like(dg_ref)
            loss_ref[...] = jnp.zeros_like(loss_ref)

        f = _dot(a_ref[...], w_ref[...])
        g = g_ref[...]
        err = x2_ref[...] + _rms_fwd(f, g) - t_ref[...]
        loss_ref[...] += 0.5 * jnp.sum(jnp.mean(err * err, axis=-1, keepdims=True), axis=0, keepdims=True)
        dy = err * (1.0 / D_MODEL)
        dy_ref[...] = dy
        df, dg = _rms_bwd(f, g, dy)
        dg_ref[...] += dg
        df_ref[...] = df.astype(BF16)

    return pl.pallas_call(
        body, name="ffn_down_loss", grid=(T // tm,),
        in_specs=[_tok(tm, D_FF), _res((D_FF, D_MODEL)), _tok(tm, D_MODEL), _res((1, D_MODEL)), _tok(tm, D_MODEL)],
        out_specs=[_tok(tm, D_MODEL), _tok(tm, D_MODEL), _acc((1, D_MODEL)), _acc((1, 1))],
        out_shape=[jax.ShapeDtypeStruct((T, D_MODEL), BF16), jax.ShapeDtypeStruct((T, D_MODEL), F32),
                   jax.ShapeDtypeStruct((1, D_MODEL), F32), jax.ShapeDtypeStruct((1, 1), F32)],
        compiler_params=_params(1),
    )(act, w_down, x2, g4, tgt)


def _ffn_bwd(df, w_down, act_dgate, act_dup, wg_t, wu_t, x2, mix, dy, g3, g2, tm):
    T = x2.shape[0]

    def body(df_ref, wd_ref, ag_ref, au_ref, wg_ref, wu_ref, x2_ref, mix_ref, dy_ref, g3_ref, g2_ref,
             dgate_ref, dup_ref, dx2_ref, dmix_ref, dg3_ref, dg2_ref):
        @pl.when(pl.program_id(0) == 0)
        def _():
            dg3_ref[...] = jnp.zeros_like(dg3_ref)
            dg2_ref[...] = jnp.zeros_like(dg2_ref)

        dact = _dot_nt(df_ref[...], wd_ref[...])
        dgate = (dact * ag_ref[...].astype(F32)).astype(BF16)
        dup = (dact * au_ref[...].astype(F32)).astype(BF16)
        dgate_ref[...] = dgate
        dup_ref[...] = dup
        dh2 = _dot(dgate, wg_ref[...]) + _dot(dup, wu_ref[...])
        dn, dg3 = _rms_bwd(x2_ref[...], g3_ref[...], dh2)
        dx2 = dy_ref[...] + dn
        dx2_ref[...] = dx2
        dg3_ref[...] += dg3
        dmix, dg2 = _rms_bwd(mix_ref[...], g2_ref[...], dx2)
        dg2_ref[...] += dg2
        dmix_ref[...] = dmix.astype(BF16)

    wide, narrow, weight, gain = _tok(tm, D_FF), _tok(tm, D_MODEL), _res((D_FF, D_MODEL)), _res((1, D_MODEL))
    return pl.pallas_call(
        body, name="ffn_bwd", grid=(T // tm,),
        in_specs=[narrow, weight, wide, wide, weight, weight, narrow, narrow, narrow, gain, gain],
        out_specs=[wide, wide, narrow, narrow, _acc((1, D_MODEL)), _acc((1, D_MODEL))],
        out_shape=[jax.ShapeDtypeStruct((T, D_FF), BF16), jax.ShapeDtypeStruct((T, D_FF), BF16),
                   jax.ShapeDtypeStruct((T, D_MODEL), F32), jax.ShapeDtypeStruct((T, D_MODEL), BF16),
                   jax.ShapeDtypeStruct((1, D_MODEL), F32), jax.ShapeDtypeStruct((1, D_MODEL), F32)],
        compiler_params=_params(1),
    )(df, w_down, act_dgate, act_dup, wg_t, wu_t, x2, mix, dy, g3, g2)


def _mix_bwd(dmix, w_out, attn, after, tm):
    T = dmix.shape[0]

    def body(d_ref, w_ref, a_ref, after_ref, dp_ref, da_ref, ds_ref):
        dcat = _dot_nt(d_ref[...], w_ref[...])
        dp_ref[...] = dcat[:, :POOL_W].astype(BF16)
        dattn = dcat[:, POOL_W:].astype(BF16)
        _store_packed(da_ref, dattn)
        prod = dattn.astype(F32) * _load_packed(a_ref).astype(F32)
        lane = lax.broadcasted_iota(jnp.int32, (tm, LANES), 1)
        lo = lane < HEAD_DIM
        dsum = jnp.zeros((tm, LANES), F32)
        for g in range(N_GROUPS):
            pg = prod[:, g * LANES:(g + 1) * LANES]
            dsum = jnp.where(lane == 2 * g, jnp.sum(jnp.where(lo, pg, 0.0), axis=1, keepdims=True), dsum)
            dsum = jnp.where(lane == 2 * g + 1, jnp.sum(jnp.where(lo, 0.0, pg), axis=1, keepdims=True), dsum)
        ds_ref[...] = dsum

    return pl.pallas_call(
        body, name="mix_bwd", grid=(T // tm,),
        in_specs=[_tok(tm, D_MODEL), _res((D_MODEL, D_MODEL)), _tok_packed(tm, ATTN_W), pl.BlockSpec(memory_space=pl.ANY)],
        out_specs=[_tok(tm, POOL_W), _tok_packed(tm, ATTN_W), _tok(tm, LANES)],
        out_shape=[jax.ShapeDtypeStruct((T, POOL_W), BF16), _packed(T, ATTN_W), jax.ShapeDtypeStruct((T, LANES), F32)],
        compiler_params=_params(1),
    )(dmix, w_out, attn, after)


def _attn_bwd(q, k, v, dout, dsum, lse, dil):
    T = 2 * q.shape[1]
    reps = _units_per_step(dil, True)
    nb = T // _chunk_tokens(dil, reps)
    cw, cur, prv, prv_out, heads = _attn_specs(dil, nb, reps)
    ncb = ATTN_W // cw
    heads_per_step = cw // HEAD_DIM
    n_str = min(dil, 2)

    def body(q_ref, kc_ref, kp_ref, vc_ref, vp_ref, do_ref, dsum_ref, lse_ref,
             dq_ref, dkc_ref, dkp_ref, dvc_ref, dvp_ref):
        j = pl.program_id(1)
        lane = lax.broadcasted_iota(jnp.int32, (BLK, LANES), 1)
        lo = lane < HEAD_DIM
        unit = lambda rep: rep if dil == 1 else pl.program_id(2) * reps + rep
        stats = [[_load_streams_f32(ref, dil, unit(rep), None) for ref in (lse_ref, dsum_ref)] for rep in range(reps)]

        def unit_grads(rep, g):
            r2 = unit(rep)
            valid = _attn_mask(True if dil == 1 and r2 > 0 else pl.program_id(0) > 0)
            valid2 = jnp.concatenate([valid, valid], axis=0)
            lse_tiles, dsum_tiles = stats[rep]
            qs, kcs, vcs, dos = [_load_streams(r, dil, r2, g) for r in (q_ref, kc_ref, vc_ref, do_ref)]
            kps, vps = [_load_prev_streams(p, c, dil, reps, r2, g) for p, c in ((kp_ref, kc_ref), (vp_ref, vc_ref))]
            dqs, dks, dvs = [], [], []
            for e in range(n_str):
                qg = qs[e] * 0.125
                dog = dos[e]
                kcat = jnp.concatenate([kps[e], kcs[e]], axis=0)
                vcat = jnp.concatenate([vps[e], vcs[e]], axis=0)
                h0 = j * heads_per_step + 2 * g
                q2 = _stack_heads(qg, lo)
                do2 = _stack_heads(dog, lo)
                both = lambda tile: jnp.concatenate([_head_col(tile, lane, h0), _head_col(tile, lane, h0 + 1)], axis=0)
                lse2, dsum2 = both(lse_tiles[e]), both(dsum_tiles[e])
                p = jnp.exp(jnp.where(valid2, _dot_nt(q2, kcat), NEG) - lse2)
                ds = (p * (_dot_nt(do2, vcat) - dsum2)).astype(BF16)
                dvs.append(_dot_tn(p.astype(BF16), do2))
                dks.append(_dot_tn(ds, q2))
                dq2 = _dot(ds, kcat) * 0.125
                dqs.append(jnp.where(lo, dq2[:BLK], dq2[BLK:]))
            return dqs, dks, dvs

        for g in range(cw // LANES):
            if dil > 1:
                for rep in range(reps):
                    r2 = unit(rep)
                    dqs, dks, dvs = unit_grads(rep, g)
                    _store_streams(dq_ref, dil, r2, g, dqs)
                    _store_streams(dkp_ref, dil, r2, g, [t[:BLK] for t in dks])
                    _store_streams(dkc_ref, dil, r2, g, [t[BLK:] for t in dks])
                    _store_streams(dvp_ref, dil, r2, g, [t[:BLK] for t in dvs])
                    _store_streams(dvc_ref, dil, r2, g, [t[BLK:] for t in dvs])
            else:
                blocks = [unit_grads(b, g) for b in range(reps)]
                for b, (dqs, dks, dvs) in enumerate(blocks):
                    _store_streams(dq_ref, 1, b, g, dqs)
                    for cur_ref, prev_ref, which in ((dkc_ref, dkp_ref, 1), (dvc_ref, dvp_ref, 2)):
                        own = blocks[b][which][0][BLK:]
                        if b + 1 < reps:
                            own = own + blocks[b + 1][which][0][:BLK]
                        _store_streams(cur_ref, 1, b, g, [own])
                        edge = blocks[0][which][0][:BLK] if b == reps - 1 else jnp.zeros((BLK, LANES), F32)
                        _store_streams(prev_ref, 1, b, g, [edge])

    return pl.pallas_call(
        body, name=f"attn_bwd_d{dil}", grid=(nb, ncb, _unit_steps(dil, reps)),
        in_specs=[cur, cur, prv, cur, prv, cur, heads, heads],
        out_specs=[cur, cur, prv_out, cur, prv_out],
        out_shape=[_packed(T, ATTN_W)] * 5,
        compiler_params=_params(3),
    )(q, k, k, v, v, dout, dsum, lse)


def _pool_bwd(u, dy, wbd, scale, tm):
    T = u.shape[0]
    nt = T // tm
    hb = tm // POOL_HALO

    def body(u_ref, prev_ref, dy_ref, next_ref, w_ref, sc_ref, du_ref, dw_ref, dsc_ref):
        i = pl.program_id(0)

        @pl.when(i == 0)
        def _():
            dw_ref[...] = jnp.zeros_like(dw_ref)
            dsc_ref[...] = jnp.zeros_like(dsc_ref)

        w = w_ref[...]
        sc = sc_ref[...]
        d = _pool_delta(u_ref[...], prev_ref[...], i, tm).astype(BF16)
        dyc = dy_ref[...].astype(F32)
        dsc_ref[...] += jnp.sum(dyc * _dot(d, w), axis=0, keepdims=True)
        nxt = jnp.where(i < nt - 1, next_ref[...].astype(F32), 0.0)
        dypre = (jnp.concatenate([dyc, nxt], axis=0) * sc).astype(BF16)
        dw_ref[...] += _dot_tn(d, dypre[:tm])
        dd = _dot_nt(dypre, w)
        n = tm + POOL_HALO
        lane = lax.broadcasted_iota(jnp.int32, (n, POOL_W), 1)
        row = lax.broadcasted_iota(jnp.int32, (n, POOL_W), 0) + i * tm
        gx = dd / jnp.minimum(row + 1, _pool_window(lane)).astype(F32)
        a2 = gx + pltpu.roll(gx, n - 1, 0)
        a4 = a2 + pltpu.roll(a2, n - 2, 0)
        a8 = a4 + pltpu.roll(a4, n - 4, 0)
        a16 = a8 + pltpu.roll(a8, n - 8, 0)
        fs = _pool_select(lane[:tm], a2[:tm], a4[:tm], a8[:tm], a16[:tm])
        du_ref[...] = (fs - dd[:tm]).astype(BF16)

    return pl.pallas_call(
        body, name="pool_bwd", grid=(nt,),
        in_specs=[_tok(tm, POOL_W), pl.BlockSpec((POOL_HALO, POOL_W), lambda i: (jnp.maximum(i * hb - 1, 0), 0)),
                  _tok(tm, POOL_W), pl.BlockSpec((POOL_HALO, POOL_W), lambda i: (jnp.minimum((i + 1) * hb, nt * hb - 1), 0)),
                  _res((POOL_W, POOL_W)), _res((1, POOL_W))],
        out_specs=[_tok(tm, POOL_W), _acc((POOL_W, POOL_W)), _acc((1, POOL_W))],
        out_shape=[jax.ShapeDtypeStruct((T, POOL_W), BF16), jax.ShapeDtypeStruct((POOL_W, POOL_W), F32),
                   jax.ShapeDtypeStruct((1, POOL_W), F32)],
        compiler_params=_params(1),
    )(u, u, dy, dy, wbd, scale)


def _dproj_combine(du, dqs, dkcs, dkps, dvcs, dvps, cos, sin, tm):
    T = du.shape[0]
    n_cfg = len(dqs)

    def body(*refs):
        du_ref = refs[0]
        groups = [refs[1 + j * n_cfg:1 + (j + 1) * n_cfg] for j in range(5)]
        c_ref, s_ref, out_ref = refs[1 + 5 * n_cfg:]
        tot = lambda rs: sum(_load_packed(r).astype(F32) for r in rs)
        c = c_ref[...]
        s = s_ref[...]
        dq = _rope(tot(groups[0]), c, s, -1.0)
        dk = _rope(tot(groups[1]) + tot(groups[2]), c, s, -1.0)
        dv = tot(groups[3]) + tot(groups[4])
        out_ref[...] = jnp.concatenate([du_ref[...], dq.astype(BF16), dk.astype(BF16), dv.astype(BF16)], axis=1)

    return pl.pallas_call(
        body, name="dproj_combine", grid=(T // tm,),
        in_specs=[_tok(tm, POOL_W)] + [_tok_packed(tm, ATTN_W)] * (5 * n_cfg) + [_tok(tm, LANES)] * 2,
        out_specs=_tok(tm, IN_W),
        out_shape=jax.ShapeDtypeStruct((T, IN_W), BF16),
        compiler_params=_params(1),
    )(du, *dqs, *dkcs, *dkps, *dvcs, *dvps, cos, sin)


def _proj_bwd(dproj, w_in_t, x, dx2, g1, tm):
    T = x.shape[0]

    def body(d_ref, w_ref, x_ref, r_ref, g_ref, dx_ref, dg_ref):
        @pl.when(pl.program_id(0) == 0)
        def _():
            dg_ref[...] = jnp.zeros_like(dg_ref)

        dn, dg = _rms_bwd(x_ref[...], g_ref[...], _dot(d_ref[...], w_ref[...]))
        dg_ref[...] += dg
        dx_ref[...] = r_ref[...] + dn

    return pl.pallas_call(
        body, name="proj_bwd", grid=(T // tm,),
        in_specs=[_tok(tm, IN_W), _res((IN_W, D_MODEL)), _tok(tm, D_MODEL), _tok(tm, D_MODEL), _res((1, D_MODEL))],
        out_specs=[_tok(tm, D_MODEL), _acc((1, D_MODEL))],
        out_shape=[jax.ShapeDtypeStruct((T, D_MODEL), F32), jax.ShapeDtypeStruct((1, D_MODEL), F32)],
        compiler_params=_params(1),
    )(dproj, w_in_t, x, dx2, g1)


def _wgrad(a, b, name, tile_m, tk):
    T, M = a.shape
    N = b.shape[1]
    nk = T // tk

    def body(a_ref, b_ref, o_ref, acc_ref):
        kk = pl.program_id(1)

        @pl.when(kk == 0)
        def _():
            acc_ref[...] = jnp.zeros_like(acc_ref)

        acc_ref[...] += _dot_tn(a_ref[...], b_ref[...])

        @pl.when(kk == nk - 1)
        def _():
            o_ref[...] = acc_ref[...].astype(BF16)

    return pl.pallas_call(
        body, name=name, grid=(M // tile_m, nk),
        in_specs=[pl.BlockSpec((tk, tile_m), lambda j, kk: (kk, j)), pl.BlockSpec((tk, N), lambda j, kk: (kk, 0))],
        out_specs=pl.BlockSpec((tile_m, N), lambda j, kk: (j, 0)),
        out_shape=jax.ShapeDtypeStruct((M, N), BF16),
        scratch_shapes=[pltpu.VMEM((tile_m, N), F32)],
        compiler_params=_params(2),
    )(a, b)


def _exchange(arrs, scatter, name):
    n = len(arrs)
    out_shapes = [jax.ShapeDtypeStruct((N_DEV,) + (a.shape[1:] if sc else a.shape), a.dtype)
                  for a, sc in zip(arrs, scatter)]

    def body(*refs):
        ins, outs = refs[:n], refs[n:2 * n]
        send_sems, recv_sems, loc_sems = refs[2 * n:]
        x, y, c = lax.axis_index("x"), lax.axis_index("y"), lax.axis_index("c")
        me = 4 * x + 2 * y + c
        local, sends, recvs = [], [], []
        for i in range(n):
            own = ins[i].at[me] if scatter[i] else ins[i]
            loc = pltpu.make_async_copy(own, outs[i].at[me], loc_sems.at[i])
            loc.start()
            local.append(loc)
            for kbits in range(1, N_DEV):
                px = 1 - x if kbits & 4 else x
                py = 1 - y if kbits & 2 else y
                pc = 1 - c if kbits & 1 else c
                pid = 4 * px + 2 * py + pc
                src = ins[i].at[pid] if scatter[i] else ins[i]
                cp = pltpu.make_async_remote_copy(
                    src_ref=src, dst_ref=outs[i].at[me],
                    send_sem=send_sems.at[i, kbits - 1], recv_sem=recv_sems.at[i, kbits - 1],
                    device_id=(px, py, pc), device_id_type=pl.DeviceIdType.MESH)
                cp.start()
                sends.append(cp)
                recvs.append(pltpu.make_async_remote_copy(
                    src_ref=src, dst_ref=outs[i].at[pid],
                    send_sem=send_sems.at[i, kbits - 1], recv_sem=recv_sems.at[i, kbits - 1],
                    device_id=(px, py, pc), device_id_type=pl.DeviceIdType.MESH))
        for cp in recvs:
            cp.wait_recv()
        for cp in sends:
            cp.wait_send()
        for cp in local:
            cp.wait()

    hbm = pl.BlockSpec(memory_space=pl.ANY)
    return pl.pallas_call(
        body, name=name, in_specs=[hbm] * n, out_specs=[hbm] * n, out_shape=out_shapes,
        scratch_shapes=[pltpu.SemaphoreType.DMA((n, N_DEV - 1)), pltpu.SemaphoreType.DMA((n, N_DEV - 1)),
                        pltpu.SemaphoreType.DMA((n,))],
    )(*arrs)


def _gather_two_level(arr, name):
    def body(x_ref, out_ref, send_sems, recv_sems, local_sem):
        x, y, c = lax.axis_index("x"), lax.axis_index("y"), lax.axis_index("c")
        me, sibling = (x, y, c), (x, y, 1 - c)
        chips = [(1 - x, y), (x, 1 - y), (1 - x, 1 - y)]
        slot = lambda px, py, pc: out_ref.at[4 * px + 2 * py + pc]

        def copy(k, block, to, src=None):
            return pltpu.make_async_remote_copy(
                src_ref=slot(*block) if src is None else src, dst_ref=slot(*block),
                send_sem=send_sems.at[k], recv_sem=recv_sems.at[k],
                device_id=to, device_id_type=pl.DeviceIdType.MESH)

        mine = pltpu.make_async_copy(x_ref, slot(*me), local_sem)
        mine.start()
        first = [copy(0, me, sibling, src=x_ref)]
        first += [copy(1 + i, me, (*chip, c), src=x_ref) for i, chip in enumerate(chips)]
        for cp in first:
            cp.start()
        passed = [copy(4 + i, (*chip, c), sibling) for i, chip in enumerate(chips)]
        for i, chip in enumerate(chips):
            copy(1 + i, (*chip, c), me).wait_recv()
            passed[i].start()
        copy(0, sibling, me).wait_recv()
        for i, chip in enumerate(chips):
            copy(4 + i, (*chip, 1 - c), me).wait_recv()
        for cp in first + passed:
            cp.wait_send()
        mine.wait()

    hbm = pl.BlockSpec(memory_space=pl.ANY)
    return pl.pallas_call(
        body, name=name, in_specs=[hbm], out_specs=hbm,
        out_shape=jax.ShapeDtypeStruct((N_DEV,) + arr.shape, arr.dtype),
        scratch_shapes=[pltpu.SemaphoreType.DMA((N_DEV - 1,)), pltpu.SemaphoreType.DMA((N_DEV - 1,)),
                        pltpu.SemaphoreType.DMA],
    )(arr)


def _peers(x, y, c):
    for kbits in range(1, N_DEV):
        px = 1 - x if kbits & 4 else x
        py = 1 - y if kbits & 2 else y
        pc = 1 - c if kbits & 1 else c
        yield kbits - 1, (px, py, pc), 4 * px + 2 * py + pc


def _peer_copies(ins, lands, scatter, send_sems, recv_sems, incoming):
    x, y, c = lax.axis_index("x"), lax.axis_index("y"), lax.axis_index("c")
    me = 4 * x + 2 * y + c
    copies = []
    for i in range(len(ins)):
        for k, peer, pid in _peers(x, y, c):
            slot = i * (N_DEV - 1) + k
            copies.append(pltpu.make_async_remote_copy(
                src_ref=ins[i].at[pid] if scatter[i] else ins[i], dst_ref=lands[i].at[pid if incoming else me],
                send_sem=send_sems.at[slot], recv_sem=recv_sems.at[slot],
                device_id=peer, device_id_type=pl.DeviceIdType.MESH))
    return copies


_HBM = pl.BlockSpec(memory_space=pltpu.HBM)
_SEM = pl.BlockSpec(memory_space=pltpu.SEMAPHORE)
_DATAFLOW = pltpu.SideEffectType.DATAFLOW_SIDE_EFFECTING


def _exchange_start(arrs, scatter, after, name):
    n = len(arrs)
    lands = [lax.empty((N_DEV,) + (a.shape[1:] if sc else a.shape), a.dtype) for a, sc in zip(arrs, scatter)]

    def body(*refs):
        ins, lz = refs[:n], refs[n:2 * n]
        send_sems, recv_sems = refs[2 * n + 1:2 * n + 3]
        token = refs[-1]
        for cp in _peer_copies(ins, lz, scatter, send_sems, recv_sems, False):
            cp.start()
        token[...] = jnp.zeros_like(token)

    sem_shape = pltpu.SemaphoreType.DMA((n * (N_DEV - 1),))
    outs = pl.pallas_call(
        body, name=name,
        out_shape=(sem_shape, sem_shape, *[pltpu.HBM(a.shape, a.dtype) for a in arrs + lands],
                   jax.ShapeDtypeStruct((8, LANES), F32)),
        in_specs=[_HBM] * (2 * n) + [pl.BlockSpec(memory_space=pl.ANY)],
        out_specs=(_SEM, _SEM, *[_HBM] * (2 * n), pl.BlockSpec(memory_space=pltpu.VMEM)),
        input_output_aliases={i: 2 + i for i in range(2 * n)},
        compiler_params=pltpu.CompilerParams(has_side_effects=_DATAFLOW),
    )(*[pltpu.with_memory_space_constraint(a, pltpu.HBM) for a in arrs + lands], after)
    return outs[0], outs[1], list(outs[2:2 + n]), list(outs[2 + n:2 + 2 * n]), outs[-1]


def _exchange_wait(handle, scatter, after, name):
    send_sems, recv_sems, srcs, lands, _ = handle
    n = len(srcs)

    def body(*refs):
        ins, lz = refs[:n], refs[n:2 * n]
        for cp in _peer_copies(ins, lz, scatter, refs[2 * n], refs[2 * n + 1], False):
            cp.wait_send()
        for cp in _peer_copies(ins, lz, scatter, refs[2 * n], refs[2 * n + 1], True):
            cp.wait_recv()

    outs = pl.pallas_call(
        body, name=name,
        out_shape=[pltpu.HBM(a.shape, a.dtype) for a in srcs + lands],
        in_specs=[_HBM] * (2 * n) + [_SEM, _SEM, pl.BlockSpec(memory_space=pl.ANY)],
        out_specs=[_HBM] * (2 * n),
        input_output_aliases={i: i for i in range(2 * n)},
        compiler_params=pltpu.CompilerParams(has_side_effects=_DATAFLOW),
    )(*srcs, *lands, send_sems, recv_sems, after)
    return list(outs[:n]), list(outs[n:])


def _fill_own(lands, srcs, scatter):
    me = 4 * lax.axis_index("x") + 2 * lax.axis_index("y") + lax.axis_index("c")
    own = [lax.dynamic_index_in_dim(s, me, 0, keepdims=False) if sc else s for s, sc in zip(srcs, scatter)]
    return [lax.dynamic_update_index_in_dim(land, o, me, 0) for land, o in zip(lands, own)]


def _slot_sum(parts, name, tr):
    _, R, C = parts.shape

    def body(p_ref, o_ref):
        acc = p_ref[0].astype(F32)
        for s in range(1, N_DEV):
            acc = acc + p_ref[s].astype(F32)
        o_ref[...] = acc

    return pl.pallas_call(
        body, name=name, grid=(R // tr,),
        in_specs=[pl.BlockSpec((N_DEV, tr, C), lambda i: (0, i, 0))],
        out_specs=pl.BlockSpec((tr, C), lambda i: (i, 0)),
        out_shape=jax.ShapeDtypeStruct((R, C), F32),
        compiler_params=_params(1),
    )(parts)


def _adamw(w, g, m, v, name):
    def body(w_ref, g_ref, m_ref, v_ref, d_ref, nm_ref, nv_ref):
        g = g_ref[...]
        nm = ADAM_B1 * m_ref[...] + (1.0 - ADAM_B1) * g
        nv = ADAM_B2 * v_ref[...] + (1.0 - ADAM_B2) * jnp.square(g)
        m_hat = nm / (1.0 - ADAM_B1 ** ADAM_STEP)
        v_hat = nv / (1.0 - ADAM_B2 ** ADAM_STEP)
        d_ref[...] = -ADAM_LR * (m_hat / (jnp.sqrt(v_hat) + ADAM_EPS) + ADAM_WD * w_ref[...])
        nm_ref[...] = nm
        nv_ref[...] = nv

    return pl.pallas_call(
        body, name=name, out_shape=[jax.ShapeDtypeStruct(w.shape, F32)] * 3,
        compiler_params=pltpu.CompilerParams(vmem_limit_bytes=VMEM_LIMIT),
    )(w, g, m, v)


def _rope_tables(T):
    half = HEAD_DIM // 2
    freqs = ROPE_THETA ** (-jnp.arange(half, dtype=F32) * (2.0 / HEAD_DIM))
    ang = jnp.arange(T).astype(F32)[:, None] * jnp.tile(freqs, LANES // half)[None, :]
    sign = jnp.tile(jnp.concatenate([-jnp.ones((half,), F32), jnp.ones((half,), F32)]), LANES // HEAD_DIM)
    return jnp.cos(ang), jnp.sin(ang) * sign[None, :]


def _block_diag(w_pool):
    wbd = jnp.zeros((POOL_W, POOL_W), F32)
    g = POOL_W // len(POOL_WINDOWS)
    for i in range(len(POOL_WINDOWS)):
        wbd = wbd.at[i * g:(i + 1) * g, i * g:(i + 1) * g].set(w_pool[i])
    return wbd


def _pack_small(g1, w_pool, pool_scale, g2, g3, g4, extra):
    pad = lambda a: jnp.pad(a.reshape(1, -1), ((0, 0), (0, D_MODEL - a.size)))
    rows = [g1.reshape(1, -1), g2.reshape(1, -1), g3.reshape(1, -1), g4.reshape(1, -1),
            w_pool.reshape(-1, D_MODEL), pad(pool_scale), pad(extra)]
    buf = jnp.concatenate(rows, axis=0)
    return jnp.pad(buf, ((0, SMALL_ROWS - buf.shape[0]), (0, 0)))


def _unpack_small(buf):
    n_pool = len(POOL_WINDOWS) * (POOL_W // len(POOL_WINDOWS)) ** 2 // D_MODEL
    g = POOL_W // len(POOL_WINDOWS)
    return (buf[0:1], buf[4:4 + n_pool].reshape(1, len(POOL_WINDOWS), g, g), buf[4 + n_pool:5 + n_pool, :POOL_W],
            buf[1:2], buf[2:3], buf[3:4], buf[5 + n_pool])


class _LocalStep:
    def __init__(self, x, tgt, g1, w_pool, pool_scale, g2, g3, g4):
        self.x, self.tgt, self.pool_scale = x, tgt, pool_scale
        self.g1, self.g2, self.g3, self.g4 = g1, g2, g3, g4
        self.cos, self.sin = _rope_tables(x.shape[0])
        self.wbd = _block_diag(w_pool).astype(BF16)

    def mixer_fwd(self, w_in_t, token):
        self.w_in_t = w_in_t
        self.h1, self.u, self.q, self.k, self.v = _proj_fwd(
            self.x, self.g1 + token[0, 0], w_in_t, self.cos, self.sin, 1024)
        self.pool = _pool_fwd(self.u, self.wbd, self.pool_scale, 1024)
        alone = [_attn_fwd(self.q, self.k, self.v, dil, []) for dil in DILATIONS[:-1]]
        self.attn, self.lse = _attn_fwd(self.q, self.k, self.v, DILATIONS[-1], alone)
        return self.attn

    def ffn_fwd_bwd(self, w_out, wg_t, wu_t, w_down):
        self.w_out = w_out
        self.cat, self.mix, self.x2, h2 = _mix_fwd(self.pool, self.attn, self.x, w_out, self.g2, self.g3, 1024)
        act_dgate, act_dup, act = _ffn_up(h2, wg_t, wu_t, 256)
        df, dy, self.dg4, self.loss = _ffn_down_loss(act, w_down, self.x2, self.g4, self.tgt, 512)
        self.dgate, dup, self.dx2, self.dmix, self.dg3, self.dg2 = _ffn_bwd(
            df, w_down, act_dgate, act_dup, wg_t, wu_t, self.x2, self.mix, dy, self.g3, self.g2, 256)
        return (_wgrad(self.dgate, h2, "wgrad_gate", D_FF // 2, 2048), _wgrad(dup, h2, "wgrad_up", D_FF // 2, 2048),
                _wgrad(act, df, "wgrad_down", D_FF // 2, 2048))

    def mixer_bwd(self, token):
        dmix = self.dmix
        dpool, dattn, dsum = _mix_bwd(dmix, self.w_out, self.attn, token, 1024)
        parts = [_attn_bwd(self.q, self.k, self.v, dattn, dsum, self.lse, dil) for dil in DILATIONS]
        du, dwbd, self.dscale = _pool_bwd(self.u, dpool, self.wbd, self.pool_scale, 1024)
        g = POOL_W // len(POOL_WINDOWS)
        self.dw_pool = jnp.stack([dwbd[i * g:(i + 1) * g, i * g:(i + 1) * g] for i in range(len(POOL_WINDOWS))])
        self.dproj = _dproj_combine(du, *[[p[j] for p in parts] for j in range(5)], self.cos, self.sin, 512)
        return _wgrad(self.dproj, self.h1, "wgrad_in", IN_W // 2, 2048), _wgrad(self.cat, dmix, "wgrad_out", D_MODEL, 2048)

    def input_bwd(self, token):
        grad_x, dg1 = _proj_bwd(self.dproj, self.w_in_t, self.x, self.dx2, self.g1 + token[0, 0], 1024)
        return self.loss, grad_x, (dg1, self.dw_pool, self.dscale, self.dg2, self.dg3, self.dg4)


def _local_step(x, tgt, g1, w_pool, pool_scale, g2, g3, g4, w_in_t, w_out, wg_t, wu_t, w_down):
    zero = jnp.zeros((8, LANES), F32)
    step = _LocalStep(x, tgt, g1, w_pool, pool_scale, g2, g3, g4)
    step.mixer_fwd(w_in_t, zero)
    dw_gate, dw_up, dw_down = step.ffn_fwd_bwd(w_out, wg_t, wu_t, w_down)
    dw_in, dw_out = step.mixer_bwd(zero)
    loss, grad_x, small = step.input_bwd(zero)
    return loss, grad_x, small, (dw_in, dw_out, dw_gate, dw_up, dw_down)


def kernel(x, ln_pre_mix, w_in, w_pool, pool_scale, w_out, ln_post_mix, ln_pre_ffn, w_gate, w_up, w_down, ln_post_ffn, loss_target, m_ln_pre_mix, m_w_in, m_w_pool, m_pool_scale, m_w_out, m_ln_post_mix, m_ln_pre_ffn, m_w_gate, m_w_up, m_w_down, m_ln_post_ffn, v_ln_pre_mix, v_w_in, v_w_pool, v_pool_scale, v_w_out, v_ln_post_mix, v_ln_pre_ffn, v_w_gate, v_w_up, v_w_down, v_ln_post_ffn):
    shards = [w_in[0].T.astype(BF16), w_out[0].astype(BF16), w_gate[0].T.astype(BF16),
              w_up[0].T.astype(BF16), w_down[0].astype(BF16)]
    flat = lambda a: a.reshape(-1, D_MODEL)
    blocks = lambda a: a.reshape(N_DEV, -1, D_MODEL)
    step = _LocalStep(x[0], loss_target[0], ln_pre_mix, w_pool[0], pool_scale, ln_post_mix, ln_pre_ffn, ln_post_ffn)

    w_in_t = flat(_gather_two_level(shards[0], "gather_w_in"))
    rest = _exchange_start(shards[1:], [False] * 4, w_in_t, "gather_rest_start")
    attn = step.mixer_fwd(w_in_t, rest[4])
    srcs, lands = _exchange_wait(rest, [False] * 4, attn, "gather_rest_wait")
    w_out_f, wg_t, wu_t, w_down_f = [flat(a) for a in _fill_own(lands, srcs, [False] * 4)]

    ffn = _exchange_start([blocks(a) for a in step.ffn_fwd_bwd(w_out_f, wg_t, wu_t, w_down_f)], [True] * 3,
                          step.dgate, "grads_ffn_start")
    mixer = _exchange_start([blocks(a) for a in step.mixer_bwd(ffn[4])], [True] * 2, step.dproj, "grads_mixer_start")
    loss, grad_x, small = step.input_bwd(mixer[4])
    got = []
    for handle, n_arr, nm in ((mixer, 2, "grads_mixer"), (ffn, 3, "grads_ffn")):
        srcs, lands = _exchange_wait(handle, [True] * n_arr, grad_x, nm + "_wait")
        got += _fill_own(lands, srcs, [True] * n_arr)
    sums = [_slot_sum(got[i], f"sum_grad_{i}", got[i].shape[1] // 2) for i in range(5)]

    small_buf = _pack_small(small[0], small[1], small[2], small[3], small[4], small[5], loss)
    small_sum = _slot_sum(_exchange([small_buf], [False], "gather_small")[0], "sum_small", SMALL_ROWS)

    g_in, g_out, g_gate, g_up, g_down = sums[0].T, sums[1], sums[2].T, sums[3].T, sums[4]
    upd = [_adamw(w[0], g, m[0], v[0], f"adamw_{nm}") for nm, w, g, m, v in (
        ("in", w_in, g_in, m_w_in, v_w_in), ("out", w_out, g_out, m_w_out, v_w_out),
        ("gate", w_gate, g_gate, m_w_gate, v_w_gate), ("up", w_up, g_up, m_w_up, v_w_up),
        ("down", w_down, g_down, m_w_down, v_w_down))]
    pack = lambda a, b, c, d, e, f: _pack_small(a, b[0], c, d, e, f, jnp.zeros((1,), F32))
    small_upd = _adamw(
        pack(ln_pre_mix, w_pool, pool_scale, ln_post_mix, ln_pre_ffn, ln_post_ffn), small_sum,
        pack(m_ln_pre_mix, m_w_pool, m_pool_scale, m_ln_post_mix, m_ln_pre_ffn, m_ln_post_ffn),
        pack(v_ln_pre_mix, v_w_pool, v_pool_scale, v_ln_post_mix, v_ln_pre_ffn, v_ln_post_ffn), "adamw_small")

    def tree(small6, big5):
        s1, spool, sscale, s2, s3, s4 = small6
        b_in, b_out, b_gate, b_up, b_down = [b[None] for b in big5]
        return [s1, b_in, spool, sscale, b_out, s2, s3, b_gate, b_up, b_down, s4]

    g_small = _unpack_small(small_sum)
    outs = [g_small[6][0], grad_x[None]]
    outs += tree(g_small[:6], [g_in, g_out, g_gate, g_up, g_down])
    for j in range(3):
        outs += tree(_unpack_small(small_upd[j])[:6], [u[j] for u in upd])
    return tuple(outs)
```

```python
import jax
import jax.numpy as jnp
from jax import lax
from jax.experimental import pallas as pl
from jax.experimental.pallas import tpu as pltpu

F32 = jnp.float32
BF16 = jnp.bfloat16

D_MODEL = 1024
POOL_W = 256
ATTN_W = 768
IN_W = 2560
D_FF = 2816
POOL_WINDOWS = (2, 4, 8, 16)
POOL_HALO = 16
DILATIONS = (1, 4, 16)
BLK = 128
LANES = 128
HEAD_DIM = 64
N_GROUPS = ATTN_W // LANES
ROPE_THETA = 10000.0
EPS = 1e-6
NEG = -1e30
N_DEV = 8
SMALL_ROWS = 24

ADAM_LR = 0.001
ADAM_B1 = 0.9
ADAM_B2 = 0.999
ADAM_EPS = 1e-08
ADAM_WD = 0.01
ADAM_STEP = 10

VMEM_LIMIT = 56 * 1024 * 1024


def _dot(a, b):
    return jnp.dot(a, b, preferred_element_type=F32)


def _dot_nt(a, b):
    return lax.dot_general(a, b, (((1,), (1,)), ((), ())), preferred_element_type=F32)


def _dot_tn(a, b):
    return lax.dot_general(a, b, (((0,), (0,)), ((), ())), preferred_element_type=F32)


def _params(n_grid):
    return pltpu.CompilerParams(dimension_semantics=("arbitrary",) * n_grid, vmem_limit_bytes=VMEM_LIMIT)


def _tok(tm, c, buffers=2):
    mode = {} if buffers == 2 else dict(pipeline_mode=pl.Buffered(buffers))
    return pl.BlockSpec((tm, c), lambda i: (i, 0), **mode)


def _res(shape):
    return pl.BlockSpec(shape, lambda i: (0,) * len(shape), pipeline_mode=pl.Buffered(1))


def _acc(shape):
    return pl.BlockSpec(shape, lambda i: (0,) * len(shape))


def _rms_fwd(x, g):
    r = lax.rsqrt(jnp.mean(x * x, axis=-1, keepdims=True) + EPS)
    return x * r * g


def _rms_bwd(x, g, dy):
    r = lax.rsqrt(jnp.mean(x * x, axis=-1, keepdims=True) + EPS)
    xh = x * r
    gd = dy * g
    dx = r * (gd - xh * jnp.mean(gd * xh, axis=-1, keepdims=True))
    return dx, jnp.sum(dy * xh, axis=0, keepdims=True)


def _rope(x, c, s, sign):
    lane = lax.broadcasted_iota(jnp.int32, (x.shape[0], LANES), 1)
    first = (lane % HEAD_DIM) < (HEAD_DIM // 2)
    outs = []
    for g in range(x.shape[1] // LANES):
        xg = x[:, g * LANES:(g + 1) * LANES]
        rot = jnp.where(first, pltpu.roll(xg, LANES - HEAD_DIM // 2, 1), pltpu.roll(xg, HEAD_DIM // 2, 1))
        outs.append(xg * c + sign * (rot * s))
    return jnp.concatenate(outs, axis=1)


def _proj_fwd(x, g1, w_in_t, cos, sin, tm):
    T = x.shape[0]

    def body(x_ref, g_ref, w_ref, c_ref, s_ref, h_ref, u_ref, q_ref, k_ref, v_ref):
        h = _rms_fwd(x_ref[...], g_ref[...]).astype(BF16)
        h_ref[...] = h
        proj = _dot_nt(h, w_ref[...])
        c = c_ref[...]
        s = s_ref[...]
        u_ref[...] = proj[:, :POOL_W]
        _store_packed(q_ref, _rope(proj[:, POOL_W:POOL_W + ATTN_W], c, s, 1.0))
        _store_packed(k_ref, _rope(proj[:, POOL_W + ATTN_W:POOL_W + 2 * ATTN_W], c, s, 1.0))
        _store_packed(v_ref, proj[:, POOL_W + 2 * ATTN_W:])

    return pl.pallas_call(
        body, name="proj_fwd", grid=(T // tm,),
        in_specs=[_tok(tm, D_MODEL), _res((1, D_MODEL)), _res((IN_W, D_MODEL)), _tok(tm, LANES), _tok(tm, LANES)],
        out_specs=[_tok(tm, D_MODEL), _tok(tm, POOL_W)] + [_tok_packed(tm, ATTN_W)] * 3,
        out_shape=[jax.ShapeDtypeStruct((T, D_MODEL), BF16), jax.ShapeDtypeStruct((T, POOL_W), F32)]
        + [_packed(T, ATTN_W)] * 3,
        compiler_params=_params(1),
    )(x, g1, w_in_t, cos, sin)


def _pool_window(lane):
    return jnp.where(lane < 64, 2, jnp.where(lane < 128, 4, jnp.where(lane < 192, 8, 16)))


def _pool_select(lane, a2, a4, a8, a16):
    return jnp.where(lane < 64, a2, jnp.where(lane < 128, a4, jnp.where(lane < 192, a8, a16)))


def _pool_delta(cur, prev, i, tm):
    prev = jnp.where(i > 0, prev, 0.0)
    ext = jnp.concatenate([prev, cur], axis=0)
    s2 = ext + pltpu.roll(ext, 1, 0)
    s4 = s2 + pltpu.roll(s2, 2, 0)
    s8 = s4 + pltpu.roll(s4, 4, 0)
    s16 = s8 + pltpu.roll(s8, 8, 0)
    lane = lax.broadcasted_iota(jnp.int32, (tm, POOL_W), 1)
    row = lax.broadcasted_iota(jnp.int32, (tm, POOL_W), 0) + i * tm
    ws = _pool_select(lane, s2[POOL_HALO:], s4[POOL_HALO:], s8[POOL_HALO:], s16[POOL_HALO:])
    cnt = jnp.minimum(row + 1, _pool_window(lane)).astype(F32)
    return ws / cnt - cur


def _pool_fwd(u, wbd, scale, tm):
    T = u.shape[0]
    hb = tm // POOL_HALO

    def body(u_ref, prev_ref, w_ref, sc_ref, o_ref):
        d = _pool_delta(u_ref[...], prev_ref[...], pl.program_id(0), tm)
        o_ref[...] = (_dot(d.astype(BF16), w_ref[...]) * sc_ref[...]).astype(BF16)

    return pl.pallas_call(
        body, name="pool_fwd", grid=(T // tm,),
        in_specs=[_tok(tm, POOL_W), pl.BlockSpec((POOL_HALO, POOL_W), lambda i: (jnp.maximum(i * hb - 1, 0), 0)),
                  _res((POOL_W, POOL_W)), _res((1, POOL_W))],
        out_specs=_tok(tm, POOL_W),
        out_shape=jax.ShapeDtypeStruct((T, POOL_W), BF16),
        compiler_params=_params(1),
    )(u, u, wbd, scale)


def _attn_mask(has_prev):
    qi = lax.broadcasted_iota(jnp.int32, (BLK, 2 * BLK), 0)
    kj = lax.broadcasted_iota(jnp.int32, (BLK, 2 * BLK), 1)
    dist = qi + BLK - kj
    return (dist >= 0) & (dist <= BLK) & ((kj >= BLK) | has_prev)


def _stack_heads(x, lo):
    zero = jnp.zeros_like(x)
    return jnp.concatenate([jnp.where(lo, x, zero), jnp.where(lo, zero, x)], axis=0)


def _head_col(tile, lane, h):
    return jnp.sum(jnp.where(lane == h, tile, 0.0), axis=1, keepdims=True)


def _attn_cols(dil):
    return ATTN_W // 2 if dil >= 16 else ATTN_W


def _units_per_step(dil, backward):
    return {1: 4, 4: 2, 16: 8}[dil]


def _chunk_tokens(dil, units):
    return BLK * (units if dil == 1 else dil)


def _unit_steps(dil, units):
    return 1 if dil == 1 else dil // 2 // units


def _attn_specs(dil, nb, units):
    cw = _attn_cols(dil)
    ch = _chunk_tokens(dil, units)
    wide = lambda f: pl.BlockSpec((cw // LANES, ch // 2, LANES), f)
    cur = lambda n, j, r: (j, n, 0)
    prv = lambda n, j, r: (j, jnp.maximum(n - 1, 0), 0)
    prv_out = lambda n, j, r: (j, (n + nb - 1) % nb, 0)
    heads = pl.BlockSpec((ch, LANES), lambda n, j, r: (n, 0))
    return cw, wide(cur), wide(prv), wide(prv_out), heads


HIGH_HALF = 0xFFFF0000


def _pack(x):
    return pltpu.bitcast(x.astype(BF16), F32)


def _unpack(words):
    return pltpu.bitcast(words, BF16)


def _packed(rows, cols):
    return jax.ShapeDtypeStruct((cols // LANES, rows // 2, LANES), F32)


def _tok_packed(tm, cols, buffers=2):
    mode = {} if buffers == 2 else dict(pipeline_mode=pl.Buffered(buffers))
    return pl.BlockSpec((cols // LANES, tm // 2, LANES), lambda i: (0, i, 0), **mode)


def _store_packed(ref, x):
    for g in range(x.shape[1] // LANES):
        ref[g] = _pack(x[:, g * LANES:(g + 1) * LANES])


def _load_packed(ref):
    return jnp.concatenate([_unpack(ref[g]) for g in range(ref.shape[0])], axis=1)


def _load_streams(ref, dil, r2, sl):
    if dil == 1:
        return [_unpack(ref.at[sl][pl.ds(r2 * (BLK // 2), BLK // 2), :])]
    words = lax.bitcast_convert_type(ref.at[sl][pl.ds(r2, BLK, stride=dil // 2), :], jnp.uint32)
    even = lax.bitcast_convert_type(words << 16, F32).astype(BF16)
    odd = lax.bitcast_convert_type(words & jnp.uint32(HIGH_HALF), F32).astype(BF16)
    return [even, odd]


def _load_prev_streams(prev_ref, cur_ref, dil, units, r2, sl):
    if dil > 1:
        return _load_streams(prev_ref, dil, r2, sl)
    return _load_streams(cur_ref, 1, r2 - 1, sl) if r2 > 0 else _load_streams(prev_ref, 1, units - 1, sl)


def _load_streams_f32(ref, dil, r2, sl):
    ref = ref if sl is None else ref.at[sl]
    if dil == 1:
        return [ref[pl.ds(r2 * BLK, BLK), :]]
    return [ref[pl.ds(2 * r2 + e, BLK, stride=dil), :] for e in range(2)]


def _store_streams_f32(ref, dil, r2, sl, tiles):
    ref = ref if sl is None else ref.at[sl]
    if dil == 1:
        ref[pl.ds(r2 * BLK, BLK), :] = tiles[0]
    else:
        for e, t in enumerate(tiles):
            ref[pl.ds(2 * r2 + e, BLK, stride=dil), :] = t


def _store_streams(ref, dil, r2, sl, tiles):
    if dil == 1:
        ref.at[sl][pl.ds(r2 * (BLK // 2), BLK // 2), :] = _pack(tiles[0])
    else:
        even, odd = [lax.bitcast_convert_type(t.astype(BF16).astype(F32), jnp.uint32) for t in tiles]
        words = (odd & jnp.uint32(HIGH_HALF)) | (even >> 16)
        ref.at[sl][pl.ds(r2, BLK, stride=dil // 2), :] = lax.bitcast_convert_type(words, F32)


def _attn_fwd(q, k, v, dil, others):
    T = 2 * q.shape[1]
    reps = _units_per_step(dil, False)
    nb = T // _chunk_tokens(dil, reps)
    first = not others
    cw, cur, prv, _, heads = _attn_specs(dil, nb, reps)
    ncb = ATTN_W // cw
    heads_per_step = cw // HEAD_DIM
    n_str = min(dil, 2)
    everything = None

    def body(*refs):
        q_ref, kc_ref, kp_ref, vc_ref, vp_ref = refs[:5]
        acc_ins, lse_ins = refs[5:5 + 2 * len(others):2], refs[6:6 + 2 * len(others):2]
        acc_ref, lse_ref = refs[-2:]
        j = pl.program_id(1)
        lane = lax.broadcasted_iota(jnp.int32, (BLK, LANES), 1)
        lo = lane < HEAD_DIM

        def stream_pair(r2):
            valid = _attn_mask(True if dil == 1 and r2 > 0 else pl.program_id(0) > 0)
            lse_tiles = [jnp.zeros((BLK, LANES), F32) for _ in range(n_str)]
            own = []
            for g in range(cw // LANES):
                qs, kcs, vcs = [_load_streams(r, dil, r2, g) for r in (q_ref, kc_ref, vc_ref)]
                kps, vps = [_load_prev_streams(p, c, dil, reps, r2, g) for p, c in ((kp_ref, kc_ref), (vp_ref, vc_ref))]
                pairs = []
                for e in range(n_str):
                    qg = qs[e] * 0.125
                    kcat = jnp.concatenate([kps[e], kcs[e]], axis=0)
                    vcat = jnp.concatenate([vps[e], vcs[e]], axis=0)
                    pair = None
                    for hh in range(2):
                        h = j * heads_per_step + 2 * g + hh
                        hm = lo if hh == 0 else jnp.logical_not(lo)
                        s = _dot_nt(jnp.where(hm, qg, jnp.zeros_like(qg)), kcat)
                        s = jnp.where(valid, s, NEG)
                        m = jnp.max(s, axis=1, keepdims=True)
                        p = jnp.exp(s - m)
                        den = jnp.sum(p, axis=1, keepdims=True)
                        o = _dot(p.astype(BF16), vcat) / den
                        pair = o if hh == 0 else jnp.where(lo, pair, o)
                        lse_tiles[e] = jnp.where(lane == h, m + jnp.log(den), lse_tiles[e])
                    pairs.append(pair)
                if first:
                    _store_streams(acc_ref, dil, r2, g, pairs)
                else:
                    own.append(pairs)
            if not first:
                mine = (lane >= j * heads_per_step) & (lane < (j + 1) * heads_per_step)
                theirs = [_load_streams_f32(ref, dil, r2, everything) for ref in lse_ins]
                w_theirs, w_own = [[] for _ in others], []
                for e in range(n_str):
                    parts = [t[e] for t in theirs] + [lse_tiles[e]]
                    mx = parts[0]
                    for part in parts[1:]:
                        mx = jnp.maximum(mx, part)
                    total = mx + jnp.log(sum(jnp.exp(part - mx) for part in parts))
                    for i, t in enumerate(theirs):
                        w_theirs[i].append(jnp.exp(t[e] - total))
                    w_own.append(jnp.exp(lse_tiles[e] - total))
                    lse_tiles[e] = jnp.where(mine, total, 0.0)
                for g in range(cw // LANES):
                    h0 = j * heads_per_step + 2 * g
                    spread = lambda w: jnp.where(lo, _head_col(w, lane, h0), _head_col(w, lane, h0 + 1))
                    olds = [_load_streams(ref, dil, r2, g) for ref in acc_ins]
                    _store_streams(acc_ref, dil, r2, g, [
                        sum(olds[i][e].astype(F32) * spread(w_theirs[i][e]) for i in range(len(others)))
                        + own[g][e] * spread(w_own[e]) for e in range(n_str)])
            if ncb == 1:
                _store_streams_f32(lse_ref, dil, r2, everything, lse_tiles)
            else:
                @pl.when(j == 0)
                def _():
                    _store_streams_f32(lse_ref, dil, r2, everything, lse_tiles)

                @pl.when(j > 0)
                def _():
                    before = _load_streams_f32(lse_ref, dil, r2, everything)
                    _store_streams_f32(lse_ref, dil, r2, everything, [a + b for a, b in zip(before, lse_tiles)])

        for rep in range(reps):
            stream_pair(rep if dil == 1 else pl.program_id(2) * reps + rep)

    ins = [q, k, k, v, v]
    in_specs = [cur, cur, prv, cur, prv]
    for acc, lse in others:
        ins += [acc, lse]
        in_specs += [cur, heads]
    return pl.pallas_call(
        body, name=f"attn_fwd_d{dil}", grid=(nb, ncb, _unit_steps(dil, reps)),
        in_specs=in_specs, out_specs=[cur, heads],
        out_shape=[_packed(T, ATTN_W), jax.ShapeDtypeStruct((T, LANES), F32)],
        compiler_params=_params(3),
    )(*ins)


def _mix_fwd(pool, attn, x, w_out, g2, g3, tm):
    T = x.shape[0]

    def body(p_ref, a_ref, x_ref, w_ref, g2_ref, g3_ref, cat_ref, mix_ref, x2_ref, h2_ref):
        p = p_ref[...]
        a = _load_packed(a_ref)
        cat_ref[...] = jnp.concatenate([p, a], axis=1)
        mix = _dot(p, w_ref[:POOL_W, :]) + _dot(a, w_ref[POOL_W:, :])
        mix_ref[...] = mix
        x2 = x_ref[...] + _rms_fwd(mix, g2_ref[...])
        x2_ref[...] = x2
        h2_ref[...] = _rms_fwd(x2, g3_ref[...]).astype(BF16)

    in_specs = [_tok(tm, POOL_W, 3), _tok_packed(tm, ATTN_W, 3), _tok(tm, D_MODEL, 3), _acc((D_MODEL, D_MODEL)),
                _acc((1, D_MODEL)), _acc((1, D_MODEL))]
    out_specs = [_tok(tm, D_MODEL)] * 4

    def streamed(*refs):
        pltpu.emit_pipeline(body, grid=(T // tm,), in_specs=in_specs, out_specs=out_specs)(*refs)

    hbm = pl.BlockSpec(memory_space=pl.ANY)
    return pl.pallas_call(
        streamed, name="mix_fwd", in_specs=[hbm] * 6, out_specs=[hbm] * 4,
        out_shape=[jax.ShapeDtypeStruct((T, D_MODEL), BF16), jax.ShapeDtypeStruct((T, D_MODEL), F32),
                   jax.ShapeDtypeStruct((T, D_MODEL), F32), jax.ShapeDtypeStruct((T, D_MODEL), BF16)],
        compiler_params=pltpu.CompilerParams(vmem_limit_bytes=VMEM_LIMIT),
    )(pool, attn, x, w_out, g2, g3)


def _ffn_up(h2, wg_t, wu_t, tm):
    T = h2.shape[0]

    def body(h_ref, wg_ref, wu_ref, dg_ref, du_ref, a_ref):
        h = h_ref[...]
        gate = _dot_nt(h, wg_ref[...])
        up = _dot_nt(h, wu_ref[...])
        sg = 1.0 / (1.0 + jnp.exp(-gate))
        silu = gate * sg
        a_ref[...] = (silu * up).astype(BF16)
        dg_ref[...] = (up * (sg * (1.0 + gate * (1.0 - sg)))).astype(BF16)
        du_ref[...] = silu.astype(BF16)

    return pl.pallas_call(
        body, name="ffn_up", grid=(T // tm,),
        in_specs=[_tok(tm, D_MODEL), _res((D_FF, D_MODEL)), _res((D_FF, D_MODEL))],
        out_specs=[_tok(tm, D_FF)] * 3,
        out_shape=[jax.ShapeDtypeStruct((T, D_FF), BF16)] * 3,
        compiler_params=_params(1),
    )(h2, wg_t, wu_t)


def _ffn_down_loss(act, w_down, x2, g4, tgt, tm):
    T = act.shape[0]

    def body(a_ref, w_ref, x2_ref, g_ref, t_ref, df_ref, dy_ref, dg_ref, loss_ref):
        i = pl.program_id(0)

        @pl.when(i == 0)
        def _():
            dg_ref[...] = jnp.zeros_like(dg_ref)
            loss_ref[...] = jnp.zeros_like(loss_ref)

        f = _dot(a_ref[...], w_ref[...])
        g = g_ref[...]
        err = x2_ref[...] + _rms_fwd(f, g) - t_ref[...]
        loss_ref[...] += 0.5 * jnp.sum(jnp.mean(err * err, axis=-1, keepdims=True), axis=0, keepdims=True)
        dy = err * (1.0 / D_MODEL)
        dy_ref[...] = dy
        df, dg = _rms_bwd(f, g, dy)
        dg_ref[...] += dg
        df_ref[...] = df.astype(BF16)

    return pl.pallas_call(
        body, name="ffn_down_loss", grid=(T // tm,),
        in_specs=[_tok(tm, D_FF), _res((D_FF, D_MODEL)), _tok(tm, D_MODEL), _res((1, D_MODEL)), _tok(tm, D_MODEL)],
        out_specs=[_tok(tm, D_MODEL), _tok(tm, D_MODEL), _acc((1, D_MODEL)), _acc((1, 1))],
        out_shape=[jax.ShapeDtypeStruct((T, D_MODEL), BF16), jax.ShapeDtypeStruct((T, D_MODEL), F32),
                   jax.ShapeDtypeStruct((1, D_MODEL), F32), jax.ShapeDtypeStruct((1, 1), F32)],
        compiler_params=_params(1),
    )(act, w_down, x2, g4, tgt)


def _ffn_bwd(df, w_down, act_dgate, act_dup, wg_t, wu_t, x2, mix, dy, g3, g2, tm):
    T = x2.shape[0]

    def body(df_ref, wd_ref, ag_ref, au_ref, wg_ref, wu_ref, x2_ref, mix_ref, dy_ref, g3_ref, g2_ref,
             dgate_ref, dup_ref, dx2_ref, dmix_ref, dg3_ref, dg2_ref):
        @pl.when(pl.program_id(0) == 0)
        def _():
            dg3_ref[...] = jnp.zeros_like(dg3_ref)
            dg2_ref[...] = jnp.zeros_like(dg2_ref)

        dact = _dot_nt(df_ref[...], wd_ref[...])
        dgate = (dact * ag_ref[...].astype(F32)).astype(BF16)
        dup = (dact * au_ref[...].astype(F32)).astype(BF16)
        dgate_ref[...] = dgate
        dup_ref[...] = dup
        dh2 = _dot(dgate, wg_ref[...]) + _dot(dup, wu_ref[...])
        dn, dg3 = _rms_bwd(x2_ref[...], g3_ref[...], dh2)
        dx2 = dy_ref[...] + dn
        dx2_ref[...] = dx2
        dg3_ref[...] += dg3
        dmix, dg2 = _rms_bwd(mix_ref[...], g2_ref[...], dx2)
        dg2_ref[...] += dg2
        dmix_ref[...] = dmix.astype(BF16)

    wide, narrow, weight, gain = _tok(tm, D_FF), _tok(tm, D_MODEL), _res((D_FF, D_MODEL)), _res((1, D_MODEL))
    return pl.pallas_call(
        body, name="ffn_bwd", grid=(T // tm,),
        in_specs=[narrow, weight, wide, wide, weight, weight, narrow, narrow, narrow, gain, gain],
        out_specs=[wide, wide, narrow, narrow, _acc((1, D_MODEL)), _acc((1, D_MODEL))],
        out_shape=[jax.ShapeDtypeStruct((T, D_FF), BF16), jax.ShapeDtypeStruct((T, D_FF), BF16),
                   jax.ShapeDtypeStruct((T, D_MODEL), F32), jax.ShapeDtypeStruct((T, D_MODEL), BF16),
                   jax.ShapeDtypeStruct((1, D_MODEL), F32), jax.ShapeDtypeStruct((1, D_MODEL), F32)],
        compiler_params=_params(1),
    )(df, w_down, act_dgate, act_dup, wg_t, wu_t, x2, mix, dy, g3, g2)


def _mix_bwd(dmix, w_out, attn, after, tm):
    T = dmix.shape[0]

    def body(d_ref, w_ref, a_ref, after_ref, dp_ref, da_ref, ds_ref):
        dcat = _dot_nt(d_ref[...], w_ref[...])
        dp_ref[...] = dcat[:, :POOL_W].astype(BF16)
        dattn = dcat[:, POOL_W:].astype(BF16)
        _store_packed(da_ref, dattn)
        prod = dattn.astype(F32) * _load_packed(a_ref).astype(F32)
        lane = lax.broadcasted_iota(jnp.int32, (tm, LANES), 1)
        lo = lane < HEAD_DIM
        dsum = jnp.zeros((tm, LANES), F32)
        for g in range(N_GROUPS):
            pg = prod[:, g * LANES:(g + 1) * LANES]
            dsum = jnp.where(lane == 2 * g, jnp.sum(jnp.where(lo, pg, 0.0), axis=1, keepdims=True), dsum)
            dsum = jnp.where(lane == 2 * g + 1, jnp.sum(jnp.where(lo, 0.0, pg), axis=1, keepdims=True), dsum)
        ds_ref[...] = dsum

    return pl.pallas_call(
        body, name="mix_bwd", grid=(T // tm,),
        in_specs=[_tok(tm, D_MODEL), _res((D_MODEL, D_MODEL)), _tok_packed(tm, ATTN_W), pl.BlockSpec(memory_space=pl.ANY)],
        out_specs=[_tok(tm, POOL_W), _tok_packed(tm, ATTN_W), _tok(tm, LANES)],
        out_shape=[jax.ShapeDtypeStruct((T, POOL_W), BF16), _packed(T, ATTN_W), jax.ShapeDtypeStruct((T, LANES), F32)],
        compiler_params=_params(1),
    )(dmix, w_out, attn, after)


def _attn_bwd(q, k, v, dout, dsum, lse, dil):
    T = 2 * q.shape[1]
    reps = _units_per_step(dil, True)
    nb = T // _chunk_tokens(dil, reps)
    cw, cur, prv, prv_out, heads = _attn_specs(dil, nb, reps)
    ncb = ATTN_W // cw
    heads_per_step = cw // HEAD_DIM
    n_str = min(dil, 2)

    def body(q_ref, kc_ref, kp_ref, vc_ref, vp_ref, do_ref, dsum_ref, lse_ref,
             dq_ref, dkc_ref, dkp_ref, dvc_ref, dvp_ref):
        j = pl.program_id(1)
        lane = lax.broadcasted_iota(jnp.int32, (BLK, LANES), 1)
        lo = lane < HEAD_DIM
        unit = lambda rep: rep if dil == 1 else pl.program_id(2) * reps + rep
        stats = [[_load_streams_f32(ref, dil, unit(rep), None) for ref in (lse_ref, dsum_ref)] for rep in range(reps)]

        def unit_grads(rep, g):
            r2 = unit(rep)
            valid = _attn_mask(True if dil == 1 and r2 > 0 else pl.program_id(0) > 0)
            valid2 = jnp.concatenate([valid, valid], axis=0)
            lse_tiles, dsum_tiles = stats[rep]
            qs, kcs, vcs, dos = [_load_streams(r, dil, r2, g) for r in (q_ref, kc_ref, vc_ref, do_ref)]
            kps, vps = [_load_prev_streams(p, c, dil, reps, r2, g) for p, c in ((kp_ref, kc_ref), (vp_ref, vc_ref))]
            dqs, dks, dvs = [], [], []
            for e in range(n_str):
                qg = qs[e] * 0.125
                dog = dos[e]
                kcat = jnp.concatenate([kps[e], kcs[e]], axis=0)
                vcat = jnp.concatenate([vps[e], vcs[e]], axis=0)
                h0 = j * heads_per_step + 2 * g
                q2 = _stack_heads(qg, lo)
                do2 = _stack_heads(dog, lo)
                both = lambda tile: jnp.concatenate([_head_col(tile, lane, h0), _head_col(tile, lane, h0 + 1)], axis=0)
                lse2, dsum2 = both(lse_tiles[e]), both(dsum_tiles[e])
                p = jnp.exp(jnp.where(valid2, _dot_nt(q2, kcat), NEG) - lse2)
                ds = (p * (_dot_nt(do2, vcat) - dsum2)).astype(BF16)
                dvs.append(_dot_tn(p.astype(BF16), do2))
                dks.append(_dot_tn(ds, q2))
                dq2 = _dot(ds, kcat) * 0.125
                dqs.append(jnp.where(lo, dq2[:BLK], dq2[BLK:]))
            return dqs, dks, dvs

        for g in range(cw // LANES):
            if dil > 1:
                for rep in range(reps):
                    r2 = unit(rep)
                    dqs, dks, dvs = unit_grads(rep, g)
                    _store_streams(dq_ref, dil, r2, g, dqs)
                    _store_streams(dkp_ref, dil, r2, g, [t[:BLK] for t in dks])
                    _store_streams(dkc_ref, dil, r2, g, [t[BLK:] for t in dks])
                    _store_streams(dvp_ref, dil, r2, g, [t[:BLK] for t in dvs])
                    _store_streams(dvc_ref, dil, r2, g, [t[BLK:] for t in dvs])
            else:
                blocks = [unit_grads(b, g) for b in range(reps)]
                for b, (dqs, dks, dvs) in enumerate(blocks):
                    _store_streams(dq_ref, 1, b, g, dqs)
                    for cur_ref, prev_ref, which in ((dkc_ref, dkp_ref, 1), (dvc_ref, dvp_ref, 2)):
                        own = blocks[b][which][0][BLK:]
                        if b + 1 < reps:
                            own = own + blocks[b + 1][which][0][:BLK]
                        _store_streams(cur_ref, 1, b, g, [own])
                        edge = blocks[0][which][0][:BLK] if b == reps - 1 else jnp.zeros((BLK, LANES), F32)
                        _store_streams(prev_ref, 1, b, g, [edge])

    return pl.pallas_call(
        body, name=f"attn_bwd_d{dil}", grid=(nb, ncb, _unit_steps(dil, reps)),
        in_specs=[cur, cur, prv, cur, prv, cur, heads, heads],
        out_specs=[cur, cur, prv_out, cur, prv_out],
        out_shape=[_packed(T, ATTN_W)] * 5,
        compiler_params=_params(3),
    )(q, k, k, v, v, dout, dsum, lse)


def _pool_bwd(u, dy, wbd, scale, tm):
    T = u.shape[0]
    nt = T // tm
    hb = tm // POOL_HALO

    def body(u_ref, prev_ref, dy_ref, next_ref, w_ref, sc_ref, du_ref, dw_ref, dsc_ref):
        i = pl.program_id(0)

        @pl.when(i == 0)
        def _():
            dw_ref[...] = jnp.zeros_like(dw_ref)
            dsc_ref[...] = jnp.zeros_like(dsc_ref)

        w = w_ref[...]
        sc = sc_ref[...]
        d = _pool_delta(u_ref[...], prev_ref[...], i, tm).astype(BF16)
        dyc = dy_ref[...].astype(F32)
        dsc_ref[...] += jnp.sum(dyc * _dot(d, w), axis=0, keepdims=True)
        nxt = jnp.where(i < nt - 1, next_ref[...].astype(F32), 0.0)
        dypre = (jnp.concatenate([dyc, nxt], axis=0) * sc).astype(BF16)
        dw_ref[...] += _dot_tn(d, dypre[:tm])
        dd = _dot_nt(dypre, w)
        n = tm + POOL_HALO
        lane = lax.broadcasted_iota(jnp.int32, (n, POOL_W), 1)
        row = lax.broadcasted_iota(jnp.int32, (n, POOL_W), 0) + i * tm
        gx = dd / jnp.minimum(row + 1, _pool_window(lane)).astype(F32)
        a2 = gx + pltpu.roll(gx, n - 1, 0)
        a4 = a2 + pltpu.roll(a2, n - 2, 0)
        a8 = a4 + pltpu.roll(a4, n - 4, 0)
        a16 = a8 + pltpu.roll(a8, n - 8, 0)
        fs = _pool_select(lane[:tm], a2[:tm], a4[:tm], a8[:tm], a16[:tm])
        du_ref[...] = (fs - dd[:tm]).astype(BF16)

    return pl.pallas_call(
        body, name="pool_bwd", grid=(nt,),
        in_specs=[_tok(tm, POOL_W), pl.BlockSpec((POOL_HALO, POOL_W), lambda i: (jnp.maximum(i * hb - 1, 0), 0)),
                  _tok(tm, POOL_W), pl.BlockSpec((POOL_HALO, POOL_W), lambda i: (jnp.minimum((i + 1) * hb, nt * hb - 1), 0)),
                  _res((POOL_W, POOL_W)), _res((1, POOL_W))],
        out_specs=[_tok(tm, POOL_W), _acc((POOL_W, POOL_W)), _acc((1, POOL_W))],
        out_shape=[jax.ShapeDtypeStruct((T, POOL_W), BF16), jax.ShapeDtypeStruct((POOL_W, POOL_W), F32),
                   jax.ShapeDtypeStruct((1, POOL_W), F32)],
        compiler_params=_params(1),
    )(u, u, dy, dy, wbd, scale)


def _dproj_combine(du, dqs, dkcs, dkps, dvcs, dvps, cos, sin, tm):
    T = du.shape[0]
    n_cfg = len(dqs)

    def body(*refs):
        du_ref = refs[0]
        groups = [refs[1 + j * n_cfg:1 + (j + 1) * n_cfg] for j in range(5)]
        c_ref, s_ref, out_ref = refs[1 + 5 * n_cfg:]
        tot = lambda rs: sum(_load_packed(r).astype(F32) for r in rs)
        c = c_ref[...]
        s = s_ref[...]
        dq = _rope(tot(groups[0]), c, s, -1.0)
        dk = _rope(tot(groups[1]) + tot(groups[2]), c, s, -1.0)
        dv = tot(groups[3]) + tot(groups[4])
        out_ref[...] = jnp.concatenate([du_ref[...], dq.astype(BF16), dk.astype(BF16), dv.astype(BF16)], axis=1)

    return pl.pallas_call(
        body, name="dproj_combine", grid=(T // tm,),
        in_specs=[_tok(tm, POOL_W)] + [_tok_packed(tm, ATTN_W)] * (5 * n_cfg) + [_tok(tm, LANES)] * 2,
        out_specs=_tok(tm, IN_W),
        out_shape=jax.ShapeDtypeStruct((T, IN_W), BF16),
        compiler_params=_params(1),
    )(du, *dqs, *dkcs, *dkps, *dvcs, *dvps, cos, sin)


def _proj_bwd(dproj, w_in_t, x, dx2, g1, tm):
    T = x.shape[0]

    def body(d_ref, w_ref, x_ref, r_ref, g_ref, dx_ref, dg_ref):
        @pl.when(pl.program_id(0) == 0)
        def _():
            dg_ref[...] = jnp.zeros_like(dg_ref)

        dn, dg = _rms_bwd(x_ref[...], g_ref[...], _dot(d_ref[...], w_ref[...]))
        dg_ref[...] += dg
        dx_ref[...] = r_ref[...] + dn

    return pl.pallas_call(
        body, name="proj_bwd", grid=(T // tm,),
        in_specs=[_tok(tm, IN_W), _res((IN_W, D_MODEL)), _tok(tm, D_MODEL), _tok(tm, D_MODEL), _res((1, D_MODEL))],
        out_specs=[_tok(tm, D_MODEL), _acc((1, D_MODEL))],
        out_shape=[jax.ShapeDtypeStruct((T, D_MODEL), F32), jax.ShapeDtypeStruct((1, D_MODEL), F32)],
        compiler_params=_params(1),
    )(dproj, w_in_t, x, dx2, g1)


def _wgrad(a, b, name, tile_m, tk):
    T, M = a.shape
    N = b.shape[1]
    nk = T // tk

    def body(a_ref, b_ref, o_ref, acc_ref):
        kk = pl.program_id(1)

        @pl.when(kk == 0)
        def _():
            acc_ref[...] = jnp.zeros_like(acc_ref)

        acc_ref[...] += _dot_tn(a_ref[...], b_ref[...])

        @pl.when(kk == nk - 1)
        def _():
            o_ref[...] = acc_ref[...].astype(BF16)

    return pl.pallas_call(
        body, name=name, grid=(M // tile_m, nk),
        in_specs=[pl.BlockSpec((tk, tile_m), lambda j, kk: (kk, j)), pl.BlockSpec((tk, N), lambda j, kk: (kk, 0))],
        out_specs=pl.BlockSpec((tile_m, N), lambda j, kk: (j, 0)),
        out_shape=jax.ShapeDtypeStruct((M, N), BF16),
        scratch_shapes=[pltpu.VMEM((tile_m, N), F32)],
        compiler_params=_params(2),
    )(a, b)


def _exchange(arrs, scatter, name):
    n = len(arrs)
    out_shapes = [jax.ShapeDtypeStruct((N_DEV,) + (a.shape[1:] if sc else a.shape), a.dtype)
                  for a, sc in zip(arrs, scatter)]

    def body(*refs):
        ins, outs = refs[:n], refs[n:2 * n]
        send_sems, recv_sems, loc_sems = refs[2 * n:]
        x, y, c = lax.axis_index("x"), lax.axis_index("y"), lax.axis_index("c")
        me = 4 * x + 2 * y + c
        local, sends, recvs = [], [], []
        for i in range(n):
            own = ins[i].at[me] if scatter[i] else ins[i]
            loc = pltpu.make_async_copy(own, outs[i].at[me], loc_sems.at[i])
            loc.start()
            local.append(loc)
            for kbits in range(1, N_DEV):
                px = 1 - x if kbits & 4 else x
                py = 1 - y if kbits & 2 else y
                pc = 1 - c if kbits & 1 else c
                pid = 4 * px + 2 * py + pc
                src = ins[i].at[pid] if scatter[i] else ins[i]
                cp = pltpu.make_async_remote_copy(
                    src_ref=src, dst_ref=outs[i].at[me],
                    send_sem=send_sems.at[i, kbits - 1], recv_sem=recv_sems.at[i, kbits - 1],
                    device_id=(px, py, pc), device_id_type=pl.DeviceIdType.MESH)
                cp.start()
                sends.append(cp)
                recvs.append(pltpu.make_async_remote_copy(
                    src_ref=src, dst_ref=outs[i].at[pid],
                    send_sem=send_sems.at[i, kbits - 1], recv_sem=recv_sems.at[i, kbits - 1],
                    device_id=(px, py, pc), device_id_type=pl.DeviceIdType.MESH))
        for cp in recvs:
            cp.wait_recv()
        for cp in sends:
            cp.wait_send()
        for cp in local:
            cp.wait()

    hbm = pl.BlockSpec(memory_space=pl.ANY)
    return pl.pallas_call(
        body, name=name, in_specs=[hbm] * n, out_specs=[hbm] * n, out_shape=out_shapes,
        scratch_shapes=[pltpu.SemaphoreType.DMA((n, N_DEV - 1)), pltpu.SemaphoreType.DMA((n, N_DEV - 1)),
                        pltpu.SemaphoreType.DMA((n,))],
    )(*arrs)


def _gather_two_level(arr, name):
    def body(x_ref, out_ref, send_sems, recv_sems, local_sem):
        x, y, c = lax.axis_index("x"), lax.axis_index("y"), lax.axis_index("c")
        me, sibling = (x, y, c), (x, y, 1 - c)
        chips = [(1 - x, y), (x, 1 - y), (1 - x, 1 - y)]
        slot = lambda px, py, pc: out_ref.at[4 * px + 2 * py + pc]

        def copy(k, block, to, src=None):
            return pltpu.make_async_remote_copy(
                src_ref=slot(*block) if src is None else src, dst_ref=slot(*block),
                send_sem=send_sems.at[k], recv_sem=recv_sems.at[k],
                device_id=to, device_id_type=pl.DeviceIdType.MESH)

        mine = pltpu.make_async_copy(x_ref, slot(*me), local_sem)
        mine.start()
        first = [copy(0, me, sibling, src=x_ref)]
        first += [copy(1 + i, me, (*chip, c), src=x_ref) for i, chip in enumerate(chips)]
        for cp in first:
            cp.start()
        passed = [copy(4 + i, (*chip, c), sibling) for i, chip in enumerate(chips)]
        for i, chip in enumerate(chips):
            copy(1 + i, (*chip, c), me).wait_recv()
            passed[i].start()
        copy(0, sibling, me).wait_recv()
        for i, chip in enumerate(chips):
            copy(4 + i, (*chip, 1 - c), me).wait_recv()
        for cp in first + passed:
            cp.wait_send()
        mine.wait()

    hbm = pl.BlockSpec(memory_space=pl.ANY)
    return pl.pallas_call(
        body, name=name, in_specs=[hbm], out_specs=hbm,
        out_shape=jax.ShapeDtypeStruct((N_DEV,) + arr.shape, arr.dtype),
        scratch_shapes=[pltpu.SemaphoreType.DMA((N_DEV - 1,)), pltpu.SemaphoreType.DMA((N_DEV - 1,)),
                        pltpu.SemaphoreType.DMA],
    )(arr)


def _peers(x, y, c):
    for kbits in range(1, N_DEV):
        px = 1 - x if kbits & 4 else x
        py = 1 - y if kbits & 2 else y
        pc = 1 - c if kbits & 1 else c
        yield kbits - 1, (px, py, pc), 4 * px + 2 * py + pc


def _peer_copies(ins, lands, scatter, send_sems, recv_sems, incoming):
    x, y, c = lax.axis_index("x"), lax.axis_index("y"), lax.axis_index("c")
    me = 4 * x + 2 * y + c
    copies = []
    for i in range(len(ins)):
        for k, peer, pid in _peers(x, y, c):
            slot = i * (N_DEV - 1) + k
            copies.append(pltpu.make_async_remote_copy(
                src_ref=ins[i].at[pid] if scatter[i] else ins[i], dst_ref=lands[i].at[pid if incoming else me],
                send_sem=send_sems.at[slot], recv_sem=recv_sems.at[slot],
                device_id=peer, device_id_type=pl.DeviceIdType.MESH))
    return copies


_HBM = pl.BlockSpec(memory_space=pltpu.HBM)
_SEM = pl.BlockSpec(memory_space=pltpu.SEMAPHORE)
_DATAFLOW = pltpu.SideEffectType.DATAFLOW_SIDE_EFFECTING


def _exchange_start(arrs, scatter, after, name):
    n = len(arrs)
    lands = [lax.empty((N_DEV,) + (a.shape[1:] if sc else a.shape), a.dtype) for a, sc in zip(arrs, scatter)]

    def body(*refs):
        ins, lz = refs[:n], refs[n:2 * n]
        send_sems, recv_sems = refs[2 * n + 1:2 * n + 3]
        token = refs[-1]
        for cp in _peer_copies(ins, lz, scatter, send_sems, recv_sems, False):
            cp.start()
        token[...] = jnp.zeros_like(token)

    sem_shape = pltpu.SemaphoreType.DMA((n * (N_DEV - 1),))
    outs = pl.pallas_call(
        body, name=name,
        out_shape=(sem_shape, sem_shape, *[pltpu.HBM(a.shape, a.dtype) for a in arrs + lands],
                   jax.ShapeDtypeStruct((8, LANES), F32)),
        in_specs=[_HBM] * (2 * n) + [pl.BlockSpec(memory_space=pl.ANY)],
        out_specs=(_SEM, _SEM, *[_HBM] * (2 * n), pl.BlockSpec(memory_space=pltpu.VMEM)),
        input_output_aliases={i: 2 + i for i in range(2 * n)},
        compiler_params=pltpu.CompilerParams(has_side_effects=_DATAFLOW),
    )(*[pltpu.with_memory_space_constraint(a, pltpu.HBM) for a in arrs + lands], after)
    return outs[0], outs[1], list(outs[2:2 + n]), list(outs[2 + n:2 + 2 * n]), outs[-1]


def _exchange_wait(handle, scatter, after, name):
    send_sems, recv_sems, srcs, lands, _ = handle
    n = len(srcs)

    def body(*refs):
        ins, lz = refs[:n], refs[n:2 * n]
        for cp in _peer_copies(ins, lz, scatter, refs[2 * n], refs[2 * n + 1], False):
            cp.wait_send()
        for cp in _peer_copies(ins, lz, scatter, refs[2 * n], refs[2 * n + 1], True):
            cp.wait_recv()

    outs = pl.pallas_call(
        body, name=name,
        out_shape=[pltpu.HBM(a.shape, a.dtype) for a in srcs + lands],
        in_specs=[_HBM] * (2 * n) + [_SEM, _SEM, pl.BlockSpec(memory_space=pl.ANY)],
        out_specs=[_HBM] * (2 * n),
        input_output_aliases={i: i for i in range(2 * n)},
        compiler_params=pltpu.CompilerParams(has_side_effects=_DATAFLOW),
    )(*srcs, *lands, send_sems, recv_sems, after)
    return list(outs[:n]), list(outs[n:])


def _fill_own(lands, srcs, scatter):
    me = 4 * lax.axis_index("x") + 2 * lax.axis_index("y") + lax.axis_index("c")
    own = [lax.dynamic_index_in_dim(s, me, 0, keepdims=False) if sc else s for s, sc in zip(srcs, scatter)]
    return [lax.dynamic_update_index_in_dim(land, o, me, 0) for land, o in zip(lands, own)]


def _slot_sum(parts, name, tr):
    _, R, C = parts.shape

    def body(p_ref, o_ref):
        acc = p_ref[0].astype(F32)
        for s in range(1, N_DEV):
            acc = acc + p_ref[s].astype(F32)
        o_ref[...] = acc

    return pl.pallas_call(
        body, name=name, grid=(R // tr,),
        in_specs=[pl.BlockSpec((N_DEV, tr, C), lambda i: (0, i, 0))],
        out_specs=pl.BlockSpec((tr, C), lambda i: (i, 0)),
        out_shape=jax.ShapeDtypeStruct((R, C), F32),
        compiler_params=_params(1),
    )(parts)


def _adamw(w, g, m, v, name):
    def body(w_ref, g_ref, m_ref, v_ref, d_ref, nm_ref, nv_ref):
        g = g_ref[...]
        nm = ADAM_B1 * m_ref[...] + (1.0 - ADAM_B1) * g
        nv = ADAM_B2 * v_ref[...] + (1.0 - ADAM_B2) * jnp.square(g)
        m_hat = nm / (1.0 - ADAM_B1 ** ADAM_STEP)
        v_hat = nv / (1.0 - ADAM_B2 ** ADAM_STEP)
        d_ref[...] = -ADAM_LR * (m_hat / (jnp.sqrt(v_hat) + ADAM_EPS) + ADAM_WD * w_ref[...])
        nm_ref[...] = nm
        nv_ref[...] = nv

    return pl.pallas_call(
        body, name=name, out_shape=[jax.ShapeDtypeStruct(w.shape, F32)] * 3,
        compiler_params=pltpu.CompilerParams(vmem_limit_bytes=VMEM_LIMIT),
    )(w, g, m, v)


def _rope_tables(T):
    half = HEAD_DIM // 2
    freqs = ROPE_THETA ** (-jnp.arange(half, dtype=F32) * (2.0 / HEAD_DIM))
    ang = jnp.arange(T).astype(F32)[:, None] * jnp.tile(freqs, LANES // half)[None, :]
    sign = jnp.tile(jnp.concatenate([-jnp.ones((half,), F32), jnp.ones((half,), F32)]), LANES // HEAD_DIM)
    return jnp.cos(ang), jnp.sin(ang) * sign[None, :]


def _block_diag(w_pool):
    wbd = jnp.zeros((POOL_W, POOL_W), F32)
    g = POOL_W // len(POOL_WINDOWS)
    for i in range(len(POOL_WINDOWS)):
        wbd = wbd.at[i * g:(i + 1) * g, i * g:(i + 1) * g].set(w_pool[i])
    return wbd


def _pack_small(g1, w_pool, pool_scale, g2, g3, g4, extra):
    pad = lambda a: jnp.pad(a.reshape(1, -1), ((0, 0), (0, D_MODEL - a.size)))
    rows = [g1.reshape(1, -1), g2.reshape(1, -1), g3.reshape(1, -1), g4.reshape(1, -1),
            w_pool.reshape(-1, D_MODEL), pad(pool_scale), pad(extra)]
    buf = jnp.concatenate(rows, axis=0)
    return jnp.pad(buf, ((0, SMALL_ROWS - buf.shape[0]), (0, 0)))


def _unpack_small(buf):
    n_pool = len(POOL_WINDOWS) * (POOL_W // len(POOL_WINDOWS)) ** 2 // D_MODEL
    g = POOL_W // len(POOL_WINDOWS)
    return (buf[0:1], buf[4:4 + n_pool].reshape(1, len(POOL_WINDOWS), g, g), buf[4 + n_pool:5 + n_pool, :POOL_W],
            buf[1:2], buf[2:3], buf[3:4], buf[5 + n_pool])


class _LocalStep:
    def __init__(self, x, tgt, g1, w_pool, pool_scale, g2, g3, g4):
        self.x, self.tgt, self.pool_scale = x, tgt, pool_scale
        self.g1, self.g2, self.g3, self.g4 = g1, g2, g3, g4
        self.cos, self.sin = _rope_tables(x.shape[0])
        self.wbd = _block_diag(w_pool).astype(BF16)

    def mixer_fwd(self, w_in_t, token):
        self.w_in_t = w_in_t
        self.h1, self.u, self.q, self.k, self.v = _proj_fwd(
            self.x, self.g1 + token[0, 0], w_in_t, self.cos, self.sin, 1024)
        self.pool = _pool_fwd(self.u, self.wbd, self.pool_scale, 1024)
        alone = [_attn_fwd(self.q, self.k, self.v, dil, []) for dil in DILATIONS[:-1]]
        self.attn, self.lse = _attn_fwd(self.q, self.k, self.v, DILATIONS[-1], alone)
        return self.attn

    def ffn_fwd_bwd(self, w_out, wg_t, wu_t, w_down):
        self.w_out = w_out
        self.cat, self.mix, self.x2, h2 = _mix_fwd(self.pool, self.attn, self.x, w_out, self.g2, self.g3, 1024)
        act_dgate, act_dup, act = _ffn_up(h2, wg_t, wu_t, 256)
        df, dy, self.dg4, self.loss = _ffn_down_loss(act, w_down, self.x2, self.g4, self.tgt, 512)
        self.dgate, dup, self.dx2, self.dmix, self.dg3, self.dg2 = _ffn_bwd(
            df, w_down, act_dgate, act_dup, wg_t, wu_t, self.x2, self.mix, dy, self.g3, self.g2, 256)
        return (_wgrad(self.dgate, h2, "wgrad_gate", D_FF // 2, 1024), _wgrad(dup, h2, "wgrad_up", D_FF // 2, 1024),
                _wgrad(act, df, "wgrad_down", D_FF // 2, 1024))

    def mixer_bwd(self, token):
        dmix = self.dmix
        dpool, dattn, dsum = _mix_bwd(dmix, self.w_out, self.attn, token, 1024)
        parts = [_attn_bwd(self.q, self.k, self.v, dattn, dsum, self.lse, dil) for dil in DILATIONS]
        du, dwbd, self.dscale = _pool_bwd(self.u, dpool, self.wbd, self.pool_scale, 1024)
        g = POOL_W // len(POOL_WINDOWS)
        self.dw_pool = jnp.stack([dwbd[i * g:(i + 1) * g, i * g:(i + 1) * g] for i in range(len(POOL_WINDOWS))])
        self.dproj = _dproj_combine(du, *[[p[j] for p in parts] for j in range(5)], self.cos, self.sin, 512)
        return _wgrad(self.dproj, self.h1, "wgrad_in", IN_W // 2, 1024), _wgrad(self.cat, dmix, "wgrad_out", D_MODEL, 1024)

    def input_bwd(self, token):
        grad_x, dg1 = _proj_bwd(self.dproj, self.w_in_t, self.x, self.dx2, self.g1 + token[0, 0], 1024)
        return self.loss, grad_x, (dg1, self.dw_pool, self.dscale, self.dg2, self.dg3, self.dg4)


def _local_step(x, tgt, g1, w_pool, pool_scale, g2, g3, g4, w_in_t, w_out, wg_t, wu_t, w_down):
    zero = jnp.zeros((8, LANES), F32)
    step = _LocalStep(x, tgt, g1, w_pool, pool_scale, g2, g3, g4)
    step.mixer_fwd(w_in_t, zero)
    dw_gate, dw_up, dw_down = step.ffn_fwd_bwd(w_out, wg_t, wu_t, w_down)
    dw_in, dw_out = step.mixer_bwd(zero)
    loss, grad_x, small = step.input_bwd(zero)
    return loss, grad_x, small, (dw_in, dw_out, dw_gate, dw_up, dw_down)


def kernel(x, ln_pre_mix, w_in, w_pool, pool_scale, w_out, ln_post_mix, ln_pre_ffn, w_gate, w_up, w_down, ln_post_ffn, loss_target, m_ln_pre_mix, m_w_in, m_w_pool, m_pool_scale, m_w_out, m_ln_post_mix, m_ln_pre_ffn, m_w_gate, m_w_up, m_w_down, m_ln_post_ffn, v_ln_pre_mix, v_w_in, v_w_pool, v_pool_scale, v_w_out, v_ln_post_mix, v_ln_pre_ffn, v_w_gate, v_w_up, v_w_down, v_ln_post_ffn):
    shards = [w_in[0].T.astype(BF16), w_out[0].astype(BF16), w_gate[0].T.astype(BF16),
              w_up[0].T.astype(BF16), w_down[0].astype(BF16)]
    flat = lambda a: a.reshape(-1, D_MODEL)
    blocks = lambda a: a.reshape(N_DEV, -1, D_MODEL)
    step = _LocalStep(x[0], loss_target[0], ln_pre_mix, w_pool[0], pool_scale, ln_post_mix, ln_pre_ffn, ln_post_ffn)

    w_in_t = flat(_gather_two_level(shards[0], "gather_w_in"))
    rest = _exchange_start(shards[1:], [False] * 4, w_in_t, "gather_rest_start")
    attn = step.mixer_fwd(w_in_t, rest[4])
    srcs, lands = _exchange_wait(rest, [False] * 4, attn, "gather_rest_wait")
    w_out_f, wg_t, wu_t, w_down_f = [flat(a) for a in _fill_own(lands, srcs, [False] * 4)]

    ffn = _exchange_start([blocks(a) for a in step.ffn_fwd_bwd(w_out_f, wg_t, wu_t, w_down_f)], [True] * 3,
                          step.dgate, "grads_ffn_start")
    mixer = _exchange_start([blocks(a) for a in step.mixer_bwd(ffn[4])], [True] * 2, step.dproj, "grads_mixer_start")
    loss, grad_x, small = step.input_bwd(mixer[4])
    got = []
    for handle, n_arr, nm in ((mixer, 2, "grads_mixer"), (ffn, 3, "grads_ffn")):
        srcs, lands = _exchange_wait(handle, [True] * n_arr, grad_x, nm + "_wait")
        got += _fill_own(lands, srcs, [True] * n_arr)
    sums = [_slot_sum(got[i], f"sum_grad_{i}", got[i].shape[1] // 2) for i in range(5)]

    small_buf = _pack_small(small[0], small[1], small[2], small[3], small[4], small[5], loss)
    small_sum = _slot_sum(_exchange([small_buf], [False], "gather_small")[0], "sum_small", SMALL_ROWS)

    g_in, g_out, g_gate, g_up, g_down = sums[0].T, sums[1], sums[2].T, sums[3].T, sums[4]
    upd = [_adamw(w[0], g, m[0], v[0], f"adamw_{nm}") for nm, w, g, m, v in (
        ("in", w_in, g_in, m_w_in, v_w_in), ("out", w_out, g_out, m_w_out, v_w_out),
        ("gate", w_gate, g_gate, m_w_gate, v_w_gate), ("up", w_up, g_up, m_w_up, v_w_up),
        ("down", w_down, g_down, m_w_down, v_w_down))]
    pack = lambda a, b, c, d, e, f: _pack_small(a, b[0], c, d, e, f, jnp.zeros((1,), F32))
    small_upd = _adamw(
        pack(ln_pre_mix, w_pool, pool_scale, ln_post_mix, ln_pre_ffn, ln_post_ffn), small_sum,
        pack(m_ln_pre_mix, m_w_pool, m_pool_scale, m_ln_post_mix, m_ln_pre_ffn, m_ln_post_ffn),
        pack(v_ln_pre_mix, v_w_pool, v_pool_scale, v_ln_post_mix, v_ln_pre_ffn, v_ln_post_ffn), "adamw_small")

    def tree(small6, big5):
        s1, spool, sscale, s2, s3, s4 = small6
        b_in, b_out, b_gate, b_up, b_down = [b[None] for b in big5]
        return [s1, b_in, spool, sscale, b_out, s2, s3, b_gate, b_up, b_down, s4]

    g_small = _unpack_small(small_sum)
    outs = [g_small[6][0], grad_x[None]]
    outs += tree(g_small[:6], [g_in, g_out, g_gate, g_up, g_down])
    for j in range(3):
        outs += tree(_unpack_small(small_upd[j])[:6], [u[j] for u in upd])
    return tuple(outs)
```
